```python
import math
import jax, jax.numpy as jnp
from jax import lax
import numpy as np

D_MODEL = 1024
BATCH = 4
SEQ = 4096
DEPTH = 1
DEC_BATCH = 128
DEC_SEQ = 4
PAST_LEN = 16384
PAGE_SIZE = 128

DN_HEADS = 8
DN_DK = 128
DN_DV = 128
DN_CONV = 4
DN_CHUNK = 64
DN_QK_W = DN_HEADS * DN_DK
DN_V_W = DN_HEADS * DN_DV
DN_CONV_W = 2 * DN_QK_W + DN_V_W
SW_HEADS = 16
SW_KV_HEADS = 2
SW_GROUP = SW_HEADS // SW_KV_HEADS
SW_HD = 64
SW_Q_W = SW_HEADS * SW_HD
SW_KV_W = SW_KV_HEADS * SW_HD
WINDOW = 128
SW_BLOCK = 128
REL_BUCKETS = 32
REL_MAX_DIST = 128
N_GROUPS = 8
EXP_PER_GROUP = 8
N_EXPERTS = N_GROUPS * EXP_PER_GROUP
TOP_K = 2
D_EXPERT = 256
MOE_BLOCK = 128
EPS = 1e-6
IN_SIZES = (DN_QK_W, DN_QK_W, DN_V_W, DN_V_W, DN_HEADS, DN_HEADS, SW_Q_W, SW_KV_W, SW_KV_W, D_MODEL, D_MODEL)
D_IN = sum(IN_SIZES)

kernel_name = 'hybrid_gdn_swa_hiermoe_step'


def rmsnorm(x, w):
    xf = x.astype(jnp.float32)
    y = xf * lax.rsqrt(jnp.mean(xf * xf, axis=-1, keepdims=True) + EPS)
    return (y * w.astype(jnp.float32)).astype(x.dtype)


def l2norm(x):
    return x * lax.rsqrt(jnp.sum(x * x, axis=-1, keepdims=True) + 1e-6)


def causal_conv(u, buf, w):
    L = u.shape[1]
    full = jnp.concatenate([buf.astype(u.dtype), u], axis=1)
    out = full[:, 0:L] * w[0]
    for i in range(1, DN_CONV):
        out = out + full[:, i:i + L] * w[i]
    return jax.nn.silu(out), full[:, L:]


def gated_delta(q, k, v, beta, g, s0):
    B, L, H, _ = q.shape
    C = min(DN_CHUNK, L)
    pad = (-L) % C
    if pad:
        padw = lambda t: jnp.pad(t, [(0, 0), (0, pad)] + [(0, 0)] * (t.ndim - 2))
        q, k, v, beta, g = (padw(t) for t in (q, k, v, beta, g))
    N = (L + pad) // C

    def chunks(t):
        return jnp.moveaxis(t.reshape((B, N, C) + t.shape[2:]), 3, 1)

    qc, kc, vc, bc, gc = (chunks(t) for t in (q, k, v, beta, g))
    G = jnp.cumsum(gc, axis=-1)
    idx = jnp.arange(C)
    causal = idx[:, None] >= idx[None, :]
    strict = idx[:, None] > idx[None, :]
    gamma = jnp.exp(jnp.where(causal, G[..., :, None] - G[..., None, :], -jnp.inf))
    kb = kc * bc[..., None]
    a = jnp.where(strict, jnp.einsum('bhncd,bhnsd->bhncs', kb, kc) * gamma, 0.0) + jnp.eye(C, dtype=kc.dtype)
    solve = lambda rhs: lax.linalg.triangular_solve(a, rhs, left_side=True, lower=True, unit_diagonal=True)
    u = solve(vc * bc[..., None])
    w = solve(kb * jnp.exp(G)[..., None])
    attn = jnp.einsum('bhncd,bhnsd->bhncs', qc, kc) * gamma
    qg = qc * jnp.exp(G)[..., None]
    kd = kc * jnp.exp(G[..., -1:] - G)[..., None]
    gl = jnp.exp(G[..., -1])
    xs = tuple(jnp.moveaxis(t, 2, 0) for t in (u, w, attn, qg, kd, gl))

    def step(s, inp):
        u_i, w_i, attn_i, qg_i, kd_i, gl_i = inp
        v_new = u_i - jnp.einsum('bhcd,bhdv->bhcv', w_i, s)
        o_i = jnp.einsum('bhcd,bhdv->bhcv', qg_i, s) + jnp.einsum('bhcs,bhsv->bhcv', attn_i, v_new)
        s = s * gl_i[..., None, None] + jnp.einsum('bhcd,bhcv->bhdv', kd_i, v_new)
        return s, o_i

    s, o = lax.scan(step, s0, xs)
    o = jnp.transpose(o, (1, 0, 3, 2, 4)).reshape(B, N * C, H, -1)[:, :L]
    return o, s


def deltanet_branch(dq, dk, dv, dz, db, da, conv_buf, s0, conv_w, a_log, dt_bias, norm_w):
    f32 = jnp.float32
    B, L = dq.shape[:2]
    qkv, new_buf = causal_conv(jnp.concatenate([dq, dk, dv], axis=-1), conv_buf, conv_w)
    q = qkv[..., :DN_QK_W].astype(f32).reshape(B, L, DN_HEADS, DN_DK)
    k = qkv[..., DN_QK_W:2 * DN_QK_W].astype(f32).reshape(B, L, DN_HEADS, DN_DK)
    v = qkv[..., 2 * DN_QK_W:].astype(f32).reshape(B, L, DN_HEADS, DN_DV)
    q = l2norm(q) * DN_DK ** -0.5
    k = l2norm(k)
    beta = jax.nn.sigmoid(db.astype(f32))
    g = -jnp.exp(a_log.astype(f32)) * jax.nn.softplus(da.astype(f32) + dt_bias.astype(f32))
    o, s = gated_delta(q, k, v, beta, g, s0.astype(f32))
    z = dz.astype(f32).reshape(B, L, DN_HEADS, DN_DV)
    o = rmsnorm(o, norm_w) * jax.nn.silu(z)
    return o.reshape(B, L, DN_V_W), new_buf, s


def rel_bucket(dist):
    n = jnp.maximum(dist, 0)
    max_exact = REL_BUCKETS // 2
    large = max_exact + (jnp.log(jnp.maximum(n, 1).astype(jnp.float32) / max_exact)
                         / math.log(REL_MAX_DIST / max_exact) * (REL_BUCKETS - max_exact)).astype(jnp.int32)
    return jnp.where(n < max_exact, n, jnp.minimum(large, REL_BUCKETS - 1))


def sink_attention(q, k, v, dist, valid, sinks, rel_table):
    f32 = jnp.float32
    Q, S = dist.shape
    s = jnp.einsum('bnqkgd,bnskd->bnkgqs', q.astype(f32), k.astype(f32)) * SW_HD ** -0.5
    bias = jnp.transpose(rel_table.astype(f32)[rel_bucket(dist)], (2, 0, 1)).reshape(SW_KV_HEADS, SW_GROUP, Q, S)
    s = jnp.where(valid[None, :, None, None], s + bias, -jnp.inf)
    sink = sinks.astype(f32).reshape(1, 1, SW_KV_HEADS, SW_GROUP, 1, 1)
    m = jnp.maximum(jnp.max(s, axis=-1, keepdims=True), sink)
    p = jnp.exp(s - m)
    p = p / (jnp.sum(p, axis=-1, keepdims=True) + jnp.exp(sink - m))
    return jnp.einsum('bnkgqs,bnskd->bnqkgd', p, v.astype(f32))


def swa_prompt(q, k, v, sinks, rel_table):
    B, L = q.shape[:2]
    nb = L // SW_BLOCK
    qb = q.reshape(B, nb, SW_BLOCK, SW_KV_HEADS, SW_GROUP, SW_HD)
    kb = k.reshape(B, nb, SW_BLOCK, SW_KV_HEADS, SW_HD)
    vb = v.reshape(B, nb, SW_BLOCK, SW_KV_HEADS, SW_HD)
    shift = lambda t: jnp.concatenate([jnp.zeros_like(t[:, :1]), t[:, :-1]], axis=1)
    kk = jnp.concatenate([shift(kb), kb], axis=2)
    vv = jnp.concatenate([shift(vb), vb], axis=2)
    kpos = jnp.arange(2 * SW_BLOCK) - SW_BLOCK
    qpos = jnp.arange(SW_BLOCK)
    dist = qpos[:, None] - kpos[None, :]
    blk_start = jnp.arange(nb) * SW_BLOCK
    valid = (((dist >= 0) & (dist < WINDOW))[None]
             & ((blk_start[:, None, None] + kpos[None, None, :]) >= 0))
    o = sink_attention(qb, kk, vv, dist, valid, sinks, rel_table)
    return o.reshape(B, L, SW_Q_W)


def swa_step(q, k_all, v_all, sinks, rel_table):
    B, Lq = q.shape[:2]
    S = k_all.shape[1]
    qpos = (S - Lq) + jnp.arange(Lq)
    dist = qpos[:, None] - jnp.arange(S)[None, :]
    valid = ((dist >= 0) & (dist < WINDOW))[None]
    o = sink_attention(q[:, None], k_all[:, None], v_all[:, None], dist, valid, sinks, rel_table)
    return o.reshape(B, Lq, SW_Q_W)


def hier_moe(h, w_rg, b_rg, w_re, b_re, w_gate, w_up, w_down):
    f32 = jnp.float32
    B, L, D = h.shape
    hf = h.reshape(-1, D)
    T = hf.shape[0]
    gprob = jax.nn.softmax((hf @ w_rg).astype(f32) + b_rg.astype(f32), axis=-1)
    gval, gidx = lax.top_k(gprob, 1)
    grp = gidx[:, 0]
    elog_all = ((hf @ w_re).astype(f32) + b_re.astype(f32)).reshape(T, N_GROUPS, EXP_PER_GROUP)
    elog = jnp.take_along_axis(elog_all, grp[:, None, None], axis=1)[:, 0]
    top_v, top_i = lax.top_k(elog, TOP_K)
    wts = jax.nn.softmax(top_v, axis=-1) * gval
    eidx = grp[:, None] * EXP_PER_GROUP + top_i
    A = T * TOP_K
    e_flat = eidx.reshape(-1)
    order = jnp.argsort(e_flat)
    e_sorted = e_flat[order]
    tok_sorted = (order // TOP_K).astype(jnp.int32)
    w_sorted = wts.reshape(-1)[order]
    counts = jnp.bincount(e_flat, length=N_EXPERTS)
    padded = ((counts + MOE_BLOCK - 1) // MOE_BLOCK) * MOE_BLOCK
    pad_end = jnp.cumsum(padded)
    pad_start = pad_end - padded
    start = jnp.cumsum(counts) - counts
    dest = pad_start[e_sorted] + (jnp.arange(A) - start[e_sorted])
    NB = -(-A // MOE_BLOCK) + N_EXPERTS
    R = NB * MOE_BLOCK
    slot_tok = jnp.full((R,), T, dtype=jnp.int32).at[dest].set(tok_sorted)
    slot_w = jnp.zeros((R,), f32).at[dest].set(w_sorted)
    block_expert = jnp.minimum(jnp.searchsorted(pad_end, jnp.arange(NB) * MOE_BLOCK, side='right'), N_EXPERTS - 1)
    xb = jnp.concatenate([hf, jnp.zeros((1, D), hf.dtype)], axis=0)[slot_tok].reshape(NB, MOE_BLOCK, D)

    def expert_block(args):
        xi, e = args
        return (jax.nn.silu(xi @ w_gate[e]) * (xi @ w_up[e])) @ w_down[e]

    yb = lax.map(expert_block, (xb, block_expert)).reshape(R, D)
    y = jax.ops.segment_sum(yb.astype(f32) * slot_w[:, None], slot_tok, num_segments=T + 1)[:T]
    return y.reshape(B, L, D).astype(h.dtype)


def hybrid_layer(x, conv_buf, dn_state, kv_prev, w_in, conv_w, a_log, dt_bias, dn_norm_w, sinks, rel_bias,
                 w_out, norm_mix_w, norm_ffn_w, w_rg, b_rg, w_re, b_re, w_gate, w_up, w_down):
    B, L, _ = x.shape
    h = rmsnorm(x, norm_mix_w)
    proj = h @ w_in
    offsets = np.cumsum(IN_SIZES)[:-1].tolist()
    dq, dk, dv, dz, db, da, sq, sk, sv, gate_a, gate_b = jnp.split(proj, offsets, axis=-1)
    o_a, new_conv, new_dn = deltanet_branch(dq, dk, dv, dz, db, da, conv_buf, dn_state,
                                            conv_w, a_log, dt_bias, dn_norm_w)
    sq = sq.reshape(B, L, SW_KV_HEADS, SW_GROUP, SW_HD)
    sk = sk.reshape(B, L, SW_KV_HEADS, SW_HD)
    sv = sv.reshape(B, L, SW_KV_HEADS, SW_HD)
    if kv_prev is None:
        o_b = swa_prompt(sq, sk, sv, sinks, rel_bias)
        keep = min(WINDOW, L)
        new_k, new_v = sk[:, L - keep:], sv[:, L - keep:]
    else:
        k_prev, v_prev = kv_prev
        k_all = jnp.concatenate([k_prev.astype(sk.dtype), sk], axis=1)
        v_all = jnp.concatenate([v_prev.astype(sv.dtype), sv], axis=1)
        o_b = swa_step(sq, k_all, v_all, sinks, rel_bias)
        keep = k_prev.shape[1]
        new_k, new_v = k_all[:, -keep:], v_all[:, -keep:]
    f32 = jnp.float32
    mixed = (jax.nn.sigmoid(gate_a.astype(f32)) * o_a + jax.nn.sigmoid(gate_b.astype(f32)) * o_b).astype(x.dtype)
    x = x + mixed @ w_out
    x = x + hier_moe(rmsnorm(x, norm_ffn_w), w_rg, b_rg, w_re, b_re, w_gate, w_up, w_down)
    return x, new_conv, new_dn, new_k, new_v


def setup_inputs(seed: int = 0) -> dict:
    key = jax.random.key(seed)
    ks = jax.random.split(key, 24)
    f32 = jnp.float32
    nrm = lambda k, shape, scale: jax.random.normal(k, shape, f32) * scale
    win_buf = min(WINDOW, PAST_LEN)
    dt = jnp.exp(jax.random.uniform(ks[9], (DEPTH, DN_HEADS), f32, math.log(1e-3), math.log(1e-1)))
    return {
        'x_prompt': nrm(ks[0], (BATCH, SEQ, D_MODEL), 1.0),
        'x_sample': nrm(ks[1], (DEC_BATCH, DEC_SEQ, D_MODEL), 1.0),
        'state_dn_conv': nrm(ks[2], (DEPTH, DEC_BATCH, DN_CONV - 1, DN_CONV_W), 1.0),
        'state_dn': nrm(ks[3], (DEPTH, DEC_BATCH, DN_HEADS, DN_DK, DN_DV), 0.1),
        'cache_swa_k': nrm(ks[4], (DEPTH, DEC_BATCH, win_buf, SW_KV_HEADS, SW_HD), 1.0),
        'cache_swa_v': nrm(ks[5], (DEPTH, DEC_BATCH, win_buf, SW_KV_HEADS, SW_HD), 1.0),
        'w_in': nrm(ks[6], (DEPTH, D_MODEL, D_IN), D_MODEL ** -0.5),
        'conv_w': nrm(ks[7], (DEPTH, DN_CONV, DN_CONV_W), DN_CONV ** -0.5),
        'a_log': jnp.log(jax.random.uniform(ks[8], (DEPTH, DN_HEADS), f32, 1.0, 16.0)),
        'dt_bias': dt + jnp.log(-jnp.expm1(-dt)),
        'dn_norm_w': 1.0 + nrm(ks[10], (DEPTH, DN_DV), 0.02),
        'sinks': nrm(ks[11], (DEPTH, SW_HEADS), 0.5),
        'rel_bias': nrm(ks[12], (REL_BUCKETS, SW_HEADS), 0.1),
        'w_out': nrm(ks[13], (DEPTH, D_MODEL, D_MODEL), D_MODEL ** -0.5),
        'norm_mix_w': 1.0 + nrm(ks[14], (DEPTH, D_MODEL), 0.02),
        'norm_ffn_w': 1.0 + nrm(ks[15], (DEPTH, D_MODEL), 0.02),
        'w_router_group': nrm(ks[16], (DEPTH, D_MODEL, N_GROUPS), D_MODEL ** -0.5),
        'b_router_group': nrm(ks[17], (DEPTH, N_GROUPS), 0.01),
        'w_router_expert': nrm(ks[18], (DEPTH, D_MODEL, N_EXPERTS), D_MODEL ** -0.5),
        'b_router_expert': nrm(ks[19], (DEPTH, N_EXPERTS), 0.01),
        'w_gate': nrm(ks[20], (DEPTH, N_EXPERTS, D_MODEL, D_EXPERT), D_MODEL ** -0.5),
        'w_up': nrm(ks[21], (DEPTH, N_EXPERTS, D_MODEL, D_EXPERT), D_MODEL ** -0.5),
        'w_down': nrm(ks[22], (DEPTH, N_EXPERTS, D_EXPERT, D_MODEL), D_EXPERT ** -0.5),
        'norm_final_w': 1.0 + nrm(ks[23], (D_MODEL,), 0.02),
    }


def reference(x_prompt, x_sample, state_dn_conv, state_dn, cache_swa_k, cache_swa_v, w_in, conv_w, a_log, dt_bias,
              dn_norm_w, sinks, rel_bias, w_out, norm_mix_w, norm_ffn_w, w_router_group, b_router_group,
              w_router_expert, b_router_expert, w_gate, w_up, w_down, norm_final_w):
    xp, xs = x_prompt, x_sample
    bp = xp.shape[0]
    p_conv, p_dn, p_k, p_v = [], [], [], []
    s_conv, s_dn, s_k, s_v = [], [], [], []
    for l in range(DEPTH):
        params = (w_in[l], conv_w[l], a_log[l], dt_bias[l], dn_norm_w[l], sinks[l], rel_bias,
                  w_out[l], norm_mix_w[l], norm_ffn_w[l], w_router_group[l], b_router_group[l],
                  w_router_expert[l], b_router_expert[l], w_gate[l], w_up[l], w_down[l])
        conv0 = jnp.zeros((bp, DN_CONV - 1, DN_CONV_W), xp.dtype)
        dn0 = jnp.zeros((bp, DN_HEADS, DN_DK, DN_DV), jnp.float32)
        xp, c, s, k, v = hybrid_layer(xp, conv0, dn0, None, *params)
        p_conv.append(c.astype(xp.dtype)); p_dn.append(s.astype(xp.dtype))
        p_k.append(k.astype(xp.dtype)); p_v.append(v.astype(xp.dtype))
        xs, c, s, k, v = hybrid_layer(xs, state_dn_conv[l], state_dn[l], (cache_swa_k[l], cache_swa_v[l]), *params)
        s_conv.append(c.astype(state_dn_conv.dtype)); s_dn.append(s.astype(state_dn.dtype))
        s_k.append(k.astype(cache_swa_k.dtype)); s_v.append(v.astype(cache_swa_v.dtype))
    y_prompt = rmsnorm(xp, norm_final_w)
    y_sample = rmsnorm(xs, norm_final_w)
    return (y_prompt, y_sample, jnp.stack(p_conv), jnp.stack(p_dn), jnp.stack(p_k), jnp.stack(p_v),
            jnp.stack(s_conv), jnp.stack(s_dn), jnp.stack(s_k), jnp.stack(s_v))
```

```python
import functools
import math

import jax
import jax.numpy as jnp
import numpy as np
from jax import lax
from jax.experimental import pallas as pl
from jax.experimental.pallas import tpu as pltpu

F32 = jnp.float32
BF16 = jnp.bfloat16
I32 = jnp.int32

D_MODEL = 1024
DN_HEADS = 8
DN_DK = 128
DN_DV = 128
DN_CONV = 4
DN_CHUNK = 64
DN_QK_W = DN_HEADS * DN_DK
DN_V_W = DN_HEADS * DN_DV
DN_CONV_W = 2 * DN_QK_W + DN_V_W
SW_HEADS = 16
SW_KV_HEADS = 2
SW_GROUP = SW_HEADS // SW_KV_HEADS
SW_HD = 64
SW_KV_W = SW_KV_HEADS * SW_HD
WINDOW = 128
SW_BLOCK = 128
REL_BUCKETS = 32
REL_MAX_DIST = 128
N_GROUPS = 8
EXP_PER_GROUP = 8
N_EXPERTS = N_GROUPS * EXP_PER_GROUP
D_EXPERT = 256
MOE_BLOCK = 128
EPS = 1e-6

LANES = 128
SUBLANES = 8
VMEM_LIMIT = 56 * 1024 * 1024

COL_QKV = 0
COL_Z = 3072
COL_SQ = 4096
COL_GA = 5120
COL_GB = 6144
COL_SK = 7168
COL_SV = 7296
PROJ_W = 7424
PROJ_CHUNK = 512


def _cparams(sem):
    return pltpu.CompilerParams(dimension_semantics=sem, vmem_limit_bytes=VMEM_LIMIT)


def _sigmoid(x):
    return 1.0 / (1.0 + jnp.exp(-x))


def _dot(a, b):
    return jnp.dot(a.astype(BF16), b.astype(BF16), preferred_element_type=F32)


def _dot_nt(a, b):
    return lax.dot_general(a.astype(BF16), b.astype(BF16), (((1,), (1,)), ((), ())), preferred_element_type=F32)


def _dot_tn(a, b):
    return lax.dot_general(a.astype(BF16), b.astype(BF16), (((0,), (0,)), ((), ())), preferred_element_type=F32)


def _dot_exact(a, b):
    return jnp.dot(a, b, precision=lax.Precision.HIGHEST, preferred_element_type=F32)


def _lane_iota(shape):
    return lax.broadcasted_iota(I32, shape, len(shape) - 1)


def _row_iota(shape):
    return lax.broadcasted_iota(I32, shape, len(shape) - 2)


def _col(x, j):
    return jnp.sum(jnp.where(_lane_iota(x.shape) == j, x, 0.0), axis=-1, keepdims=True)


def _token_tile(*sizes):
    for t in (256, 128, 64, 32, 16, 8):
        if all(s % t == 0 for s in sizes):
            return t
    raise ValueError(f"token counts {sizes} need a common tile that is a multiple of 8")


def _inproj_kernel(xp_ref, xs_ref, nw_ref, wb_ref, ws_ref, proj_ref, ba_ref, *, n_prompt_tiles):
    i = pl.program_id(0)

    def run(x_ref):
        x = x_ref[...]
        h = (x * lax.rsqrt(jnp.mean(x * x, axis=-1, keepdims=True) + EPS) * nw_ref[...]).astype(BF16)
        ba_ref[...] = jnp.dot(h, ws_ref[...], preferred_element_type=F32)
        for c0 in range(0, PROJ_W, PROJ_CHUNK):
            c1 = min(c0 + PROJ_CHUNK, PROJ_W)
            proj_ref[:, c0:c1] = jnp.dot(h, wb_ref[:, c0:c1], preferred_element_type=F32)

    @pl.when(i < n_prompt_tiles)
    def _():
        run(xp_ref)

    @pl.when(i >= n_prompt_tiles)
    def _():
        run(xs_ref)


def _inproj(xp, xs, norm_w, w_big, w_small):
    tp, ts = xp.shape[0], xs.shape[0]
    tm = _token_tile(tp, ts)
    npt, nst = tp // tm, ts // tm
    const = lambda i: (0, 0)
    return pl.pallas_call(
        functools.partial(_inproj_kernel, n_prompt_tiles=npt),
        grid=(npt + nst,),
        in_specs=[
            pl.BlockSpec((tm, D_MODEL), lambda i: (jnp.minimum(i, npt - 1), 0)),
            pl.BlockSpec((tm, D_MODEL), lambda i: (jnp.maximum(i - npt, 0), 0)),
            pl.BlockSpec((1, D_MODEL), const),
            pl.BlockSpec((D_MODEL, PROJ_W), const, pipeline_mode=pl.Buffered(1)),
            pl.BlockSpec((D_MODEL, LANES), const),
        ],
        out_specs=[
            pl.BlockSpec((tm, PROJ_W), lambda i: (i, 0)),
            pl.BlockSpec((tm, LANES), lambda i: (i, 0)),
        ],
        out_shape=[
            jax.ShapeDtypeStruct((tp + ts, PROJ_W), F32),
            jax.ShapeDtypeStruct((tp + ts, LANES), F32),
        ],
        compiler_params=_cparams(("arbitrary",)),
        name="inproj",
    )(xp, xs, norm_w, w_big, w_small)


def _dn_conv(cbuf_ref, convw_ref, rows):
    off = SUBLANES - (DN_CONV - 1)
    w = convw_ref[...]
    acc = cbuf_ref[off:off + rows, :] * w[0:1, :]
    for i in range(1, DN_CONV):
        acc = acc + cbuf_ref[off + i:off + i + rows, :] * w[i:i + 1, :]
    return acc * _sigmoid(acc)


def _dn_heads(qkv, z, ba, alog, dtb, nw, read_state, write_state, n_valid):
    cp = qkv.shape[0]
    ri = _row_iota((cp, cp))
    ci = _lane_iota((cp, cp))
    incl = ri >= ci
    strict = ri > ci
    l_incl = incl.astype(F32)
    u_strict = strict.astype(F32)
    eye = (ri == ci).astype(F32)
    levels = max(1, math.ceil(math.log2(n_valid)))

    beta_all = _sigmoid(ba)
    sp = ba + dtb
    softplus = jnp.maximum(sp, 0.0) + jnp.log1p(jnp.exp(-jnp.abs(sp)))
    g_all = -jnp.exp(alog) * softplus
    if n_valid < cp:
        live = _row_iota((cp, LANES)) < n_valid
        beta_all = jnp.where(live, beta_all, 0.0)
        g_all = jnp.where(live, g_all, 0.0)
    gsum_all = _dot_exact(l_incl, g_all)

    outs = []
    for h in range(DN_HEADS):
        q = qkv[:, h * DN_DK:(h + 1) * DN_DK]
        k = qkv[:, DN_QK_W + h * DN_DK:DN_QK_W + (h + 1) * DN_DK]
        v = qkv[:, 2 * DN_QK_W + h * DN_DV:2 * DN_QK_W + (h + 1) * DN_DV]
        q = q * lax.rsqrt(jnp.sum(q * q, axis=-1, keepdims=True) + 1e-6) * (DN_DK ** -0.5)
        k = k * lax.rsqrt(jnp.sum(k * k, axis=-1, keepdims=True) + 1e-6)
        beta = _col(beta_all, h)
        g = _col(g_all, DN_HEADS + h)
        gsum = _col(gsum_all, DN_HEADS + h)
        glast = gsum[cp - 1:cp, :]
        diff = _dot_exact(l_incl, g * u_strict)
        gamma = jnp.exp(diff)
        kb = k * beta
        kq = _dot_nt(jnp.concatenate([kb, q], axis=0), k)
        a = jnp.where(strict, kq[:cp] * gamma, 0.0)
        attn = jnp.where(incl, kq[cp:] * gamma, 0.0)
        p = -a
        t = eye + p
        for _ in range(1, levels):
            p = _dot(p, p)
            t = t + _dot(t, p)
        eg = jnp.exp(gsum)
        uw = _dot(t, jnp.concatenate([v * beta, kb * eg], axis=1))
        s = read_state(h)
        wq = _dot(jnp.concatenate([uw[:, DN_DV:], q * eg], axis=0), s)
        v_new = uw[:, :DN_DV] - wq[:cp]
        o = wq[cp:] + _dot(attn, v_new)
        kd = k * jnp.exp(glast - gsum)
        write_state(h, s * jnp.exp(glast) + _dot_tn(kd, v_new))
        zz = z[:, h * DN_DV:(h + 1) * DN_DV]
        on = o * lax.rsqrt(jnp.mean(o * o, axis=-1, keepdims=True) + EPS) * nw
        outs.append(on * (zz * _sigmoid(zz)))
    return jnp.concatenate(outs, axis=1)


def _dn_prompt_kernel(qkv_ref, z_ref, ba_ref, convw_ref, alog_ref, dtb_ref, nw_ref, conv0_ref, s0_ref,
                      o_ref, convout_ref, sout_ref, cbuf_ref, *, chunk):
    c = pl.program_id(1)
    hist = SUBLANES - (DN_CONV - 1)

    @pl.when(c == 0)
    def _():
        cbuf_ref[hist:SUBLANES, :] = conv0_ref[0]
        sout_ref[...] = s0_ref[...]

    cbuf_ref[SUBLANES:SUBLANES + chunk, :] = qkv_ref[...]
    qkv = _dn_conv(cbuf_ref, convw_ref, chunk)
    tail = cbuf_ref[SUBLANES + chunk - (DN_CONV - 1):SUBLANES + chunk, :]
    cbuf_ref[hist:SUBLANES, :] = tail
    convout_ref[0] = tail

    def read_state(h):
        return sout_ref[0, h]

    def write_state(h, val):
        sout_ref[0, h] = val

    o_ref[...] = _dn_heads(qkv, z_ref[...], ba_ref[...], alog_ref[...], dtb_ref[...], nw_ref[...],
                           read_state, write_state, chunk)


def _dn_prompt(proj, ba, conv_w, alog_row, dtb_row, dn_nw, conv0, s0, n_batch, seq):
    chunk = min(DN_CHUNK, seq)
    assert seq % chunk == 0 and chunk % SUBLANES == 0
    nc = seq // chunk
    const2 = lambda b, c: (0, 0)
    return pl.pallas_call(
        functools.partial(_dn_prompt_kernel, chunk=chunk),
        grid=(n_batch, nc),
        in_specs=[
            pl.BlockSpec((chunk, DN_CONV_W), lambda b, c: (b * nc + c, COL_QKV // DN_CONV_W)),
            pl.BlockSpec((chunk, DN_V_W), lambda b, c: (b * nc + c, COL_Z // DN_V_W)),
            pl.BlockSpec((chunk, LANES), lambda b, c: (b * nc + c, 0)),
            pl.BlockSpec((DN_CONV, DN_CONV_W), const2),
            pl.BlockSpec((1, LANES), const2),
            pl.BlockSpec((1, LANES), const2),
            pl.BlockSpec((1, DN_DV), const2),
            pl.BlockSpec((1, DN_CONV - 1, DN_CONV_W), lambda b, c: (b, 0, 0)),
            pl.BlockSpec((1, DN_HEADS, DN_DK, DN_DV), lambda b, c: (b, 0, 0, 0)),
        ],
        out_specs=[
            pl.BlockSpec((chunk, DN_V_W), lambda b, c: (b * nc + c, 0)),
            pl.BlockSpec((1, DN_CONV - 1, DN_CONV_W), lambda b, c: (b, 0, 0)),
            pl.BlockSpec((1, DN_HEADS, DN_DK, DN_DV), lambda b, c: (b, 0, 0, 0)),
        ],
        out_shape=[
            jax.ShapeDtypeStruct((n_batch * seq, DN_V_W), F32),
            jax.ShapeDtypeStruct((n_batch, DN_CONV - 1, DN_CONV_W), F32),
            jax.ShapeDtypeStruct((n_batch, DN_HEADS, DN_DK, DN_DV), F32),
        ],
        scratch_shapes=[pltpu.VMEM((SUBLANES + chunk, DN_CONV_W), F32)],
        compiler_params=_cparams(("arbitrary", "arbitrary")),
        name="dn_prompt",
    )(proj, proj, ba, conv_w, alog_row, dtb_row, dn_nw, conv0, s0)


def _dn_sample_kernel(qkv_ref, z_ref, ba_ref, convw_ref, alog_ref, dtb_ref, nw_ref, conv0_ref, s0_ref,
                      o_ref, convout_ref, sout_ref, cbuf_ref, *, seq, n_bb):
    hist = SUBLANES - (DN_CONV - 1)
    live = _row_iota((SUBLANES, DN_CONV_W)) < seq
    out = None
    for bb in range(n_bb):
        shift = (SUBLANES - bb * seq) % SUBLANES

        def top(x, shift=shift):
            return x if shift == 0 else pltpu.roll(x, shift, axis=0)

        cbuf_ref[hist:SUBLANES, :] = conv0_ref[bb]
        cbuf_ref[SUBLANES:2 * SUBLANES, :] = top(qkv_ref[...])
        qkv = jnp.where(live, _dn_conv(cbuf_ref, convw_ref, SUBLANES), 0.0)
        convout_ref[bb] = cbuf_ref[SUBLANES + seq - (DN_CONV - 1):SUBLANES + seq, :]

        def read_state(h, bb=bb):
            return s0_ref[bb, h]

        def write_state(h, val, bb=bb):
            sout_ref[bb, h] = val

        o = _dn_heads(qkv, top(z_ref[...]), top(ba_ref[...]), alog_ref[...], dtb_ref[...], nw_ref[...],
                      read_state, write_state, seq)
        back = (bb * seq) % SUBLANES
        o = o if back == 0 else pltpu.roll(o, back, axis=0)
        rows = _row_iota(o.shape)
        sel = (rows >= bb * seq) & (rows < (bb + 1) * seq)
        out = jnp.where(sel, o, 0.0 if out is None else out)
    o_ref[...] = out


def _dn_sample(proj, ba, conv_w, alog_row, dtb_row, dn_nw, conv0, s0, row0, n_batch, seq):
    assert SUBLANES % seq == 0 and seq >= DN_CONV - 1
    n_bb = SUBLANES // seq
    assert n_batch % n_bb == 0 and row0 % SUBLANES == 0
    rb0 = row0 // SUBLANES
    const1 = lambda i: (0, 0)
    return pl.pallas_call(
        functools.partial(_dn_sample_kernel, seq=seq, n_bb=n_bb),
        grid=(n_batch // n_bb,),
        in_specs=[
            pl.BlockSpec((SUBLANES, DN_CONV_W), lambda i: (rb0 + i, COL_QKV // DN_CONV_W)),
            pl.BlockSpec((SUBLANES, DN_V_W), lambda i: (rb0 + i, COL_Z // DN_V_W)),
            pl.BlockSpec((SUBLANES, LANES), lambda i: (rb0 + i, 0)),
            pl.BlockSpec((DN_CONV, DN_CONV_W), const1),
            pl.BlockSpec((1, LANES), const1),
            pl.BlockSpec((1, LANES), const1),
            pl.BlockSpec((1, DN_DV), const1),
            pl.BlockSpec((n_bb, DN_CONV - 1, DN_CONV_W), lambda i: (i, 0, 0)),
            pl.BlockSpec((n_bb, DN_HEADS, DN_DK, DN_DV), lambda i: (i, 0, 0, 0)),
        ],
        out_specs=[
            pl.BlockSpec((SUBLANES, DN_V_W), lambda i: (i, 0)),
            pl.BlockSpec((n_bb, DN_CONV - 1, DN_CONV_W), lambda i: (i, 0, 0)),
            pl.BlockSpec((n_bb, DN_HEADS, DN_DK, DN_DV), lambda i: (i, 0, 0, 0)),
        ],
        out_shape=[
            jax.ShapeDtypeStruct((n_batch * seq, DN_V_W), F32),
            jax.ShapeDtypeStruct((n_batch, DN_CONV - 1, DN_CONV_W), F32),
            jax.ShapeDtypeStruct((n_batch, DN_HEADS, DN_DK, DN_DV), F32),
        ],
        scratch_shapes=[pltpu.VMEM((2 * SUBLANES, DN_CONV_W), F32)],
        compiler_params=_cparams(("arbitrary",)),
        name="dn_sample",
    )(proj, proj, ba, conv_w, alog_row, dtb_row, dn_nw, conv0, s0)


def _rel_bucket(dist):
    n = jnp.maximum(dist, 0)
    max_exact = REL_BUCKETS // 2
    large = max_exact + (jnp.log(jnp.maximum(n, 1).astype(F32) / max_exact)
                         / math.log(REL_MAX_DIST / max_exact) * (REL_BUCKETS - max_exact)).astype(I32)
    return jnp.where(n < max_exact, n, jnp.minimum(large, REL_BUCKETS - 1))


def _relbias_kernel(tab_ref, bucket_ref, o_ref):
    h = pl.program_id(0)
    bk = bucket_ref[...]
    acc = jnp.zeros(bk.shape, F32)
    for b in range(REL_BUCKETS):
        acc = jnp.where(bk == b, tab_ref[b * SW_HEADS + h], acc)
    o_ref[0] = acc


def _relbias(rel_table, bucket):
    nq, ns = bucket.shape
    return pl.pallas_call(
        _relbias_kernel,
        grid=(SW_HEADS,),
        in_specs=[
            pl.BlockSpec(memory_space=pltpu.SMEM),
            pl.BlockSpec((nq, ns), lambda h: (0, 0)),
        ],
        out_specs=pl.BlockSpec((1, nq, ns), lambda h: (h, 0, 0)),
        out_shape=jax.ShapeDtypeStruct((SW_HEADS, nq, ns), F32),
        compiler_params=_cparams(("arbitrary",)),
        name="relbias",
    )(rel_table.reshape(-1), bucket)


def _dup_halves(x):
    lo = _lane_iota(x.shape) < SW_HD
    xr = pltpu.roll(x, SW_HD, axis=1)
    return jnp.where(lo, x, xr).astype(BF16), jnp.where(lo, xr, x).astype(BF16)


def _sink_softmax_pv(s, valid, sink, vv):
    s = jnp.where(valid, s, -jnp.inf)
    m = jnp.maximum(jnp.max(s, axis=-1, keepdims=True), sink)
    p = jnp.exp(s - m)
    p = p / (jnp.sum(p, axis=-1, keepdims=True) + jnp.exp(sink - m))
    return _dot(p, vv)


def _swa_prompt_kernel(sinks_ref, q_ref, kc_ref, kp_ref, vc_ref, vp_ref, bias_ref, o_ref):
    i = pl.program_id(1)
    kk = _dup_halves(jnp.concatenate([kp_ref[...], kc_ref[...]], axis=0))
    vv = _dup_halves(jnp.concatenate([vp_ref[...], vc_ref[...]], axis=0))
    shape = (SW_BLOCK, 2 * SW_BLOCK)
    qi = _row_iota(shape)
    kj = _lane_iota(shape)
    dist = qi - kj + SW_BLOCK
    first_key = jnp.where(i > 0, 0, SW_BLOCK)
    valid = (dist >= 0) & (dist < WINDOW) & (kj >= first_key)
    lo = _lane_iota((SW_BLOCK, LANES)) < SW_HD
    for pair in range(SW_HEADS // 2):
        qp = q_ref[:, pair * LANES:(pair + 1) * LANES]
        outs = []
        for half in range(2):
            hq = 2 * pair + half
            kv = hq // SW_GROUP
            qm = jnp.where(lo if half == 0 else ~lo, qp, 0.0)
            s = _dot_nt(qm, kk[kv]) * (SW_HD ** -0.5) + bias_ref[hq]
            outs.append(_sink_softmax_pv(s, valid, sinks_ref[hq], vv[kv]))
        o_ref[:, pair * LANES:(pair + 1) * LANES] = jnp.where(lo, outs[0], outs[1])


def _swa_prompt(proj, sinks, bias, n_batch, seq):
    assert seq % SW_BLOCK == 0
    nb = seq // SW_BLOCK
    cur = lambda col: (lambda b, i: (b * nb + i, col))
    prev = lambda col: (lambda b, i: (b * nb + jnp.maximum(i - 1, 0), col))
    return pl.pallas_call(
        _swa_prompt_kernel,
        grid=(n_batch, nb),
        in_specs=[
            pl.BlockSpec(memory_space=pltpu.SMEM),
            pl.BlockSpec((SW_BLOCK, SW_HEADS * SW_HD), cur(COL_SQ // (SW_HEADS * SW_HD))),
            pl.BlockSpec((SW_BLOCK, SW_KV_W), cur(COL_SK // SW_KV_W)),
            pl.BlockSpec((SW_BLOCK, SW_KV_W), prev(COL_SK // SW_KV_W)),
            pl.BlockSpec((SW_BLOCK, SW_KV_W), cur(COL_SV // SW_KV_W)),
            pl.BlockSpec((SW_BLOCK, SW_KV_W), prev(COL_SV // SW_KV_W)),
            pl.BlockSpec((SW_HEADS, SW_BLOCK, 2 * SW_BLOCK), lambda b, i: (0, 0, 0)),
        ],
        out_specs=pl.BlockSpec((SW_BLOCK, SW_HEADS * SW_HD), lambda b, i: (b * nb + i, 0)),
        out_shape=jax.ShapeDtypeStruct((n_batch * seq, SW_HEADS * SW_HD), F32),
        compiler_params=_cparams(("arbitrary", "arbitrary")),
        name="swa_prompt",
    )(sinks, proj, proj, proj, proj, proj, bias)


def _swa_sample_kernel(q_ref, kn_ref, vn_ref, kc_ref, vc_ref, bias_ref, sink_ref,
                       o_ref, ko_ref, vo_ref, kall_ref, vall_ref, *, seq, n_bb, n_cache):
    n_keys = kall_ref.shape[0]
    zeros_tail = jnp.zeros((n_keys - n_cache - SUBLANES, LANES), F32)
    shape = (SW_GROUP * SUBLANES, n_keys)
    t = _row_iota(shape) % SUBLANES
    s_idx = _lane_iota(shape)
    dist = n_cache + t - s_idx
    valid = (dist >= 0) & (dist < WINDOW)
    lo = _lane_iota((SUBLANES, LANES)) < SW_HD
    out = None
    for bb in range(n_bb):
        shift = (SUBLANES - bb * seq) % SUBLANES

        def top(x, shift=shift):
            return x if shift == 0 else pltpu.roll(x, shift, axis=0)

        kall_ref[0:n_cache, :] = kc_ref[bb]
        kall_ref[n_cache:n_cache + SUBLANES, :] = top(kn_ref[...])
        kall_ref[n_cache + SUBLANES:, :] = zeros_tail
        vall_ref[0:n_cache, :] = vc_ref[bb]
        vall_ref[n_cache:n_cache + SUBLANES, :] = top(vn_ref[...])
        vall_ref[n_cache + SUBLANES:, :] = zeros_tail
        ko_ref[bb] = kall_ref[seq:seq + n_cache, :]
        vo_ref[bb] = vall_ref[seq:seq + n_cache, :]
        kk = _dup_halves(kall_ref[...])
        vv = _dup_halves(vall_ref[...])
        q8 = top(q_ref[...])
        pairs = []
        for kv in range(SW_KV_HEADS):
            pieces = []
            for g in range(SW_GROUP):
                hq = kv * SW_GROUP + g
                qp = q8[:, (hq // 2) * LANES:(hq // 2 + 1) * LANES]
                pieces.append(jnp.where(lo if hq % 2 == 0 else ~lo, qp, 0.0))
            qs = jnp.concatenate(pieces, axis=0)
            s = _dot_nt(qs, kk[kv]) * (SW_HD ** -0.5) + bias_ref[kv]
            res = _sink_softmax_pv(s, valid, sink_ref[kv], vv[kv])
            for g in range(0, SW_GROUP, 2):
                pairs.append(jnp.where(lo, res[g * SUBLANES:(g + 1) * SUBLANES],
                                       res[(g + 1) * SUBLANES:(g + 2) * SUBLANES]))
        o = jnp.concatenate(pairs, axis=1)
        back = (bb * seq) % SUBLANES
        o = o if back == 0 else pltpu.roll(o, back, axis=0)
        rows = _row_iota(o.shape)
        sel = (rows >= bb * seq) & (rows < (bb + 1) * seq)
        out = jnp.where(sel, o, 0.0 if out is None else out)
    o_ref[...] = out


def _swa_sample(proj, k_cache, v_cache, bias, sink_rows, row0, n_batch, seq):
    assert SUBLANES % seq == 0
    n_bb = SUBLANES // seq
    n_cache = k_cache.shape[1]
    assert n_batch % n_bb == 0 and row0 % SUBLANES == 0 and n_cache % SUBLANES == 0
    n_keys = bias.shape[-1]
    rb0 = row0 // SUBLANES
    blk = lambda col: (lambda i: (rb0 + i, col))
    return pl.pallas_call(
        functools.partial(_swa_sample_kernel, seq=seq, n_bb=n_bb, n_cache=n_cache),
        grid=(n_batch // n_bb,),
        in_specs=[
            pl.BlockSpec((SUBLANES, SW_HEADS * SW_HD), blk(COL_SQ // (SW_HEADS * SW_HD))),
            pl.BlockSpec((SUBLANES, SW_KV_W), blk(COL_SK // SW_KV_W)),
            pl.BlockSpec((SUBLANES, SW_KV_W), blk(COL_SV // SW_KV_W)),
            pl.BlockSpec((n_bb, n_cache, SW_KV_W), lambda i: (i, 0, 0)),
            pl.BlockSpec((n_bb, n_cache, SW_KV_W), lambda i: (i, 0, 0)),
            pl.BlockSpec((SW_KV_HEADS, SW_GROUP * SUBLANES, n_keys), lambda i: (0, 0, 0)),
            pl.BlockSpec((SW_KV_HEADS, SW_GROUP * SUBLANES, n_keys), lambda i: (0, 0, 0)),
        ],
        out_specs=[
            pl.BlockSpec((SUBLANES, SW_HEADS * SW_HD), lambda i: (i, 0)),
            pl.BlockSpec((n_bb, n_cache, SW_KV_W), lambda i: (i, 0, 0)),
            pl.BlockSpec((n_bb, n_cache, SW_KV_W), lambda i: (i, 0, 0)),
        ],
        out_shape=[
            jax.ShapeDtypeStruct((n_batch * seq, SW_HEADS * SW_HD), F32),
            jax.ShapeDtypeStruct(k_cache.shape, F32),
            jax.ShapeDtypeStruct(v_cache.shape, F32),
        ],
        scratch_shapes=[pltpu.VMEM((n_keys, SW_KV_W), F32), pltpu.VMEM((n_keys, SW_KV_W), F32)],
        compiler_params=_cparams(("arbitrary",)),
        name="swa_sample",
    )(proj, proj, proj, k_cache, v_cache, bias, sink_rows)


def _mix_kernel(xp_ref, xs_ref, oap_ref, oas_ref, obp_ref, obs_ref, ga_ref, gb_ref, wo_ref, nw_ref, wr_ref, br_ref,
                x1_ref, h2_ref, route_ref, *, n_prompt_tiles):
    i = pl.program_id(0)

    def run(x_ref, oa_ref, ob_ref):
        mixed = _sigmoid(ga_ref[...]) * oa_ref[...] + _sigmoid(gb_ref[...]) * ob_ref[...]
        x1 = x_ref[...] + _dot(mixed, wo_ref[...])
        x1_ref[...] = x1
        h2 = x1 * lax.rsqrt(jnp.mean(x1 * x1, axis=-1, keepdims=True) + EPS) * nw_ref[...]
        h2_ref[...] = h2
        logits = _dot_exact(h2, wr_ref[...]) + br_ref[...]
        lane = _lane_iota(logits.shape)
        lanef = lane.astype(F32)
        big = float(2 * LANES)
        is_g = lane < N_GROUPS
        gl = jnp.where(is_g, logits, -jnp.inf)
        gmax = jnp.max(gl, axis=-1, keepdims=True)
        gval = 1.0 / jnp.sum(jnp.where(is_g, jnp.exp(gl - gmax), 0.0), axis=-1, keepdims=True)
        grp = jnp.min(jnp.where(gl == gmax, lanef, big), axis=-1, keepdims=True)
        e_grp = ((lane - N_GROUPS) >> 3).astype(F32)
        is_e = (lane >= N_GROUPS) & (lane < N_GROUPS + N_EXPERTS) & (e_grp == grp)
        el = jnp.where(is_e, logits, -jnp.inf)
        v1 = jnp.max(el, axis=-1, keepdims=True)
        i1 = jnp.min(jnp.where(el == v1, lanef, big), axis=-1, keepdims=True)
        el2 = jnp.where(lanef == i1, -jnp.inf, el)
        v2 = jnp.max(el2, axis=-1, keepdims=True)
        i2 = jnp.min(jnp.where(el2 == v2, lanef, big), axis=-1, keepdims=True)
        e2 = jnp.exp(v2 - v1)
        w1 = gval / (1.0 + e2)
        w2 = gval * e2 / (1.0 + e2)
        route_ref[...] = jnp.where(lane == 0, i1 - N_GROUPS,
                                   jnp.where(lane == 1, i2 - N_GROUPS,
                                             jnp.where(lane == 2, w1, jnp.where(lane == 3, w2, 0.0))))

    @pl.when(i < n_prompt_tiles)
    def _():
        run(xp_ref, oap_ref, obp_ref)

    @pl.when(i >= n_prompt_tiles)
    def _():
        run(xs_ref, oas_ref, obs_ref)


def _mix(xp, xs, oa_p, oa_s, ob_p, ob_s, proj, w_out, norm_w, w_router, b_router):
    tp, ts = xp.shape[0], xs.shape[0]
    tm = _token_tile(tp, ts)
    npt, nst = tp // tm, ts // tm
    const = lambda i: (0, 0)
    row = lambda i: (i, 0)
    return pl.pallas_call(
        functools.partial(_mix_kernel, n_prompt_tiles=npt),
        grid=(npt + nst,),
        in_specs=[
            pl.BlockSpec((tm, D_MODEL), lambda i: (jnp.minimum(i, npt - 1), 0)),
            pl.BlockSpec((tm, D_MODEL), lambda i: (jnp.maximum(i - npt, 0), 0)),
            pl.BlockSpec((tm, D_MODEL), lambda i: (jnp.minimum(i, npt - 1), 0)),
            pl.BlockSpec((tm, D_MODEL), lambda i: (jnp.maximum(i - npt, 0), 0)),
            pl.BlockSpec((tm, D_MODEL), lambda i: (jnp.minimum(i, npt - 1), 0)),
            pl.BlockSpec((tm, D_MODEL), lambda i: (jnp.maximum(i - npt, 0), 0)),
            pl.BlockSpec((tm, D_MODEL), lambda i: (i, COL_GA // D_MODEL)),
            pl.BlockSpec((tm, D_MODEL), lambda i: (i, COL_GB // D_MODEL)),
            pl.BlockSpec((D_MODEL, D_MODEL), const),
            pl.BlockSpec((1, D_MODEL), const),
            pl.BlockSpec((D_MODEL, LANES), const),
            pl.BlockSpec((1, LANES), const),
        ],
        out_specs=[
            pl.BlockSpec((tm, D_MODEL), row),
            pl.BlockSpec((tm, D_MODEL), row),
            pl.BlockSpec((tm, LANES), row),
        ],
        out_shape=[
            jax.ShapeDtypeStruct((tp + ts, D_MODEL), F32),
            jax.ShapeDtypeStruct((tp + ts, D_MODEL), F32),
            jax.ShapeDtypeStruct((tp + ts, LANES), F32),
        ],
        compiler_params=_cparams(("arbitrary",)),
        name="mix_router",
    )(xp, xs, oa_p, oa_s, ob_p, ob_s, proj, proj, w_out, norm_w, w_router, b_router)


def _rank_kernel(route_ref, dest_ref, meta_ref, rank_ref, cnt_ref, *, tile, blk):
    phase = pl.program_id(0)
    i = pl.program_id(1)
    shape = (tile, LANES)
    lane = _lane_iota(shape)
    lanef = lane.astype(F32)
    r = route_ref[...]
    oh0 = lanef == _col(r, 0)
    oh1 = lanef == _col(r, 1)
    rows = pl.ds(pl.multiple_of(i * tile, tile), tile)

    @pl.when(phase == 0)
    def _():
        @pl.when(i == 0)
        def _():
            cnt_ref[...] = jnp.zeros(cnt_ref.shape, F32)

        oh = jnp.where(oh0 | oh1, 1.0, 0.0)
        tri = jnp.where(_row_iota((tile, tile)) > _lane_iota((tile, tile)), 1.0, 0.0)
        before = _dot(tri, oh) + cnt_ref[0:1, :]
        rank0 = jnp.sum(jnp.where(oh0, before, 0.0), axis=-1, keepdims=True)
        rank1 = jnp.sum(jnp.where(oh1, before, 0.0), axis=-1, keepdims=True)
        rank_ref[rows, :] = jnp.where(lane == 0, rank0, jnp.where(lane == 1, rank1, 0.0))
        cnt_ref[0:1, :] = cnt_ref[0:1, :] + jnp.sum(oh, axis=0, keepdims=True)

    @pl.when(phase == 1)
    def _():
        cnt = cnt_ref[0:1, :]
        padded = jnp.floor((cnt + (blk - 1)) / blk) * blk
        before_lane = jnp.where(_row_iota((LANES, LANES)) < _lane_iota((LANES, LANES)), 1.0, 0.0)
        start = _dot_exact(jnp.broadcast_to(padded, (SUBLANES, LANES)), before_lane)[0:1, :]
        rk = rank_ref[rows, :]
        d0 = jnp.sum(jnp.where(oh0, start, 0.0), axis=-1, keepdims=True) + _col(rk, 0)
        d1 = jnp.sum(jnp.where(oh1, start, 0.0), axis=-1, keepdims=True) + _col(rk, 1)
        dest_ref[...] = jnp.where(lane == 0, d0, jnp.where(lane == 1, d1, 0.0)).astype(I32)

        @pl.when(i == 0)
        def _():
            end = start + padded
            mshape = meta_ref.shape
            blk_start = (_row_iota(mshape) * blk).astype(F32)
            hit = (_lane_iota(mshape) < N_EXPERTS) & (end <= blk_start)
            be = jnp.minimum(jnp.sum(jnp.where(hit, 1.0, 0.0), axis=-1, keepdims=True), N_EXPERTS - 1.0)
            n_used = _col(end, N_EXPERTS - 1) / blk
            ml = _lane_iota(mshape)
            meta_ref[...] = jnp.where(ml == 0, be, jnp.where(ml == 1, n_used, 0.0)).astype(I32)


def _rank(route, tile, blk, n_blocks):
    t = route.shape[0]
    nt = t // tile
    nbp = -(-n_blocks // SUBLANES) * SUBLANES
    return pl.pallas_call(
        functools.partial(_rank_kernel, tile=tile, blk=blk),
        grid=(2, nt),
        in_specs=[pl.BlockSpec((tile, LANES), lambda p, i: (i, 0))],
        out_specs=[
            pl.BlockSpec((tile, LANES), lambda p, i: (i * p, 0)),
            pl.BlockSpec((nbp, LANES), lambda p, i: (0, 0)),
        ],
        out_shape=[
            jax.ShapeDtypeStruct((t, LANES), I32),
            jax.ShapeDtypeStruct((nbp, LANES), I32),
        ],
        scratch_shapes=[pltpu.VMEM((t, LANES), F32), pltpu.VMEM((SUBLANES, LANES), F32)],
        compiler_params=_cparams(("arbitrary", "arbitrary")),
        name="moe_rank",
    )(route)


def _row_copy(src, src_row, dst, dst_row, sem):
    return pltpu.make_async_copy(src.at[pl.ds(src_row, 1)], dst.at[pl.ds(dst_row, 1)], sem)


def _dispatch_kernel(dest_ref, h2_ref, xin_ref, xs_ref, sem, *, tile):
    del xin_ref
    base = pl.program_id(0) * tile

    def issue(t, carry):
        for k in range(2):
            _row_copy(h2_ref, base + t, xs_ref, dest_ref[2 * (base + t) + k], sem).start()
        return carry

    def drain(t, carry):
        for k in range(2):
            _row_copy(h2_ref, 0, xs_ref, 0, sem).wait()
        return carry

    lax.fori_loop(0, tile, issue, 0)
    lax.fori_loop(0, tile, drain, 0)


def _dispatch(dest_flat, h2, xs_init, tile):
    t = h2.shape[0]
    return pl.pallas_call(
        functools.partial(_dispatch_kernel, tile=tile),
        grid_spec=pltpu.PrefetchScalarGridSpec(
            num_scalar_prefetch=1,
            grid=(t // tile,),
            in_specs=[pl.BlockSpec(memory_space=pl.ANY), pl.BlockSpec(memory_space=pl.ANY)],
            out_specs=pl.BlockSpec(memory_space=pl.ANY),
            scratch_shapes=[pltpu.SemaphoreType.DMA(())],
        ),
        out_shape=jax.ShapeDtypeStruct(xs_init.shape, F32),
        input_output_aliases={2: 0},
        compiler_params=_cparams(("arbitrary",)),
        name="moe_dispatch",
    )(dest_flat, h2, xs_init)


def _expert_kernel(be_ref, nu_ref, x_ref, wg_ref, wu_ref, wd_ref, y_ref):
    del be_ref
    used = pl.program_id(0) < nu_ref[0]

    @pl.when(used)
    def _():
        x = x_ref[...]
        g = _dot(x, wg_ref[0])
        u = _dot(x, wu_ref[0])
        y_ref[...] = _dot(g * _sigmoid(g) * u, wd_ref[0])

    @pl.when(jnp.logical_not(used))
    def _():
        y_ref[...] = jnp.zeros(y_ref.shape, F32)


def _experts(block_expert, n_used, xs, w_gate, w_up, w_down, blk):
    n_blocks = xs.shape[0] // blk
    used = lambda i, be, nu: (jnp.minimum(i, nu[0] - 1), 0)
    wsel = lambda i, be, nu: (be[jnp.minimum(i, nu[0] - 1)], 0, 0)
    return pl.pallas_call(
        _expert_kernel,
        grid_spec=pltpu.PrefetchScalarGridSpec(
            num_scalar_prefetch=2,
            grid=(n_blocks,),
            in_specs=[
                pl.BlockSpec((blk, D_MODEL), used),
                pl.BlockSpec((1, D_MODEL, D_EXPERT), wsel),
                pl.BlockSpec((1, D_MODEL, D_EXPERT), wsel),
                pl.BlockSpec((1, D_EXPERT, D_MODEL), wsel),
            ],
            out_specs=pl.BlockSpec((blk, D_MODEL), lambda i, be, nu: (i, 0)),
        ),
        out_shape=jax.ShapeDtypeStruct(xs.shape, F32),
        compiler_params=_cparams(("arbitrary",)),
        name="moe_experts",
    )(block_expert, n_used, xs, w_gate, w_up, w_down)


def _combine_kernel(dest_ref, x1_ref, route_ref, nw_ref, ys_ref, yp_ref, ysm_ref, ybuf_ref, sem,
                    *, tile, n_prompt_tiles):
    i = pl.program_id(0)
    base = i * tile

    def issue(t, carry):
        for k in range(2):
            _row_copy(ys_ref, dest_ref[2 * (base + t) + k], ybuf_ref.at[k], t, sem).start()
        return carry

    def drain(t, carry):
        for k in range(2):
            _row_copy(ys_ref, 0, ybuf_ref.at[k], 0, sem).wait()
        return carry

    lax.fori_loop(0, tile, issue, 0)
    lax.fori_loop(0, tile, drain, 0)
    r = route_ref[...]
    y = ybuf_ref[0] * _col(r, 2) + ybuf_ref[1] * _col(r, 3)
    x2 = x1_ref[...] + y
    out = x2 * lax.rsqrt(jnp.mean(x2 * x2, axis=-1, keepdims=True) + EPS) * nw_ref[...]

    @pl.when(i < n_prompt_tiles)
    def _():
        yp_ref[...] = out

    @pl.when(i >= n_prompt_tiles)
    def _():
        ysm_ref[...] = out


def _combine(dest_flat, x1, route, norm_w, ys, tp, ts):
    tile = _token_tile(tp, ts)
    npt, nst = tp // tile, ts // tile
    return pl.pallas_call(
        functools.partial(_combine_kernel, tile=tile, n_prompt_tiles=npt),
        grid_spec=pltpu.PrefetchScalarGridSpec(
            num_scalar_prefetch=1,
            grid=(npt + nst,),
            in_specs=[
                pl.BlockSpec((tile, D_MODEL), lambda i, d: (i, 0)),
                pl.BlockSpec((tile, LANES), lambda i, d: (i, 0)),
                pl.BlockSpec((1, D_MODEL), lambda i, d: (0, 0)),
                pl.BlockSpec(memory_space=pl.ANY),
            ],
            out_specs=[
                pl.BlockSpec((tile, D_MODEL), lambda i, d: (jnp.minimum(i, npt - 1), 0)),
                pl.BlockSpec((tile, D_MODEL), lambda i, d: (jnp.maximum(i - npt, 0), 0)),
            ],
            scratch_shapes=[pltpu.VMEM((2, tile, D_MODEL), F32), pltpu.SemaphoreType.DMA(())],
        ),
        out_shape=[
            jax.ShapeDtypeStruct((tp, D_MODEL), F32),
            jax.ShapeDtypeStruct((ts, D_MODEL), F32),
        ],
        compiler_params=_cparams(("arbitrary",)),
        name="moe_combine",
    )(dest_flat, x1, route, norm_w, ys)


def _layer(xp, xs, n_batch, seq, s_batch, s_seq, conv_state, dn_state, k_cache, v_cache,
           w_in, conv_w, a_log, dt_bias, dn_norm_w, sinks, rel_bias, w_out, norm_mix_w, norm_ffn_w,
           w_rg, b_rg, w_re, b_re, w_gate, w_up, w_down, norm_final_w):
    tp, ts = xp.shape[0], xs.shape[0]
    t_all = tp + ts
    row = lambda v: v.reshape(1, -1).astype(F32)

    o = np.cumsum((0, DN_QK_W, DN_QK_W, DN_V_W, DN_V_W, DN_HEADS, DN_HEADS, SW_HEADS * SW_HD, SW_KV_W, SW_KV_W,
                   D_MODEL, D_MODEL)).tolist()
    w_big = jnp.concatenate([w_in[:, o[0]:o[4]], w_in[:, o[6]:o[7]], w_in[:, o[9]:o[11]], w_in[:, o[7]:o[9]]],
                            axis=1).astype(BF16)
    w_small = jnp.pad(w_in[:, o[4]:o[6]], ((0, 0), (0, LANES - 2 * DN_HEADS))).astype(BF16)
    head_row = lambda v: jnp.pad(v.astype(F32), (DN_HEADS, LANES - 2 * DN_HEADS)).reshape(1, LANES)
    w_router = jnp.pad(jnp.concatenate([w_rg, w_re], axis=1).astype(F32),
                       ((0, 0), (0, LANES - N_GROUPS - N_EXPERTS)))
    b_router = jnp.pad(jnp.concatenate([b_rg, b_re]).astype(F32), (0, LANES - N_GROUPS - N_EXPERTS)).reshape(1, LANES)

    proj, ba = _inproj(xp, xs, row(norm_mix_w), w_big, w_small)

    dn_args = (proj, ba, conv_w.astype(F32), head_row(a_log), head_row(dt_bias), row(dn_norm_w))
    conv0 = jnp.zeros((n_batch, DN_CONV - 1, DN_CONV_W), F32)
    dn0 = jnp.zeros((n_batch, DN_HEADS, DN_DK, DN_DV), F32)
    oa_p, p_conv, p_dn = _dn_prompt(*dn_args, conv0, dn0, n_batch, seq)
    oa_s, s_conv, s_dn = _dn_sample(*dn_args, conv_state, dn_state, tp, s_batch, s_seq)

    qpos = jnp.arange(SW_BLOCK)[:, None]
    kpos = jnp.arange(2 * SW_BLOCK)[None, :] - SW_BLOCK
    bias_p = _relbias(rel_bias.astype(F32), _rel_bucket(qpos - kpos))
    ob_p = _swa_prompt(proj, sinks.astype(F32), bias_p, n_batch, seq)
    n_cache = k_cache.shape[1]
    n_keys = -(-(n_cache + SUBLANES) // LANES) * LANES
    tq = n_cache + jnp.arange(SUBLANES)[:, None]
    bias_s = _relbias(rel_bias.astype(F32), _rel_bucket(tq - jnp.arange(n_keys)[None, :]))
    bias_s = bias_s.reshape(SW_KV_HEADS, SW_GROUP * SUBLANES, n_keys)
    sink_rows = jnp.broadcast_to(jnp.repeat(sinks.astype(F32).reshape(SW_KV_HEADS, SW_GROUP), SUBLANES, axis=1)[:, :, None],
                                 (SW_KV_HEADS, SW_GROUP * SUBLANES, n_keys))
    ob_s, s_k, s_v = _swa_sample(proj, k_cache.reshape(s_batch, n_cache, SW_KV_W), v_cache.reshape(s_batch, n_cache, SW_KV_W),
                                 bias_s, sink_rows, tp, s_batch, s_seq)

    x1, h2, route = _mix(xp, xs, oa_p, oa_s, ob_p, ob_s, proj, w_out.astype(BF16), row(norm_ffn_w), w_router, b_router)

    tile = _token_tile(tp, ts)
    n_blocks = -(-2 * t_all // MOE_BLOCK) + N_EXPERTS
    dest, meta = _rank(route, tile, MOE_BLOCK, n_blocks)
    dest_flat = dest[:, :2].reshape(-1)
    block_expert = meta[:n_blocks, 0]
    n_used = meta[0:1, 1]
    xs_sorted = _dispatch(dest_flat, h2, jnp.zeros((n_blocks * MOE_BLOCK, D_MODEL), F32), tile)
    ys = _experts(block_expert, n_used, xs_sorted, w_gate, w_up, w_down, MOE_BLOCK)
    y_p, y_s = _combine(dest_flat, x1, route, row(norm_final_w), ys, tp, ts)

    keep = min(WINDOW, seq)
    kv_tail = proj[:tp].reshape(n_batch, seq, PROJ_W)[:, seq - keep:, COL_SK:COL_SV + SW_KV_W]
    p_k = kv_tail[..., :SW_KV_W].reshape(n_batch, keep, SW_KV_HEADS, SW_HD)
    p_v = kv_tail[..., SW_KV_W:].reshape(n_batch, keep, SW_KV_HEADS, SW_HD)
    return (y_p, y_s, p_conv, p_dn, p_k, p_v, s_conv, s_dn,
            s_k.reshape(k_cache.shape), s_v.reshape(v_cache.shape))


def kernel(x_prompt, x_sample, state_dn_conv, state_dn, cache_swa_k, cache_swa_v, w_in, conv_w, a_log, dt_bias, dn_norm_w, sinks, rel_bias, w_out, norm_mix_w, norm_ffn_w, w_router_group, b_router_group, w_router_expert, b_router_expert, w_gate, w_up, w_down, norm_final_w):
    depth = w_in.shape[0]
    assert depth == 1, "the final-norm fusion below assumes a single layer"
    n_batch, seq, _ = x_prompt.shape
    s_batch, s_seq, _ = x_sample.shape
    outs = _layer(x_prompt.reshape(-1, D_MODEL), x_sample.reshape(-1, D_MODEL), n_batch, seq, s_batch, s_seq,
                  state_dn_conv[0], state_dn[0], cache_swa_k[0], cache_swa_v[0],
                  w_in[0], conv_w[0], a_log[0], dt_bias[0], dn_norm_w[0], sinks[0], rel_bias,
                  w_out[0], norm_mix_w[0], norm_ffn_w[0], w_router_group[0], b_router_group[0],
                  w_router_expert[0], b_router_expert[0], w_gate[0], w_up[0], w_down[0], norm_final_w)
    y_p, y_s, p_conv, p_dn, p_k, p_v, s_conv, s_dn, s_k, s_v = outs
    return (y_p.reshape(x_prompt.shape), y_s.reshape(x_sample.shape), p_conv[None], p_dn[None], p_k[None], p_v[None],
            s_conv[None], s_dn[None], s_k[None], s_v[None])
```

```python
import functools
import math

import jax
import jax.numpy as jnp
import numpy as np
from jax import lax
from jax.experimental import pallas as pl
from jax.experimental.pallas import tpu as pltpu

F32 = jnp.float32
BF16 = jnp.bfloat16
I32 = jnp.int32

D_MODEL = 1024
DN_HEADS = 8
DN_DK = 128
DN_DV = 128
DN_CONV = 4
DN_CHUNK = 64
DN_QK_W = DN_HEADS * DN_DK
DN_V_W = DN_HEADS * DN_DV
DN_CONV_W = 2 * DN_QK_W + DN_V_W
SW_HEADS = 16
SW_KV_HEADS = 2
SW_GROUP = SW_HEADS // SW_KV_HEADS
SW_HD = 64
SW_KV_W = SW_KV_HEADS * SW_HD
WINDOW = 128
SW_BLOCK = 128
REL_BUCKETS = 32
REL_MAX_DIST = 128
N_GROUPS = 8
EXP_PER_GROUP = 8
N_EXPERTS = N_GROUPS * EXP_PER_GROUP
D_EXPERT = 256
MOE_BLOCK = 128
EPS = 1e-6

LANES = 128
SUBLANES = 8
VMEM_LIMIT = 56 * 1024 * 1024

COL_QKV = 0
COL_Z = 3072
COL_SQ = 4096
COL_GA = 5120
COL_GB = 6144
COL_SK = 7168
COL_SV = 7296
PROJ_W = 7424
PROJ_CHUNK = 512


def _cparams(sem):
    return pltpu.CompilerParams(dimension_semantics=sem, vmem_limit_bytes=VMEM_LIMIT)


def _sigmoid(x):
    return 1.0 / (1.0 + jnp.exp(-x))


def _dot(a, b):
    return jnp.dot(a.astype(BF16), b.astype(BF16), preferred_element_type=F32)


def _dot_nt(a, b):
    return lax.dot_general(a.astype(BF16), b.astype(BF16), (((1,), (1,)), ((), ())), preferred_element_type=F32)


def _dot_tn(a, b):
    return lax.dot_general(a.astype(BF16), b.astype(BF16), (((0,), (0,)), ((), ())), preferred_element_type=F32)


def _dot_exact(a, b):
    return jnp.dot(a, b, precision=lax.Precision.HIGHEST, preferred_element_type=F32)


def _lane_iota(shape):
    return lax.broadcasted_iota(I32, shape, len(shape) - 1)


def _row_iota(shape):
    return lax.broadcasted_iota(I32, shape, len(shape) - 2)


def _col(x, j):
    return jnp.sum(jnp.where(_lane_iota(x.shape) == j, x, 0.0), axis=-1, keepdims=True)


def _token_tile(*sizes):
    for t in (256, 128, 64, 32, 16, 8):
        if all(s % t == 0 for s in sizes):
            return t
    raise ValueError(f"token counts {sizes} need a common tile that is a multiple of 8")


def _inproj_kernel(xp_ref, xs_ref, nw_ref, wb_ref, ws_ref, proj_ref, ba_ref, *, n_prompt_tiles):
    i = pl.program_id(0)

    def run(x_ref):
        x = x_ref[...]
        h = (x * lax.rsqrt(jnp.mean(x * x, axis=-1, keepdims=True) + EPS) * nw_ref[...]).astype(BF16)
        ba_ref[...] = jnp.dot(h, ws_ref[...], preferred_element_type=F32)
        for c0 in range(0, PROJ_W, PROJ_CHUNK):
            c1 = min(c0 + PROJ_CHUNK, PROJ_W)
            proj_ref[:, c0:c1] = jnp.dot(h, wb_ref[:, c0:c1], preferred_element_type=F32)

    @pl.when(i < n_prompt_tiles)
    def _():
        run(xp_ref)

    @pl.when(i >= n_prompt_tiles)
    def _():
        run(xs_ref)


def _inproj(xp, xs, norm_w, w_big, w_small):
    tp, ts = xp.shape[0], xs.shape[0]
    tm = _token_tile(tp, ts)
    npt, nst = tp // tm, ts // tm
    const = lambda i: (0, 0)
    return pl.pallas_call(
        functools.partial(_inproj_kernel, n_prompt_tiles=npt),
        grid=(npt + nst,),
        in_specs=[
            pl.BlockSpec((tm, D_MODEL), lambda i: (jnp.minimum(i, npt - 1), 0)),
            pl.BlockSpec((tm, D_MODEL), lambda i: (jnp.maximum(i - npt, 0), 0)),
            pl.BlockSpec((1, D_MODEL), const),
            pl.BlockSpec((D_MODEL, PROJ_W), const, pipeline_mode=pl.Buffered(1)),
            pl.BlockSpec((D_MODEL, LANES), const),
        ],
        out_specs=[
            pl.BlockSpec((tm, PROJ_W), lambda i: (i, 0)),
            pl.BlockSpec((tm, LANES), lambda i: (i, 0)),
        ],
        out_shape=[
            jax.ShapeDtypeStruct((tp + ts, PROJ_W), F32),
            jax.ShapeDtypeStruct((tp + ts, LANES), F32),
        ],
        compiler_params=_cparams(("arbitrary",)),
        name="inproj",
    )(xp, xs, norm_w, w_big, w_small)


def _dn_conv(cbuf_ref, convw_ref, rows):
    off = SUBLANES - (DN_CONV - 1)
    w = convw_ref[...]
    acc = cbuf_ref[off:off + rows, :] * w[0:1, :]
    for i in range(1, DN_CONV):
        acc = acc + cbuf_ref[off + i:off + i + rows, :] * w[i:i + 1, :]
    return acc * _sigmoid(acc)


def _dn_core(groups, alog, dtb, nw, read_state, write_state, n_seg, seg_valid):
    rows = groups[0][0].shape[0]
    sr = rows // n_seg
    assert sr * n_seg == rows and sr & (sr - 1) == 0 and rows <= LANES
    seg_shift = sr.bit_length() - 1
    ri = _row_iota((rows, rows))
    ci = _lane_iota((rows, rows))
    incl = ri >= ci
    strict = ri > ci
    if n_seg > 1:
        same = (ri >> seg_shift) == (ci >> seg_shift)
        incl = incl & same
        strict = strict & same
    l_incl = incl.astype(F32)
    eye = (ri == ci).astype(F32)
    levels = max(1, math.ceil(math.log2(seg_valid)))

    beta_all, gsum_all, gtot_all, gsum_t = [], [], [], []
    for _, _, ba in groups:
        b_all = _sigmoid(ba)
        sp = ba + dtb
        softplus = jnp.maximum(sp, 0.0) + jnp.log1p(jnp.exp(-jnp.abs(sp)))
        g_all = -jnp.exp(alog) * softplus
        if seg_valid < sr:
            live = (_row_iota((rows, LANES)) & (sr - 1)) < seg_valid
            b_all = jnp.where(live, b_all, 0.0)
            g_all = jnp.where(live, g_all, 0.0)
        gs = _dot_exact(l_incl, g_all)
        beta_all.append(b_all)
        gsum_all.append(gs)
        gtot_all.append(_dot_exact(same.astype(F32), g_all) if n_seg > 1 else gs[rows - 1:rows, :])
        padded = gs if rows == LANES else jnp.concatenate([gs, jnp.zeros((LANES - rows, LANES), F32)], axis=0)
        gsum_t.append(padded.T)

    probs = [(g, h) for g in range(len(groups)) for h in range(DN_HEADS)]
    segs = range(n_seg)
    q, k, v, kb, beta, gsum, gtot = {}, {}, {}, {}, {}, {}, {}
    for p in probs:
        g, h = p
        qkv = groups[g][0]
        qh = qkv[:, h * DN_DK:(h + 1) * DN_DK]
        kh = qkv[:, DN_QK_W + h * DN_DK:DN_QK_W + (h + 1) * DN_DK]
        v[p] = qkv[:, 2 * DN_QK_W + h * DN_DV:2 * DN_QK_W + (h + 1) * DN_DV]
        q[p] = qh * lax.rsqrt(jnp.sum(qh * qh, axis=-1, keepdims=True) + 1e-6) * (DN_DK ** -0.5)
        k[p] = kh * lax.rsqrt(jnp.sum(kh * kh, axis=-1, keepdims=True) + 1e-6)
        beta[p] = _col(beta_all[g], h)
        gsum[p] = _col(gsum_all[g], DN_HEADS + h)
        gtot[p] = _col(gtot_all[g], DN_HEADS + h)
        kb[p] = k[p] * beta[p]
    kq = {p: _dot_nt(jnp.concatenate([kb[p], q[p]], axis=0), k[p]) for p in probs}
    gamma = {(g, h): jnp.exp(jnp.where(incl, gsum[(g, h)] - gsum_t[g][DN_HEADS + h:DN_HEADS + h + 1, :rows], -jnp.inf))
             for g, h in probs}
    attn = {p: kq[p][rows:] * gamma[p] for p in probs}
    pw = {p: -jnp.where(strict, kq[p][:rows] * gamma[p], 0.0) for p in probs}
    t = {p: eye + pw[p] for p in probs}
    for _ in range(1, levels):
        pw = {p: _dot(pw[p], pw[p]) for p in probs}
        t = {p: t[p] + _dot(t[p], pw[p]) for p in probs}
    eg = {p: jnp.exp(gsum[p]) for p in probs}
    uw = {p: _dot(t[p], jnp.concatenate([v[p] * beta[p], kb[p] * eg[p]], axis=1)) for p in probs}
    qg = {p: q[p] * eg[p] for p in probs}
    state = {(p, s): read_state(p[0], s, p[1]) for p in probs for s in segs}
    wq = {(p, s): _dot(jnp.concatenate([uw[p][s * sr:(s + 1) * sr, DN_DV:], qg[p][s * sr:(s + 1) * sr]], axis=0),
                       state[(p, s)]) for p in probs for s in segs}
    join = lambda pieces: pieces[0] if len(pieces) == 1 else jnp.concatenate(pieces, axis=0)
    v_new = {p: uw[p][:, :DN_DV] - join([wq[(p, s)][:sr] for s in segs]) for p in probs}
    o = {p: join([wq[(p, s)][sr:] for s in segs]) + _dot(attn[p], v_new[p]) for p in probs}
    kd = {p: k[p] * jnp.exp(gtot[p] - gsum[p]) for p in probs}
    for p in probs:
        for s in segs:
            r0 = s * sr if n_seg > 1 else 0
            decay = jnp.exp(gtot[p][r0:r0 + 1, :])
            write_state(p[0], s, p[1],
                        state[(p, s)] * decay + _dot_tn(kd[p][s * sr:(s + 1) * sr], v_new[p][s * sr:(s + 1) * sr]))
    outs = []
    for g, (_, z, _) in enumerate(groups):
        heads = []
        for h in range(DN_HEADS):
            oh = o[(g, h)]
            zz = z[:, h * DN_DV:(h + 1) * DN_DV]
            on = oh * lax.rsqrt(jnp.mean(oh * oh, axis=-1, keepdims=True) + EPS) * nw
            heads.append(on * (zz * _sigmoid(zz)))
        outs.append(jnp.concatenate(heads, axis=1))
    return outs


def _dn_prompt_kernel(*refs, chunk, n_batch):
    nb = n_batch
    qkv_refs, z_refs, ba_refs = refs[0:nb], refs[nb:2 * nb], refs[2 * nb:3 * nb]
    convw_ref, alog_ref, dtb_ref, nw_ref, conv0_ref, s0_ref = refs[3 * nb:3 * nb + 6]
    o_ref, convout_ref, sout_ref, cbuf_ref = refs[3 * nb + 6:]
    hist = SUBLANES - (DN_CONV - 1)

    @pl.when(pl.program_id(0) == 0)
    def _():
        cbuf_ref[:, hist:SUBLANES, :] = conv0_ref[...]
        sout_ref[...] = s0_ref[...]

    groups = []
    for b in range(nb):
        cbuf_ref[b, SUBLANES:SUBLANES + chunk, :] = qkv_refs[b][...]
        qkv = _dn_conv(cbuf_ref.at[b], convw_ref, chunk)
        tail = cbuf_ref[b, SUBLANES + chunk - (DN_CONV - 1):SUBLANES + chunk, :]
        cbuf_ref[b, hist:SUBLANES, :] = tail
        convout_ref[b] = tail
        groups.append((qkv, z_refs[b][...], ba_refs[b][...]))

    def read_state(g, s, h):
        return sout_ref[g, h]

    def write_state(g, s, h, val):
        sout_ref[g, h] = val

    outs = _dn_core(groups, alog_ref[...], dtb_ref[...], nw_ref[...], read_state, write_state, 1, chunk)
    for b in range(nb):
        o_ref[b] = outs[b]


def _dn_prompt(proj, ba, conv_w, alog_row, dtb_row, dn_nw, conv0, s0, n_batch, seq):
    chunk = min(DN_CHUNK, seq)
    assert seq % chunk == 0 and chunk % SUBLANES == 0
    nc = seq // chunk
    const2 = lambda c: (0, 0)
    const3 = lambda c: (0, 0, 0)
    rows = lambda b, col: (lambda c: (b * nc + c, col))
    batches = range(n_batch)
    o, conv_out, s_out = pl.pallas_call(
        functools.partial(_dn_prompt_kernel, chunk=chunk, n_batch=n_batch),
        grid=(nc,),
        in_specs=(
            [pl.BlockSpec((chunk, DN_CONV_W), rows(b, COL_QKV // DN_CONV_W)) for b in batches]
            + [pl.BlockSpec((chunk, DN_V_W), rows(b, COL_Z // DN_V_W)) for b in batches]
            + [pl.BlockSpec((chunk, LANES), rows(b, 0)) for b in batches]
            + [
                pl.BlockSpec((DN_CONV, DN_CONV_W), const2),
                pl.BlockSpec((1, LANES), const2),
                pl.BlockSpec((1, LANES), const2),
                pl.BlockSpec((1, DN_DV), const2),
                pl.BlockSpec((n_batch, DN_CONV - 1, DN_CONV_W), const3),
                pl.BlockSpec((n_batch, DN_HEADS, DN_DK, DN_DV), lambda c: (0, 0, 0, 0)),
            ]
        ),
        out_specs=[
            pl.BlockSpec((n_batch, chunk, DN_V_W), lambda c: (0, c, 0)),
            pl.BlockSpec((n_batch, DN_CONV - 1, DN_CONV_W), const3),
            pl.BlockSpec((n_batch, DN_HEADS, DN_DK, DN_DV), lambda c: (0, 0, 0, 0)),
        ],
        out_shape=[
            jax.ShapeDtypeStruct((n_batch, seq, DN_V_W), F32),
            jax.ShapeDtypeStruct((n_batch, DN_CONV - 1, DN_CONV_W), F32),
            jax.ShapeDtypeStruct((n_batch, DN_HEADS, DN_DK, DN_DV), F32),
        ],
        scratch_shapes=[pltpu.VMEM((n_batch, SUBLANES + chunk, DN_CONV_W), F32)],
        compiler_params=_cparams(("arbitrary",)),
        name="dn_prompt",
    )(*([proj] * n_batch), *([proj] * n_batch), *([ba] * n_batch), conv_w, alog_row, dtb_row, dn_nw, conv0, s0)
    return o.reshape(n_batch * seq, DN_V_W), conv_out, s_out


def _dn_sample_kernel(qkv_ref, z_ref, ba_ref, convw_ref, alog_ref, dtb_ref, nw_ref, conv0_ref, s0_ref,
                      o_ref, convout_ref, sout_ref, cbuf_ref, *, seq, n_bb):
    hist = SUBLANES - (DN_CONV - 1)
    per = SUBLANES // seq
    n_tiles = n_bb // per

    def spread(ref):
        pieces = []
        for j in range(n_tiles):
            x8 = ref[j * SUBLANES:(j + 1) * SUBLANES, :]
            for r in range(per):
                pieces.append(x8 if r == 0 else pltpu.roll(x8, SUBLANES - r * seq, axis=0))
        return pieces

    for bb, piece in enumerate(spread(qkv_ref)):
        cbuf_ref[bb, SUBLANES:2 * SUBLANES, :] = piece
    cbuf_ref[:, hist:SUBLANES, :] = conv0_ref[...]
    w = convw_ref[...]
    acc = cbuf_ref[:, hist:hist + SUBLANES, :] * w[0:1, :]
    for i in range(1, DN_CONV):
        acc = acc + cbuf_ref[:, hist + i:hist + i + SUBLANES, :] * w[i:i + 1, :]
    live = _row_iota(acc.shape) < seq
    qkv = jnp.where(live, acc * _sigmoid(acc), 0.0).reshape(n_bb * SUBLANES, DN_CONV_W)
    convout_ref[...] = cbuf_ref[:, SUBLANES + seq - (DN_CONV - 1):SUBLANES + seq, :]

    def read_state(g, s, h):
        return s0_ref[s, h]

    def write_state(g, s, h, val):
        sout_ref[s, h] = val

    group = (qkv, jnp.concatenate(spread(z_ref), axis=0), jnp.concatenate(spread(ba_ref), axis=0))
    o = _dn_core([group], alog_ref[...], dtb_ref[...], nw_ref[...], read_state, write_state, n_bb, seq)[0]
    rows = _row_iota((SUBLANES, DN_V_W))
    for j in range(n_tiles):
        tile = None
        for r in range(per):
            bb = j * per + r
            piece = o[bb * SUBLANES:(bb + 1) * SUBLANES]
            piece = piece if r == 0 else pltpu.roll(piece, r * seq, axis=0)
            tile = piece if tile is None else jnp.where(rows >= r * seq, piece, tile)
        o_ref[j * SUBLANES:(j + 1) * SUBLANES, :] = tile


def _dn_sample(proj, ba, conv_w, alog_row, dtb_row, dn_nw, conv0, s0, row0, n_batch, seq):
    assert SUBLANES % seq == 0 and seq >= DN_CONV - 1
    n_bb = SUBLANES
    rows_in = n_bb * seq
    assert n_batch % n_bb == 0 and row0 % rows_in == 0
    rb0 = row0 // rows_in
    const1 = lambda i: (0, 0)
    return pl.pallas_call(
        functools.partial(_dn_sample_kernel, seq=seq, n_bb=n_bb),
        grid=(n_batch // n_bb,),
        in_specs=[
            pl.BlockSpec((rows_in, DN_CONV_W), lambda i: (rb0 + i, COL_QKV // DN_CONV_W)),
            pl.BlockSpec((rows_in, DN_V_W), lambda i: (rb0 + i, COL_Z // DN_V_W)),
            pl.BlockSpec((rows_in, LANES), lambda i: (rb0 + i, 0)),
            pl.BlockSpec((DN_CONV, DN_CONV_W), const1),
            pl.BlockSpec((1, LANES), const1),
            pl.BlockSpec((1, LANES), const1),
            pl.BlockSpec((1, DN_DV), const1),
            pl.BlockSpec((n_bb, DN_CONV - 1, DN_CONV_W), lambda i: (i, 0, 0)),
            pl.BlockSpec((n_bb, DN_HEADS, DN_DK, DN_DV), lambda i: (i, 0, 0, 0)),
        ],
        out_specs=[
            pl.BlockSpec((rows_in, DN_V_W), lambda i: (i, 0)),
            pl.BlockSpec((n_bb, DN_CONV - 1, DN_CONV_W), lambda i: (i, 0, 0)),
            pl.BlockSpec((n_bb, DN_HEADS, DN_DK, DN_DV), lambda i: (i, 0, 0, 0)),
        ],
        out_shape=[
            jax.ShapeDtypeStruct((n_batch * seq, DN_V_W), F32),
            jax.ShapeDtypeStruct((n_batch, DN_CONV - 1, DN_CONV_W), F32),
            jax.ShapeDtypeStruct((n_batch, DN_HEADS, DN_DK, DN_DV), F32),
        ],
        scratch_shapes=[pltpu.VMEM((n_bb, 2 * SUBLANES, DN_CONV_W), F32)],
        compiler_params=_cparams(("arbitrary",)),
        name="dn_sample",
    )(proj, proj, ba, conv_w, alog_row, dtb_row, dn_nw, conv0, s0)


def _rel_bucket(dist):
    n = jnp.maximum(dist, 0)
    max_exact = REL_BUCKETS // 2
    large = max_exact + (jnp.log(jnp.maximum(n, 1).astype(F32) / max_exact)
                         / math.log(REL_MAX_DIST / max_exact) * (REL_BUCKETS - max_exact)).astype(I32)
    return jnp.where(n < max_exact, n, jnp.minimum(large, REL_BUCKETS - 1))


def _relbias_kernel(tab_ref, bucket_ref, o_ref):
    h = pl.program_id(0)
    bk = bucket_ref[...]
    acc = jnp.zeros(bk.shape, F32)
    for b in range(REL_BUCKETS):
        acc = jnp.where(bk == b, tab_ref[b * SW_HEADS + h], acc)
    o_ref[0] = acc


def _relbias(rel_table, bucket):
    nq, ns = bucket.shape
    return pl.pallas_call(
        _relbias_kernel,
        grid=(SW_HEADS,),
        in_specs=[
            pl.BlockSpec(memory_space=pltpu.SMEM),
            pl.BlockSpec((nq, ns), lambda h: (0, 0)),
        ],
        out_specs=pl.BlockSpec((1, nq, ns), lambda h: (h, 0, 0)),
        out_shape=jax.ShapeDtypeStruct((SW_HEADS, nq, ns), F32),
        compiler_params=_cparams(("arbitrary",)),
        name="relbias",
    )(rel_table.reshape(-1), bucket)


def _dup_halves(x):
    lo = _lane_iota(x.shape) < SW_HD
    xr = pltpu.roll(x, SW_HD, axis=1)
    return jnp.where(lo, x, xr).astype(BF16), jnp.where(lo, xr, x).astype(BF16)


def _sink_softmax_pv(s, valid, sink, vv):
    s = jnp.where(valid, s, -jnp.inf)
    m = jnp.maximum(jnp.max(s, axis=-1, keepdims=True), sink)
    p = jnp.exp(s - m)
    p = p / (jnp.sum(p, axis=-1, keepdims=True) + jnp.exp(sink - m))
    return _dot(p, vv)


def _swa_prompt_kernel(sinks_ref, q_ref, kc_ref, kp_ref, vc_ref, vp_ref, bias_ref, o_ref, klast_ref, vlast_ref):
    i = pl.program_id(1)
    klast_ref[0] = kc_ref[...]
    vlast_ref[0] = vc_ref[...]
    kk = _dup_halves(jnp.concatenate([kp_ref[...], kc_ref[...]], axis=0))
    vv = _dup_halves(jnp.concatenate([vp_ref[...], vc_ref[...]], axis=0))
    shape = (SW_BLOCK, 2 * SW_BLOCK)
    qi = _row_iota(shape)
    kj = _lane_iota(shape)
    dist = qi - kj + SW_BLOCK
    first_key = jnp.where(i > 0, 0, SW_BLOCK)
    valid = (dist >= 0) & (dist < WINDOW) & (kj >= first_key)
    lo = _lane_iota((SW_BLOCK, LANES)) < SW_HD
    for pair in range(SW_HEADS // 2):
        qp = q_ref[:, pair * LANES:(pair + 1) * LANES]
        outs = []
        for half in range(2):
            hq = 2 * pair + half
            kv = hq // SW_GROUP
            qm = jnp.where(lo if half == 0 else ~lo, qp, 0.0)
            s = _dot_nt(qm, kk[kv]) * (SW_HD ** -0.5) + bias_ref[hq]
            outs.append(_sink_softmax_pv(s, valid, sinks_ref[hq], vv[kv]))
        o_ref[:, pair * LANES:(pair + 1) * LANES] = jnp.where(lo, outs[0], outs[1])


def _swa_prompt(proj, sinks, bias, n_batch, seq):
    assert seq % SW_BLOCK == 0 and WINDOW == SW_BLOCK
    nb = seq // SW_BLOCK
    cur = lambda col: (lambda b, i: (b * nb + i, col))
    prev = lambda col: (lambda b, i: (b * nb + jnp.maximum(i - 1, 0), col))
    return pl.pallas_call(
        _swa_prompt_kernel,
        grid=(n_batch, nb),
        in_specs=[
            pl.BlockSpec(memory_space=pltpu.SMEM),
            pl.BlockSpec((SW_BLOCK, SW_HEADS * SW_HD), cur(COL_SQ // (SW_HEADS * SW_HD))),
            pl.BlockSpec((SW_BLOCK, SW_KV_W), cur(COL_SK // SW_KV_W)),
            pl.BlockSpec((SW_BLOCK, SW_KV_W), prev(COL_SK // SW_KV_W)),
            pl.BlockSpec((SW_BLOCK, SW_KV_W), cur(COL_SV // SW_KV_W)),
            pl.BlockSpec((SW_BLOCK, SW_KV_W), prev(COL_SV // SW_KV_W)),
            pl.BlockSpec((SW_HEADS, SW_BLOCK, 2 * SW_BLOCK), lambda b, i: (0, 0, 0)),
        ],
        out_specs=[
            pl.BlockSpec((SW_BLOCK, SW_HEADS * SW_HD), lambda b, i: (b * nb + i, 0)),
            pl.BlockSpec((1, SW_BLOCK, SW_KV_W), lambda b, i: (b, 0, 0)),
            pl.BlockSpec((1, SW_BLOCK, SW_KV_W), lambda b, i: (b, 0, 0)),
        ],
        out_shape=[
            jax.ShapeDtypeStruct((n_batch * seq, SW_HEADS * SW_HD), F32),
            jax.ShapeDtypeStruct((n_batch, SW_BLOCK, SW_KV_W), F32),
            jax.ShapeDtypeStruct((n_batch, SW_BLOCK, SW_KV_W), F32),
        ],
        compiler_params=_cparams(("arbitrary", "arbitrary")),
        name="swa_prompt",
    )(sinks, proj, proj, proj, proj, proj, bias)


def _swa_sample_kernel(q_ref, kn_ref, vn_ref, kc_ref, vc_ref, bias_ref, sink_ref,
                       o_ref, ko_ref, vo_ref, kall_ref, vall_ref, *, seq, n_bb, n_cache):
    n_keys = kall_ref.shape[0]
    zeros_tail = jnp.zeros((n_keys - n_cache - SUBLANES, LANES), F32)
    shape = (SW_GROUP * SUBLANES, n_keys)
    t = _row_iota(shape) % SUBLANES
    s_idx = _lane_iota(shape)
    dist = n_cache + t - s_idx
    valid = (dist >= 0) & (dist < WINDOW)
    lo = _lane_iota((SUBLANES, LANES)) < SW_HD
    out = None
    for bb in range(n_bb):
        shift = (SUBLANES - bb * seq) % SUBLANES

        def top(x, shift=shift):
            return x if shift == 0 else pltpu.roll(x, shift, axis=0)

        kall_ref[0:n_cache, :] = kc_ref[bb]
        kall_ref[n_cache:n_cache + SUBLANES, :] = top(kn_ref[...])
        kall_ref[n_cache + SUBLANES:, :] = zeros_tail
        vall_ref[0:n_cache, :] = vc_ref[bb]
        vall_ref[n_cache:n_cache + SUBLANES, :] = top(vn_ref[...])
        vall_ref[n_cache + SUBLANES:, :] = zeros_tail
        ko_ref[bb] = kall_ref[seq:seq + n_cache, :]
        vo_ref[bb] = vall_ref[seq:seq + n_cache, :]
        kk = _dup_halves(kall_ref[...])
        vv = _dup_halves(vall_ref[...])
        q8 = top(q_ref[...])
        pairs = []
        for kv in range(SW_KV_HEADS):
            pieces = []
            for g in range(SW_GROUP):
                hq = kv * SW_GROUP + g
                qp = q8[:, (hq // 2) * LANES:(hq // 2 + 1) * LANES]
                pieces.append(jnp.where(lo if hq % 2 == 0 else ~lo, qp, 0.0))
            qs = jnp.concatenate(pieces, axis=0)
            s = _dot_nt(qs, kk[kv]) * (SW_HD ** -0.5) + bias_ref[kv]
            res = _sink_softmax_pv(s, valid, sink_ref[kv], vv[kv])
            for g in range(0, SW_GROUP, 2):
                pairs.append(jnp.where(lo, res[g * SUBLANES:(g + 1) * SUBLANES],
                                       res[(g + 1) * SUBLANES:(g + 2) * SUBLANES]))
        o = jnp.concatenate(pairs, axis=1)
        back = (bb * seq) % SUBLANES
        o = o if back == 0 else pltpu.roll(o, back, axis=0)
        rows = _row_iota(o.shape)
        sel = (rows >= bb * seq) & (rows < (bb + 1) * seq)
        out = jnp.where(sel, o, 0.0 if out is None else out)
    o_ref[...] = out


def _swa_sample(proj, k_cache, v_cache, bias, sink_rows, row0, n_batch, seq):
    assert SUBLANES % seq == 0
    n_bb = SUBLANES // seq
    n_cache = k_cache.shape[1]
    assert n_batch % n_bb == 0 and row0 % SUBLANES == 0 and n_cache % SUBLANES == 0
    n_keys = bias.shape[-1]
    rb0 = row0 // SUBLANES
    blk = lambda col: (lambda i: (rb0 + i, col))
    return pl.pallas_call(
        functools.partial(_swa_sample_kernel, seq=seq, n_bb=n_bb, n_cache=n_cache),
        grid=(n_batch // n_bb,),
        in_specs=[
            pl.BlockSpec((SUBLANES, SW_HEADS * SW_HD), blk(COL_SQ // (SW_HEADS * SW_HD))),
            pl.BlockSpec((SUBLANES, SW_KV_W), blk(COL_SK // SW_KV_W)),
            pl.BlockSpec((SUBLANES, SW_KV_W), blk(COL_SV // SW_KV_W)),
            pl.BlockSpec((n_bb, n_cache, SW_KV_W), lambda i: (i, 0, 0)),
            pl.BlockSpec((n_bb, n_cache, SW_KV_W), lambda i: (i, 0, 0)),
            pl.BlockSpec((SW_KV_HEADS, SW_GROUP * SUBLANES, n_keys), lambda i: (0, 0, 0)),
            pl.BlockSpec((SW_KV_HEADS, SW_GROUP * SUBLANES, n_keys), lambda i: (0, 0, 0)),
        ],
        out_specs=[
            pl.BlockSpec((SUBLANES, SW_HEADS * SW_HD), lambda i: (i, 0)),
            pl.BlockSpec((n_bb, n_cache, SW_KV_W), lambda i: (i, 0, 0)),
            pl.BlockSpec((n_bb, n_cache, SW_KV_W), lambda i: (i, 0, 0)),
        ],
        out_shape=[
            jax.ShapeDtypeStruct((n_batch * seq, SW_HEADS * SW_HD), F32),
            jax.ShapeDtypeStruct(k_cache.shape, F32),
            jax.ShapeDtypeStruct(v_cache.shape, F32),
        ],
        scratch_shapes=[pltpu.VMEM((n_keys, SW_KV_W), F32), pltpu.VMEM((n_keys, SW_KV_W), F32)],
        compiler_params=_cparams(("arbitrary",)),
        name="swa_sample",
    )(proj, proj, proj, k_cache, v_cache, bias, sink_rows)


def _mix_kernel(xp_ref, xs_ref, oap_ref, oas_ref, obp_ref, obs_ref, ga_ref, gb_ref, wo_ref, nw_ref, wr_ref, br_ref,
                x1_ref, h2_ref, route_ref, *, n_prompt_tiles):
    i = pl.program_id(0)

    def run(x_ref, oa_ref, ob_ref):
        mixed = _sigmoid(ga_ref[...]) * oa_ref[...] + _sigmoid(gb_ref[...]) * ob_ref[...]
        x1 = x_ref[...] + _dot(mixed, wo_ref[...])
        x1_ref[...] = x1
        h2 = x1 * lax.rsqrt(jnp.mean(x1 * x1, axis=-1, keepdims=True) + EPS) * nw_ref[...]
        h2_ref[...] = h2
        logits = _dot_exact(h2, wr_ref[...]) + br_ref[...]
        lane = _lane_iota(logits.shape)
        lanef = lane.astype(F32)
        big = float(2 * LANES)
        is_g = lane < N_GROUPS
        gl = jnp.where(is_g, logits, -jnp.inf)
        gmax = jnp.max(gl, axis=-1, keepdims=True)
        gval = 1.0 / jnp.sum(jnp.where(is_g, jnp.exp(gl - gmax), 0.0), axis=-1, keepdims=True)
        grp = jnp.min(jnp.where(gl == gmax, lanef, big), axis=-1, keepdims=True)
        e_grp = ((lane - N_GROUPS) >> 3).astype(F32)
        is_e = (lane >= N_GROUPS) & (lane < N_GROUPS + N_EXPERTS) & (e_grp == grp)
        el = jnp.where(is_e, logits, -jnp.inf)
        v1 = jnp.max(el, axis=-1, keepdims=True)
        i1 = jnp.min(jnp.where(el == v1, lanef, big), axis=-1, keepdims=True)
        el2 = jnp.where(lanef == i1, -jnp.inf, el)
        v2 = jnp.max(el2, axis=-1, keepdims=True)
        i2 = jnp.min(jnp.where(el2 == v2, lanef, big), axis=-1, keepdims=True)
        e2 = jnp.exp(v2 - v1)
        w1 = gval / (1.0 + e2)
        w2 = gval * e2 / (1.0 + e2)
        route_ref[...] = jnp.where(lane == 0, i1 - N_GROUPS,
                                   jnp.where(lane == 1, i2 - N_GROUPS,
                                             jnp.where(lane == 2, w1, jnp.where(lane == 3, w2, 0.0))))

    @pl.when(i < n_prompt_tiles)
    def _():
        run(xp_ref, oap_ref, obp_ref)

    @pl.when(i >= n_prompt_tiles)
    def _():
        run(xs_ref, oas_ref, obs_ref)


def _mix(xp, xs, oa_p, oa_s, ob_p, ob_s, proj, w_out, norm_w, w_router, b_router):
    tp, ts = xp.shape[0], xs.shape[0]
    tm = _token_tile(tp, ts)
    npt, nst = tp // tm, ts // tm
    const = lambda i: (0, 0)
    row = lambda i: (i, 0)
    return pl.pallas_call(
        functools.partial(_mix_kernel, n_prompt_tiles=npt),
        grid=(npt + nst,),
        in_specs=[
            pl.BlockSpec((tm, D_MODEL), lambda i: (jnp.minimum(i, npt - 1), 0)),
            pl.BlockSpec((tm, D_MODEL), lambda i: (jnp.maximum(i - npt, 0), 0)),
            pl.BlockSpec((tm, D_MODEL), lambda i: (jnp.minimum(i, npt - 1), 0)),
            pl.BlockSpec((tm, D_MODEL), lambda i: (jnp.maximum(i - npt, 0), 0)),
            pl.BlockSpec((tm, D_MODEL), lambda i: (jnp.minimum(i, npt - 1), 0)),
            pl.BlockSpec((tm, D_MODEL), lambda i: (jnp.maximum(i - npt, 0), 0)),
            pl.BlockSpec((tm, D_MODEL), lambda i: (i, COL_GA // D_MODEL)),
            pl.BlockSpec((tm, D_MODEL), lambda i: (i, COL_GB // D_MODEL)),
            pl.BlockSpec((D_MODEL, D_MODEL), const),
            pl.BlockSpec((1, D_MODEL), const),
            pl.BlockSpec((D_MODEL, LANES), const),
            pl.BlockSpec((1, LANES), const),
        ],
        out_specs=[
            pl.BlockSpec((tm, D_MODEL), row),
            pl.BlockSpec((tm, D_MODEL), row),
            pl.BlockSpec((tm, LANES), row),
        ],
        out_shape=[
            jax.ShapeDtypeStruct((tp + ts, D_MODEL), F32),
            jax.ShapeDtypeStruct((tp + ts, D_MODEL), F32),
            jax.ShapeDtypeStruct((tp + ts, LANES), F32),
        ],
        compiler_params=_cparams(("arbitrary",)),
        name="mix_router",
    )(xp, xs, oa_p, oa_s, ob_p, ob_s, proj, proj, w_out, norm_w, w_router, b_router)


def _rank_kernel(route_ref, dest_ref, meta_ref, rank_ref, cnt_ref, *, tile, blk):
    phase = pl.program_id(0)
    i = pl.program_id(1)
    shape = (tile, LANES)
    lane = _lane_iota(shape)
    lanef = lane.astype(F32)
    r = route_ref[...]
    oh0 = lanef == _col(r, 0)
    oh1 = lanef == _col(r, 1)
    rows = pl.ds(pl.multiple_of(i * tile, tile), tile)

    @pl.when(phase == 0)
    def _():
        @pl.when(i == 0)
        def _():
            cnt_ref[...] = jnp.zeros(cnt_ref.shape, F32)

        oh = jnp.where(oh0 | oh1, 1.0, 0.0)
        tri = jnp.where(_row_iota((tile, tile)) > _lane_iota((tile, tile)), 1.0, 0.0)
        before = _dot(tri, oh) + cnt_ref[0:1, :]
        rank0 = jnp.sum(jnp.where(oh0, before, 0.0), axis=-1, keepdims=True)
        rank1 = jnp.sum(jnp.where(oh1, before, 0.0), axis=-1, keepdims=True)
        rank_ref[rows, :] = jnp.where(lane == 0, rank0, jnp.where(lane == 1, rank1, 0.0))
        cnt_ref[0:1, :] = cnt_ref[0:1, :] + jnp.sum(oh, axis=0, keepdims=True)

    @pl.when(phase == 1)
    def _():
        cnt = cnt_ref[0:1, :]
        padded = jnp.floor((cnt + (blk - 1)) / blk) * blk
        before_lane = jnp.where(_row_iota((LANES, LANES)) < _lane_iota((LANES, LANES)), 1.0, 0.0)
        start = _dot_exact(jnp.broadcast_to(padded, (SUBLANES, LANES)), before_lane)[0:1, :]
        rk = rank_ref[rows, :]
        d0 = jnp.sum(jnp.where(oh0, start, 0.0), axis=-1, keepdims=True) + _col(rk, 0)
        d1 = jnp.sum(jnp.where(oh1, start, 0.0), axis=-1, keepdims=True) + _col(rk, 1)
        dest_ref[...] = jnp.where(lane == 0, d0, jnp.where(lane == 1, d1, 0.0)).astype(I32)

        @pl.when(i == 0)
        def _():
            end = start + padded
            mshape = meta_ref.shape
            blk_start = (_row_iota(mshape) * blk).astype(F32)
            hit = (_lane_iota(mshape) < N_EXPERTS) & (end <= blk_start)
            be = jnp.minimum(jnp.sum(jnp.where(hit, 1.0, 0.0), axis=-1, keepdims=True), N_EXPERTS - 1.0)
            n_used = _col(end, N_EXPERTS - 1) / blk
            ml = _lane_iota(mshape)
            meta_ref[...] = jnp.where(ml == 0, be, jnp.where(ml == 1, n_used, 0.0)).astype(I32)


def _rank(route, tile, blk, n_blocks):
    t = route.shape[0]
    nt = t // tile
    nbp = -(-n_blocks // SUBLANES) * SUBLANES
    return pl.pallas_call(
        functools.partial(_rank_kernel, tile=tile, blk=blk),
        grid=(2, nt),
        in_specs=[pl.BlockSpec((tile, LANES), lambda p, i: (i, 0))],
        out_specs=[
            pl.BlockSpec((tile, LANES), lambda p, i: (i * p, 0)),
            pl.BlockSpec((nbp, LANES), lambda p, i: (0, 0)),
        ],
        out_shape=[
            jax.ShapeDtypeStruct((t, LANES), I32),
            jax.ShapeDtypeStruct((nbp, LANES), I32),
        ],
        scratch_shapes=[pltpu.VMEM((t, LANES), F32), pltpu.VMEM((SUBLANES, LANES), F32)],
        compiler_params=_cparams(("arbitrary", "arbitrary")),
        name="moe_rank",
    )(route)


def _row_copy(src, src_row, dst, dst_row, sem):
    return pltpu.make_async_copy(src.at[pl.ds(src_row, 1)], dst.at[pl.ds(dst_row, 1)], sem)


def _dispatch_kernel(dest_ref, h2_ref, xin_ref, xs_ref, sem, *, tile):
    del xin_ref
    base = pl.program_id(0) * tile

    def issue(t, carry):
        for k in range(2):
            _row_copy(h2_ref, t, xs_ref, dest_ref[2 * (base + t) + k], sem).start()
        return carry

    def drain(t, carry):
        for k in range(2):
            _row_copy(h2_ref, 0, xs_ref, 0, sem).wait()
        return carry

    lax.fori_loop(0, tile, issue, 0)
    lax.fori_loop(0, tile, drain, 0)


def _dispatch(dest_flat, h2, xs_init, tile):
    t = h2.shape[0]
    return pl.pallas_call(
        functools.partial(_dispatch_kernel, tile=tile),
        grid_spec=pltpu.PrefetchScalarGridSpec(
            num_scalar_prefetch=1,
            grid=(t // tile,),
            in_specs=[pl.BlockSpec((tile, D_MODEL), lambda i, d: (i, 0)), pl.BlockSpec(memory_space=pl.ANY)],
            out_specs=pl.BlockSpec(memory_space=pl.ANY),
            scratch_shapes=[pltpu.SemaphoreType.DMA(())],
        ),
        out_shape=jax.ShapeDtypeStruct(xs_init.shape, F32),
        input_output_aliases={2: 0},
        compiler_params=_cparams(("arbitrary",)),
        name="moe_dispatch",
    )(dest_flat, h2, xs_init)


def _expert_kernel(be_ref, nu_ref, x_ref, wg_ref, wu_ref, wd_ref, y_ref):
    del be_ref
    used = pl.program_id(0) < nu_ref[0]

    @pl.when(used)
    def _():
        x = x_ref[...]
        g = _dot(x, wg_ref[0])
        u = _dot(x, wu_ref[0])
        y_ref[...] = _dot(g * _sigmoid(g) * u, wd_ref[0])

    @pl.when(jnp.logical_not(used))
    def _():
        y_ref[...] = jnp.zeros(y_ref.shape, F32)


def _experts(block_expert, n_used, xs, w_gate, w_up, w_down, blk):
    n_blocks = xs.shape[0] // blk
    used = lambda i, be, nu: (jnp.minimum(i, nu[0] - 1), 0)
    wsel = lambda i, be, nu: (be[jnp.minimum(i, nu[0] - 1)], 0, 0)
    return pl.pallas_call(
        _expert_kernel,
        grid_spec=pltpu.PrefetchScalarGridSpec(
            num_scalar_prefetch=2,
            grid=(n_blocks,),
            in_specs=[
                pl.BlockSpec((blk, D_MODEL), used),
                pl.BlockSpec((1, D_MODEL, D_EXPERT), wsel),
                pl.BlockSpec((1, D_MODEL, D_EXPERT), wsel),
                pl.BlockSpec((1, D_EXPERT, D_MODEL), wsel),
            ],
            out_specs=pl.BlockSpec((blk, D_MODEL), lambda i, be, nu: (i, 0)),
        ),
        out_shape=jax.ShapeDtypeStruct(xs.shape, F32),
        compiler_params=_cparams(("arbitrary",)),
        name="moe_experts",
    )(block_expert, n_used, xs, w_gate, w_up, w_down)


def _combine_kernel(dest_ref, x1_ref, route_ref, nw_ref, ys_ref, yp_ref, ysm_ref, ybuf_ref, sem,
                    *, tile, n_prompt_tiles):
    i = pl.program_id(0)
    base = i * tile

    def issue(t, carry):
        for k in range(2):
            _row_copy(ys_ref, dest_ref[2 * (base + t) + k], ybuf_ref.at[k], t, sem).start()
        return carry

    def drain(t, carry):
        for k in range(2):
            _row_copy(ys_ref, 0, ybuf_ref.at[k], 0, sem).wait()
        return carry

    lax.fori_loop(0, tile, issue, 0)
    lax.fori_loop(0, tile, drain, 0)
    r = route_ref[...]
    y = ybuf_ref[0] * _col(r, 2) + ybuf_ref[1] * _col(r, 3)
    x2 = x1_ref[...] + y
    out = x2 * lax.rsqrt(jnp.mean(x2 * x2, axis=-1, keepdims=True) + EPS) * nw_ref[...]

    @pl.when(i < n_prompt_tiles)
    def _():
        yp_ref[...] = out

    @pl.when(i >= n_prompt_tiles)
    def _():
        ysm_ref[...] = out


def _combine(dest_flat, x1, route, norm_w, ys, tp, ts):
    tile = _token_tile(tp, ts)
    npt, nst = tp // tile, ts // tile
    return pl.pallas_call(
        functools.partial(_combine_kernel, tile=tile, n_prompt_tiles=npt),
        grid_spec=pltpu.PrefetchScalarGridSpec(
            num_scalar_prefetch=1,
            grid=(npt + nst,),
            in_specs=[
                pl.BlockSpec((tile, D_MODEL), lambda i, d: (i, 0)),
                pl.BlockSpec((tile, LANES), lambda i, d: (i, 0)),
                pl.BlockSpec((1, D_MODEL), lambda i, d: (0, 0)),
                pl.BlockSpec(memory_space=pl.ANY),
            ],
            out_specs=[
                pl.BlockSpec((tile, D_MODEL), lambda i, d: (jnp.minimum(i, npt - 1), 0)),
                pl.BlockSpec((tile, D_MODEL), lambda i, d: (jnp.maximum(i - npt, 0), 0)),
            ],
            scratch_shapes=[pltpu.VMEM((2, tile, D_MODEL), F32), pltpu.SemaphoreType.DMA(())],
        ),
        out_shape=[
            jax.ShapeDtypeStruct((tp, D_MODEL), F32),
            jax.ShapeDtypeStruct((ts, D_MODEL), F32),
        ],
        compiler_params=_cparams(("arbitrary",)),
        name="moe_combine",
    )(dest_flat, x1, route, norm_w, ys)


def _layer(xp, xs, n_batch, seq, s_batch, s_seq, conv_state, dn_state, k_cache, v_cache,
           w_in, conv_w, a_log, dt_bias, dn_norm_w, sinks, rel_bias, w_out, norm_mix_w, norm_ffn_w,
           w_rg, b_rg, w_re, b_re, w_gate, w_up, w_down, norm_final_w):
    tp, ts = xp.shape[0], xs.shape[0]
    t_all = tp + ts
    row = lambda v: v.reshape(1, -1).astype(F32)

    o = np.cumsum((0, DN_QK_W, DN_QK_W, DN_V_W, DN_V_W, DN_HEADS, DN_HEADS, SW_HEADS * SW_HD, SW_KV_W, SW_KV_W,
                   D_MODEL, D_MODEL)).tolist()
    w_big = jnp.concatenate([w_in[:, o[0]:o[4]], w_in[:, o[6]:o[7]], w_in[:, o[9]:o[11]], w_in[:, o[7]:o[9]]],
                            axis=1).astype(BF16)
    w_small = jnp.pad(w_in[:, o[4]:o[6]], ((0, 0), (0, LANES - 2 * DN_HEADS))).astype(BF16)
    head_row = lambda v: jnp.pad(v.astype(F32), (DN_HEADS, LANES - 2 * DN_HEADS)).reshape(1, LANES)
    w_router = jnp.pad(jnp.concatenate([w_rg, w_re], axis=1).astype(F32),
                       ((0, 0), (0, LANES - N_GROUPS - N_EXPERTS)))
    b_router = jnp.pad(jnp.concatenate([b_rg, b_re]).astype(F32), (0, LANES - N_GROUPS - N_EXPERTS)).reshape(1, LANES)

    proj, ba = _inproj(xp, xs, row(norm_mix_w), w_big, w_small)

    dn_args = (proj, ba, conv_w.astype(F32), head_row(a_log), head_row(dt_bias), row(dn_norm_w))
    conv0 = jnp.zeros((n_batch, DN_CONV - 1, DN_CONV_W), F32)
    dn0 = jnp.zeros((n_batch, DN_HEADS, DN_DK, DN_DV), F32)
    oa_p, p_conv, p_dn = _dn_prompt(*dn_args, conv0, dn0, n_batch, seq)
    oa_s, s_conv, s_dn = _dn_sample(*dn_args, conv_state, dn_state, tp, s_batch, s_seq)

    qpos = jnp.arange(SW_BLOCK)[:, None]
    kpos = jnp.arange(2 * SW_BLOCK)[None, :] - SW_BLOCK
    bias_p = _relbias(rel_bias.astype(F32), _rel_bucket(qpos - kpos))
    ob_p, p_k, p_v = _swa_prompt(proj, sinks.astype(F32), bias_p, n_batch, seq)
    n_cache = k_cache.shape[1]
    n_keys = -(-(n_cache + SUBLANES) // LANES) * LANES
    tq = n_cache + jnp.arange(SUBLANES)[:, None]
    bias_s = _relbias(rel_bias.astype(F32), _rel_bucket(tq - jnp.arange(n_keys)[None, :]))
    bias_s = bias_s.reshape(SW_KV_HEADS, SW_GROUP * SUBLANES, n_keys)
    sink_rows = jnp.broadcast_to(jnp.repeat(sinks.astype(F32).reshape(SW_KV_HEADS, SW_GROUP), SUBLANES, axis=1)[:, :, None],
                                 (SW_KV_HEADS, SW_GROUP * SUBLANES, n_keys))
    ob_s, s_k, s_v = _swa_sample(proj, k_cache.reshape(s_batch, n_cache, SW_KV_W), v_cache.reshape(s_batch, n_cache, SW_KV_W),
                                 bias_s, sink_rows, tp, s_batch, s_seq)

    x1, h2, route = _mix(xp, xs, oa_p, oa_s, ob_p, ob_s, proj, w_out.astype(BF16), row(norm_ffn_w), w_router, b_router)

    tile = _token_tile(tp, ts)
    n_blocks = -(-2 * t_all // MOE_BLOCK) + N_EXPERTS
    dest, meta = _rank(route, tile, MOE_BLOCK, n_blocks)
    dest_flat = dest[:, :2].reshape(-1)
    block_expert = meta[:n_blocks, 0]
    n_used = meta[0:1, 1]
    xs_sorted = _dispatch(dest_flat, h2, jnp.zeros((n_blocks * MOE_BLOCK, D_MODEL), F32), tile)
    ys = _experts(block_expert, n_used, xs_sorted, w_gate, w_up, w_down, MOE_BLOCK)
    y_p, y_s = _combine(dest_flat, x1, route, row(norm_final_w), ys, tp, ts)

    kv_shape = (n_batch, WINDOW, SW_KV_HEADS, SW_HD)
    return (y_p, y_s, p_conv, p_dn, p_k.reshape(kv_shape), p_v.reshape(kv_shape), s_conv, s_dn,
            s_k.reshape(k_cache.shape), s_v.reshape(v_cache.shape))


def kernel(x_prompt, x_sample, state_dn_conv, state_dn, cache_swa_k, cache_swa_v, w_in, conv_w, a_log, dt_bias, dn_norm_w, sinks, rel_bias, w_out, norm_mix_w, norm_ffn_w, w_router_group, b_router_group, w_router_expert, b_router_expert, w_gate, w_up, w_down, norm_final_w):
    depth = w_in.shape[0]
    assert depth == 1, "the final-norm fusion below assumes a single layer"
    n_batch, seq, _ = x_prompt.shape
    s_batch, s_seq, _ = x_sample.shape
    outs = _layer(x_prompt.reshape(-1, D_MODEL), x_sample.reshape(-1, D_MODEL), n_batch, seq, s_batch, s_seq,
                  state_dn_conv[0], state_dn[0], cache_swa_k[0], cache_swa_v[0],
                  w_in[0], conv_w[0], a_log[0], dt_bias[0], dn_norm_w[0], sinks[0], rel_bias,
                  w_out[0], norm_mix_w[0], norm_ffn_w[0], w_router_group[0], b_router_group[0],
                  w_router_expert[0], b_router_expert[0], w_gate[0], w_up[0], w_down[0], norm_final_w)
    y_p, y_s, p_conv, p_dn, p_k, p_v, s_conv, s_dn, s_k, s_v = outs
    return (y_p.reshape(x_prompt.shape), y_s.reshape(x_sample.shape), p_conv[None], p_dn[None], p_k[None], p_v[None],
            s_conv[None], s_dn[None], s_k[None], s_v[None])
```

```python
import functools
import math

import jax
import jax.numpy as jnp
import numpy as np
from jax import lax
from jax.experimental import pallas as pl
from jax.experimental.pallas import tpu as pltpu

F32 = jnp.float32
BF16 = jnp.bfloat16
I32 = jnp.int32

D_MODEL = 1024
DN_HEADS = 8
DN_DK = 128
DN_DV = 128
DN_CONV = 4
DN_CHUNK = 64
DN_QK_W = DN_HEADS * DN_DK
DN_V_W = DN_HEADS * DN_DV
DN_CONV_W = 2 * DN_QK_W + DN_V_W
SW_HEADS = 16
SW_KV_HEADS = 2
SW_GROUP = SW_HEADS // SW_KV_HEADS
SW_HD = 64
SW_KV_W = SW_KV_HEADS * SW_HD
WINDOW = 128
SW_BLOCK = 128
REL_BUCKETS = 32
REL_MAX_DIST = 128
N_GROUPS = 8
EXP_PER_GROUP = 8
N_EXPERTS = N_GROUPS * EXP_PER_GROUP
D_EXPERT = 256
MOE_BLOCK = 256
EPS = 1e-6

LANES = 128
SUBLANES = 8
VMEM_LIMIT = 56 * 1024 * 1024

COL_QKV = 0
COL_Z = 3072
COL_SQ = 4096
COL_GA = 5120
COL_GB = 6144
COL_SK = 7168
COL_SV = 7296
PROJ_W = 7424
PROJ_CHUNK = 512


def _cparams(sem):
    return pltpu.CompilerParams(dimension_semantics=sem, vmem_limit_bytes=VMEM_LIMIT)


def _sigmoid(x):
    return 1.0 / (1.0 + jnp.exp(-x))


def _dot(a, b):
    return jnp.dot(a.astype(BF16), b.astype(BF16), preferred_element_type=F32)


def _dot_nt(a, b):
    return lax.dot_general(a.astype(BF16), b.astype(BF16), (((1,), (1,)), ((), ())), preferred_element_type=F32)


def _dot_tn(a, b):
    return lax.dot_general(a.astype(BF16), b.astype(BF16), (((0,), (0,)), ((), ())), preferred_element_type=F32)


def _dot_exact(a, b):
    return jnp.dot(a, b, precision=lax.Precision.HIGHEST, preferred_element_type=F32)


def _lane_iota(shape):
    return lax.broadcasted_iota(I32, shape, len(shape) - 1)


def _row_iota(shape):
    return lax.broadcasted_iota(I32, shape, len(shape) - 2)


def _col(x, j):
    return jnp.sum(jnp.where(_lane_iota(x.shape) == j, x, 0.0), axis=-1, keepdims=True)


def _token_tile(*sizes):
    for t in (256, 128, 64, 32, 16, 8):
        if all(s % t == 0 for s in sizes):
            return t
    raise ValueError(f"token counts {sizes} need a common tile that is a multiple of 8")


def _inproj_kernel(xp_ref, xs_ref, nw_ref, wb_ref, ws_ref, convw_ref, conv0_ref, proj_ref, ba_ref, tail_ref, cbuf_ref,
                   *, n_prompt_tiles, tiles_per_seq):
    i = pl.program_id(0)
    tm = xp_ref.shape[0]
    hist = SUBLANES - (DN_CONV - 1)

    def project(x_ref, conv):
        x = x_ref[...]
        h = (x * lax.rsqrt(jnp.mean(x * x, axis=-1, keepdims=True) + EPS) * nw_ref[...]).astype(BF16)
        ba_ref[...] = jnp.dot(h, ws_ref[...], preferred_element_type=F32)
        for c0 in range(0, PROJ_W, PROJ_CHUNK):
            c1 = min(c0 + PROJ_CHUNK, PROJ_W)
            res = jnp.dot(h, wb_ref[:, c0:c1], preferred_element_type=F32)
            if conv and c1 <= COL_QKV + DN_CONV_W:
                cbuf_ref[SUBLANES:SUBLANES + tm, c0:c1] = res
                acc = cbuf_ref[hist:hist + tm, c0:c1] * convw_ref[0:1, c0:c1]
                for j in range(1, DN_CONV):
                    acc = acc + cbuf_ref[hist + j:hist + j + tm, c0:c1] * convw_ref[j:j + 1, c0:c1]
                res = acc * _sigmoid(acc)
            proj_ref[:, c0:c1] = res

    @pl.when(i < n_prompt_tiles)
    def _():
        @pl.when(i % tiles_per_seq == 0)
        def _():
            cbuf_ref[hist:SUBLANES, :] = conv0_ref[0]

        project(xp_ref, True)
        tail_ref[0] = cbuf_ref[tm:tm + SUBLANES, :]
        cbuf_ref[hist:SUBLANES, :] = cbuf_ref[SUBLANES + tm - (DN_CONV - 1):SUBLANES + tm, :]

    @pl.when(i >= n_prompt_tiles)
    def _():
        project(xs_ref, False)


def _inproj(xp, xs, norm_w, w_big, w_small, conv_w, conv0, seq):
    tp, ts = xp.shape[0], xs.shape[0]
    tm = _token_tile(tp, ts, seq)
    assert COL_QKV == 0 and DN_CONV_W % PROJ_CHUNK == 0
    npt, nst = tp // tm, ts // tm
    tps = seq // tm
    const = lambda i: (0, 0)
    seq_of = lambda i: (jnp.minimum(i, npt - 1) // tps, 0, 0)
    return pl.pallas_call(
        functools.partial(_inproj_kernel, n_prompt_tiles=npt, tiles_per_seq=tps),
        grid=(npt + nst,),
        in_specs=[
            pl.BlockSpec((tm, D_MODEL), lambda i: (jnp.minimum(i, npt - 1), 0)),
            pl.BlockSpec((tm, D_MODEL), lambda i: (jnp.maximum(i - npt, 0), 0)),
            pl.BlockSpec((1, D_MODEL), const),
            pl.BlockSpec((D_MODEL, PROJ_W), const, pipeline_mode=pl.Buffered(1)),
            pl.BlockSpec((D_MODEL, LANES), const),
            pl.BlockSpec((DN_CONV, DN_CONV_W), const),
            pl.BlockSpec((1, DN_CONV - 1, DN_CONV_W), seq_of),
        ],
        out_specs=[
            pl.BlockSpec((tm, PROJ_W), lambda i: (i, 0)),
            pl.BlockSpec((tm, LANES), lambda i: (i, 0)),
            pl.BlockSpec((1, SUBLANES, DN_CONV_W), seq_of),
        ],
        out_shape=[
            jax.ShapeDtypeStruct((tp + ts, PROJ_W), F32),
            jax.ShapeDtypeStruct((tp + ts, LANES), F32),
            jax.ShapeDtypeStruct((tp // seq, SUBLANES, DN_CONV_W), F32),
        ],
        scratch_shapes=[pltpu.VMEM((SUBLANES + tm, DN_CONV_W), F32)],
        compiler_params=_cparams(("arbitrary",)),
        name="inproj",
    )(xp, xs, norm_w, w_big, w_small, conv_w, conv0)


def _dn_core(groups, alog, dtb, nw, read_state, write_state, n_seg, seg_valid):
    rows = groups[0][0].shape[0]
    sr = rows // n_seg
    assert sr * n_seg == rows and sr & (sr - 1) == 0 and rows <= LANES
    seg_shift = sr.bit_length() - 1
    ri = _row_iota((rows, rows))
    ci = _lane_iota((rows, rows))
    incl = ri >= ci
    strict = ri > ci
    if n_seg > 1:
        same = (ri >> seg_shift) == (ci >> seg_shift)
        incl = incl & same
        strict = strict & same
    l_incl = incl.astype(F32)
    eye = (ri == ci).astype(F32)
    levels = max(1, math.ceil(math.log2(seg_valid)))

    beta_all, gsum_all, gtot_all, gsum_t = [], [], [], []
    for _, _, ba in groups:
        b_all = _sigmoid(ba)
        sp = ba + dtb
        softplus = jnp.maximum(sp, 0.0) + jnp.log1p(jnp.exp(-jnp.abs(sp)))
        g_all = -jnp.exp(alog) * softplus
        if seg_valid < sr:
            live = (_row_iota((rows, LANES)) & (sr - 1)) < seg_valid
            b_all = jnp.where(live, b_all, 0.0)
            g_all = jnp.where(live, g_all, 0.0)
        gs = _dot_exact(l_incl, g_all)
        beta_all.append(b_all)
        gsum_all.append(gs)
        gtot_all.append(_dot_exact(same.astype(F32), g_all) if n_seg > 1 else gs[rows - 1:rows, :])
        padded = gs if rows == LANES else jnp.concatenate([gs, jnp.zeros((LANES - rows, LANES), F32)], axis=0)
        gsum_t.append(padded.T)

    probs = [(g, h) for g in range(len(groups)) for h in range(DN_HEADS)]
    segs = range(n_seg)
    q, k, v, kb, beta, gsum, gtot = {}, {}, {}, {}, {}, {}, {}
    for p in probs:
        g, h = p
        qkv = groups[g][0]
        qh = qkv[:, h * DN_DK:(h + 1) * DN_DK]
        kh = qkv[:, DN_QK_W + h * DN_DK:DN_QK_W + (h + 1) * DN_DK]
        v[p] = qkv[:, 2 * DN_QK_W + h * DN_DV:2 * DN_QK_W + (h + 1) * DN_DV]
        q[p] = qh * lax.rsqrt(jnp.sum(qh * qh, axis=-1, keepdims=True) + 1e-6) * (DN_DK ** -0.5)
        k[p] = kh * lax.rsqrt(jnp.sum(kh * kh, axis=-1, keepdims=True) + 1e-6)
        beta[p] = _col(beta_all[g], h)
        gsum[p] = _col(gsum_all[g], DN_HEADS + h)
        gtot[p] = _col(gtot_all[g], DN_HEADS + h)
        kb[p] = k[p] * beta[p]
    kq = {p: _dot_nt(jnp.concatenate([kb[p], q[p]], axis=0), k[p]) for p in probs}
    gamma = {(g, h): jnp.exp(jnp.where(incl, gsum[(g, h)] - gsum_t[g][DN_HEADS + h:DN_HEADS + h + 1, :rows], -jnp.inf))
             for g, h in probs}
    attn = {p: kq[p][rows:] * gamma[p] for p in probs}
    pw = {p: -jnp.where(strict, kq[p][:rows] * gamma[p], 0.0) for p in probs}
    t = {p: eye + pw[p] for p in probs}
    for _ in range(1, levels):
        pw = {p: _dot(pw[p], pw[p]) for p in probs}
        t = {p: t[p] + _dot(t[p], pw[p]) for p in probs}
    eg = {p: jnp.exp(gsum[p]) for p in probs}
    uw = {p: _dot(t[p], jnp.concatenate([v[p] * beta[p], kb[p] * eg[p]], axis=1)) for p in probs}
    qg = {p: q[p] * eg[p] for p in probs}
    state = {(p, s): read_state(p[0], s, p[1]) for p in probs for s in segs}
    wq = {(p, s): _dot(jnp.concatenate([uw[p][s * sr:(s + 1) * sr, DN_DV:], qg[p][s * sr:(s + 1) * sr]], axis=0),
                       state[(p, s)]) for p in probs for s in segs}
    join = lambda pieces: pieces[0] if len(pieces) == 1 else jnp.concatenate(pieces, axis=0)
    v_new = {p: uw[p][:, :DN_DV] - join([wq[(p, s)][:sr] for s in segs]) for p in probs}
    o = {p: join([wq[(p, s)][sr:] for s in segs]) + _dot(attn[p], v_new[p]) for p in probs}
    kd = {p: k[p] * jnp.exp(gtot[p] - gsum[p]) for p in probs}
    for p in probs:
        for s in segs:
            r0 = s * sr if n_seg > 1 else 0
            decay = jnp.exp(gtot[p][r0:r0 + 1, :])
            write_state(p[0], s, p[1],
                        state[(p, s)] * decay + _dot_tn(kd[p][s * sr:(s + 1) * sr], v_new[p][s * sr:(s + 1) * sr]))
    outs = []
    for g, (_, z, _) in enumerate(groups):
        heads = []
        for h in range(DN_HEADS):
            oh = o[(g, h)]
            zz = z[:, h * DN_DV:(h + 1) * DN_DV]
            on = oh * lax.rsqrt(jnp.mean(oh * oh, axis=-1, keepdims=True) + EPS) * nw
            heads.append(on * (zz * _sigmoid(zz)))
        outs.append(jnp.concatenate(heads, axis=1))
    return outs


def _dn_prompt_kernel(*refs, chunk, n_batch):
    nb = n_batch
    qkv_refs, z_refs, ba_refs = refs[0:nb], refs[nb:2 * nb], refs[2 * nb:3 * nb]
    alog_ref, dtb_ref, nw_ref, s0_ref, o_ref, sout_ref = refs[3 * nb:]

    @pl.when(pl.program_id(0) == 0)
    def _():
        sout_ref[...] = s0_ref[...]

    groups = [(qkv_refs[b][...], z_refs[b][...], ba_refs[b][...]) for b in range(nb)]

    def read_state(g, s, h):
        return sout_ref[g, h]

    def write_state(g, s, h, val):
        sout_ref[g, h] = val

    outs = _dn_core(groups, alog_ref[...], dtb_ref[...], nw_ref[...], read_state, write_state, 1, chunk)
    for b in range(nb):
        o_ref[b] = outs[b]


def _dn_prompt(proj, ba, alog_row, dtb_row, dn_nw, s0, n_batch, seq):
    chunk = min(DN_CHUNK, seq)
    assert seq % chunk == 0 and chunk % SUBLANES == 0
    nc = seq // chunk
    const2 = lambda c: (0, 0)
    rows = lambda b, col: (lambda c: (b * nc + c, col))
    batches = range(n_batch)
    o, s_out = pl.pallas_call(
        functools.partial(_dn_prompt_kernel, chunk=chunk, n_batch=n_batch),
        grid=(nc,),
        in_specs=(
            [pl.BlockSpec((chunk, DN_CONV_W), rows(b, COL_QKV // DN_CONV_W)) for b in batches]
            + [pl.BlockSpec((chunk, DN_V_W), rows(b, COL_Z // DN_V_W)) for b in batches]
            + [pl.BlockSpec((chunk, LANES), rows(b, 0)) for b in batches]
            + [
                pl.BlockSpec((1, LANES), const2),
                pl.BlockSpec((1, LANES), const2),
                pl.BlockSpec((1, DN_DV), const2),
                pl.BlockSpec((n_batch, DN_HEADS, DN_DK, DN_DV), lambda c: (0, 0, 0, 0)),
            ]
        ),
        out_specs=[
            pl.BlockSpec((n_batch, chunk, DN_V_W), lambda c: (0, c, 0)),
            pl.BlockSpec((n_batch, DN_HEADS, DN_DK, DN_DV), lambda c: (0, 0, 0, 0)),
        ],
        out_shape=[
            jax.ShapeDtypeStruct((n_batch, seq, DN_V_W), F32),
            jax.ShapeDtypeStruct((n_batch, DN_HEADS, DN_DK, DN_DV), F32),
        ],
        compiler_params=_cparams(("arbitrary",)),
        name="dn_prompt",
    )(*([proj] * n_batch), *([proj] * n_batch), *([ba] * n_batch), alog_row, dtb_row, dn_nw, s0)
    return o.reshape(n_batch * seq, DN_V_W), s_out


def _dn_sample_kernel(qkv_ref, z_ref, ba_ref, convw_ref, alog_ref, dtb_ref, nw_ref, conv0_ref, s0_ref,
                      o_ref, convout_ref, sout_ref, cbuf_ref, *, seq, n_bb):
    hist = SUBLANES - (DN_CONV - 1)
    per = SUBLANES // seq
    n_tiles = n_bb // per

    def spread(ref):
        pieces = []
        for j in range(n_tiles):
            x8 = ref[j * SUBLANES:(j + 1) * SUBLANES, :]
            for r in range(per):
                pieces.append(x8 if r == 0 else pltpu.roll(x8, SUBLANES - r * seq, axis=0))
        return pieces

    for bb, piece in enumerate(spread(qkv_ref)):
        cbuf_ref[bb, SUBLANES:2 * SUBLANES, :] = piece
    cbuf_ref[:, hist:SUBLANES, :] = conv0_ref[...]
    w = convw_ref[...]
    acc = cbuf_ref[:, hist:hist + SUBLANES, :] * w[0:1, :]
    for i in range(1, DN_CONV):
        acc = acc + cbuf_ref[:, hist + i:hist + i + SUBLANES, :] * w[i:i + 1, :]
    live = _row_iota(acc.shape) < seq
    qkv = jnp.where(live, acc * _sigmoid(acc), 0.0).reshape(n_bb * SUBLANES, DN_CONV_W)
    convout_ref[...] = cbuf_ref[:, SUBLANES + seq - (DN_CONV - 1):SUBLANES + seq, :]

    def read_state(g, s, h):
        return s0_ref[s, h]

    def write_state(g, s, h, val):
        sout_ref[s, h] = val

    group = (qkv, jnp.concatenate(spread(z_ref), axis=0), jnp.concatenate(spread(ba_ref), axis=0))
    o = _dn_core([group], alog_ref[...], dtb_ref[...], nw_ref[...], read_state, write_state, n_bb, seq)[0]
    rows = _row_iota((SUBLANES, DN_V_W))
    for j in range(n_tiles):
        tile = None
        for r in range(per):
            bb = j * per + r
            piece = o[bb * SUBLANES:(bb + 1) * SUBLANES]
            piece = piece if r == 0 else pltpu.roll(piece, r * seq, axis=0)
            tile = piece if tile is None else jnp.where(rows >= r * seq, piece, tile)
        o_ref[j * SUBLANES:(j + 1) * SUBLANES, :] = tile


def _dn_sample(proj, ba, conv_w, alog_row, dtb_row, dn_nw, conv0, s0, row0, n_batch, seq):
    assert SUBLANES % seq == 0 and seq >= DN_CONV - 1
    n_bb = SUBLANES
    rows_in = n_bb * seq
    assert n_batch % n_bb == 0 and row0 % rows_in == 0
    rb0 = row0 // rows_in
    const1 = lambda i: (0, 0)
    return pl.pallas_call(
        functools.partial(_dn_sample_kernel, seq=seq, n_bb=n_bb),
        grid=(n_batch // n_bb,),
        in_specs=[
            pl.BlockSpec((rows_in, DN_CONV_W), lambda i: (rb0 + i, COL_QKV // DN_CONV_W)),
            pl.BlockSpec((rows_in, DN_V_W), lambda i: (rb0 + i, COL_Z // DN_V_W)),
            pl.BlockSpec((rows_in, LANES), lambda i: (rb0 + i, 0)),
            pl.BlockSpec((DN_CONV, DN_CONV_W), const1),
            pl.BlockSpec((1, LANES), const1),
            pl.BlockSpec((1, LANES), const1),
            pl.BlockSpec((1, DN_DV), const1),
            pl.BlockSpec((n_bb, DN_CONV - 1, DN_CONV_W), lambda i: (i, 0, 0)),
            pl.BlockSpec((n_bb, DN_HEADS, DN_DK, DN_DV), lambda i: (i, 0, 0, 0)),
        ],
        out_specs=[
            pl.BlockSpec((rows_in, DN_V_W), lambda i: (i, 0)),
            pl.BlockSpec((n_bb, DN_CONV - 1, DN_CONV_W), lambda i: (i, 0, 0)),
            pl.BlockSpec((n_bb, DN_HEADS, DN_DK, DN_DV), lambda i: (i, 0, 0, 0)),
        ],
        out_shape=[
            jax.ShapeDtypeStruct((n_batch * seq, DN_V_W), F32),
            jax.ShapeDtypeStruct((n_batch, DN_CONV - 1, DN_CONV_W), F32),
            jax.ShapeDtypeStruct((n_batch, DN_HEADS, DN_DK, DN_DV), F32),
        ],
        scratch_shapes=[pltpu.VMEM((n_bb, 2 * SUBLANES, DN_CONV_W), F32)],
        compiler_params=_cparams(("arbitrary",)),
        name="dn_sample",
    )(proj, proj, ba, conv_w, alog_row, dtb_row, dn_nw, conv0, s0)


def _rel_bucket(dist):
    n = jnp.maximum(dist, 0)
    max_exact = REL_BUCKETS // 2
    large = max_exact + (jnp.log(jnp.maximum(n, 1).astype(F32) / max_exact)
                         / math.log(REL_MAX_DIST / max_exact) * (REL_BUCKETS - max_exact)).astype(I32)
    return jnp.where(n < max_exact, n, jnp.minimum(large, REL_BUCKETS - 1))


def _relbias_kernel(tab_ref, bucket_ref, o_ref):
    h = pl.program_id(0)
    bk = bucket_ref[...]
    acc = jnp.zeros(bk.shape, F32)
    for b in range(REL_BUCKETS):
        acc = jnp.where(bk == b, tab_ref[b * SW_HEADS + h], acc)
    o_ref[0] = acc


def _relbias(rel_table, bucket):
    nq, ns = bucket.shape
    return pl.pallas_call(
        _relbias_kernel,
        grid=(SW_HEADS,),
        in_specs=[
            pl.BlockSpec(memory_space=pltpu.SMEM),
            pl.BlockSpec((nq, ns), lambda h: (0, 0)),
        ],
        out_specs=pl.BlockSpec((1, nq, ns), lambda h: (h, 0, 0)),
        out_shape=jax.ShapeDtypeStruct((SW_HEADS, nq, ns), F32),
        compiler_params=_cparams(("arbitrary",)),
        name="relbias",
    )(rel_table.reshape(-1), bucket)


def _dup_halves(x):
    lo = _lane_iota(x.shape) < SW_HD
    xr = pltpu.roll(x, SW_HD, axis=1)
    return jnp.where(lo, x, xr).astype(BF16), jnp.where(lo, xr, x).astype(BF16)


def _sink_softmax_pv(s, valid, sink, vv):
    s = jnp.where(valid, s, -jnp.inf)
    m = jnp.maximum(jnp.max(s, axis=-1, keepdims=True), sink)
    p = jnp.exp(s - m)
    p = p / (jnp.sum(p, axis=-1, keepdims=True) + jnp.exp(sink - m))
    return _dot(p, vv)


def _swa_prompt_kernel(sinks_ref, q_ref, kc_ref, kp_ref, vc_ref, vp_ref, bias_ref, o_ref, klast_ref, vlast_ref):
    i = pl.program_id(1)
    klast_ref[0] = kc_ref[...]
    vlast_ref[0] = vc_ref[...]
    kk = _dup_halves(jnp.concatenate([kp_ref[...], kc_ref[...]], axis=0))
    vv = _dup_halves(jnp.concatenate([vp_ref[...], vc_ref[...]], axis=0))
    shape = (SW_BLOCK, 2 * SW_BLOCK)
    qi = _row_iota(shape)
    kj = _lane_iota(shape)
    dist = qi - kj + SW_BLOCK
    first_key = jnp.where(i > 0, 0, SW_BLOCK)
    valid = (dist >= 0) & (dist < WINDOW) & (kj >= first_key)
    lo = _lane_iota((SW_BLOCK, LANES)) < SW_HD
    for pair in range(SW_HEADS // 2):
        qp = q_ref[:, pair * LANES:(pair + 1) * LANES]
        outs = []
        for half in range(2):
            hq = 2 * pair + half
            kv = hq // SW_GROUP
            qm = jnp.where(lo if half == 0 else ~lo, qp, 0.0)
            s = _dot_nt(qm, kk[kv]) * (SW_HD ** -0.5) + bias_ref[hq]
            outs.append(_sink_softmax_pv(s, valid, sinks_ref[hq], vv[kv]))
        o_ref[:, pair * LANES:(pair + 1) * LANES] = jnp.where(lo, outs[0], outs[1])


def _swa_prompt(proj, sinks, bias, n_batch, seq):
    assert seq % SW_BLOCK == 0 and WINDOW == SW_BLOCK
    nb = seq // SW_BLOCK
    cur = lambda col: (lambda b, i: (b * nb + i, col))
    prev = lambda col: (lambda b, i: (b * nb + jnp.maximum(i - 1, 0), col))
    return pl.pallas_call(
        _swa_prompt_kernel,
        grid=(n_batch, nb),
        in_specs=[
            pl.BlockSpec(memory_space=pltpu.SMEM),
            pl.BlockSpec((SW_BLOCK, SW_HEADS * SW_HD), cur(COL_SQ // (SW_HEADS * SW_HD))),
            pl.BlockSpec((SW_BLOCK, SW_KV_W), cur(COL_SK // SW_KV_W)),
            pl.BlockSpec((SW_BLOCK, SW_KV_W), prev(COL_SK // SW_KV_W)),
            pl.BlockSpec((SW_BLOCK, SW_KV_W), cur(COL_SV // SW_KV_W)),
            pl.BlockSpec((SW_BLOCK, SW_KV_W), prev(COL_SV // SW_KV_W)),
            pl.BlockSpec((SW_HEADS, SW_BLOCK, 2 * SW_BLOCK), lambda b, i: (0, 0, 0)),
        ],
        out_specs=[
            pl.BlockSpec((SW_BLOCK, SW_HEADS * SW_HD), lambda b, i: (b * nb + i, 0)),
            pl.BlockSpec((1, SW_BLOCK, SW_KV_W), lambda b, i: (b, 0, 0)),
            pl.BlockSpec((1, SW_BLOCK, SW_KV_W), lambda b, i: (b, 0, 0)),
        ],
        out_shape=[
            jax.ShapeDtypeStruct((n_batch * seq, SW_HEADS * SW_HD), F32),
            jax.ShapeDtypeStruct((n_batch, SW_BLOCK, SW_KV_W), F32),
            jax.ShapeDtypeStruct((n_batch, SW_BLOCK, SW_KV_W), F32),
        ],
        compiler_params=_cparams(("arbitrary", "arbitrary")),
        name="swa_prompt",
    )(sinks, proj, proj, proj, proj, proj, bias)


def _swa_sample_kernel(q_ref, kn_ref, vn_ref, kc_ref, vc_ref, bias_ref, sink_ref,
                       o_ref, ko_ref, vo_ref, kall_ref, vall_ref, *, seq, n_bb, n_cache):
    n_keys = kall_ref.shape[0]
    zeros_tail = jnp.zeros((n_keys - n_cache - SUBLANES, LANES), F32)
    shape = (SW_GROUP * SUBLANES, n_keys)
    t = _row_iota(shape) % SUBLANES
    s_idx = _lane_iota(shape)
    dist = n_cache + t - s_idx
    valid = (dist >= 0) & (dist < WINDOW)
    lo = _lane_iota((SUBLANES, LANES)) < SW_HD
    out = None
    for bb in range(n_bb):
        shift = (SUBLANES - bb * seq) % SUBLANES

        def top(x, shift=shift):
            return x if shift == 0 else pltpu.roll(x, shift, axis=0)

        kall_ref[0:n_cache, :] = kc_ref[bb]
        kall_ref[n_cache:n_cache + SUBLANES, :] = top(kn_ref[...])
        kall_ref[n_cache + SUBLANES:, :] = zeros_tail
        vall_ref[0:n_cache, :] = vc_ref[bb]
        vall_ref[n_cache:n_cache + SUBLANES, :] = top(vn_ref[...])
        vall_ref[n_cache + SUBLANES:, :] = zeros_tail
        ko_ref[bb] = kall_ref[seq:seq + n_cache, :]
        vo_ref[bb] = vall_ref[seq:seq + n_cache, :]
        kk = _dup_halves(kall_ref[...])
        vv = _dup_halves(vall_ref[...])
        q8 = top(q_ref[...])
        pairs = []
        for kv in range(SW_KV_HEADS):
            pieces = []
            for g in range(SW_GROUP):
                hq = kv * SW_GROUP + g
                qp = q8[:, (hq // 2) * LANES:(hq // 2 + 1) * LANES]
                pieces.append(jnp.where(lo if hq % 2 == 0 else ~lo, qp, 0.0))
            qs = jnp.concatenate(pieces, axis=0)
            s = _dot_nt(qs, kk[kv]) * (SW_HD ** -0.5) + bias_ref[kv]
            res = _sink_softmax_pv(s, valid, sink_ref[kv], vv[kv])
            for g in range(0, SW_GROUP, 2):
                pairs.append(jnp.where(lo, res[g * SUBLANES:(g + 1) * SUBLANES],
                                       res[(g + 1) * SUBLANES:(g + 2) * SUBLANES]))
        o = jnp.concatenate(pairs, axis=1)
        back = (bb * seq) % SUBLANES
        o = o if back == 0 else pltpu.roll(o, back, axis=0)
        rows = _row_iota(o.shape)
        sel = (rows >= bb * seq) & (rows < (bb + 1) * seq)
        out = jnp.where(sel, o, 0.0 if out is None else out)
    o_ref[...] = out


def _swa_sample(proj, k_cache, v_cache, bias, sink_rows, row0, n_batch, seq):
    assert SUBLANES % seq == 0
    n_bb = SUBLANES // seq
    n_cache = k_cache.shape[1]
    assert n_batch % n_bb == 0 and row0 % SUBLANES == 0 and n_cache % SUBLANES == 0
    n_keys = bias.shape[-1]
    rb0 = row0 // SUBLANES
    blk = lambda col: (lambda i: (rb0 + i, col))
    return pl.pallas_call(
        functools.partial(_swa_sample_kernel, seq=seq, n_bb=n_bb, n_cache=n_cache),
        grid=(n_batch // n_bb,),
        in_specs=[
            pl.BlockSpec((SUBLANES, SW_HEADS * SW_HD), blk(COL_SQ // (SW_HEADS * SW_HD))),
            pl.BlockSpec((SUBLANES, SW_KV_W), blk(COL_SK // SW_KV_W)),
            pl.BlockSpec((SUBLANES, SW_KV_W), blk(COL_SV // SW_KV_W)),
            pl.BlockSpec((n_bb, n_cache, SW_KV_W), lambda i: (i, 0, 0)),
            pl.BlockSpec((n_bb, n_cache, SW_KV_W), lambda i: (i, 0, 0)),
            pl.BlockSpec((SW_KV_HEADS, SW_GROUP * SUBLANES, n_keys), lambda i: (0, 0, 0)),
            pl.BlockSpec((SW_KV_HEADS, SW_GROUP * SUBLANES, n_keys), lambda i: (0, 0, 0)),
        ],
        out_specs=[
            pl.BlockSpec((SUBLANES, SW_HEADS * SW_HD), lambda i: (i, 0)),
            pl.BlockSpec((n_bb, n_cache, SW_KV_W), lambda i: (i, 0, 0)),
            pl.BlockSpec((n_bb, n_cache, SW_KV_W), lambda i: (i, 0, 0)),
        ],
        out_shape=[
            jax.ShapeDtypeStruct((n_batch * seq, SW_HEADS * SW_HD), F32),
            jax.ShapeDtypeStruct(k_cache.shape, F32),
            jax.ShapeDtypeStruct(v_cache.shape, F32),
        ],
        scratch_shapes=[pltpu.VMEM((n_keys, SW_KV_W), F32), pltpu.VMEM((n_keys, SW_KV_W), F32)],
        compiler_params=_cparams(("arbitrary",)),
        name="swa_sample",
    )(proj, proj, proj, k_cache, v_cache, bias, sink_rows)


def _mix_kernel(xp_ref, xs_ref, oap_ref, oas_ref, obp_ref, obs_ref, ga_ref, gb_ref, wo_ref, nw_ref, wr_ref, br_ref,
                x1_ref, h2_ref, route_ref, *, n_prompt_tiles):
    i = pl.program_id(0)

    def run(x_ref, oa_ref, ob_ref):
        mixed = _sigmoid(ga_ref[...]) * oa_ref[...] + _sigmoid(gb_ref[...]) * ob_ref[...]
        x1 = x_ref[...] + _dot(mixed, wo_ref[...])
        x1_ref[...] = x1
        h2 = x1 * lax.rsqrt(jnp.mean(x1 * x1, axis=-1, keepdims=True) + EPS) * nw_ref[...]
        h2_ref[...] = h2
        logits = _dot_exact(h2, wr_ref[...]) + br_ref[...]
        lane = _lane_iota(logits.shape)
        lanef = lane.astype(F32)
        big = float(2 * LANES)
        is_g = lane < N_GROUPS
        gl = jnp.where(is_g, logits, -jnp.inf)
        gmax = jnp.max(gl, axis=-1, keepdims=True)
        gval = 1.0 / jnp.sum(jnp.where(is_g, jnp.exp(gl - gmax), 0.0), axis=-1, keepdims=True)
        grp = jnp.min(jnp.where(gl == gmax, lanef, big), axis=-1, keepdims=True)
        e_grp = ((lane - N_GROUPS) >> 3).astype(F32)
        is_e = (lane >= N_GROUPS) & (lane < N_GROUPS + N_EXPERTS) & (e_grp == grp)
        el = jnp.where(is_e, logits, -jnp.inf)
        v1 = jnp.max(el, axis=-1, keepdims=True)
        i1 = jnp.min(jnp.where(el == v1, lanef, big), axis=-1, keepdims=True)
        el2 = jnp.where(lanef == i1, -jnp.inf, el)
        v2 = jnp.max(el2, axis=-1, keepdims=True)
        i2 = jnp.min(jnp.where(el2 == v2, lanef, big), axis=-1, keepdims=True)
        e2 = jnp.exp(v2 - v1)
        w1 = gval / (1.0 + e2)
        w2 = gval * e2 / (1.0 + e2)
        route_ref[...] = jnp.where(lane == 0, i1 - N_GROUPS,
                                   jnp.where(lane == 1, i2 - N_GROUPS,
                                             jnp.where(lane == 2, w1, jnp.where(lane == 3, w2, 0.0))))

    @pl.when(i < n_prompt_tiles)
    def _():
        run(xp_ref, oap_ref, obp_ref)

    @pl.when(i >= n_prompt_tiles)
    def _():
        run(xs_ref, oas_ref, obs_ref)


def _mix(xp, xs, oa_p, oa_s, ob_p, ob_s, proj, w_out, norm_w, w_router, b_router):
    tp, ts = xp.shape[0], xs.shape[0]
    tm = _token_tile(tp, ts)
    npt, nst = tp // tm, ts // tm
    const = lambda i: (0, 0)
    row = lambda i: (i, 0)
    return pl.pallas_call(
        functools.partial(_mix_kernel, n_prompt_tiles=npt),
        grid=(npt + nst,),
        in_specs=[
            pl.BlockSpec((tm, D_MODEL), lambda i: (jnp.minimum(i, npt - 1), 0)),
            pl.BlockSpec((tm, D_MODEL), lambda i: (jnp.maximum(i - npt, 0), 0)),
            pl.BlockSpec((tm, D_MODEL), lambda i: (jnp.minimum(i, npt - 1), 0)),
            pl.BlockSpec((tm, D_MODEL), lambda i: (jnp.maximum(i - npt, 0), 0)),
            pl.BlockSpec((tm, D_MODEL), lambda i: (jnp.minimum(i, npt - 1), 0)),
            pl.BlockSpec((tm, D_MODEL), lambda i: (jnp.maximum(i - npt, 0), 0)),
            pl.BlockSpec((tm, D_MODEL), lambda i: (i, COL_GA // D_MODEL)),
            pl.BlockSpec((tm, D_MODEL), lambda i: (i, COL_GB // D_MODEL)),
            pl.BlockSpec((D_MODEL, D_MODEL), const),
            pl.BlockSpec((1, D_MODEL), const),
            pl.BlockSpec((D_MODEL, LANES), const),
            pl.BlockSpec((1, LANES), const),
        ],
        out_specs=[
            pl.BlockSpec((tm, D_MODEL), row),
            pl.BlockSpec((tm, D_MODEL), row),
            pl.BlockSpec((tm, LANES), row),
        ],
        out_shape=[
            jax.ShapeDtypeStruct((tp + ts, D_MODEL), F32),
            jax.ShapeDtypeStruct((tp + ts, D_MODEL), F32),
            jax.ShapeDtypeStruct((tp + ts, LANES), F32),
        ],
        compiler_params=_cparams(("arbitrary",)),
        name="mix_router",
    )(xp, xs, oa_p, oa_s, ob_p, ob_s, proj, proj, w_out, norm_w, w_router, b_router)


def _rank_kernel(route_ref, dest_ref, meta_ref, rank_ref, cnt_ref, *, tile, blk):
    phase = pl.program_id(0)
    i = pl.program_id(1)
    shape = (tile, LANES)
    lane = _lane_iota(shape)
    lanef = lane.astype(F32)
    r = route_ref[...]
    oh0 = lanef == _col(r, 0)
    oh1 = lanef == _col(r, 1)
    rows = pl.ds(pl.multiple_of(i * tile, tile), tile)

    @pl.when(phase == 0)
    def _():
        @pl.when(i == 0)
        def _():
            cnt_ref[...] = jnp.zeros(cnt_ref.shape, F32)

        oh = jnp.where(oh0 | oh1, 1.0, 0.0)
        tri = jnp.where(_row_iota((tile, tile)) > _lane_iota((tile, tile)), 1.0, 0.0)
        before = _dot(tri, oh) + cnt_ref[0:1, :]
        rank0 = jnp.sum(jnp.where(oh0, before, 0.0), axis=-1, keepdims=True)
        rank1 = jnp.sum(jnp.where(oh1, before, 0.0), axis=-1, keepdims=True)
        rank_ref[rows, :] = jnp.where(lane == 0, rank0, jnp.where(lane == 1, rank1, 0.0))
        cnt_ref[0:1, :] = cnt_ref[0:1, :] + jnp.sum(oh, axis=0, keepdims=True)

    @pl.when(phase == 1)
    def _():
        cnt = cnt_ref[0:1, :]
        padded = jnp.floor((cnt + (blk - 1)) / blk) * blk
        before_lane = jnp.where(_row_iota((LANES, LANES)) < _lane_iota((LANES, LANES)), 1.0, 0.0)
        start = _dot_exact(jnp.broadcast_to(padded, (SUBLANES, LANES)), before_lane)[0:1, :]
        rk = rank_ref[rows, :]
        d0 = jnp.sum(jnp.where(oh0, start, 0.0), axis=-1, keepdims=True) + _col(rk, 0)
        d1 = jnp.sum(jnp.where(oh1, start, 0.0), axis=-1, keepdims=True) + _col(rk, 1)
        dest_ref[...] = jnp.where(lane == 0, d0, jnp.where(lane == 1, d1, 0.0)).astype(I32)

        @pl.when(i == 0)
        def _():
            end = start + padded
            mshape = meta_ref.shape
            blk_start = (_row_iota(mshape) * blk).astype(F32)
            hit = (_lane_iota(mshape) < N_EXPERTS) & (end <= blk_start)
            be = jnp.minimum(jnp.sum(jnp.where(hit, 1.0, 0.0), axis=-1, keepdims=True), N_EXPERTS - 1.0)
            n_used = _col(end, N_EXPERTS - 1) / blk
            ml = _lane_iota(mshape)
            meta_ref[...] = jnp.where(ml == 0, be, jnp.where(ml == 1, n_used, 0.0)).astype(I32)


def _rank(route, tile, blk, n_blocks):
    t = route.shape[0]
    nt = t // tile
    nbp = -(-n_blocks // SUBLANES) * SUBLANES
    return pl.pallas_call(
        functools.partial(_rank_kernel, tile=tile, blk=blk),
        grid=(2, nt),
        in_specs=[pl.BlockSpec((tile, LANES), lambda p, i: (i, 0))],
        out_specs=[
            pl.BlockSpec((tile, LANES), lambda p, i: (i * p, 0)),
            pl.BlockSpec((nbp, LANES), lambda p, i: (0, 0)),
        ],
        out_shape=[
            jax.ShapeDtypeStruct((t, LANES), I32),
            jax.ShapeDtypeStruct((nbp, LANES), I32),
        ],
        scratch_shapes=[pltpu.VMEM((t, LANES), F32), pltpu.VMEM((SUBLANES, LANES), F32)],
        compiler_params=_cparams(("arbitrary", "arbitrary")),
        name="moe_rank",
    )(route)


def _row_copy(src, src_row, dst, dst_row, sem):
    return pltpu.make_async_copy(src.at[pl.ds(src_row, 1)], dst.at[pl.ds(dst_row, 1)], sem)


def _dispatch_kernel(dest_ref, h2_ref, xin_ref, xs_ref, sem, *, tile):
    del xin_ref
    base = pl.program_id(0) * tile

    def issue(t, carry):
        for k in range(2):
            _row_copy(h2_ref, t, xs_ref, dest_ref[2 * (base + t) + k], sem).start()
        return carry

    def drain(t, carry):
        for k in range(2):
            _row_copy(h2_ref, 0, xs_ref, 0, sem).wait()
        return carry

    lax.fori_loop(0, tile, issue, 0)
    lax.fori_loop(0, tile, drain, 0)


def _dispatch(dest_flat, h2, xs_init, tile):
    t = h2.shape[0]
    return pl.pallas_call(
        functools.partial(_dispatch_kernel, tile=tile),
        grid_spec=pltpu.PrefetchScalarGridSpec(
            num_scalar_prefetch=1,
            grid=(t // tile,),
            in_specs=[pl.BlockSpec((tile, D_MODEL), lambda i, d: (i, 0)), pl.BlockSpec(memory_space=pl.ANY)],
            out_specs=pl.BlockSpec(memory_space=pl.ANY),
            scratch_shapes=[pltpu.SemaphoreType.DMA(())],
        ),
        out_shape=jax.ShapeDtypeStruct(xs_init.shape, F32),
        input_output_aliases={2: 0},
        compiler_params=_cparams(("arbitrary",)),
        name="moe_dispatch",
    )(dest_flat, h2, xs_init)


def _last_used(i, nu_ref):
    return jnp.minimum(i, jnp.maximum(nu_ref[0] - 1, 0))


def _expert_kernel(be_ref, nu_ref, x_ref, wg_ref, wu_ref, wd_ref, y_ref, wgu_ref, wdn_ref):
    i = pl.program_id(0)
    used = i < nu_ref[0]
    blk_i = _last_used(i, nu_ref)
    fresh = (i == 0) | (be_ref[blk_i] != be_ref[jnp.maximum(blk_i - 1, 0)])

    @pl.when(used & fresh)
    def _():
        wgu_ref[:, :D_EXPERT] = wg_ref[0].astype(BF16)
        wgu_ref[:, D_EXPERT:] = wu_ref[0].astype(BF16)
        wdn_ref[...] = wd_ref[0].astype(BF16)

    @pl.when(used)
    def _():
        gu = jnp.dot(x_ref[...].astype(BF16), wgu_ref[...], preferred_element_type=F32)
        g = gu[:, :D_EXPERT]
        hidden = (g * _sigmoid(g) * gu[:, D_EXPERT:]).astype(BF16)
        y_ref[...] = jnp.dot(hidden, wdn_ref[...], preferred_element_type=F32)

    @pl.when(jnp.logical_not(used))
    def _():
        y_ref[...] = jnp.zeros(y_ref.shape, F32)


def _experts(block_expert, n_used, xs, w_gate, w_up, w_down, blk):
    n_blocks = xs.shape[0] // blk
    used = lambda i, be, nu: (_last_used(i, nu), 0)
    wsel = lambda i, be, nu: (be[_last_used(i, nu)], 0, 0)
    return pl.pallas_call(
        _expert_kernel,
        grid_spec=pltpu.PrefetchScalarGridSpec(
            num_scalar_prefetch=2,
            grid=(n_blocks,),
            in_specs=[
                pl.BlockSpec((blk, D_MODEL), used),
                pl.BlockSpec((1, D_MODEL, D_EXPERT), wsel),
                pl.BlockSpec((1, D_MODEL, D_EXPERT), wsel),
                pl.BlockSpec((1, D_EXPERT, D_MODEL), wsel),
            ],
            out_specs=pl.BlockSpec((blk, D_MODEL), lambda i, be, nu: (i, 0)),
            scratch_shapes=[pltpu.VMEM((D_MODEL, 2 * D_EXPERT), BF16), pltpu.VMEM((D_EXPERT, D_MODEL), BF16)],
        ),
        out_shape=jax.ShapeDtypeStruct(xs.shape, F32),
        compiler_params=_cparams(("arbitrary",)),
        name="moe_experts",
    )(block_expert, n_used, xs, w_gate, w_up, w_down)


def _combine_kernel(dest_ref, x1_ref, route_ref, nw_ref, ys_ref, yp_ref, ysm_ref, ybuf_ref, sem,
                    *, tile, n_prompt_tiles):
    i = pl.program_id(0)
    base = i * tile

    def issue(t, carry):
        for k in range(2):
            _row_copy(ys_ref, dest_ref[2 * (base + t) + k], ybuf_ref.at[k], t, sem).start()
        return carry

    def drain(t, carry):
        for k in range(2):
            _row_copy(ys_ref, 0, ybuf_ref.at[k], 0, sem).wait()
        return carry

    lax.fori_loop(0, tile, issue, 0)
    lax.fori_loop(0, tile, drain, 0)
    r = route_ref[...]
    y = ybuf_ref[0] * _col(r, 2) + ybuf_ref[1] * _col(r, 3)
    x2 = x1_ref[...] + y
    out = x2 * lax.rsqrt(jnp.mean(x2 * x2, axis=-1, keepdims=True) + EPS) * nw_ref[...]

    @pl.when(i < n_prompt_tiles)
    def _():
        yp_ref[...] = out

    @pl.when(i >= n_prompt_tiles)
    def _():
        ysm_ref[...] = out


def _combine(dest_flat, x1, route, norm_w, ys, tp, ts):
    tile = _token_tile(tp, ts)
    npt, nst = tp // tile, ts // tile
    return pl.pallas_call(
        functools.partial(_combine_kernel, tile=tile, n_prompt_tiles=npt),
        grid_spec=pltpu.PrefetchScalarGridSpec(
            num_scalar_prefetch=1,
            grid=(npt + nst,),
            in_specs=[
                pl.BlockSpec((tile, D_MODEL), lambda i, d: (i, 0)),
                pl.BlockSpec((tile, LANES), lambda i, d: (i, 0)),
                pl.BlockSpec((1, D_MODEL), lambda i, d: (0, 0)),
                pl.BlockSpec(memory_space=pl.ANY),
            ],
            out_specs=[
                pl.BlockSpec((tile, D_MODEL), lambda i, d: (jnp.minimum(i, npt - 1), 0)),
                pl.BlockSpec((tile, D_MODEL), lambda i, d: (jnp.maximum(i - npt, 0), 0)),
            ],
            scratch_shapes=[pltpu.VMEM((2, tile, D_MODEL), F32), pltpu.SemaphoreType.DMA(())],
        ),
        out_shape=[
            jax.ShapeDtypeStruct((tp, D_MODEL), F32),
            jax.ShapeDtypeStruct((ts, D_MODEL), F32),
        ],
        compiler_params=_cparams(("arbitrary",)),
        name="moe_combine",
    )(dest_flat, x1, route, norm_w, ys)


def _layer(xp, xs, n_batch, seq, s_batch, s_seq, conv_state, dn_state, k_cache, v_cache,
           w_in, conv_w, a_log, dt_bias, dn_norm_w, sinks, rel_bias, w_out, norm_mix_w, norm_ffn_w,
           w_rg, b_rg, w_re, b_re, w_gate, w_up, w_down, norm_final_w):
    tp, ts = xp.shape[0], xs.shape[0]
    t_all = tp + ts
    row = lambda v: v.reshape(1, -1).astype(F32)

    o = np.cumsum((0, DN_QK_W, DN_QK_W, DN_V_W, DN_V_W, DN_HEADS, DN_HEADS, SW_HEADS * SW_HD, SW_KV_W, SW_KV_W,
                   D_MODEL, D_MODEL)).tolist()
    w_big = jnp.concatenate([w_in[:, o[0]:o[4]], w_in[:, o[6]:o[7]], w_in[:, o[9]:o[11]], w_in[:, o[7]:o[9]]],
                            axis=1).astype(BF16)
    w_small = jnp.pad(w_in[:, o[4]:o[6]], ((0, 0), (0, LANES - 2 * DN_HEADS))).astype(BF16)
    head_row = lambda v: jnp.pad(v.astype(F32), (DN_HEADS, LANES - 2 * DN_HEADS)).reshape(1, LANES)
    w_router = jnp.pad(jnp.concatenate([w_rg, w_re], axis=1).astype(F32),
                       ((0, 0), (0, LANES - N_GROUPS - N_EXPERTS)))
    b_router = jnp.pad(jnp.concatenate([b_rg, b_re]).astype(F32), (0, LANES - N_GROUPS - N_EXPERTS)).reshape(1, LANES)

    conv0 = jnp.zeros((n_batch, DN_CONV - 1, DN_CONV_W), F32)
    proj, ba, conv_tail = _inproj(xp, xs, row(norm_mix_w), w_big, w_small, conv_w.astype(F32), conv0, seq)
    p_conv = conv_tail[:, SUBLANES - (DN_CONV - 1):, :]

    dn0 = jnp.zeros((n_batch, DN_HEADS, DN_DK, DN_DV), F32)
    oa_p, p_dn = _dn_prompt(proj, ba, head_row(a_log), head_row(dt_bias), row(dn_norm_w), dn0, n_batch, seq)
    oa_s, s_conv, s_dn = _dn_sample(proj, ba, conv_w.astype(F32), head_row(a_log), head_row(dt_bias), row(dn_norm_w),
                                    conv_state, dn_state, tp, s_batch, s_seq)

    qpos = jnp.arange(SW_BLOCK)[:, None]
    kpos = jnp.arange(2 * SW_BLOCK)[None, :] - SW_BLOCK
    bias_p = _relbias(rel_bias.astype(F32), _rel_bucket(qpos - kpos))
    ob_p, p_k, p_v = _swa_prompt(proj, sinks.astype(F32), bias_p, n_batch, seq)
    n_cache = k_cache.shape[1]
    n_keys = -(-(n_cache + SUBLANES) // LANES) * LANES
    tq = n_cache + jnp.arange(SUBLANES)[:, None]
    bias_s = _relbias(rel_bias.astype(F32), _rel_bucket(tq - jnp.arange(n_keys)[None, :]))
    bias_s = bias_s.reshape(SW_KV_HEADS, SW_GROUP * SUBLANES, n_keys)
    sink_rows = jnp.broadcast_to(jnp.repeat(sinks.astype(F32).reshape(SW_KV_HEADS, SW_GROUP), SUBLANES, axis=1)[:, :, None],
                                 (SW_KV_HEADS, SW_GROUP * SUBLANES, n_keys))
    ob_s, s_k, s_v = _swa_sample(proj, k_cache.reshape(s_batch, n_cache, SW_KV_W), v_cache.reshape(s_batch, n_cache, SW_KV_W),
                                 bias_s, sink_rows, tp, s_batch, s_seq)

    x1, h2, route = _mix(xp, xs, oa_p, oa_s, ob_p, ob_s, proj, w_out.astype(BF16), row(norm_ffn_w), w_router, b_router)

    tile = _token_tile(tp, ts)
    n_blocks = -(-2 * t_all // MOE_BLOCK) + N_EXPERTS
    dest, meta = _rank(route, tile, MOE_BLOCK, n_blocks)
    dest_flat = dest[:, :2].reshape(-1)
    block_expert = meta[:n_blocks, 0]
    n_used = meta[0:1, 1]
    xs_sorted = _dispatch(dest_flat, h2, jnp.zeros((n_blocks * MOE_BLOCK, D_MODEL), F32), tile)
    ys = _experts(block_expert, n_used, xs_sorted, w_gate, w_up, w_down, MOE_BLOCK)
    y_p, y_s = _combine(dest_flat, x1, route, row(norm_final_w), ys, tp, ts)

    kv_shape = (n_batch, WINDOW, SW_KV_HEADS, SW_HD)
    return (y_p, y_s, p_conv, p_dn, p_k.reshape(kv_shape), p_v.reshape(kv_shape), s_conv, s_dn,
            s_k.reshape(k_cache.shape), s_v.reshape(v_cache.shape))


def kernel(x_prompt, x_sample, state_dn_conv, state_dn, cache_swa_k, cache_swa_v, w_in, conv_w, a_log, dt_bias, dn_norm_w, sinks, rel_bias, w_out, norm_mix_w, norm_ffn_w, w_router_group, b_router_group, w_router_expert, b_router_expert, w_gate, w_up, w_down, norm_final_w):
    depth = w_in.shape[0]
    assert depth == 1, "the final-norm fusion below assumes a single layer"
    n_batch, seq, _ = x_prompt.shape
    s_batch, s_seq, _ = x_sample.shape
    outs = _layer(x_prompt.reshape(-1, D_MODEL), x_sample.reshape(-1, D_MODEL), n_batch, seq, s_batch, s_seq,
                  state_dn_conv[0], state_dn[0], cache_swa_k[0], cache_swa_v[0],
                  w_in[0], conv_w[0], a_log[0], dt_bias[0], dn_norm_w[0], sinks[0], rel_bias,
                  w_out[0], norm_mix_w[0], norm_ffn_w[0], w_router_group[0], b_router_group[0],
                  w_router_expert[0], b_router_expert[0], w_gate[0], w_up[0], w_down[0], norm_final_w)
    y_p, y_s, p_conv, p_dn, p_k, p_v, s_conv, s_dn, s_k, s_v = outs
    return (y_p.reshape(x_prompt.shape), y_s.reshape(x_sample.shape), p_conv[None], p_dn[None], p_k[None], p_v[None],
            s_conv[None], s_dn[None], s_k[None], s_v[None])
```

```python
import functools
import math

import jax
import jax.numpy as jnp
import numpy as np
from jax import lax
from jax.experimental import pallas as pl
from jax.experimental.pallas import tpu as pltpu

F32 = jnp.float32
BF16 = jnp.bfloat16
I32 = jnp.int32

D_MODEL = 1024
DN_HEADS = 8
DN_DK = 128
DN_DV = 128
DN_CONV = 4
DN_CHUNK = 64
DN_QK_W = DN_HEADS * DN_DK
DN_V_W = DN_HEADS * DN_DV
DN_CONV_W = 2 * DN_QK_W + DN_V_W
SW_HEADS = 16
SW_KV_HEADS = 2
SW_GROUP = SW_HEADS // SW_KV_HEADS
SW_HD = 64
SW_KV_W = SW_KV_HEADS * SW_HD
WINDOW = 128
SW_BLOCK = 128
REL_BUCKETS = 32
REL_MAX_DIST = 128
N_GROUPS = 8
EXP_PER_GROUP = 8
N_EXPERTS = N_GROUPS * EXP_PER_GROUP
D_EXPERT = 256
MOE_BLOCK = 256
EPS = 1e-6

LANES = 128
SUBLANES = 8
VMEM_LIMIT = 56 * 1024 * 1024

COL_QKV = 0
COL_Z = 3072
COL_SQ = 4096
COL_GA = 5120
COL_GB = 6144
COL_SK = 7168
COL_SV = 7296
PROJ_W = 7424
PROJ_CHUNK = 512


def _cparams(sem):
    return pltpu.CompilerParams(dimension_semantics=sem, vmem_limit_bytes=VMEM_LIMIT)


def _sigmoid(x):
    return 0.5 * jnp.tanh(0.5 * x) + 0.5


def _dot(a, b):
    return jnp.dot(a.astype(BF16), b.astype(BF16), preferred_element_type=F32)


def _dot_nt(a, b):
    return lax.dot_general(a.astype(BF16), b.astype(BF16), (((1,), (1,)), ((), ())), preferred_element_type=F32)


def _dot_tn(a, b):
    return lax.dot_general(a.astype(BF16), b.astype(BF16), (((0,), (0,)), ((), ())), preferred_element_type=F32)


def _dot_exact(a, b):
    return jnp.dot(a, b, precision=lax.Precision.HIGHEST, preferred_element_type=F32)


def _lane_iota(shape):
    return lax.broadcasted_iota(I32, shape, len(shape) - 1)


def _row_iota(shape):
    return lax.broadcasted_iota(I32, shape, len(shape) - 2)


def _col(x, j):
    return jnp.sum(jnp.where(_lane_iota(x.shape) == j, x, 0.0), axis=-1, keepdims=True)


def _token_tile(*sizes):
    for t in (256, 128, 64, 32, 16, 8):
        if all(s % t == 0 for s in sizes):
            return t
    raise ValueError(f"token counts {sizes} need a common tile that is a multiple of 8")


def _inproj_kernel(xp_ref, xs_ref, nw_ref, wb_ref, ws_ref, convw_ref, conv0_ref, proj_ref, ba_ref, tail_ref, cbuf_ref,
                   *, n_prompt_tiles, tiles_per_seq):
    i = pl.program_id(0)
    tm = xp_ref.shape[0]
    hist = SUBLANES - (DN_CONV - 1)

    def project(x_ref, conv):
        x = x_ref[...]
        h = (x * lax.rsqrt(jnp.mean(x * x, axis=-1, keepdims=True) + EPS) * nw_ref[...]).astype(BF16)
        ba_ref[...] = jnp.dot(h, ws_ref[...], preferred_element_type=F32)
        chunks = [(c0, min(c0 + PROJ_CHUNK, PROJ_W)) for c0 in range(0, PROJ_W, PROJ_CHUNK)]
        to_conv = [c for c in chunks if conv and c[1] <= COL_QKV + DN_CONV_W]

        def matmul(c0, c1):
            return jnp.dot(h, wb_ref[:, c0:c1], preferred_element_type=F32)

        def conv_chunk(c0, c1):
            acc = cbuf_ref[hist:hist + tm, c0:c1] * convw_ref[0:1, c0:c1]
            for j in range(1, DN_CONV):
                acc = acc + cbuf_ref[hist + j:hist + j + tm, c0:c1] * convw_ref[j:j + 1, c0:c1]
            proj_ref[:, c0:c1] = acc * _sigmoid(acc)

        for c0, c1 in to_conv:
            cbuf_ref[SUBLANES:SUBLANES + tm, c0:c1] = matmul(c0, c1)
        pending = list(to_conv)
        for c0, c1 in chunks:
            if (c0, c1) in to_conv:
                continue
            proj_ref[:, c0:c1] = matmul(c0, c1)
            if pending:
                conv_chunk(*pending.pop(0))
        for c in pending:
            conv_chunk(*c)

    @pl.when(i < n_prompt_tiles)
    def _():
        @pl.when(i % tiles_per_seq == 0)
        def _():
            cbuf_ref[hist:SUBLANES, :] = conv0_ref[0]

        project(xp_ref, True)
        tail_ref[0] = cbuf_ref[tm:tm + SUBLANES, :]
        cbuf_ref[hist:SUBLANES, :] = cbuf_ref[SUBLANES + tm - (DN_CONV - 1):SUBLANES + tm, :]

    @pl.when(i >= n_prompt_tiles)
    def _():
        project(xs_ref, False)


def _inproj(xp, xs, norm_w, w_big, w_small, conv_w, conv0, seq):
    tp, ts = xp.shape[0], xs.shape[0]
    tm = _token_tile(tp, ts, seq)
    assert COL_QKV == 0 and DN_CONV_W % PROJ_CHUNK == 0
    npt, nst = tp // tm, ts // tm
    tps = seq // tm
    const = lambda i: (0, 0)
    seq_of = lambda i: (jnp.minimum(i, npt - 1) // tps, 0, 0)
    return pl.pallas_call(
        functools.partial(_inproj_kernel, n_prompt_tiles=npt, tiles_per_seq=tps),
        grid=(npt + nst,),
        in_specs=[
            pl.BlockSpec((tm, D_MODEL), lambda i: (jnp.minimum(i, npt - 1), 0)),
            pl.BlockSpec((tm, D_MODEL), lambda i: (jnp.maximum(i - npt, 0), 0)),
            pl.BlockSpec((1, D_MODEL), const),
            pl.BlockSpec((D_MODEL, PROJ_W), const, pipeline_mode=pl.Buffered(1)),
            pl.BlockSpec((D_MODEL, LANES), const),
            pl.BlockSpec((DN_CONV, DN_CONV_W), const),
            pl.BlockSpec((1, DN_CONV - 1, DN_CONV_W), seq_of),
        ],
        out_specs=[
            pl.BlockSpec((tm, PROJ_W), lambda i: (i, 0)),
            pl.BlockSpec((tm, LANES), lambda i: (i, 0)),
            pl.BlockSpec((1, SUBLANES, DN_CONV_W), seq_of),
        ],
        out_shape=[
            jax.ShapeDtypeStruct((tp + ts, PROJ_W), F32),
            jax.ShapeDtypeStruct((tp + ts, LANES), F32),
            jax.ShapeDtypeStruct((tp // seq, SUBLANES, DN_CONV_W), F32),
        ],
        scratch_shapes=[pltpu.VMEM((SUBLANES + tm, DN_CONV_W), F32)],
        compiler_params=_cparams(("arbitrary",)),
        name="inproj",
    )(xp, xs, norm_w, w_big, w_small, conv_w, conv0)


def _dn_core(groups, alog, dtb, nw, read_state, write_state, n_seg, seg_valid):
    rows = groups[0][0].shape[0]
    sr = rows // n_seg
    assert sr * n_seg == rows and sr & (sr - 1) == 0 and rows <= LANES
    seg_shift = sr.bit_length() - 1
    ri = _row_iota((rows, rows))
    ci = _lane_iota((rows, rows))
    incl = ri >= ci
    strict = ri > ci
    if n_seg > 1:
        same = (ri >> seg_shift) == (ci >> seg_shift)
        incl = incl & same
        strict = strict & same
    l_incl = incl.astype(F32)
    eye = (ri == ci).astype(F32)
    levels = max(1, math.ceil(math.log2(seg_valid)))

    beta_all, gsum_all, gtot_all, gsum_t = [], [], [], []
    for _, _, ba in groups:
        b_all = _sigmoid(ba)
        sp = ba + dtb
        softplus = jnp.maximum(sp, 0.0) + jnp.log1p(jnp.exp(-jnp.abs(sp)))
        g_all = -jnp.exp(alog) * softplus
        if seg_valid < sr:
            live = (_row_iota((rows, LANES)) & (sr - 1)) < seg_valid
            b_all = jnp.where(live, b_all, 0.0)
            g_all = jnp.where(live, g_all, 0.0)
        gs = _dot_exact(l_incl, g_all)
        beta_all.append(b_all)
        gsum_all.append(gs)
        gtot_all.append(_dot_exact(same.astype(F32), g_all) if n_seg > 1 else gs[rows - 1:rows, :])
        padded = gs if rows == LANES else jnp.concatenate([gs, jnp.zeros((LANES - rows, LANES), F32)], axis=0)
        gsum_t.append(padded.T)

    probs = [(g, h) for g in range(len(groups)) for h in range(DN_HEADS)]
    segs = range(n_seg)
    q, k, v, kb, beta, gsum, gtot = {}, {}, {}, {}, {}, {}, {}
    for p in probs:
        g, h = p
        qkv = groups[g][0]
        qh = qkv[:, h * DN_DK:(h + 1) * DN_DK]
        kh = qkv[:, DN_QK_W + h * DN_DK:DN_QK_W + (h + 1) * DN_DK]
        v[p] = qkv[:, 2 * DN_QK_W + h * DN_DV:2 * DN_QK_W + (h + 1) * DN_DV]
        q[p] = qh * lax.rsqrt(jnp.sum(qh * qh, axis=-1, keepdims=True) + 1e-6) * (DN_DK ** -0.5)
        k[p] = kh * lax.rsqrt(jnp.sum(kh * kh, axis=-1, keepdims=True) + 1e-6)
        beta[p] = _col(beta_all[g], h)
        gsum[p] = _col(gsum_all[g], DN_HEADS + h)
        gtot[p] = _col(gtot_all[g], DN_HEADS + h)
        kb[p] = k[p] * beta[p]
    kq = {p: _dot_nt(jnp.concatenate([kb[p], q[p]], axis=0), k[p]) for p in probs}
    gamma = {(g, h): jnp.exp(jnp.where(incl, gsum[(g, h)] - gsum_t[g][DN_HEADS + h:DN_HEADS + h + 1, :rows], -jnp.inf))
             for g, h in probs}
    attn = {p: kq[p][rows:] * gamma[p] for p in probs}
    pw = {p: -jnp.where(strict, kq[p][:rows] * gamma[p], 0.0) for p in probs}
    t = {p: eye + pw[p] for p in probs}
    for _ in range(1, levels):
        pw = {p: _dot(pw[p], pw[p]) for p in probs}
        t = {p: t[p] + _dot(t[p], pw[p]) for p in probs}
    eg = {p: jnp.exp(gsum[p]) for p in probs}
    uw = {p: _dot(t[p], jnp.concatenate([v[p] * beta[p], kb[p] * eg[p]], axis=1)) for p in probs}
    qg = {p: q[p] * eg[p] for p in probs}
    state = {(p, s): read_state(p[0], s, p[1]) for p in probs for s in segs}
    wq = {(p, s): _dot(jnp.concatenate([uw[p][s * sr:(s + 1) * sr, DN_DV:], qg[p][s * sr:(s + 1) * sr]], axis=0),
                       state[(p, s)]) for p in probs for s in segs}
    join = lambda pieces: pieces[0] if len(pieces) == 1 else jnp.concatenate(pieces, axis=0)
    v_new = {p: uw[p][:, :DN_DV] - join([wq[(p, s)][:sr] for s in segs]) for p in probs}
    o = {p: join([wq[(p, s)][sr:] for s in segs]) + _dot(attn[p], v_new[p]) for p in probs}
    kd = {p: k[p] * jnp.exp(gtot[p] - gsum[p]) for p in probs}
    for p in probs:
        for s in segs:
            r0 = s * sr if n_seg > 1 else 0
            decay = jnp.exp(gtot[p][r0:r0 + 1, :])
            write_state(p[0], s, p[1],
                        state[(p, s)] * decay + _dot_tn(kd[p][s * sr:(s + 1) * sr], v_new[p][s * sr:(s + 1) * sr]))
    outs = []
    for g, (_, z, _) in enumerate(groups):
        heads = []
        for h in range(DN_HEADS):
            oh = o[(g, h)]
            zz = z[:, h * DN_DV:(h + 1) * DN_DV]
            on = oh * lax.rsqrt(jnp.mean(oh * oh, axis=-1, keepdims=True) + EPS) * nw
            heads.append(on * (zz * _sigmoid(zz)))
        outs.append(jnp.concatenate(heads, axis=1))
    return outs


def _dn_prompt_kernel(*refs, chunk, n_batch):
    nb = n_batch
    qkv_refs, z_refs, ba_refs = refs[0:nb], refs[nb:2 * nb], refs[2 * nb:3 * nb]
    alog_ref, dtb_ref, nw_ref, s0_ref, o_ref, sout_ref = refs[3 * nb:]

    @pl.when(pl.program_id(0) == 0)
    def _():
        sout_ref[...] = s0_ref[...]

    groups = [(qkv_refs[b][...], z_refs[b][...], ba_refs[b][...]) for b in range(nb)]

    def read_state(g, s, h):
        return sout_ref[g, h]

    def write_state(g, s, h, val):
        sout_ref[g, h] = val

    outs = _dn_core(groups, alog_ref[...], dtb_ref[...], nw_ref[...], read_state, write_state, 1, chunk)
    for b in range(nb):
        o_ref[b] = outs[b]


def _dn_prompt(proj, ba, alog_row, dtb_row, dn_nw, s0, n_batch, seq):
    chunk = min(DN_CHUNK, seq)
    assert seq % chunk == 0 and chunk % SUBLANES == 0
    nc = seq // chunk
    const2 = lambda c: (0, 0)
    rows = lambda b, col: (lambda c: (b * nc + c, col))
    batches = range(n_batch)
    o, s_out = pl.pallas_call(
        functools.partial(_dn_prompt_kernel, chunk=chunk, n_batch=n_batch),
        grid=(nc,),
        in_specs=(
            [pl.BlockSpec((chunk, DN_CONV_W), rows(b, COL_QKV // DN_CONV_W)) for b in batches]
            + [pl.BlockSpec((chunk, DN_V_W), rows(b, COL_Z // DN_V_W)) for b in batches]
            + [pl.BlockSpec((chunk, LANES), rows(b, 0)) for b in batches]
            + [
                pl.BlockSpec((1, LANES), const2),
                pl.BlockSpec((1, LANES), const2),
                pl.BlockSpec((1, DN_DV), const2),
                pl.BlockSpec((n_batch, DN_HEADS, DN_DK, DN_DV), lambda c: (0, 0, 0, 0)),
            ]
        ),
        out_specs=[
            pl.BlockSpec((n_batch, chunk, DN_V_W), lambda c: (0, c, 0)),
            pl.BlockSpec((n_batch, DN_HEADS, DN_DK, DN_DV), lambda c: (0, 0, 0, 0)),
        ],
        out_shape=[
            jax.ShapeDtypeStruct((n_batch, seq, DN_V_W), F32),
            jax.ShapeDtypeStruct((n_batch, DN_HEADS, DN_DK, DN_DV), F32),
        ],
        compiler_params=_cparams(("arbitrary",)),
        name="dn_prompt",
    )(*([proj] * n_batch), *([proj] * n_batch), *([ba] * n_batch), alog_row, dtb_row, dn_nw, s0)
    return o.reshape(n_batch * seq, DN_V_W), s_out


def _dn_sample_kernel(qkv_ref, z_ref, ba_ref, convw_ref, alog_ref, dtb_ref, nw_ref, conv0_ref, s0_ref,
                      o_ref, convout_ref, sout_ref, cbuf_ref, *, seq, n_bb):
    hist = SUBLANES - (DN_CONV - 1)
    per = SUBLANES // seq
    n_tiles = n_bb // per

    def spread(ref):
        pieces = []
        for j in range(n_tiles):
            x8 = ref[j * SUBLANES:(j + 1) * SUBLANES, :]
            for r in range(per):
                pieces.append(x8 if r == 0 else pltpu.roll(x8, SUBLANES - r * seq, axis=0))
        return pieces

    for bb, piece in enumerate(spread(qkv_ref)):
        cbuf_ref[bb, SUBLANES:2 * SUBLANES, :] = piece
    cbuf_ref[:, hist:SUBLANES, :] = conv0_ref[...]
    w = convw_ref[...]
    acc = cbuf_ref[:, hist:hist + SUBLANES, :] * w[0:1, :]
    for i in range(1, DN_CONV):
        acc = acc + cbuf_ref[:, hist + i:hist + i + SUBLANES, :] * w[i:i + 1, :]
    live = _row_iota(acc.shape) < seq
    qkv = jnp.where(live, acc * _sigmoid(acc), 0.0).reshape(n_bb * SUBLANES, DN_CONV_W)
    convout_ref[...] = cbuf_ref[:, SUBLANES + seq - (DN_CONV - 1):SUBLANES + seq, :]

    def read_state(g, s, h):
        return s0_ref[s, h]

    def write_state(g, s, h, val):
        sout_ref[s, h] = val

    group = (qkv, jnp.concatenate(spread(z_ref), axis=0), jnp.concatenate(spread(ba_ref), axis=0))
    o = _dn_core([group], alog_ref[...], dtb_ref[...], nw_ref[...], read_state, write_state, n_bb, seq)[0]
    rows = _row_iota((SUBLANES, DN_V_W))
    for j in range(n_tiles):
        tile = None
        for r in range(per):
            bb = j * per + r
            piece = o[bb * SUBLANES:(bb + 1) * SUBLANES]
            piece = piece if r == 0 else pltpu.roll(piece, r * seq, axis=0)
            tile = piece if tile is None else jnp.where(rows >= r * seq, piece, tile)
        o_ref[j * SUBLANES:(j + 1) * SUBLANES, :] = tile


def _dn_sample(proj, ba, conv_w, alog_row, dtb_row, dn_nw, conv0, s0, row0, n_batch, seq):
    assert SUBLANES % seq == 0 and seq >= DN_CONV - 1
    n_bb = SUBLANES
    rows_in = n_bb * seq
    assert n_batch % n_bb == 0 and row0 % rows_in == 0
    rb0 = row0 // rows_in
    const1 = lambda i: (0, 0)
    return pl.pallas_call(
        functools.partial(_dn_sample_kernel, seq=seq, n_bb=n_bb),
        grid=(n_batch // n_bb,),
        in_specs=[
            pl.BlockSpec((rows_in, DN_CONV_W), lambda i: (rb0 + i, COL_QKV // DN_CONV_W)),
            pl.BlockSpec((rows_in, DN_V_W), lambda i: (rb0 + i, COL_Z // DN_V_W)),
            pl.BlockSpec((rows_in, LANES), lambda i: (rb0 + i, 0)),
            pl.BlockSpec((DN_CONV, DN_CONV_W), const1),
            pl.BlockSpec((1, LANES), const1),
            pl.BlockSpec((1, LANES), const1),
            pl.BlockSpec((1, DN_DV), const1),
            pl.BlockSpec((n_bb, DN_CONV - 1, DN_CONV_W), lambda i: (i, 0, 0)),
            pl.BlockSpec((n_bb, DN_HEADS, DN_DK, DN_DV), lambda i: (i, 0, 0, 0)),
        ],
        out_specs=[
            pl.BlockSpec((rows_in, DN_V_W), lambda i: (i, 0)),
            pl.BlockSpec((n_bb, DN_CONV - 1, DN_CONV_W), lambda i: (i, 0, 0)),
            pl.BlockSpec((n_bb, DN_HEADS, DN_DK, DN_DV), lambda i: (i, 0, 0, 0)),
        ],
        out_shape=[
            jax.ShapeDtypeStruct((n_batch * seq, DN_V_W), F32),
            jax.ShapeDtypeStruct((n_batch, DN_CONV - 1, DN_CONV_W), F32),
            jax.ShapeDtypeStruct((n_batch, DN_HEADS, DN_DK, DN_DV), F32),
        ],
        scratch_shapes=[pltpu.VMEM((n_bb, 2 * SUBLANES, DN_CONV_W), F32)],
        compiler_params=_cparams(("arbitrary",)),
        name="dn_sample",
    )(proj, proj, ba, conv_w, alog_row, dtb_row, dn_nw, conv0, s0)


def _rel_bucket(dist):
    n = jnp.maximum(dist, 0)
    max_exact = REL_BUCKETS // 2
    large = max_exact + (jnp.log(jnp.maximum(n, 1).astype(F32) / max_exact)
                         / math.log(REL_MAX_DIST / max_exact) * (REL_BUCKETS - max_exact)).astype(I32)
    return jnp.where(n < max_exact, n, jnp.minimum(large, REL_BUCKETS - 1))


def _relbias_kernel(tab_ref, bucket_ref, o_ref):
    h = pl.program_id(0)
    bk = bucket_ref[...]
    acc = jnp.zeros(bk.shape, F32)
    for b in range(REL_BUCKETS):
        acc = jnp.where(bk == b, tab_ref[b * SW_HEADS + h], acc)
    o_ref[0] = acc


def _relbias(rel_table, bucket):
    nq, ns = bucket.shape
    return pl.pallas_call(
        _relbias_kernel,
        grid=(SW_HEADS,),
        in_specs=[
            pl.BlockSpec(memory_space=pltpu.SMEM),
            pl.BlockSpec((nq, ns), lambda h: (0, 0)),
        ],
        out_specs=pl.BlockSpec((1, nq, ns), lambda h: (h, 0, 0)),
        out_shape=jax.ShapeDtypeStruct((SW_HEADS, nq, ns), F32),
        compiler_params=_cparams(("arbitrary",)),
        name="relbias",
    )(rel_table.reshape(-1), bucket)


def _dup_halves(x):
    lo = _lane_iota(x.shape) < SW_HD
    xr = pltpu.roll(x, SW_HD, axis=1)
    return jnp.where(lo, x, xr).astype(BF16), jnp.where(lo, xr, x).astype(BF16)


def _sink_softmax_pv(s, valid, sink, vv):
    s = jnp.where(valid, s, -jnp.inf)
    m = jnp.maximum(jnp.max(s, axis=-1, keepdims=True), sink)
    p = jnp.exp(s - m)
    p = p / (jnp.sum(p, axis=-1, keepdims=True) + jnp.exp(sink - m))
    return _dot(p, vv)


def _swa_prompt_kernel(sinks_ref, q_ref, kc_ref, kp_ref, vc_ref, vp_ref, bias_ref, o_ref, klast_ref, vlast_ref):
    i = pl.program_id(1)
    klast_ref[0] = kc_ref[...]
    vlast_ref[0] = vc_ref[...]
    kk = _dup_halves(jnp.concatenate([kp_ref[...], kc_ref[...]], axis=0))
    vv = _dup_halves(jnp.concatenate([vp_ref[...], vc_ref[...]], axis=0))
    shape = (SW_BLOCK, 2 * SW_BLOCK)
    qi = _row_iota(shape)
    kj = _lane_iota(shape)
    dist = qi - kj + SW_BLOCK
    first_key = jnp.where(i > 0, 0, SW_BLOCK)
    valid = (dist >= 0) & (dist < WINDOW) & (kj >= first_key)
    lo = _lane_iota((SW_BLOCK, LANES)) < SW_HD
    for pair in range(SW_HEADS // 2):
        qp = q_ref[:, pair * LANES:(pair + 1) * LANES]
        outs = []
        for half in range(2):
            hq = 2 * pair + half
            kv = hq // SW_GROUP
            qm = jnp.where(lo if half == 0 else ~lo, qp, 0.0)
            s = _dot_nt(qm, kk[kv]) * (SW_HD ** -0.5) + bias_ref[hq]
            outs.append(_sink_softmax_pv(s, valid, sinks_ref[hq], vv[kv]))
        o_ref[:, pair * LANES:(pair + 1) * LANES] = jnp.where(lo, outs[0], outs[1])


def _swa_prompt(proj, sinks, bias, n_batch, seq):
    assert seq % SW_BLOCK == 0 and WINDOW == SW_BLOCK
    nb = seq // SW_BLOCK
    cur = lambda col: (lambda b, i: (b * nb + i, col))
    prev = lambda col: (lambda b, i: (b * nb + jnp.maximum(i - 1, 0), col))
    return pl.pallas_call(
        _swa_prompt_kernel,
        grid=(n_batch, nb),
        in_specs=[
            pl.BlockSpec(memory_space=pltpu.SMEM),
            pl.BlockSpec((SW_BLOCK, SW_HEADS * SW_HD), cur(COL_SQ // (SW_HEADS * SW_HD))),
            pl.BlockSpec((SW_BLOCK, SW_KV_W), cur(COL_SK // SW_KV_W)),
            pl.BlockSpec((SW_BLOCK, SW_KV_W), prev(COL_SK // SW_KV_W)),
            pl.BlockSpec((SW_BLOCK, SW_KV_W), cur(COL_SV // SW_KV_W)),
            pl.BlockSpec((SW_BLOCK, SW_KV_W), prev(COL_SV // SW_KV_W)),
            pl.BlockSpec((SW_HEADS, SW_BLOCK, 2 * SW_BLOCK), lambda b, i: (0, 0, 0)),
        ],
        out_specs=[
            pl.BlockSpec((SW_BLOCK, SW_HEADS * SW_HD), lambda b, i: (b * nb + i, 0)),
            pl.BlockSpec((1, SW_BLOCK, SW_KV_W), lambda b, i: (b, 0, 0)),
            pl.BlockSpec((1, SW_BLOCK, SW_KV_W), lambda b, i: (b, 0, 0)),
        ],
        out_shape=[
            jax.ShapeDtypeStruct((n_batch * seq, SW_HEADS * SW_HD), F32),
            jax.ShapeDtypeStruct((n_batch, SW_BLOCK, SW_KV_W), F32),
            jax.ShapeDtypeStruct((n_batch, SW_BLOCK, SW_KV_W), F32),
        ],
        compiler_params=_cparams(("arbitrary", "arbitrary")),
        name="swa_prompt",
    )(sinks, proj, proj, proj, proj, proj, bias)


def _swa_sample_kernel(q_ref, kn_ref, vn_ref, kc_ref, vc_ref, bias_ref, sink_ref,
                       o_ref, ko_ref, vo_ref, kall_ref, vall_ref, *, seq, n_bb, n_cache):
    n_keys = kall_ref.shape[0]
    zeros_tail = jnp.zeros((n_keys - n_cache - SUBLANES, LANES), F32)
    shape = (SW_GROUP * SUBLANES, n_keys)
    t = _row_iota(shape) % SUBLANES
    s_idx = _lane_iota(shape)
    dist = n_cache + t - s_idx
    valid = (dist >= 0) & (dist < WINDOW)
    lo = _lane_iota((SUBLANES, LANES)) < SW_HD
    out = None
    for bb in range(n_bb):
        shift = (SUBLANES - bb * seq) % SUBLANES

        def top(x, shift=shift):
            return x if shift == 0 else pltpu.roll(x, shift, axis=0)

        kall_ref[0:n_cache, :] = kc_ref[bb]
        kall_ref[n_cache:n_cache + SUBLANES, :] = top(kn_ref[...])
        kall_ref[n_cache + SUBLANES:, :] = zeros_tail
        vall_ref[0:n_cache, :] = vc_ref[bb]
        vall_ref[n_cache:n_cache + SUBLANES, :] = top(vn_ref[...])
        vall_ref[n_cache + SUBLANES:, :] = zeros_tail
        ko_ref[bb] = kall_ref[seq:seq + n_cache, :]
        vo_ref[bb] = vall_ref[seq:seq + n_cache, :]
        kk = _dup_halves(kall_ref[...])
        vv = _dup_halves(vall_ref[...])
        q8 = top(q_ref[...])
        pairs = []
        for kv in range(SW_KV_HEADS):
            pieces = []
            for g in range(SW_GROUP):
                hq = kv * SW_GROUP + g
                qp = q8[:, (hq // 2) * LANES:(hq // 2 + 1) * LANES]
                pieces.append(jnp.where(lo if hq % 2 == 0 else ~lo, qp, 0.0))
            qs = jnp.concatenate(pieces, axis=0)
            s = _dot_nt(qs, kk[kv]) * (SW_HD ** -0.5) + bias_ref[kv]
            res = _sink_softmax_pv(s, valid, sink_ref[kv], vv[kv])
            for g in range(0, SW_GROUP, 2):
                pairs.append(jnp.where(lo, res[g * SUBLANES:(g + 1) * SUBLANES],
                                       res[(g + 1) * SUBLANES:(g + 2) * SUBLANES]))
        o = jnp.concatenate(pairs, axis=1)
        back = (bb * seq) % SUBLANES
        o = o if back == 0 else pltpu.roll(o, back, axis=0)
        rows = _row_iota(o.shape)
        sel = (rows >= bb * seq) & (rows < (bb + 1) * seq)
        out = jnp.where(sel, o, 0.0 if out is None else out)
    o_ref[...] = out


def _swa_sample(proj, k_cache, v_cache, bias, sink_rows, row0, n_batch, seq):
    assert SUBLANES % seq == 0
    n_bb = SUBLANES // seq
    n_cache = k_cache.shape[1]
    assert n_batch % n_bb == 0 and row0 % SUBLANES == 0 and n_cache % SUBLANES == 0
    n_keys = bias.shape[-1]
    rb0 = row0 // SUBLANES
    blk = lambda col: (lambda i: (rb0 + i, col))
    return pl.pallas_call(
        functools.partial(_swa_sample_kernel, seq=seq, n_bb=n_bb, n_cache=n_cache),
        grid=(n_batch // n_bb,),
        in_specs=[
            pl.BlockSpec((SUBLANES, SW_HEADS * SW_HD), blk(COL_SQ // (SW_HEADS * SW_HD))),
            pl.BlockSpec((SUBLANES, SW_KV_W), blk(COL_SK // SW_KV_W)),
            pl.BlockSpec((SUBLANES, SW_KV_W), blk(COL_SV // SW_KV_W)),
            pl.BlockSpec((n_bb, n_cache, SW_KV_W), lambda i: (i, 0, 0)),
            pl.BlockSpec((n_bb, n_cache, SW_KV_W), lambda i: (i, 0, 0)),
            pl.BlockSpec((SW_KV_HEADS, SW_GROUP * SUBLANES, n_keys), lambda i: (0, 0, 0)),
            pl.BlockSpec((SW_KV_HEADS, SW_GROUP * SUBLANES, n_keys), lambda i: (0, 0, 0)),
        ],
        out_specs=[
            pl.BlockSpec((SUBLANES, SW_HEADS * SW_HD), lambda i: (i, 0)),
            pl.BlockSpec((n_bb, n_cache, SW_KV_W), lambda i: (i, 0, 0)),
            pl.BlockSpec((n_bb, n_cache, SW_KV_W), lambda i: (i, 0, 0)),
        ],
        out_shape=[
            jax.ShapeDtypeStruct((n_batch * seq, SW_HEADS * SW_HD), F32),
            jax.ShapeDtypeStruct(k_cache.shape, F32),
            jax.ShapeDtypeStruct(v_cache.shape, F32),
        ],
        scratch_shapes=[pltpu.VMEM((n_keys, SW_KV_W), F32), pltpu.VMEM((n_keys, SW_KV_W), F32)],
        compiler_params=_cparams(("arbitrary",)),
        name="swa_sample",
    )(proj, proj, proj, k_cache, v_cache, bias, sink_rows)


def _mix_kernel(xp_ref, xs_ref, oap_ref, oas_ref, obp_ref, obs_ref, ga_ref, gb_ref, wo_ref, nw_ref, wr_ref, br_ref,
                x1_ref, h2_ref, route_ref, *, n_prompt_tiles):
    i = pl.program_id(0)

    def run(x_ref, oa_ref, ob_ref):
        mixed = _sigmoid(ga_ref[...]) * oa_ref[...] + _sigmoid(gb_ref[...]) * ob_ref[...]
        x1 = x_ref[...] + _dot(mixed, wo_ref[...])
        x1_ref[...] = x1
        h2 = x1 * lax.rsqrt(jnp.mean(x1 * x1, axis=-1, keepdims=True) + EPS) * nw_ref[...]
        _to_tiles(h2_ref, h2)
        logits = _dot_exact(h2, wr_ref[...]) + br_ref[...]
        lane = _lane_iota(logits.shape)
        lanef = lane.astype(F32)
        big = float(2 * LANES)
        is_g = lane < N_GROUPS
        gl = jnp.where(is_g, logits, -jnp.inf)
        gmax = jnp.max(gl, axis=-1, keepdims=True)
        gval = 1.0 / jnp.sum(jnp.where(is_g, jnp.exp(gl - gmax), 0.0), axis=-1, keepdims=True)
        grp = jnp.min(jnp.where(gl == gmax, lanef, big), axis=-1, keepdims=True)
        e_grp = ((lane - N_GROUPS) >> 3).astype(F32)
        is_e = (lane >= N_GROUPS) & (lane < N_GROUPS + N_EXPERTS) & (e_grp == grp)
        el = jnp.where(is_e, logits, -jnp.inf)
        v1 = jnp.max(el, axis=-1, keepdims=True)
        i1 = jnp.min(jnp.where(el == v1, lanef, big), axis=-1, keepdims=True)
        el2 = jnp.where(lanef == i1, -jnp.inf, el)
        v2 = jnp.max(el2, axis=-1, keepdims=True)
        i2 = jnp.min(jnp.where(el2 == v2, lanef, big), axis=-1, keepdims=True)
        e2 = jnp.exp(v2 - v1)
        w1 = gval / (1.0 + e2)
        w2 = gval * e2 / (1.0 + e2)
        route_ref[...] = jnp.where(lane == 0, i1 - N_GROUPS,
                                   jnp.where(lane == 1, i2 - N_GROUPS,
                                             jnp.where(lane == 2, w1, jnp.where(lane == 3, w2, 0.0))))

    @pl.when(i < n_prompt_tiles)
    def _():
        run(xp_ref, oap_ref, obp_ref)

    @pl.when(i >= n_prompt_tiles)
    def _():
        run(xs_ref, oas_ref, obs_ref)


def _mix(xp, xs, oa_p, oa_s, ob_p, ob_s, proj, w_out, norm_w, w_router, b_router):
    tp, ts = xp.shape[0], xs.shape[0]
    tm = _token_tile(tp, ts)
    npt, nst = tp // tm, ts // tm
    const = lambda i: (0, 0)
    row = lambda i: (i, 0)
    return pl.pallas_call(
        functools.partial(_mix_kernel, n_prompt_tiles=npt),
        grid=(npt + nst,),
        in_specs=[
            pl.BlockSpec((tm, D_MODEL), lambda i: (jnp.minimum(i, npt - 1), 0)),
            pl.BlockSpec((tm, D_MODEL), lambda i: (jnp.maximum(i - npt, 0), 0)),
            pl.BlockSpec((tm, D_MODEL), lambda i: (jnp.minimum(i, npt - 1), 0)),
            pl.BlockSpec((tm, D_MODEL), lambda i: (jnp.maximum(i - npt, 0), 0)),
            pl.BlockSpec((tm, D_MODEL), lambda i: (jnp.minimum(i, npt - 1), 0)),
            pl.BlockSpec((tm, D_MODEL), lambda i: (jnp.maximum(i - npt, 0), 0)),
            pl.BlockSpec((tm, D_MODEL), lambda i: (i, COL_GA // D_MODEL)),
            pl.BlockSpec((tm, D_MODEL), lambda i: (i, COL_GB // D_MODEL)),
            pl.BlockSpec((D_MODEL, D_MODEL), const),
            pl.BlockSpec((1, D_MODEL), const),
            pl.BlockSpec((D_MODEL, LANES), const),
            pl.BlockSpec((1, LANES), const),
        ],
        out_specs=[
            pl.BlockSpec((tm, D_MODEL), row),
            pl.BlockSpec((tm,) + TOK_TILE, lambda i: (i, 0, 0)),
            pl.BlockSpec((tm, LANES), row),
        ],
        out_shape=[
            jax.ShapeDtypeStruct((tp + ts, D_MODEL), F32),
            jax.ShapeDtypeStruct((tp + ts,) + TOK_TILE, F32),
            jax.ShapeDtypeStruct((tp + ts, LANES), F32),
        ],
        compiler_params=_cparams(("arbitrary",)),
        name="mix_router",
    )(xp, xs, oa_p, oa_s, ob_p, ob_s, proj, proj, w_out, norm_w, w_router, b_router)


def _rank_kernel(route_ref, dest_ref, meta_ref, rank_ref, cnt_ref, *, tile, blk):
    phase = pl.program_id(0)
    i = pl.program_id(1)
    shape = (tile, LANES)
    lane = _lane_iota(shape)
    lanef = lane.astype(F32)
    r = route_ref[...]
    oh0 = lanef == _col(r, 0)
    oh1 = lanef == _col(r, 1)
    rows = pl.ds(pl.multiple_of(i * tile, tile), tile)

    @pl.when(phase == 0)
    def _():
        @pl.when(i == 0)
        def _():
            cnt_ref[...] = jnp.zeros(cnt_ref.shape, F32)

        oh = jnp.where(oh0 | oh1, 1.0, 0.0)
        tri = jnp.where(_row_iota((tile, tile)) > _lane_iota((tile, tile)), 1.0, 0.0)
        before = _dot(tri, oh) + cnt_ref[0:1, :]
        rank0 = jnp.sum(jnp.where(oh0, before, 0.0), axis=-1, keepdims=True)
        rank1 = jnp.sum(jnp.where(oh1, before, 0.0), axis=-1, keepdims=True)
        rank_ref[rows, :] = jnp.where(lane == 0, rank0, jnp.where(lane == 1, rank1, 0.0))
        cnt_ref[0:1, :] = cnt_ref[0:1, :] + jnp.sum(oh, axis=0, keepdims=True)

    @pl.when(phase == 1)
    def _():
        cnt = cnt_ref[0:1, :]
        padded = jnp.floor((cnt + (blk - 1)) / blk) * blk
        before_lane = jnp.where(_row_iota((LANES, LANES)) < _lane_iota((LANES, LANES)), 1.0, 0.0)
        start = _dot_exact(jnp.broadcast_to(padded, (SUBLANES, LANES)), before_lane)[0:1, :]
        rk = rank_ref[rows, :]
        d0 = jnp.sum(jnp.where(oh0, start, 0.0), axis=-1, keepdims=True) + _col(rk, 0)
        d1 = jnp.sum(jnp.where(oh1, start, 0.0), axis=-1, keepdims=True) + _col(rk, 1)
        dest_ref[...] = jnp.where(lane == 0, d0, jnp.where(lane == 1, d1, 0.0)).astype(I32)

        @pl.when(i == 0)
        def _():
            end = start + padded
            mshape = meta_ref.shape
            blk_start = (_row_iota(mshape) * blk).astype(F32)
            hit = (_lane_iota(mshape) < N_EXPERTS) & (end <= blk_start)
            be = jnp.minimum(jnp.sum(jnp.where(hit, 1.0, 0.0), axis=-1, keepdims=True), N_EXPERTS - 1.0)
            n_used = _col(end, N_EXPERTS - 1) / blk
            ml = _lane_iota(mshape)
            meta_ref[...] = jnp.where(ml == 0, be, jnp.where(ml == 1, n_used, 0.0)).astype(I32)


def _rank(route, tile, blk, n_blocks):
    t = route.shape[0]
    nt = t // tile
    nbp = -(-n_blocks // SUBLANES) * SUBLANES
    return pl.pallas_call(
        functools.partial(_rank_kernel, tile=tile, blk=blk),
        grid=(2, nt),
        in_specs=[pl.BlockSpec((tile, LANES), lambda p, i: (i, 0))],
        out_specs=[
            pl.BlockSpec((tile, LANES), lambda p, i: (i * p, 0)),
            pl.BlockSpec((nbp, LANES), lambda p, i: (0, 0)),
        ],
        out_shape=[
            jax.ShapeDtypeStruct((t, LANES), I32),
            jax.ShapeDtypeStruct((nbp, LANES), I32),
        ],
        scratch_shapes=[pltpu.VMEM((t, LANES), F32), pltpu.VMEM((SUBLANES, LANES), F32)],
        compiler_params=_cparams(("arbitrary", "arbitrary")),
        name="moe_rank",
    )(route)


TOK_TILE = (D_MODEL // LANES, LANES)


def _to_tiles(ref, x):
    for j in range(TOK_TILE[0]):
        ref[:, j, :] = x[:, j * LANES:(j + 1) * LANES]


def _from_tiles(ref):
    return jnp.concatenate([ref[:, j, :] for j in range(TOK_TILE[0])], axis=1)


def _row_copy(src, src_row, dst, dst_row, sem):
    return pltpu.make_async_copy(src.at[src_row], dst.at[dst_row], sem)


def _dispatch_kernel(dest_ref, h2_ref, xin_ref, xs_ref, sem, *, tile):
    del xin_ref
    base = pl.program_id(0) * tile

    def issue(t, carry):
        for k in range(2):
            _row_copy(h2_ref, t, xs_ref, dest_ref[2 * (base + t) + k], sem).start()
        return carry

    def drain(t, carry):
        for k in range(2):
            _row_copy(h2_ref, 0, xs_ref, 0, sem).wait()
        return carry

    lax.fori_loop(0, tile, issue, 0)
    lax.fori_loop(0, tile, drain, 0)


def _dispatch(dest_flat, h2, xs_init, tile):
    t = h2.shape[0]
    return pl.pallas_call(
        functools.partial(_dispatch_kernel, tile=tile),
        grid_spec=pltpu.PrefetchScalarGridSpec(
            num_scalar_prefetch=1,
            grid=(t // tile,),
            in_specs=[pl.BlockSpec((tile,) + TOK_TILE, lambda i, d: (i, 0, 0)), pl.BlockSpec(memory_space=pl.ANY)],
            out_specs=pl.BlockSpec(memory_space=pl.ANY),
            scratch_shapes=[pltpu.SemaphoreType.DMA(())],
        ),
        out_shape=jax.ShapeDtypeStruct(xs_init.shape, F32),
        input_output_aliases={2: 0},
        compiler_params=_cparams(("arbitrary",)),
        name="moe_dispatch",
    )(dest_flat, h2, xs_init)


def _last_used(i, nu_ref):
    return jnp.minimum(i, jnp.maximum(nu_ref[0] - 1, 0))


def _expert_kernel(be_ref, nu_ref, x_ref, wg_ref, wu_ref, wd_ref, y_ref, wgu_ref, wdn_ref):
    i = pl.program_id(0)
    used = i < nu_ref[0]
    blk_i = _last_used(i, nu_ref)
    fresh = (i == 0) | (be_ref[blk_i] != be_ref[jnp.maximum(blk_i - 1, 0)])

    @pl.when(used & fresh)
    def _():
        wgu_ref[:, :D_EXPERT] = wg_ref[0].astype(BF16)
        wgu_ref[:, D_EXPERT:] = wu_ref[0].astype(BF16)
        wdn_ref[...] = wd_ref[0].astype(BF16)

    @pl.when(used)
    def _():
        gu = jnp.dot(_from_tiles(x_ref).astype(BF16), wgu_ref[...], preferred_element_type=F32)
        g = gu[:, :D_EXPERT]
        hidden = (g * _sigmoid(g) * gu[:, D_EXPERT:]).astype(BF16)
        _to_tiles(y_ref, jnp.dot(hidden, wdn_ref[...], preferred_element_type=F32))

    @pl.when(jnp.logical_not(used))
    def _():
        y_ref[...] = jnp.zeros(y_ref.shape, F32)


def _experts(block_expert, n_used, xs, w_gate, w_up, w_down, blk):
    n_blocks = xs.shape[0] // blk
    used = lambda i, be, nu: (_last_used(i, nu), 0, 0)
    wsel = lambda i, be, nu: (be[_last_used(i, nu)], 0, 0)
    return pl.pallas_call(
        _expert_kernel,
        grid_spec=pltpu.PrefetchScalarGridSpec(
            num_scalar_prefetch=2,
            grid=(n_blocks,),
            in_specs=[
                pl.BlockSpec((blk,) + TOK_TILE, used),
                pl.BlockSpec((1, D_MODEL, D_EXPERT), wsel),
                pl.BlockSpec((1, D_MODEL, D_EXPERT), wsel),
                pl.BlockSpec((1, D_EXPERT, D_MODEL), wsel),
            ],
            out_specs=pl.BlockSpec((blk,) + TOK_TILE, lambda i, be, nu: (i, 0, 0)),
            scratch_shapes=[pltpu.VMEM((D_MODEL, 2 * D_EXPERT), BF16), pltpu.VMEM((D_EXPERT, D_MODEL), BF16)],
        ),
        out_shape=jax.ShapeDtypeStruct(xs.shape, F32),
        compiler_params=_cparams(("arbitrary",)),
        name="moe_experts",
    )(block_expert, n_used, xs, w_gate, w_up, w_down)


def _combine_kernel(dest_ref, x1_ref, route_ref, nw_ref, ys_ref, yp_ref, ysm_ref, ybuf_ref, sem,
                    *, tile, n_prompt_tiles):
    i = pl.program_id(0)
    base = i * tile

    def issue(t, carry):
        for k in range(2):
            _row_copy(ys_ref, dest_ref[2 * (base + t) + k], ybuf_ref.at[k], t, sem).start()
        return carry

    def drain(t, carry):
        for k in range(2):
            _row_copy(ys_ref, 0, ybuf_ref.at[k], 0, sem).wait()
        return carry

    lax.fori_loop(0, tile, issue, 0)
    lax.fori_loop(0, tile, drain, 0)
    r = route_ref[...]
    y = _from_tiles(ybuf_ref.at[0]) * _col(r, 2) + _from_tiles(ybuf_ref.at[1]) * _col(r, 3)
    x2 = x1_ref[...] + y
    out = x2 * lax.rsqrt(jnp.mean(x2 * x2, axis=-1, keepdims=True) + EPS) * nw_ref[...]

    @pl.when(i < n_prompt_tiles)
    def _():
        yp_ref[...] = out

    @pl.when(i >= n_prompt_tiles)
    def _():
        ysm_ref[...] = out


def _combine(dest_flat, x1, route, norm_w, ys, tp, ts):
    tile = _token_tile(tp, ts)
    npt, nst = tp // tile, ts // tile
    return pl.pallas_call(
        functools.partial(_combine_kernel, tile=tile, n_prompt_tiles=npt),
        grid_spec=pltpu.PrefetchScalarGridSpec(
            num_scalar_prefetch=1,
            grid=(npt + nst,),
            in_specs=[
                pl.BlockSpec((tile, D_MODEL), lambda i, d: (i, 0)),
                pl.BlockSpec((tile, LANES), lambda i, d: (i, 0)),
                pl.BlockSpec((1, D_MODEL), lambda i, d: (0, 0)),
                pl.BlockSpec(memory_space=pl.ANY),
            ],
            out_specs=[
                pl.BlockSpec((tile, D_MODEL), lambda i, d: (jnp.minimum(i, npt - 1), 0)),
                pl.BlockSpec((tile, D_MODEL), lambda i, d: (jnp.maximum(i - npt, 0), 0)),
            ],
            scratch_shapes=[pltpu.VMEM((2, tile) + TOK_TILE, F32), pltpu.SemaphoreType.DMA(())],
        ),
        out_shape=[
            jax.ShapeDtypeStruct((tp, D_MODEL), F32),
            jax.ShapeDtypeStruct((ts, D_MODEL), F32),
        ],
        compiler_params=_cparams(("arbitrary",)),
        name="moe_combine",
    )(dest_flat, x1, route, norm_w, ys)


def _layer(xp, xs, n_batch, seq, s_batch, s_seq, conv_state, dn_state, k_cache, v_cache,
           w_in, conv_w, a_log, dt_bias, dn_norm_w, sinks, rel_bias, w_out, norm_mix_w, norm_ffn_w,
           w_rg, b_rg, w_re, b_re, w_gate, w_up, w_down, norm_final_w):
    tp, ts = xp.shape[0], xs.shape[0]
    t_all = tp + ts
    row = lambda v: v.reshape(1, -1).astype(F32)

    o = np.cumsum((0, DN_QK_W, DN_QK_W, DN_V_W, DN_V_W, DN_HEADS, DN_HEADS, SW_HEADS * SW_HD, SW_KV_W, SW_KV_W,
                   D_MODEL, D_MODEL)).tolist()
    w_big = jnp.concatenate([w_in[:, o[0]:o[4]], w_in[:, o[6]:o[7]], w_in[:, o[9]:o[11]], w_in[:, o[7]:o[9]]],
                            axis=1).astype(BF16)
    w_small = jnp.pad(w_in[:, o[4]:o[6]], ((0, 0), (0, LANES - 2 * DN_HEADS))).astype(BF16)
    head_row = lambda v: jnp.pad(v.astype(F32), (DN_HEADS, LANES - 2 * DN_HEADS)).reshape(1, LANES)
    w_router = jnp.pad(jnp.concatenate([w_rg, w_re], axis=1).astype(F32),
                       ((0, 0), (0, LANES - N_GROUPS - N_EXPERTS)))
    b_router = jnp.pad(jnp.concatenate([b_rg, b_re]).astype(F32), (0, LANES - N_GROUPS - N_EXPERTS)).reshape(1, LANES)

    conv0 = jnp.zeros((n_batch, DN_CONV - 1, DN_CONV_W), F32)
    proj, ba, conv_tail = _inproj(xp, xs, row(norm_mix_w), w_big, w_small, conv_w.astype(F32), conv0, seq)
    p_conv = conv_tail[:, SUBLANES - (DN_CONV - 1):, :]

    dn0 = jnp.zeros((n_batch, DN_HEADS, DN_DK, DN_DV), F32)
    oa_p, p_dn = _dn_prompt(proj, ba, head_row(a_log), head_row(dt_bias), row(dn_norm_w), dn0, n_batch, seq)
    oa_s, s_conv, s_dn = _dn_sample(proj, ba, conv_w.astype(F32), head_row(a_log), head_row(dt_bias), row(dn_norm_w),
                                    conv_state, dn_state, tp, s_batch, s_seq)

    qpos = jnp.arange(SW_BLOCK)[:, None]
    kpos = jnp.arange(2 * SW_BLOCK)[None, :] - SW_BLOCK
    bias_p = _relbias(rel_bias.astype(F32), _rel_bucket(qpos - kpos))
    ob_p, p_k, p_v = _swa_prompt(proj, sinks.astype(F32), bias_p, n_batch, seq)
    n_cache = k_cache.shape[1]
    n_keys = -(-(n_cache + SUBLANES) // LANES) * LANES
    tq = n_cache + jnp.arange(SUBLANES)[:, None]
    bias_s = _relbias(rel_bias.astype(F32), _rel_bucket(tq - jnp.arange(n_keys)[None, :]))
    bias_s = bias_s.reshape(SW_KV_HEADS, SW_GROUP * SUBLANES, n_keys)
    sink_rows = jnp.broadcast_to(jnp.repeat(sinks.astype(F32).reshape(SW_KV_HEADS, SW_GROUP), SUBLANES, axis=1)[:, :, None],
                                 (SW_KV_HEADS, SW_GROUP * SUBLANES, n_keys))
    ob_s, s_k, s_v = _swa_sample(proj, k_cache.reshape(s_batch, n_cache, SW_KV_W), v_cache.reshape(s_batch, n_cache, SW_KV_W),
                                 bias_s, sink_rows, tp, s_batch, s_seq)

    x1, h2, route = _mix(xp, xs, oa_p, oa_s, ob_p, ob_s, proj, w_out.astype(BF16), row(norm_ffn_w), w_router, b_router)

    tile = _token_tile(tp, ts)
    n_blocks = -(-2 * t_all // MOE_BLOCK) + N_EXPERTS
    dest, meta = _rank(route, tile, MOE_BLOCK, n_blocks)
    dest_flat = dest[:, :2].reshape(-1)
    block_expert = meta[:n_blocks, 0]
    n_used = meta[0:1, 1]
    xs_sorted = _dispatch(dest_flat, h2, jnp.zeros((n_blocks * MOE_BLOCK,) + TOK_TILE, F32), tile)
    ys = _experts(block_expert, n_used, xs_sorted, w_gate, w_up, w_down, MOE_BLOCK)
    y_p, y_s = _combine(dest_flat, x1, route, row(norm_final_w), ys, tp, ts)

    kv_shape = (n_batch, WINDOW, SW_KV_HEADS, SW_HD)
    return (y_p, y_s, p_conv, p_dn, p_k.reshape(kv_shape), p_v.reshape(kv_shape), s_conv, s_dn,
            s_k.reshape(k_cache.shape), s_v.reshape(v_cache.shape))


def kernel(x_prompt, x_sample, state_dn_conv, state_dn, cache_swa_k, cache_swa_v, w_in, conv_w, a_log, dt_bias, dn_norm_w, sinks, rel_bias, w_out, norm_mix_w, norm_ffn_w, w_router_group, b_router_group, w_router_expert, b_router_expert, w_gate, w_up, w_down, norm_final_w):
    depth = w_in.shape[0]
    assert depth == 1, "the final-norm fusion below assumes a single layer"
    n_batch, seq, _ = x_prompt.shape
    s_batch, s_seq, _ = x_sample.shape
    outs = _layer(x_prompt.reshape(-1, D_MODEL), x_sample.reshape(-1, D_MODEL), n_batch, seq, s_batch, s_seq,
                  state_dn_conv[0], state_dn[0], cache_swa_k[0], cache_swa_v[0],
                  w_in[0], conv_w[0], a_log[0], dt_bias[0], dn_norm_w[0], sinks[0], rel_bias,
                  w_out[0], norm_mix_w[0], norm_ffn_w[0], w_router_group[0], b_router_group[0],
                  w_router_expert[0], b_router_expert[0], w_gate[0], w_up[0], w_down[0], norm_final_w)
    y_p, y_s, p_conv, p_dn, p_k, p_v, s_conv, s_dn, s_k, s_v = outs
    return (y_p.reshape(x_prompt.shape), y_s.reshape(x_sample.shape), p_conv[None], p_dn[None], p_k[None], p_v[None],
            s_conv[None], s_dn[None], s_k[None], s_v[None])
```

```python
import functools
import math

import jax
import jax.numpy as jnp
import numpy as np
from jax import lax
from jax.experimental import pallas as pl
from jax.experimental.pallas import tpu as pltpu

F32 = jnp.float32
BF16 = jnp.bfloat16
I32 = jnp.int32

D_MODEL = 1024
DN_HEADS = 8
DN_DK = 128
DN_DV = 128
DN_CONV = 4
DN_CHUNK = 64
DN_QK_W = DN_HEADS * DN_DK
DN_V_W = DN_HEADS * DN_DV
DN_CONV_W = 2 * DN_QK_W + DN_V_W
SW_HEADS = 16
SW_KV_HEADS = 2
SW_GROUP = SW_HEADS // SW_KV_HEADS
SW_HD = 64
SW_KV_W = SW_KV_HEADS * SW_HD
WINDOW = 128
SW_BLOCK = 128
REL_BUCKETS = 32
REL_MAX_DIST = 128
N_GROUPS = 8
EXP_PER_GROUP = 8
N_EXPERTS = N_GROUPS * EXP_PER_GROUP
D_EXPERT = 256
MOE_BLOCK = 256
EPS = 1e-6

LANES = 128
SUBLANES = 8
VMEM_LIMIT = 56 * 1024 * 1024

COL_QKV = 0
COL_Z = 3072
COL_SQ = 4096
COL_GA = 5120
COL_GB = 6144
COL_SK = 7168
COL_SV = 7296
PROJ_W = 7424
PROJ_CHUNK = 512


def _cparams(sem):
    return pltpu.CompilerParams(dimension_semantics=sem, vmem_limit_bytes=VMEM_LIMIT)


def _sigmoid(x):
    return 0.5 * jnp.tanh(0.5 * x) + 0.5


def _dot(a, b):
    return jnp.dot(a.astype(BF16), b.astype(BF16), preferred_element_type=F32)


def _dot_nt(a, b):
    return lax.dot_general(a.astype(BF16), b.astype(BF16), (((1,), (1,)), ((), ())), preferred_element_type=F32)


def _dot_tn(a, b):
    return lax.dot_general(a.astype(BF16), b.astype(BF16), (((0,), (0,)), ((), ())), preferred_element_type=F32)


def _dot_exact(a, b):
    return jnp.dot(a, b, precision=lax.Precision.HIGHEST, preferred_element_type=F32)


def _lane_iota(shape):
    return lax.broadcasted_iota(I32, shape, len(shape) - 1)


def _row_iota(shape):
    return lax.broadcasted_iota(I32, shape, len(shape) - 2)


def _col(x, j):
    return jnp.sum(jnp.where(_lane_iota(x.shape) == j, x, 0.0), axis=-1, keepdims=True)


def _token_tile(*sizes):
    for t in (256, 128, 64, 32, 16, 8):
        if all(s % t == 0 for s in sizes):
            return t
    raise ValueError(f"token counts {sizes} need a common tile that is a multiple of 8")


def _inproj_kernel(xp_ref, xs_ref, nw_ref, wb_ref, ws_ref, convw_ref, conv0_ref, proj_ref, ba_ref, tail_ref, cbuf_ref,
                   *, n_prompt_tiles, tiles_per_seq):
    i = pl.program_id(0)
    tm = xp_ref.shape[0]
    hist = SUBLANES - (DN_CONV - 1)

    def project(x_ref, conv):
        x = x_ref[...]
        h = (x * lax.rsqrt(jnp.mean(x * x, axis=-1, keepdims=True) + EPS) * nw_ref[...]).astype(BF16)
        ba_ref[...] = jnp.dot(h, ws_ref[...], preferred_element_type=F32)
        top = _row_iota((SUBLANES, PROJ_CHUNK))
        for c0 in range(0, PROJ_W, PROJ_CHUNK):
            c1 = min(c0 + PROJ_CHUNK, PROJ_W)
            cur = jnp.dot(h, wb_ref[:, c0:c1], preferred_element_type=F32)
            if conv and c1 <= COL_QKV + DN_CONV_W:
                prev = cbuf_ref[:, c0:c1]
                acc = cur * convw_ref[DN_CONV - 1:DN_CONV, c0:c1]
                for s in range(1, DN_CONV):
                    sh = pltpu.roll(cur, s, axis=0)
                    head = jnp.where(top < s, pltpu.roll(prev, s, axis=0), sh[:SUBLANES])
                    sh = jnp.concatenate([head, sh[SUBLANES:]], axis=0)
                    acc = acc + sh * convw_ref[DN_CONV - 1 - s:DN_CONV - s, c0:c1]
                cbuf_ref[:, c0:c1] = cur[tm - SUBLANES:]
                cur = acc * _sigmoid(acc)
            proj_ref[:, c0:c1] = cur

    @pl.when(i < n_prompt_tiles)
    def _():
        @pl.when(i % tiles_per_seq == 0)
        def _():
            cbuf_ref[...] = jnp.zeros(cbuf_ref.shape, F32)
            cbuf_ref[hist:SUBLANES, :] = conv0_ref[0]

        project(xp_ref, True)
        tail_ref[0] = cbuf_ref[...]

    @pl.when(i >= n_prompt_tiles)
    def _():
        project(xs_ref, False)


def _inproj(xp, xs, norm_w, w_big, w_small, conv_w, conv0, seq):
    tp, ts = xp.shape[0], xs.shape[0]
    tm = _token_tile(tp, ts, seq)
    assert COL_QKV == 0 and DN_CONV_W % PROJ_CHUNK == 0
    npt, nst = tp // tm, ts // tm
    tps = seq // tm
    const = lambda i: (0, 0)
    seq_of = lambda i: (jnp.minimum(i, npt - 1) // tps, 0, 0)
    return pl.pallas_call(
        functools.partial(_inproj_kernel, n_prompt_tiles=npt, tiles_per_seq=tps),
        grid=(npt + nst,),
        in_specs=[
            pl.BlockSpec((tm, D_MODEL), lambda i: (jnp.minimum(i, npt - 1), 0)),
            pl.BlockSpec((tm, D_MODEL), lambda i: (jnp.maximum(i - npt, 0), 0)),
            pl.BlockSpec((1, D_MODEL), const),
            pl.BlockSpec((D_MODEL, PROJ_W), const, pipeline_mode=pl.Buffered(1)),
            pl.BlockSpec((D_MODEL, LANES), const),
            pl.BlockSpec((DN_CONV, DN_CONV_W), const),
            pl.BlockSpec((1, DN_CONV - 1, DN_CONV_W), seq_of),
        ],
        out_specs=[
            pl.BlockSpec((tm, PROJ_W), lambda i: (i, 0)),
            pl.BlockSpec((tm, LANES), lambda i: (i, 0)),
            pl.BlockSpec((1, SUBLANES, DN_CONV_W), seq_of),
        ],
        out_shape=[
            jax.ShapeDtypeStruct((tp + ts, PROJ_W), F32),
            jax.ShapeDtypeStruct((tp + ts, LANES), F32),
            jax.ShapeDtypeStruct((tp // seq, SUBLANES, DN_CONV_W), F32),
        ],
        scratch_shapes=[pltpu.VMEM((SUBLANES, DN_CONV_W), F32)],
        compiler_params=_cparams(("arbitrary",)),
        name="inproj",
    )(xp, xs, norm_w, w_big, w_small, conv_w, conv0)


def _dn_core(groups, alog, dtb, nw, read_state, write_state, n_seg, seg_valid):
    rows = groups[0][0].shape[0]
    sr = rows // n_seg
    assert sr * n_seg == rows and sr & (sr - 1) == 0 and rows <= LANES
    seg_shift = sr.bit_length() - 1
    ri = _row_iota((rows, rows))
    ci = _lane_iota((rows, rows))
    incl = ri >= ci
    strict = ri > ci
    if n_seg > 1:
        same = (ri >> seg_shift) == (ci >> seg_shift)
        incl = incl & same
        strict = strict & same
    l_incl = incl.astype(F32)
    eye = (ri == ci).astype(F32)
    levels = max(1, math.ceil(math.log2(seg_valid)))

    beta_all, gsum_all, gtot_all, gsum_t = [], [], [], []
    for _, _, ba in groups:
        b_all = _sigmoid(ba)
        sp = ba + dtb
        softplus = jnp.maximum(sp, 0.0) + jnp.log1p(jnp.exp(-jnp.abs(sp)))
        g_all = -jnp.exp(alog) * softplus
        if seg_valid < sr:
            live = (_row_iota((rows, LANES)) & (sr - 1)) < seg_valid
            b_all = jnp.where(live, b_all, 0.0)
            g_all = jnp.where(live, g_all, 0.0)
        gs = _dot_exact(l_incl, g_all)
        beta_all.append(b_all)
        gsum_all.append(gs)
        gtot_all.append(_dot_exact(same.astype(F32), g_all) if n_seg > 1 else gs[rows - 1:rows, :])
        padded = gs if rows == LANES else jnp.concatenate([gs, jnp.zeros((LANES - rows, LANES), F32)], axis=0)
        gsum_t.append(padded.T)

    probs = [(g, h) for g in range(len(groups)) for h in range(DN_HEADS)]
    segs = range(n_seg)
    q, k, v, kb, beta, gsum, gtot = {}, {}, {}, {}, {}, {}, {}
    for p in probs:
        g, h = p
        qkv = groups[g][0]
        qh = qkv[:, h * DN_DK:(h + 1) * DN_DK]
        kh = qkv[:, DN_QK_W + h * DN_DK:DN_QK_W + (h + 1) * DN_DK]
        v[p] = qkv[:, 2 * DN_QK_W + h * DN_DV:2 * DN_QK_W + (h + 1) * DN_DV]
        q[p] = qh * lax.rsqrt(jnp.sum(qh * qh, axis=-1, keepdims=True) + 1e-6) * (DN_DK ** -0.5)
        k[p] = kh * lax.rsqrt(jnp.sum(kh * kh, axis=-1, keepdims=True) + 1e-6)
        beta[p] = _col(beta_all[g], h)
        gsum[p] = _col(gsum_all[g], DN_HEADS + h)
        gtot[p] = _col(gtot_all[g], DN_HEADS + h)
        kb[p] = k[p] * beta[p]
    kq = {p: _dot_nt(jnp.concatenate([kb[p], q[p]], axis=0), k[p]) for p in probs}
    gamma = {(g, h): jnp.exp(jnp.where(incl, gsum[(g, h)] - gsum_t[g][DN_HEADS + h:DN_HEADS + h + 1, :rows], -jnp.inf))
             for g, h in probs}
    attn = {p: kq[p][rows:] * gamma[p] for p in probs}
    pw = {p: -jnp.where(strict, kq[p][:rows] * gamma[p], 0.0) for p in probs}
    t = {p: eye + pw[p] for p in probs}
    for _ in range(1, levels):
        pw = {p: _dot(pw[p], pw[p]) for p in probs}
        t = {p: t[p] + _dot(t[p], pw[p]) for p in probs}
    eg = {p: jnp.exp(gsum[p]) for p in probs}
    uw = {p: _dot(t[p], jnp.concatenate([v[p] * beta[p], kb[p] * eg[p]], axis=1)) for p in probs}
    qg = {p: q[p] * eg[p] for p in probs}
    state = {(p, s): read_state(p[0], s, p[1]) for p in probs for s in segs}
    wq = {(p, s): _dot(jnp.concatenate([uw[p][s * sr:(s + 1) * sr, DN_DV:], qg[p][s * sr:(s + 1) * sr]], axis=0),
                       state[(p, s)]) for p in probs for s in segs}
    join = lambda pieces: pieces[0] if len(pieces) == 1 else jnp.concatenate(pieces, axis=0)
    v_new = {p: uw[p][:, :DN_DV] - join([wq[(p, s)][:sr] for s in segs]) for p in probs}
    o = {p: join([wq[(p, s)][sr:] for s in segs]) + _dot(attn[p], v_new[p]) for p in probs}
    kd = {p: k[p] * jnp.exp(gtot[p] - gsum[p]) for p in probs}
    for p in probs:
        for s in segs:
            r0 = s * sr if n_seg > 1 else 0
            decay = jnp.exp(gtot[p][r0:r0 + 1, :])
            write_state(p[0], s, p[1],
                        state[(p, s)] * decay + _dot_tn(kd[p][s * sr:(s + 1) * sr], v_new[p][s * sr:(s + 1) * sr]))
    outs = []
    for g, (_, z, _) in enumerate(groups):
        heads = []
        for h in range(DN_HEADS):
            oh = o[(g, h)]
            zz = z[:, h * DN_DV:(h + 1) * DN_DV]
            on = oh * lax.rsqrt(jnp.mean(oh * oh, axis=-1, keepdims=True) + EPS) * nw
            heads.append(on * (zz * _sigmoid(zz)))
        outs.append(jnp.concatenate(heads, axis=1))
    return outs


def _dn_prompt_kernel(*refs, chunk, n_batch):
    nb = n_batch
    qkv_refs, z_refs, ba_refs = refs[0:nb], refs[nb:2 * nb], refs[2 * nb:3 * nb]
    alog_ref, dtb_ref, nw_ref, s0_ref, o_ref, sout_ref = refs[3 * nb:]

    @pl.when(pl.program_id(0) == 0)
    def _():
        sout_ref[...] = s0_ref[...]

    groups = [(qkv_refs[b][...], z_refs[b][...], ba_refs[b][...]) for b in range(nb)]

    def read_state(g, s, h):
        return sout_ref[g, h]

    def write_state(g, s, h, val):
        sout_ref[g, h] = val

    outs = _dn_core(groups, alog_ref[...], dtb_ref[...], nw_ref[...], read_state, write_state, 1, chunk)
    for b in range(nb):
        o_ref[b] = outs[b]


def _dn_prompt(proj, ba, alog_row, dtb_row, dn_nw, s0, n_batch, seq):
    chunk = min(DN_CHUNK, seq)
    assert seq % chunk == 0 and chunk % SUBLANES == 0
    nc = seq // chunk
    const2 = lambda c: (0, 0)
    rows = lambda b, col: (lambda c: (b * nc + c, col))
    batches = range(n_batch)
    o, s_out = pl.pallas_call(
        functools.partial(_dn_prompt_kernel, chunk=chunk, n_batch=n_batch),
        grid=(nc,),
        in_specs=(
            [pl.BlockSpec((chunk, DN_CONV_W), rows(b, COL_QKV // DN_CONV_W)) for b in batches]
            + [pl.BlockSpec((chunk, DN_V_W), rows(b, COL_Z // DN_V_W)) for b in batches]
            + [pl.BlockSpec((chunk, LANES), rows(b, 0)) for b in batches]
            + [
                pl.BlockSpec((1, LANES), const2),
                pl.BlockSpec((1, LANES), const2),
                pl.BlockSpec((1, DN_DV), const2),
                pl.BlockSpec((n_batch, DN_HEADS, DN_DK, DN_DV), lambda c: (0, 0, 0, 0)),
            ]
        ),
        out_specs=[
            pl.BlockSpec((n_batch, chunk, DN_V_W), lambda c: (0, c, 0)),
            pl.BlockSpec((n_batch, DN_HEADS, DN_DK, DN_DV), lambda c: (0, 0, 0, 0)),
        ],
        out_shape=[
            jax.ShapeDtypeStruct((n_batch, seq, DN_V_W), F32),
            jax.ShapeDtypeStruct((n_batch, DN_HEADS, DN_DK, DN_DV), F32),
        ],
        compiler_params=_cparams(("arbitrary",)),
        name="dn_prompt",
    )(*([proj] * n_batch), *([proj] * n_batch), *([ba] * n_batch), alog_row, dtb_row, dn_nw, s0)
    return o.reshape(n_batch * seq, DN_V_W), s_out


def _dn_sample_kernel(qkv_ref, z_ref, ba_ref, convw_ref, alog_ref, dtb_ref, nw_ref, conv0_ref, s0_ref,
                      o_ref, convout_ref, sout_ref, cbuf_ref, *, seq, n_bb):
    hist = SUBLANES - (DN_CONV - 1)
    per = SUBLANES // seq
    n_tiles = n_bb // per

    def spread(ref):
        pieces = []
        for j in range(n_tiles):
            x8 = ref[j * SUBLANES:(j + 1) * SUBLANES, :]
            for r in range(per):
                pieces.append(x8 if r == 0 else pltpu.roll(x8, SUBLANES - r * seq, axis=0))
        return pieces

    for bb, piece in enumerate(spread(qkv_ref)):
        cbuf_ref[bb, SUBLANES:2 * SUBLANES, :] = piece
    cbuf_ref[:, hist:SUBLANES, :] = conv0_ref[...]
    w = convw_ref[...]
    acc = cbuf_ref[:, hist:hist + SUBLANES, :] * w[0:1, :]
    for i in range(1, DN_CONV):
        acc = acc + cbuf_ref[:, hist + i:hist + i + SUBLANES, :] * w[i:i + 1, :]
    live = _row_iota(acc.shape) < seq
    qkv = jnp.where(live, acc * _sigmoid(acc), 0.0).reshape(n_bb * SUBLANES, DN_CONV_W)
    convout_ref[...] = cbuf_ref[:, SUBLANES + seq - (DN_CONV - 1):SUBLANES + seq, :]

    def read_state(g, s, h):
        return s0_ref[s, h]

    def write_state(g, s, h, val):
        sout_ref[s, h] = val

    group = (qkv, jnp.concatenate(spread(z_ref), axis=0), jnp.concatenate(spread(ba_ref), axis=0))
    o = _dn_core([group], alog_ref[...], dtb_ref[...], nw_ref[...], read_state, write_state, n_bb, seq)[0]
    rows = _row_iota((SUBLANES, DN_V_W))
    for j in range(n_tiles):
        tile = None
        for r in range(per):
            bb = j * per + r
            piece = o[bb * SUBLANES:(bb + 1) * SUBLANES]
            piece = piece if r == 0 else pltpu.roll(piece, r * seq, axis=0)
            tile = piece if tile is None else jnp.where(rows >= r * seq, piece, tile)
        o_ref[j * SUBLANES:(j + 1) * SUBLANES, :] = tile


def _dn_sample(proj, ba, conv_w, alog_row, dtb_row, dn_nw, conv0, s0, row0, n_batch, seq):
    assert SUBLANES % seq == 0 and seq >= DN_CONV - 1
    n_bb = SUBLANES
    rows_in = n_bb * seq
    assert n_batch % n_bb == 0 and row0 % rows_in == 0
    rb0 = row0 // rows_in
    const1 = lambda i: (0, 0)
    return pl.pallas_call(
        functools.partial(_dn_sample_kernel, seq=seq, n_bb=n_bb),
        grid=(n_batch // n_bb,),
        in_specs=[
            pl.BlockSpec((rows_in, DN_CONV_W), lambda i: (rb0 + i, COL_QKV // DN_CONV_W)),
            pl.BlockSpec((rows_in, DN_V_W), lambda i: (rb0 + i, COL_Z // DN_V_W)),
            pl.BlockSpec((rows_in, LANES), lambda i: (rb0 + i, 0)),
            pl.BlockSpec((DN_CONV, DN_CONV_W), const1),
            pl.BlockSpec((1, LANES), const1),
            pl.BlockSpec((1, LANES), const1),
            pl.BlockSpec((1, DN_DV), const1),
            pl.BlockSpec((n_bb, DN_CONV - 1, DN_CONV_W), lambda i: (i, 0, 0)),
            pl.BlockSpec((n_bb, DN_HEADS, DN_DK, DN_DV), lambda i: (i, 0, 0, 0)),
        ],
        out_specs=[
            pl.BlockSpec((rows_in, DN_V_W), lambda i: (i, 0)),
            pl.BlockSpec((n_bb, DN_CONV - 1, DN_CONV_W), lambda i: (i, 0, 0)),
            pl.BlockSpec((n_bb, DN_HEADS, DN_DK, DN_DV), lambda i: (i, 0, 0, 0)),
        ],
        out_shape=[
            jax.ShapeDtypeStruct((n_batch * seq, DN_V_W), F32),
            jax.ShapeDtypeStruct((n_batch, DN_CONV - 1, DN_CONV_W), F32),
            jax.ShapeDtypeStruct((n_batch, DN_HEADS, DN_DK, DN_DV), F32),
        ],
        scratch_shapes=[pltpu.VMEM((n_bb, 2 * SUBLANES, DN_CONV_W), F32)],
        compiler_params=_cparams(("arbitrary",)),
        name="dn_sample",
    )(proj, proj, ba, conv_w, alog_row, dtb_row, dn_nw, conv0, s0)


def _rel_bucket(dist):
    n = jnp.maximum(dist, 0)
    max_exact = REL_BUCKETS // 2
    large = max_exact + (jnp.log(jnp.maximum(n, 1).astype(F32) / max_exact)
                         / math.log(REL_MAX_DIST / max_exact) * (REL_BUCKETS - max_exact)).astype(I32)
    return jnp.where(n < max_exact, n, jnp.minimum(large, REL_BUCKETS - 1))


def _relbias_kernel(tab_ref, bucket_ref, o_ref):
    h = pl.program_id(0)
    bk = bucket_ref[...]
    acc = jnp.zeros(bk.shape, F32)
    for b in range(REL_BUCKETS):
        acc = jnp.where(bk == b, tab_ref[b * SW_HEADS + h], acc)
    o_ref[0] = acc


def _relbias(rel_table, bucket):
    nq, ns = bucket.shape
    return pl.pallas_call(
        _relbias_kernel,
        grid=(SW_HEADS,),
        in_specs=[
            pl.BlockSpec(memory_space=pltpu.SMEM),
            pl.BlockSpec((nq, ns), lambda h: (0, 0)),
        ],
        out_specs=pl.BlockSpec((1, nq, ns), lambda h: (h, 0, 0)),
        out_shape=jax.ShapeDtypeStruct((SW_HEADS, nq, ns), F32),
        compiler_params=_cparams(("arbitrary",)),
        name="relbias",
    )(rel_table.reshape(-1), bucket)


def _dup_halves(x):
    lo = _lane_iota(x.shape) < SW_HD
    xr = pltpu.roll(x, SW_HD, axis=1)
    return jnp.where(lo, x, xr).astype(BF16), jnp.where(lo, xr, x).astype(BF16)


def _sink_softmax_pv(s, valid, sink, vv):
    s = jnp.where(valid, s, -jnp.inf)
    m = jnp.maximum(jnp.max(s, axis=-1, keepdims=True), sink)
    p = jnp.exp(s - m)
    p = p / (jnp.sum(p, axis=-1, keepdims=True) + jnp.exp(sink - m))
    return _dot(p, vv)


def _swa_prompt_kernel(sinks_ref, q_ref, kc_ref, kp_ref, vc_ref, vp_ref, bias_ref, o_ref, klast_ref, vlast_ref):
    i = pl.program_id(1)
    klast_ref[0] = kc_ref[...]
    vlast_ref[0] = vc_ref[...]
    kk = _dup_halves(jnp.concatenate([kp_ref[...], kc_ref[...]], axis=0))
    vv = _dup_halves(jnp.concatenate([vp_ref[...], vc_ref[...]], axis=0))
    shape = (SW_BLOCK, 2 * SW_BLOCK)
    qi = _row_iota(shape)
    kj = _lane_iota(shape)
    dist = qi - kj + SW_BLOCK
    first_key = jnp.where(i > 0, 0, SW_BLOCK)
    valid = (dist >= 0) & (dist < WINDOW) & (kj >= first_key)
    lo = _lane_iota((SW_BLOCK, LANES)) < SW_HD
    for pair in range(SW_HEADS // 2):
        qp = q_ref[:, pair * LANES:(pair + 1) * LANES]
        outs = []
        for half in range(2):
            hq = 2 * pair + half
            kv = hq // SW_GROUP
            qm = jnp.where(lo if half == 0 else ~lo, qp, 0.0)
            s = _dot_nt(qm, kk[kv]) * (SW_HD ** -0.5) + bias_ref[hq]
            outs.append(_sink_softmax_pv(s, valid, sinks_ref[hq], vv[kv]))
        o_ref[:, pair * LANES:(pair + 1) * LANES] = jnp.where(lo, outs[0], outs[1])


def _swa_prompt(proj, sinks, bias, n_batch, seq):
    assert seq % SW_BLOCK == 0 and WINDOW == SW_BLOCK
    nb = seq // SW_BLOCK
    cur = lambda col: (lambda b, i: (b * nb + i, col))
    prev = lambda col: (lambda b, i: (b * nb + jnp.maximum(i - 1, 0), col))
    return pl.pallas_call(
        _swa_prompt_kernel,
        grid=(n_batch, nb),
        in_specs=[
            pl.BlockSpec(memory_space=pltpu.SMEM),
            pl.BlockSpec((SW_BLOCK, SW_HEADS * SW_HD), cur(COL_SQ // (SW_HEADS * SW_HD))),
            pl.BlockSpec((SW_BLOCK, SW_KV_W), cur(COL_SK // SW_KV_W)),
            pl.BlockSpec((SW_BLOCK, SW_KV_W), prev(COL_SK // SW_KV_W)),
            pl.BlockSpec((SW_BLOCK, SW_KV_W), cur(COL_SV // SW_KV_W)),
            pl.BlockSpec((SW_BLOCK, SW_KV_W), prev(COL_SV // SW_KV_W)),
            pl.BlockSpec((SW_HEADS, SW_BLOCK, 2 * SW_BLOCK), lambda b, i: (0, 0, 0)),
        ],
        out_specs=[
            pl.BlockSpec((SW_BLOCK, SW_HEADS * SW_HD), lambda b, i: (b * nb + i, 0)),
            pl.BlockSpec((1, SW_BLOCK, SW_KV_W), lambda b, i: (b, 0, 0)),
            pl.BlockSpec((1, SW_BLOCK, SW_KV_W), lambda b, i: (b, 0, 0)),
        ],
        out_shape=[
            jax.ShapeDtypeStruct((n_batch * seq, SW_HEADS * SW_HD), F32),
            jax.ShapeDtypeStruct((n_batch, SW_BLOCK, SW_KV_W), F32),
            jax.ShapeDtypeStruct((n_batch, SW_BLOCK, SW_KV_W), F32),
        ],
        compiler_params=_cparams(("arbitrary", "arbitrary")),
        name="swa_prompt",
    )(sinks, proj, proj, proj, proj, proj, bias)


def _swa_sample_kernel(q_ref, kn_ref, vn_ref, kc_ref, vc_ref, bias_ref, sink_ref,
                       o_ref, ko_ref, vo_ref, kall_ref, vall_ref, *, seq, n_bb, n_cache):
    n_keys = kall_ref.shape[0]
    zeros_tail = jnp.zeros((n_keys - n_cache - SUBLANES, LANES), F32)
    shape = (SW_GROUP * SUBLANES, n_keys)
    t = _row_iota(shape) % SUBLANES
    s_idx = _lane_iota(shape)
    dist = n_cache + t - s_idx
    valid = (dist >= 0) & (dist < WINDOW)
    lo = _lane_iota((SUBLANES, LANES)) < SW_HD
    out = None
    for bb in range(n_bb):
        shift = (SUBLANES - bb * seq) % SUBLANES

        def top(x, shift=shift):
            return x if shift == 0 else pltpu.roll(x, shift, axis=0)

        kall_ref[0:n_cache, :] = kc_ref[bb]
        kall_ref[n_cache:n_cache + SUBLANES, :] = top(kn_ref[...])
        kall_ref[n_cache + SUBLANES:, :] = zeros_tail
        vall_ref[0:n_cache, :] = vc_ref[bb]
        vall_ref[n_cache:n_cache + SUBLANES, :] = top(vn_ref[...])
        vall_ref[n_cache + SUBLANES:, :] = zeros_tail
        ko_ref[bb] = kall_ref[seq:seq + n_cache, :]
        vo_ref[bb] = vall_ref[seq:seq + n_cache, :]
        kk = _dup_halves(kall_ref[...])
        vv = _dup_halves(vall_ref[...])
        q8 = top(q_ref[...])
        pairs = []
        for kv in range(SW_KV_HEADS):
            pieces = []
            for g in range(SW_GROUP):
                hq = kv * SW_GROUP + g
                qp = q8[:, (hq // 2) * LANES:(hq // 2 + 1) * LANES]
                pieces.append(jnp.where(lo if hq % 2 == 0 else ~lo, qp, 0.0))
            qs = jnp.concatenate(pieces, axis=0)
            s = _dot_nt(qs, kk[kv]) * (SW_HD ** -0.5) + bias_ref[kv]
            res = _sink_softmax_pv(s, valid, sink_ref[kv], vv[kv])
            for g in range(0, SW_GROUP, 2):
                pairs.append(jnp.where(lo, res[g * SUBLANES:(g + 1) * SUBLANES],
                                       res[(g + 1) * SUBLANES:(g + 2) * SUBLANES]))
        o = jnp.concatenate(pairs, axis=1)
        back = (bb * seq) % SUBLANES
        o = o if back == 0 else pltpu.roll(o, back, axis=0)
        rows = _row_iota(o.shape)
        sel = (rows >= bb * seq) & (rows < (bb + 1) * seq)
        out = jnp.where(sel, o, 0.0 if out is None else out)
    o_ref[...] = out


def _swa_sample(proj, k_cache, v_cache, bias, sink_rows, row0, n_batch, seq):
    assert SUBLANES % seq == 0
    n_bb = SUBLANES // seq
    n_cache = k_cache.shape[1]
    assert n_batch % n_bb == 0 and row0 % SUBLANES == 0 and n_cache % SUBLANES == 0
    n_keys = bias.shape[-1]
    rb0 = row0 // SUBLANES
    blk = lambda col: (lambda i: (rb0 + i, col))
    return pl.pallas_call(
        functools.partial(_swa_sample_kernel, seq=seq, n_bb=n_bb, n_cache=n_cache),
        grid=(n_batch // n_bb,),
        in_specs=[
            pl.BlockSpec((SUBLANES, SW_HEADS * SW_HD), blk(COL_SQ // (SW_HEADS * SW_HD))),
            pl.BlockSpec((SUBLANES, SW_KV_W), blk(COL_SK // SW_KV_W)),
            pl.BlockSpec((SUBLANES, SW_KV_W), blk(COL_SV // SW_KV_W)),
            pl.BlockSpec((n_bb, n_cache, SW_KV_W), lambda i: (i, 0, 0)),
            pl.BlockSpec((n_bb, n_cache, SW_KV_W), lambda i: (i, 0, 0)),
            pl.BlockSpec((SW_KV_HEADS, SW_GROUP * SUBLANES, n_keys), lambda i: (0, 0, 0)),
            pl.BlockSpec((SW_KV_HEADS, SW_GROUP * SUBLANES, n_keys), lambda i: (0, 0, 0)),
        ],
        out_specs=[
            pl.BlockSpec((SUBLANES, SW_HEADS * SW_HD), lambda i: (i, 0)),
            pl.BlockSpec((n_bb, n_cache, SW_KV_W), lambda i: (i, 0, 0)),
            pl.BlockSpec((n_bb, n_cache, SW_KV_W), lambda i: (i, 0, 0)),
        ],
        out_shape=[
            jax.ShapeDtypeStruct((n_batch * seq, SW_HEADS * SW_HD), F32),
            jax.ShapeDtypeStruct(k_cache.shape, F32),
            jax.ShapeDtypeStruct(v_cache.shape, F32),
        ],
        scratch_shapes=[pltpu.VMEM((n_keys, SW_KV_W), F32), pltpu.VMEM((n_keys, SW_KV_W), F32)],
        compiler_params=_cparams(("arbitrary",)),
        name="swa_sample",
    )(proj, proj, proj, k_cache, v_cache, bias, sink_rows)


def _mix_kernel(xp_ref, xs_ref, oap_ref, oas_ref, obp_ref, obs_ref, ga_ref, gb_ref, wo_ref, nw_ref, wr_ref, br_ref,
                x1_ref, h2_ref, route_ref, *, n_prompt_tiles):
    i = pl.program_id(0)

    def run(x_ref, oa_ref, ob_ref):
        mixed = _sigmoid(ga_ref[...]) * oa_ref[...] + _sigmoid(gb_ref[...]) * ob_ref[...]
        x1 = x_ref[...] + _dot(mixed, wo_ref[...])
        x1_ref[...] = x1
        h2 = x1 * lax.rsqrt(jnp.mean(x1 * x1, axis=-1, keepdims=True) + EPS) * nw_ref[...]
        _to_tiles(h2_ref, h2)
        logits = _dot_exact(h2, wr_ref[...]) + br_ref[...]
        lane = _lane_iota(logits.shape)
        lanef = lane.astype(F32)
        big = float(2 * LANES)
        is_g = lane < N_GROUPS
        gl = jnp.where(is_g, logits, -jnp.inf)
        gmax = jnp.max(gl, axis=-1, keepdims=True)
        gval = 1.0 / jnp.sum(jnp.where(is_g, jnp.exp(gl - gmax), 0.0), axis=-1, keepdims=True)
        grp = jnp.min(jnp.where(gl == gmax, lanef, big), axis=-1, keepdims=True)
        e_grp = ((lane - N_GROUPS) >> 3).astype(F32)
        is_e = (lane >= N_GROUPS) & (lane < N_GROUPS + N_EXPERTS) & (e_grp == grp)
        el = jnp.where(is_e, logits, -jnp.inf)
        v1 = jnp.max(el, axis=-1, keepdims=True)
        i1 = jnp.min(jnp.where(el == v1, lanef, big), axis=-1, keepdims=True)
        el2 = jnp.where(lanef == i1, -jnp.inf, el)
        v2 = jnp.max(el2, axis=-1, keepdims=True)
        i2 = jnp.min(jnp.where(el2 == v2, lanef, big), axis=-1, keepdims=True)
        e2 = jnp.exp(v2 - v1)
        w1 = gval / (1.0 + e2)
        w2 = gval * e2 / (1.0 + e2)
        route_ref[...] = jnp.where(lane == 0, i1 - N_GROUPS,
                                   jnp.where(lane == 1, i2 - N_GROUPS,
                                             jnp.where(lane == 2, w1, jnp.where(lane == 3, w2, 0.0))))

    @pl.when(i < n_prompt_tiles)
    def _():
        run(xp_ref, oap_ref, obp_ref)

    @pl.when(i >= n_prompt_tiles)
    def _():
        run(xs_ref, oas_ref, obs_ref)


def _mix(xp, xs, oa_p, oa_s, ob_p, ob_s, proj, w_out, norm_w, w_router, b_router):
    tp, ts = xp.shape[0], xs.shape[0]
    tm = _token_tile(tp, ts)
    npt, nst = tp // tm, ts // tm
    const = lambda i: (0, 0)
    row = lambda i: (i, 0)
    return pl.pallas_call(
        functools.partial(_mix_kernel, n_prompt_tiles=npt),
        grid=(npt + nst,),
        in_specs=[
            pl.BlockSpec((tm, D_MODEL), lambda i: (jnp.minimum(i, npt - 1), 0)),
            pl.BlockSpec((tm, D_MODEL), lambda i: (jnp.maximum(i - npt, 0), 0)),
            pl.BlockSpec((tm, D_MODEL), lambda i: (jnp.minimum(i, npt - 1), 0)),
            pl.BlockSpec((tm, D_MODEL), lambda i: (jnp.maximum(i - npt, 0), 0)),
            pl.BlockSpec((tm, D_MODEL), lambda i: (jnp.minimum(i, npt - 1), 0)),
            pl.BlockSpec((tm, D_MODEL), lambda i: (jnp.maximum(i - npt, 0), 0)),
            pl.BlockSpec((tm, D_MODEL), lambda i: (i, COL_GA // D_MODEL)),
            pl.BlockSpec((tm, D_MODEL), lambda i: (i, COL_GB // D_MODEL)),
            pl.BlockSpec((D_MODEL, D_MODEL), const),
            pl.BlockSpec((1, D_MODEL), const),
            pl.BlockSpec((D_MODEL, LANES), const),
            pl.BlockSpec((1, LANES), const),
        ],
        out_specs=[
            pl.BlockSpec((tm, D_MODEL), row),
            pl.BlockSpec((tm,) + TOK_TILE, lambda i: (i, 0, 0)),
            pl.BlockSpec((tm, LANES), row),
        ],
        out_shape=[
            jax.ShapeDtypeStruct((tp + ts, D_MODEL), F32),
            jax.ShapeDtypeStruct((tp + ts,) + TOK_TILE, F32),
            jax.ShapeDtypeStruct((tp + ts, LANES), F32),
        ],
        compiler_params=_cparams(("arbitrary",)),
        name="mix_router",
    )(xp, xs, oa_p, oa_s, ob_p, ob_s, proj, proj, w_out, norm_w, w_router, b_router)


def _rank_kernel(route_ref, dest_ref, meta_ref, rank_ref, cnt_ref, *, tile, blk):
    phase = pl.program_id(0)
    i = pl.program_id(1)
    shape = (tile, LANES)
    lane = _lane_iota(shape)
    lanef = lane.astype(F32)
    r = route_ref[...]
    oh0 = lanef == _col(r, 0)
    oh1 = lanef == _col(r, 1)
    rows = pl.ds(pl.multiple_of(i * tile, tile), tile)

    @pl.when(phase == 0)
    def _():
        @pl.when(i == 0)
        def _():
            cnt_ref[...] = jnp.zeros(cnt_ref.shape, F32)

        oh = jnp.where(oh0 | oh1, 1.0, 0.0)
        tri = jnp.where(_row_iota((tile, tile)) > _lane_iota((tile, tile)), 1.0, 0.0)
        before = _dot(tri, oh) + cnt_ref[0:1, :]
        rank0 = jnp.sum(jnp.where(oh0, before, 0.0), axis=-1, keepdims=True)
        rank1 = jnp.sum(jnp.where(oh1, before, 0.0), axis=-1, keepdims=True)
        rank_ref[rows, :] = jnp.where(lane == 0, rank0, jnp.where(lane == 1, rank1, 0.0))
        cnt_ref[0:1, :] = cnt_ref[0:1, :] + jnp.sum(oh, axis=0, keepdims=True)

    @pl.when(phase == 1)
    def _():
        cnt = cnt_ref[0:1, :]
        padded = jnp.floor((cnt + (blk - 1)) / blk) * blk
        before_lane = jnp.where(_row_iota((LANES, LANES)) < _lane_iota((LANES, LANES)), 1.0, 0.0)
        start = _dot_exact(jnp.broadcast_to(padded, (SUBLANES, LANES)), before_lane)[0:1, :]
        rk = rank_ref[rows, :]
        d0 = jnp.sum(jnp.where(oh0, start, 0.0), axis=-1, keepdims=True) + _col(rk, 0)
        d1 = jnp.sum(jnp.where(oh1, start, 0.0), axis=-1, keepdims=True) + _col(rk, 1)
        dest_ref[...] = jnp.where(lane == 0, d0, jnp.where(lane == 1, d1, 0.0)).astype(I32)

        @pl.when(i == 0)
        def _():
            end = start + padded
            mshape = meta_ref.shape
            blk_start = (_row_iota(mshape) * blk).astype(F32)
            hit = (_lane_iota(mshape) < N_EXPERTS) & (end <= blk_start)
            be = jnp.minimum(jnp.sum(jnp.where(hit, 1.0, 0.0), axis=-1, keepdims=True), N_EXPERTS - 1.0)
            n_used = _col(end, N_EXPERTS - 1) / blk
            ml = _lane_iota(mshape)
            meta_ref[...] = jnp.where(ml == 0, be, jnp.where(ml == 1, n_used, 0.0)).astype(I32)


def _rank(route, tile, blk, n_blocks):
    t = route.shape[0]
    nt = t // tile
    nbp = -(-n_blocks // SUBLANES) * SUBLANES
    return pl.pallas_call(
        functools.partial(_rank_kernel, tile=tile, blk=blk),
        grid=(2, nt),
        in_specs=[pl.BlockSpec((tile, LANES), lambda p, i: (i, 0))],
        out_specs=[
            pl.BlockSpec((tile, LANES), lambda p, i: (i * p, 0)),
            pl.BlockSpec((nbp, LANES), lambda p, i: (0, 0)),
        ],
        out_shape=[
            jax.ShapeDtypeStruct((t, LANES), I32),
            jax.ShapeDtypeStruct((nbp, LANES), I32),
        ],
        scratch_shapes=[pltpu.VMEM((t, LANES), F32), pltpu.VMEM((SUBLANES, LANES), F32)],
        compiler_params=_cparams(("arbitrary", "arbitrary")),
        name="moe_rank",
    )(route)


TOK_TILE = (D_MODEL // LANES, LANES)


def _to_tiles(ref, x):
    for j in range(TOK_TILE[0]):
        ref[:, j, :] = x[:, j * LANES:(j + 1) * LANES]


def _from_tiles(ref):
    return jnp.concatenate([ref[:, j, :] for j in range(TOK_TILE[0])], axis=1)


def _row_copy(src, src_row, dst, dst_row, sem):
    return pltpu.make_async_copy(src.at[src_row], dst.at[dst_row], sem)


def _invert_kernel(dest_ref, inv_ref):
    def clear(j, carry):
        inv_ref[j] = 0
        return carry

    def place(a, carry):
        inv_ref[dest_ref[a]] = a >> 1
        return carry

    lax.fori_loop(0, inv_ref.shape[0], clear, 0, unroll=8)
    lax.fori_loop(0, dest_ref.shape[0], place, 0, unroll=8)


def _invert(dest_flat, n_slots):
    return pl.pallas_call(
        _invert_kernel,
        in_specs=[pl.BlockSpec(memory_space=pltpu.SMEM)],
        out_specs=pl.BlockSpec(memory_space=pltpu.SMEM),
        out_shape=jax.ShapeDtypeStruct((n_slots,), I32),
        name="moe_invert",
    )(dest_flat)


def _last_used(i, nu_ref):
    return jnp.minimum(i, jnp.maximum(nu_ref[0] - 1, 0))


def _expert_kernel(be_ref, nu_ref, inv_ref, h2_ref, wg_ref, wu_ref, wd_ref, y_ref, xbuf_ref, wgu_ref, wdn_ref, sem,
                   *, blk):
    i = pl.program_id(0)
    n_used = nu_ref[0]
    used = i < n_used
    slot = i % 2
    blk_i = _last_used(i, nu_ref)
    fresh = (i == 0) | (be_ref[blk_i] != be_ref[jnp.maximum(blk_i - 1, 0)])

    def gather(block, into):
        def issue(t, carry):
            _row_copy(h2_ref, inv_ref[block * blk + t], xbuf_ref.at[into], t, sem.at[into]).start()
            return carry

        lax.fori_loop(0, blk, issue, 0, unroll=8)

    @pl.when(i == 0)
    def _():
        gather(0, 0)

    @pl.when(i + 1 < n_used)
    def _():
        gather(i + 1, 1 - slot)

    @pl.when(used & fresh)
    def _():
        wgu_ref[:, :D_EXPERT] = wg_ref[0].astype(BF16)
        wgu_ref[:, D_EXPERT:] = wu_ref[0].astype(BF16)
        wdn_ref[...] = wd_ref[0].astype(BF16)

    @pl.when(used)
    def _():
        def drain(t, carry):
            _row_copy(h2_ref, 0, xbuf_ref.at[slot], 0, sem.at[slot]).wait()
            return carry

        lax.fori_loop(0, blk, drain, 0, unroll=8)
        gu = jnp.dot(_from_tiles(xbuf_ref.at[slot]).astype(BF16), wgu_ref[...], preferred_element_type=F32)
        g = gu[:, :D_EXPERT]
        hidden = (g * _sigmoid(g) * gu[:, D_EXPERT:]).astype(BF16)
        _to_tiles(y_ref, jnp.dot(hidden, wdn_ref[...], preferred_element_type=F32))

    @pl.when(jnp.logical_not(used))
    def _():
        y_ref[...] = jnp.zeros(y_ref.shape, F32)


def _experts(block_expert, n_used, inv, h2, w_gate, w_up, w_down, blk):
    n_blocks = inv.shape[0] // blk
    wsel = lambda i, be, nu, iv: (be[_last_used(i, nu)], 0, 0)
    return pl.pallas_call(
        functools.partial(_expert_kernel, blk=blk),
        grid_spec=pltpu.PrefetchScalarGridSpec(
            num_scalar_prefetch=3,
            grid=(n_blocks,),
            in_specs=[
                pl.BlockSpec(memory_space=pl.ANY),
                pl.BlockSpec((1, D_MODEL, D_EXPERT), wsel),
                pl.BlockSpec((1, D_MODEL, D_EXPERT), wsel),
                pl.BlockSpec((1, D_EXPERT, D_MODEL), wsel),
            ],
            out_specs=pl.BlockSpec((blk,) + TOK_TILE, lambda i, be, nu, iv: (i, 0, 0)),
            scratch_shapes=[
                pltpu.VMEM((2, blk) + TOK_TILE, F32),
                pltpu.VMEM((D_MODEL, 2 * D_EXPERT), BF16),
                pltpu.VMEM((D_EXPERT, D_MODEL), BF16),
                pltpu.SemaphoreType.DMA((2,)),
            ],
        ),
        out_shape=jax.ShapeDtypeStruct((n_blocks * blk,) + TOK_TILE, F32),
        compiler_params=_cparams(("arbitrary",)),
        name="moe_experts",
    )(block_expert, n_used, inv, h2, w_gate, w_up, w_down)


def _combine_kernel(dest_ref, x1_ref, route_ref, nw_ref, ys_ref, yp_ref, ysm_ref, ybuf_ref, sem,
                    *, tile, n_prompt_tiles):
    i = pl.program_id(0)
    slot = i % 2

    def gather(step, into):
        def issue(t, carry):
            for k in range(2):
                _row_copy(ys_ref, dest_ref[2 * (step * tile + t) + k], ybuf_ref.at[into], k * tile + t,
                          sem.at[into]).start()
            return carry

        lax.fori_loop(0, tile, issue, 0, unroll=4)

    @pl.when(i == 0)
    def _():
        gather(0, 0)

    @pl.when(i + 1 < pl.num_programs(0))
    def _():
        gather(i + 1, 1 - slot)

    def drain(t, carry):
        _row_copy(ys_ref, 0, ybuf_ref.at[slot], 0, sem.at[slot]).wait()
        return carry

    lax.fori_loop(0, 2 * tile, drain, 0, unroll=8)
    r = route_ref[...]
    ybuf = ybuf_ref.at[slot]
    y = (_from_tiles(ybuf.at[pl.ds(0, tile)]) * _col(r, 2)
         + _from_tiles(ybuf.at[pl.ds(tile, tile)]) * _col(r, 3))
    x2 = x1_ref[...] + y
    out = x2 * lax.rsqrt(jnp.mean(x2 * x2, axis=-1, keepdims=True) + EPS) * nw_ref[...]

    @pl.when(i < n_prompt_tiles)
    def _():
        yp_ref[...] = out

    @pl.when(i >= n_prompt_tiles)
    def _():
        ysm_ref[...] = out


def _combine(dest_flat, x1, route, norm_w, ys, tp, ts):
    tile = _token_tile(tp, ts)
    npt, nst = tp // tile, ts // tile
    return pl.pallas_call(
        functools.partial(_combine_kernel, tile=tile, n_prompt_tiles=npt),
        grid_spec=pltpu.PrefetchScalarGridSpec(
            num_scalar_prefetch=1,
            grid=(npt + nst,),
            in_specs=[
                pl.BlockSpec((tile, D_MODEL), lambda i, d: (i, 0)),
                pl.BlockSpec((tile, LANES), lambda i, d: (i, 0)),
                pl.BlockSpec((1, D_MODEL), lambda i, d: (0, 0)),
                pl.BlockSpec(memory_space=pl.ANY),
            ],
            out_specs=[
                pl.BlockSpec((tile, D_MODEL), lambda i, d: (jnp.minimum(i, npt - 1), 0)),
                pl.BlockSpec((tile, D_MODEL), lambda i, d: (jnp.maximum(i - npt, 0), 0)),
            ],
            scratch_shapes=[pltpu.VMEM((2, 2 * tile) + TOK_TILE, F32), pltpu.SemaphoreType.DMA((2,))],
        ),
        out_shape=[
            jax.ShapeDtypeStruct((tp, D_MODEL), F32),
            jax.ShapeDtypeStruct((ts, D_MODEL), F32),
        ],
        compiler_params=_cparams(("arbitrary",)),
        name="moe_combine",
    )(dest_flat, x1, route, norm_w, ys)


def _layer(xp, xs, n_batch, seq, s_batch, s_seq, conv_state, dn_state, k_cache, v_cache,
           w_in, conv_w, a_log, dt_bias, dn_norm_w, sinks, rel_bias, w_out, norm_mix_w, norm_ffn_w,
           w_rg, b_rg, w_re, b_re, w_gate, w_up, w_down, norm_final_w):
    tp, ts = xp.shape[0], xs.shape[0]
    t_all = tp + ts
    row = lambda v: v.reshape(1, -1).astype(F32)

    o = np.cumsum((0, DN_QK_W, DN_QK_W, DN_V_W, DN_V_W, DN_HEADS, DN_HEADS, SW_HEADS * SW_HD, SW_KV_W, SW_KV_W,
                   D_MODEL, D_MODEL)).tolist()
    w_big = jnp.concatenate([w_in[:, o[0]:o[4]], w_in[:, o[6]:o[7]], w_in[:, o[9]:o[11]], w_in[:, o[7]:o[9]]],
                            axis=1).astype(BF16)
    w_small = jnp.pad(w_in[:, o[4]:o[6]], ((0, 0), (0, LANES - 2 * DN_HEADS))).astype(BF16)
    head_row = lambda v: jnp.pad(v.astype(F32), (DN_HEADS, LANES - 2 * DN_HEADS)).reshape(1, LANES)
    w_router = jnp.pad(jnp.concatenate([w_rg, w_re], axis=1).astype(F32),
                       ((0, 0), (0, LANES - N_GROUPS - N_EXPERTS)))
    b_router = jnp.pad(jnp.concatenate([b_rg, b_re]).astype(F32), (0, LANES - N_GROUPS - N_EXPERTS)).reshape(1, LANES)

    conv0 = jnp.zeros((n_batch, DN_CONV - 1, DN_CONV_W), F32)
    proj, ba, conv_tail = _inproj(xp, xs, row(norm_mix_w), w_big, w_small, conv_w.astype(F32), conv0, seq)
    p_conv = conv_tail[:, SUBLANES - (DN_CONV - 1):, :]

    dn0 = jnp.zeros((n_batch, DN_HEADS, DN_DK, DN_DV), F32)
    oa_p, p_dn = _dn_prompt(proj, ba, head_row(a_log), head_row(dt_bias), row(dn_norm_w), dn0, n_batch, seq)
    oa_s, s_conv, s_dn = _dn_sample(proj, ba, conv_w.astype(F32), head_row(a_log), head_row(dt_bias), row(dn_norm_w),
                                    conv_state, dn_state, tp, s_batch, s_seq)

    qpos = jnp.arange(SW_BLOCK)[:, None]
    kpos = jnp.arange(2 * SW_BLOCK)[None, :] - SW_BLOCK
    bias_p = _relbias(rel_bias.astype(F32), _rel_bucket(qpos - kpos))
    ob_p, p_k, p_v = _swa_prompt(proj, sinks.astype(F32), bias_p, n_batch, seq)
    n_cache = k_cache.shape[1]
    n_keys = -(-(n_cache + SUBLANES) // LANES) * LANES
    tq = n_cache + jnp.arange(SUBLANES)[:, None]
    bias_s = _relbias(rel_bias.astype(F32), _rel_bucket(tq - jnp.arange(n_keys)[None, :]))
    bias_s = bias_s.reshape(SW_KV_HEADS, SW_GROUP * SUBLANES, n_keys)
    sink_rows = jnp.broadcast_to(jnp.repeat(sinks.astype(F32).reshape(SW_KV_HEADS, SW_GROUP), SUBLANES, axis=1)[:, :, None],
                                 (SW_KV_HEADS, SW_GROUP * SUBLANES, n_keys))
    ob_s, s_k, s_v = _swa_sample(proj, k_cache.reshape(s_batch, n_cache, SW_KV_W), v_cache.reshape(s_batch, n_cache, SW_KV_W),
                                 bias_s, sink_rows, tp, s_batch, s_seq)

    x1, h2, route = _mix(xp, xs, oa_p, oa_s, ob_p, ob_s, proj, w_out.astype(BF16), row(norm_ffn_w), w_router, b_router)

    tile = _token_tile(tp, ts)
    n_blocks = -(-2 * t_all // MOE_BLOCK) + N_EXPERTS
    dest, meta = _rank(route, tile, MOE_BLOCK, n_blocks)
    dest_flat = dest[:, :2].reshape(-1)
    block_expert = meta[:n_blocks, 0]
    n_used = meta[0:1, 1]
    inv = _invert(dest_flat, n_blocks * MOE_BLOCK)
    ys = _experts(block_expert, n_used, inv, h2, w_gate, w_up, w_down, MOE_BLOCK)
    y_p, y_s = _combine(dest_flat, x1, route, row(norm_final_w), ys, tp, ts)

    kv_shape = (n_batch, WINDOW, SW_KV_HEADS, SW_HD)
    return (y_p, y_s, p_conv, p_dn, p_k.reshape(kv_shape), p_v.reshape(kv_shape), s_conv, s_dn,
            s_k.reshape(k_cache.shape), s_v.reshape(v_cache.shape))


def kernel(x_prompt, x_sample, state_dn_conv, state_dn, cache_swa_k, cache_swa_v, w_in, conv_w, a_log, dt_bias, dn_norm_w, sinks, rel_bias, w_out, norm_mix_w, norm_ffn_w, w_router_group, b_router_group, w_router_expert, b_router_expert, w_gate, w_up, w_down, norm_final_w):
    depth = w_in.shape[0]
    assert depth == 1, "the final-norm fusion below assumes a single layer"
    n_batch, seq, _ = x_prompt.shape
    s_batch, s_seq, _ = x_sample.shape
    outs = _layer(x_prompt.reshape(-1, D_MODEL), x_sample.reshape(-1, D_MODEL), n_batch, seq, s_batch, s_seq,
                  state_dn_conv[0], state_dn[0], cache_swa_k[0], cache_swa_v[0],
                  w_in[0], conv_w[0], a_log[0], dt_bias[0], dn_norm_w[0], sinks[0], rel_bias,
                  w_out[0], norm_mix_w[0], norm_ffn_w[0], w_router_group[0], b_router_group[0],
                  w_router_expert[0], b_router_expert[0], w_gate[0], w_up[0], w_down[0], norm_final_w)
    y_p, y_s, p_conv, p_dn, p_k, p_v, s_conv, s_dn, s_k, s_v = outs
    return (y_p.reshape(x_prompt.shape), y_s.reshape(x_sample.shape), p_conv[None], p_dn[None], p_k[None], p_v[None],
            s_conv[None], s_dn[None], s_k[None], s_v[None])
```

```python
import functools
import math

import jax
import jax.numpy as jnp
import numpy as np
from jax import lax
from jax.experimental import pallas as pl
from jax.experimental.pallas import tpu as pltpu

F32 = jnp.float32
BF16 = jnp.bfloat16
I32 = jnp.int32

D_MODEL = 1024
DN_HEADS = 8
DN_DK = 128
DN_DV = 128
DN_CONV = 4
DN_CHUNK = 64
DN_QK_W = DN_HEADS * DN_DK
DN_V_W = DN_HEADS * DN_DV
DN_CONV_W = 2 * DN_QK_W + DN_V_W
SW_HEADS = 16
SW_KV_HEADS = 2
SW_GROUP = SW_HEADS // SW_KV_HEADS
SW_HD = 64
SW_KV_W = SW_KV_HEADS * SW_HD
WINDOW = 128
SW_BLOCK = 128
REL_BUCKETS = 32
REL_MAX_DIST = 128
N_GROUPS = 8
EXP_PER_GROUP = 8
N_EXPERTS = N_GROUPS * EXP_PER_GROUP
D_EXPERT = 256
MOE_BLOCK = 256
EPS = 1e-6

LANES = 128
SUBLANES = 8
VMEM_LIMIT = 56 * 1024 * 1024

COL_QKV = 0
COL_Z = 3072
COL_SQ = 4096
COL_GA = 5120
COL_GB = 6144
COL_SK = 7168
COL_SV = 7296
PROJ_W = 7424
PROJ_CHUNK = 512


def _cparams(sem):
    return pltpu.CompilerParams(dimension_semantics=sem, vmem_limit_bytes=VMEM_LIMIT)


def _sigmoid(x):
    return 0.5 * jnp.tanh(0.5 * x) + 0.5


def _dot(a, b):
    return jnp.dot(a.astype(BF16), b.astype(BF16), preferred_element_type=F32)


def _dot_nt(a, b):
    return lax.dot_general(a.astype(BF16), b.astype(BF16), (((1,), (1,)), ((), ())), preferred_element_type=F32)


def _dot_tn(a, b):
    return lax.dot_general(a.astype(BF16), b.astype(BF16), (((0,), (0,)), ((), ())), preferred_element_type=F32)


def _dot_exact(a, b):
    return jnp.dot(a, b, precision=lax.Precision.HIGHEST, preferred_element_type=F32)


def _lane_iota(shape):
    return lax.broadcasted_iota(I32, shape, len(shape) - 1)


def _row_iota(shape):
    return lax.broadcasted_iota(I32, shape, len(shape) - 2)


def _col(x, j):
    return jnp.sum(jnp.where(_lane_iota(x.shape) == j, x, 0.0), axis=-1, keepdims=True)


def _token_tile(*sizes):
    for t in (256, 128, 64, 32, 16, 8):
        if all(s % t == 0 for s in sizes):
            return t
    raise ValueError(f"token counts {sizes} need a common tile that is a multiple of 8")


def _inproj_kernel(xp_ref, xs_ref, nw_ref, wb_ref, ws_ref, convw_ref, conv0_ref, proj_ref, ba_ref, tail_ref, cbuf_ref,
                   *, n_prompt_tiles, tiles_per_seq):
    i = pl.program_id(0)
    tm = xp_ref.shape[0]
    hist = SUBLANES - (DN_CONV - 1)

    def project(x_ref, conv):
        x = x_ref[...]
        h = (x * lax.rsqrt(jnp.mean(x * x, axis=-1, keepdims=True) + EPS) * nw_ref[...]).astype(BF16)
        ba_ref[...] = jnp.dot(h, ws_ref[...], preferred_element_type=F32)
        top = _row_iota((SUBLANES, PROJ_CHUNK))
        for c0 in range(0, PROJ_W, PROJ_CHUNK):
            c1 = min(c0 + PROJ_CHUNK, PROJ_W)
            cur = jnp.dot(h, wb_ref[:, c0:c1], preferred_element_type=F32)
            if conv and c1 <= COL_QKV + DN_CONV_W:
                prev = cbuf_ref[:, c0:c1]
                acc = cur * convw_ref[DN_CONV - 1:DN_CONV, c0:c1]
                for s in range(1, DN_CONV):
                    sh = pltpu.roll(cur, s, axis=0)
                    head = jnp.where(top < s, pltpu.roll(prev, s, axis=0), sh[:SUBLANES])
                    sh = jnp.concatenate([head, sh[SUBLANES:]], axis=0)
                    acc = acc + sh * convw_ref[DN_CONV - 1 - s:DN_CONV - s, c0:c1]
                cbuf_ref[:, c0:c1] = cur[tm - SUBLANES:]
                cur = acc * _sigmoid(acc)
            proj_ref[:, c0:c1] = cur

    @pl.when(i < n_prompt_tiles)
    def _():
        @pl.when(i % tiles_per_seq == 0)
        def _():
            cbuf_ref[...] = jnp.zeros(cbuf_ref.shape, F32)
            cbuf_ref[hist:SUBLANES, :] = conv0_ref[0]

        project(xp_ref, True)
        tail_ref[0] = cbuf_ref[...]

    @pl.when(i >= n_prompt_tiles)
    def _():
        project(xs_ref, False)


def _inproj(xp, xs, norm_w, w_big, w_small, conv_w, conv0, seq):
    tp, ts = xp.shape[0], xs.shape[0]
    tm = _token_tile(tp, ts, seq)
    assert COL_QKV == 0 and DN_CONV_W % PROJ_CHUNK == 0
    npt, nst = tp // tm, ts // tm
    tps = seq // tm
    const = lambda i: (0, 0)
    seq_of = lambda i: (jnp.minimum(i, npt - 1) // tps, 0, 0)
    return pl.pallas_call(
        functools.partial(_inproj_kernel, n_prompt_tiles=npt, tiles_per_seq=tps),
        grid=(npt + nst,),
        in_specs=[
            pl.BlockSpec((tm, D_MODEL), lambda i: (jnp.minimum(i, npt - 1), 0)),
            pl.BlockSpec((tm, D_MODEL), lambda i: (jnp.maximum(i - npt, 0), 0)),
            pl.BlockSpec((1, D_MODEL), const),
            pl.BlockSpec((D_MODEL, PROJ_W), const, pipeline_mode=pl.Buffered(1)),
            pl.BlockSpec((D_MODEL, LANES), const),
            pl.BlockSpec((DN_CONV, DN_CONV_W), const),
            pl.BlockSpec((1, DN_CONV - 1, DN_CONV_W), seq_of),
        ],
        out_specs=[
            pl.BlockSpec((tm, PROJ_W), lambda i: (i, 0)),
            pl.BlockSpec((tm, LANES), lambda i: (i, 0)),
            pl.BlockSpec((1, SUBLANES, DN_CONV_W), seq_of),
        ],
        out_shape=[
            jax.ShapeDtypeStruct((tp + ts, PROJ_W), F32),
            jax.ShapeDtypeStruct((tp + ts, LANES), F32),
            jax.ShapeDtypeStruct((tp // seq, SUBLANES, DN_CONV_W), F32),
        ],
        scratch_shapes=[pltpu.VMEM((SUBLANES, DN_CONV_W), F32)],
        compiler_params=_cparams(("arbitrary",)),
        name="inproj",
    )(xp, xs, norm_w, w_big, w_small, conv_w, conv0)


def _dn_core(groups, alog, dtb, nw, read_state, write_state, n_seg, seg_valid):
    rows = groups[0][0].shape[0]
    sr = rows // n_seg
    assert sr * n_seg == rows and sr & (sr - 1) == 0 and rows <= LANES
    seg_shift = sr.bit_length() - 1
    ri = _row_iota((rows, rows))
    ci = _lane_iota((rows, rows))
    incl = ri >= ci
    strict = ri > ci
    if n_seg > 1:
        same = (ri >> seg_shift) == (ci >> seg_shift)
        incl = incl & same
        strict = strict & same
    l_incl = incl.astype(F32)
    eye = (ri == ci).astype(F32)
    levels = max(1, math.ceil(math.log2(seg_valid)))

    beta_all, gsum_all, gtot_all, gsum_t = [], [], [], []
    for _, _, ba in groups:
        b_all = _sigmoid(ba)
        sp = ba + dtb
        softplus = jnp.maximum(sp, 0.0) + jnp.log1p(jnp.exp(-jnp.abs(sp)))
        g_all = -jnp.exp(alog) * softplus
        if seg_valid < sr:
            live = (_row_iota((rows, LANES)) & (sr - 1)) < seg_valid
            b_all = jnp.where(live, b_all, 0.0)
            g_all = jnp.where(live, g_all, 0.0)
        gs = _dot_exact(l_incl, g_all)
        beta_all.append(b_all)
        gsum_all.append(gs)
        gtot_all.append(_dot_exact(same.astype(F32), g_all) if n_seg > 1 else gs[rows - 1:rows, :])
        padded = gs if rows == LANES else jnp.concatenate([gs, jnp.zeros((LANES - rows, LANES), F32)], axis=0)
        gsum_t.append(padded.T)

    probs = [(g, h) for g in range(len(groups)) for h in range(DN_HEADS)]
    segs = range(n_seg)
    q, k, v, kb, beta, gsum, gtot = {}, {}, {}, {}, {}, {}, {}
    for p in probs:
        g, h = p
        qkv = groups[g][0]
        qh = qkv[:, h * DN_DK:(h + 1) * DN_DK]
        kh = qkv[:, DN_QK_W + h * DN_DK:DN_QK_W + (h + 1) * DN_DK]
        v[p] = qkv[:, 2 * DN_QK_W + h * DN_DV:2 * DN_QK_W + (h + 1) * DN_DV]
        q[p] = qh * lax.rsqrt(jnp.sum(qh * qh, axis=-1, keepdims=True) + 1e-6) * (DN_DK ** -0.5)
        k[p] = kh * lax.rsqrt(jnp.sum(kh * kh, axis=-1, keepdims=True) + 1e-6)
        beta[p] = _col(beta_all[g], h)
        gsum[p] = _col(gsum_all[g], DN_HEADS + h)
        gtot[p] = _col(gtot_all[g], DN_HEADS + h)
        kb[p] = k[p] * beta[p]
    kq = {p: _dot_nt(jnp.concatenate([kb[p], q[p]], axis=0), k[p]) for p in probs}
    gamma = {(g, h): jnp.exp(jnp.where(incl, gsum[(g, h)] - gsum_t[g][DN_HEADS + h:DN_HEADS + h + 1, :rows], -jnp.inf))
             for g, h in probs}
    attn = {p: kq[p][rows:] * gamma[p] for p in probs}
    pw = {p: -jnp.where(strict, kq[p][:rows] * gamma[p], 0.0) for p in probs}
    t = {p: eye + pw[p] for p in probs}
    for _ in range(1, levels):
        pw = {p: _dot(pw[p], pw[p]) for p in probs}
        t = {p: t[p] + _dot(t[p], pw[p]) for p in probs}
    eg = {p: jnp.exp(gsum[p]) for p in probs}
    uw = {p: _dot(t[p], jnp.concatenate([v[p] * beta[p], kb[p] * eg[p]], axis=1)) for p in probs}
    qg = {p: q[p] * eg[p] for p in probs}
    state = {(p, s): read_state(p[0], s, p[1]) for p in probs for s in segs}
    wq = {(p, s): _dot(jnp.concatenate([uw[p][s * sr:(s + 1) * sr, DN_DV:], qg[p][s * sr:(s + 1) * sr]], axis=0),
                       state[(p, s)]) for p in probs for s in segs}
    join = lambda pieces: pieces[0] if len(pieces) == 1 else jnp.concatenate(pieces, axis=0)
    v_new = {p: uw[p][:, :DN_DV] - join([wq[(p, s)][:sr] for s in segs]) for p in probs}
    o = {p: join([wq[(p, s)][sr:] for s in segs]) + _dot(attn[p], v_new[p]) for p in probs}
    kd = {p: k[p] * jnp.exp(gtot[p] - gsum[p]) for p in probs}
    for p in probs:
        for s in segs:
            r0 = s * sr if n_seg > 1 else 0
            decay = jnp.exp(gtot[p][r0:r0 + 1, :])
            write_state(p[0], s, p[1],
                        state[(p, s)] * decay + _dot_tn(kd[p][s * sr:(s + 1) * sr], v_new[p][s * sr:(s + 1) * sr]))
    outs = []
    for g, (_, z, _) in enumerate(groups):
        heads = []
        for h in range(DN_HEADS):
            oh = o[(g, h)]
            zz = z[:, h * DN_DV:(h + 1) * DN_DV]
            on = oh * lax.rsqrt(jnp.mean(oh * oh, axis=-1, keepdims=True) + EPS) * nw
            heads.append(on * (zz * _sigmoid(zz)))
        outs.append(jnp.concatenate(heads, axis=1))
    return outs


def _dn_prompt_kernel(*refs, chunk, n_batch):
    nb = n_batch
    qkv_refs, z_refs, ba_refs = refs[0:nb], refs[nb:2 * nb], refs[2 * nb:3 * nb]
    alog_ref, dtb_ref, nw_ref, s0_ref, o_ref, sout_ref = refs[3 * nb:]

    @pl.when(pl.program_id(0) == 0)
    def _():
        sout_ref[...] = s0_ref[...]

    groups = [(qkv_refs[b][...], z_refs[b][...], ba_refs[b][...]) for b in range(nb)]

    def read_state(g, s, h):
        return sout_ref[g, h]

    def write_state(g, s, h, val):
        sout_ref[g, h] = val

    outs = _dn_core(groups, alog_ref[...], dtb_ref[...], nw_ref[...], read_state, write_state, 1, chunk)
    for b in range(nb):
        o_ref[b] = outs[b]


def _dn_prompt(proj, ba, alog_row, dtb_row, dn_nw, s0, n_batch, seq):
    chunk = min(DN_CHUNK, seq)
    assert seq % chunk == 0 and chunk % SUBLANES == 0
    nc = seq // chunk
    const2 = lambda c: (0, 0)
    rows = lambda b, col: (lambda c: (b * nc + c, col))
    batches = range(n_batch)
    o, s_out = pl.pallas_call(
        functools.partial(_dn_prompt_kernel, chunk=chunk, n_batch=n_batch),
        grid=(nc,),
        in_specs=(
            [pl.BlockSpec((chunk, DN_CONV_W), rows(b, COL_QKV // DN_CONV_W)) for b in batches]
            + [pl.BlockSpec((chunk, DN_V_W), rows(b, COL_Z // DN_V_W)) for b in batches]
            + [pl.BlockSpec((chunk, LANES), rows(b, 0)) for b in batches]
            + [
                pl.BlockSpec((1, LANES), const2),
                pl.BlockSpec((1, LANES), const2),
                pl.BlockSpec((1, DN_DV), const2),
                pl.BlockSpec((n_batch, DN_HEADS, DN_DK, DN_DV), lambda c: (0, 0, 0, 0)),
            ]
        ),
        out_specs=[
            pl.BlockSpec((n_batch, chunk, DN_V_W), lambda c: (0, c, 0)),
            pl.BlockSpec((n_batch, DN_HEADS, DN_DK, DN_DV), lambda c: (0, 0, 0, 0)),
        ],
        out_shape=[
            jax.ShapeDtypeStruct((n_batch, seq, DN_V_W), F32),
            jax.ShapeDtypeStruct((n_batch, DN_HEADS, DN_DK, DN_DV), F32),
        ],
        compiler_params=_cparams(("arbitrary",)),
        name="dn_prompt",
    )(*([proj] * n_batch), *([proj] * n_batch), *([ba] * n_batch), alog_row, dtb_row, dn_nw, s0)
    return o.reshape(n_batch * seq, DN_V_W), s_out


def _dn_sample_kernel(qkv_ref, z_ref, ba_ref, convw_ref, alog_ref, dtb_ref, nw_ref, conv0_ref, s0_ref,
                      o_ref, convout_ref, sout_ref, cbuf_ref, *, seq, n_bb):
    hist = SUBLANES - (DN_CONV - 1)
    per = SUBLANES // seq
    n_tiles = n_bb // per

    def spread(ref):
        pieces = []
        for j in range(n_tiles):
            x8 = ref[j * SUBLANES:(j + 1) * SUBLANES, :]
            for r in range(per):
                pieces.append(x8 if r == 0 else pltpu.roll(x8, SUBLANES - r * seq, axis=0))
        return pieces

    for bb, piece in enumerate(spread(qkv_ref)):
        cbuf_ref[bb, SUBLANES:2 * SUBLANES, :] = piece
    cbuf_ref[:, hist:SUBLANES, :] = conv0_ref[...]
    w = convw_ref[...]
    acc = cbuf_ref[:, hist:hist + SUBLANES, :] * w[0:1, :]
    for i in range(1, DN_CONV):
        acc = acc + cbuf_ref[:, hist + i:hist + i + SUBLANES, :] * w[i:i + 1, :]
    live = _row_iota(acc.shape) < seq
    qkv = jnp.where(live, acc * _sigmoid(acc), 0.0).reshape(n_bb * SUBLANES, DN_CONV_W)
    convout_ref[...] = cbuf_ref[:, SUBLANES + seq - (DN_CONV - 1):SUBLANES + seq, :]

    def read_state(g, s, h):
        return s0_ref[s, h]

    def write_state(g, s, h, val):
        sout_ref[s, h] = val

    group = (qkv, jnp.concatenate(spread(z_ref), axis=0), jnp.concatenate(spread(ba_ref), axis=0))
    o = _dn_core([group], alog_ref[...], dtb_ref[...], nw_ref[...], read_state, write_state, n_bb, seq)[0]
    rows = _row_iota((SUBLANES, DN_V_W))
    for j in range(n_tiles):
        tile = None
        for r in range(per):
            bb = j * per + r
            piece = o[bb * SUBLANES:(bb + 1) * SUBLANES]
            piece = piece if r == 0 else pltpu.roll(piece, r * seq, axis=0)
            tile = piece if tile is None else jnp.where(rows >= r * seq, piece, tile)
        o_ref[j * SUBLANES:(j + 1) * SUBLANES, :] = tile


def _dn_sample(proj, ba, conv_w, alog_row, dtb_row, dn_nw, conv0, s0, row0, n_batch, seq):
    assert SUBLANES % seq == 0 and seq >= DN_CONV - 1
    n_bb = SUBLANES
    rows_in = n_bb * seq
    assert n_batch % n_bb == 0 and row0 % rows_in == 0
    rb0 = row0 // rows_in
    const1 = lambda i: (0, 0)
    return pl.pallas_call(
        functools.partial(_dn_sample_kernel, seq=seq, n_bb=n_bb),
        grid=(n_batch // n_bb,),
        in_specs=[
            pl.BlockSpec((rows_in, DN_CONV_W), lambda i: (rb0 + i, COL_QKV // DN_CONV_W)),
            pl.BlockSpec((rows_in, DN_V_W), lambda i: (rb0 + i, COL_Z // DN_V_W)),
            pl.BlockSpec((rows_in, LANES), lambda i: (rb0 + i, 0)),
            pl.BlockSpec((DN_CONV, DN_CONV_W), const1),
            pl.BlockSpec((1, LANES), const1),
            pl.BlockSpec((1, LANES), const1),
            pl.BlockSpec((1, DN_DV), const1),
            pl.BlockSpec((n_bb, DN_CONV - 1, DN_CONV_W), lambda i: (i, 0, 0)),
            pl.BlockSpec((n_bb, DN_HEADS, DN_DK, DN_DV), lambda i: (i, 0, 0, 0)),
        ],
        out_specs=[
            pl.BlockSpec((rows_in, DN_V_W), lambda i: (i, 0)),
            pl.BlockSpec((n_bb, DN_CONV - 1, DN_CONV_W), lambda i: (i, 0, 0)),
            pl.BlockSpec((n_bb, DN_HEADS, DN_DK, DN_DV), lambda i: (i, 0, 0, 0)),
        ],
        out_shape=[
            jax.ShapeDtypeStruct((n_batch * seq, DN_V_W), F32),
            jax.ShapeDtypeStruct((n_batch, DN_CONV - 1, DN_CONV_W), F32),
            jax.ShapeDtypeStruct((n_batch, DN_HEADS, DN_DK, DN_DV), F32),
        ],
        scratch_shapes=[pltpu.VMEM((n_bb, 2 * SUBLANES, DN_CONV_W), F32)],
        compiler_params=_cparams(("arbitrary",)),
        name="dn_sample",
    )(proj, proj, ba, conv_w, alog_row, dtb_row, dn_nw, conv0, s0)


def _rel_bucket(dist):
    n = jnp.maximum(dist, 0)
    max_exact = REL_BUCKETS // 2
    large = max_exact + (jnp.log(jnp.maximum(n, 1).astype(F32) / max_exact)
                         / math.log(REL_MAX_DIST / max_exact) * (REL_BUCKETS - max_exact)).astype(I32)
    return jnp.where(n < max_exact, n, jnp.minimum(large, REL_BUCKETS - 1))


def _relbias_kernel(tab_ref, bucket_ref, o_ref):
    h = pl.program_id(0)
    bk = bucket_ref[...]
    acc = jnp.zeros(bk.shape, F32)
    for b in range(REL_BUCKETS):
        acc = jnp.where(bk == b, tab_ref[b * SW_HEADS + h], acc)
    o_ref[0] = acc


def _relbias(rel_table, bucket):
    nq, ns = bucket.shape
    return pl.pallas_call(
        _relbias_kernel,
        grid=(SW_HEADS,),
        in_specs=[
            pl.BlockSpec(memory_space=pltpu.SMEM),
            pl.BlockSpec((nq, ns), lambda h: (0, 0)),
        ],
        out_specs=pl.BlockSpec((1, nq, ns), lambda h: (h, 0, 0)),
        out_shape=jax.ShapeDtypeStruct((SW_HEADS, nq, ns), F32),
        compiler_params=_cparams(("arbitrary",)),
        name="relbias",
    )(rel_table.reshape(-1), bucket)


def _dup_halves(x):
    lo = _lane_iota(x.shape) < SW_HD
    xr = pltpu.roll(x, SW_HD, axis=1)
    return jnp.where(lo, x, xr).astype(BF16), jnp.where(lo, xr, x).astype(BF16)


def _sink_softmax_pv(s, valid, sink, vv):
    s = jnp.where(valid, s, -jnp.inf)
    m = jnp.maximum(jnp.max(s, axis=-1, keepdims=True), sink)
    p = jnp.exp(s - m)
    p = p / (jnp.sum(p, axis=-1, keepdims=True) + jnp.exp(sink - m))
    return _dot(p, vv)


def _swa_prompt_kernel(sinks_ref, q_ref, kc_ref, kp_ref, vc_ref, vp_ref, bias_ref, o_ref, klast_ref, vlast_ref):
    i = pl.program_id(1)
    klast_ref[0] = kc_ref[...]
    vlast_ref[0] = vc_ref[...]
    kk = _dup_halves(jnp.concatenate([kp_ref[...], kc_ref[...]], axis=0))
    vv = _dup_halves(jnp.concatenate([vp_ref[...], vc_ref[...]], axis=0))
    shape = (SW_BLOCK, 2 * SW_BLOCK)
    qi = _row_iota(shape)
    kj = _lane_iota(shape)
    dist = qi - kj + SW_BLOCK
    first_key = jnp.where(i > 0, 0, SW_BLOCK)
    valid = (dist >= 0) & (dist < WINDOW) & (kj >= first_key)
    lo = _lane_iota((SW_BLOCK, LANES)) < SW_HD
    for pair in range(SW_HEADS // 2):
        qp = q_ref[:, pair * LANES:(pair + 1) * LANES]
        outs = []
        for half in range(2):
            hq = 2 * pair + half
            kv = hq // SW_GROUP
            qm = jnp.where(lo if half == 0 else ~lo, qp, 0.0)
            s = _dot_nt(qm, kk[kv]) * (SW_HD ** -0.5) + bias_ref[hq]
            outs.append(_sink_softmax_pv(s, valid, sinks_ref[hq], vv[kv]))
        o_ref[:, pair * LANES:(pair + 1) * LANES] = jnp.where(lo, outs[0], outs[1])


def _swa_prompt(proj, sinks, bias, n_batch, seq):
    assert seq % SW_BLOCK == 0 and WINDOW == SW_BLOCK
    nb = seq // SW_BLOCK
    cur = lambda col: (lambda b, i: (b * nb + i, col))
    prev = lambda col: (lambda b, i: (b * nb + jnp.maximum(i - 1, 0), col))
    return pl.pallas_call(
        _swa_prompt_kernel,
        grid=(n_batch, nb),
        in_specs=[
            pl.BlockSpec(memory_space=pltpu.SMEM),
            pl.BlockSpec((SW_BLOCK, SW_HEADS * SW_HD), cur(COL_SQ // (SW_HEADS * SW_HD))),
            pl.BlockSpec((SW_BLOCK, SW_KV_W), cur(COL_SK // SW_KV_W)),
            pl.BlockSpec((SW_BLOCK, SW_KV_W), prev(COL_SK // SW_KV_W)),
            pl.BlockSpec((SW_BLOCK, SW_KV_W), cur(COL_SV // SW_KV_W)),
            pl.BlockSpec((SW_BLOCK, SW_KV_W), prev(COL_SV // SW_KV_W)),
            pl.BlockSpec((SW_HEADS, SW_BLOCK, 2 * SW_BLOCK), lambda b, i: (0, 0, 0)),
        ],
        out_specs=[
            pl.BlockSpec((SW_BLOCK, SW_HEADS * SW_HD), lambda b, i: (b * nb + i, 0)),
            pl.BlockSpec((1, SW_BLOCK, SW_KV_W), lambda b, i: (b, 0, 0)),
            pl.BlockSpec((1, SW_BLOCK, SW_KV_W), lambda b, i: (b, 0, 0)),
        ],
        out_shape=[
            jax.ShapeDtypeStruct((n_batch * seq, SW_HEADS * SW_HD), F32),
            jax.ShapeDtypeStruct((n_batch, SW_BLOCK, SW_KV_W), F32),
            jax.ShapeDtypeStruct((n_batch, SW_BLOCK, SW_KV_W), F32),
        ],
        compiler_params=_cparams(("arbitrary", "arbitrary")),
        name="swa_prompt",
    )(sinks, proj, proj, proj, proj, proj, bias)


def _swa_sample_kernel(q_ref, kn_ref, vn_ref, kc_ref, vc_ref, bias_ref, sink_ref,
                       o_ref, ko_ref, vo_ref, kall_ref, vall_ref, *, seq, n_bb, n_cache):
    n_keys = kall_ref.shape[0]
    zeros_tail = jnp.zeros((n_keys - n_cache - SUBLANES, LANES), F32)
    shape = (SW_GROUP * SUBLANES, n_keys)
    t = _row_iota(shape) % SUBLANES
    s_idx = _lane_iota(shape)
    dist = n_cache + t - s_idx
    valid = (dist >= 0) & (dist < WINDOW)
    lo = _lane_iota((SUBLANES, LANES)) < SW_HD
    out = None
    for bb in range(n_bb):
        shift = (SUBLANES - bb * seq) % SUBLANES

        def top(x, shift=shift):
            return x if shift == 0 else pltpu.roll(x, shift, axis=0)

        kall_ref[0:n_cache, :] = kc_ref[bb]
        kall_ref[n_cache:n_cache + SUBLANES, :] = top(kn_ref[...])
        kall_ref[n_cache + SUBLANES:, :] = zeros_tail
        vall_ref[0:n_cache, :] = vc_ref[bb]
        vall_ref[n_cache:n_cache + SUBLANES, :] = top(vn_ref[...])
        vall_ref[n_cache + SUBLANES:, :] = zeros_tail
        ko_ref[bb] = kall_ref[seq:seq + n_cache, :]
        vo_ref[bb] = vall_ref[seq:seq + n_cache, :]
        kk = _dup_halves(kall_ref[...])
        vv = _dup_halves(vall_ref[...])
        q8 = top(q_ref[...])
        pairs = []
        for kv in range(SW_KV_HEADS):
            pieces = []
            for g in range(SW_GROUP):
                hq = kv * SW_GROUP + g
                qp = q8[:, (hq // 2) * LANES:(hq // 2 + 1) * LANES]
                pieces.append(jnp.where(lo if hq % 2 == 0 else ~lo, qp, 0.0))
            qs = jnp.concatenate(pieces, axis=0)
            s = _dot_nt(qs, kk[kv]) * (SW_HD ** -0.5) + bias_ref[kv]
            res = _sink_softmax_pv(s, valid, sink_ref[kv], vv[kv])
            for g in range(0, SW_GROUP, 2):
                pairs.append(jnp.where(lo, res[g * SUBLANES:(g + 1) * SUBLANES],
                                       res[(g + 1) * SUBLANES:(g + 2) * SUBLANES]))
        o = jnp.concatenate(pairs, axis=1)
        back = (bb * seq) % SUBLANES
        o = o if back == 0 else pltpu.roll(o, back, axis=0)
        rows = _row_iota(o.shape)
        sel = (rows >= bb * seq) & (rows < (bb + 1) * seq)
        out = jnp.where(sel, o, 0.0 if out is None else out)
    o_ref[...] = out


def _swa_sample(proj, k_cache, v_cache, bias, sink_rows, row0, n_batch, seq):
    assert SUBLANES % seq == 0
    n_bb = SUBLANES // seq
    n_cache = k_cache.shape[1]
    assert n_batch % n_bb == 0 and row0 % SUBLANES == 0 and n_cache % SUBLANES == 0
    n_keys = bias.shape[-1]
    rb0 = row0 // SUBLANES
    blk = lambda col: (lambda i: (rb0 + i, col))
    return pl.pallas_call(
        functools.partial(_swa_sample_kernel, seq=seq, n_bb=n_bb, n_cache=n_cache),
        grid=(n_batch // n_bb,),
        in_specs=[
            pl.BlockSpec((SUBLANES, SW_HEADS * SW_HD), blk(COL_SQ // (SW_HEADS * SW_HD))),
            pl.BlockSpec((SUBLANES, SW_KV_W), blk(COL_SK // SW_KV_W)),
            pl.BlockSpec((SUBLANES, SW_KV_W), blk(COL_SV // SW_KV_W)),
            pl.BlockSpec((n_bb, n_cache, SW_KV_W), lambda i: (i, 0, 0)),
            pl.BlockSpec((n_bb, n_cache, SW_KV_W), lambda i: (i, 0, 0)),
            pl.BlockSpec((SW_KV_HEADS, SW_GROUP * SUBLANES, n_keys), lambda i: (0, 0, 0)),
            pl.BlockSpec((SW_KV_HEADS, SW_GROUP * SUBLANES, n_keys), lambda i: (0, 0, 0)),
        ],
        out_specs=[
            pl.BlockSpec((SUBLANES, SW_HEADS * SW_HD), lambda i: (i, 0)),
            pl.BlockSpec((n_bb, n_cache, SW_KV_W), lambda i: (i, 0, 0)),
            pl.BlockSpec((n_bb, n_cache, SW_KV_W), lambda i: (i, 0, 0)),
        ],
        out_shape=[
            jax.ShapeDtypeStruct((n_batch * seq, SW_HEADS * SW_HD), F32),
            jax.ShapeDtypeStruct(k_cache.shape, F32),
            jax.ShapeDtypeStruct(v_cache.shape, F32),
        ],
        scratch_shapes=[pltpu.VMEM((n_keys, SW_KV_W), F32), pltpu.VMEM((n_keys, SW_KV_W), F32)],
        compiler_params=_cparams(("arbitrary",)),
        name="swa_sample",
    )(proj, proj, proj, k_cache, v_cache, bias, sink_rows)


def _mix_kernel(xp_ref, xs_ref, oap_ref, oas_ref, obp_ref, obs_ref, ga_ref, gb_ref, wo_ref, nw_ref, wr_ref, br_ref,
                x1_ref, h2_ref, route_ref, *, n_prompt_tiles):
    i = pl.program_id(0)

    def run(x_ref, oa_ref, ob_ref):
        mixed = _sigmoid(ga_ref[...]) * oa_ref[...] + _sigmoid(gb_ref[...]) * ob_ref[...]
        x1 = x_ref[...] + _dot(mixed, wo_ref[...])
        x1_ref[...] = x1
        h2 = x1 * lax.rsqrt(jnp.mean(x1 * x1, axis=-1, keepdims=True) + EPS) * nw_ref[...]
        _to_tiles(h2_ref, h2)
        logits = _dot_exact(h2, wr_ref[...]) + br_ref[...]
        lane = _lane_iota(logits.shape)
        lanef = lane.astype(F32)
        big = float(2 * LANES)
        is_g = lane < N_GROUPS
        gl = jnp.where(is_g, logits, -jnp.inf)
        gmax = jnp.max(gl, axis=-1, keepdims=True)
        gval = 1.0 / jnp.sum(jnp.where(is_g, jnp.exp(gl - gmax), 0.0), axis=-1, keepdims=True)
        grp = jnp.min(jnp.where(gl == gmax, lanef, big), axis=-1, keepdims=True)
        e_grp = ((lane - N_GROUPS) >> 3).astype(F32)
        is_e = (lane >= N_GROUPS) & (lane < N_GROUPS + N_EXPERTS) & (e_grp == grp)
        el = jnp.where(is_e, logits, -jnp.inf)
        v1 = jnp.max(el, axis=-1, keepdims=True)
        i1 = jnp.min(jnp.where(el == v1, lanef, big), axis=-1, keepdims=True)
        el2 = jnp.where(lanef == i1, -jnp.inf, el)
        v2 = jnp.max(el2, axis=-1, keepdims=True)
        i2 = jnp.min(jnp.where(el2 == v2, lanef, big), axis=-1, keepdims=True)
        e2 = jnp.exp(v2 - v1)
        w1 = gval / (1.0 + e2)
        w2 = gval * e2 / (1.0 + e2)
        route_ref[...] = jnp.where(lane == 0, i1 - N_GROUPS,
                                   jnp.where(lane == 1, i2 - N_GROUPS,
                                             jnp.where(lane == 2, w1, jnp.where(lane == 3, w2, 0.0))))

    @pl.when(i < n_prompt_tiles)
    def _():
        run(xp_ref, oap_ref, obp_ref)

    @pl.when(i >= n_prompt_tiles)
    def _():
        run(xs_ref, oas_ref, obs_ref)


def _mix(xp, xs, oa_p, oa_s, ob_p, ob_s, proj, w_out, norm_w, w_router, b_router):
    tp, ts = xp.shape[0], xs.shape[0]
    tm = _token_tile(tp, ts)
    npt, nst = tp // tm, ts // tm
    const = lambda i: (0, 0)
    row = lambda i: (i, 0)
    return pl.pallas_call(
        functools.partial(_mix_kernel, n_prompt_tiles=npt),
        grid=(npt + nst,),
        in_specs=[
            pl.BlockSpec((tm, D_MODEL), lambda i: (jnp.minimum(i, npt - 1), 0)),
            pl.BlockSpec((tm, D_MODEL), lambda i: (jnp.maximum(i - npt, 0), 0)),
            pl.BlockSpec((tm, D_MODEL), lambda i: (jnp.minimum(i, npt - 1), 0)),
            pl.BlockSpec((tm, D_MODEL), lambda i: (jnp.maximum(i - npt, 0), 0)),
            pl.BlockSpec((tm, D_MODEL), lambda i: (jnp.minimum(i, npt - 1), 0)),
            pl.BlockSpec((tm, D_MODEL), lambda i: (jnp.maximum(i - npt, 0), 0)),
            pl.BlockSpec((tm, D_MODEL), lambda i: (i, COL_GA // D_MODEL)),
            pl.BlockSpec((tm, D_MODEL), lambda i: (i, COL_GB // D_MODEL)),
            pl.BlockSpec((D_MODEL, D_MODEL), const),
            pl.BlockSpec((1, D_MODEL), const),
            pl.BlockSpec((D_MODEL, LANES), const),
            pl.BlockSpec((1, LANES), const),
        ],
        out_specs=[
            pl.BlockSpec((tm, D_MODEL), row),
            pl.BlockSpec((tm,) + TOK_TILE, lambda i: (i, 0, 0)),
            pl.BlockSpec((tm, LANES), row),
        ],
        out_shape=[
            jax.ShapeDtypeStruct((tp + ts, D_MODEL), F32),
            jax.ShapeDtypeStruct((tp + ts,) + TOK_TILE, F32),
            jax.ShapeDtypeStruct((tp + ts, LANES), F32),
        ],
        compiler_params=_cparams(("arbitrary",)),
        name="mix_router",
    )(xp, xs, oa_p, oa_s, ob_p, ob_s, proj, proj, w_out, norm_w, w_router, b_router)


def _rank_kernel(route_ref, dest_ref, meta_ref, rank_ref, cnt_ref, *, tile, blk):
    phase = pl.program_id(0)
    i = pl.program_id(1)
    shape = (tile, LANES)
    lane = _lane_iota(shape)
    lanef = lane.astype(F32)
    r = route_ref[...]
    oh0 = lanef == _col(r, 0)
    oh1 = lanef == _col(r, 1)
    rows = pl.ds(pl.multiple_of(i * tile, tile), tile)

    @pl.when(phase == 0)
    def _():
        @pl.when(i == 0)
        def _():
            cnt_ref[...] = jnp.zeros(cnt_ref.shape, F32)

        oh = jnp.where(oh0 | oh1, 1.0, 0.0)
        tri = jnp.where(_row_iota((tile, tile)) > _lane_iota((tile, tile)), 1.0, 0.0)
        before = _dot(tri, oh) + cnt_ref[0:1, :]
        rank0 = jnp.sum(jnp.where(oh0, before, 0.0), axis=-1, keepdims=True)
        rank1 = jnp.sum(jnp.where(oh1, before, 0.0), axis=-1, keepdims=True)
        rank_ref[rows, :] = jnp.where(lane == 0, rank0, jnp.where(lane == 1, rank1, 0.0))
        cnt_ref[0:1, :] = cnt_ref[0:1, :] + jnp.sum(oh, axis=0, keepdims=True)

    @pl.when(phase == 1)
    def _():
        cnt = cnt_ref[0:1, :]
        padded = jnp.floor((cnt + (blk - 1)) / blk) * blk
        before_lane = jnp.where(_row_iota((LANES, LANES)) < _lane_iota((LANES, LANES)), 1.0, 0.0)
        start = _dot_exact(jnp.broadcast_to(padded, (SUBLANES, LANES)), before_lane)[0:1, :]
        rk = rank_ref[rows, :]
        d0 = jnp.sum(jnp.where(oh0, start, 0.0), axis=-1, keepdims=True) + _col(rk, 0)
        d1 = jnp.sum(jnp.where(oh1, start, 0.0), axis=-1, keepdims=True) + _col(rk, 1)
        dest_ref[...] = jnp.where(lane == 0, d0, jnp.where(lane == 1, d1, 0.0)).astype(I32)

        @pl.when(i == 0)
        def _():
            end = start + padded
            mshape = meta_ref.shape
            blk_start = (_row_iota(mshape) * blk).astype(F32)
            hit = (_lane_iota(mshape) < N_EXPERTS) & (end <= blk_start)
            be = jnp.minimum(jnp.sum(jnp.where(hit, 1.0, 0.0), axis=-1, keepdims=True), N_EXPERTS - 1.0)
            n_used = _col(end, N_EXPERTS - 1) / blk
            ml = _lane_iota(mshape)
            mine = ml.astype(F32) == be
            seg_start = jnp.sum(jnp.where(mine, start, 0.0), axis=-1, keepdims=True)
            seg_count = jnp.sum(jnp.where(mine, cnt, 0.0), axis=-1, keepdims=True)
            n_valid = jnp.clip(seg_count - (blk_start[:, 0:1] - seg_start), 0.0, float(blk))
            meta_ref[...] = jnp.where(ml == 0, be, jnp.where(ml == 1, n_used,
                                                              jnp.where(ml == 2, n_valid, 0.0))).astype(I32)


def _rank(route, tile, blk, n_blocks):
    t = route.shape[0]
    nt = t // tile
    nbp = -(-n_blocks // SUBLANES) * SUBLANES
    return pl.pallas_call(
        functools.partial(_rank_kernel, tile=tile, blk=blk),
        grid=(2, nt),
        in_specs=[pl.BlockSpec((tile, LANES), lambda p, i: (i, 0))],
        out_specs=[
            pl.BlockSpec((tile, LANES), lambda p, i: (i * p, 0)),
            pl.BlockSpec((nbp, LANES), lambda p, i: (0, 0)),
        ],
        out_shape=[
            jax.ShapeDtypeStruct((t, LANES), I32),
            jax.ShapeDtypeStruct((nbp, LANES), I32),
        ],
        scratch_shapes=[pltpu.VMEM((t, LANES), F32), pltpu.VMEM((SUBLANES, LANES), F32)],
        compiler_params=_cparams(("arbitrary", "arbitrary")),
        name="moe_rank",
    )(route)


TOK_TILE = (D_MODEL // LANES, LANES)


def _to_tiles(ref, x):
    for j in range(TOK_TILE[0]):
        ref[:, j, :] = x[:, j * LANES:(j + 1) * LANES]


def _from_tiles(ref):
    return jnp.concatenate([ref[:, j, :] for j in range(TOK_TILE[0])], axis=1)


def _row_copy(src, src_row, dst, dst_row, sem):
    return pltpu.make_async_copy(src.at[src_row], dst.at[dst_row], sem)


def _invert_kernel(dest_ref, inv_ref):
    def clear(j, carry):
        inv_ref[j] = 0
        return carry

    def place(a, carry):
        inv_ref[dest_ref[a]] = a >> 1
        return carry

    lax.fori_loop(0, inv_ref.shape[0], clear, 0, unroll=8)
    lax.fori_loop(0, dest_ref.shape[0], place, 0, unroll=8)


def _invert(dest_flat, n_slots):
    return pl.pallas_call(
        _invert_kernel,
        in_specs=[pl.BlockSpec(memory_space=pltpu.SMEM)],
        out_specs=pl.BlockSpec(memory_space=pltpu.SMEM),
        out_shape=jax.ShapeDtypeStruct((n_slots,), I32),
        name="moe_invert",
    )(dest_flat)


def _last_used(i, nu_ref):
    return jnp.minimum(i, jnp.maximum(nu_ref[0] - 1, 0))


def _for_rows(n, body):
    def group(g, carry):
        for u in range(SUBLANES):
            body(g * SUBLANES + u)
        return carry

    def single(t, carry):
        body(t)
        return carry

    n_groups = n // SUBLANES
    lax.fori_loop(0, n_groups, group, 0)
    lax.fori_loop(n_groups * SUBLANES, n, single, 0)


def _expert_kernel(be_ref, nu_ref, nv_ref, inv_ref, h2_ref, wg_ref, wu_ref, wd_ref, y_ref, xbuf_ref, wgu_ref, wdn_ref, sem,
                   *, blk):
    i = pl.program_id(0)
    n_used = nu_ref[0]
    used = i < n_used
    slot = i % 2
    blk_i = _last_used(i, nu_ref)
    fresh = (i == 0) | (be_ref[blk_i] != be_ref[jnp.maximum(blk_i - 1, 0)])

    def gather(block, into):
        _for_rows(nv_ref[block],
                  lambda t: _row_copy(h2_ref, inv_ref[block * blk + t], xbuf_ref.at[into], t, sem.at[into]).start())

    @pl.when(i == 0)
    def _():
        xbuf_ref[...] = jnp.zeros(xbuf_ref.shape, F32)
        gather(0, 0)

    @pl.when(i + 1 < n_used)
    def _():
        gather(i + 1, 1 - slot)

    @pl.when(used & fresh)
    def _():
        wgu_ref[:, :D_EXPERT] = wg_ref[0].astype(BF16)
        wgu_ref[:, D_EXPERT:] = wu_ref[0].astype(BF16)
        wdn_ref[...] = wd_ref[0].astype(BF16)

    @pl.when(used)
    def _():
        _for_rows(nv_ref[i], lambda t: _row_copy(h2_ref, 0, xbuf_ref.at[slot], 0, sem.at[slot]).wait())
        gu = jnp.dot(_from_tiles(xbuf_ref.at[slot]).astype(BF16), wgu_ref[...], preferred_element_type=F32)
        g = gu[:, :D_EXPERT]
        hidden = (g * _sigmoid(g) * gu[:, D_EXPERT:]).astype(BF16)
        _to_tiles(y_ref, jnp.dot(hidden, wdn_ref[...], preferred_element_type=F32))

    @pl.when(jnp.logical_not(used))
    def _():
        y_ref[...] = jnp.zeros(y_ref.shape, F32)


def _experts(block_expert, n_used, n_valid, inv, h2, w_gate, w_up, w_down, blk):
    n_blocks = inv.shape[0] // blk
    wsel = lambda i, be, nu, nv, iv: (be[_last_used(i, nu)], 0, 0)
    return pl.pallas_call(
        functools.partial(_expert_kernel, blk=blk),
        grid_spec=pltpu.PrefetchScalarGridSpec(
            num_scalar_prefetch=4,
            grid=(n_blocks,),
            in_specs=[
                pl.BlockSpec(memory_space=pl.ANY),
                pl.BlockSpec((1, D_MODEL, D_EXPERT), wsel),
                pl.BlockSpec((1, D_MODEL, D_EXPERT), wsel),
                pl.BlockSpec((1, D_EXPERT, D_MODEL), wsel),
            ],
            out_specs=pl.BlockSpec((blk,) + TOK_TILE, lambda i, be, nu, nv, iv: (i, 0, 0)),
            scratch_shapes=[
                pltpu.VMEM((2, blk) + TOK_TILE, F32),
                pltpu.VMEM((D_MODEL, 2 * D_EXPERT), BF16),
                pltpu.VMEM((D_EXPERT, D_MODEL), BF16),
                pltpu.SemaphoreType.DMA((2,)),
            ],
        ),
        out_shape=jax.ShapeDtypeStruct((n_blocks * blk,) + TOK_TILE, F32),
        compiler_params=_cparams(("arbitrary",)),
        name="moe_experts",
    )(block_expert, n_used, n_valid, inv, h2, w_gate, w_up, w_down)


def _combine_kernel(dest_ref, x1_ref, route_ref, nw_ref, ys_ref, yp_ref, ysm_ref, ybuf_ref, sem,
                    *, tile, n_prompt_tiles):
    i = pl.program_id(0)
    slot = i % 2

    def gather(step, into):
        def issue(t, carry):
            for k in range(2):
                _row_copy(ys_ref, dest_ref[2 * (step * tile + t) + k], ybuf_ref.at[into], k * tile + t,
                          sem.at[into]).start()
            return carry

        lax.fori_loop(0, tile, issue, 0, unroll=4)

    @pl.when(i == 0)
    def _():
        gather(0, 0)

    @pl.when(i + 1 < pl.num_programs(0))
    def _():
        gather(i + 1, 1 - slot)

    def drain(t, carry):
        _row_copy(ys_ref, 0, ybuf_ref.at[slot], 0, sem.at[slot]).wait()
        return carry

    lax.fori_loop(0, 2 * tile, drain, 0, unroll=8)
    r = route_ref[...]
    ybuf = ybuf_ref.at[slot]
    y = (_from_tiles(ybuf.at[pl.ds(0, tile)]) * _col(r, 2)
         + _from_tiles(ybuf.at[pl.ds(tile, tile)]) * _col(r, 3))
    x2 = x1_ref[...] + y
    out = x2 * lax.rsqrt(jnp.mean(x2 * x2, axis=-1, keepdims=True) + EPS) * nw_ref[...]

    @pl.when(i < n_prompt_tiles)
    def _():
        yp_ref[...] = out

    @pl.when(i >= n_prompt_tiles)
    def _():
        ysm_ref[...] = out


def _combine(dest_flat, x1, route, norm_w, ys, tp, ts):
    tile = _token_tile(tp, ts)
    npt, nst = tp // tile, ts // tile
    return pl.pallas_call(
        functools.partial(_combine_kernel, tile=tile, n_prompt_tiles=npt),
        grid_spec=pltpu.PrefetchScalarGridSpec(
            num_scalar_prefetch=1,
            grid=(npt + nst,),
            in_specs=[
                pl.BlockSpec((tile, D_MODEL), lambda i, d: (i, 0)),
                pl.BlockSpec((tile, LANES), lambda i, d: (i, 0)),
                pl.BlockSpec((1, D_MODEL), lambda i, d: (0, 0)),
                pl.BlockSpec(memory_space=pl.ANY),
            ],
            out_specs=[
                pl.BlockSpec((tile, D_MODEL), lambda i, d: (jnp.minimum(i, npt - 1), 0)),
                pl.BlockSpec((tile, D_MODEL), lambda i, d: (jnp.maximum(i - npt, 0), 0)),
            ],
            scratch_shapes=[pltpu.VMEM((2, 2 * tile) + TOK_TILE, F32), pltpu.SemaphoreType.DMA((2,))],
        ),
        out_shape=[
            jax.ShapeDtypeStruct((tp, D_MODEL), F32),
            jax.ShapeDtypeStruct((ts, D_MODEL), F32),
        ],
        compiler_params=_cparams(("arbitrary",)),
        name="moe_combine",
    )(dest_flat, x1, route, norm_w, ys)


def _layer(xp, xs, n_batch, seq, s_batch, s_seq, conv_state, dn_state, k_cache, v_cache,
           w_in, conv_w, a_log, dt_bias, dn_norm_w, sinks, rel_bias, w_out, norm_mix_w, norm_ffn_w,
           w_rg, b_rg, w_re, b_re, w_gate, w_up, w_down, norm_final_w):
    tp, ts = xp.shape[0], xs.shape[0]
    t_all = tp + ts
    row = lambda v: v.reshape(1, -1).astype(F32)

    o = np.cumsum((0, DN_QK_W, DN_QK_W, DN_V_W, DN_V_W, DN_HEADS, DN_HEADS, SW_HEADS * SW_HD, SW_KV_W, SW_KV_W,
                   D_MODEL, D_MODEL)).tolist()
    w_big = jnp.concatenate([w_in[:, o[0]:o[4]], w_in[:, o[6]:o[7]], w_in[:, o[9]:o[11]], w_in[:, o[7]:o[9]]],
                            axis=1).astype(BF16)
    w_small = jnp.pad(w_in[:, o[4]:o[6]], ((0, 0), (0, LANES - 2 * DN_HEADS))).astype(BF16)
    head_row = lambda v: jnp.pad(v.astype(F32), (DN_HEADS, LANES - 2 * DN_HEADS)).reshape(1, LANES)
    w_router = jnp.pad(jnp.concatenate([w_rg, w_re], axis=1).astype(F32),
                       ((0, 0), (0, LANES - N_GROUPS - N_EXPERTS)))
    b_router = jnp.pad(jnp.concatenate([b_rg, b_re]).astype(F32), (0, LANES - N_GROUPS - N_EXPERTS)).reshape(1, LANES)

    conv0 = jnp.zeros((n_batch, DN_CONV - 1, DN_CONV_W), F32)
    proj, ba, conv_tail = _inproj(xp, xs, row(norm_mix_w), w_big, w_small, conv_w.astype(F32), conv0, seq)
    p_conv = conv_tail[:, SUBLANES - (DN_CONV - 1):, :]

    dn0 = jnp.zeros((n_batch, DN_HEADS, DN_DK, DN_DV), F32)
    oa_p, p_dn = _dn_prompt(proj, ba, head_row(a_log), head_row(dt_bias), row(dn_norm_w), dn0, n_batch, seq)
    oa_s, s_conv, s_dn = _dn_sample(proj, ba, conv_w.astype(F32), head_row(a_log), head_row(dt_bias), row(dn_norm_w),
                                    conv_state, dn_state, tp, s_batch, s_seq)

    qpos = jnp.arange(SW_BLOCK)[:, None]
    kpos = jnp.arange(2 * SW_BLOCK)[None, :] - SW_BLOCK
    bias_p = _relbias(rel_bias.astype(F32), _rel_bucket(qpos - kpos))
    ob_p, p_k, p_v = _swa_prompt(proj, sinks.astype(F32), bias_p, n_batch, seq)
    n_cache = k_cache.shape[1]
    n_keys = -(-(n_cache + SUBLANES) // LANES) * LANES
    tq = n_cache + jnp.arange(SUBLANES)[:, None]
    bias_s = _relbias(rel_bias.astype(F32), _rel_bucket(tq - jnp.arange(n_keys)[None, :]))
    bias_s = bias_s.reshape(SW_KV_HEADS, SW_GROUP * SUBLANES, n_keys)
    sink_rows = jnp.broadcast_to(jnp.repeat(sinks.astype(F32).reshape(SW_KV_HEADS, SW_GROUP), SUBLANES, axis=1)[:, :, None],
                                 (SW_KV_HEADS, SW_GROUP * SUBLANES, n_keys))
    ob_s, s_k, s_v = _swa_sample(proj, k_cache.reshape(s_batch, n_cache, SW_KV_W), v_cache.reshape(s_batch, n_cache, SW_KV_W),
                                 bias_s, sink_rows, tp, s_batch, s_seq)

    x1, h2, route = _mix(xp, xs, oa_p, oa_s, ob_p, ob_s, proj, w_out.astype(BF16), row(norm_ffn_w), w_router, b_router)

    tile = _token_tile(tp, ts)
    n_blocks = -(-2 * t_all // MOE_BLOCK) + N_EXPERTS
    dest, meta = _rank(route, tile, MOE_BLOCK, n_blocks)
    dest_flat = dest[:, :2].reshape(-1)
    block_expert = meta[:n_blocks, 0]
    n_used = meta[0:1, 1]
    n_valid = meta[:n_blocks, 2]
    inv = _invert(dest_flat, n_blocks * MOE_BLOCK)
    ys = _experts(block_expert, n_used, n_valid, inv, h2, w_gate, w_up, w_down, MOE_BLOCK)
    y_p, y_s = _combine(dest_flat, x1, route, row(norm_final_w), ys, tp, ts)

    kv_shape = (n_batch, WINDOW, SW_KV_HEADS, SW_HD)
    return (y_p, y_s, p_conv, p_dn, p_k.reshape(kv_shape), p_v.reshape(kv_shape), s_conv, s_dn,
            s_k.reshape(k_cache.shape), s_v.reshape(v_cache.shape))


def kernel(x_prompt, x_sample, state_dn_conv, state_dn, cache_swa_k, cache_swa_v, w_in, conv_w, a_log, dt_bias, dn_norm_w, sinks, rel_bias, w_out, norm_mix_w, norm_ffn_w, w_router_group, b_router_group, w_router_expert, b_router_expert, w_gate, w_up, w_down, norm_final_w):
    depth = w_in.shape[0]
    assert depth == 1, "the final-norm fusion below assumes a single layer"
    n_batch, seq, _ = x_prompt.shape
    s_batch, s_seq, _ = x_sample.shape
    outs = _layer(x_prompt.reshape(-1, D_MODEL), x_sample.reshape(-1, D_MODEL), n_batch, seq, s_batch, s_seq,
                  state_dn_conv[0], state_dn[0], cache_swa_k[0], cache_swa_v[0],
                  w_in[0], conv_w[0], a_log[0], dt_bias[0], dn_norm_w[0], sinks[0], rel_bias,
                  w_out[0], norm_mix_w[0], norm_ffn_w[0], w_router_group[0], b_router_group[0],
                  w_router_expert[0], b_router_expert[0], w_gate[0], w_up[0], w_down[0], norm_final_w)
    y_p, y_s, p_conv, p_dn, p_k, p_v, s_conv, s_dn, s_k, s_v = outs
    return (y_p.reshape(x_prompt.shape), y_s.reshape(x_sample.shape), p_conv[None], p_dn[None], p_k[None], p_v[None],
            s_conv[None], s_dn[None], s_k[None], s_v[None])
```

```python
import functools
import math

import jax
import jax.numpy as jnp
import numpy as np
from jax import lax
from jax.experimental import pallas as pl
from jax.experimental.pallas import tpu as pltpu

F32 = jnp.float32
BF16 = jnp.bfloat16
I32 = jnp.int32

D_MODEL = 1024
DN_HEADS = 8
DN_DK = 128
DN_DV = 128
DN_CONV = 4
DN_CHUNK = 64
DN_QK_W = DN_HEADS * DN_DK
DN_V_W = DN_HEADS * DN_DV
DN_CONV_W = 2 * DN_QK_W + DN_V_W
SW_HEADS = 16
SW_KV_HEADS = 2
SW_GROUP = SW_HEADS // SW_KV_HEADS
SW_HD = 64
SW_KV_W = SW_KV_HEADS * SW_HD
WINDOW = 128
SW_BLOCK = 128
REL_BUCKETS = 32
REL_MAX_DIST = 128
N_GROUPS = 8
EXP_PER_GROUP = 8
N_EXPERTS = N_GROUPS * EXP_PER_GROUP
D_EXPERT = 256
MOE_BLOCK = 256
EPS = 1e-6

LANES = 128
SUBLANES = 8
VMEM_LIMIT = 56 * 1024 * 1024

COL_QKV = 0
COL_Z = 3072
COL_SQ = 4096
COL_GA = 5120
COL_GB = 6144
COL_SK = 7168
COL_SV = 7296
PROJ_W = 7424
PROJ_CHUNK = 512


def _cparams(sem):
    return pltpu.CompilerParams(dimension_semantics=sem, vmem_limit_bytes=VMEM_LIMIT)


def _sigmoid(x):
    return 0.5 * jnp.tanh(0.5 * x) + 0.5


def _dot(a, b):
    return jnp.dot(a.astype(BF16), b.astype(BF16), preferred_element_type=F32)


def _dot_nt(a, b):
    return lax.dot_general(a.astype(BF16), b.astype(BF16), (((1,), (1,)), ((), ())), preferred_element_type=F32)


def _dot_tn(a, b):
    return lax.dot_general(a.astype(BF16), b.astype(BF16), (((0,), (0,)), ((), ())), preferred_element_type=F32)


def _dot_exact(a, b):
    return jnp.dot(a, b, precision=lax.Precision.HIGHEST, preferred_element_type=F32)


def _lane_iota(shape):
    return lax.broadcasted_iota(I32, shape, len(shape) - 1)


def _row_iota(shape):
    return lax.broadcasted_iota(I32, shape, len(shape) - 2)


def _col(x, j):
    return jnp.sum(jnp.where(_lane_iota(x.shape) == j, x, 0.0), axis=-1, keepdims=True)


def _token_tile(*sizes, cands=(256, 128, 64, 32, 16, 8)):
    for t in cands:
        if all(s % t == 0 for s in sizes):
            return t
    raise ValueError(f"token counts {sizes} need a common tile that is a multiple of 8")


def _inproj_kernel(xp_ref, xs_ref, nw_ref, wb_ref, ws_ref, convw_ref, conv0_ref, proj_ref, ba_ref, tail_ref, cbuf_ref,
                   *, n_prompt_tiles, tiles_per_seq):
    i = pl.program_id(0)
    tm = xp_ref.shape[0]
    hist = SUBLANES - (DN_CONV - 1)

    def project(x_ref, conv):
        x = x_ref[...]
        h = (x * lax.rsqrt(jnp.mean(x * x, axis=-1, keepdims=True) + EPS) * nw_ref[...]).astype(BF16)
        ba_ref[...] = jnp.dot(h, ws_ref[...], preferred_element_type=F32)
        top = _row_iota((SUBLANES, PROJ_CHUNK))
        for c0 in range(0, PROJ_W, PROJ_CHUNK):
            c1 = min(c0 + PROJ_CHUNK, PROJ_W)
            cur = jnp.dot(h, wb_ref[:, c0:c1], preferred_element_type=F32)
            if conv and c1 <= COL_QKV + DN_CONV_W:
                prev = cbuf_ref[:, c0:c1]
                acc = cur * convw_ref[DN_CONV - 1:DN_CONV, c0:c1]
                for s in range(1, DN_CONV):
                    sh = pltpu.roll(cur, s, axis=0)
                    head = jnp.where(top < s, pltpu.roll(prev, s, axis=0), sh[:SUBLANES])
                    sh = jnp.concatenate([head, sh[SUBLANES:]], axis=0)
                    acc = acc + sh * convw_ref[DN_CONV - 1 - s:DN_CONV - s, c0:c1]
                cbuf_ref[:, c0:c1] = cur[tm - SUBLANES:]
                cur = acc * _sigmoid(acc)
            proj_ref[:, c0:c1] = cur

    @pl.when(i < n_prompt_tiles)
    def _():
        @pl.when(i % tiles_per_seq == 0)
        def _():
            cbuf_ref[...] = jnp.zeros(cbuf_ref.shape, F32)
            cbuf_ref[hist:SUBLANES, :] = conv0_ref[0]

        project(xp_ref, True)
        tail_ref[0] = cbuf_ref[...]

    @pl.when(i >= n_prompt_tiles)
    def _():
        project(xs_ref, False)


def _inproj(xp, xs, norm_w, w_big, w_small, conv_w, conv0, seq):
    tp, ts = xp.shape[0], xs.shape[0]
    tm = _token_tile(tp, ts, seq)
    assert COL_QKV == 0 and DN_CONV_W % PROJ_CHUNK == 0
    npt, nst = tp // tm, ts // tm
    tps = seq // tm
    const = lambda i: (0, 0)
    seq_of = lambda i: (jnp.minimum(i, npt - 1) // tps, 0, 0)
    return pl.pallas_call(
        functools.partial(_inproj_kernel, n_prompt_tiles=npt, tiles_per_seq=tps),
        grid=(npt + nst,),
        in_specs=[
            pl.BlockSpec((tm, D_MODEL), lambda i: (jnp.minimum(i, npt - 1), 0)),
            pl.BlockSpec((tm, D_MODEL), lambda i: (jnp.maximum(i - npt, 0), 0)),
            pl.BlockSpec((1, D_MODEL), const),
            pl.BlockSpec((D_MODEL, PROJ_W), const, pipeline_mode=pl.Buffered(1)),
            pl.BlockSpec((D_MODEL, LANES), const),
            pl.BlockSpec((DN_CONV, DN_CONV_W), const),
            pl.BlockSpec((1, DN_CONV - 1, DN_CONV_W), seq_of),
        ],
        out_specs=[
            pl.BlockSpec((tm, PROJ_W), lambda i: (i, 0)),
            pl.BlockSpec((tm, LANES), lambda i: (i, 0)),
            pl.BlockSpec((1, SUBLANES, DN_CONV_W), seq_of),
        ],
        out_shape=[
            jax.ShapeDtypeStruct((tp + ts, PROJ_W), F32),
            jax.ShapeDtypeStruct((tp + ts, LANES), F32),
            jax.ShapeDtypeStruct((tp // seq, SUBLANES, DN_CONV_W), F32),
        ],
        scratch_shapes=[pltpu.VMEM((SUBLANES, DN_CONV_W), F32)],
        compiler_params=_cparams(("arbitrary",)),
        name="inproj",
    )(xp, xs, norm_w, w_big, w_small, conv_w, conv0)


def _dn_core(groups, alog, dtb, nw, read_state, write_state, n_seg, seg_valid):
    rows = groups[0][0].shape[0]
    sr = rows // n_seg
    assert sr * n_seg == rows and sr & (sr - 1) == 0 and rows <= LANES
    seg_shift = sr.bit_length() - 1
    ri = _row_iota((rows, rows))
    ci = _lane_iota((rows, rows))
    incl = ri >= ci
    strict = ri > ci
    if n_seg > 1:
        same = (ri >> seg_shift) == (ci >> seg_shift)
        incl = incl & same
        strict = strict & same
    l_incl = incl.astype(F32)
    eye = (ri == ci).astype(F32)
    levels = max(1, math.ceil(math.log2(seg_valid)))

    beta_all, gsum_all, gtot_all, gsum_t = [], [], [], []
    for _, _, ba in groups:
        b_all = _sigmoid(ba)
        sp = ba + dtb
        softplus = jnp.maximum(sp, 0.0) + jnp.log1p(jnp.exp(-jnp.abs(sp)))
        g_all = -jnp.exp(alog) * softplus
        if seg_valid < sr:
            live = (_row_iota((rows, LANES)) & (sr - 1)) < seg_valid
            b_all = jnp.where(live, b_all, 0.0)
            g_all = jnp.where(live, g_all, 0.0)
        gs = _dot_exact(l_incl, g_all)
        beta_all.append(b_all)
        gsum_all.append(gs)
        gtot_all.append(_dot_exact(same.astype(F32), g_all) if n_seg > 1 else gs[rows - 1:rows, :])
        padded = gs if rows == LANES else jnp.concatenate([gs, jnp.zeros((LANES - rows, LANES), F32)], axis=0)
        gsum_t.append(padded.T)

    probs = [(g, h) for g in range(len(groups)) for h in range(DN_HEADS)]
    segs = range(n_seg)
    q, k, v, kb, beta, gsum, gtot = {}, {}, {}, {}, {}, {}, {}
    for p in probs:
        g, h = p
        qkv = groups[g][0]
        qh = qkv[:, h * DN_DK:(h + 1) * DN_DK]
        kh = qkv[:, DN_QK_W + h * DN_DK:DN_QK_W + (h + 1) * DN_DK]
        v[p] = qkv[:, 2 * DN_QK_W + h * DN_DV:2 * DN_QK_W + (h + 1) * DN_DV]
        q[p] = qh * lax.rsqrt(jnp.sum(qh * qh, axis=-1, keepdims=True) + 1e-6) * (DN_DK ** -0.5)
        k[p] = kh * lax.rsqrt(jnp.sum(kh * kh, axis=-1, keepdims=True) + 1e-6)
        beta[p] = _col(beta_all[g], h)
        gsum[p] = _col(gsum_all[g], DN_HEADS + h)
        gtot[p] = _col(gtot_all[g], DN_HEADS + h)
        kb[p] = k[p] * beta[p]
    kq = {p: _dot_nt(jnp.concatenate([kb[p], q[p]], axis=0), k[p]) for p in probs}
    gamma = {(g, h): jnp.exp(jnp.where(incl, gsum[(g, h)] - gsum_t[g][DN_HEADS + h:DN_HEADS + h + 1, :rows], -jnp.inf))
             for g, h in probs}
    attn = {p: kq[p][rows:] * gamma[p] for p in probs}
    pw = {p: -jnp.where(strict, kq[p][:rows] * gamma[p], 0.0) for p in probs}
    t = {p: eye + pw[p] for p in probs}
    for _ in range(1, levels):
        pw = {p: _dot(pw[p], pw[p]) for p in probs}
        t = {p: t[p] + _dot(t[p], pw[p]) for p in probs}
    eg = {p: jnp.exp(gsum[p]) for p in probs}
    uw = {p: _dot(t[p], jnp.concatenate([v[p] * beta[p], kb[p] * eg[p]], axis=1)) for p in probs}
    qg = {p: q[p] * eg[p] for p in probs}
    state = {(p, s): read_state(p[0], s, p[1]) for p in probs for s in segs}
    wq = {(p, s): _dot(jnp.concatenate([uw[p][s * sr:(s + 1) * sr, DN_DV:], qg[p][s * sr:(s + 1) * sr]], axis=0),
                       state[(p, s)]) for p in probs for s in segs}
    join = lambda pieces: pieces[0] if len(pieces) == 1 else jnp.concatenate(pieces, axis=0)
    v_new = {p: uw[p][:, :DN_DV] - join([wq[(p, s)][:sr] for s in segs]) for p in probs}
    o = {p: join([wq[(p, s)][sr:] for s in segs]) + _dot(attn[p], v_new[p]) for p in probs}
    kd = {p: k[p] * jnp.exp(gtot[p] - gsum[p]) for p in probs}
    for p in probs:
        for s in segs:
            r0 = s * sr if n_seg > 1 else 0
            decay = jnp.exp(gtot[p][r0:r0 + 1, :])
            write_state(p[0], s, p[1],
                        state[(p, s)] * decay + _dot_tn(kd[p][s * sr:(s + 1) * sr], v_new[p][s * sr:(s + 1) * sr]))
    outs = []
    for g, (_, z, _) in enumerate(groups):
        heads = []
        for h in range(DN_HEADS):
            oh = o[(g, h)]
            zz = z[:, h * DN_DV:(h + 1) * DN_DV]
            on = oh * lax.rsqrt(jnp.mean(oh * oh, axis=-1, keepdims=True) + EPS) * nw
            heads.append(on * (zz * _sigmoid(zz)))
        outs.append(jnp.concatenate(heads, axis=1))
    return outs


def _dn_prompt_kernel(*refs, chunk, n_batch):
    nb = n_batch
    qkv_refs, z_refs, ba_refs = refs[0:nb], refs[nb:2 * nb], refs[2 * nb:3 * nb]
    alog_ref, dtb_ref, nw_ref, s0_ref, o_ref, sout_ref = refs[3 * nb:]

    @pl.when(pl.program_id(0) == 0)
    def _():
        sout_ref[...] = s0_ref[...]

    groups = [(qkv_refs[b][...], z_refs[b][...], ba_refs[b][...]) for b in range(nb)]

    def read_state(g, s, h):
        return sout_ref[g, h]

    def write_state(g, s, h, val):
        sout_ref[g, h] = val

    outs = _dn_core(groups, alog_ref[...], dtb_ref[...], nw_ref[...], read_state, write_state, 1, chunk)
    for b in range(nb):
        o_ref[b] = outs[b]


def _dn_prompt(proj, ba, alog_row, dtb_row, dn_nw, s0, n_batch, seq):
    chunk = min(DN_CHUNK, seq)
    assert seq % chunk == 0 and chunk % SUBLANES == 0
    nc = seq // chunk
    const2 = lambda c: (0, 0)
    rows = lambda b, col: (lambda c: (b * nc + c, col))
    batches = range(n_batch)
    o, s_out = pl.pallas_call(
        functools.partial(_dn_prompt_kernel, chunk=chunk, n_batch=n_batch),
        grid=(nc,),
        in_specs=(
            [pl.BlockSpec((chunk, DN_CONV_W), rows(b, COL_QKV // DN_CONV_W)) for b in batches]
            + [pl.BlockSpec((chunk, DN_V_W), rows(b, COL_Z // DN_V_W)) for b in batches]
            + [pl.BlockSpec((chunk, LANES), rows(b, 0)) for b in batches]
            + [
                pl.BlockSpec((1, LANES), const2),
                pl.BlockSpec((1, LANES), const2),
                pl.BlockSpec((1, DN_DV), const2),
                pl.BlockSpec((n_batch, DN_HEADS, DN_DK, DN_DV), lambda c: (0, 0, 0, 0)),
            ]
        ),
        out_specs=[
            pl.BlockSpec((n_batch, chunk, DN_V_W), lambda c: (0, c, 0)),
            pl.BlockSpec((n_batch, DN_HEADS, DN_DK, DN_DV), lambda c: (0, 0, 0, 0)),
        ],
        out_shape=[
            jax.ShapeDtypeStruct((n_batch, seq, DN_V_W), F32),
            jax.ShapeDtypeStruct((n_batch, DN_HEADS, DN_DK, DN_DV), F32),
        ],
        compiler_params=_cparams(("arbitrary",)),
        name="dn_prompt",
    )(*([proj] * n_batch), *([proj] * n_batch), *([ba] * n_batch), alog_row, dtb_row, dn_nw, s0)
    return o.reshape(n_batch * seq, DN_V_W), s_out


def _dn_sample_kernel(qkv_ref, z_ref, ba_ref, convw_ref, alog_ref, dtb_ref, nw_ref, conv0_ref, s0_ref,
                      o_ref, convout_ref, sout_ref, cbuf_ref, *, seq, n_bb):
    hist = SUBLANES - (DN_CONV - 1)
    per = SUBLANES // seq
    n_tiles = n_bb // per

    def spread(ref):
        pieces = []
        for j in range(n_tiles):
            x8 = ref[j * SUBLANES:(j + 1) * SUBLANES, :]
            for r in range(per):
                pieces.append(x8 if r == 0 else pltpu.roll(x8, SUBLANES - r * seq, axis=0))
        return pieces

    for bb, piece in enumerate(spread(qkv_ref)):
        cbuf_ref[bb, SUBLANES:2 * SUBLANES, :] = piece
    cbuf_ref[:, hist:SUBLANES, :] = conv0_ref[...]
    w = convw_ref[...]
    acc = cbuf_ref[:, hist:hist + SUBLANES, :] * w[0:1, :]
    for i in range(1, DN_CONV):
        acc = acc + cbuf_ref[:, hist + i:hist + i + SUBLANES, :] * w[i:i + 1, :]
    live = _row_iota(acc.shape) < seq
    qkv = jnp.where(live, acc * _sigmoid(acc), 0.0).reshape(n_bb * SUBLANES, DN_CONV_W)
    convout_ref[...] = cbuf_ref[:, SUBLANES + seq - (DN_CONV - 1):SUBLANES + seq, :]

    def read_state(g, s, h):
        return s0_ref[s, h]

    def write_state(g, s, h, val):
        sout_ref[s, h] = val

    group = (qkv, jnp.concatenate(spread(z_ref), axis=0), jnp.concatenate(spread(ba_ref), axis=0))
    o = _dn_core([group], alog_ref[...], dtb_ref[...], nw_ref[...], read_state, write_state, n_bb, seq)[0]
    rows = _row_iota((SUBLANES, DN_V_W))
    for j in range(n_tiles):
        tile = None
        for r in range(per):
            bb = j * per + r
            piece = o[bb * SUBLANES:(bb + 1) * SUBLANES]
            piece = piece if r == 0 else pltpu.roll(piece, r * seq, axis=0)
            tile = piece if tile is None else jnp.where(rows >= r * seq, piece, tile)
        o_ref[j * SUBLANES:(j + 1) * SUBLANES, :] = tile


def _dn_sample(proj, ba, conv_w, alog_row, dtb_row, dn_nw, conv0, s0, row0, n_batch, seq):
    assert SUBLANES % seq == 0 and seq >= DN_CONV - 1
    n_bb = SUBLANES
    rows_in = n_bb * seq
    assert n_batch % n_bb == 0 and row0 % rows_in == 0
    rb0 = row0 // rows_in
    const1 = lambda i: (0, 0)
    return pl.pallas_call(
        functools.partial(_dn_sample_kernel, seq=seq, n_bb=n_bb),
        grid=(n_batch // n_bb,),
        in_specs=[
            pl.BlockSpec((rows_in, DN_CONV_W), lambda i: (rb0 + i, COL_QKV // DN_CONV_W)),
            pl.BlockSpec((rows_in, DN_V_W), lambda i: (rb0 + i, COL_Z // DN_V_W)),
            pl.BlockSpec((rows_in, LANES), lambda i: (rb0 + i, 0)),
            pl.BlockSpec((DN_CONV, DN_CONV_W), const1),
            pl.BlockSpec((1, LANES), const1),
            pl.BlockSpec((1, LANES), const1),
            pl.BlockSpec((1, DN_DV), const1),
            pl.BlockSpec((n_bb, DN_CONV - 1, DN_CONV_W), lambda i: (i, 0, 0)),
            pl.BlockSpec((n_bb, DN_HEADS, DN_DK, DN_DV), lambda i: (i, 0, 0, 0)),
        ],
        out_specs=[
            pl.BlockSpec((rows_in, DN_V_W), lambda i: (i, 0)),
            pl.BlockSpec((n_bb, DN_CONV - 1, DN_CONV_W), lambda i: (i, 0, 0)),
            pl.BlockSpec((n_bb, DN_HEADS, DN_DK, DN_DV), lambda i: (i, 0, 0, 0)),
        ],
        out_shape=[
            jax.ShapeDtypeStruct((n_batch * seq, DN_V_W), F32),
            jax.ShapeDtypeStruct((n_batch, DN_CONV - 1, DN_CONV_W), F32),
            jax.ShapeDtypeStruct((n_batch, DN_HEADS, DN_DK, DN_DV), F32),
        ],
        scratch_shapes=[pltpu.VMEM((n_bb, 2 * SUBLANES, DN_CONV_W), F32)],
        compiler_params=_cparams(("arbitrary",)),
        name="dn_sample",
    )(proj, proj, ba, conv_w, alog_row, dtb_row, dn_nw, conv0, s0)


def _rel_bucket(dist):
    n = jnp.maximum(dist, 0)
    max_exact = REL_BUCKETS // 2
    large = max_exact + (jnp.log(jnp.maximum(n, 1).astype(F32) / max_exact)
                         / math.log(REL_MAX_DIST / max_exact) * (REL_BUCKETS - max_exact)).astype(I32)
    return jnp.where(n < max_exact, n, jnp.minimum(large, REL_BUCKETS - 1))


def _relbias_kernel(tab_ref, bucket_ref, o_ref):
    h = pl.program_id(0)
    bk = bucket_ref[...]
    acc = jnp.full(bk.shape, -jnp.inf, F32)
    for b in range(REL_BUCKETS):
        acc = jnp.where(bk == b, tab_ref[b * SW_HEADS + h], acc)
    o_ref[0] = acc


def _relbias(rel_table, bucket):
    nq, ns = bucket.shape
    return pl.pallas_call(
        _relbias_kernel,
        grid=(SW_HEADS,),
        in_specs=[
            pl.BlockSpec(memory_space=pltpu.SMEM),
            pl.BlockSpec((nq, ns), lambda h: (0, 0)),
        ],
        out_specs=pl.BlockSpec((1, nq, ns), lambda h: (h, 0, 0)),
        out_shape=jax.ShapeDtypeStruct((SW_HEADS, nq, ns), F32),
        compiler_params=_cparams(("arbitrary",)),
        name="relbias",
    )(rel_table.reshape(-1), bucket)


def _dup_halves(x):
    lo = _lane_iota(x.shape) < SW_HD
    xr = pltpu.roll(x, SW_HD, axis=1)
    return jnp.where(lo, x, xr).astype(BF16), jnp.where(lo, xr, x).astype(BF16)


def _sink_softmax_pv(s, sink, vv):
    m = jnp.maximum(jnp.max(s, axis=-1, keepdims=True), sink)
    p = jnp.exp(s - m)
    denom = jnp.sum(p, axis=-1, keepdims=True) + jnp.exp(sink - m)
    return _dot(p, vv) * (1.0 / denom)


def _swa_prompt_kernel(sinks_ref, q_ref, kc_ref, kp_ref, vc_ref, vp_ref, bias_ref, o_ref, klast_ref, vlast_ref):
    klast_ref[0] = kc_ref[...]
    vlast_ref[0] = vc_ref[...]
    kk = _dup_halves(jnp.concatenate([kp_ref[...], kc_ref[...]], axis=0))
    vv = _dup_halves(jnp.concatenate([vp_ref[...], vc_ref[...]], axis=0))
    lo = _lane_iota((SW_BLOCK, LANES)) < SW_HD
    for pair in range(SW_HEADS // 2):
        qp = q_ref[:, pair * LANES:(pair + 1) * LANES] * (SW_HD ** -0.5)
        outs = []
        for half in range(2):
            hq = 2 * pair + half
            kv = hq // SW_GROUP
            qm = jnp.where(lo if half == 0 else ~lo, qp, 0.0)
            s = _dot_nt(qm, kk[kv]) + bias_ref[0, hq]
            outs.append(_sink_softmax_pv(s, sinks_ref[hq], vv[kv]))
        o_ref[:, pair * LANES:(pair + 1) * LANES] = jnp.where(lo, outs[0], outs[1])


def _swa_prompt(proj, sinks, bias, n_batch, seq):
    assert seq % SW_BLOCK == 0 and WINDOW == SW_BLOCK
    nb = seq // SW_BLOCK
    cur = lambda col: (lambda b, i: (b * nb + i, col))
    prev = lambda col: (lambda b, i: (b * nb + jnp.maximum(i - 1, 0), col))
    return pl.pallas_call(
        _swa_prompt_kernel,
        grid=(n_batch, nb),
        in_specs=[
            pl.BlockSpec(memory_space=pltpu.SMEM),
            pl.BlockSpec((SW_BLOCK, SW_HEADS * SW_HD), cur(COL_SQ // (SW_HEADS * SW_HD))),
            pl.BlockSpec((SW_BLOCK, SW_KV_W), cur(COL_SK // SW_KV_W)),
            pl.BlockSpec((SW_BLOCK, SW_KV_W), prev(COL_SK // SW_KV_W)),
            pl.BlockSpec((SW_BLOCK, SW_KV_W), cur(COL_SV // SW_KV_W)),
            pl.BlockSpec((SW_BLOCK, SW_KV_W), prev(COL_SV // SW_KV_W)),
            pl.BlockSpec((1, SW_HEADS, SW_BLOCK, 2 * SW_BLOCK), lambda b, i: (jnp.minimum(i, 1), 0, 0, 0)),
        ],
        out_specs=[
            pl.BlockSpec((SW_BLOCK, SW_HEADS * SW_HD), lambda b, i: (b * nb + i, 0)),
            pl.BlockSpec((1, SW_BLOCK, SW_KV_W), lambda b, i: (b, 0, 0)),
            pl.BlockSpec((1, SW_BLOCK, SW_KV_W), lambda b, i: (b, 0, 0)),
        ],
        out_shape=[
            jax.ShapeDtypeStruct((n_batch * seq, SW_HEADS * SW_HD), F32),
            jax.ShapeDtypeStruct((n_batch, SW_BLOCK, SW_KV_W), F32),
            jax.ShapeDtypeStruct((n_batch, SW_BLOCK, SW_KV_W), F32),
        ],
        compiler_params=_cparams(("arbitrary", "arbitrary")),
        name="swa_prompt",
    )(sinks, proj, proj, proj, proj, proj, bias)


def _swa_sample_kernel(q_ref, kn_ref, vn_ref, kc_ref, vc_ref, bias_ref, sink_ref,
                       o_ref, ko_ref, vo_ref, kall_ref, vall_ref, *, seq, n_bb, n_cache):
    n_keys = kall_ref.shape[0]
    zeros_tail = jnp.zeros((n_keys - n_cache - SUBLANES, LANES), F32)
    lo = _lane_iota((SUBLANES, LANES)) < SW_HD
    out = None
    for bb in range(n_bb):
        shift = (SUBLANES - bb * seq) % SUBLANES

        def top(x, shift=shift):
            return x if shift == 0 else pltpu.roll(x, shift, axis=0)

        kall_ref[0:n_cache, :] = kc_ref[bb]
        kall_ref[n_cache:n_cache + SUBLANES, :] = top(kn_ref[...])
        kall_ref[n_cache + SUBLANES:, :] = zeros_tail
        vall_ref[0:n_cache, :] = vc_ref[bb]
        vall_ref[n_cache:n_cache + SUBLANES, :] = top(vn_ref[...])
        vall_ref[n_cache + SUBLANES:, :] = zeros_tail
        ko_ref[bb] = kall_ref[seq:seq + n_cache, :]
        vo_ref[bb] = vall_ref[seq:seq + n_cache, :]
        kk = _dup_halves(kall_ref[...])
        vv = _dup_halves(vall_ref[...])
        q8 = top(q_ref[...])
        pairs = []
        for kv in range(SW_KV_HEADS):
            pieces = []
            for g in range(SW_GROUP):
                hq = kv * SW_GROUP + g
                qp = q8[:, (hq // 2) * LANES:(hq // 2 + 1) * LANES]
                pieces.append(jnp.where(lo if hq % 2 == 0 else ~lo, qp, 0.0))
            qs = jnp.concatenate(pieces, axis=0) * (SW_HD ** -0.5)
            s = _dot_nt(qs, kk[kv]) + bias_ref[kv]
            res = _sink_softmax_pv(s, _col(sink_ref[kv], 0), vv[kv])
            for g in range(0, SW_GROUP, 2):
                pairs.append(jnp.where(lo, res[g * SUBLANES:(g + 1) * SUBLANES],
                                       res[(g + 1) * SUBLANES:(g + 2) * SUBLANES]))
        o = jnp.concatenate(pairs, axis=1)
        back = (bb * seq) % SUBLANES
        o = o if back == 0 else pltpu.roll(o, back, axis=0)
        rows = _row_iota(o.shape)
        sel = (rows >= bb * seq) & (rows < (bb + 1) * seq)
        out = jnp.where(sel, o, 0.0 if out is None else out)
    o_ref[...] = out


def _swa_sample(proj, k_cache, v_cache, bias, sink_rows, row0, n_batch, seq):
    assert SUBLANES % seq == 0
    n_bb = SUBLANES // seq
    n_cache = k_cache.shape[1]
    assert n_batch % n_bb == 0 and row0 % SUBLANES == 0 and n_cache % SUBLANES == 0
    n_keys = bias.shape[-1]
    rb0 = row0 // SUBLANES
    blk = lambda col: (lambda i: (rb0 + i, col))
    return pl.pallas_call(
        functools.partial(_swa_sample_kernel, seq=seq, n_bb=n_bb, n_cache=n_cache),
        grid=(n_batch // n_bb,),
        in_specs=[
            pl.BlockSpec((SUBLANES, SW_HEADS * SW_HD), blk(COL_SQ // (SW_HEADS * SW_HD))),
            pl.BlockSpec((SUBLANES, SW_KV_W), blk(COL_SK // SW_KV_W)),
            pl.BlockSpec((SUBLANES, SW_KV_W), blk(COL_SV // SW_KV_W)),
            pl.BlockSpec((n_bb, n_cache, SW_KV_W), lambda i: (i, 0, 0)),
            pl.BlockSpec((n_bb, n_cache, SW_KV_W), lambda i: (i, 0, 0)),
            pl.BlockSpec((SW_KV_HEADS, SW_GROUP * SUBLANES, n_keys), lambda i: (0, 0, 0)),
            pl.BlockSpec((SW_KV_HEADS, SW_GROUP * SUBLANES, n_keys), lambda i: (0, 0, 0)),
        ],
        out_specs=[
            pl.BlockSpec((SUBLANES, SW_HEADS * SW_HD), lambda i: (i, 0)),
            pl.BlockSpec((n_bb, n_cache, SW_KV_W), lambda i: (i, 0, 0)),
            pl.BlockSpec((n_bb, n_cache, SW_KV_W), lambda i: (i, 0, 0)),
        ],
        out_shape=[
            jax.ShapeDtypeStruct((n_batch * seq, SW_HEADS * SW_HD), F32),
            jax.ShapeDtypeStruct(k_cache.shape, F32),
            jax.ShapeDtypeStruct(v_cache.shape, F32),
        ],
        scratch_shapes=[pltpu.VMEM((n_keys, SW_KV_W), F32), pltpu.VMEM((n_keys, SW_KV_W), F32)],
        compiler_params=_cparams(("arbitrary",)),
        name="swa_sample",
    )(proj, proj, proj, k_cache, v_cache, bias, sink_rows)


def _mix_kernel(xp_ref, xs_ref, oap_ref, oas_ref, obp_ref, obs_ref, ga_ref, gb_ref, wo_ref, nw_ref, wr_ref, br_ref,
                x1_ref, h2_ref, route_ref, *, n_prompt_tiles):
    i = pl.program_id(0)

    def run(x_ref, oa_ref, ob_ref):
        mixed = _sigmoid(ga_ref[...]) * oa_ref[...] + _sigmoid(gb_ref[...]) * ob_ref[...]
        x1 = x_ref[...] + _dot(mixed, wo_ref[...])
        x1_ref[...] = x1
        h2 = x1 * lax.rsqrt(jnp.mean(x1 * x1, axis=-1, keepdims=True) + EPS) * nw_ref[...]
        _to_tiles(h2_ref, h2)
        logits = _dot(h2, wr_ref[...]) + br_ref[...]
        lane = _lane_iota(logits.shape)
        lanef = lane.astype(F32)
        big = float(2 * LANES)
        is_g = lane < N_GROUPS
        gl = jnp.where(is_g, logits, -jnp.inf)
        gmax = jnp.max(gl, axis=-1, keepdims=True)
        gval = 1.0 / jnp.sum(jnp.where(is_g, jnp.exp(gl - gmax), 0.0), axis=-1, keepdims=True)
        grp = jnp.min(jnp.where(gl == gmax, lanef, big), axis=-1, keepdims=True)
        e_grp = ((lane - N_GROUPS) >> 3).astype(F32)
        is_e = (lane >= N_GROUPS) & (lane < N_GROUPS + N_EXPERTS) & (e_grp == grp)
        el = jnp.where(is_e, logits, -jnp.inf)
        v1 = jnp.max(el, axis=-1, keepdims=True)
        i1 = jnp.min(jnp.where(el == v1, lanef, big), axis=-1, keepdims=True)
        el2 = jnp.where(lanef == i1, -jnp.inf, el)
        v2 = jnp.max(el2, axis=-1, keepdims=True)
        i2 = jnp.min(jnp.where(el2 == v2, lanef, big), axis=-1, keepdims=True)
        e2 = jnp.exp(v2 - v1)
        w1 = gval / (1.0 + e2)
        w2 = gval * e2 / (1.0 + e2)
        route_ref[...] = jnp.where(lane == 0, i1 - N_GROUPS,
                                   jnp.where(lane == 1, i2 - N_GROUPS,
                                             jnp.where(lane == 2, w1, jnp.where(lane == 3, w2, 0.0))))

    @pl.when(i < n_prompt_tiles)
    def _():
        run(xp_ref, oap_ref, obp_ref)

    @pl.when(i >= n_prompt_tiles)
    def _():
        run(xs_ref, oas_ref, obs_ref)


def _mix(xp, xs, oa_p, oa_s, ob_p, ob_s, proj, w_out, norm_w, w_router, b_router):
    tp, ts = xp.shape[0], xs.shape[0]
    tm = _token_tile(tp, ts)
    npt, nst = tp // tm, ts // tm
    const = lambda i: (0, 0)
    row = lambda i: (i, 0)
    return pl.pallas_call(
        functools.partial(_mix_kernel, n_prompt_tiles=npt),
        grid=(npt + nst,),
        in_specs=[
            pl.BlockSpec((tm, D_MODEL), lambda i: (jnp.minimum(i, npt - 1), 0)),
            pl.BlockSpec((tm, D_MODEL), lambda i: (jnp.maximum(i - npt, 0), 0)),
            pl.BlockSpec((tm, D_MODEL), lambda i: (jnp.minimum(i, npt - 1), 0)),
            pl.BlockSpec((tm, D_MODEL), lambda i: (jnp.maximum(i - npt, 0), 0)),
            pl.BlockSpec((tm, D_MODEL), lambda i: (jnp.minimum(i, npt - 1), 0)),
            pl.BlockSpec((tm, D_MODEL), lambda i: (jnp.maximum(i - npt, 0), 0)),
            pl.BlockSpec((tm, D_MODEL), lambda i: (i, COL_GA // D_MODEL)),
            pl.BlockSpec((tm, D_MODEL), lambda i: (i, COL_GB // D_MODEL)),
            pl.BlockSpec((D_MODEL, D_MODEL), const),
            pl.BlockSpec((1, D_MODEL), const),
            pl.BlockSpec((D_MODEL, LANES), const),
            pl.BlockSpec((1, LANES), const),
        ],
        out_specs=[
            pl.BlockSpec((tm, D_MODEL), row),
            pl.BlockSpec((tm,) + TOK_TILE, lambda i: (i, 0, 0)),
            pl.BlockSpec((tm, LANES), row),
        ],
        out_shape=[
            jax.ShapeDtypeStruct((tp + ts, D_MODEL), F32),
            jax.ShapeDtypeStruct((tp + ts,) + TOK_TILE, F32),
            jax.ShapeDtypeStruct((tp + ts, LANES), F32),
        ],
        compiler_params=_cparams(("arbitrary",)),
        name="mix_router",
    )(xp, xs, oa_p, oa_s, ob_p, ob_s, proj, proj, w_out, norm_w, w_router, b_router)


def _rank_kernel(route_ref, dest_ref, meta_ref, rank_ref, cnt_ref, *, tile, blk):
    phase = pl.program_id(0)
    i = pl.program_id(1)
    shape = (tile, LANES)
    lane = _lane_iota(shape)
    lanef = lane.astype(F32)
    r = route_ref[...]
    oh0 = lanef == _col(r, 0)
    oh1 = lanef == _col(r, 1)
    rows = pl.ds(pl.multiple_of(i * tile, tile), tile)

    @pl.when(phase == 0)
    def _():
        @pl.when(i == 0)
        def _():
            cnt_ref[...] = jnp.zeros(cnt_ref.shape, F32)

        oh = jnp.where(oh0 | oh1, 1.0, 0.0)
        tri = jnp.where(_row_iota((tile, tile)) > _lane_iota((tile, tile)), 1.0, 0.0)
        before = _dot(tri, oh) + cnt_ref[0:1, :]
        rank0 = jnp.sum(jnp.where(oh0, before, 0.0), axis=-1, keepdims=True)
        rank1 = jnp.sum(jnp.where(oh1, before, 0.0), axis=-1, keepdims=True)
        rank_ref[rows, :] = jnp.where(lane == 0, rank0, jnp.where(lane == 1, rank1, 0.0))
        cnt_ref[0:1, :] = cnt_ref[0:1, :] + jnp.sum(oh, axis=0, keepdims=True)

    @pl.when(phase == 1)
    def _():
        cnt = cnt_ref[0:1, :]
        padded = jnp.floor((cnt + (blk - 1)) / blk) * blk
        before_lane = jnp.where(_row_iota((LANES, LANES)) < _lane_iota((LANES, LANES)), 1.0, 0.0)
        start = _dot_exact(jnp.broadcast_to(padded, (SUBLANES, LANES)), before_lane)[0:1, :]
        rk = rank_ref[rows, :]
        d0 = jnp.sum(jnp.where(oh0, start, 0.0), axis=-1, keepdims=True) + _col(rk, 0)
        d1 = jnp.sum(jnp.where(oh1, start, 0.0), axis=-1, keepdims=True) + _col(rk, 1)
        dest_ref[...] = jnp.where(lane == 0, d0, jnp.where(lane == 1, d1, 0.0)).astype(I32)

        @pl.when(i == 0)
        def _():
            end = start + padded
            mshape = meta_ref.shape
            blk_start = (_row_iota(mshape) * blk).astype(F32)
            hit = (_lane_iota(mshape) < N_EXPERTS) & (end <= blk_start)
            be = jnp.minimum(jnp.sum(jnp.where(hit, 1.0, 0.0), axis=-1, keepdims=True), N_EXPERTS - 1.0)
            n_used = _col(end, N_EXPERTS - 1) / blk
            ml = _lane_iota(mshape)
            mine = ml.astype(F32) == be
            seg_start = jnp.sum(jnp.where(mine, start, 0.0), axis=-1, keepdims=True)
            seg_count = jnp.sum(jnp.where(mine, cnt, 0.0), axis=-1, keepdims=True)
            n_valid = jnp.clip(seg_count - (blk_start[:, 0:1] - seg_start), 0.0, float(blk))
            meta_ref[...] = jnp.where(ml == 0, be, jnp.where(ml == 1, n_used,
                                                              jnp.where(ml == 2, n_valid, 0.0))).astype(I32)


def _rank(route, tile, blk, n_blocks):
    t = route.shape[0]
    nt = t // tile
    nbp = -(-n_blocks // SUBLANES) * SUBLANES
    return pl.pallas_call(
        functools.partial(_rank_kernel, tile=tile, blk=blk),
        grid=(2, nt),
        in_specs=[pl.BlockSpec((tile, LANES), lambda p, i: (i, 0))],
        out_specs=[
            pl.BlockSpec((tile, LANES), lambda p, i: (i * p, 0)),
            pl.BlockSpec((nbp, LANES), lambda p, i: (0, 0)),
        ],
        out_shape=[
            jax.ShapeDtypeStruct((t, LANES), I32),
            jax.ShapeDtypeStruct((nbp, LANES), I32),
        ],
        scratch_shapes=[pltpu.VMEM((t, LANES), F32), pltpu.VMEM((SUBLANES, LANES), F32)],
        compiler_params=_cparams(("arbitrary", "arbitrary")),
        name="moe_rank",
    )(route)


TOK_TILE = (D_MODEL // LANES, LANES)


def _to_tiles(ref, x):
    for j in range(TOK_TILE[0]):
        ref[:, j, :] = x[:, j * LANES:(j + 1) * LANES]


def _from_tiles(ref):
    return jnp.concatenate([ref[:, j, :] for j in range(TOK_TILE[0])], axis=1)


def _row_copy(src, src_row, dst, dst_row, sem):
    if len(src.shape) == 2:
        return pltpu.make_async_copy(src.at[pl.ds(src_row, 1)], dst.at[pl.ds(dst_row, 1)], sem)
    return pltpu.make_async_copy(src.at[src_row], dst.at[dst_row], sem)


def _invert_kernel(dest_ref, inv_ref):
    def clear(j, carry):
        inv_ref[j] = 0
        return carry

    def place(a, carry):
        inv_ref[dest_ref[a]] = a >> 1
        return carry

    lax.fori_loop(0, inv_ref.shape[0], clear, 0, unroll=8)
    lax.fori_loop(0, dest_ref.shape[0], place, 0, unroll=8)


def _invert(dest_flat, n_slots):
    return pl.pallas_call(
        _invert_kernel,
        in_specs=[pl.BlockSpec(memory_space=pltpu.SMEM)],
        out_specs=pl.BlockSpec(memory_space=pltpu.SMEM),
        out_shape=jax.ShapeDtypeStruct((n_slots,), I32),
        name="moe_invert",
    )(dest_flat)


def _last_used(i, nu_ref):
    return jnp.minimum(i, jnp.maximum(nu_ref[0] - 1, 0))


def _for_rows(n, body):
    def group(g, carry):
        for u in range(SUBLANES):
            body(g * SUBLANES + u)
        return carry

    def single(t, carry):
        body(t)
        return carry

    n_groups = n // SUBLANES
    lax.fori_loop(0, n_groups, group, 0)
    lax.fori_loop(n_groups * SUBLANES, n, single, 0)


def _expert_kernel(be_ref, nu_ref, nv_ref, inv_ref, h2_ref, wg_ref, wu_ref, wd_ref, y_ref, xbuf_ref, wgu_ref, wdn_ref, sem,
                   *, blk):
    i = pl.program_id(0)
    n_used = nu_ref[0]
    used = i < n_used
    slot = i % 2
    blk_i = _last_used(i, nu_ref)
    fresh = (i == 0) | (be_ref[blk_i] != be_ref[jnp.maximum(blk_i - 1, 0)])

    def gather(block, into):
        _for_rows(nv_ref[block],
                  lambda t: _row_copy(h2_ref, inv_ref[block * blk + t], xbuf_ref.at[into], t, sem.at[into]).start())

    @pl.when(i == 0)
    def _():
        xbuf_ref[...] = jnp.zeros(xbuf_ref.shape, F32)
        gather(0, 0)

    @pl.when(i + 1 < n_used)
    def _():
        gather(i + 1, 1 - slot)

    @pl.when(used & fresh)
    def _():
        wgu_ref[:, :D_EXPERT] = wg_ref[0].astype(BF16)
        wgu_ref[:, D_EXPERT:] = wu_ref[0].astype(BF16)
        wdn_ref[...] = wd_ref[0].astype(BF16)

    @pl.when(used)
    def _():
        _for_rows(nv_ref[i], lambda t: _row_copy(h2_ref, 0, xbuf_ref.at[slot], 0, sem.at[slot]).wait())
        gu = jnp.dot(_from_tiles(xbuf_ref.at[slot]).astype(BF16), wgu_ref[...], preferred_element_type=F32)
        g = gu[:, :D_EXPERT]
        hidden = (g * _sigmoid(g) * gu[:, D_EXPERT:]).astype(BF16)
        y_ref[...] = jnp.dot(hidden, wdn_ref[...], preferred_element_type=F32)

    @pl.when(jnp.logical_not(used))
    def _():
        y_ref[...] = jnp.zeros(y_ref.shape, F32)


def _experts(block_expert, n_used, n_valid, inv, h2, w_gate, w_up, w_down, blk):
    n_blocks = inv.shape[0] // blk
    wsel = lambda i, be, nu, nv, iv: (be[_last_used(i, nu)], 0, 0)
    return pl.pallas_call(
        functools.partial(_expert_kernel, blk=blk),
        grid_spec=pltpu.PrefetchScalarGridSpec(
            num_scalar_prefetch=4,
            grid=(n_blocks,),
            in_specs=[
                pl.BlockSpec(memory_space=pl.ANY),
                pl.BlockSpec((1, D_MODEL, D_EXPERT), wsel),
                pl.BlockSpec((1, D_MODEL, D_EXPERT), wsel),
                pl.BlockSpec((1, D_EXPERT, D_MODEL), wsel),
            ],
            out_specs=pl.BlockSpec((blk, D_MODEL), lambda i, be, nu, nv, iv: (i, 0)),
            scratch_shapes=[
                pltpu.VMEM((2, blk) + TOK_TILE, F32),
                pltpu.VMEM((D_MODEL, 2 * D_EXPERT), BF16),
                pltpu.VMEM((D_EXPERT, D_MODEL), BF16),
                pltpu.SemaphoreType.DMA((2,)),
            ],
        ),
        out_shape=jax.ShapeDtypeStruct((n_blocks * blk, D_MODEL), F32),
        compiler_params=_cparams(("arbitrary",)),
        name="moe_experts",
    )(block_expert, n_used, n_valid, inv, h2, w_gate, w_up, w_down)


def _combine_kernel(dest_ref, x1_ref, route_ref, nw_ref, ys_ref, yp_ref, ysm_ref, ybuf_ref, sem,
                    *, tile, n_prompt_tiles):
    i = pl.program_id(0)
    slot = i % 2

    def gather(step, into):
        def issue(t, carry):
            for k in range(2):
                _row_copy(ys_ref, dest_ref[2 * (step * tile + t) + k], ybuf_ref.at[into], k * tile + t,
                          sem.at[into]).start()
            return carry

        lax.fori_loop(0, tile, issue, 0, unroll=4)

    @pl.when(i == 0)
    def _():
        gather(0, 0)

    @pl.when(i + 1 < pl.num_programs(0))
    def _():
        gather(i + 1, 1 - slot)

    def drain(t, carry):
        _row_copy(ys_ref, 0, ybuf_ref.at[slot], 0, sem.at[slot]).wait()
        return carry

    lax.fori_loop(0, 2 * tile, drain, 0, unroll=8)
    r = route_ref[...]
    ybuf = ybuf_ref.at[slot]
    y = ybuf[0:tile, :] * _col(r, 2) + ybuf[tile:2 * tile, :] * _col(r, 3)
    x2 = x1_ref[...] + y
    out = x2 * lax.rsqrt(jnp.mean(x2 * x2, axis=-1, keepdims=True) + EPS) * nw_ref[...]

    @pl.when(i < n_prompt_tiles)
    def _():
        yp_ref[...] = out

    @pl.when(i >= n_prompt_tiles)
    def _():
        ysm_ref[...] = out


def _combine(dest_flat, x1, route, norm_w, ys, tp, ts):
    tile = _token_tile(tp, ts)
    npt, nst = tp // tile, ts // tile
    return pl.pallas_call(
        functools.partial(_combine_kernel, tile=tile, n_prompt_tiles=npt),
        grid_spec=pltpu.PrefetchScalarGridSpec(
            num_scalar_prefetch=1,
            grid=(npt + nst,),
            in_specs=[
                pl.BlockSpec((tile, D_MODEL), lambda i, d: (i, 0)),
                pl.BlockSpec((tile, LANES), lambda i, d: (i, 0)),
                pl.BlockSpec((1, D_MODEL), lambda i, d: (0, 0)),
                pl.BlockSpec(memory_space=pl.ANY),
            ],
            out_specs=[
                pl.BlockSpec((tile, D_MODEL), lambda i, d: (jnp.minimum(i, npt - 1), 0)),
                pl.BlockSpec((tile, D_MODEL), lambda i, d: (jnp.maximum(i - npt, 0), 0)),
            ],
            scratch_shapes=[pltpu.VMEM((2, 2 * tile, D_MODEL), F32), pltpu.SemaphoreType.DMA((2,))],
        ),
        out_shape=[
            jax.ShapeDtypeStruct((tp, D_MODEL), F32),
            jax.ShapeDtypeStruct((ts, D_MODEL), F32),
        ],
        compiler_params=_cparams(("arbitrary",)),
        name="moe_combine",
    )(dest_flat, x1, route, norm_w, ys)


def _layer(xp, xs, n_batch, seq, s_batch, s_seq, conv_state, dn_state, k_cache, v_cache,
           w_in, conv_w, a_log, dt_bias, dn_norm_w, sinks, rel_bias, w_out, norm_mix_w, norm_ffn_w,
           w_rg, b_rg, w_re, b_re, w_gate, w_up, w_down, norm_final_w):
    tp, ts = xp.shape[0], xs.shape[0]
    t_all = tp + ts
    row = lambda v: v.reshape(1, -1).astype(F32)

    o = np.cumsum((0, DN_QK_W, DN_QK_W, DN_V_W, DN_V_W, DN_HEADS, DN_HEADS, SW_HEADS * SW_HD, SW_KV_W, SW_KV_W,
                   D_MODEL, D_MODEL)).tolist()
    w_big = jnp.concatenate([w_in[:, o[0]:o[4]], w_in[:, o[6]:o[7]], w_in[:, o[9]:o[11]], w_in[:, o[7]:o[9]]],
                            axis=1).astype(BF16)
    w_small = jnp.pad(w_in[:, o[4]:o[6]], ((0, 0), (0, LANES - 2 * DN_HEADS))).astype(BF16)
    head_row = lambda v: jnp.pad(v.astype(F32), (DN_HEADS, LANES - 2 * DN_HEADS)).reshape(1, LANES)
    w_router = jnp.pad(jnp.concatenate([w_rg, w_re], axis=1),
                       ((0, 0), (0, LANES - N_GROUPS - N_EXPERTS))).astype(BF16)
    b_router = jnp.pad(jnp.concatenate([b_rg, b_re]).astype(F32), (0, LANES - N_GROUPS - N_EXPERTS)).reshape(1, LANES)

    conv0 = jnp.zeros((n_batch, DN_CONV - 1, DN_CONV_W), F32)
    proj, ba, conv_tail = _inproj(xp, xs, row(norm_mix_w), w_big, w_small, conv_w.astype(F32), conv0, seq)
    p_conv = conv_tail[:, SUBLANES - (DN_CONV - 1):, :]

    dn0 = jnp.zeros((n_batch, DN_HEADS, DN_DK, DN_DV), F32)
    oa_p, p_dn = _dn_prompt(proj, ba, head_row(a_log), head_row(dt_bias), row(dn_norm_w), dn0, n_batch, seq)
    oa_s, s_conv, s_dn = _dn_sample(proj, ba, conv_w.astype(F32), head_row(a_log), head_row(dt_bias), row(dn_norm_w),
                                    conv_state, dn_state, tp, s_batch, s_seq)

    qpos = jnp.arange(SW_BLOCK)[:, None]
    kpos = jnp.arange(2 * SW_BLOCK)[None, :] - SW_BLOCK
    in_window = lambda dist: (dist >= 0) & (dist < WINDOW)
    masked_bucket = lambda dist, ok: jnp.where(ok, _rel_bucket(dist), -1)
    bias_p = jnp.stack([
        _relbias(rel_bias.astype(F32), masked_bucket(qpos - kpos, in_window(qpos - kpos) & (kpos >= 0))),
        _relbias(rel_bias.astype(F32), masked_bucket(qpos - kpos, in_window(qpos - kpos)))])
    ob_p, p_k, p_v = _swa_prompt(proj, sinks.astype(F32), bias_p, n_batch, seq)
    n_cache = k_cache.shape[1]
    n_keys = -(-(n_cache + SUBLANES) // LANES) * LANES
    tq = n_cache + jnp.arange(SUBLANES)[:, None]
    dist_s = tq - jnp.arange(n_keys)[None, :]
    bias_s = _relbias(rel_bias.astype(F32), masked_bucket(dist_s, in_window(dist_s)))
    bias_s = bias_s.reshape(SW_KV_HEADS, SW_GROUP * SUBLANES, n_keys)
    sink_rows = jnp.broadcast_to(jnp.repeat(sinks.astype(F32).reshape(SW_KV_HEADS, SW_GROUP), SUBLANES, axis=1)[:, :, None],
                                 (SW_KV_HEADS, SW_GROUP * SUBLANES, n_keys))
    ob_s, s_k, s_v = _swa_sample(proj, k_cache.reshape(s_batch, n_cache, SW_KV_W), v_cache.reshape(s_batch, n_cache, SW_KV_W),
                                 bias_s, sink_rows, tp, s_batch, s_seq)

    x1, h2, route = _mix(xp, xs, oa_p, oa_s, ob_p, ob_s, proj, w_out.astype(BF16), row(norm_ffn_w), w_router, b_router)

    tile = _token_tile(tp, ts)
    n_blocks = -(-2 * t_all // MOE_BLOCK) + N_EXPERTS
    dest, meta = _rank(route, _token_tile(t_all, cands=(512, 256, 128, 64, 32, 16, 8)), MOE_BLOCK, n_blocks)
    dest_flat = dest[:, :2].reshape(-1)
    block_expert = meta[:n_blocks, 0]
    n_used = meta[0:1, 1]
    n_valid = meta[:n_blocks, 2]
    inv = _invert(dest_flat, n_blocks * MOE_BLOCK)
    ys = _experts(block_expert, n_used, n_valid, inv, h2, w_gate, w_up, w_down, MOE_BLOCK)
    y_p, y_s = _combine(dest_flat, x1, route, row(norm_final_w), ys, tp, ts)

    kv_shape = (n_batch, WINDOW, SW_KV_HEADS, SW_HD)
    return (y_p, y_s, p_conv, p_dn, p_k.reshape(kv_shape), p_v.reshape(kv_shape), s_conv, s_dn,
            s_k.reshape(k_cache.shape), s_v.reshape(v_cache.shape))


def kernel(x_prompt, x_sample, state_dn_conv, state_dn, cache_swa_k, cache_swa_v, w_in, conv_w, a_log, dt_bias, dn_norm_w, sinks, rel_bias, w_out, norm_mix_w, norm_ffn_w, w_router_group, b_router_group, w_router_expert, b_router_expert, w_gate, w_up, w_down, norm_final_w):
    depth = w_in.shape[0]
    assert depth == 1, "the final-norm fusion below assumes a single layer"
    n_batch, seq, _ = x_prompt.shape
    s_batch, s_seq, _ = x_sample.shape
    outs = _layer(x_prompt.reshape(-1, D_MODEL), x_sample.reshape(-1, D_MODEL), n_batch, seq, s_batch, s_seq,
                  state_dn_conv[0], state_dn[0], cache_swa_k[0], cache_swa_v[0],
                  w_in[0], conv_w[0], a_log[0], dt_bias[0], dn_norm_w[0], sinks[0], rel_bias,
                  w_out[0], norm_mix_w[0], norm_ffn_w[0], w_router_group[0], b_router_group[0],
                  w_router_expert[0], b_router_expert[0], w_gate[0], w_up[0], w_down[0], norm_final_w)
    y_p, y_s, p_conv, p_dn, p_k, p_v, s_conv, s_dn, s_k, s_v = outs
    return (y_p.reshape(x_prompt.shape), y_s.reshape(x_sample.shape), p_conv[None], p_dn[None], p_k[None], p_v[None],
            s_conv[None], s_dn[None], s_k[None], s_v[None])
```

```python
import functools
import math

import jax
import jax.numpy as jnp
import numpy as np
from jax import lax
from jax.experimental import pallas as pl
from jax.experimental.pallas import tpu as pltpu

F32 = jnp.float32
BF16 = jnp.bfloat16
I32 = jnp.int32

D_MODEL = 1024
DN_HEADS = 8
DN_DK = 128
DN_DV = 128
DN_CONV = 4
DN_CHUNK = 64
DN_QK_W = DN_HEADS * DN_DK
DN_V_W = DN_HEADS * DN_DV
DN_CONV_W = 2 * DN_QK_W + DN_V_W
SW_HEADS = 16
SW_KV_HEADS = 2
SW_GROUP = SW_HEADS // SW_KV_HEADS
SW_HD = 64
SW_KV_W = SW_KV_HEADS * SW_HD
WINDOW = 128
SW_BLOCK = 128
REL_BUCKETS = 32
REL_MAX_DIST = 128
N_GROUPS = 8
EXP_PER_GROUP = 8
N_EXPERTS = N_GROUPS * EXP_PER_GROUP
D_EXPERT = 256
MOE_BLOCK = 256
EPS = 1e-6

LANES = 128
SUBLANES = 8
VMEM_LIMIT = 56 * 1024 * 1024

COL_QKV = 0
COL_Z = 3072
COL_SQ = 4096
COL_GA = 5120
COL_GB = 6144
COL_SK = 7168
COL_SV = 7296
PROJ_W = 7424
PROJ_CHUNK = 512


def _cparams(sem):
    return pltpu.CompilerParams(dimension_semantics=sem, vmem_limit_bytes=VMEM_LIMIT)


def _sigmoid(x):
    return 0.5 * jnp.tanh(0.5 * x) + 0.5


def _dot(a, b):
    return jnp.dot(a.astype(BF16), b.astype(BF16), preferred_element_type=F32)


def _dot_nt(a, b):
    return lax.dot_general(a.astype(BF16), b.astype(BF16), (((1,), (1,)), ((), ())), preferred_element_type=F32)


def _dot_tn(a, b):
    return lax.dot_general(a.astype(BF16), b.astype(BF16), (((0,), (0,)), ((), ())), preferred_element_type=F32)


def _dot_exact(a, b):
    return jnp.dot(a, b, precision=lax.Precision.HIGHEST, preferred_element_type=F32)


def _lane_iota(shape):
    return lax.broadcasted_iota(I32, shape, len(shape) - 1)


def _row_iota(shape):
    return lax.broadcasted_iota(I32, shape, len(shape) - 2)


def _col(x, j):
    return jnp.sum(jnp.where(_lane_iota(x.shape) == j, x, 0.0), axis=-1, keepdims=True)


def _token_tile(*sizes, cands=(256, 128, 64, 32, 16, 8)):
    for t in cands:
        if all(s % t == 0 for s in sizes):
            return t
    raise ValueError(f"token counts {sizes} need a common tile that is a multiple of 8")


def _inproj_kernel(xp_ref, xs_ref, nw_ref, wb_ref, ws_ref, convw_ref, conv0_ref, proj_ref, ba_ref, tail_ref, cbuf_ref,
                   *, n_prompt_tiles, tiles_per_seq):
    i = pl.program_id(0)
    tm = xp_ref.shape[0]
    hist = SUBLANES - (DN_CONV - 1)

    def project(x_ref, conv):
        x = x_ref[...]
        h = (x * lax.rsqrt(jnp.mean(x * x, axis=-1, keepdims=True) + EPS) * nw_ref[...]).astype(BF16)
        ba_ref[...] = jnp.dot(h, ws_ref[...], preferred_element_type=F32)
        top = _row_iota((SUBLANES, PROJ_CHUNK))
        for c0 in range(0, PROJ_W, PROJ_CHUNK):
            c1 = min(c0 + PROJ_CHUNK, PROJ_W)
            cur = jnp.dot(h, wb_ref[:, c0:c1], preferred_element_type=F32)
            if conv and c1 <= COL_QKV + DN_CONV_W:
                prev = cbuf_ref[:, c0:c1]
                acc = cur * convw_ref[DN_CONV - 1:DN_CONV, c0:c1]
                for s in range(1, DN_CONV):
                    sh = pltpu.roll(cur, s, axis=0)
                    head = jnp.where(top < s, pltpu.roll(prev, s, axis=0), sh[:SUBLANES])
                    sh = jnp.concatenate([head, sh[SUBLANES:]], axis=0)
                    acc = acc + sh * convw_ref[DN_CONV - 1 - s:DN_CONV - s, c0:c1]
                cbuf_ref[:, c0:c1] = cur[tm - SUBLANES:]
                cur = acc * _sigmoid(acc)
            proj_ref[:, c0:c1] = cur

    @pl.when(i < n_prompt_tiles)
    def _():
        @pl.when(i % tiles_per_seq == 0)
        def _():
            cbuf_ref[...] = jnp.zeros(cbuf_ref.shape, F32)
            cbuf_ref[hist:SUBLANES, :] = conv0_ref[0]

        project(xp_ref, True)
        tail_ref[0] = cbuf_ref[...]

    @pl.when(i >= n_prompt_tiles)
    def _():
        project(xs_ref, False)


def _inproj(xp, xs, norm_w, w_big, w_small, conv_w, conv0, seq):
    tp, ts = xp.shape[0], xs.shape[0]
    tm = _token_tile(tp, ts, seq)
    assert COL_QKV == 0 and DN_CONV_W % PROJ_CHUNK == 0
    npt, nst = tp // tm, ts // tm
    tps = seq // tm
    const = lambda i: (0, 0)
    seq_of = lambda i: (jnp.minimum(i, npt - 1) // tps, 0, 0)
    return pl.pallas_call(
        functools.partial(_inproj_kernel, n_prompt_tiles=npt, tiles_per_seq=tps),
        grid=(npt + nst,),
        in_specs=[
            pl.BlockSpec((tm, D_MODEL), lambda i: (jnp.minimum(i, npt - 1), 0)),
            pl.BlockSpec((tm, D_MODEL), lambda i: (jnp.maximum(i - npt, 0), 0)),
            pl.BlockSpec((1, D_MODEL), const),
            pl.BlockSpec((D_MODEL, PROJ_W), const, pipeline_mode=pl.Buffered(1)),
            pl.BlockSpec((D_MODEL, LANES), const),
            pl.BlockSpec((DN_CONV, DN_CONV_W), const),
            pl.BlockSpec((1, DN_CONV - 1, DN_CONV_W), seq_of),
        ],
        out_specs=[
            pl.BlockSpec((tm, PROJ_W), lambda i: (i, 0)),
            pl.BlockSpec((tm, LANES), lambda i: (i, 0)),
            pl.BlockSpec((1, SUBLANES, DN_CONV_W), seq_of),
        ],
        out_shape=[
            jax.ShapeDtypeStruct((tp + ts, PROJ_W), F32),
            jax.ShapeDtypeStruct((tp + ts, LANES), F32),
            jax.ShapeDtypeStruct((tp // seq, SUBLANES, DN_CONV_W), F32),
        ],
        scratch_shapes=[pltpu.VMEM((SUBLANES, DN_CONV_W), F32)],
        compiler_params=_cparams(("arbitrary",)),
        name="inproj",
    )(xp, xs, norm_w, w_big, w_small, conv_w, conv0)


def _dn_core(groups, alog, dtb, nw, read_state, write_state, n_seg, seg_valid):
    rows = groups[0][0].shape[0]
    sr = rows // n_seg
    assert sr * n_seg == rows and sr & (sr - 1) == 0 and rows <= LANES
    seg_shift = sr.bit_length() - 1
    ri = _row_iota((rows, rows))
    ci = _lane_iota((rows, rows))
    incl = ri >= ci
    strict = ri > ci
    if n_seg > 1:
        same = (ri >> seg_shift) == (ci >> seg_shift)
        incl = incl & same
        strict = strict & same
    l_incl = incl.astype(F32)
    eye = (ri == ci).astype(F32)
    levels = max(1, math.ceil(math.log2(seg_valid)))

    beta_all, gsum_all, gtot_all, gsum_t = [], [], [], []
    for _, _, ba in groups:
        b_all = _sigmoid(ba)
        sp = ba + dtb
        softplus = jnp.maximum(sp, 0.0) + jnp.log1p(jnp.exp(-jnp.abs(sp)))
        g_all = -jnp.exp(alog) * softplus
        if seg_valid < sr:
            live = (_row_iota((rows, LANES)) & (sr - 1)) < seg_valid
            b_all = jnp.where(live, b_all, 0.0)
            g_all = jnp.where(live, g_all, 0.0)
        gs = _dot_exact(l_incl, g_all)
        beta_all.append(b_all)
        gsum_all.append(gs)
        gtot_all.append(_dot_exact(same.astype(F32), g_all) if n_seg > 1 else gs[rows - 1:rows, :])
        padded = gs if rows == LANES else jnp.concatenate([gs, jnp.zeros((LANES - rows, LANES), F32)], axis=0)
        gsum_t.append(padded.T)

    probs = [(g, h) for g in range(len(groups)) for h in range(DN_HEADS)]
    segs = range(n_seg)
    q, k, v, kb, beta, gsum, gtot = {}, {}, {}, {}, {}, {}, {}
    for p in probs:
        g, h = p
        qkv = groups[g][0]
        qh = qkv[:, h * DN_DK:(h + 1) * DN_DK]
        kh = qkv[:, DN_QK_W + h * DN_DK:DN_QK_W + (h + 1) * DN_DK]
        v[p] = qkv[:, 2 * DN_QK_W + h * DN_DV:2 * DN_QK_W + (h + 1) * DN_DV]
        q[p] = qh * lax.rsqrt(jnp.sum(qh * qh, axis=-1, keepdims=True) + 1e-6) * (DN_DK ** -0.5)
        k[p] = kh * lax.rsqrt(jnp.sum(kh * kh, axis=-1, keepdims=True) + 1e-6)
        beta[p] = _col(beta_all[g], h)
        gsum[p] = _col(gsum_all[g], DN_HEADS + h)
        gtot[p] = _col(gtot_all[g], DN_HEADS + h)
        kb[p] = k[p] * beta[p]
    kq = {p: _dot_nt(jnp.concatenate([kb[p], q[p]], axis=0), k[p]) for p in probs}
    gamma = {(g, h): jnp.exp(jnp.where(incl, gsum[(g, h)] - gsum_t[g][DN_HEADS + h:DN_HEADS + h + 1, :rows], -jnp.inf))
             for g, h in probs}
    attn = {p: kq[p][rows:] * gamma[p] for p in probs}
    pw = {p: -jnp.where(strict, kq[p][:rows] * gamma[p], 0.0) for p in probs}
    t = {p: eye + pw[p] for p in probs}
    for _ in range(1, levels):
        pw = {p: _dot(pw[p], pw[p]) for p in probs}
        t = {p: t[p] + _dot(t[p], pw[p]) for p in probs}
    eg = {p: jnp.exp(gsum[p]) for p in probs}
    uw = {p: _dot(t[p], jnp.concatenate([v[p] * beta[p], kb[p] * eg[p]], axis=1)) for p in probs}
    qg = {p: q[p] * eg[p] for p in probs}
    state = {(p, s): read_state(p[0], s, p[1]) for p in probs for s in segs}
    wq = {(p, s): _dot(jnp.concatenate([uw[p][s * sr:(s + 1) * sr, DN_DV:], qg[p][s * sr:(s + 1) * sr]], axis=0),
                       state[(p, s)]) for p in probs for s in segs}
    join = lambda pieces: pieces[0] if len(pieces) == 1 else jnp.concatenate(pieces, axis=0)
    v_new = {p: uw[p][:, :DN_DV] - join([wq[(p, s)][:sr] for s in segs]) for p in probs}
    o = {p: join([wq[(p, s)][sr:] for s in segs]) + _dot(attn[p], v_new[p]) for p in probs}
    kd = {p: k[p] * jnp.exp(gtot[p] - gsum[p]) for p in probs}
    for p in probs:
        for s in segs:
            r0 = s * sr if n_seg > 1 else 0
            decay = jnp.exp(gtot[p][r0:r0 + 1, :])
            write_state(p[0], s, p[1],
                        state[(p, s)] * decay + _dot_tn(kd[p][s * sr:(s + 1) * sr], v_new[p][s * sr:(s + 1) * sr]))
    outs = []
    for g, (_, z, _) in enumerate(groups):
        heads = []
        for h in range(DN_HEADS):
            oh = o[(g, h)]
            zz = z[:, h * DN_DV:(h + 1) * DN_DV]
            on = oh * lax.rsqrt(jnp.mean(oh * oh, axis=-1, keepdims=True) + EPS) * nw
            heads.append(on * (zz * _sigmoid(zz)))
        outs.append(jnp.concatenate(heads, axis=1))
    return outs


def _dn_prompt_kernel(*refs, chunk, n_batch):
    nb = n_batch
    qkv_refs, z_refs, ba_refs = refs[0:nb], refs[nb:2 * nb], refs[2 * nb:3 * nb]
    alog_ref, dtb_ref, nw_ref, s0_ref, o_ref, sout_ref = refs[3 * nb:]

    @pl.when(pl.program_id(0) == 0)
    def _():
        sout_ref[...] = s0_ref[...]

    groups = [(qkv_refs[b][...], z_refs[b][...], ba_refs[b][...]) for b in range(nb)]

    def read_state(g, s, h):
        return sout_ref[g, h]

    def write_state(g, s, h, val):
        sout_ref[g, h] = val

    outs = _dn_core(groups, alog_ref[...], dtb_ref[...], nw_ref[...], read_state, write_state, 1, chunk)
    for b in range(nb):
        o_ref[b] = outs[b]


def _dn_prompt(proj, ba, alog_row, dtb_row, dn_nw, s0, n_batch, seq):
    chunk = min(DN_CHUNK, seq)
    assert seq % chunk == 0 and chunk % SUBLANES == 0
    nc = seq // chunk
    const2 = lambda c: (0, 0)
    rows = lambda b, col: (lambda c: (b * nc + c, col))
    batches = range(n_batch)
    o, s_out = pl.pallas_call(
        functools.partial(_dn_prompt_kernel, chunk=chunk, n_batch=n_batch),
        grid=(nc,),
        in_specs=(
            [pl.BlockSpec((chunk, DN_CONV_W), rows(b, COL_QKV // DN_CONV_W)) for b in batches]
            + [pl.BlockSpec((chunk, DN_V_W), rows(b, COL_Z // DN_V_W)) for b in batches]
            + [pl.BlockSpec((chunk, LANES), rows(b, 0)) for b in batches]
            + [
                pl.BlockSpec((1, LANES), const2),
                pl.BlockSpec((1, LANES), const2),
                pl.BlockSpec((1, DN_DV), const2),
                pl.BlockSpec((n_batch, DN_HEADS, DN_DK, DN_DV), lambda c: (0, 0, 0, 0)),
            ]
        ),
        out_specs=[
            pl.BlockSpec((n_batch, chunk, DN_V_W), lambda c: (0, c, 0)),
            pl.BlockSpec((n_batch, DN_HEADS, DN_DK, DN_DV), lambda c: (0, 0, 0, 0)),
        ],
        out_shape=[
            jax.ShapeDtypeStruct((n_batch, seq, DN_V_W), F32),
            jax.ShapeDtypeStruct((n_batch, DN_HEADS, DN_DK, DN_DV), F32),
        ],
        compiler_params=_cparams(("arbitrary",)),
        name="dn_prompt",
    )(*([proj] * n_batch), *([proj] * n_batch), *([ba] * n_batch), alog_row, dtb_row, dn_nw, s0)
    return o.reshape(n_batch * seq, DN_V_W), s_out


def _dn_sample_kernel(qkv_ref, z_ref, ba_ref, convw_ref, alog_ref, dtb_ref, nw_ref, conv0_ref, s0_ref,
                      o_ref, convout_ref, sout_ref, cbuf_ref, *, seq, n_bb):
    hist = SUBLANES - (DN_CONV - 1)
    per = SUBLANES // seq
    n_tiles = n_bb // per

    def spread(ref):
        pieces = []
        for j in range(n_tiles):
            x8 = ref[j * SUBLANES:(j + 1) * SUBLANES, :]
            for r in range(per):
                pieces.append(x8 if r == 0 else pltpu.roll(x8, SUBLANES - r * seq, axis=0))
        return pieces

    for bb, piece in enumerate(spread(qkv_ref)):
        cbuf_ref[bb, SUBLANES:2 * SUBLANES, :] = piece
    cbuf_ref[:, hist:SUBLANES, :] = conv0_ref[...]
    w = convw_ref[...]
    acc = cbuf_ref[:, hist:hist + SUBLANES, :] * w[0:1, :]
    for i in range(1, DN_CONV):
        acc = acc + cbuf_ref[:, hist + i:hist + i + SUBLANES, :] * w[i:i + 1, :]
    live = _row_iota(acc.shape) < seq
    qkv = jnp.where(live, acc * _sigmoid(acc), 0.0).reshape(n_bb * SUBLANES, DN_CONV_W)
    convout_ref[...] = cbuf_ref[:, SUBLANES + seq - (DN_CONV - 1):SUBLANES + seq, :]

    def read_state(g, s, h):
        return s0_ref[s, h]

    def write_state(g, s, h, val):
        sout_ref[s, h] = val

    group = (qkv, jnp.concatenate(spread(z_ref), axis=0), jnp.concatenate(spread(ba_ref), axis=0))
    o = _dn_core([group], alog_ref[...], dtb_ref[...], nw_ref[...], read_state, write_state, n_bb, seq)[0]
    rows = _row_iota((SUBLANES, DN_V_W))
    for j in range(n_tiles):
        tile = None
        for r in range(per):
            bb = j * per + r
            piece = o[bb * SUBLANES:(bb + 1) * SUBLANES]
            piece = piece if r == 0 else pltpu.roll(piece, r * seq, axis=0)
            tile = piece if tile is None else jnp.where(rows >= r * seq, piece, tile)
        o_ref[j * SUBLANES:(j + 1) * SUBLANES, :] = tile


def _dn_sample(proj, ba, conv_w, alog_row, dtb_row, dn_nw, conv0, s0, row0, n_batch, seq):
    assert SUBLANES % seq == 0 and seq >= DN_CONV - 1
    n_bb = SUBLANES
    rows_in = n_bb * seq
    assert n_batch % n_bb == 0 and row0 % rows_in == 0
    rb0 = row0 // rows_in
    const1 = lambda i: (0, 0)
    return pl.pallas_call(
        functools.partial(_dn_sample_kernel, seq=seq, n_bb=n_bb),
        grid=(n_batch // n_bb,),
        in_specs=[
            pl.BlockSpec((rows_in, DN_CONV_W), lambda i: (rb0 + i, COL_QKV // DN_CONV_W)),
            pl.BlockSpec((rows_in, DN_V_W), lambda i: (rb0 + i, COL_Z // DN_V_W)),
            pl.BlockSpec((rows_in, LANES), lambda i: (rb0 + i, 0)),
            pl.BlockSpec((DN_CONV, DN_CONV_W), const1),
            pl.BlockSpec((1, LANES), const1),
            pl.BlockSpec((1, LANES), const1),
            pl.BlockSpec((1, DN_DV), const1),
            pl.BlockSpec((n_bb, DN_CONV - 1, DN_CONV_W), lambda i: (i, 0, 0)),
            pl.BlockSpec((n_bb, DN_HEADS, DN_DK, DN_DV), lambda i: (i, 0, 0, 0)),
        ],
        out_specs=[
            pl.BlockSpec((rows_in, DN_V_W), lambda i: (i, 0)),
            pl.BlockSpec((n_bb, DN_CONV - 1, DN_CONV_W), lambda i: (i, 0, 0)),
            pl.BlockSpec((n_bb, DN_HEADS, DN_DK, DN_DV), lambda i: (i, 0, 0, 0)),
        ],
        out_shape=[
            jax.ShapeDtypeStruct((n_batch * seq, DN_V_W), F32),
            jax.ShapeDtypeStruct((n_batch, DN_CONV - 1, DN_CONV_W), F32),
            jax.ShapeDtypeStruct((n_batch, DN_HEADS, DN_DK, DN_DV), F32),
        ],
        scratch_shapes=[pltpu.VMEM((n_bb, 2 * SUBLANES, DN_CONV_W), F32)],
        compiler_params=_cparams(("arbitrary",)),
        name="dn_sample",
    )(proj, proj, ba, conv_w, alog_row, dtb_row, dn_nw, conv0, s0)


def _rel_bucket(dist):
    n = jnp.maximum(dist, 0)
    max_exact = REL_BUCKETS // 2
    large = max_exact + (jnp.log(jnp.maximum(n, 1).astype(F32) / max_exact)
                         / math.log(REL_MAX_DIST / max_exact) * (REL_BUCKETS - max_exact)).astype(I32)
    return jnp.where(n < max_exact, n, jnp.minimum(large, REL_BUCKETS - 1))


def _relbias_kernel(tab_ref, bucket_ref, o_ref):
    h = pl.program_id(0)
    bk = bucket_ref[...]
    acc = jnp.full(bk.shape, -jnp.inf, F32)
    for b in range(REL_BUCKETS):
        acc = jnp.where(bk == b, tab_ref[b * SW_HEADS + h], acc)
    o_ref[0] = acc


def _relbias(rel_table, bucket):
    nq, ns = bucket.shape
    return pl.pallas_call(
        _relbias_kernel,
        grid=(SW_HEADS,),
        in_specs=[
            pl.BlockSpec(memory_space=pltpu.SMEM),
            pl.BlockSpec((nq, ns), lambda h: (0, 0)),
        ],
        out_specs=pl.BlockSpec((1, nq, ns), lambda h: (h, 0, 0)),
        out_shape=jax.ShapeDtypeStruct((SW_HEADS, nq, ns), F32),
        compiler_params=_cparams(("arbitrary",)),
        name="relbias",
    )(rel_table.reshape(-1), bucket)


def _dup_halves(x):
    lo = _lane_iota(x.shape) < SW_HD
    xr = pltpu.roll(x, SW_HD, axis=1)
    return jnp.where(lo, x, xr).astype(BF16), jnp.where(lo, xr, x).astype(BF16)


def _sink_softmax_pv(s, sink, vv):
    m = jnp.maximum(jnp.max(s, axis=-1, keepdims=True), sink)
    p = jnp.exp(s - m)
    denom = jnp.sum(p, axis=-1, keepdims=True) + jnp.exp(sink - m)
    return _dot(p, vv) * (1.0 / denom)


def _swa_prompt_kernel(sinks_ref, q_ref, kc_ref, kp_ref, vc_ref, vp_ref, bias_ref, o_ref, klast_ref, vlast_ref):
    klast_ref[0] = kc_ref[...]
    vlast_ref[0] = vc_ref[...]
    kk = _dup_halves(jnp.concatenate([kp_ref[...], kc_ref[...]], axis=0))
    vv = _dup_halves(jnp.concatenate([vp_ref[...], vc_ref[...]], axis=0))
    lo = _lane_iota((SW_BLOCK, LANES)) < SW_HD
    for pair in range(SW_HEADS // 2):
        qp = q_ref[:, pair * LANES:(pair + 1) * LANES] * (SW_HD ** -0.5)
        outs = []
        for half in range(2):
            hq = 2 * pair + half
            kv = hq // SW_GROUP
            qm = jnp.where(lo if half == 0 else ~lo, qp, 0.0)
            s = _dot_nt(qm, kk[kv]) + bias_ref[0, hq]
            outs.append(_sink_softmax_pv(s, sinks_ref[hq], vv[kv]))
        o_ref[:, pair * LANES:(pair + 1) * LANES] = jnp.where(lo, outs[0], outs[1])


def _swa_prompt(proj, sinks, bias, n_batch, seq):
    assert seq % SW_BLOCK == 0 and WINDOW == SW_BLOCK
    nb = seq // SW_BLOCK
    cur = lambda col: (lambda b, i: (b * nb + i, col))
    prev = lambda col: (lambda b, i: (b * nb + jnp.maximum(i - 1, 0), col))
    return pl.pallas_call(
        _swa_prompt_kernel,
        grid=(n_batch, nb),
        in_specs=[
            pl.BlockSpec(memory_space=pltpu.SMEM),
            pl.BlockSpec((SW_BLOCK, SW_HEADS * SW_HD), cur(COL_SQ // (SW_HEADS * SW_HD))),
            pl.BlockSpec((SW_BLOCK, SW_KV_W), cur(COL_SK // SW_KV_W)),
            pl.BlockSpec((SW_BLOCK, SW_KV_W), prev(COL_SK // SW_KV_W)),
            pl.BlockSpec((SW_BLOCK, SW_KV_W), cur(COL_SV // SW_KV_W)),
            pl.BlockSpec((SW_BLOCK, SW_KV_W), prev(COL_SV // SW_KV_W)),
            pl.BlockSpec((1, SW_HEADS, SW_BLOCK, 2 * SW_BLOCK), lambda b, i: (jnp.minimum(i, 1), 0, 0, 0)),
        ],
        out_specs=[
            pl.BlockSpec((SW_BLOCK, SW_HEADS * SW_HD), lambda b, i: (b * nb + i, 0)),
            pl.BlockSpec((1, SW_BLOCK, SW_KV_W), lambda b, i: (b, 0, 0)),
            pl.BlockSpec((1, SW_BLOCK, SW_KV_W), lambda b, i: (b, 0, 0)),
        ],
        out_shape=[
            jax.ShapeDtypeStruct((n_batch * seq, SW_HEADS * SW_HD), F32),
            jax.ShapeDtypeStruct((n_batch, SW_BLOCK, SW_KV_W), F32),
            jax.ShapeDtypeStruct((n_batch, SW_BLOCK, SW_KV_W), F32),
        ],
        compiler_params=_cparams(("arbitrary", "arbitrary")),
        name="swa_prompt",
    )(sinks, proj, proj, proj, proj, proj, bias)


def _swa_sample_kernel(q_ref, kn_ref, vn_ref, kc_ref, vc_ref, bias_ref, sink_ref,
                       o_ref, ko_ref, vo_ref, kall_ref, vall_ref, *, seq, n_bb, n_cache):
    n_keys = kall_ref.shape[0]
    zeros_tail = jnp.zeros((n_keys - n_cache - SUBLANES, LANES), F32)
    lo = _lane_iota((SUBLANES, LANES)) < SW_HD
    out = None
    for bb in range(n_bb):
        shift = (SUBLANES - bb * seq) % SUBLANES

        def top(x, shift=shift):
            return x if shift == 0 else pltpu.roll(x, shift, axis=0)

        kall_ref[0:n_cache, :] = kc_ref[bb]
        kall_ref[n_cache:n_cache + SUBLANES, :] = top(kn_ref[...])
        kall_ref[n_cache + SUBLANES:, :] = zeros_tail
        vall_ref[0:n_cache, :] = vc_ref[bb]
        vall_ref[n_cache:n_cache + SUBLANES, :] = top(vn_ref[...])
        vall_ref[n_cache + SUBLANES:, :] = zeros_tail
        ko_ref[bb] = kall_ref[seq:seq + n_cache, :]
        vo_ref[bb] = vall_ref[seq:seq + n_cache, :]
        kk = _dup_halves(kall_ref[...])
        vv = _dup_halves(vall_ref[...])
        q8 = top(q_ref[...])
        pairs = []
        for kv in range(SW_KV_HEADS):
            pieces = []
            for g in range(SW_GROUP):
                hq = kv * SW_GROUP + g
                qp = q8[:, (hq // 2) * LANES:(hq // 2 + 1) * LANES]
                pieces.append(jnp.where(lo if hq % 2 == 0 else ~lo, qp, 0.0))
            qs = jnp.concatenate(pieces, axis=0) * (SW_HD ** -0.5)
            s = _dot_nt(qs, kk[kv]) + bias_ref[kv]
            res = _sink_softmax_pv(s, _col(sink_ref[kv], 0), vv[kv])
            for g in range(0, SW_GROUP, 2):
                pairs.append(jnp.where(lo, res[g * SUBLANES:(g + 1) * SUBLANES],
                                       res[(g + 1) * SUBLANES:(g + 2) * SUBLANES]))
        o = jnp.concatenate(pairs, axis=1)
        back = (bb * seq) % SUBLANES
        o = o if back == 0 else pltpu.roll(o, back, axis=0)
        rows = _row_iota(o.shape)
        sel = (rows >= bb * seq) & (rows < (bb + 1) * seq)
        out = jnp.where(sel, o, 0.0 if out is None else out)
    o_ref[...] = out


def _swa_sample(proj, k_cache, v_cache, bias, sink_rows, row0, n_batch, seq):
    assert SUBLANES % seq == 0
    n_bb = SUBLANES // seq
    n_cache = k_cache.shape[1]
    assert n_batch % n_bb == 0 and row0 % SUBLANES == 0 and n_cache % SUBLANES == 0
    n_keys = bias.shape[-1]
    rb0 = row0 // SUBLANES
    blk = lambda col: (lambda i: (rb0 + i, col))
    return pl.pallas_call(
        functools.partial(_swa_sample_kernel, seq=seq, n_bb=n_bb, n_cache=n_cache),
        grid=(n_batch // n_bb,),
        in_specs=[
            pl.BlockSpec((SUBLANES, SW_HEADS * SW_HD), blk(COL_SQ // (SW_HEADS * SW_HD))),
            pl.BlockSpec((SUBLANES, SW_KV_W), blk(COL_SK // SW_KV_W)),
            pl.BlockSpec((SUBLANES, SW_KV_W), blk(COL_SV // SW_KV_W)),
            pl.BlockSpec((n_bb, n_cache, SW_KV_W), lambda i: (i, 0, 0)),
            pl.BlockSpec((n_bb, n_cache, SW_KV_W), lambda i: (i, 0, 0)),
            pl.BlockSpec((SW_KV_HEADS, SW_GROUP * SUBLANES, n_keys), lambda i: (0, 0, 0)),
            pl.BlockSpec((SW_KV_HEADS, SW_GROUP * SUBLANES, n_keys), lambda i: (0, 0, 0)),
        ],
        out_specs=[
            pl.BlockSpec((SUBLANES, SW_HEADS * SW_HD), lambda i: (i, 0)),
            pl.BlockSpec((n_bb, n_cache, SW_KV_W), lambda i: (i, 0, 0)),
            pl.BlockSpec((n_bb, n_cache, SW_KV_W), lambda i: (i, 0, 0)),
        ],
        out_shape=[
            jax.ShapeDtypeStruct((n_batch * seq, SW_HEADS * SW_HD), F32),
            jax.ShapeDtypeStruct(k_cache.shape, F32),
            jax.ShapeDtypeStruct(v_cache.shape, F32),
        ],
        scratch_shapes=[pltpu.VMEM((n_keys, SW_KV_W), F32), pltpu.VMEM((n_keys, SW_KV_W), F32)],
        compiler_params=_cparams(("arbitrary",)),
        name="swa_sample",
    )(proj, proj, proj, k_cache, v_cache, bias, sink_rows)


def _mix_kernel(xp_ref, xs_ref, oap_ref, oas_ref, obp_ref, obs_ref, ga_ref, gb_ref, wo_ref, nw_ref, wr_ref, br_ref,
                x1_ref, h2_ref, route_ref, *, n_prompt_tiles):
    i = pl.program_id(0)

    def run(x_ref, oa_ref, ob_ref):
        mixed = _sigmoid(ga_ref[...]) * oa_ref[...] + _sigmoid(gb_ref[...]) * ob_ref[...]
        x1 = x_ref[...] + _dot(mixed, wo_ref[...])
        x1_ref[...] = x1
        h2 = x1 * lax.rsqrt(jnp.mean(x1 * x1, axis=-1, keepdims=True) + EPS) * nw_ref[...]
        h2_ref[...] = h2
        logits = _dot(h2, wr_ref[...]) + br_ref[...]
        lane = _lane_iota(logits.shape)
        lanef = lane.astype(F32)
        big = float(2 * LANES)
        is_g = lane < N_GROUPS
        gl = jnp.where(is_g, logits, -jnp.inf)
        gmax = jnp.max(gl, axis=-1, keepdims=True)
        gval = 1.0 / jnp.sum(jnp.where(is_g, jnp.exp(gl - gmax), 0.0), axis=-1, keepdims=True)
        grp = jnp.min(jnp.where(gl == gmax, lanef, big), axis=-1, keepdims=True)
        e_grp = ((lane - N_GROUPS) >> 3).astype(F32)
        is_e = (lane >= N_GROUPS) & (lane < N_GROUPS + N_EXPERTS) & (e_grp == grp)
        el = jnp.where(is_e, logits, -jnp.inf)
        v1 = jnp.max(el, axis=-1, keepdims=True)
        i1 = jnp.min(jnp.where(el == v1, lanef, big), axis=-1, keepdims=True)
        el2 = jnp.where(lanef == i1, -jnp.inf, el)
        v2 = jnp.max(el2, axis=-1, keepdims=True)
        i2 = jnp.min(jnp.where(el2 == v2, lanef, big), axis=-1, keepdims=True)
        e2 = jnp.exp(v2 - v1)
        w1 = gval / (1.0 + e2)
        w2 = gval * e2 / (1.0 + e2)
        route_ref[...] = jnp.where(lane == 0, i1 - N_GROUPS,
                                   jnp.where(lane == 1, i2 - N_GROUPS,
                                             jnp.where(lane == 2, w1, jnp.where(lane == 3, w2, 0.0))))

    @pl.when(i < n_prompt_tiles)
    def _():
        run(xp_ref, oap_ref, obp_ref)

    @pl.when(i >= n_prompt_tiles)
    def _():
        run(xs_ref, oas_ref, obs_ref)


def _mix(xp, xs, oa_p, oa_s, ob_p, ob_s, proj, w_out, norm_w, w_router, b_router):
    tp, ts = xp.shape[0], xs.shape[0]
    tm = _token_tile(tp, ts)
    npt, nst = tp // tm, ts // tm
    const = lambda i: (0, 0)
    row = lambda i: (i, 0)
    return pl.pallas_call(
        functools.partial(_mix_kernel, n_prompt_tiles=npt),
        grid=(npt + nst,),
        in_specs=[
            pl.BlockSpec((tm, D_MODEL), lambda i: (jnp.minimum(i, npt - 1), 0)),
            pl.BlockSpec((tm, D_MODEL), lambda i: (jnp.maximum(i - npt, 0), 0)),
            pl.BlockSpec((tm, D_MODEL), lambda i: (jnp.minimum(i, npt - 1), 0)),
            pl.BlockSpec((tm, D_MODEL), lambda i: (jnp.maximum(i - npt, 0), 0)),
            pl.BlockSpec((tm, D_MODEL), lambda i: (jnp.minimum(i, npt - 1), 0)),
            pl.BlockSpec((tm, D_MODEL), lambda i: (jnp.maximum(i - npt, 0), 0)),
            pl.BlockSpec((tm, D_MODEL), lambda i: (i, COL_GA // D_MODEL)),
            pl.BlockSpec((tm, D_MODEL), lambda i: (i, COL_GB // D_MODEL)),
            pl.BlockSpec((D_MODEL, D_MODEL), const),
            pl.BlockSpec((1, D_MODEL), const),
            pl.BlockSpec((D_MODEL, LANES), const),
            pl.BlockSpec((1, LANES), const),
        ],
        out_specs=[
            pl.BlockSpec((tm, D_MODEL), row),
            pl.BlockSpec((tm, D_MODEL), row),
            pl.BlockSpec((tm, LANES), row),
        ],
        out_shape=[
            jax.ShapeDtypeStruct((tp + ts, D_MODEL), F32),
            jax.ShapeDtypeStruct((tp + ts, D_MODEL), F32),
            jax.ShapeDtypeStruct((tp + ts, LANES), F32),
        ],
        compiler_params=_cparams(("arbitrary",)),
        name="mix_router",
    )(xp, xs, oa_p, oa_s, ob_p, ob_s, proj, proj, w_out, norm_w, w_router, b_router)


def _rank_kernel(route_ref, dest_ref, meta_ref, rank_ref, cnt_ref, *, tile, blk):
    phase = pl.program_id(0)
    i = pl.program_id(1)
    shape = (tile, LANES)
    lane = _lane_iota(shape)
    lanef = lane.astype(F32)
    r = route_ref[...]
    oh0 = lanef == _col(r, 0)
    oh1 = lanef == _col(r, 1)
    rows = pl.ds(pl.multiple_of(i * tile, tile), tile)

    @pl.when(phase == 0)
    def _():
        @pl.when(i == 0)
        def _():
            cnt_ref[...] = jnp.zeros(cnt_ref.shape, F32)

        oh = jnp.where(oh0 | oh1, 1.0, 0.0)
        tri = jnp.where(_row_iota((tile, tile)) > _lane_iota((tile, tile)), 1.0, 0.0)
        before = _dot(tri, oh) + cnt_ref[0:1, :]
        rank0 = jnp.sum(jnp.where(oh0, before, 0.0), axis=-1, keepdims=True)
        rank1 = jnp.sum(jnp.where(oh1, before, 0.0), axis=-1, keepdims=True)
        rank_ref[rows, :] = jnp.where(lane == 0, rank0, jnp.where(lane == 1, rank1, 0.0))
        cnt_ref[0:1, :] = cnt_ref[0:1, :] + jnp.sum(oh, axis=0, keepdims=True)

    @pl.when(phase == 1)
    def _():
        cnt = cnt_ref[0:1, :]
        padded = jnp.floor((cnt + (blk - 1)) / blk) * blk
        before_lane = jnp.where(_row_iota((LANES, LANES)) < _lane_iota((LANES, LANES)), 1.0, 0.0)
        start = _dot_exact(jnp.broadcast_to(padded, (SUBLANES, LANES)), before_lane)[0:1, :]
        rk = rank_ref[rows, :]
        d0 = jnp.sum(jnp.where(oh0, start, 0.0), axis=-1, keepdims=True) + _col(rk, 0)
        d1 = jnp.sum(jnp.where(oh1, start, 0.0), axis=-1, keepdims=True) + _col(rk, 1)
        dest_ref[...] = jnp.where(lane == 0, d0, jnp.where(lane == 1, d1, 0.0)).astype(I32)

        @pl.when(i == 0)
        def _():
            end = start + padded
            mshape = meta_ref.shape
            blk_start = (_row_iota(mshape) * blk).astype(F32)
            hit = (_lane_iota(mshape) < N_EXPERTS) & (end <= blk_start)
            be = jnp.minimum(jnp.sum(jnp.where(hit, 1.0, 0.0), axis=-1, keepdims=True), N_EXPERTS - 1.0)
            n_used = _col(end, N_EXPERTS - 1) / blk
            ml = _lane_iota(mshape)
            mine = ml.astype(F32) == be
            seg_start = jnp.sum(jnp.where(mine, start, 0.0), axis=-1, keepdims=True)
            seg_count = jnp.sum(jnp.where(mine, cnt, 0.0), axis=-1, keepdims=True)
            n_valid = jnp.clip(seg_count - (blk_start[:, 0:1] - seg_start), 0.0, float(blk))
            meta_ref[...] = jnp.where(ml == 0, be, jnp.where(ml == 1, n_used,
                                                              jnp.where(ml == 2, n_valid, 0.0))).astype(I32)


def _rank(route, tile, blk, n_blocks):
    t = route.shape[0]
    nt = t // tile
    nbp = -(-n_blocks // SUBLANES) * SUBLANES
    return pl.pallas_call(
        functools.partial(_rank_kernel, tile=tile, blk=blk),
        grid=(2, nt),
        in_specs=[pl.BlockSpec((tile, LANES), lambda p, i: (i, 0))],
        out_specs=[
            pl.BlockSpec((tile, LANES), lambda p, i: (i * p, 0)),
            pl.BlockSpec((nbp, LANES), lambda p, i: (0, 0)),
        ],
        out_shape=[
            jax.ShapeDtypeStruct((t, LANES), I32),
            jax.ShapeDtypeStruct((nbp, LANES), I32),
        ],
        scratch_shapes=[pltpu.VMEM((t, LANES), F32), pltpu.VMEM((SUBLANES, LANES), F32)],
        compiler_params=_cparams(("arbitrary", "arbitrary")),
        name="moe_rank",
    )(route)


def _row_copy(src, src_row, dst, dst_row, sem):
    return pltpu.make_async_copy(src.at[pl.ds(src_row, 1)], dst.at[pl.ds(dst_row, 1)], sem)


def _last_used(i, nu_ref):
    return jnp.minimum(i, jnp.maximum(nu_ref[0] - 1, 0))


def _for_rows(n, body):
    def group(g, carry):
        for u in range(SUBLANES):
            body(g * SUBLANES + u, u)
        return carry

    def single(t, carry):
        body(t, 0)
        return carry

    n_groups = n // SUBLANES
    lax.fori_loop(0, n_groups, group, 0)
    lax.fori_loop(n_groups * SUBLANES, n, single, 0)


def _expert_kernel(be_ref, nu_ref, nv_ref, dest_ref, h2_ref, wg_ref, wu_ref, wd_ref, y_ref,
                   xbuf_ref, wgu_ref, wdn_ref, inv_ref, sem, *, blk):
    i = pl.program_id(0)
    n_used = nu_ref[0]
    used = i < n_used
    slot = i % 2
    blk_i = _last_used(i, nu_ref)
    fresh = (i == 0) | (be_ref[blk_i] != be_ref[jnp.maximum(blk_i - 1, 0)])

    def gather(block, into):
        _for_rows(nv_ref[block], lambda t, u: _row_copy(h2_ref, inv_ref[block * blk + t], xbuf_ref.at[into], t,
                                                       sem.at[into]).start(priority=u % 2))

    @pl.when(i == 0)
    def _():
        def place(a, carry):
            inv_ref[dest_ref[a]] = a >> 1
            return carry

        lax.fori_loop(0, dest_ref.shape[0], place, 0, unroll=16)
        xbuf_ref[...] = jnp.zeros(xbuf_ref.shape, F32)
        gather(0, 0)

    @pl.when(i + 1 < n_used)
    def _():
        gather(i + 1, 1 - slot)

    @pl.when(used & fresh)
    def _():
        wgu_ref[:, :D_EXPERT] = wg_ref[0].astype(BF16)
        wgu_ref[:, D_EXPERT:] = wu_ref[0].astype(BF16)
        wdn_ref[...] = wd_ref[0].astype(BF16)

    @pl.when(used)
    def _():
        _for_rows(nv_ref[i], lambda t, u: _row_copy(h2_ref, 0, xbuf_ref.at[slot], 0, sem.at[slot]).wait())
        gu = jnp.dot(xbuf_ref[slot].astype(BF16), wgu_ref[...], preferred_element_type=F32)
        g = gu[:, :D_EXPERT]
        hidden = (g * _sigmoid(g) * gu[:, D_EXPERT:]).astype(BF16)
        y_ref[...] = jnp.dot(hidden, wdn_ref[...], preferred_element_type=F32)

    @pl.when(jnp.logical_not(used))
    def _():
        y_ref[...] = jnp.zeros(y_ref.shape, F32)


def _experts(block_expert, n_used, n_valid, dest_flat, h2, w_gate, w_up, w_down, blk):
    n_blocks = block_expert.shape[0]
    wsel = lambda i, be, nu, nv, de: (be[_last_used(i, nu)], 0, 0)
    return pl.pallas_call(
        functools.partial(_expert_kernel, blk=blk),
        grid_spec=pltpu.PrefetchScalarGridSpec(
            num_scalar_prefetch=4,
            grid=(n_blocks,),
            in_specs=[
                pl.BlockSpec(memory_space=pl.ANY),
                pl.BlockSpec((1, D_MODEL, D_EXPERT), wsel),
                pl.BlockSpec((1, D_MODEL, D_EXPERT), wsel),
                pl.BlockSpec((1, D_EXPERT, D_MODEL), wsel),
            ],
            out_specs=pl.BlockSpec((blk, D_MODEL), lambda i, be, nu, nv, de: (i, 0)),
            scratch_shapes=[
                pltpu.VMEM((2, blk, D_MODEL), F32),
                pltpu.VMEM((D_MODEL, 2 * D_EXPERT), BF16),
                pltpu.VMEM((D_EXPERT, D_MODEL), BF16),
                pltpu.SMEM((n_blocks * blk,), I32),
                pltpu.SemaphoreType.DMA((2,)),
            ],
        ),
        out_shape=jax.ShapeDtypeStruct((n_blocks * blk, D_MODEL), F32),
        compiler_params=_cparams(("arbitrary",)),
        name="moe_experts",
    )(block_expert, n_used, n_valid, dest_flat, h2, w_gate, w_up, w_down)


def _combine_kernel(dest_ref, x1_ref, route_ref, nw_ref, ys_ref, yp_ref, ysm_ref, ybuf_ref, sem,
                    *, tile, n_prompt_tiles):
    i = pl.program_id(0)
    slot = i % 2

    def gather(step, into):
        def issue(t, carry):
            for k in range(2):
                _row_copy(ys_ref, dest_ref[2 * (step * tile + t) + k], ybuf_ref.at[into], k * tile + t,
                          sem.at[into]).start(priority=k)
            return carry

        lax.fori_loop(0, tile, issue, 0, unroll=4)

    @pl.when(i == 0)
    def _():
        gather(0, 0)

    @pl.when(i + 1 < pl.num_programs(0))
    def _():
        gather(i + 1, 1 - slot)

    def drain(t, carry):
        _row_copy(ys_ref, 0, ybuf_ref.at[slot], 0, sem.at[slot]).wait()
        return carry

    lax.fori_loop(0, 2 * tile, drain, 0, unroll=8)
    r = route_ref[...]
    ybuf = ybuf_ref.at[slot]
    y = ybuf[0:tile, :] * _col(r, 2) + ybuf[tile:2 * tile, :] * _col(r, 3)
    x2 = x1_ref[...] + y
    out = x2 * lax.rsqrt(jnp.mean(x2 * x2, axis=-1, keepdims=True) + EPS) * nw_ref[...]

    @pl.when(i < n_prompt_tiles)
    def _():
        yp_ref[...] = out

    @pl.when(i >= n_prompt_tiles)
    def _():
        ysm_ref[...] = out


def _combine(dest_flat, x1, route, norm_w, ys, tp, ts):
    tile = _token_tile(tp, ts)
    npt, nst = tp // tile, ts // tile
    return pl.pallas_call(
        functools.partial(_combine_kernel, tile=tile, n_prompt_tiles=npt),
        grid_spec=pltpu.PrefetchScalarGridSpec(
            num_scalar_prefetch=1,
            grid=(npt + nst,),
            in_specs=[
                pl.BlockSpec((tile, D_MODEL), lambda i, d: (i, 0)),
                pl.BlockSpec((tile, LANES), lambda i, d: (i, 0)),
                pl.BlockSpec((1, D_MODEL), lambda i, d: (0, 0)),
                pl.BlockSpec(memory_space=pl.ANY),
            ],
            out_specs=[
                pl.BlockSpec((tile, D_MODEL), lambda i, d: (jnp.minimum(i, npt - 1), 0)),
                pl.BlockSpec((tile, D_MODEL), lambda i, d: (jnp.maximum(i - npt, 0), 0)),
            ],
            scratch_shapes=[pltpu.VMEM((2, 2 * tile, D_MODEL), F32), pltpu.SemaphoreType.DMA((2,))],
        ),
        out_shape=[
            jax.ShapeDtypeStruct((tp, D_MODEL), F32),
            jax.ShapeDtypeStruct((ts, D_MODEL), F32),
        ],
        compiler_params=_cparams(("arbitrary",)),
        name="moe_combine",
    )(dest_flat, x1, route, norm_w, ys)


def _layer(xp, xs, n_batch, seq, s_batch, s_seq, conv_state, dn_state, k_cache, v_cache,
           w_in, conv_w, a_log, dt_bias, dn_norm_w, sinks, rel_bias, w_out, norm_mix_w, norm_ffn_w,
           w_rg, b_rg, w_re, b_re, w_gate, w_up, w_down, norm_final_w):
    tp, ts = xp.shape[0], xs.shape[0]
    t_all = tp + ts
    row = lambda v: v.reshape(1, -1).astype(F32)

    o = np.cumsum((0, DN_QK_W, DN_QK_W, DN_V_W, DN_V_W, DN_HEADS, DN_HEADS, SW_HEADS * SW_HD, SW_KV_W, SW_KV_W,
                   D_MODEL, D_MODEL)).tolist()
    w_big = jnp.concatenate([w_in[:, o[0]:o[4]], w_in[:, o[6]:o[7]], w_in[:, o[9]:o[11]], w_in[:, o[7]:o[9]]],
                            axis=1).astype(BF16)
    w_small = jnp.pad(w_in[:, o[4]:o[6]], ((0, 0), (0, LANES - 2 * DN_HEADS))).astype(BF16)
    head_row = lambda v: jnp.pad(v.astype(F32), (DN_HEADS, LANES - 2 * DN_HEADS)).reshape(1, LANES)
    w_router = jnp.pad(jnp.concatenate([w_rg, w_re], axis=1),
                       ((0, 0), (0, LANES - N_GROUPS - N_EXPERTS))).astype(BF16)
    b_router = jnp.pad(jnp.concatenate([b_rg, b_re]).astype(F32), (0, LANES - N_GROUPS - N_EXPERTS)).reshape(1, LANES)

    conv0 = jnp.zeros((n_batch, DN_CONV - 1, DN_CONV_W), F32)
    proj, ba, conv_tail = _inproj(xp, xs, row(norm_mix_w), w_big, w_small, conv_w.astype(F32), conv0, seq)
    p_conv = conv_tail[:, SUBLANES - (DN_CONV - 1):, :]

    dn0 = jnp.zeros((n_batch, DN_HEADS, DN_DK, DN_DV), F32)
    oa_p, p_dn = _dn_prompt(proj, ba, head_row(a_log), head_row(dt_bias), row(dn_norm_w), dn0, n_batch, seq)
    oa_s, s_conv, s_dn = _dn_sample(proj, ba, conv_w.astype(F32), head_row(a_log), head_row(dt_bias), row(dn_norm_w),
                                    conv_state, dn_state, tp, s_batch, s_seq)

    qpos = jnp.arange(SW_BLOCK)[:, None]
    kpos = jnp.arange(2 * SW_BLOCK)[None, :] - SW_BLOCK
    in_window = lambda dist: (dist >= 0) & (dist < WINDOW)
    masked_bucket = lambda dist, ok: jnp.where(ok, _rel_bucket(dist), -1)
    bias_p = jnp.stack([
        _relbias(rel_bias.astype(F32), masked_bucket(qpos - kpos, in_window(qpos - kpos) & (kpos >= 0))),
        _relbias(rel_bias.astype(F32), masked_bucket(qpos - kpos, in_window(qpos - kpos)))])
    ob_p, p_k, p_v = _swa_prompt(proj, sinks.astype(F32), bias_p, n_batch, seq)
    n_cache = k_cache.shape[1]
    n_keys = -(-(n_cache + SUBLANES) // LANES) * LANES
    tq = n_cache + jnp.arange(SUBLANES)[:, None]
    dist_s = tq - jnp.arange(n_keys)[None, :]
    bias_s = _relbias(rel_bias.astype(F32), masked_bucket(dist_s, in_window(dist_s)))
    bias_s = bias_s.reshape(SW_KV_HEADS, SW_GROUP * SUBLANES, n_keys)
    sink_rows = jnp.broadcast_to(jnp.repeat(sinks.astype(F32).reshape(SW_KV_HEADS, SW_GROUP), SUBLANES, axis=1)[:, :, None],
                                 (SW_KV_HEADS, SW_GROUP * SUBLANES, n_keys))
    ob_s, s_k, s_v = _swa_sample(proj, k_cache.reshape(s_batch, n_cache, SW_KV_W), v_cache.reshape(s_batch, n_cache, SW_KV_W),
                                 bias_s, sink_rows, tp, s_batch, s_seq)

    x1, h2, route = _mix(xp, xs, oa_p, oa_s, ob_p, ob_s, proj, w_out.astype(BF16), row(norm_ffn_w), w_router, b_router)

    tile = _token_tile(tp, ts)
    n_blocks = -(-2 * t_all // MOE_BLOCK) + N_EXPERTS
    dest, meta = _rank(route, _token_tile(t_all, cands=(512, 256, 128, 64, 32, 16, 8)), MOE_BLOCK, n_blocks)
    dest_flat = dest[:, :2].reshape(-1)
    block_expert = meta[:n_blocks, 0]
    n_used = meta[0:1, 1]
    n_valid = meta[:n_blocks, 2]
    ys = _experts(block_expert, n_used, n_valid, dest_flat, h2, w_gate, w_up, w_down, MOE_BLOCK)
    y_p, y_s = _combine(dest_flat, x1, route, row(norm_final_w), ys, tp, ts)

    kv_shape = (n_batch, WINDOW, SW_KV_HEADS, SW_HD)
    return (y_p, y_s, p_conv, p_dn, p_k.reshape(kv_shape), p_v.reshape(kv_shape), s_conv, s_dn,
            s_k.reshape(k_cache.shape), s_v.reshape(v_cache.shape))


def kernel(x_prompt, x_sample, state_dn_conv, state_dn, cache_swa_k, cache_swa_v, w_in, conv_w, a_log, dt_bias, dn_norm_w, sinks, rel_bias, w_out, norm_mix_w, norm_ffn_w, w_router_group, b_router_group, w_router_expert, b_router_expert, w_gate, w_up, w_down, norm_final_w):
    depth = w_in.shape[0]
    assert depth == 1, "the final-norm fusion below assumes a single layer"
    n_batch, seq, _ = x_prompt.shape
    s_batch, s_seq, _ = x_sample.shape
    outs = _layer(x_prompt.reshape(-1, D_MODEL), x_sample.reshape(-1, D_MODEL), n_batch, seq, s_batch, s_seq,
                  state_dn_conv[0], state_dn[0], cache_swa_k[0], cache_swa_v[0],
                  w_in[0], conv_w[0], a_log[0], dt_bias[0], dn_norm_w[0], sinks[0], rel_bias,
                  w_out[0], norm_mix_w[0], norm_ffn_w[0], w_router_group[0], b_router_group[0],
                  w_router_expert[0], b_router_expert[0], w_gate[0], w_up[0], w_down[0], norm_final_w)
    y_p, y_s, p_conv, p_dn, p_k, p_v, s_conv, s_dn, s_k, s_v = outs
    return (y_p.reshape(x_prompt.shape), y_s.reshape(x_sample.shape), p_conv[None], p_dn[None], p_k[None], p_v[None],
            s_conv[None], s_dn[None], s_k[None], s_v[None])
```

```python
import functools
import math

import jax
import jax.numpy as jnp
import numpy as np
from jax import lax
from jax.experimental import pallas as pl
from jax.experimental.pallas import tpu as pltpu

F32 = jnp.float32
BF16 = jnp.bfloat16
I32 = jnp.int32

D_MODEL = 1024
DN_HEADS = 8
DN_DK = 128
DN_DV = 128
DN_CONV = 4
DN_CHUNK = 64
DN_QK_W = DN_HEADS * DN_DK
DN_V_W = DN_HEADS * DN_DV
DN_CONV_W = 2 * DN_QK_W + DN_V_W
SW_HEADS = 16
SW_KV_HEADS = 2
SW_GROUP = SW_HEADS // SW_KV_HEADS
SW_HD = 64
SW_KV_W = SW_KV_HEADS * SW_HD
WINDOW = 128
SW_BLOCK = 128
REL_BUCKETS = 32
REL_MAX_DIST = 128
N_GROUPS = 8
EXP_PER_GROUP = 8
N_EXPERTS = N_GROUPS * EXP_PER_GROUP
D_EXPERT = 256
MOE_BLOCK = 256
EPS = 1e-6

LANES = 128
SUBLANES = 8
VMEM_LIMIT = 56 * 1024 * 1024

COL_QKV = 0
COL_Z = 3072
COL_SQ = 4096
COL_GA = 5120
COL_GB = 6144
COL_SK = 7168
COL_SV = 7296
PROJ_W = 7424
PROJ_CHUNK = 512


def _cparams(sem):
    return pltpu.CompilerParams(dimension_semantics=sem, vmem_limit_bytes=VMEM_LIMIT)


def _sigmoid(x):
    return 0.5 * jnp.tanh(0.5 * x) + 0.5


def _dot(a, b):
    return jnp.dot(a.astype(BF16), b.astype(BF16), preferred_element_type=F32)


def _dot_nt(a, b):
    return lax.dot_general(a.astype(BF16), b.astype(BF16), (((1,), (1,)), ((), ())), preferred_element_type=F32)


def _dot_tn(a, b):
    return lax.dot_general(a.astype(BF16), b.astype(BF16), (((0,), (0,)), ((), ())), preferred_element_type=F32)


def _dot_exact(a, b):
    return jnp.dot(a, b, precision=lax.Precision.HIGHEST, preferred_element_type=F32)


def _lane_iota(shape):
    return lax.broadcasted_iota(I32, shape, len(shape) - 1)


def _row_iota(shape):
    return lax.broadcasted_iota(I32, shape, len(shape) - 2)


def _col(x, j):
    return jnp.sum(jnp.where(_lane_iota(x.shape) == j, x, 0.0), axis=-1, keepdims=True)


def _token_tile(*sizes, cands=(256, 128, 64, 32, 16, 8)):
    for t in cands:
        if all(s % t == 0 for s in sizes):
            return t
    raise ValueError(f"token counts {sizes} need a common tile that is a multiple of 8")


def _inproj_kernel(xp_ref, xs_ref, nw_ref, wb_ref, ws_ref, convw_ref, conv0_ref, proj_ref, ba_ref, tail_ref, cbuf_ref,
                   *, n_prompt_tiles, tiles_per_seq):
    i = pl.program_id(0)
    tm = xp_ref.shape[0]
    hist = SUBLANES - (DN_CONV - 1)

    def project(x_ref, conv):
        x = x_ref[...]
        h = (x * lax.rsqrt(jnp.mean(x * x, axis=-1, keepdims=True) + EPS) * nw_ref[...]).astype(BF16)
        ba_ref[...] = jnp.dot(h, ws_ref[...], preferred_element_type=F32)
        top = _row_iota((SUBLANES, PROJ_CHUNK))
        starts = list(range(0, PROJ_W, PROJ_CHUNK))
        if conv:
            with_conv = [c for c in starts if c + PROJ_CHUNK <= COL_QKV + DN_CONV_W]
            plain = [c for c in starts if c not in with_conv]
            starts = [c for pair in zip(with_conv, plain) for c in pair] + plain[len(with_conv):]
        for c0 in starts:
            c1 = min(c0 + PROJ_CHUNK, PROJ_W)
            cur = jnp.dot(h, wb_ref[:, c0:c1], preferred_element_type=F32)
            if conv and c1 <= COL_QKV + DN_CONV_W:
                prev = cbuf_ref[:, c0:c1]
                acc = cur * convw_ref[DN_CONV - 1:DN_CONV, c0:c1]
                for s in range(1, DN_CONV):
                    sh = pltpu.roll(cur, s, axis=0)
                    head = jnp.where(top < s, pltpu.roll(prev, s, axis=0), sh[:SUBLANES])
                    sh = jnp.concatenate([head, sh[SUBLANES:]], axis=0)
                    acc = acc + sh * convw_ref[DN_CONV - 1 - s:DN_CONV - s, c0:c1]
                cbuf_ref[:, c0:c1] = cur[tm - SUBLANES:]
                cur = acc * _sigmoid(acc)
            proj_ref[:, c0:c1] = cur

    @pl.when(i < n_prompt_tiles)
    def _():
        @pl.when(i % tiles_per_seq == 0)
        def _():
            cbuf_ref[...] = jnp.zeros(cbuf_ref.shape, F32)
            cbuf_ref[hist:SUBLANES, :] = conv0_ref[0]

        project(xp_ref, True)
        tail_ref[0] = cbuf_ref[...]

    @pl.when(i >= n_prompt_tiles)
    def _():
        project(xs_ref, False)


def _inproj(xp, xs, norm_w, w_big, w_small, conv_w, conv0, seq):
    tp, ts = xp.shape[0], xs.shape[0]
    tm = _token_tile(tp, ts, seq)
    assert COL_QKV == 0 and DN_CONV_W % PROJ_CHUNK == 0
    npt, nst = tp // tm, ts // tm
    tps = seq // tm
    const = lambda i: (0, 0)
    seq_of = lambda i: (jnp.minimum(i, npt - 1) // tps, 0, 0)
    return pl.pallas_call(
        functools.partial(_inproj_kernel, n_prompt_tiles=npt, tiles_per_seq=tps),
        grid=(npt + nst,),
        in_specs=[
            pl.BlockSpec((tm, D_MODEL), lambda i: (jnp.minimum(i, npt - 1), 0)),
            pl.BlockSpec((tm, D_MODEL), lambda i: (jnp.maximum(i - npt, 0), 0)),
            pl.BlockSpec((1, D_MODEL), const),
            pl.BlockSpec((D_MODEL, PROJ_W), const, pipeline_mode=pl.Buffered(1)),
            pl.BlockSpec((D_MODEL, LANES), const),
            pl.BlockSpec((DN_CONV, DN_CONV_W), const),
            pl.BlockSpec((1, DN_CONV - 1, DN_CONV_W), seq_of),
        ],
        out_specs=[
            pl.BlockSpec((tm, PROJ_W), lambda i: (i, 0)),
            pl.BlockSpec((tm, LANES), lambda i: (i, 0)),
            pl.BlockSpec((1, SUBLANES, DN_CONV_W), seq_of),
        ],
        out_shape=[
            jax.ShapeDtypeStruct((tp + ts, PROJ_W), F32),
            jax.ShapeDtypeStruct((tp + ts, LANES), F32),
            jax.ShapeDtypeStruct((tp // seq, SUBLANES, DN_CONV_W), F32),
        ],
        scratch_shapes=[pltpu.VMEM((SUBLANES, DN_CONV_W), F32)],
        compiler_params=_cparams(("arbitrary",)),
        name="inproj",
    )(xp, xs, norm_w, w_big, w_small, conv_w, conv0)


def _dn_core(groups, alog, dtb, nw, read_state, write_state, n_seg, seg_valid):
    rows = groups[0][0].shape[0]
    sr = rows // n_seg
    assert sr * n_seg == rows and sr & (sr - 1) == 0 and rows <= LANES
    seg_shift = sr.bit_length() - 1
    ri = _row_iota((rows, rows))
    ci = _lane_iota((rows, rows))
    incl = ri >= ci
    strict = ri > ci
    if n_seg > 1:
        same = (ri >> seg_shift) == (ci >> seg_shift)
        incl = incl & same
        strict = strict & same
    l_incl = incl.astype(F32)
    eye = (ri == ci).astype(F32)
    levels = max(1, math.ceil(math.log2(seg_valid)))

    beta_all, gsum_all, gtot_all, gsum_t = [], [], [], []
    for _, _, ba in groups:
        b_all = _sigmoid(ba)
        sp = ba + dtb
        softplus = jnp.maximum(sp, 0.0) + jnp.log1p(jnp.exp(-jnp.abs(sp)))
        g_all = -jnp.exp(alog) * softplus
        if seg_valid < sr:
            live = (_row_iota((rows, LANES)) & (sr - 1)) < seg_valid
            b_all = jnp.where(live, b_all, 0.0)
            g_all = jnp.where(live, g_all, 0.0)
        gs = _dot_exact(l_incl, g_all)
        beta_all.append(b_all)
        gsum_all.append(gs)
        gtot_all.append(_dot_exact(same.astype(F32), g_all) if n_seg > 1 else gs[rows - 1:rows, :])
        padded = gs if rows == LANES else jnp.concatenate([gs, jnp.zeros((LANES - rows, LANES), F32)], axis=0)
        gsum_t.append(padded.T)

    probs = [(g, h) for g in range(len(groups)) for h in range(DN_HEADS)]
    segs = range(n_seg)
    q, k, v, kb, beta, gsum, gtot = {}, {}, {}, {}, {}, {}, {}
    for p in probs:
        g, h = p
        qkv = groups[g][0]
        qh = qkv[:, h * DN_DK:(h + 1) * DN_DK]
        kh = qkv[:, DN_QK_W + h * DN_DK:DN_QK_W + (h + 1) * DN_DK]
        v[p] = qkv[:, 2 * DN_QK_W + h * DN_DV:2 * DN_QK_W + (h + 1) * DN_DV]
        q[p] = qh * lax.rsqrt(jnp.sum(qh * qh, axis=-1, keepdims=True) + 1e-6) * (DN_DK ** -0.5)
        k[p] = kh * lax.rsqrt(jnp.sum(kh * kh, axis=-1, keepdims=True) + 1e-6)
        beta[p] = _col(beta_all[g], h)
        gsum[p] = _col(gsum_all[g], DN_HEADS + h)
        gtot[p] = _col(gtot_all[g], DN_HEADS + h)
        kb[p] = k[p] * beta[p]
    kq = {p: _dot_nt(jnp.concatenate([kb[p], q[p]], axis=0), k[p]) for p in probs}
    gamma = {(g, h): jnp.exp(jnp.where(incl, gsum[(g, h)] - gsum_t[g][DN_HEADS + h:DN_HEADS + h + 1, :rows], -jnp.inf))
             for g, h in probs}
    attn = {p: kq[p][rows:] * gamma[p] for p in probs}
    pw = {p: -jnp.where(strict, kq[p][:rows] * gamma[p], 0.0) for p in probs}
    t = {p: eye + pw[p] for p in probs}
    for _ in range(1, levels):
        pw = {p: _dot(pw[p], pw[p]) for p in probs}
        t = {p: t[p] + _dot(t[p], pw[p]) for p in probs}
    eg = {p: jnp.exp(gsum[p]) for p in probs}
    uw = {p: _dot(t[p], jnp.concatenate([v[p] * beta[p], kb[p] * eg[p]], axis=1)) for p in probs}
    qg = {p: q[p] * eg[p] for p in probs}
    state = {(p, s): read_state(p[0], s, p[1]) for p in probs for s in segs}
    wq = {(p, s): _dot(jnp.concatenate([uw[p][s * sr:(s + 1) * sr, DN_DV:], qg[p][s * sr:(s + 1) * sr]], axis=0),
                       state[(p, s)]) for p in probs for s in segs}
    join = lambda pieces: pieces[0] if len(pieces) == 1 else jnp.concatenate(pieces, axis=0)
    v_new = {p: uw[p][:, :DN_DV] - join([wq[(p, s)][:sr] for s in segs]) for p in probs}
    o = {p: join([wq[(p, s)][sr:] for s in segs]) + _dot(attn[p], v_new[p]) for p in probs}
    kd = {p: k[p] * jnp.exp(gtot[p] - gsum[p]) for p in probs}
    for p in probs:
        for s in segs:
            r0 = s * sr if n_seg > 1 else 0
            decay = jnp.exp(gtot[p][r0:r0 + 1, :])
            write_state(p[0], s, p[1],
                        state[(p, s)] * decay + _dot_tn(kd[p][s * sr:(s + 1) * sr], v_new[p][s * sr:(s + 1) * sr]))
    outs = []
    for g, (_, z, _) in enumerate(groups):
        heads = []
        for h in range(DN_HEADS):
            oh = o[(g, h)]
            zz = z[:, h * DN_DV:(h + 1) * DN_DV]
            on = oh * lax.rsqrt(jnp.mean(oh * oh, axis=-1, keepdims=True) + EPS) * nw
            heads.append(on * (zz * _sigmoid(zz)))
        outs.append(jnp.concatenate(heads, axis=1))
    return outs


def _dn_prompt_kernel(*refs, chunk, n_batch):
    nb = n_batch
    qkv_refs, z_refs, ba_refs = refs[0:nb], refs[nb:2 * nb], refs[2 * nb:3 * nb]
    alog_ref, dtb_ref, nw_ref, s0_ref, o_ref, sout_ref = refs[3 * nb:]

    @pl.when(pl.program_id(0) == 0)
    def _():
        sout_ref[...] = s0_ref[...]

    groups = [(qkv_refs[b][...], z_refs[b][...], ba_refs[b][...]) for b in range(nb)]

    def read_state(g, s, h):
        return sout_ref[g, h]

    def write_state(g, s, h, val):
        sout_ref[g, h] = val

    outs = _dn_core(groups, alog_ref[...], dtb_ref[...], nw_ref[...], read_state, write_state, 1, chunk)
    for b in range(nb):
        o_ref[b] = outs[b]


def _dn_prompt(proj, ba, alog_row, dtb_row, dn_nw, s0, n_batch, seq):
    chunk = min(DN_CHUNK, seq)
    assert seq % chunk == 0 and chunk % SUBLANES == 0
    nc = seq // chunk
    const2 = lambda c: (0, 0)
    rows = lambda b, col: (lambda c: (b * nc + c, col))
    batches = range(n_batch)
    o, s_out = pl.pallas_call(
        functools.partial(_dn_prompt_kernel, chunk=chunk, n_batch=n_batch),
        grid=(nc,),
        in_specs=(
            [pl.BlockSpec((chunk, DN_CONV_W), rows(b, COL_QKV // DN_CONV_W)) for b in batches]
            + [pl.BlockSpec((chunk, DN_V_W), rows(b, COL_Z // DN_V_W)) for b in batches]
            + [pl.BlockSpec((chunk, LANES), rows(b, 0)) for b in batches]
            + [
                pl.BlockSpec((1, LANES), const2),
                pl.BlockSpec((1, LANES), const2),
                pl.BlockSpec((1, DN_DV), const2),
                pl.BlockSpec((n_batch, DN_HEADS, DN_DK, DN_DV), lambda c: (0, 0, 0, 0)),
            ]
        ),
        out_specs=[
            pl.BlockSpec((n_batch, chunk, DN_V_W), lambda c: (0, c, 0)),
            pl.BlockSpec((n_batch, DN_HEADS, DN_DK, DN_DV), lambda c: (0, 0, 0, 0)),
        ],
        out_shape=[
            jax.ShapeDtypeStruct((n_batch, seq, DN_V_W), F32),
            jax.ShapeDtypeStruct((n_batch, DN_HEADS, DN_DK, DN_DV), F32),
        ],
        compiler_params=_cparams(("arbitrary",)),
        name="dn_prompt",
    )(*([proj] * n_batch), *([proj] * n_batch), *([ba] * n_batch), alog_row, dtb_row, dn_nw, s0)
    return o.reshape(n_batch * seq, DN_V_W), s_out


def _dn_sample_kernel(qkv_ref, z_ref, ba_ref, convw_ref, alog_ref, dtb_ref, nw_ref, conv0_ref, s0_ref,
                      o_ref, convout_ref, sout_ref, cbuf_ref, *, seq, n_bb):
    hist = SUBLANES - (DN_CONV - 1)
    per = SUBLANES // seq
    n_tiles = n_bb // per

    def spread(ref):
        pieces = []
        for j in range(n_tiles):
            x8 = ref[j * SUBLANES:(j + 1) * SUBLANES, :]
            for r in range(per):
                pieces.append(x8 if r == 0 else pltpu.roll(x8, SUBLANES - r * seq, axis=0))
        return pieces

    for bb, piece in enumerate(spread(qkv_ref)):
        cbuf_ref[bb, SUBLANES:2 * SUBLANES, :] = piece
    cbuf_ref[:, hist:SUBLANES, :] = conv0_ref[...]
    w = convw_ref[...]
    acc = cbuf_ref[:, hist:hist + SUBLANES, :] * w[0:1, :]
    for i in range(1, DN_CONV):
        acc = acc + cbuf_ref[:, hist + i:hist + i + SUBLANES, :] * w[i:i + 1, :]
    live = _row_iota(acc.shape) < seq
    qkv = jnp.where(live, acc * _sigmoid(acc), 0.0).reshape(n_bb * SUBLANES, DN_CONV_W)
    convout_ref[...] = cbuf_ref[:, SUBLANES + seq - (DN_CONV - 1):SUBLANES + seq, :]

    def read_state(g, s, h):
        return s0_ref[s, h]

    def write_state(g, s, h, val):
        sout_ref[s, h] = val

    group = (qkv, jnp.concatenate(spread(z_ref), axis=0), jnp.concatenate(spread(ba_ref), axis=0))
    o = _dn_core([group], alog_ref[...], dtb_ref[...], nw_ref[...], read_state, write_state, n_bb, seq)[0]
    rows = _row_iota((SUBLANES, DN_V_W))
    for j in range(n_tiles):
        tile = None
        for r in range(per):
            bb = j * per + r
            piece = o[bb * SUBLANES:(bb + 1) * SUBLANES]
            piece = piece if r == 0 else pltpu.roll(piece, r * seq, axis=0)
            tile = piece if tile is None else jnp.where(rows >= r * seq, piece, tile)
        o_ref[j * SUBLANES:(j + 1) * SUBLANES, :] = tile


def _dn_sample(proj, ba, conv_w, alog_row, dtb_row, dn_nw, conv0, s0, row0, n_batch, seq):
    assert SUBLANES % seq == 0 and seq >= DN_CONV - 1
    n_bb = SUBLANES
    rows_in = n_bb * seq
    assert n_batch % n_bb == 0 and row0 % rows_in == 0
    rb0 = row0 // rows_in
    const1 = lambda i: (0, 0)
    return pl.pallas_call(
        functools.partial(_dn_sample_kernel, seq=seq, n_bb=n_bb),
        grid=(n_batch // n_bb,),
        in_specs=[
            pl.BlockSpec((rows_in, DN_CONV_W), lambda i: (rb0 + i, COL_QKV // DN_CONV_W)),
            pl.BlockSpec((rows_in, DN_V_W), lambda i: (rb0 + i, COL_Z // DN_V_W)),
            pl.BlockSpec((rows_in, LANES), lambda i: (rb0 + i, 0)),
            pl.BlockSpec((DN_CONV, DN_CONV_W), const1),
            pl.BlockSpec((1, LANES), const1),
            pl.BlockSpec((1, LANES), const1),
            pl.BlockSpec((1, DN_DV), const1),
            pl.BlockSpec((n_bb, DN_CONV - 1, DN_CONV_W), lambda i: (i, 0, 0)),
            pl.BlockSpec((n_bb, DN_HEADS, DN_DK, DN_DV), lambda i: (i, 0, 0, 0)),
        ],
        out_specs=[
            pl.BlockSpec((rows_in, DN_V_W), lambda i: (i, 0)),
            pl.BlockSpec((n_bb, DN_CONV - 1, DN_CONV_W), lambda i: (i, 0, 0)),
            pl.BlockSpec((n_bb, DN_HEADS, DN_DK, DN_DV), lambda i: (i, 0, 0, 0)),
        ],
        out_shape=[
            jax.ShapeDtypeStruct((n_batch * seq, DN_V_W), F32),
            jax.ShapeDtypeStruct((n_batch, DN_CONV - 1, DN_CONV_W), F32),
            jax.ShapeDtypeStruct((n_batch, DN_HEADS, DN_DK, DN_DV), F32),
        ],
        scratch_shapes=[pltpu.VMEM((n_bb, 2 * SUBLANES, DN_CONV_W), F32)],
        compiler_params=_cparams(("arbitrary",)),
        name="dn_sample",
    )(proj, proj, ba, conv_w, alog_row, dtb_row, dn_nw, conv0, s0)


def _rel_bucket(dist):
    n = jnp.maximum(dist, 0)
    max_exact = REL_BUCKETS // 2
    large = max_exact + (jnp.log(jnp.maximum(n, 1).astype(F32) / max_exact)
                         / math.log(REL_MAX_DIST / max_exact) * (REL_BUCKETS - max_exact)).astype(I32)
    return jnp.where(n < max_exact, n, jnp.minimum(large, REL_BUCKETS - 1))


def _relbias_kernel(tab_ref, bucket_ref, o_ref):
    h = pl.program_id(0)
    bk = bucket_ref[...]
    acc = jnp.full(bk.shape, -jnp.inf, F32)
    for b in range(REL_BUCKETS):
        acc = jnp.where(bk == b, tab_ref[b * SW_HEADS + h], acc)
    o_ref[0] = acc


def _relbias(rel_table, bucket):
    nq, ns = bucket.shape
    return pl.pallas_call(
        _relbias_kernel,
        grid=(SW_HEADS,),
        in_specs=[
            pl.BlockSpec(memory_space=pltpu.SMEM),
            pl.BlockSpec((nq, ns), lambda h: (0, 0)),
        ],
        out_specs=pl.BlockSpec((1, nq, ns), lambda h: (h, 0, 0)),
        out_shape=jax.ShapeDtypeStruct((SW_HEADS, nq, ns), F32),
        compiler_params=_cparams(("arbitrary",)),
        name="relbias",
    )(rel_table.reshape(-1), bucket)


def _dup_halves(x):
    lo = _lane_iota(x.shape) < SW_HD
    xr = pltpu.roll(x, SW_HD, axis=1)
    return jnp.where(lo, x, xr).astype(BF16), jnp.where(lo, xr, x).astype(BF16)


def _sink_softmax_pv(s, sink, vv):
    m = jnp.maximum(jnp.max(s, axis=-1, keepdims=True), sink)
    p = jnp.exp(s - m)
    denom = jnp.sum(p, axis=-1, keepdims=True) + jnp.exp(sink - m)
    return _dot(p, vv) * (1.0 / denom)


def _swa_prompt_kernel(sinks_ref, q_ref, kc_ref, kp_ref, vc_ref, vp_ref, bias_ref, o_ref, klast_ref, vlast_ref):
    klast_ref[0] = kc_ref[...]
    vlast_ref[0] = vc_ref[...]
    kk = _dup_halves(jnp.concatenate([kp_ref[...], kc_ref[...]], axis=0))
    vv = _dup_halves(jnp.concatenate([vp_ref[...], vc_ref[...]], axis=0))
    lo = _lane_iota((SW_BLOCK, LANES)) < SW_HD
    for pair in range(SW_HEADS // 2):
        qp = q_ref[:, pair * LANES:(pair + 1) * LANES] * (SW_HD ** -0.5)
        outs = []
        for half in range(2):
            hq = 2 * pair + half
            kv = hq // SW_GROUP
            qm = jnp.where(lo if half == 0 else ~lo, qp, 0.0)
            s = _dot_nt(qm, kk[kv]) + bias_ref[0, hq]
            outs.append(_sink_softmax_pv(s, sinks_ref[hq], vv[kv]))
        o_ref[:, pair * LANES:(pair + 1) * LANES] = jnp.where(lo, outs[0], outs[1])


def _swa_prompt(proj, sinks, bias, n_batch, seq):
    assert seq % SW_BLOCK == 0 and WINDOW == SW_BLOCK
    nb = seq // SW_BLOCK
    cur = lambda col: (lambda b, i: (b * nb + i, col))
    prev = lambda col: (lambda b, i: (b * nb + jnp.maximum(i - 1, 0), col))
    return pl.pallas_call(
        _swa_prompt_kernel,
        grid=(n_batch, nb),
        in_specs=[
            pl.BlockSpec(memory_space=pltpu.SMEM),
            pl.BlockSpec((SW_BLOCK, SW_HEADS * SW_HD), cur(COL_SQ // (SW_HEADS * SW_HD))),
            pl.BlockSpec((SW_BLOCK, SW_KV_W), cur(COL_SK // SW_KV_W)),
            pl.BlockSpec((SW_BLOCK, SW_KV_W), prev(COL_SK // SW_KV_W)),
            pl.BlockSpec((SW_BLOCK, SW_KV_W), cur(COL_SV // SW_KV_W)),
            pl.BlockSpec((SW_BLOCK, SW_KV_W), prev(COL_SV // SW_KV_W)),
            pl.BlockSpec((1, SW_HEADS, SW_BLOCK, 2 * SW_BLOCK), lambda b, i: (jnp.minimum(i, 1), 0, 0, 0)),
        ],
        out_specs=[
            pl.BlockSpec((SW_BLOCK, SW_HEADS * SW_HD), lambda b, i: (b * nb + i, 0)),
            pl.BlockSpec((1, SW_BLOCK, SW_KV_W), lambda b, i: (b, 0, 0)),
            pl.BlockSpec((1, SW_BLOCK, SW_KV_W), lambda b, i: (b, 0, 0)),
        ],
        out_shape=[
            jax.ShapeDtypeStruct((n_batch * seq, SW_HEADS * SW_HD), F32),
            jax.ShapeDtypeStruct((n_batch, SW_BLOCK, SW_KV_W), F32),
            jax.ShapeDtypeStruct((n_batch, SW_BLOCK, SW_KV_W), F32),
        ],
        compiler_params=_cparams(("arbitrary", "arbitrary")),
        name="swa_prompt",
    )(sinks, proj, proj, proj, proj, proj, bias)


def _swa_sample_kernel(q_ref, kn_ref, vn_ref, kc_ref, vc_ref, bias_ref, sink_ref,
                       o_ref, ko_ref, vo_ref, kall_ref, vall_ref, *, seq, n_bb, n_cache):
    n_keys = kall_ref.shape[0]
    zeros_tail = jnp.zeros((n_keys - n_cache - SUBLANES, LANES), F32)
    lo = _lane_iota((SUBLANES, LANES)) < SW_HD
    out = None
    for bb in range(n_bb):
        shift = (SUBLANES - bb * seq) % SUBLANES

        def top(x, shift=shift):
            return x if shift == 0 else pltpu.roll(x, shift, axis=0)

        kall_ref[0:n_cache, :] = kc_ref[bb]
        kall_ref[n_cache:n_cache + SUBLANES, :] = top(kn_ref[...])
        kall_ref[n_cache + SUBLANES:, :] = zeros_tail
        vall_ref[0:n_cache, :] = vc_ref[bb]
        vall_ref[n_cache:n_cache + SUBLANES, :] = top(vn_ref[...])
        vall_ref[n_cache + SUBLANES:, :] = zeros_tail
        ko_ref[bb] = kall_ref[seq:seq + n_cache, :]
        vo_ref[bb] = vall_ref[seq:seq + n_cache, :]
        kk = _dup_halves(kall_ref[...])
        vv = _dup_halves(vall_ref[...])
        q8 = top(q_ref[...])
        pairs = []
        for kv in range(SW_KV_HEADS):
            pieces = []
            for g in range(SW_GROUP):
                hq = kv * SW_GROUP + g
                qp = q8[:, (hq // 2) * LANES:(hq // 2 + 1) * LANES]
                pieces.append(jnp.where(lo if hq % 2 == 0 else ~lo, qp, 0.0))
            qs = jnp.concatenate(pieces, axis=0) * (SW_HD ** -0.5)
            s = _dot_nt(qs, kk[kv]) + bias_ref[kv]
            res = _sink_softmax_pv(s, _col(sink_ref[kv], 0), vv[kv])
            for g in range(0, SW_GROUP, 2):
                pairs.append(jnp.where(lo, res[g * SUBLANES:(g + 1) * SUBLANES],
                                       res[(g + 1) * SUBLANES:(g + 2) * SUBLANES]))
        o = jnp.concatenate(pairs, axis=1)
        back = (bb * seq) % SUBLANES
        o = o if back == 0 else pltpu.roll(o, back, axis=0)
        rows = _row_iota(o.shape)
        sel = (rows >= bb * seq) & (rows < (bb + 1) * seq)
        out = jnp.where(sel, o, 0.0 if out is None else out)
    o_ref[...] = out


def _swa_sample(proj, k_cache, v_cache, bias, sink_rows, row0, n_batch, seq):
    assert SUBLANES % seq == 0
    n_bb = SUBLANES // seq
    n_cache = k_cache.shape[1]
    assert n_batch % n_bb == 0 and row0 % SUBLANES == 0 and n_cache % SUBLANES == 0
    n_keys = bias.shape[-1]
    rb0 = row0 // SUBLANES
    blk = lambda col: (lambda i: (rb0 + i, col))
    return pl.pallas_call(
        functools.partial(_swa_sample_kernel, seq=seq, n_bb=n_bb, n_cache=n_cache),
        grid=(n_batch // n_bb,),
        in_specs=[
            pl.BlockSpec((SUBLANES, SW_HEADS * SW_HD), blk(COL_SQ // (SW_HEADS * SW_HD))),
            pl.BlockSpec((SUBLANES, SW_KV_W), blk(COL_SK // SW_KV_W)),
            pl.BlockSpec((SUBLANES, SW_KV_W), blk(COL_SV // SW_KV_W)),
            pl.BlockSpec((n_bb, n_cache, SW_KV_W), lambda i: (i, 0, 0)),
            pl.BlockSpec((n_bb, n_cache, SW_KV_W), lambda i: (i, 0, 0)),
            pl.BlockSpec((SW_KV_HEADS, SW_GROUP * SUBLANES, n_keys), lambda i: (0, 0, 0)),
            pl.BlockSpec((SW_KV_HEADS, SW_GROUP * SUBLANES, n_keys), lambda i: (0, 0, 0)),
        ],
        out_specs=[
            pl.BlockSpec((SUBLANES, SW_HEADS * SW_HD), lambda i: (i, 0)),
            pl.BlockSpec((n_bb, n_cache, SW_KV_W), lambda i: (i, 0, 0)),
            pl.BlockSpec((n_bb, n_cache, SW_KV_W), lambda i: (i, 0, 0)),
        ],
        out_shape=[
            jax.ShapeDtypeStruct((n_batch * seq, SW_HEADS * SW_HD), F32),
            jax.ShapeDtypeStruct(k_cache.shape, F32),
            jax.ShapeDtypeStruct(v_cache.shape, F32),
        ],
        scratch_shapes=[pltpu.VMEM((n_keys, SW_KV_W), F32), pltpu.VMEM((n_keys, SW_KV_W), F32)],
        compiler_params=_cparams(("arbitrary",)),
        name="swa_sample",
    )(proj, proj, proj, k_cache, v_cache, bias, sink_rows)


def _mix_kernel(xp_ref, xs_ref, oap_ref, oas_ref, obp_ref, obs_ref, ga_ref, gb_ref, wo_ref, nw_ref, wr_ref, br_ref,
                x1_ref, h2_ref, route_ref, *, n_prompt_tiles):
    i = pl.program_id(0)

    def run(x_ref, oa_ref, ob_ref):
        mixed = _sigmoid(ga_ref[...]) * oa_ref[...] + _sigmoid(gb_ref[...]) * ob_ref[...]
        x1 = x_ref[...] + _dot(mixed, wo_ref[...])
        x1_ref[...] = x1
        h2 = x1 * lax.rsqrt(jnp.mean(x1 * x1, axis=-1, keepdims=True) + EPS) * nw_ref[...]
        h2_ref[...] = h2
        logits = _dot(h2, wr_ref[...]) + br_ref[...]
        lane = _lane_iota(logits.shape)
        lanef = lane.astype(F32)
        big = float(2 * LANES)
        is_g = lane < N_GROUPS
        gl = jnp.where(is_g, logits, -jnp.inf)
        gmax = jnp.max(gl, axis=-1, keepdims=True)
        gval = 1.0 / jnp.sum(jnp.where(is_g, jnp.exp(gl - gmax), 0.0), axis=-1, keepdims=True)
        grp = jnp.min(jnp.where(gl == gmax, lanef, big), axis=-1, keepdims=True)
        e_grp = ((lane - N_GROUPS) >> 3).astype(F32)
        is_e = (lane >= N_GROUPS) & (lane < N_GROUPS + N_EXPERTS) & (e_grp == grp)
        el = jnp.where(is_e, logits, -jnp.inf)
        v1 = jnp.max(el, axis=-1, keepdims=True)
        i1 = jnp.min(jnp.where(el == v1, lanef, big), axis=-1, keepdims=True)
        el2 = jnp.where(lanef == i1, -jnp.inf, el)
        v2 = jnp.max(el2, axis=-1, keepdims=True)
        i2 = jnp.min(jnp.where(el2 == v2, lanef, big), axis=-1, keepdims=True)
        e2 = jnp.exp(v2 - v1)
        w1 = gval / (1.0 + e2)
        w2 = gval * e2 / (1.0 + e2)
        route_ref[...] = jnp.where(lane == 0, i1 - N_GROUPS,
                                   jnp.where(lane == 1, i2 - N_GROUPS,
                                             jnp.where(lane == 2, w1, jnp.where(lane == 3, w2, 0.0))))

    @pl.when(i < n_prompt_tiles)
    def _():
        run(xp_ref, oap_ref, obp_ref)

    @pl.when(i >= n_prompt_tiles)
    def _():
        run(xs_ref, oas_ref, obs_ref)


def _mix(xp, xs, oa_p, oa_s, ob_p, ob_s, proj, w_out, norm_w, w_router, b_router):
    tp, ts = xp.shape[0], xs.shape[0]
    tm = _token_tile(tp, ts)
    npt, nst = tp // tm, ts // tm
    const = lambda i: (0, 0)
    row = lambda i: (i, 0)
    return pl.pallas_call(
        functools.partial(_mix_kernel, n_prompt_tiles=npt),
        grid=(npt + nst,),
        in_specs=[
            pl.BlockSpec((tm, D_MODEL), lambda i: (jnp.minimum(i, npt - 1), 0)),
            pl.BlockSpec((tm, D_MODEL), lambda i: (jnp.maximum(i - npt, 0), 0)),
            pl.BlockSpec((tm, D_MODEL), lambda i: (jnp.minimum(i, npt - 1), 0)),
            pl.BlockSpec((tm, D_MODEL), lambda i: (jnp.maximum(i - npt, 0), 0)),
            pl.BlockSpec((tm, D_MODEL), lambda i: (jnp.minimum(i, npt - 1), 0)),
            pl.BlockSpec((tm, D_MODEL), lambda i: (jnp.maximum(i - npt, 0), 0)),
            pl.BlockSpec((tm, D_MODEL), lambda i: (i, COL_GA // D_MODEL)),
            pl.BlockSpec((tm, D_MODEL), lambda i: (i, COL_GB // D_MODEL)),
            pl.BlockSpec((D_MODEL, D_MODEL), const),
            pl.BlockSpec((1, D_MODEL), const),
            pl.BlockSpec((D_MODEL, LANES), const),
            pl.BlockSpec((1, LANES), const),
        ],
        out_specs=[
            pl.BlockSpec((tm, D_MODEL), row),
            pl.BlockSpec((tm, D_MODEL), row),
            pl.BlockSpec((tm, LANES), row),
        ],
        out_shape=[
            jax.ShapeDtypeStruct((tp + ts, D_MODEL), F32),
            jax.ShapeDtypeStruct((tp + ts, D_MODEL), F32),
            jax.ShapeDtypeStruct((tp + ts, LANES), F32),
        ],
        compiler_params=_cparams(("arbitrary",)),
        name="mix_router",
    )(xp, xs, oa_p, oa_s, ob_p, ob_s, proj, proj, w_out, norm_w, w_router, b_router)


def _rank_kernel(route_ref, dest_ref, meta_ref, rank_ref, cnt_ref, *, tile, blk):
    phase = pl.program_id(0)
    i = pl.program_id(1)
    shape = (tile, LANES)
    lane = _lane_iota(shape)
    lanef = lane.astype(F32)
    r = route_ref[...]
    oh0 = lanef == _col(r, 0)
    oh1 = lanef == _col(r, 1)
    rows = pl.ds(pl.multiple_of(i * tile, tile), tile)

    @pl.when(phase == 0)
    def _():
        @pl.when(i == 0)
        def _():
            cnt_ref[...] = jnp.zeros(cnt_ref.shape, F32)

        oh = jnp.where(oh0 | oh1, 1.0, 0.0)
        tri = jnp.where(_row_iota((tile, tile)) > _lane_iota((tile, tile)), 1.0, 0.0)
        before = _dot(tri, oh) + cnt_ref[0:1, :]
        rank0 = jnp.sum(jnp.where(oh0, before, 0.0), axis=-1, keepdims=True)
        rank1 = jnp.sum(jnp.where(oh1, before, 0.0), axis=-1, keepdims=True)
        rank_ref[rows, :] = jnp.where(lane == 0, rank0, jnp.where(lane == 1, rank1, 0.0))
        cnt_ref[0:1, :] = cnt_ref[0:1, :] + jnp.sum(oh, axis=0, keepdims=True)

    @pl.when(phase == 1)
    def _():
        cnt = cnt_ref[0:1, :]
        padded = jnp.floor((cnt + (blk - 1)) / blk) * blk
        before_lane = jnp.where(_row_iota((LANES, LANES)) < _lane_iota((LANES, LANES)), 1.0, 0.0)
        start = _dot_exact(jnp.broadcast_to(padded, (SUBLANES, LANES)), before_lane)[0:1, :]
        rk = rank_ref[rows, :]
        d0 = jnp.sum(jnp.where(oh0, start, 0.0), axis=-1, keepdims=True) + _col(rk, 0)
        d1 = jnp.sum(jnp.where(oh1, start, 0.0), axis=-1, keepdims=True) + _col(rk, 1)
        dest_ref[...] = jnp.where(lane == 0, d0, jnp.where(lane == 1, d1, 0.0)).astype(I32)

        @pl.when(i == 0)
        def _():
            end = start + padded
            mshape = meta_ref.shape
            blk_start = (_row_iota(mshape) * blk).astype(F32)
            hit = (_lane_iota(mshape) < N_EXPERTS) & (end <= blk_start)
            be = jnp.minimum(jnp.sum(jnp.where(hit, 1.0, 0.0), axis=-1, keepdims=True), N_EXPERTS - 1.0)
            n_used = _col(end, N_EXPERTS - 1) / blk
            ml = _lane_iota(mshape)
            mine = ml.astype(F32) == be
            seg_start = jnp.sum(jnp.where(mine, start, 0.0), axis=-1, keepdims=True)
            seg_count = jnp.sum(jnp.where(mine, cnt, 0.0), axis=-1, keepdims=True)
            n_valid = jnp.clip(seg_count - (blk_start[:, 0:1] - seg_start), 0.0, float(blk))
            meta_ref[...] = jnp.where(ml == 0, be, jnp.where(ml == 1, n_used,
                                                              jnp.where(ml == 2, n_valid, 0.0))).astype(I32)


def _rank(route, tile, blk, n_blocks):
    t = route.shape[0]
    nt = t // tile
    nbp = -(-n_blocks // SUBLANES) * SUBLANES
    return pl.pallas_call(
        functools.partial(_rank_kernel, tile=tile, blk=blk),
        grid=(2, nt),
        in_specs=[pl.BlockSpec((tile, LANES), lambda p, i: (i, 0))],
        out_specs=[
            pl.BlockSpec((tile, LANES), lambda p, i: (i * p, 0)),
            pl.BlockSpec((nbp, LANES), lambda p, i: (0, 0)),
        ],
        out_shape=[
            jax.ShapeDtypeStruct((t, LANES), I32),
            jax.ShapeDtypeStruct((nbp, LANES), I32),
        ],
        scratch_shapes=[pltpu.VMEM((t, LANES), F32), pltpu.VMEM((SUBLANES, LANES), F32)],
        compiler_params=_cparams(("arbitrary", "arbitrary")),
        name="moe_rank",
    )(route)


def _row_copy(src, src_row, dst, dst_row, sem):
    return pltpu.make_async_copy(src.at[pl.ds(src_row, 1)], dst.at[pl.ds(dst_row, 1)], sem)


def _last_used(i, nu_ref):
    return jnp.minimum(i, jnp.maximum(nu_ref[0] - 1, 0))


def _for_rows(n, body):
    def group(g, carry):
        for u in range(SUBLANES):
            body(g * SUBLANES + u, u)
        return carry

    def single(t, carry):
        body(t, 0)
        return carry

    n_groups = n // SUBLANES
    lax.fori_loop(0, n_groups, group, 0)
    lax.fori_loop(n_groups * SUBLANES, n, single, 0)


def _expert_kernel(be_ref, nu_ref, nv_ref, dest_ref, h2_ref, wg_ref, wu_ref, wd_ref, y_ref,
                   xbuf_ref, wgu_ref, wdn_ref, inv_ref, sem, *, blk):
    i = pl.program_id(0)
    n_used = nu_ref[0]
    used = i < n_used
    slot = i % 2
    blk_i = _last_used(i, nu_ref)
    fresh = (i == 0) | (be_ref[blk_i] != be_ref[jnp.maximum(blk_i - 1, 0)])

    def gather(block, into):
        _for_rows(nv_ref[block], lambda t, u: _row_copy(h2_ref, inv_ref[block * blk + t], xbuf_ref.at[into], t,
                                                       sem.at[into]).start(priority=u % 2))

    @pl.when(i == 0)
    def _():
        def place(a, carry):
            inv_ref[dest_ref[a]] = a >> 1
            return carry

        lax.fori_loop(0, dest_ref.shape[0], place, 0, unroll=16)
        xbuf_ref[...] = jnp.zeros(xbuf_ref.shape, F32)
        gather(0, 0)

    @pl.when(i + 1 < n_used)
    def _():
        gather(i + 1, 1 - slot)

    @pl.when(used & fresh)
    def _():
        wgu_ref[:, :D_EXPERT] = wg_ref[0].astype(BF16)
        wgu_ref[:, D_EXPERT:] = wu_ref[0].astype(BF16)
        wdn_ref[...] = wd_ref[0].astype(BF16)

    @pl.when(used)
    def _():
        _for_rows(nv_ref[i], lambda t, u: _row_copy(h2_ref, 0, xbuf_ref.at[slot], 0, sem.at[slot]).wait())
        gu = jnp.dot(xbuf_ref[slot].astype(BF16), wgu_ref[...], preferred_element_type=F32)
        g = gu[:, :D_EXPERT]
        hidden = (g * _sigmoid(g) * gu[:, D_EXPERT:]).astype(BF16)
        y_ref[...] = jnp.dot(hidden, wdn_ref[...], preferred_element_type=F32)

    @pl.when(jnp.logical_not(used))
    def _():
        y_ref[...] = jnp.zeros(y_ref.shape, F32)


def _experts(block_expert, n_used, n_valid, dest_flat, h2, w_gate, w_up, w_down, blk):
    n_blocks = block_expert.shape[0]
    wsel = lambda i, be, nu, nv, de: (be[_last_used(i, nu)], 0, 0)
    return pl.pallas_call(
        functools.partial(_expert_kernel, blk=blk),
        grid_spec=pltpu.PrefetchScalarGridSpec(
            num_scalar_prefetch=4,
            grid=(n_blocks,),
            in_specs=[
                pl.BlockSpec(memory_space=pl.ANY),
                pl.BlockSpec((1, D_MODEL, D_EXPERT), wsel),
                pl.BlockSpec((1, D_MODEL, D_EXPERT), wsel),
                pl.BlockSpec((1, D_EXPERT, D_MODEL), wsel),
            ],
            out_specs=pl.BlockSpec((blk, D_MODEL), lambda i, be, nu, nv, de: (i, 0)),
            scratch_shapes=[
                pltpu.VMEM((2, blk, D_MODEL), F32),
                pltpu.VMEM((D_MODEL, 2 * D_EXPERT), BF16),
                pltpu.VMEM((D_EXPERT, D_MODEL), BF16),
                pltpu.SMEM((n_blocks * blk,), I32),
                pltpu.SemaphoreType.DMA((2,)),
            ],
        ),
        out_shape=jax.ShapeDtypeStruct((n_blocks * blk, D_MODEL), F32),
        compiler_params=_cparams(("arbitrary",)),
        name="moe_experts",
    )(block_expert, n_used, n_valid, dest_flat, h2, w_gate, w_up, w_down)


def _combine_kernel(dest_ref, x1_ref, route_ref, nw_ref, ys_ref, yp_ref, ysm_ref, ybuf_ref, sem,
                    *, tile, n_prompt_tiles):
    i = pl.program_id(0)
    slot = i % 2

    def gather(step, into):
        def issue(t, carry):
            for k in range(2):
                _row_copy(ys_ref, dest_ref[2 * (step * tile + t) + k], ybuf_ref.at[into], k * tile + t,
                          sem.at[into]).start(priority=k)
            return carry

        lax.fori_loop(0, tile, issue, 0, unroll=4)

    @pl.when(i == 0)
    def _():
        gather(0, 0)

    @pl.when(i + 1 < pl.num_programs(0))
    def _():
        gather(i + 1, 1 - slot)

    def drain(t, carry):
        _row_copy(ys_ref, 0, ybuf_ref.at[slot], 0, sem.at[slot]).wait()
        return carry

    lax.fori_loop(0, 2 * tile, drain, 0, unroll=8)
    r = route_ref[...]
    ybuf = ybuf_ref.at[slot]
    y = ybuf[0:tile, :] * _col(r, 2) + ybuf[tile:2 * tile, :] * _col(r, 3)
    x2 = x1_ref[...] + y
    out = x2 * lax.rsqrt(jnp.mean(x2 * x2, axis=-1, keepdims=True) + EPS) * nw_ref[...]

    @pl.when(i < n_prompt_tiles)
    def _():
        yp_ref[...] = out

    @pl.when(i >= n_prompt_tiles)
    def _():
        ysm_ref[...] = out


def _combine(dest_flat, x1, route, norm_w, ys, tp, ts):
    tile = _token_tile(tp, ts)
    npt, nst = tp // tile, ts // tile
    return pl.pallas_call(
        functools.partial(_combine_kernel, tile=tile, n_prompt_tiles=npt),
        grid_spec=pltpu.PrefetchScalarGridSpec(
            num_scalar_prefetch=1,
            grid=(npt + nst,),
            in_specs=[
                pl.BlockSpec((tile, D_MODEL), lambda i, d: (i, 0)),
                pl.BlockSpec((tile, LANES), lambda i, d: (i, 0)),
                pl.BlockSpec((1, D_MODEL), lambda i, d: (0, 0)),
                pl.BlockSpec(memory_space=pl.ANY),
            ],
            out_specs=[
                pl.BlockSpec((tile, D_MODEL), lambda i, d: (jnp.minimum(i, npt - 1), 0)),
                pl.BlockSpec((tile, D_MODEL), lambda i, d: (jnp.maximum(i - npt, 0), 0)),
            ],
            scratch_shapes=[pltpu.VMEM((2, 2 * tile, D_MODEL), F32), pltpu.SemaphoreType.DMA((2,))],
        ),
        out_shape=[
            jax.ShapeDtypeStruct((tp, D_MODEL), F32),
            jax.ShapeDtypeStruct((ts, D_MODEL), F32),
        ],
        compiler_params=_cparams(("arbitrary",)),
        name="moe_combine",
    )(dest_flat, x1, route, norm_w, ys)


def _layer(xp, xs, n_batch, seq, s_batch, s_seq, conv_state, dn_state, k_cache, v_cache,
           w_in, conv_w, a_log, dt_bias, dn_norm_w, sinks, rel_bias, w_out, norm_mix_w, norm_ffn_w,
           w_rg, b_rg, w_re, b_re, w_gate, w_up, w_down, norm_final_w):
    tp, ts = xp.shape[0], xs.shape[0]
    t_all = tp + ts
    row = lambda v: v.reshape(1, -1).astype(F32)

    o = np.cumsum((0, DN_QK_W, DN_QK_W, DN_V_W, DN_V_W, DN_HEADS, DN_HEADS, SW_HEADS * SW_HD, SW_KV_W, SW_KV_W,
                   D_MODEL, D_MODEL)).tolist()
    w_big = jnp.concatenate([w_in[:, o[0]:o[4]], w_in[:, o[6]:o[7]], w_in[:, o[9]:o[11]], w_in[:, o[7]:o[9]]],
                            axis=1).astype(BF16)
    w_small = jnp.pad(w_in[:, o[4]:o[6]], ((0, 0), (0, LANES - 2 * DN_HEADS))).astype(BF16)
    head_row = lambda v: jnp.pad(v.astype(F32), (DN_HEADS, LANES - 2 * DN_HEADS)).reshape(1, LANES)
    w_router = jnp.pad(jnp.concatenate([w_rg, w_re], axis=1),
                       ((0, 0), (0, LANES - N_GROUPS - N_EXPERTS))).astype(BF16)
    b_router = jnp.pad(jnp.concatenate([b_rg, b_re]).astype(F32), (0, LANES - N_GROUPS - N_EXPERTS)).reshape(1, LANES)

    conv0 = jnp.zeros((n_batch, DN_CONV - 1, DN_CONV_W), F32)
    proj, ba, conv_tail = _inproj(xp, xs, row(norm_mix_w), w_big, w_small, conv_w.astype(F32), conv0, seq)
    p_conv = conv_tail[:, SUBLANES - (DN_CONV - 1):, :]

    dn0 = jnp.zeros((n_batch, DN_HEADS, DN_DK, DN_DV), F32)
    oa_p, p_dn = _dn_prompt(proj, ba, head_row(a_log), head_row(dt_bias), row(dn_norm_w), dn0, n_batch, seq)
    oa_s, s_conv, s_dn = _dn_sample(proj, ba, conv_w.astype(F32), head_row(a_log), head_row(dt_bias), row(dn_norm_w),
                                    conv_state, dn_state, tp, s_batch, s_seq)

    qpos = jnp.arange(SW_BLOCK)[:, None]
    kpos = jnp.arange(2 * SW_BLOCK)[None, :] - SW_BLOCK
    in_window = lambda dist: (dist >= 0) & (dist < WINDOW)
    masked_bucket = lambda dist, ok: jnp.where(ok, _rel_bucket(dist), -1)
    bias_p = jnp.stack([
        _relbias(rel_bias.astype(F32), masked_bucket(qpos - kpos, in_window(qpos - kpos) & (kpos >= 0))),
        _relbias(rel_bias.astype(F32), masked_bucket(qpos - kpos, in_window(qpos - kpos)))])
    ob_p, p_k, p_v = _swa_prompt(proj, sinks.astype(F32), bias_p, n_batch, seq)
    n_cache = k_cache.shape[1]
    n_keys = -(-(n_cache + SUBLANES) // LANES) * LANES
    tq = n_cache + jnp.arange(SUBLANES)[:, None]
    dist_s = tq - jnp.arange(n_keys)[None, :]
    bias_s = _relbias(rel_bias.astype(F32), masked_bucket(dist_s, in_window(dist_s)))
    bias_s = bias_s.reshape(SW_KV_HEADS, SW_GROUP * SUBLANES, n_keys)
    sink_rows = jnp.broadcast_to(jnp.repeat(sinks.astype(F32).reshape(SW_KV_HEADS, SW_GROUP), SUBLANES, axis=1)[:, :, None],
                                 (SW_KV_HEADS, SW_GROUP * SUBLANES, n_keys))
    ob_s, s_k, s_v = _swa_sample(proj, k_cache.reshape(s_batch, n_cache, SW_KV_W), v_cache.reshape(s_batch, n_cache, SW_KV_W),
                                 bias_s, sink_rows, tp, s_batch, s_seq)

    x1, h2, route = _mix(xp, xs, oa_p, oa_s, ob_p, ob_s, proj, w_out.astype(BF16), row(norm_ffn_w), w_router, b_router)

    tile = _token_tile(tp, ts)
    n_blocks = -(-2 * t_all // MOE_BLOCK) + N_EXPERTS
    dest, meta = _rank(route, _token_tile(t_all, cands=(512, 256, 128, 64, 32, 16, 8)), MOE_BLOCK, n_blocks)
    dest_flat = dest[:, :2].reshape(-1)
    block_expert = meta[:n_blocks, 0]
    n_used = meta[0:1, 1]
    n_valid = meta[:n_blocks, 2]
    ys = _experts(block_expert, n_used, n_valid, dest_flat, h2, w_gate, w_up, w_down, MOE_BLOCK)
    y_p, y_s = _combine(dest_flat, x1, route, row(norm_final_w), ys, tp, ts)

    kv_shape = (n_batch, WINDOW, SW_KV_HEADS, SW_HD)
    return (y_p, y_s, p_conv, p_dn, p_k.reshape(kv_shape), p_v.reshape(kv_shape), s_conv, s_dn,
            s_k.reshape(k_cache.shape), s_v.reshape(v_cache.shape))


def kernel(x_prompt, x_sample, state_dn_conv, state_dn, cache_swa_k, cache_swa_v, w_in, conv_w, a_log, dt_bias, dn_norm_w, sinks, rel_bias, w_out, norm_mix_w, norm_ffn_w, w_router_group, b_router_group, w_router_expert, b_router_expert, w_gate, w_up, w_down, norm_final_w):
    depth = w_in.shape[0]
    assert depth == 1, "the final-norm fusion below assumes a single layer"
    n_batch, seq, _ = x_prompt.shape
    s_batch, s_seq, _ = x_sample.shape
    outs = _layer(x_prompt.reshape(-1, D_MODEL), x_sample.reshape(-1, D_MODEL), n_batch, seq, s_batch, s_seq,
                  state_dn_conv[0], state_dn[0], cache_swa_k[0], cache_swa_v[0],
                  w_in[0], conv_w[0], a_log[0], dt_bias[0], dn_norm_w[0], sinks[0], rel_bias,
                  w_out[0], norm_mix_w[0], norm_ffn_w[0], w_router_group[0], b_router_group[0],
                  w_router_expert[0], b_router_expert[0], w_gate[0], w_up[0], w_down[0], norm_final_w)
    y_p, y_s, p_conv, p_dn, p_k, p_v, s_conv, s_dn, s_k, s_v = outs
    return (y_p.reshape(x_prompt.shape), y_s.reshape(x_sample.shape), p_conv[None], p_dn[None], p_k[None], p_v[None],
            s_conv[None], s_dn[None], s_k[None], s_v[None])
```

```python
import functools
import math

import jax
import jax.numpy as jnp
import numpy as np
from jax import lax
from jax.experimental import pallas as pl
from jax.experimental.pallas import tpu as pltpu

F32 = jnp.float32
BF16 = jnp.bfloat16
I32 = jnp.int32

D_MODEL = 1024
DN_HEADS = 8
DN_DK = 128
DN_DV = 128
DN_CONV = 4
DN_CHUNK = 64
DN_QK_W = DN_HEADS * DN_DK
DN_V_W = DN_HEADS * DN_DV
DN_CONV_W = 2 * DN_QK_W + DN_V_W
SW_HEADS = 16
SW_KV_HEADS = 2
SW_GROUP = SW_HEADS // SW_KV_HEADS
SW_HD = 64
SW_KV_W = SW_KV_HEADS * SW_HD
WINDOW = 128
SW_BLOCK = 128
REL_BUCKETS = 32
REL_MAX_DIST = 128
N_GROUPS = 8
EXP_PER_GROUP = 8
N_EXPERTS = N_GROUPS * EXP_PER_GROUP
D_EXPERT = 256
MOE_BLOCK = 256
EPS = 1e-6

LANES = 128
SUBLANES = 8
VMEM_LIMIT = 56 * 1024 * 1024

COL_QKV = 0
COL_Z = 3072
COL_SQ = 4096
COL_GA = 5120
COL_GB = 6144
COL_SK = 7168
COL_SV = 7296
PROJ_W = 7424
PROJ_CHUNK = 512


def _cparams(sem):
    return pltpu.CompilerParams(dimension_semantics=sem, vmem_limit_bytes=VMEM_LIMIT)


def _sigmoid(x):
    return 0.5 * jnp.tanh(0.5 * x) + 0.5


def _dot(a, b):
    return jnp.dot(a.astype(BF16), b.astype(BF16), preferred_element_type=F32)


def _dot_nt(a, b):
    return lax.dot_general(a.astype(BF16), b.astype(BF16), (((1,), (1,)), ((), ())), preferred_element_type=F32)


def _dot_tn(a, b):
    return lax.dot_general(a.astype(BF16), b.astype(BF16), (((0,), (0,)), ((), ())), preferred_element_type=F32)


def _dot_exact(a, b):
    return jnp.dot(a, b, precision=lax.Precision.HIGHEST, preferred_element_type=F32)


def _lane_iota(shape):
    return lax.broadcasted_iota(I32, shape, len(shape) - 1)


def _row_iota(shape):
    return lax.broadcasted_iota(I32, shape, len(shape) - 2)


def _col(x, j):
    return jnp.sum(jnp.where(_lane_iota(x.shape) == j, x, 0.0), axis=-1, keepdims=True)


def _token_tile(*sizes, cands=(256, 128, 64, 32, 16, 8)):
    for t in cands:
        if all(s % t == 0 for s in sizes):
            return t
    raise ValueError(f"token counts {sizes} need a common tile that is a multiple of 8")


def _inproj_kernel(xp_ref, xs_ref, nw_ref, wb_ref, ws_ref, convw_ref, conv0_ref, proj_ref, ba_ref, tail_ref, cbuf_ref,
                   *, n_prompt_tiles, tiles_per_seq):
    i = pl.program_id(0)
    tm = xp_ref.shape[0]
    hist = SUBLANES - (DN_CONV - 1)

    def project(x_ref, conv):
        x = x_ref[...]
        h = (x * lax.rsqrt(jnp.mean(x * x, axis=-1, keepdims=True) + EPS) * nw_ref[...]).astype(BF16)
        ba_ref[...] = jnp.dot(h, ws_ref[...], preferred_element_type=F32)
        top = _row_iota((SUBLANES, PROJ_CHUNK))
        starts = list(range(0, PROJ_W, PROJ_CHUNK))
        if conv:
            with_conv = [c for c in starts if c + PROJ_CHUNK <= COL_QKV + DN_CONV_W]
            plain = [c for c in starts if c not in with_conv]
            starts = [c for pair in zip(with_conv, plain) for c in pair] + plain[len(with_conv):]
        for c0 in starts:
            c1 = min(c0 + PROJ_CHUNK, PROJ_W)
            cur = jnp.dot(h, wb_ref[:, c0:c1], preferred_element_type=F32)
            if conv and c1 <= COL_QKV + DN_CONV_W:
                prev = cbuf_ref[:, c0:c1]
                acc = cur * convw_ref[DN_CONV - 1:DN_CONV, c0:c1]
                for s in range(1, DN_CONV):
                    sh = pltpu.roll(cur, s, axis=0)
                    head = jnp.where(top < s, pltpu.roll(prev, s, axis=0), sh[:SUBLANES])
                    sh = jnp.concatenate([head, sh[SUBLANES:]], axis=0)
                    acc = acc + sh * convw_ref[DN_CONV - 1 - s:DN_CONV - s, c0:c1]
                cbuf_ref[:, c0:c1] = cur[tm - SUBLANES:]
                cur = acc * _sigmoid(acc)
            proj_ref[:, c0:c1] = cur

    @pl.when(i < n_prompt_tiles)
    def _():
        @pl.when(i % tiles_per_seq == 0)
        def _():
            cbuf_ref[...] = jnp.zeros(cbuf_ref.shape, F32)
            cbuf_ref[hist:SUBLANES, :] = conv0_ref[0]

        project(xp_ref, True)
        tail_ref[0] = cbuf_ref[...]

    @pl.when(i >= n_prompt_tiles)
    def _():
        project(xs_ref, False)


def _inproj(xp, xs, norm_w, w_big, w_small, conv_w, conv0, seq):
    tp, ts = xp.shape[0], xs.shape[0]
    tm = _token_tile(tp, ts, seq)
    assert COL_QKV == 0 and DN_CONV_W % PROJ_CHUNK == 0
    npt, nst = tp // tm, ts // tm
    tps = seq // tm
    const = lambda i: (0, 0)
    seq_of = lambda i: (jnp.minimum(i, npt - 1) // tps, 0, 0)
    return pl.pallas_call(
        functools.partial(_inproj_kernel, n_prompt_tiles=npt, tiles_per_seq=tps),
        grid=(npt + nst,),
        in_specs=[
            pl.BlockSpec((tm, D_MODEL), lambda i: (jnp.minimum(i, npt - 1), 0)),
            pl.BlockSpec((tm, D_MODEL), lambda i: (jnp.maximum(i - npt, 0), 0)),
            pl.BlockSpec((1, D_MODEL), const),
            pl.BlockSpec((D_MODEL, PROJ_W), const, pipeline_mode=pl.Buffered(1)),
            pl.BlockSpec((D_MODEL, LANES), const),
            pl.BlockSpec((DN_CONV, DN_CONV_W), const),
            pl.BlockSpec((1, DN_CONV - 1, DN_CONV_W), seq_of),
        ],
        out_specs=[
            pl.BlockSpec((tm, PROJ_W), lambda i: (i, 0)),
            pl.BlockSpec((tm, LANES), lambda i: (i, 0)),
            pl.BlockSpec((1, SUBLANES, DN_CONV_W), seq_of),
        ],
        out_shape=[
            jax.ShapeDtypeStruct((tp + ts, PROJ_W), F32),
            jax.ShapeDtypeStruct((tp + ts, LANES), F32),
            jax.ShapeDtypeStruct((tp // seq, SUBLANES, DN_CONV_W), F32),
        ],
        scratch_shapes=[pltpu.VMEM((SUBLANES, DN_CONV_W), F32)],
        compiler_params=_cparams(("arbitrary",)),
        name="inproj",
    )(xp, xs, norm_w, w_big, w_small, conv_w, conv0)


def _dn_core(groups, alog, dtb, nw, read_state, write_state, n_seg, seg_valid):
    rows = groups[0][0].shape[0]
    sr = rows // n_seg
    assert sr * n_seg == rows and sr & (sr - 1) == 0 and rows <= LANES
    seg_shift = sr.bit_length() - 1
    ri = _row_iota((rows, rows))
    ci = _lane_iota((rows, rows))
    incl = ri >= ci
    strict = ri > ci
    if n_seg > 1:
        same = (ri >> seg_shift) == (ci >> seg_shift)
        incl = incl & same
        strict = strict & same
    l_incl = incl.astype(F32)
    eye = (ri == ci).astype(F32)
    levels = max(1, math.ceil(math.log2(seg_valid)))

    beta_all, gsum_all, gtot_all, gsum_t = [], [], [], []
    for _, _, ba in groups:
        b_all = _sigmoid(ba)
        sp = ba + dtb
        softplus = jnp.maximum(sp, 0.0) + jnp.log1p(jnp.exp(-jnp.abs(sp)))
        g_all = -jnp.exp(alog) * softplus
        if seg_valid < sr:
            live = (_row_iota((rows, LANES)) & (sr - 1)) < seg_valid
            b_all = jnp.where(live, b_all, 0.0)
            g_all = jnp.where(live, g_all, 0.0)
        gs = _dot_exact(l_incl, g_all)
        beta_all.append(b_all)
        gsum_all.append(gs)
        gtot_all.append(_dot_exact(same.astype(F32), g_all) if n_seg > 1 else gs[rows - 1:rows, :])
        padded = gs if rows == LANES else jnp.concatenate([gs, jnp.zeros((LANES - rows, LANES), F32)], axis=0)
        gsum_t.append(padded.T)

    probs = [(g, h) for g in range(len(groups)) for h in range(DN_HEADS)]
    segs = range(n_seg)
    q, k, v, kb, beta, gsum, gtot = {}, {}, {}, {}, {}, {}, {}
    for p in probs:
        g, h = p
        qkv = groups[g][0]
        qh = qkv[:, h * DN_DK:(h + 1) * DN_DK]
        kh = qkv[:, DN_QK_W + h * DN_DK:DN_QK_W + (h + 1) * DN_DK]
        v[p] = qkv[:, 2 * DN_QK_W + h * DN_DV:2 * DN_QK_W + (h + 1) * DN_DV]
        q[p] = qh * lax.rsqrt(jnp.sum(qh * qh, axis=-1, keepdims=True) + 1e-6) * (DN_DK ** -0.5)
        k[p] = kh * lax.rsqrt(jnp.sum(kh * kh, axis=-1, keepdims=True) + 1e-6)
        beta[p] = _col(beta_all[g], h)
        gsum[p] = _col(gsum_all[g], DN_HEADS + h)
        gtot[p] = _col(gtot_all[g], DN_HEADS + h)
        kb[p] = k[p] * beta[p]
    kq = {p: _dot_nt(jnp.concatenate([kb[p], q[p]], axis=0), k[p]) for p in probs}
    gamma = {(g, h): jnp.exp(jnp.where(incl, gsum[(g, h)] - gsum_t[g][DN_HEADS + h:DN_HEADS + h + 1, :rows], -jnp.inf))
             for g, h in probs}
    attn = {p: kq[p][rows:] * gamma[p] for p in probs}
    pw = {p: -jnp.where(strict, kq[p][:rows] * gamma[p], 0.0) for p in probs}
    t = {p: eye + pw[p] for p in probs}
    for _ in range(1, levels):
        pw = {p: _dot(pw[p], pw[p]) for p in probs}
        t = {p: t[p] + _dot(t[p], pw[p]) for p in probs}
    eg = {p: jnp.exp(gsum[p]) for p in probs}
    uw = {p: _dot(t[p], jnp.concatenate([v[p] * beta[p], kb[p] * eg[p]], axis=1)) for p in probs}
    qg = {p: q[p] * eg[p] for p in probs}
    state = {(p, s): read_state(p[0], s, p[1]) for p in probs for s in segs}
    wq = {(p, s): _dot(jnp.concatenate([uw[p][s * sr:(s + 1) * sr, DN_DV:], qg[p][s * sr:(s + 1) * sr]], axis=0),
                       state[(p, s)]) for p in probs for s in segs}
    join = lambda pieces: pieces[0] if len(pieces) == 1 else jnp.concatenate(pieces, axis=0)
    v_new = {p: uw[p][:, :DN_DV] - join([wq[(p, s)][:sr] for s in segs]) for p in probs}
    o = {p: join([wq[(p, s)][sr:] for s in segs]) + _dot(attn[p], v_new[p]) for p in probs}
    kd = {p: k[p] * jnp.exp(gtot[p] - gsum[p]) for p in probs}
    for p in probs:
        for s in segs:
            r0 = s * sr if n_seg > 1 else 0
            decay = jnp.exp(gtot[p][r0:r0 + 1, :])
            write_state(p[0], s, p[1],
                        state[(p, s)] * decay + _dot_tn(kd[p][s * sr:(s + 1) * sr], v_new[p][s * sr:(s + 1) * sr]))
    outs = []
    for g, (_, z, _) in enumerate(groups):
        heads = []
        for h in range(DN_HEADS):
            oh = o[(g, h)]
            zz = z[:, h * DN_DV:(h + 1) * DN_DV]
            on = oh * lax.rsqrt(jnp.mean(oh * oh, axis=-1, keepdims=True) + EPS) * nw
            heads.append(on * (zz * _sigmoid(zz)))
        outs.append(jnp.concatenate(heads, axis=1))
    return outs


def _dn_prompt_kernel(*refs, chunk, n_batch):
    nb = n_batch
    qkv_refs, z_refs, ba_refs = refs[0:nb], refs[nb:2 * nb], refs[2 * nb:3 * nb]
    alog_ref, dtb_ref, nw_ref, s0_ref, o_ref, sout_ref = refs[3 * nb:]

    @pl.when(pl.program_id(0) == 0)
    def _():
        sout_ref[...] = s0_ref[...]

    groups = [(qkv_refs[b][...], z_refs[b][...], ba_refs[b][...]) for b in range(nb)]

    def read_state(g, s, h):
        return sout_ref[g, h]

    def write_state(g, s, h, val):
        sout_ref[g, h] = val

    outs = _dn_core(groups, alog_ref[...], dtb_ref[...], nw_ref[...], read_state, write_state, 1, chunk)
    for b in range(nb):
        o_ref[b] = outs[b]


def _dn_prompt(proj, ba, alog_row, dtb_row, dn_nw, s0, n_batch, seq):
    chunk = min(DN_CHUNK, seq)
    assert seq % chunk == 0 and chunk % SUBLANES == 0
    nc = seq // chunk
    const2 = lambda c: (0, 0)
    rows = lambda b, col: (lambda c: (b * nc + c, col))
    batches = range(n_batch)
    o, s_out = pl.pallas_call(
        functools.partial(_dn_prompt_kernel, chunk=chunk, n_batch=n_batch),
        grid=(nc,),
        in_specs=(
            [pl.BlockSpec((chunk, DN_CONV_W), rows(b, COL_QKV // DN_CONV_W)) for b in batches]
            + [pl.BlockSpec((chunk, DN_V_W), rows(b, COL_Z // DN_V_W)) for b in batches]
            + [pl.BlockSpec((chunk, LANES), rows(b, 0)) for b in batches]
            + [
                pl.BlockSpec((1, LANES), const2),
                pl.BlockSpec((1, LANES), const2),
                pl.BlockSpec((1, DN_DV), const2),
                pl.BlockSpec((n_batch, DN_HEADS, DN_DK, DN_DV), lambda c: (0, 0, 0, 0)),
            ]
        ),
        out_specs=[
            pl.BlockSpec((n_batch, chunk, DN_V_W), lambda c: (0, c, 0)),
            pl.BlockSpec((n_batch, DN_HEADS, DN_DK, DN_DV), lambda c: (0, 0, 0, 0)),
        ],
        out_shape=[
            jax.ShapeDtypeStruct((n_batch, seq, DN_V_W), F32),
            jax.ShapeDtypeStruct((n_batch, DN_HEADS, DN_DK, DN_DV), F32),
        ],
        compiler_params=_cparams(("arbitrary",)),
        name="dn_prompt",
    )(*([proj] * n_batch), *([proj] * n_batch), *([ba] * n_batch), alog_row, dtb_row, dn_nw, s0)
    return o.reshape(n_batch * seq, DN_V_W), s_out


def _dn_sample_kernel(qkv_ref, z_ref, ba_ref, convw_ref, alog_ref, dtb_ref, nw_ref, conv0_ref, s0_ref,
                      o_ref, convout_ref, sout_ref, cbuf_ref, *, seq, n_bb):
    hist = SUBLANES - (DN_CONV - 1)
    per = SUBLANES // seq
    n_tiles = n_bb // per

    def spread(ref):
        pieces = []
        for j in range(n_tiles):
            x8 = ref[j * SUBLANES:(j + 1) * SUBLANES, :]
            for r in range(per):
                pieces.append(x8 if r == 0 else pltpu.roll(x8, SUBLANES - r * seq, axis=0))
        return pieces

    for bb, piece in enumerate(spread(qkv_ref)):
        cbuf_ref[bb, SUBLANES:2 * SUBLANES, :] = piece
    cbuf_ref[:, hist:SUBLANES, :] = conv0_ref[...]
    w = convw_ref[...]
    acc = cbuf_ref[:, hist:hist + SUBLANES, :] * w[0:1, :]
    for i in range(1, DN_CONV):
        acc = acc + cbuf_ref[:, hist + i:hist + i + SUBLANES, :] * w[i:i + 1, :]
    live = _row_iota(acc.shape) < seq
    qkv = jnp.where(live, acc * _sigmoid(acc), 0.0).reshape(n_bb * SUBLANES, DN_CONV_W)
    convout_ref[...] = cbuf_ref[:, SUBLANES + seq - (DN_CONV - 1):SUBLANES + seq, :]

    def read_state(g, s, h):
        return s0_ref[s, h]

    def write_state(g, s, h, val):
        sout_ref[s, h] = val

    group = (qkv, jnp.concatenate(spread(z_ref), axis=0), jnp.concatenate(spread(ba_ref), axis=0))
    o = _dn_core([group], alog_ref[...], dtb_ref[...], nw_ref[...], read_state, write_state, n_bb, seq)[0]
    rows = _row_iota((SUBLANES, DN_V_W))
    for j in range(n_tiles):
        tile = None
        for r in range(per):
            bb = j * per + r
            piece = o[bb * SUBLANES:(bb + 1) * SUBLANES]
            piece = piece if r == 0 else pltpu.roll(piece, r * seq, axis=0)
            tile = piece if tile is None else jnp.where(rows >= r * seq, piece, tile)
        o_ref[j * SUBLANES:(j + 1) * SUBLANES, :] = tile


def _dn_sample(proj, ba, conv_w, alog_row, dtb_row, dn_nw, conv0, s0, row0, n_batch, seq):
    assert SUBLANES % seq == 0 and seq >= DN_CONV - 1
    n_bb = SUBLANES
    rows_in = n_bb * seq
    assert n_batch % n_bb == 0 and row0 % rows_in == 0
    rb0 = row0 // rows_in
    const1 = lambda i: (0, 0)
    return pl.pallas_call(
        functools.partial(_dn_sample_kernel, seq=seq, n_bb=n_bb),
        grid=(n_batch // n_bb,),
        in_specs=[
            pl.BlockSpec((rows_in, DN_CONV_W), lambda i: (rb0 + i, COL_QKV // DN_CONV_W)),
            pl.BlockSpec((rows_in, DN_V_W), lambda i: (rb0 + i, COL_Z // DN_V_W)),
            pl.BlockSpec((rows_in, LANES), lambda i: (rb0 + i, 0)),
            pl.BlockSpec((DN_CONV, DN_CONV_W), const1),
            pl.BlockSpec((1, LANES), const1),
            pl.BlockSpec((1, LANES), const1),
            pl.BlockSpec((1, DN_DV), const1),
            pl.BlockSpec((n_bb, DN_CONV - 1, DN_CONV_W), lambda i: (i, 0, 0)),
            pl.BlockSpec((n_bb, DN_HEADS, DN_DK, DN_DV), lambda i: (i, 0, 0, 0)),
        ],
        out_specs=[
            pl.BlockSpec((rows_in, DN_V_W), lambda i: (i, 0)),
            pl.BlockSpec((n_bb, DN_CONV - 1, DN_CONV_W), lambda i: (i, 0, 0)),
            pl.BlockSpec((n_bb, DN_HEADS, DN_DK, DN_DV), lambda i: (i, 0, 0, 0)),
        ],
        out_shape=[
            jax.ShapeDtypeStruct((n_batch * seq, DN_V_W), F32),
            jax.ShapeDtypeStruct((n_batch, DN_CONV - 1, DN_CONV_W), F32),
            jax.ShapeDtypeStruct((n_batch, DN_HEADS, DN_DK, DN_DV), F32),
        ],
        scratch_shapes=[pltpu.VMEM((n_bb, 2 * SUBLANES, DN_CONV_W), F32)],
        compiler_params=_cparams(("arbitrary",)),
        name="dn_sample",
    )(proj, proj, ba, conv_w, alog_row, dtb_row, dn_nw, conv0, s0)


def _rel_bucket(dist):
    n = jnp.maximum(dist, 0)
    max_exact = REL_BUCKETS // 2
    large = max_exact + (jnp.log(jnp.maximum(n, 1).astype(F32) / max_exact)
                         / math.log(REL_MAX_DIST / max_exact) * (REL_BUCKETS - max_exact)).astype(I32)
    return jnp.where(n < max_exact, n, jnp.minimum(large, REL_BUCKETS - 1))


def _relbias_kernel(tab_ref, bucket_ref, o_ref):
    h = pl.program_id(0)
    bk = bucket_ref[...]
    acc = jnp.full(bk.shape, -jnp.inf, F32)
    for b in range(REL_BUCKETS):
        acc = jnp.where(bk == b, tab_ref[b * SW_HEADS + h], acc)
    o_ref[0] = acc


def _relbias(rel_table, bucket):
    nq, ns = bucket.shape
    return pl.pallas_call(
        _relbias_kernel,
        grid=(SW_HEADS,),
        in_specs=[
            pl.BlockSpec(memory_space=pltpu.SMEM),
            pl.BlockSpec((nq, ns), lambda h: (0, 0)),
        ],
        out_specs=pl.BlockSpec((1, nq, ns), lambda h: (h, 0, 0)),
        out_shape=jax.ShapeDtypeStruct((SW_HEADS, nq, ns), F32),
        compiler_params=_cparams(("arbitrary",)),
        name="relbias",
    )(rel_table.reshape(-1), bucket)


def _dup_halves(x):
    lo = _lane_iota(x.shape) < SW_HD
    xr = pltpu.roll(x, SW_HD, axis=1)
    return jnp.where(lo, x, xr).astype(BF16), jnp.where(lo, xr, x).astype(BF16)


def _sink_softmax_pv(s, sink, vv):
    m = jnp.maximum(jnp.max(s, axis=-1, keepdims=True), sink)
    p = jnp.exp(s - m)
    denom = jnp.sum(p, axis=-1, keepdims=True) + jnp.exp(sink - m)
    return _dot(p, vv) * (1.0 / denom)


def _swa_prompt_kernel(sinks_ref, q_ref, kc_ref, kp_ref, vc_ref, vp_ref, bias_ref, o_ref, klast_ref, vlast_ref):
    klast_ref[0] = kc_ref[...]
    vlast_ref[0] = vc_ref[...]
    kk = _dup_halves(jnp.concatenate([kp_ref[...], kc_ref[...]], axis=0))
    vv = _dup_halves(jnp.concatenate([vp_ref[...], vc_ref[...]], axis=0))
    lo = _lane_iota((SW_BLOCK, LANES)) < SW_HD
    for pair in range(SW_HEADS // 2):
        qp = q_ref[:, pair * LANES:(pair + 1) * LANES] * (SW_HD ** -0.5)
        outs = []
        for half in range(2):
            hq = 2 * pair + half
            kv = hq // SW_GROUP
            qm = jnp.where(lo if half == 0 else ~lo, qp, 0.0)
            s = _dot_nt(qm, kk[kv]) + bias_ref[0, hq]
            outs.append(_sink_softmax_pv(s, sinks_ref[hq], vv[kv]))
        o_ref[:, pair * LANES:(pair + 1) * LANES] = jnp.where(lo, outs[0], outs[1])


def _swa_prompt(proj, sinks, bias, n_batch, seq):
    assert seq % SW_BLOCK == 0 and WINDOW == SW_BLOCK
    nb = seq // SW_BLOCK
    cur = lambda col: (lambda b, i: (b * nb + i, col))
    prev = lambda col: (lambda b, i: (b * nb + jnp.maximum(i - 1, 0), col))
    return pl.pallas_call(
        _swa_prompt_kernel,
        grid=(n_batch, nb),
        in_specs=[
            pl.BlockSpec(memory_space=pltpu.SMEM),
            pl.BlockSpec((SW_BLOCK, SW_HEADS * SW_HD), cur(COL_SQ // (SW_HEADS * SW_HD))),
            pl.BlockSpec((SW_BLOCK, SW_KV_W), cur(COL_SK // SW_KV_W)),
            pl.BlockSpec((SW_BLOCK, SW_KV_W), prev(COL_SK // SW_KV_W)),
            pl.BlockSpec((SW_BLOCK, SW_KV_W), cur(COL_SV // SW_KV_W)),
            pl.BlockSpec((SW_BLOCK, SW_KV_W), prev(COL_SV // SW_KV_W)),
            pl.BlockSpec((1, SW_HEADS, SW_BLOCK, 2 * SW_BLOCK), lambda b, i: (jnp.minimum(i, 1), 0, 0, 0)),
        ],
        out_specs=[
            pl.BlockSpec((SW_BLOCK, SW_HEADS * SW_HD), lambda b, i: (b * nb + i, 0)),
            pl.BlockSpec((1, SW_BLOCK, SW_KV_W), lambda b, i: (b, 0, 0)),
            pl.BlockSpec((1, SW_BLOCK, SW_KV_W), lambda b, i: (b, 0, 0)),
        ],
        out_shape=[
            jax.ShapeDtypeStruct((n_batch * seq, SW_HEADS * SW_HD), F32),
            jax.ShapeDtypeStruct((n_batch, SW_BLOCK, SW_KV_W), F32),
            jax.ShapeDtypeStruct((n_batch, SW_BLOCK, SW_KV_W), F32),
        ],
        compiler_params=_cparams(("arbitrary", "arbitrary")),
        name="swa_prompt",
    )(sinks, proj, proj, proj, proj, proj, bias)


def _swa_sample_kernel(q_ref, kn_ref, vn_ref, kc_ref, vc_ref, bias_ref, sink_ref,
                       o_ref, ko_ref, vo_ref, kall_ref, vall_ref, *, seq, n_bb, n_cache):
    n_keys = kall_ref.shape[0]
    zeros_tail = jnp.zeros((n_keys - n_cache - SUBLANES, LANES), F32)
    lo = _lane_iota((SUBLANES, LANES)) < SW_HD
    out = None
    for bb in range(n_bb):
        shift = (SUBLANES - bb * seq) % SUBLANES

        def top(x, shift=shift):
            return x if shift == 0 else pltpu.roll(x, shift, axis=0)

        kall_ref[0:n_cache, :] = kc_ref[bb]
        kall_ref[n_cache:n_cache + SUBLANES, :] = top(kn_ref[...])
        kall_ref[n_cache + SUBLANES:, :] = zeros_tail
        vall_ref[0:n_cache, :] = vc_ref[bb]
        vall_ref[n_cache:n_cache + SUBLANES, :] = top(vn_ref[...])
        vall_ref[n_cache + SUBLANES:, :] = zeros_tail
        ko_ref[bb] = kall_ref[seq:seq + n_cache, :]
        vo_ref[bb] = vall_ref[seq:seq + n_cache, :]
        kk = _dup_halves(kall_ref[...])
        vv = _dup_halves(vall_ref[...])
        q8 = top(q_ref[...])
        pairs = []
        for kv in range(SW_KV_HEADS):
            pieces = []
            for g in range(SW_GROUP):
                hq = kv * SW_GROUP + g
                qp = q8[:, (hq // 2) * LANES:(hq // 2 + 1) * LANES]
                pieces.append(jnp.where(lo if hq % 2 == 0 else ~lo, qp, 0.0))
            qs = jnp.concatenate(pieces, axis=0) * (SW_HD ** -0.5)
            s = _dot_nt(qs, kk[kv]) + bias_ref[kv]
            res = _sink_softmax_pv(s, _col(sink_ref[kv], 0), vv[kv])
            for g in range(0, SW_GROUP, 2):
                pairs.append(jnp.where(lo, res[g * SUBLANES:(g + 1) * SUBLANES],
                                       res[(g + 1) * SUBLANES:(g + 2) * SUBLANES]))
        o = jnp.concatenate(pairs, axis=1)
        back = (bb * seq) % SUBLANES
        o = o if back == 0 else pltpu.roll(o, back, axis=0)
        rows = _row_iota(o.shape)
        sel = (rows >= bb * seq) & (rows < (bb + 1) * seq)
        out = jnp.where(sel, o, 0.0 if out is None else out)
    o_ref[...] = out


def _swa_sample(proj, k_cache, v_cache, bias, sink_rows, row0, n_batch, seq):
    assert SUBLANES % seq == 0
    n_bb = SUBLANES // seq
    n_cache = k_cache.shape[1]
    assert n_batch % n_bb == 0 and row0 % SUBLANES == 0 and n_cache % SUBLANES == 0
    n_keys = bias.shape[-1]
    rb0 = row0 // SUBLANES
    blk = lambda col: (lambda i: (rb0 + i, col))
    return pl.pallas_call(
        functools.partial(_swa_sample_kernel, seq=seq, n_bb=n_bb, n_cache=n_cache),
        grid=(n_batch // n_bb,),
        in_specs=[
            pl.BlockSpec((SUBLANES, SW_HEADS * SW_HD), blk(COL_SQ // (SW_HEADS * SW_HD))),
            pl.BlockSpec((SUBLANES, SW_KV_W), blk(COL_SK // SW_KV_W)),
            pl.BlockSpec((SUBLANES, SW_KV_W), blk(COL_SV // SW_KV_W)),
            pl.BlockSpec((n_bb, n_cache, SW_KV_W), lambda i: (i, 0, 0)),
            pl.BlockSpec((n_bb, n_cache, SW_KV_W), lambda i: (i, 0, 0)),
            pl.BlockSpec((SW_KV_HEADS, SW_GROUP * SUBLANES, n_keys), lambda i: (0, 0, 0)),
            pl.BlockSpec((SW_KV_HEADS, SW_GROUP * SUBLANES, n_keys), lambda i: (0, 0, 0)),
        ],
        out_specs=[
            pl.BlockSpec((SUBLANES, SW_HEADS * SW_HD), lambda i: (i, 0)),
            pl.BlockSpec((n_bb, n_cache, SW_KV_W), lambda i: (i, 0, 0)),
            pl.BlockSpec((n_bb, n_cache, SW_KV_W), lambda i: (i, 0, 0)),
        ],
        out_shape=[
            jax.ShapeDtypeStruct((n_batch * seq, SW_HEADS * SW_HD), F32),
            jax.ShapeDtypeStruct(k_cache.shape, F32),
            jax.ShapeDtypeStruct(v_cache.shape, F32),
        ],
        scratch_shapes=[pltpu.VMEM((n_keys, SW_KV_W), F32), pltpu.VMEM((n_keys, SW_KV_W), F32)],
        compiler_params=_cparams(("arbitrary",)),
        name="swa_sample",
    )(proj, proj, proj, k_cache, v_cache, bias, sink_rows)


def _mix_kernel(xp_ref, xs_ref, oap_ref, oas_ref, obp_ref, obs_ref, ga_ref, gb_ref, wo_ref, nw_ref, wr_ref, br_ref,
                x1_ref, h2_ref, route_ref, *, n_prompt_tiles):
    i = pl.program_id(0)

    def run(x_ref, oa_ref, ob_ref):
        mixed = _sigmoid(ga_ref[...]) * oa_ref[...] + _sigmoid(gb_ref[...]) * ob_ref[...]
        x1 = x_ref[...] + _dot(mixed, wo_ref[...])
        x1_ref[...] = x1
        h2 = x1 * lax.rsqrt(jnp.mean(x1 * x1, axis=-1, keepdims=True) + EPS) * nw_ref[...]
        h2_ref[...] = h2
        logits = _dot(h2, wr_ref[...]) + br_ref[...]
        lane = _lane_iota(logits.shape)
        lanef = lane.astype(F32)
        big = float(2 * LANES)
        is_g = lane < N_GROUPS
        gl = jnp.where(is_g, logits, -jnp.inf)
        gmax = jnp.max(gl, axis=-1, keepdims=True)
        gval = 1.0 / jnp.sum(jnp.where(is_g, jnp.exp(gl - gmax), 0.0), axis=-1, keepdims=True)
        grp = jnp.min(jnp.where(gl == gmax, lanef, big), axis=-1, keepdims=True)
        e_grp = ((lane - N_GROUPS) >> 3).astype(F32)
        is_e = (lane >= N_GROUPS) & (lane < N_GROUPS + N_EXPERTS) & (e_grp == grp)
        el = jnp.where(is_e, logits, -jnp.inf)
        v1 = jnp.max(el, axis=-1, keepdims=True)
        i1 = jnp.min(jnp.where(el == v1, lanef, big), axis=-1, keepdims=True)
        el2 = jnp.where(lanef == i1, -jnp.inf, el)
        v2 = jnp.max(el2, axis=-1, keepdims=True)
        i2 = jnp.min(jnp.where(el2 == v2, lanef, big), axis=-1, keepdims=True)
        e2 = jnp.exp(v2 - v1)
        w1 = gval / (1.0 + e2)
        w2 = gval * e2 / (1.0 + e2)
        route_ref[...] = jnp.where(lane == 0, i1 - N_GROUPS,
                                   jnp.where(lane == 1, i2 - N_GROUPS,
                                             jnp.where(lane == 2, w1, jnp.where(lane == 3, w2, 0.0))))

    @pl.when(i < n_prompt_tiles)
    def _():
        run(xp_ref, oap_ref, obp_ref)

    @pl.when(i >= n_prompt_tiles)
    def _():
        run(xs_ref, oas_ref, obs_ref)


def _mix(xp, xs, oa_p, oa_s, ob_p, ob_s, proj, w_out, norm_w, w_router, b_router):
    tp, ts = xp.shape[0], xs.shape[0]
    tm = _token_tile(tp, ts)
    npt, nst = tp // tm, ts // tm
    const = lambda i: (0, 0)
    row = lambda i: (i, 0)
    return pl.pallas_call(
        functools.partial(_mix_kernel, n_prompt_tiles=npt),
        grid=(npt + nst,),
        in_specs=[
            pl.BlockSpec((tm, D_MODEL), lambda i: (jnp.minimum(i, npt - 1), 0)),
            pl.BlockSpec((tm, D_MODEL), lambda i: (jnp.maximum(i - npt, 0), 0)),
            pl.BlockSpec((tm, D_MODEL), lambda i: (jnp.minimum(i, npt - 1), 0)),
            pl.BlockSpec((tm, D_MODEL), lambda i: (jnp.maximum(i - npt, 0), 0)),
            pl.BlockSpec((tm, D_MODEL), lambda i: (jnp.minimum(i, npt - 1), 0)),
            pl.BlockSpec((tm, D_MODEL), lambda i: (jnp.maximum(i - npt, 0), 0)),
            pl.BlockSpec((tm, D_MODEL), lambda i: (i, COL_GA // D_MODEL)),
            pl.BlockSpec((tm, D_MODEL), lambda i: (i, COL_GB // D_MODEL)),
            pl.BlockSpec((D_MODEL, D_MODEL), const),
            pl.BlockSpec((1, D_MODEL), const),
            pl.BlockSpec((D_MODEL, LANES), const),
            pl.BlockSpec((1, LANES), const),
        ],
        out_specs=[
            pl.BlockSpec((tm, D_MODEL), row),
            pl.BlockSpec((tm, D_MODEL), row),
            pl.BlockSpec((tm, LANES), row),
        ],
        out_shape=[
            jax.ShapeDtypeStruct((tp + ts, D_MODEL), F32),
            jax.ShapeDtypeStruct((tp + ts, D_MODEL), F32),
            jax.ShapeDtypeStruct((tp + ts, LANES), F32),
        ],
        compiler_params=_cparams(("arbitrary",)),
        name="mix_router",
    )(xp, xs, oa_p, oa_s, ob_p, ob_s, proj, proj, w_out, norm_w, w_router, b_router)


def _rank_kernel(route_ref, dest_ref, meta_ref, rank_ref, cnt_ref, *, tile, blk):
    phase = pl.program_id(0)
    i = pl.program_id(1)
    shape = (tile, LANES)
    lane = _lane_iota(shape)
    lanef = lane.astype(F32)
    r = route_ref[...]
    oh0 = lanef == _col(r, 0)
    oh1 = lanef == _col(r, 1)
    rows = pl.ds(pl.multiple_of(i * tile, tile), tile)

    @pl.when(phase == 0)
    def _():
        @pl.when(i == 0)
        def _():
            cnt_ref[...] = jnp.zeros(cnt_ref.shape, F32)

        oh = jnp.where(oh0 | oh1, 1.0, 0.0)
        tri = jnp.where(_row_iota((tile, tile)) > _lane_iota((tile, tile)), 1.0, 0.0)
        before = _dot(tri, oh) + cnt_ref[0:1, :]
        rank0 = jnp.sum(jnp.where(oh0, before, 0.0), axis=-1, keepdims=True)
        rank1 = jnp.sum(jnp.where(oh1, before, 0.0), axis=-1, keepdims=True)
        rank_ref[rows, :] = jnp.where(lane == 0, rank0, jnp.where(lane == 1, rank1, 0.0))
        cnt_ref[0:1, :] = cnt_ref[0:1, :] + jnp.sum(oh, axis=0, keepdims=True)

    @pl.when(phase == 1)
    def _():
        cnt = cnt_ref[0:1, :]
        padded = jnp.floor((cnt + (blk - 1)) / blk) * blk
        before_lane = jnp.where(_row_iota((LANES, LANES)) < _lane_iota((LANES, LANES)), 1.0, 0.0)
        start = _dot_exact(jnp.broadcast_to(padded, (SUBLANES, LANES)), before_lane)[0:1, :]
        rk = rank_ref[rows, :]
        d0 = jnp.sum(jnp.where(oh0, start, 0.0), axis=-1, keepdims=True) + _col(rk, 0)
        d1 = jnp.sum(jnp.where(oh1, start, 0.0), axis=-1, keepdims=True) + _col(rk, 1)
        dest_ref[...] = jnp.where(lane == 0, d0, jnp.where(lane == 1, d1, 0.0)).astype(I32)

        @pl.when(i == 0)
        def _():
            end = start + padded
            mshape = meta_ref.shape
            blk_start = (_row_iota(mshape) * blk).astype(F32)
            hit = (_lane_iota(mshape) < N_EXPERTS) & (end <= blk_start)
            be = jnp.minimum(jnp.sum(jnp.where(hit, 1.0, 0.0), axis=-1, keepdims=True), N_EXPERTS - 1.0)
            n_used = _col(end, N_EXPERTS - 1) / blk
            ml = _lane_iota(mshape)
            mine = ml.astype(F32) == be
            seg_start = jnp.sum(jnp.where(mine, start, 0.0), axis=-1, keepdims=True)
            seg_count = jnp.sum(jnp.where(mine, cnt, 0.0), axis=-1, keepdims=True)
            n_valid = jnp.clip(seg_count - (blk_start[:, 0:1] - seg_start), 0.0, float(blk))
            meta_ref[...] = jnp.where(ml == 0, be, jnp.where(ml == 1, n_used,
                                                              jnp.where(ml == 2, n_valid, 0.0))).astype(I32)


def _rank(route, tile, blk, n_blocks):
    t = route.shape[0]
    nt = t // tile
    nbp = -(-n_blocks // SUBLANES) * SUBLANES
    return pl.pallas_call(
        functools.partial(_rank_kernel, tile=tile, blk=blk),
        grid=(2, nt),
        in_specs=[pl.BlockSpec((tile, LANES), lambda p, i: (i, 0))],
        out_specs=[
            pl.BlockSpec((tile, LANES), lambda p, i: (i * p, 0)),
            pl.BlockSpec((nbp, LANES), lambda p, i: (0, 0)),
        ],
        out_shape=[
            jax.ShapeDtypeStruct((t, LANES), I32),
            jax.ShapeDtypeStruct((nbp, LANES), I32),
        ],
        scratch_shapes=[pltpu.VMEM((t, LANES), F32), pltpu.VMEM((SUBLANES, LANES), F32)],
        compiler_params=_cparams(("arbitrary", "arbitrary")),
        name="moe_rank",
    )(route)


TOK_TILE = (D_MODEL // LANES, LANES)


def _to_tiles(ref, x):
    for j in range(TOK_TILE[0]):
        ref[:, j, :] = x[:, j * LANES:(j + 1) * LANES]


def _from_tiles(ref):
    return jnp.concatenate([ref[:, j, :] for j in range(TOK_TILE[0])], axis=1)


def _row_copy(src, src_row, dst, dst_row, sem):
    if len(src.shape) == 2:
        return pltpu.make_async_copy(src.at[pl.ds(src_row, 1)], dst.at[pl.ds(dst_row, 1)], sem)
    return pltpu.make_async_copy(src.at[src_row], dst.at[dst_row], sem)


def _last_used(i, nu_ref):
    return jnp.minimum(i, jnp.maximum(nu_ref[0] - 1, 0))


def _for_rows(n, body):
    def group(g, carry):
        for u in range(SUBLANES):
            body(g * SUBLANES + u, u)
        return carry

    def single(t, carry):
        body(t, 0)
        return carry

    n_groups = n // SUBLANES
    lax.fori_loop(0, n_groups, group, 0)
    lax.fori_loop(n_groups * SUBLANES, n, single, 0)


def _expert_kernel(be_ref, nu_ref, nv_ref, dest_ref, h2_ref, wg_ref, wu_ref, wd_ref, y_ref,
                   xbuf_ref, wgu_ref, wdn_ref, inv_ref, sem, *, blk):
    i = pl.program_id(0)
    n_used = nu_ref[0]
    used = i < n_used
    slot = i % 2
    blk_i = _last_used(i, nu_ref)
    fresh = (i == 0) | (be_ref[blk_i] != be_ref[jnp.maximum(blk_i - 1, 0)])

    def gather(block, into):
        _for_rows(nv_ref[block], lambda t, u: _row_copy(h2_ref, inv_ref[block * blk + t], xbuf_ref.at[into], t,
                                                       sem.at[into]).start(priority=u % 2))

    @pl.when(i == 0)
    def _():
        def place(a, carry):
            inv_ref[dest_ref[a]] = a >> 1
            return carry

        lax.fori_loop(0, dest_ref.shape[0], place, 0, unroll=16)
        xbuf_ref[...] = jnp.zeros(xbuf_ref.shape, F32)
        gather(0, 0)

    @pl.when(i + 1 < n_used)
    def _():
        gather(i + 1, 1 - slot)

    @pl.when(used & fresh)
    def _():
        wgu_ref[:, :D_EXPERT] = wg_ref[0].astype(BF16)
        wgu_ref[:, D_EXPERT:] = wu_ref[0].astype(BF16)
        wdn_ref[...] = wd_ref[0].astype(BF16)

    @pl.when(used)
    def _():
        _for_rows(nv_ref[i], lambda t, u: _row_copy(h2_ref, 0, xbuf_ref.at[slot], 0, sem.at[slot]).wait())
        gu = jnp.dot(xbuf_ref[slot].astype(BF16), wgu_ref[...], preferred_element_type=F32)
        g = gu[:, :D_EXPERT]
        hidden = (g * _sigmoid(g) * gu[:, D_EXPERT:]).astype(BF16)
        _to_tiles(y_ref, jnp.dot(hidden, wdn_ref[...], preferred_element_type=F32))

    @pl.when(jnp.logical_not(used))
    def _():
        y_ref[...] = jnp.zeros(y_ref.shape, F32)


def _experts(block_expert, n_used, n_valid, dest_flat, h2, w_gate, w_up, w_down, blk):
    n_blocks = block_expert.shape[0]
    wsel = lambda i, be, nu, nv, de: (be[_last_used(i, nu)], 0, 0)
    return pl.pallas_call(
        functools.partial(_expert_kernel, blk=blk),
        grid_spec=pltpu.PrefetchScalarGridSpec(
            num_scalar_prefetch=4,
            grid=(n_blocks,),
            in_specs=[
                pl.BlockSpec(memory_space=pl.ANY),
                pl.BlockSpec((1, D_MODEL, D_EXPERT), wsel),
                pl.BlockSpec((1, D_MODEL, D_EXPERT), wsel),
                pl.BlockSpec((1, D_EXPERT, D_MODEL), wsel),
            ],
            out_specs=pl.BlockSpec((blk,) + TOK_TILE, lambda i, be, nu, nv, de: (i, 0, 0)),
            scratch_shapes=[
                pltpu.VMEM((2, blk, D_MODEL), F32),
                pltpu.VMEM((D_MODEL, 2 * D_EXPERT), BF16),
                pltpu.VMEM((D_EXPERT, D_MODEL), BF16),
                pltpu.SMEM((n_blocks * blk,), I32),
                pltpu.SemaphoreType.DMA((2,)),
            ],
        ),
        out_shape=jax.ShapeDtypeStruct((n_blocks * blk,) + TOK_TILE, F32),
        compiler_params=_cparams(("arbitrary",)),
        name="moe_experts",
    )(block_expert, n_used, n_valid, dest_flat, h2, w_gate, w_up, w_down)


def _combine_kernel(dest_ref, x1_ref, route_ref, nw_ref, ys_ref, yp_ref, ysm_ref, ybuf_ref, sem,
                    *, tile, n_prompt_tiles):
    i = pl.program_id(0)
    slot = i % 2

    def gather(step, into):
        def issue(t, carry):
            for k in range(2):
                _row_copy(ys_ref, dest_ref[2 * (step * tile + t) + k], ybuf_ref.at[into], k * tile + t,
                          sem.at[into]).start(priority=k)
            return carry

        lax.fori_loop(0, tile, issue, 0, unroll=4)

    @pl.when(i == 0)
    def _():
        gather(0, 0)

    @pl.when(i + 1 < pl.num_programs(0))
    def _():
        gather(i + 1, 1 - slot)

    def drain(t, carry):
        _row_copy(ys_ref, 0, ybuf_ref.at[slot], 0, sem.at[slot]).wait()
        return carry

    lax.fori_loop(0, 2 * tile, drain, 0, unroll=8)
    r = route_ref[...]
    ybuf = ybuf_ref.at[slot]
    y = (_from_tiles(ybuf.at[pl.ds(0, tile)]) * _col(r, 2)
         + _from_tiles(ybuf.at[pl.ds(tile, tile)]) * _col(r, 3))
    x2 = x1_ref[...] + y
    out = x2 * lax.rsqrt(jnp.mean(x2 * x2, axis=-1, keepdims=True) + EPS) * nw_ref[...]

    @pl.when(i < n_prompt_tiles)
    def _():
        yp_ref[...] = out

    @pl.when(i >= n_prompt_tiles)
    def _():
        ysm_ref[...] = out


def _combine(dest_flat, x1, route, norm_w, ys, tp, ts):
    tile = _token_tile(tp, ts)
    npt, nst = tp // tile, ts // tile
    return pl.pallas_call(
        functools.partial(_combine_kernel, tile=tile, n_prompt_tiles=npt),
        grid_spec=pltpu.PrefetchScalarGridSpec(
            num_scalar_prefetch=1,
            grid=(npt + nst,),
            in_specs=[
                pl.BlockSpec((tile, D_MODEL), lambda i, d: (i, 0)),
                pl.BlockSpec((tile, LANES), lambda i, d: (i, 0)),
                pl.BlockSpec((1, D_MODEL), lambda i, d: (0, 0)),
                pl.BlockSpec(memory_space=pl.ANY),
            ],
            out_specs=[
                pl.BlockSpec((tile, D_MODEL), lambda i, d: (jnp.minimum(i, npt - 1), 0)),
                pl.BlockSpec((tile, D_MODEL), lambda i, d: (jnp.maximum(i - npt, 0), 0)),
            ],
            scratch_shapes=[pltpu.VMEM((2, 2 * tile) + TOK_TILE, F32), pltpu.SemaphoreType.DMA((2,))],
        ),
        out_shape=[
            jax.ShapeDtypeStruct((tp, D_MODEL), F32),
            jax.ShapeDtypeStruct((ts, D_MODEL), F32),
        ],
        compiler_params=_cparams(("arbitrary",)),
        name="moe_combine",
    )(dest_flat, x1, route, norm_w, ys)


def _layer(xp, xs, n_batch, seq, s_batch, s_seq, conv_state, dn_state, k_cache, v_cache,
           w_in, conv_w, a_log, dt_bias, dn_norm_w, sinks, rel_bias, w_out, norm_mix_w, norm_ffn_w,
           w_rg, b_rg, w_re, b_re, w_gate, w_up, w_down, norm_final_w):
    tp, ts = xp.shape[0], xs.shape[0]
    t_all = tp + ts
    row = lambda v: v.reshape(1, -1).astype(F32)

    o = np.cumsum((0, DN_QK_W, DN_QK_W, DN_V_W, DN_V_W, DN_HEADS, DN_HEADS, SW_HEADS * SW_HD, SW_KV_W, SW_KV_W,
                   D_MODEL, D_MODEL)).tolist()
    w_big = jnp.concatenate([w_in[:, o[0]:o[4]], w_in[:, o[6]:o[7]], w_in[:, o[9]:o[11]], w_in[:, o[7]:o[9]]],
                            axis=1).astype(BF16)
    w_small = jnp.pad(w_in[:, o[4]:o[6]], ((0, 0), (0, LANES - 2 * DN_HEADS))).astype(BF16)
    head_row = lambda v: jnp.pad(v.astype(F32), (DN_HEADS, LANES - 2 * DN_HEADS)).reshape(1, LANES)
    w_router = jnp.pad(jnp.concatenate([w_rg, w_re], axis=1),
                       ((0, 0), (0, LANES - N_GROUPS - N_EXPERTS))).astype(BF16)
    b_router = jnp.pad(jnp.concatenate([b_rg, b_re]).astype(F32), (0, LANES - N_GROUPS - N_EXPERTS)).reshape(1, LANES)

    conv0 = jnp.zeros((n_batch, DN_CONV - 1, DN_CONV_W), F32)
    proj, ba, conv_tail = _inproj(xp, xs, row(norm_mix_w), w_big, w_small, conv_w.astype(F32), conv0, seq)
    p_conv = conv_tail[:, SUBLANES - (DN_CONV - 1):, :]

    dn0 = jnp.zeros((n_batch, DN_HEADS, DN_DK, DN_DV), F32)
    oa_p, p_dn = _dn_prompt(proj, ba, head_row(a_log), head_row(dt_bias), row(dn_norm_w), dn0, n_batch, seq)
    oa_s, s_conv, s_dn = _dn_sample(proj, ba, conv_w.astype(F32), head_row(a_log), head_row(dt_bias), row(dn_norm_w),
                                    conv_state, dn_state, tp, s_batch, s_seq)

    qpos = jnp.arange(SW_BLOCK)[:, None]
    kpos = jnp.arange(2 * SW_BLOCK)[None, :] - SW_BLOCK
    in_window = lambda dist: (dist >= 0) & (dist < WINDOW)
    masked_bucket = lambda dist, ok: jnp.where(ok, _rel_bucket(dist), -1)
    bias_p = jnp.stack([
        _relbias(rel_bias.astype(F32), masked_bucket(qpos - kpos, in_window(qpos - kpos) & (kpos >= 0))),
        _relbias(rel_bias.astype(F32), masked_bucket(qpos - kpos, in_window(qpos - kpos)))])
    ob_p, p_k, p_v = _swa_prompt(proj, sinks.astype(F32), bias_p, n_batch, seq)
    n_cache = k_cache.shape[1]
    n_keys = -(-(n_cache + SUBLANES) // LANES) * LANES
    tq = n_cache + jnp.arange(SUBLANES)[:, None]
    dist_s = tq - jnp.arange(n_keys)[None, :]
    bias_s = _relbias(rel_bias.astype(F32), masked_bucket(dist_s, in_window(dist_s)))
    bias_s = bias_s.reshape(SW_KV_HEADS, SW_GROUP * SUBLANES, n_keys)
    sink_rows = jnp.broadcast_to(jnp.repeat(sinks.astype(F32).reshape(SW_KV_HEADS, SW_GROUP), SUBLANES, axis=1)[:, :, None],
                                 (SW_KV_HEADS, SW_GROUP * SUBLANES, n_keys))
    ob_s, s_k, s_v = _swa_sample(proj, k_cache.reshape(s_batch, n_cache, SW_KV_W), v_cache.reshape(s_batch, n_cache, SW_KV_W),
                                 bias_s, sink_rows, tp, s_batch, s_seq)

    x1, h2, route = _mix(xp, xs, oa_p, oa_s, ob_p, ob_s, proj, w_out.astype(BF16), row(norm_ffn_w), w_router, b_router)

    tile = _token_tile(tp, ts)
    n_blocks = -(-2 * t_all // MOE_BLOCK) + N_EXPERTS
    dest, meta = _rank(route, _token_tile(t_all, cands=(512, 256, 128, 64, 32, 16, 8)), MOE_BLOCK, n_blocks)
    dest_flat = dest[:, :2].reshape(-1)
    block_expert = meta[:n_blocks, 0]
    n_used = meta[0:1, 1]
    n_valid = meta[:n_blocks, 2]
    ys = _experts(block_expert, n_used, n_valid, dest_flat, h2, w_gate, w_up, w_down, MOE_BLOCK)
    y_p, y_s = _combine(dest_flat, x1, route, row(norm_final_w), ys, tp, ts)

    kv_shape = (n_batch, WINDOW, SW_KV_HEADS, SW_HD)
    return (y_p, y_s, p_conv, p_dn, p_k.reshape(kv_shape), p_v.reshape(kv_shape), s_conv, s_dn,
            s_k.reshape(k_cache.shape), s_v.reshape(v_cache.shape))


def kernel(x_prompt, x_sample, state_dn_conv, state_dn, cache_swa_k, cache_swa_v, w_in, conv_w, a_log, dt_bias, dn_norm_w, sinks, rel_bias, w_out, norm_mix_w, norm_ffn_w, w_router_group, b_router_group, w_router_expert, b_router_expert, w_gate, w_up, w_down, norm_final_w):
    depth = w_in.shape[0]
    assert depth == 1, "the final-norm fusion below assumes a single layer"
    n_batch, seq, _ = x_prompt.shape
    s_batch, s_seq, _ = x_sample.shape
    outs = _layer(x_prompt.reshape(-1, D_MODEL), x_sample.reshape(-1, D_MODEL), n_batch, seq, s_batch, s_seq,
                  state_dn_conv[0], state_dn[0], cache_swa_k[0], cache_swa_v[0],
                  w_in[0], conv_w[0], a_log[0], dt_bias[0], dn_norm_w[0], sinks[0], rel_bias,
                  w_out[0], norm_mix_w[0], norm_ffn_w[0], w_router_group[0], b_router_group[0],
                  w_router_expert[0], b_router_expert[0], w_gate[0], w_up[0], w_down[0], norm_final_w)
    y_p, y_s, p_conv, p_dn, p_k, p_v, s_conv, s_dn, s_k, s_v = outs
    return (y_p.reshape(x_prompt.shape), y_s.reshape(x_sample.shape), p_conv[None], p_dn[None], p_k[None], p_v[None],
            s_conv[None], s_dn[None], s_k[None], s_v[None])
```

```python
import functools
import math

import jax
import jax.numpy as jnp
import numpy as np
from jax import lax
from jax.experimental import pallas as pl
from jax.experimental.pallas import tpu as pltpu

F32 = jnp.float32
BF16 = jnp.bfloat16
I32 = jnp.int32

D_MODEL = 1024
DN_HEADS = 8
DN_DK = 128
DN_DV = 128
DN_CONV = 4
DN_CHUNK = 64
DN_QK_W = DN_HEADS * DN_DK
DN_V_W = DN_HEADS * DN_DV
DN_CONV_W = 2 * DN_QK_W + DN_V_W
SW_HEADS = 16
SW_KV_HEADS = 2
SW_GROUP = SW_HEADS // SW_KV_HEADS
SW_HD = 64
SW_KV_W = SW_KV_HEADS * SW_HD
WINDOW = 128
SW_BLOCK = 128
REL_BUCKETS = 32
REL_MAX_DIST = 128
N_GROUPS = 8
EXP_PER_GROUP = 8
N_EXPERTS = N_GROUPS * EXP_PER_GROUP
D_EXPERT = 256
MOE_BLOCK = 256
EPS = 1e-6

LANES = 128
SUBLANES = 8
VMEM_LIMIT = 56 * 1024 * 1024

COL_QKV = 0
COL_Z = 3072
COL_SQ = 4096
COL_GA = 5120
COL_GB = 6144
COL_SK = 7168
COL_SV = 7296
PROJ_W = 7424
PROJ_CHUNK = 512


def _cparams(sem):
    return pltpu.CompilerParams(dimension_semantics=sem, vmem_limit_bytes=VMEM_LIMIT)


def _sigmoid(x):
    return 0.5 * jnp.tanh(0.5 * x) + 0.5


def _dot(a, b):
    return jnp.dot(a.astype(BF16), b.astype(BF16), preferred_element_type=F32)


def _dot_nt(a, b):
    return lax.dot_general(a.astype(BF16), b.astype(BF16), (((1,), (1,)), ((), ())), preferred_element_type=F32)


def _dot_tn(a, b):
    return lax.dot_general(a.astype(BF16), b.astype(BF16), (((0,), (0,)), ((), ())), preferred_element_type=F32)


def _dot_exact(a, b):
    return jnp.dot(a, b, precision=lax.Precision.HIGHEST, preferred_element_type=F32)


def _lane_iota(shape):
    return lax.broadcasted_iota(I32, shape, len(shape) - 1)


def _row_iota(shape):
    return lax.broadcasted_iota(I32, shape, len(shape) - 2)


def _col(x, j):
    return jnp.sum(jnp.where(_lane_iota(x.shape) == j, x, 0.0), axis=-1, keepdims=True)


def _token_tile(*sizes, cands=(256, 128, 64, 32, 16, 8)):
    for t in cands:
        if all(s % t == 0 for s in sizes):
            return t
    raise ValueError(f"token counts {sizes} need a common tile that is a multiple of 8")


def _inproj_kernel(xp_ref, xs_ref, nw_ref, wb_ref, ws_ref, convw_ref, conv0_ref, proj_ref, ba_ref, tail_ref, cbuf_ref,
                   *, n_prompt_tiles, tiles_per_seq):
    i = pl.program_id(0)
    tm = xp_ref.shape[0]
    hist = SUBLANES - (DN_CONV - 1)

    def project(x_ref, conv):
        x = x_ref[...]
        h = (x * lax.rsqrt(jnp.mean(x * x, axis=-1, keepdims=True) + EPS) * nw_ref[...]).astype(BF16)
        ba_ref[...] = jnp.dot(h, ws_ref[...], preferred_element_type=F32)
        top = _row_iota((SUBLANES, PROJ_CHUNK))
        starts = list(range(0, PROJ_W, PROJ_CHUNK))
        if conv:
            with_conv = [c for c in starts if c + PROJ_CHUNK <= COL_QKV + DN_CONV_W]
            plain = [c for c in starts if c not in with_conv]
            starts = [c for pair in zip(with_conv, plain) for c in pair] + plain[len(with_conv):]
        for c0 in starts:
            c1 = min(c0 + PROJ_CHUNK, PROJ_W)
            cur = jnp.dot(h, wb_ref[:, c0:c1], preferred_element_type=F32)
            if conv and c1 <= COL_QKV + DN_CONV_W:
                prev = cbuf_ref[:, c0:c1]
                acc = cur * convw_ref[DN_CONV - 1:DN_CONV, c0:c1]
                for s in range(1, DN_CONV):
                    sh = pltpu.roll(cur, s, axis=0)
                    head = jnp.where(top < s, pltpu.roll(prev, s, axis=0), sh[:SUBLANES])
                    sh = jnp.concatenate([head, sh[SUBLANES:]], axis=0)
                    acc = acc + sh * convw_ref[DN_CONV - 1 - s:DN_CONV - s, c0:c1]
                cbuf_ref[:, c0:c1] = cur[tm - SUBLANES:]
                cur = acc * _sigmoid(acc)
            proj_ref[:, c0:c1] = cur

    @pl.when(i < n_prompt_tiles)
    def _():
        @pl.when(i % tiles_per_seq == 0)
        def _():
            cbuf_ref[...] = jnp.zeros(cbuf_ref.shape, F32)
            cbuf_ref[hist:SUBLANES, :] = conv0_ref[0]

        project(xp_ref, True)
        tail_ref[0] = cbuf_ref[...]

    @pl.when(i >= n_prompt_tiles)
    def _():
        project(xs_ref, False)


def _inproj(xp, xs, norm_w, w_big, w_small, conv_w, conv0, seq):
    tp, ts = xp.shape[0], xs.shape[0]
    tm = _token_tile(tp, ts, seq)
    assert COL_QKV == 0 and DN_CONV_W % PROJ_CHUNK == 0
    npt, nst = tp // tm, ts // tm
    tps = seq // tm
    const = lambda i: (0, 0)
    seq_of = lambda i: (jnp.minimum(i, npt - 1) // tps, 0, 0)
    return pl.pallas_call(
        functools.partial(_inproj_kernel, n_prompt_tiles=npt, tiles_per_seq=tps),
        grid=(npt + nst,),
        in_specs=[
            pl.BlockSpec((tm, D_MODEL), lambda i: (jnp.minimum(i, npt - 1), 0)),
            pl.BlockSpec((tm, D_MODEL), lambda i: (jnp.maximum(i - npt, 0), 0)),
            pl.BlockSpec((1, D_MODEL), const),
            pl.BlockSpec((D_MODEL, PROJ_W), const, pipeline_mode=pl.Buffered(1)),
            pl.BlockSpec((D_MODEL, LANES), const),
            pl.BlockSpec((DN_CONV, DN_CONV_W), const),
            pl.BlockSpec((1, DN_CONV - 1, DN_CONV_W), seq_of),
        ],
        out_specs=[
            pl.BlockSpec((tm, PROJ_W), lambda i: (i, 0)),
            pl.BlockSpec((tm, LANES), lambda i: (i, 0)),
            pl.BlockSpec((1, SUBLANES, DN_CONV_W), seq_of),
        ],
        out_shape=[
            jax.ShapeDtypeStruct((tp + ts, PROJ_W), F32),
            jax.ShapeDtypeStruct((tp + ts, LANES), F32),
            jax.ShapeDtypeStruct((tp // seq, SUBLANES, DN_CONV_W), F32),
        ],
        scratch_shapes=[pltpu.VMEM((SUBLANES, DN_CONV_W), F32)],
        compiler_params=_cparams(("arbitrary",)),
        name="inproj",
    )(xp, xs, norm_w, w_big, w_small, conv_w, conv0)


def _dn_core(groups, alog, dtb, nw, read_state, write_state, n_seg, seg_valid):
    rows = groups[0][0].shape[0]
    sr = rows // n_seg
    assert sr * n_seg == rows and sr & (sr - 1) == 0 and rows <= LANES
    seg_shift = sr.bit_length() - 1
    ri = _row_iota((rows, rows))
    ci = _lane_iota((rows, rows))
    incl = ri >= ci
    strict = ri > ci
    if n_seg > 1:
        same = (ri >> seg_shift) == (ci >> seg_shift)
        incl = incl & same
        strict = strict & same
    l_incl = incl.astype(F32)
    eye = (ri == ci).astype(F32)
    levels = max(1, math.ceil(math.log2(seg_valid)))

    beta_all, gsum_all, gtot_all, gsum_t = [], [], [], []
    for _, _, ba, _ in groups:
        b_all = _sigmoid(ba)
        sp = ba + dtb
        softplus = jnp.maximum(sp, 0.0) + jnp.log1p(jnp.exp(-jnp.abs(sp)))
        g_all = -jnp.exp(alog) * softplus
        if seg_valid < sr:
            live = (_row_iota((rows, LANES)) & (sr - 1)) < seg_valid
            b_all = jnp.where(live, b_all, 0.0)
            g_all = jnp.where(live, g_all, 0.0)
        gs = _dot_exact(l_incl, g_all)
        beta_all.append(b_all)
        gsum_all.append(gs)
        gtot_all.append(_dot_exact(same.astype(F32), g_all) if n_seg > 1 else gs[rows - 1:rows, :])
        padded = gs if rows == LANES else jnp.concatenate([gs, jnp.zeros((LANES - rows, LANES), F32)], axis=0)
        gsum_t.append(padded.T)

    probs = [(g, h) for g in range(len(groups)) for h in range(DN_HEADS)]
    segs = range(n_seg)
    q, k, v, kb, beta, gsum, gtot = {}, {}, {}, {}, {}, {}, {}
    for p in probs:
        g, h = p
        qkv = groups[g][0]
        qh = qkv[:, h * DN_DK:(h + 1) * DN_DK]
        kh = qkv[:, DN_QK_W + h * DN_DK:DN_QK_W + (h + 1) * DN_DK]
        v[p] = qkv[:, 2 * DN_QK_W + h * DN_DV:2 * DN_QK_W + (h + 1) * DN_DV]
        q[p] = qh * lax.rsqrt(jnp.sum(qh * qh, axis=-1, keepdims=True) + 1e-6) * (DN_DK ** -0.5)
        k[p] = kh * lax.rsqrt(jnp.sum(kh * kh, axis=-1, keepdims=True) + 1e-6)
        beta[p] = _col(beta_all[g], h)
        gsum[p] = _col(gsum_all[g], DN_HEADS + h)
        gtot[p] = _col(gtot_all[g], DN_HEADS + h)
        kb[p] = k[p] * beta[p]
    kq = {p: _dot_nt(jnp.concatenate([kb[p], q[p]], axis=0), k[p]) for p in probs}
    gamma = {(g, h): jnp.exp(jnp.where(incl, gsum[(g, h)] - gsum_t[g][DN_HEADS + h:DN_HEADS + h + 1, :rows], -jnp.inf))
             for g, h in probs}
    attn = {p: kq[p][rows:] * gamma[p] for p in probs}
    pw = {p: -jnp.where(strict, kq[p][:rows] * gamma[p], 0.0) for p in probs}
    t = {p: eye + pw[p] for p in probs}
    for _ in range(1, levels):
        pw = {p: _dot(pw[p], pw[p]) for p in probs}
        t = {p: t[p] + _dot(t[p], pw[p]) for p in probs}
    eg = {p: jnp.exp(gsum[p]) for p in probs}
    uw = {p: _dot(t[p], jnp.concatenate([v[p] * beta[p], kb[p] * eg[p]], axis=1)) for p in probs}
    qg = {p: q[p] * eg[p] for p in probs}
    state = {(p, s): read_state(p[0], s, p[1]) for p in probs for s in segs}
    wq = {(p, s): _dot(jnp.concatenate([uw[p][s * sr:(s + 1) * sr, DN_DV:], qg[p][s * sr:(s + 1) * sr]], axis=0),
                       state[(p, s)]) for p in probs for s in segs}
    join = lambda pieces: pieces[0] if len(pieces) == 1 else jnp.concatenate(pieces, axis=0)
    v_new = {p: uw[p][:, :DN_DV] - join([wq[(p, s)][:sr] for s in segs]) for p in probs}
    o = {p: join([wq[(p, s)][sr:] for s in segs]) + _dot(attn[p], v_new[p]) for p in probs}
    kd = {p: k[p] * jnp.exp(gtot[p] - gsum[p]) for p in probs}
    for p in probs:
        for s in segs:
            r0 = s * sr if n_seg > 1 else 0
            decay = jnp.exp(gtot[p][r0:r0 + 1, :])
            write_state(p[0], s, p[1],
                        state[(p, s)] * decay + _dot_tn(kd[p][s * sr:(s + 1) * sr], v_new[p][s * sr:(s + 1) * sr]))
    outs = []
    for g, (_, z, _, gate) in enumerate(groups):
        heads = []
        for h in range(DN_HEADS):
            oh = o[(g, h)]
            zz = z[:, h * DN_DV:(h + 1) * DN_DV]
            on = oh * lax.rsqrt(jnp.mean(oh * oh, axis=-1, keepdims=True) + EPS) * nw
            heads.append(on * (zz * _sigmoid(zz)) * _sigmoid(gate[:, h * DN_DV:(h + 1) * DN_DV]))
        outs.append(jnp.concatenate(heads, axis=1))
    return outs


def _dn_prompt_kernel(*refs, chunk, n_batch):
    nb = n_batch
    qkv_refs, z_refs, ba_refs, gate_refs = refs[0:nb], refs[nb:2 * nb], refs[2 * nb:3 * nb], refs[3 * nb:4 * nb]
    alog_ref, dtb_ref, nw_ref, s0_ref, o_ref, sout_ref = refs[4 * nb:]

    @pl.when(pl.program_id(0) == 0)
    def _():
        sout_ref[...] = s0_ref[...]

    groups = [(qkv_refs[b][...], z_refs[b][...], ba_refs[b][...], gate_refs[b][...]) for b in range(nb)]

    def read_state(g, s, h):
        return sout_ref[g, h]

    def write_state(g, s, h, val):
        sout_ref[g, h] = val

    outs = _dn_core(groups, alog_ref[...], dtb_ref[...], nw_ref[...], read_state, write_state, 1, chunk)
    for b in range(nb):
        o_ref[b] = outs[b]


def _dn_prompt(proj, ba, alog_row, dtb_row, dn_nw, s0, n_batch, seq):
    chunk = min(DN_CHUNK, seq)
    assert seq % chunk == 0 and chunk % SUBLANES == 0
    nc = seq // chunk
    const2 = lambda c: (0, 0)
    rows = lambda b, col: (lambda c: (b * nc + c, col))
    batches = range(n_batch)
    o, s_out = pl.pallas_call(
        functools.partial(_dn_prompt_kernel, chunk=chunk, n_batch=n_batch),
        grid=(nc,),
        in_specs=(
            [pl.BlockSpec((chunk, DN_CONV_W), rows(b, COL_QKV // DN_CONV_W)) for b in batches]
            + [pl.BlockSpec((chunk, DN_V_W), rows(b, COL_Z // DN_V_W)) for b in batches]
            + [pl.BlockSpec((chunk, LANES), rows(b, 0)) for b in batches]
            + [pl.BlockSpec((chunk, D_MODEL), rows(b, COL_GA // D_MODEL)) for b in batches]
            + [
                pl.BlockSpec((1, LANES), const2),
                pl.BlockSpec((1, LANES), const2),
                pl.BlockSpec((1, DN_DV), const2),
                pl.BlockSpec((n_batch, DN_HEADS, DN_DK, DN_DV), lambda c: (0, 0, 0, 0)),
            ]
        ),
        out_specs=[
            pl.BlockSpec((n_batch, chunk, DN_V_W), lambda c: (0, c, 0)),
            pl.BlockSpec((n_batch, DN_HEADS, DN_DK, DN_DV), lambda c: (0, 0, 0, 0)),
        ],
        out_shape=[
            jax.ShapeDtypeStruct((n_batch, seq, DN_V_W), F32),
            jax.ShapeDtypeStruct((n_batch, DN_HEADS, DN_DK, DN_DV), F32),
        ],
        compiler_params=_cparams(("arbitrary",)),
        name="dn_prompt",
    )(*([proj] * n_batch), *([proj] * n_batch), *([ba] * n_batch), *([proj] * n_batch), alog_row, dtb_row, dn_nw, s0)
    return o.reshape(n_batch * seq, DN_V_W), s_out


def _spread_rows(ref, n_bb, seq):
    per = SUBLANES // seq
    pieces = []
    for j in range(n_bb // per):
        x8 = ref[j * SUBLANES:(j + 1) * SUBLANES, :]
        for r in range(per):
            pieces.append(x8 if r == 0 else pltpu.roll(x8, SUBLANES - r * seq, axis=0))
    return pieces


def _gather_rows(pieces, seq):
    per = SUBLANES // seq
    rows = _row_iota(pieces[0].shape)
    tiles = []
    for j in range(len(pieces) // per):
        tile = pieces[j * per]
        for r in range(1, per):
            tile = jnp.where(rows >= r * seq, pltpu.roll(pieces[j * per + r], r * seq, axis=0), tile)
        tiles.append(tile)
    return jnp.concatenate(tiles, axis=0)


def _dn_sample_kernel(qkv_ref, z_ref, ba_ref, gate_ref, convw_ref, alog_ref, dtb_ref, nw_ref, conv0_ref, s0_ref,
                      o_ref, convout_ref, sout_ref, cbuf_ref, *, seq, n_bb):
    hist = SUBLANES - (DN_CONV - 1)
    spread = lambda ref: _spread_rows(ref, n_bb, seq)
    for bb, piece in enumerate(spread(qkv_ref)):
        cbuf_ref[bb, SUBLANES:2 * SUBLANES, :] = piece
    cbuf_ref[:, hist:SUBLANES, :] = conv0_ref[...]
    w = convw_ref[...]
    acc = cbuf_ref[:, hist:hist + SUBLANES, :] * w[0:1, :]
    for i in range(1, DN_CONV):
        acc = acc + cbuf_ref[:, hist + i:hist + i + SUBLANES, :] * w[i:i + 1, :]
    live = _row_iota(acc.shape) < seq
    qkv = jnp.where(live, acc * _sigmoid(acc), 0.0).reshape(n_bb * SUBLANES, DN_CONV_W)
    convout_ref[...] = cbuf_ref[:, SUBLANES + seq - (DN_CONV - 1):SUBLANES + seq, :]

    def read_state(g, s, h):
        return s0_ref[s, h]

    def write_state(g, s, h, val):
        sout_ref[s, h] = val

    group = (qkv, jnp.concatenate(spread(z_ref), axis=0), jnp.concatenate(spread(ba_ref), axis=0),
             jnp.concatenate(spread(gate_ref), axis=0))
    o = _dn_core([group], alog_ref[...], dtb_ref[...], nw_ref[...], read_state, write_state, n_bb, seq)[0]
    o_ref[...] = _gather_rows([o[bb * SUBLANES:(bb + 1) * SUBLANES] for bb in range(n_bb)], seq)


def _dn_sample(proj, ba, conv_w, alog_row, dtb_row, dn_nw, conv0, s0, row0, n_batch, seq):
    assert SUBLANES % seq == 0 and seq >= DN_CONV - 1
    n_bb = SUBLANES
    rows_in = n_bb * seq
    assert n_batch % n_bb == 0 and row0 % rows_in == 0
    rb0 = row0 // rows_in
    const1 = lambda i: (0, 0)
    return pl.pallas_call(
        functools.partial(_dn_sample_kernel, seq=seq, n_bb=n_bb),
        grid=(n_batch // n_bb,),
        in_specs=[
            pl.BlockSpec((rows_in, DN_CONV_W), lambda i: (rb0 + i, COL_QKV // DN_CONV_W)),
            pl.BlockSpec((rows_in, DN_V_W), lambda i: (rb0 + i, COL_Z // DN_V_W)),
            pl.BlockSpec((rows_in, LANES), lambda i: (rb0 + i, 0)),
            pl.BlockSpec((rows_in, D_MODEL), lambda i: (rb0 + i, COL_GA // D_MODEL)),
            pl.BlockSpec((DN_CONV, DN_CONV_W), const1),
            pl.BlockSpec((1, LANES), const1),
            pl.BlockSpec((1, LANES), const1),
            pl.BlockSpec((1, DN_DV), const1),
            pl.BlockSpec((n_bb, DN_CONV - 1, DN_CONV_W), lambda i: (i, 0, 0)),
            pl.BlockSpec((n_bb, DN_HEADS, DN_DK, DN_DV), lambda i: (i, 0, 0, 0)),
        ],
        out_specs=[
            pl.BlockSpec((rows_in, DN_V_W), lambda i: (i, 0)),
            pl.BlockSpec((n_bb, DN_CONV - 1, DN_CONV_W), lambda i: (i, 0, 0)),
            pl.BlockSpec((n_bb, DN_HEADS, DN_DK, DN_DV), lambda i: (i, 0, 0, 0)),
        ],
        out_shape=[
            jax.ShapeDtypeStruct((n_batch * seq, DN_V_W), F32),
            jax.ShapeDtypeStruct((n_batch, DN_CONV - 1, DN_CONV_W), F32),
            jax.ShapeDtypeStruct((n_batch, DN_HEADS, DN_DK, DN_DV), F32),
        ],
        scratch_shapes=[pltpu.VMEM((n_bb, 2 * SUBLANES, DN_CONV_W), F32)],
        compiler_params=_cparams(("arbitrary",)),
        name="dn_sample",
    )(proj, proj, ba, proj, conv_w, alog_row, dtb_row, dn_nw, conv0, s0)


def _rel_bucket(dist):
    n = jnp.maximum(dist, 0)
    max_exact = REL_BUCKETS // 2
    large = max_exact + (jnp.log(jnp.maximum(n, 1).astype(F32) / max_exact)
                         / math.log(REL_MAX_DIST / max_exact) * (REL_BUCKETS - max_exact)).astype(I32)
    return jnp.where(n < max_exact, n, jnp.minimum(large, REL_BUCKETS - 1))


def _relbias_kernel(tab_ref, bucket_ref, o_ref):
    h = pl.program_id(0)
    bk = bucket_ref[...]
    acc = jnp.full(bk.shape, -jnp.inf, F32)
    for b in range(REL_BUCKETS):
        acc = jnp.where(bk == b, tab_ref[b * SW_HEADS + h], acc)
    o_ref[0] = acc


def _relbias(rel_table, bucket):
    nq, ns = bucket.shape
    return pl.pallas_call(
        _relbias_kernel,
        grid=(SW_HEADS,),
        in_specs=[
            pl.BlockSpec(memory_space=pltpu.SMEM),
            pl.BlockSpec((nq, ns), lambda h: (0, 0)),
        ],
        out_specs=pl.BlockSpec((1, nq, ns), lambda h: (h, 0, 0)),
        out_shape=jax.ShapeDtypeStruct((SW_HEADS, nq, ns), F32),
        compiler_params=_cparams(("arbitrary",)),
        name="relbias",
    )(rel_table.reshape(-1), bucket)


def _dup_halves(x):
    lo = _lane_iota(x.shape) < SW_HD
    xr = pltpu.roll(x, SW_HD, axis=1)
    return jnp.where(lo, x, xr).astype(BF16), jnp.where(lo, xr, x).astype(BF16)


def _sink_softmax_pv(s, sink, vv):
    m = jnp.maximum(jnp.max(s, axis=-1, keepdims=True), sink)
    p = jnp.exp(s - m)
    denom = jnp.sum(p, axis=-1, keepdims=True) + jnp.exp(sink - m)
    return _dot(p, vv) * (1.0 / denom)


def _swa_prompt_kernel(sinks_ref, q_ref, kc_ref, kp_ref, vc_ref, vp_ref, bias_ref, gate_ref, other_ref,
                       o_ref, klast_ref, vlast_ref):
    klast_ref[0] = kc_ref[...]
    vlast_ref[0] = vc_ref[...]
    kk = _dup_halves(jnp.concatenate([kp_ref[...], kc_ref[...]], axis=0))
    vv = _dup_halves(jnp.concatenate([vp_ref[...], vc_ref[...]], axis=0))
    lo = _lane_iota((SW_BLOCK, LANES)) < SW_HD
    for pair in range(SW_HEADS // 2):
        qp = q_ref[:, pair * LANES:(pair + 1) * LANES] * (SW_HD ** -0.5)
        outs = []
        for half in range(2):
            hq = 2 * pair + half
            kv = hq // SW_GROUP
            qm = jnp.where(lo if half == 0 else ~lo, qp, 0.0)
            s = _dot_nt(qm, kk[kv]) + bias_ref[0, hq]
            outs.append(_sink_softmax_pv(s, sinks_ref[hq], vv[kv]))
        cols = slice(pair * LANES, (pair + 1) * LANES)
        o_ref[:, cols] = other_ref[:, cols] + _sigmoid(gate_ref[:, cols]) * jnp.where(lo, outs[0], outs[1])


def _swa_prompt(proj, sinks, bias, other, n_batch, seq):
    assert seq % SW_BLOCK == 0 and WINDOW == SW_BLOCK
    nb = seq // SW_BLOCK
    cur = lambda col: (lambda b, i: (b * nb + i, col))
    prev = lambda col: (lambda b, i: (b * nb + jnp.maximum(i - 1, 0), col))
    return pl.pallas_call(
        _swa_prompt_kernel,
        grid=(n_batch, nb),
        in_specs=[
            pl.BlockSpec(memory_space=pltpu.SMEM),
            pl.BlockSpec((SW_BLOCK, SW_HEADS * SW_HD), cur(COL_SQ // (SW_HEADS * SW_HD))),
            pl.BlockSpec((SW_BLOCK, SW_KV_W), cur(COL_SK // SW_KV_W)),
            pl.BlockSpec((SW_BLOCK, SW_KV_W), prev(COL_SK // SW_KV_W)),
            pl.BlockSpec((SW_BLOCK, SW_KV_W), cur(COL_SV // SW_KV_W)),
            pl.BlockSpec((SW_BLOCK, SW_KV_W), prev(COL_SV // SW_KV_W)),
            pl.BlockSpec((1, SW_HEADS, SW_BLOCK, 2 * SW_BLOCK), lambda b, i: (jnp.minimum(i, 1), 0, 0, 0)),
            pl.BlockSpec((SW_BLOCK, D_MODEL), cur(COL_GB // D_MODEL)),
            pl.BlockSpec((SW_BLOCK, D_MODEL), lambda b, i: (b * nb + i, 0)),
        ],
        out_specs=[
            pl.BlockSpec((SW_BLOCK, SW_HEADS * SW_HD), lambda b, i: (b * nb + i, 0)),
            pl.BlockSpec((1, SW_BLOCK, SW_KV_W), lambda b, i: (b, 0, 0)),
            pl.BlockSpec((1, SW_BLOCK, SW_KV_W), lambda b, i: (b, 0, 0)),
        ],
        out_shape=[
            jax.ShapeDtypeStruct((n_batch * seq, SW_HEADS * SW_HD), F32),
            jax.ShapeDtypeStruct((n_batch, SW_BLOCK, SW_KV_W), F32),
            jax.ShapeDtypeStruct((n_batch, SW_BLOCK, SW_KV_W), F32),
        ],
        compiler_params=_cparams(("arbitrary", "arbitrary")),
        name="swa_prompt",
    )(sinks, proj, proj, proj, proj, proj, bias, proj, other)


def _swa_sample_kernel(q_ref, kn_ref, vn_ref, kc_ref, vc_ref, bias_ref, sink_ref, gate_ref, other_ref,
                       o_ref, ko_ref, vo_ref, kall_ref, vall_ref, *, seq, n_bb, n_cache):
    n_keys = kall_ref.shape[1]
    zeros_tail = jnp.zeros((n_bb, n_keys - n_cache - SUBLANES, LANES), F32)
    lo = _lane_iota((SUBLANES, LANES)) < SW_HD
    for ref, cache_ref, new_ref in ((kall_ref, kc_ref, kn_ref), (vall_ref, vc_ref, vn_ref)):
        ref[:, 0:n_cache, :] = cache_ref[...]
        for bb, piece in enumerate(_spread_rows(new_ref, n_bb, seq)):
            ref[bb, n_cache:n_cache + SUBLANES, :] = piece
        ref[:, n_cache + SUBLANES:, :] = zeros_tail
    ko_ref[...] = kall_ref[:, seq:seq + n_cache, :]
    vo_ref[...] = vall_ref[:, seq:seq + n_cache, :]
    outs = []
    for bb, q8 in enumerate(_spread_rows(q_ref, n_bb, seq)):
        kk = _dup_halves(kall_ref[bb])
        vv = _dup_halves(vall_ref[bb])
        pairs = []
        for kv in range(SW_KV_HEADS):
            pieces = []
            for g in range(SW_GROUP):
                hq = kv * SW_GROUP + g
                qp = q8[:, (hq // 2) * LANES:(hq // 2 + 1) * LANES]
                pieces.append(jnp.where(lo if hq % 2 == 0 else ~lo, qp, 0.0))
            qs = jnp.concatenate(pieces, axis=0) * (SW_HD ** -0.5)
            s = _dot_nt(qs, kk[kv]) + bias_ref[kv]
            res = _sink_softmax_pv(s, _col(sink_ref[kv], 0), vv[kv])
            for g in range(0, SW_GROUP, 2):
                pairs.append(jnp.where(lo, res[g * SUBLANES:(g + 1) * SUBLANES],
                                       res[(g + 1) * SUBLANES:(g + 2) * SUBLANES]))
        outs.append(jnp.concatenate(pairs, axis=1))
    o_ref[...] = other_ref[...] + _sigmoid(gate_ref[...]) * _gather_rows(outs, seq)


def _swa_sample(proj, k_cache, v_cache, bias, sink_rows, other, row0, n_batch, seq):
    assert SUBLANES % seq == 0
    n_bb = SUBLANES
    rows_in = n_bb * seq
    n_cache = k_cache.shape[1]
    assert n_batch % n_bb == 0 and row0 % rows_in == 0 and n_cache % SUBLANES == 0
    n_keys = bias.shape[-1]
    rb0 = row0 // rows_in
    blk = lambda col: (lambda i: (rb0 + i, col))
    return pl.pallas_call(
        functools.partial(_swa_sample_kernel, seq=seq, n_bb=n_bb, n_cache=n_cache),
        grid=(n_batch // n_bb,),
        in_specs=[
            pl.BlockSpec((rows_in, SW_HEADS * SW_HD), blk(COL_SQ // (SW_HEADS * SW_HD))),
            pl.BlockSpec((rows_in, SW_KV_W), blk(COL_SK // SW_KV_W)),
            pl.BlockSpec((rows_in, SW_KV_W), blk(COL_SV // SW_KV_W)),
            pl.BlockSpec((n_bb, n_cache, SW_KV_W), lambda i: (i, 0, 0)),
            pl.BlockSpec((n_bb, n_cache, SW_KV_W), lambda i: (i, 0, 0)),
            pl.BlockSpec((SW_KV_HEADS, SW_GROUP * SUBLANES, n_keys), lambda i: (0, 0, 0)),
            pl.BlockSpec((SW_KV_HEADS, SW_GROUP * SUBLANES, n_keys), lambda i: (0, 0, 0)),
            pl.BlockSpec((rows_in, D_MODEL), blk(COL_GB // D_MODEL)),
            pl.BlockSpec((rows_in, D_MODEL), lambda i: (i, 0)),
        ],
        out_specs=[
            pl.BlockSpec((rows_in, SW_HEADS * SW_HD), lambda i: (i, 0)),
            pl.BlockSpec((n_bb, n_cache, SW_KV_W), lambda i: (i, 0, 0)),
            pl.BlockSpec((n_bb, n_cache, SW_KV_W), lambda i: (i, 0, 0)),
        ],
        out_shape=[
            jax.ShapeDtypeStruct((n_batch * seq, SW_HEADS * SW_HD), F32),
            jax.ShapeDtypeStruct(k_cache.shape, F32),
            jax.ShapeDtypeStruct(v_cache.shape, F32),
        ],
        scratch_shapes=[pltpu.VMEM((n_bb, n_keys, SW_KV_W), F32), pltpu.VMEM((n_bb, n_keys, SW_KV_W), F32)],
        compiler_params=_cparams(("arbitrary",)),
        name="swa_sample",
    )(proj, proj, proj, k_cache, v_cache, bias, sink_rows, proj, other)


def _mix_kernel(xp_ref, xs_ref, mp_ref, ms_ref, wo_ref, nw_ref, wr_ref, br_ref,
                x1_ref, h2_ref, route_ref, *, n_prompt_tiles):
    i = pl.program_id(0)

    def run(x_ref, mixed_ref):
        x1 = x_ref[...] + _dot(mixed_ref[...], wo_ref[...])
        x1_ref[...] = x1
        h2 = x1 * lax.rsqrt(jnp.mean(x1 * x1, axis=-1, keepdims=True) + EPS) * nw_ref[...]
        h2_ref[...] = h2
        logits = _dot(h2, wr_ref[...]) + br_ref[...]
        lane = _lane_iota(logits.shape)
        lanef = lane.astype(F32)
        big = float(2 * LANES)
        is_g = lane < N_GROUPS
        gl = jnp.where(is_g, logits, -jnp.inf)
        gmax = jnp.max(gl, axis=-1, keepdims=True)
        gval = 1.0 / jnp.sum(jnp.where(is_g, jnp.exp(gl - gmax), 0.0), axis=-1, keepdims=True)
        grp = jnp.min(jnp.where(gl == gmax, lanef, big), axis=-1, keepdims=True)
        e_grp = ((lane - N_GROUPS) >> 3).astype(F32)
        is_e = (lane >= N_GROUPS) & (lane < N_GROUPS + N_EXPERTS) & (e_grp == grp)
        el = jnp.where(is_e, logits, -jnp.inf)
        v1 = jnp.max(el, axis=-1, keepdims=True)
        i1 = jnp.min(jnp.where(el == v1, lanef, big), axis=-1, keepdims=True)
        el2 = jnp.where(lanef == i1, -jnp.inf, el)
        v2 = jnp.max(el2, axis=-1, keepdims=True)
        i2 = jnp.min(jnp.where(el2 == v2, lanef, big), axis=-1, keepdims=True)
        e2 = jnp.exp(v2 - v1)
        w1 = gval / (1.0 + e2)
        w2 = gval * e2 / (1.0 + e2)
        route_ref[...] = jnp.where(lane == 0, i1 - N_GROUPS,
                                   jnp.where(lane == 1, i2 - N_GROUPS,
                                             jnp.where(lane == 2, w1, jnp.where(lane == 3, w2, 0.0))))

    @pl.when(i < n_prompt_tiles)
    def _():
        run(xp_ref, mp_ref)

    @pl.when(i >= n_prompt_tiles)
    def _():
        run(xs_ref, ms_ref)


def _mix(xp, xs, mixed_p, mixed_s, w_out, norm_w, w_router, b_router):
    tp, ts = xp.shape[0], xs.shape[0]
    tm = _token_tile(tp, ts)
    npt, nst = tp // tm, ts // tm
    const = lambda i: (0, 0)
    row = lambda i: (i, 0)
    return pl.pallas_call(
        functools.partial(_mix_kernel, n_prompt_tiles=npt),
        grid=(npt + nst,),
        in_specs=[
            pl.BlockSpec((tm, D_MODEL), lambda i: (jnp.minimum(i, npt - 1), 0)),
            pl.BlockSpec((tm, D_MODEL), lambda i: (jnp.maximum(i - npt, 0), 0)),
            pl.BlockSpec((tm, D_MODEL), lambda i: (jnp.minimum(i, npt - 1), 0)),
            pl.BlockSpec((tm, D_MODEL), lambda i: (jnp.maximum(i - npt, 0), 0)),
            pl.BlockSpec((D_MODEL, D_MODEL), const),
            pl.BlockSpec((1, D_MODEL), const),
            pl.BlockSpec((D_MODEL, LANES), const),
            pl.BlockSpec((1, LANES), const),
        ],
        out_specs=[
            pl.BlockSpec((tm, D_MODEL), row),
            pl.BlockSpec((tm, D_MODEL), row),
            pl.BlockSpec((tm, LANES), row),
        ],
        out_shape=[
            jax.ShapeDtypeStruct((tp + ts, D_MODEL), F32),
            jax.ShapeDtypeStruct((tp + ts, D_MODEL), F32),
            jax.ShapeDtypeStruct((tp + ts, LANES), F32),
        ],
        compiler_params=_cparams(("arbitrary",)),
        name="mix_router",
    )(xp, xs, mixed_p, mixed_s, w_out, norm_w, w_router, b_router)


def _rank_kernel(route_ref, dest_ref, meta_ref, rank_ref, cnt_ref, *, tile, blk):
    phase = pl.program_id(0)
    i = pl.program_id(1)
    shape = (tile, LANES)
    lane = _lane_iota(shape)
    lanef = lane.astype(F32)
    r = route_ref[...]
    oh0 = lanef == _col(r, 0)
    oh1 = lanef == _col(r, 1)
    rows = pl.ds(pl.multiple_of(i * tile, tile), tile)

    @pl.when(phase == 0)
    def _():
        @pl.when(i == 0)
        def _():
            cnt_ref[...] = jnp.zeros(cnt_ref.shape, F32)

        oh = jnp.where(oh0 | oh1, 1.0, 0.0)
        tri = jnp.where(_row_iota((tile, tile)) > _lane_iota((tile, tile)), 1.0, 0.0)
        before = _dot(tri, oh) + cnt_ref[0:1, :]
        rank0 = jnp.sum(jnp.where(oh0, before, 0.0), axis=-1, keepdims=True)
        rank1 = jnp.sum(jnp.where(oh1, before, 0.0), axis=-1, keepdims=True)
        rank_ref[rows, :] = jnp.where(lane == 0, rank0, jnp.where(lane == 1, rank1, 0.0))
        cnt_ref[0:1, :] = cnt_ref[0:1, :] + jnp.sum(oh, axis=0, keepdims=True)

    @pl.when(phase == 1)
    def _():
        cnt = cnt_ref[0:1, :]
        padded = jnp.floor((cnt + (blk - 1)) / blk) * blk
        before_lane = jnp.where(_row_iota((LANES, LANES)) < _lane_iota((LANES, LANES)), 1.0, 0.0)
        start = _dot_exact(jnp.broadcast_to(padded, (SUBLANES, LANES)), before_lane)[0:1, :]
        rk = rank_ref[rows, :]
        d0 = jnp.sum(jnp.where(oh0, start, 0.0), axis=-1, keepdims=True) + _col(rk, 0)
        d1 = jnp.sum(jnp.where(oh1, start, 0.0), axis=-1, keepdims=True) + _col(rk, 1)
        dest_ref[...] = jnp.where(lane == 0, d0, jnp.where(lane == 1, d1, 0.0)).astype(I32)

        @pl.when(i == 0)
        def _():
            end = start + padded
            mshape = meta_ref.shape
            blk_start = (_row_iota(mshape) * blk).astype(F32)
            hit = (_lane_iota(mshape) < N_EXPERTS) & (end <= blk_start)
            be = jnp.minimum(jnp.sum(jnp.where(hit, 1.0, 0.0), axis=-1, keepdims=True), N_EXPERTS - 1.0)
            n_used = _col(end, N_EXPERTS - 1) / blk
            ml = _lane_iota(mshape)
            mine = ml.astype(F32) == be
            seg_start = jnp.sum(jnp.where(mine, start, 0.0), axis=-1, keepdims=True)
            seg_count = jnp.sum(jnp.where(mine, cnt, 0.0), axis=-1, keepdims=True)
            n_valid = jnp.clip(seg_count - (blk_start[:, 0:1] - seg_start), 0.0, float(blk))
            meta_ref[...] = jnp.where(ml == 0, be, jnp.where(ml == 1, n_used,
                                                              jnp.where(ml == 2, n_valid, 0.0))).astype(I32)


def _rank(route, tile, blk, n_blocks):
    t = route.shape[0]
    nt = t // tile
    nbp = -(-n_blocks // SUBLANES) * SUBLANES
    return pl.pallas_call(
        functools.partial(_rank_kernel, tile=tile, blk=blk),
        grid=(2, nt),
        in_specs=[pl.BlockSpec((tile, LANES), lambda p, i: (i, 0))],
        out_specs=[
            pl.BlockSpec((tile, LANES), lambda p, i: (i * p, 0)),
            pl.BlockSpec((nbp, LANES), lambda p, i: (0, 0)),
        ],
        out_shape=[
            jax.ShapeDtypeStruct((t, LANES), I32),
            jax.ShapeDtypeStruct((nbp, LANES), I32),
        ],
        scratch_shapes=[pltpu.VMEM((t, LANES), F32), pltpu.VMEM((SUBLANES, LANES), F32)],
        compiler_params=_cparams(("arbitrary", "arbitrary")),
        name="moe_rank",
    )(route)


def _row_copy(src, src_row, dst, dst_row, sem):
    return pltpu.make_async_copy(src.at[pl.ds(src_row, 1)], dst.at[pl.ds(dst_row, 1)], sem)


def _last_used(i, nu_ref):
    return jnp.minimum(i, jnp.maximum(nu_ref[0] - 1, 0))


def _for_rows(n, body):
    def group(g, carry):
        for u in range(SUBLANES):
            body(g * SUBLANES + u, u)
        return carry

    def single(t, carry):
        body(t, 0)
        return carry

    n_groups = n // SUBLANES
    lax.fori_loop(0, n_groups, group, 0)
    lax.fori_loop(n_groups * SUBLANES, n, single, 0)


def _expert_kernel(be_ref, nu_ref, nv_ref, dest_ref, h2_ref, wg_ref, wu_ref, wd_ref, y_ref,
                   xbuf_ref, wgu_ref, wdn_ref, inv_ref, sem, *, blk):
    i = pl.program_id(0)
    n_used = nu_ref[0]
    used = i < n_used
    slot = i % 2
    blk_i = _last_used(i, nu_ref)
    fresh = (i == 0) | (be_ref[blk_i] != be_ref[jnp.maximum(blk_i - 1, 0)])

    def gather(block, into):
        _for_rows(nv_ref[block], lambda t, u: _row_copy(h2_ref, inv_ref[block * blk + t], xbuf_ref.at[into], t,
                                                       sem.at[into]).start(priority=u % 2))

    @pl.when(i == 0)
    def _():
        def place(a, carry):
            inv_ref[dest_ref[a]] = a >> 1
            return carry

        lax.fori_loop(0, dest_ref.shape[0], place, 0, unroll=16)
        xbuf_ref[...] = jnp.zeros(xbuf_ref.shape, F32)
        gather(0, 0)

    @pl.when(i + 1 < n_used)
    def _():
        gather(i + 1, 1 - slot)

    @pl.when(used & fresh)
    def _():
        wgu_ref[:, :D_EXPERT] = wg_ref[0].astype(BF16)
        wgu_ref[:, D_EXPERT:] = wu_ref[0].astype(BF16)
        wdn_ref[...] = wd_ref[0].astype(BF16)

    @pl.when(used)
    def _():
        _for_rows(nv_ref[i], lambda t, u: _row_copy(h2_ref, 0, xbuf_ref.at[slot], 0, sem.at[slot]).wait())
        gu = jnp.dot(xbuf_ref[slot].astype(BF16), wgu_ref[...], preferred_element_type=F32)
        g = gu[:, :D_EXPERT]
        hidden = (g * _sigmoid(g) * gu[:, D_EXPERT:]).astype(BF16)
        y_ref[...] = jnp.dot(hidden, wdn_ref[...], preferred_element_type=F32)

    @pl.when(jnp.logical_not(used))
    def _():
        y_ref[...] = jnp.zeros(y_ref.shape, F32)


def _experts(block_expert, n_used, n_valid, dest_flat, h2, w_gate, w_up, w_down, blk):
    n_blocks = block_expert.shape[0]
    wsel = lambda i, be, nu, nv, de: (be[_last_used(i, nu)], 0, 0)
    return pl.pallas_call(
        functools.partial(_expert_kernel, blk=blk),
        grid_spec=pltpu.PrefetchScalarGridSpec(
            num_scalar_prefetch=4,
            grid=(n_blocks,),
            in_specs=[
                pl.BlockSpec(memory_space=pl.ANY),
                pl.BlockSpec((1, D_MODEL, D_EXPERT), wsel),
                pl.BlockSpec((1, D_MODEL, D_EXPERT), wsel),
                pl.BlockSpec((1, D_EXPERT, D_MODEL), wsel),
            ],
            out_specs=pl.BlockSpec((blk, D_MODEL), lambda i, be, nu, nv, de: (i, 0)),
            scratch_shapes=[
                pltpu.VMEM((2, blk, D_MODEL), F32),
                pltpu.VMEM((D_MODEL, 2 * D_EXPERT), BF16),
                pltpu.VMEM((D_EXPERT, D_MODEL), BF16),
                pltpu.SMEM((n_blocks * blk,), I32),
                pltpu.SemaphoreType.DMA((2,)),
            ],
        ),
        out_shape=jax.ShapeDtypeStruct((n_blocks * blk, D_MODEL), F32),
        compiler_params=_cparams(("arbitrary",)),
        name="moe_experts",
    )(block_expert, n_used, n_valid, dest_flat, h2, w_gate, w_up, w_down)


def _combine_kernel(dest_ref, x1_ref, route_ref, nw_ref, ys_ref, yp_ref, ysm_ref, ybuf_ref, sem,
                    *, tile, n_prompt_tiles):
    i = pl.program_id(0)
    slot = i % 2

    def gather(step, into):
        def issue(t, carry):
            for k in range(2):
                _row_copy(ys_ref, dest_ref[2 * (step * tile + t) + k], ybuf_ref.at[into], k * tile + t,
                          sem.at[into]).start(priority=k)
            return carry

        lax.fori_loop(0, tile, issue, 0, unroll=4)

    @pl.when(i == 0)
    def _():
        gather(0, 0)

    @pl.when(i + 1 < pl.num_programs(0))
    def _():
        gather(i + 1, 1 - slot)

    def drain(t, carry):
        _row_copy(ys_ref, 0, ybuf_ref.at[slot], 0, sem.at[slot]).wait()
        return carry

    lax.fori_loop(0, 2 * tile, drain, 0, unroll=8)
    r = route_ref[...]
    ybuf = ybuf_ref.at[slot]
    y = ybuf[0:tile, :] * _col(r, 2) + ybuf[tile:2 * tile, :] * _col(r, 3)
    x2 = x1_ref[...] + y
    out = x2 * lax.rsqrt(jnp.mean(x2 * x2, axis=-1, keepdims=True) + EPS) * nw_ref[...]

    @pl.when(i < n_prompt_tiles)
    def _():
        yp_ref[...] = out

    @pl.when(i >= n_prompt_tiles)
    def _():
        ysm_ref[...] = out


def _combine(dest_flat, x1, route, norm_w, ys, tp, ts):
    tile = _token_tile(tp, ts)
    npt, nst = tp // tile, ts // tile
    return pl.pallas_call(
        functools.partial(_combine_kernel, tile=tile, n_prompt_tiles=npt),
        grid_spec=pltpu.PrefetchScalarGridSpec(
            num_scalar_prefetch=1,
            grid=(npt + nst,),
            in_specs=[
                pl.BlockSpec((tile, D_MODEL), lambda i, d: (i, 0)),
                pl.BlockSpec((tile, LANES), lambda i, d: (i, 0)),
                pl.BlockSpec((1, D_MODEL), lambda i, d: (0, 0)),
                pl.BlockSpec(memory_space=pl.ANY),
            ],
            out_specs=[
                pl.BlockSpec((tile, D_MODEL), lambda i, d: (jnp.minimum(i, npt - 1), 0)),
                pl.BlockSpec((tile, D_MODEL), lambda i, d: (jnp.maximum(i - npt, 0), 0)),
            ],
            scratch_shapes=[pltpu.VMEM((2, 2 * tile, D_MODEL), F32), pltpu.SemaphoreType.DMA((2,))],
        ),
        out_shape=[
            jax.ShapeDtypeStruct((tp, D_MODEL), F32),
            jax.ShapeDtypeStruct((ts, D_MODEL), F32),
        ],
        compiler_params=_cparams(("arbitrary",)),
        name="moe_combine",
    )(dest_flat, x1, route, norm_w, ys)


def _layer(xp, xs, n_batch, seq, s_batch, s_seq, conv_state, dn_state, k_cache, v_cache,
           w_in, conv_w, a_log, dt_bias, dn_norm_w, sinks, rel_bias, w_out, norm_mix_w, norm_ffn_w,
           w_rg, b_rg, w_re, b_re, w_gate, w_up, w_down, norm_final_w):
    tp, ts = xp.shape[0], xs.shape[0]
    t_all = tp + ts
    row = lambda v: v.reshape(1, -1).astype(F32)

    o = np.cumsum((0, DN_QK_W, DN_QK_W, DN_V_W, DN_V_W, DN_HEADS, DN_HEADS, SW_HEADS * SW_HD, SW_KV_W, SW_KV_W,
                   D_MODEL, D_MODEL)).tolist()
    w_big = jnp.concatenate([w_in[:, o[0]:o[4]], w_in[:, o[6]:o[7]], w_in[:, o[9]:o[11]], w_in[:, o[7]:o[9]]],
                            axis=1).astype(BF16)
    w_small = jnp.pad(w_in[:, o[4]:o[6]], ((0, 0), (0, LANES - 2 * DN_HEADS))).astype(BF16)
    head_row = lambda v: jnp.pad(v.astype(F32), (DN_HEADS, LANES - 2 * DN_HEADS)).reshape(1, LANES)
    w_router = jnp.pad(jnp.concatenate([w_rg, w_re], axis=1),
                       ((0, 0), (0, LANES - N_GROUPS - N_EXPERTS))).astype(BF16)
    b_router = jnp.pad(jnp.concatenate([b_rg, b_re]).astype(F32), (0, LANES - N_GROUPS - N_EXPERTS)).reshape(1, LANES)

    conv0 = jnp.zeros((n_batch, DN_CONV - 1, DN_CONV_W), F32)
    proj, ba, conv_tail = _inproj(xp, xs, row(norm_mix_w), w_big, w_small, conv_w.astype(F32), conv0, seq)
    p_conv = conv_tail[:, SUBLANES - (DN_CONV - 1):, :]

    dn0 = jnp.zeros((n_batch, DN_HEADS, DN_DK, DN_DV), F32)
    oa_p, p_dn = _dn_prompt(proj, ba, head_row(a_log), head_row(dt_bias), row(dn_norm_w), dn0, n_batch, seq)
    oa_s, s_conv, s_dn = _dn_sample(proj, ba, conv_w.astype(F32), head_row(a_log), head_row(dt_bias), row(dn_norm_w),
                                    conv_state, dn_state, tp, s_batch, s_seq)

    qpos = jnp.arange(SW_BLOCK)[:, None]
    kpos = jnp.arange(2 * SW_BLOCK)[None, :] - SW_BLOCK
    in_window = lambda dist: (dist >= 0) & (dist < WINDOW)
    masked_bucket = lambda dist, ok: jnp.where(ok, _rel_bucket(dist), -1)
    bias_p = jnp.stack([
        _relbias(rel_bias.astype(F32), masked_bucket(qpos - kpos, in_window(qpos - kpos) & (kpos >= 0))),
        _relbias(rel_bias.astype(F32), masked_bucket(qpos - kpos, in_window(qpos - kpos)))])
    mixed_p, p_k, p_v = _swa_prompt(proj, sinks.astype(F32), bias_p, oa_p, n_batch, seq)
    n_cache = k_cache.shape[1]
    n_keys = -(-(n_cache + SUBLANES) // LANES) * LANES
    tq = n_cache + jnp.arange(SUBLANES)[:, None]
    dist_s = tq - jnp.arange(n_keys)[None, :]
    bias_s = _relbias(rel_bias.astype(F32), masked_bucket(dist_s, in_window(dist_s)))
    bias_s = bias_s.reshape(SW_KV_HEADS, SW_GROUP * SUBLANES, n_keys)
    sink_rows = jnp.broadcast_to(jnp.repeat(sinks.astype(F32).reshape(SW_KV_HEADS, SW_GROUP), SUBLANES, axis=1)[:, :, None],
                                 (SW_KV_HEADS, SW_GROUP * SUBLANES, n_keys))
    mixed_s, s_k, s_v = _swa_sample(proj, k_cache.reshape(s_batch, n_cache, SW_KV_W),
                                    v_cache.reshape(s_batch, n_cache, SW_KV_W), bias_s, sink_rows, oa_s, tp, s_batch, s_seq)

    x1, h2, route = _mix(xp, xs, mixed_p, mixed_s, w_out.astype(BF16), row(norm_ffn_w), w_router, b_router)

    tile = _token_tile(tp, ts)
    n_blocks = -(-2 * t_all // MOE_BLOCK) + N_EXPERTS
    dest, meta = _rank(route, _token_tile(t_all, cands=(512, 256, 128, 64, 32, 16, 8)), MOE_BLOCK, n_blocks)
    dest_flat = dest[:, :2].reshape(-1)
    block_expert = meta[:n_blocks, 0]
    n_used = meta[0:1, 1]
    n_valid = meta[:n_blocks, 2]
    ys = _experts(block_expert, n_used, n_valid, dest_flat, h2, w_gate, w_up, w_down, MOE_BLOCK)
    y_p, y_s = _combine(dest_flat, x1, route, row(norm_final_w), ys, tp, ts)

    kv_shape = (n_batch, WINDOW, SW_KV_HEADS, SW_HD)
    return (y_p, y_s, p_conv, p_dn, p_k.reshape(kv_shape), p_v.reshape(kv_shape), s_conv, s_dn,
            s_k.reshape(k_cache.shape), s_v.reshape(v_cache.shape))


def kernel(x_prompt, x_sample, state_dn_conv, state_dn, cache_swa_k, cache_swa_v, w_in, conv_w, a_log, dt_bias, dn_norm_w, sinks, rel_bias, w_out, norm_mix_w, norm_ffn_w, w_router_group, b_router_group, w_router_expert, b_router_expert, w_gate, w_up, w_down, norm_final_w):
    depth = w_in.shape[0]
    assert depth == 1, "the final-norm fusion below assumes a single layer"
    n_batch, seq, _ = x_prompt.shape
    s_batch, s_seq, _ = x_sample.shape
    outs = _layer(x_prompt.reshape(-1, D_MODEL), x_sample.reshape(-1, D_MODEL), n_batch, seq, s_batch, s_seq,
                  state_dn_conv[0], state_dn[0], cache_swa_k[0], cache_swa_v[0],
                  w_in[0], conv_w[0], a_log[0], dt_bias[0], dn_norm_w[0], sinks[0], rel_bias,
                  w_out[0], norm_mix_w[0], norm_ffn_w[0], w_router_group[0], b_router_group[0],
                  w_router_expert[0], b_router_expert[0], w_gate[0], w_up[0], w_down[0], norm_final_w)
    y_p, y_s, p_conv, p_dn, p_k, p_v, s_conv, s_dn, s_k, s_v = outs
    return (y_p.reshape(x_prompt.shape), y_s.reshape(x_sample.shape), p_conv[None], p_dn[None], p_k[None], p_v[None],
            s_conv[None], s_dn[None], s_k[None], s_v[None])
```

```python
import functools
import math

import jax
import jax.numpy as jnp
import numpy as np
from jax import lax
from jax.experimental import pallas as pl
from jax.experimental.pallas import tpu as pltpu

F32 = jnp.float32
BF16 = jnp.bfloat16
I32 = jnp.int32

D_MODEL = 1024
DN_HEADS = 8
DN_DK = 128
DN_DV = 128
DN_CONV = 4
DN_CHUNK = 64
DN_QK_W = DN_HEADS * DN_DK
DN_V_W = DN_HEADS * DN_DV
DN_CONV_W = 2 * DN_QK_W + DN_V_W
SW_HEADS = 16
SW_KV_HEADS = 2
SW_GROUP = SW_HEADS // SW_KV_HEADS
SW_HD = 64
SW_KV_W = SW_KV_HEADS * SW_HD
WINDOW = 128
SW_BLOCK = 128
REL_BUCKETS = 32
REL_MAX_DIST = 128
N_GROUPS = 8
EXP_PER_GROUP = 8
N_EXPERTS = N_GROUPS * EXP_PER_GROUP
D_EXPERT = 256
MOE_BLOCK = 256
EPS = 1e-6

LANES = 128
SUBLANES = 8
VMEM_LIMIT = 56 * 1024 * 1024

COL_QKV = 0
COL_Z = 3072
COL_SQ = 4096
COL_GA = 5120
COL_GB = 6144
COL_SK = 7168
COL_SV = 7296
PROJ_W = 7424
PROJ_CHUNK = 512


def _cparams(sem):
    return pltpu.CompilerParams(dimension_semantics=sem, vmem_limit_bytes=VMEM_LIMIT)


def _sigmoid(x):
    return 0.5 * jnp.tanh(0.5 * x) + 0.5


def _dot(a, b):
    return jnp.dot(a.astype(BF16), b.astype(BF16), preferred_element_type=F32)


def _dot_nt(a, b):
    return lax.dot_general(a.astype(BF16), b.astype(BF16), (((1,), (1,)), ((), ())), preferred_element_type=F32)


def _dot_tn(a, b):
    return lax.dot_general(a.astype(BF16), b.astype(BF16), (((0,), (0,)), ((), ())), preferred_element_type=F32)


def _dot_exact(a, b):
    return jnp.dot(a, b, precision=lax.Precision.HIGHEST, preferred_element_type=F32)


def _lane_iota(shape):
    return lax.broadcasted_iota(I32, shape, len(shape) - 1)


def _row_iota(shape):
    return lax.broadcasted_iota(I32, shape, len(shape) - 2)


def _col(x, j):
    return jnp.sum(jnp.where(_lane_iota(x.shape) == j, x, 0.0), axis=-1, keepdims=True)


def _token_tile(*sizes, cands=(256, 128, 64, 32, 16, 8)):
    for t in cands:
        if all(s % t == 0 for s in sizes):
            return t
    raise ValueError(f"token counts {sizes} need a common tile that is a multiple of 8")


def _inproj_kernel(xp_ref, xs_ref, nw_ref, wb_ref, ws_ref, convw_ref, conv0_ref, proj_ref, ba_ref, tail_ref, cbuf_ref,
                   *, n_prompt_tiles, tiles_per_seq):
    i = pl.program_id(0)
    tm = xp_ref.shape[0]
    hist = SUBLANES - (DN_CONV - 1)

    def project(x_ref, conv):
        x = x_ref[...]
        h = (x * lax.rsqrt(jnp.mean(x * x, axis=-1, keepdims=True) + EPS) * nw_ref[...]).astype(BF16)
        ba_ref[...] = jnp.dot(h, ws_ref[...], preferred_element_type=F32)
        top = _row_iota((SUBLANES, PROJ_CHUNK))
        starts = list(range(0, PROJ_W, PROJ_CHUNK))
        if conv:
            with_conv = [c for c in starts if c + PROJ_CHUNK <= COL_QKV + DN_CONV_W]
            plain = [c for c in starts if c not in with_conv]
            starts = [c for pair in zip(with_conv, plain) for c in pair] + plain[len(with_conv):]
        for c0 in starts:
            c1 = min(c0 + PROJ_CHUNK, PROJ_W)
            cur = jnp.dot(h, wb_ref[:, c0:c1], preferred_element_type=F32)
            if conv and c1 <= COL_QKV + DN_CONV_W:
                prev = cbuf_ref[:, c0:c1]
                acc = cur * convw_ref[DN_CONV - 1:DN_CONV, c0:c1]
                for s in range(1, DN_CONV):
                    sh = pltpu.roll(cur, s, axis=0)
                    head = jnp.where(top < s, pltpu.roll(prev, s, axis=0), sh[:SUBLANES])
                    sh = jnp.concatenate([head, sh[SUBLANES:]], axis=0)
                    acc = acc + sh * convw_ref[DN_CONV - 1 - s:DN_CONV - s, c0:c1]
                cbuf_ref[:, c0:c1] = cur[tm - SUBLANES:]
                cur = acc * _sigmoid(acc)
            proj_ref[:, c0:c1] = cur

    @pl.when(i < n_prompt_tiles)
    def _():
        @pl.when(i % tiles_per_seq == 0)
        def _():
            cbuf_ref[...] = jnp.zeros(cbuf_ref.shape, F32)
            cbuf_ref[hist:SUBLANES, :] = conv0_ref[0]

        project(xp_ref, True)
        tail_ref[0] = cbuf_ref[...]

    @pl.when(i >= n_prompt_tiles)
    def _():
        project(xs_ref, False)


def _inproj(xp, xs, norm_w, w_big, w_small, conv_w, conv0, seq):
    tp, ts = xp.shape[0], xs.shape[0]
    tm = _token_tile(tp, ts, seq)
    assert COL_QKV == 0 and DN_CONV_W % PROJ_CHUNK == 0
    npt, nst = tp // tm, ts // tm
    tps = seq // tm
    const = lambda i: (0, 0)
    seq_of = lambda i: (jnp.minimum(i, npt - 1) // tps, 0, 0)
    return pl.pallas_call(
        functools.partial(_inproj_kernel, n_prompt_tiles=npt, tiles_per_seq=tps),
        grid=(npt + nst,),
        in_specs=[
            pl.BlockSpec((tm, D_MODEL), lambda i: (jnp.minimum(i, npt - 1), 0)),
            pl.BlockSpec((tm, D_MODEL), lambda i: (jnp.maximum(i - npt, 0), 0)),
            pl.BlockSpec((1, D_MODEL), const),
            pl.BlockSpec((D_MODEL, PROJ_W), const, pipeline_mode=pl.Buffered(1)),
            pl.BlockSpec((D_MODEL, LANES), const),
            pl.BlockSpec((DN_CONV, DN_CONV_W), const),
            pl.BlockSpec((1, DN_CONV - 1, DN_CONV_W), seq_of),
        ],
        out_specs=[
            pl.BlockSpec((tm, PROJ_W), lambda i: (i, 0)),
            pl.BlockSpec((tm, LANES), lambda i: (i, 0)),
            pl.BlockSpec((1, SUBLANES, DN_CONV_W), seq_of),
        ],
        out_shape=[
            jax.ShapeDtypeStruct((tp + ts, PROJ_W), F32),
            jax.ShapeDtypeStruct((tp + ts, LANES), F32),
            jax.ShapeDtypeStruct((tp // seq, SUBLANES, DN_CONV_W), F32),
        ],
        scratch_shapes=[pltpu.VMEM((SUBLANES, DN_CONV_W), F32)],
        compiler_params=_cparams(("arbitrary",)),
        name="inproj",
    )(xp, xs, norm_w, w_big, w_small, conv_w, conv0)


def _dn_core(groups, alog, dtb, nw, read_state, write_state, n_seg, seg_valid):
    rows = groups[0][0].shape[0]
    sr = rows // n_seg
    assert sr * n_seg == rows and sr & (sr - 1) == 0 and rows <= LANES
    seg_shift = sr.bit_length() - 1
    ri = _row_iota((rows, rows))
    ci = _lane_iota((rows, rows))
    incl = ri >= ci
    strict = ri > ci
    if n_seg > 1:
        same = (ri >> seg_shift) == (ci >> seg_shift)
        incl = incl & same
        strict = strict & same
    l_incl = incl.astype(F32)
    eye = (ri == ci).astype(F32)
    levels = max(1, math.ceil(math.log2(seg_valid)))

    beta_all, gsum_all, gtot_all, gsum_t = [], [], [], []
    for _, _, ba, _ in groups:
        b_all = _sigmoid(ba)
        sp = ba + dtb
        softplus = jnp.maximum(sp, 0.0) + jnp.log1p(jnp.exp(-jnp.abs(sp)))
        g_all = -jnp.exp(alog) * softplus
        if seg_valid < sr:
            live = (_row_iota((rows, LANES)) & (sr - 1)) < seg_valid
            b_all = jnp.where(live, b_all, 0.0)
            g_all = jnp.where(live, g_all, 0.0)
        gs = _dot_exact(l_incl, g_all)
        beta_all.append(b_all)
        gsum_all.append(gs)
        gtot_all.append(_dot_exact(same.astype(F32), g_all) if n_seg > 1 else gs[rows - 1:rows, :])
        padded = gs if rows == LANES else jnp.concatenate([gs, jnp.zeros((LANES - rows, LANES), F32)], axis=0)
        gsum_t.append(padded.T)

    probs = [(g, h) for g in range(len(groups)) for h in range(DN_HEADS)]
    segs = range(n_seg)
    q, k, v, kb, beta, gsum, gtot = {}, {}, {}, {}, {}, {}, {}
    for p in probs:
        g, h = p
        qkv = groups[g][0]
        qh = qkv[:, h * DN_DK:(h + 1) * DN_DK]
        kh = qkv[:, DN_QK_W + h * DN_DK:DN_QK_W + (h + 1) * DN_DK]
        v[p] = qkv[:, 2 * DN_QK_W + h * DN_DV:2 * DN_QK_W + (h + 1) * DN_DV]
        q[p] = qh * lax.rsqrt(jnp.sum(qh * qh, axis=-1, keepdims=True) + 1e-6) * (DN_DK ** -0.5)
        k[p] = kh * lax.rsqrt(jnp.sum(kh * kh, axis=-1, keepdims=True) + 1e-6)
        beta[p] = _col(beta_all[g], h)
        gsum[p] = _col(gsum_all[g], DN_HEADS + h)
        gtot[p] = _col(gtot_all[g], DN_HEADS + h)
        kb[p] = k[p] * beta[p]
    kq = {p: _dot_nt(jnp.concatenate([kb[p], q[p]], axis=0), k[p]) for p in probs}
    gamma = {(g, h): jnp.exp(jnp.where(incl, gsum[(g, h)] - gsum_t[g][DN_HEADS + h:DN_HEADS + h + 1, :rows], -jnp.inf))
             for g, h in probs}
    attn = {p: kq[p][rows:] * gamma[p] for p in probs}
    pw = {p: -jnp.where(strict, kq[p][:rows] * gamma[p], 0.0) for p in probs}
    t = {p: eye + pw[p] for p in probs}
    for _ in range(1, levels):
        pw = {p: _dot(pw[p], pw[p]) for p in probs}
        t = {p: t[p] + _dot(t[p], pw[p]) for p in probs}
    eg = {p: jnp.exp(gsum[p]) for p in probs}
    uw = {p: _dot(t[p], jnp.concatenate([v[p] * beta[p], kb[p] * eg[p]], axis=1)) for p in probs}
    qg = {p: q[p] * eg[p] for p in probs}
    state = {(p, s): read_state(p[0], s, p[1]) for p in probs for s in segs}
    wq = {(p, s): _dot(jnp.concatenate([uw[p][s * sr:(s + 1) * sr, DN_DV:], qg[p][s * sr:(s + 1) * sr]], axis=0),
                       state[(p, s)]) for p in probs for s in segs}
    join = lambda pieces: pieces[0] if len(pieces) == 1 else jnp.concatenate(pieces, axis=0)
    v_new = {p: uw[p][:, :DN_DV] - join([wq[(p, s)][:sr] for s in segs]) for p in probs}
    o = {p: join([wq[(p, s)][sr:] for s in segs]) + _dot(attn[p], v_new[p]) for p in probs}
    kd = {p: k[p] * jnp.exp(gtot[p] - gsum[p]) for p in probs}
    for p in probs:
        for s in segs:
            r0 = s * sr if n_seg > 1 else 0
            decay = jnp.exp(gtot[p][r0:r0 + 1, :])
            write_state(p[0], s, p[1],
                        state[(p, s)] * decay + _dot_tn(kd[p][s * sr:(s + 1) * sr], v_new[p][s * sr:(s + 1) * sr]))
    outs = []
    for g, (_, z, _, gate) in enumerate(groups):
        heads = []
        for h in range(DN_HEADS):
            oh = o[(g, h)]
            zz = z[:, h * DN_DV:(h + 1) * DN_DV]
            on = oh * lax.rsqrt(jnp.mean(oh * oh, axis=-1, keepdims=True) + EPS) * nw
            heads.append(on * (zz * _sigmoid(zz)) * _sigmoid(gate[:, h * DN_DV:(h + 1) * DN_DV]))
        outs.append(jnp.concatenate(heads, axis=1))
    return outs


def _dn_prompt_kernel(*refs, chunk, n_batch):
    nb = n_batch
    qkv_refs, z_refs, ba_refs, gate_refs = refs[0:nb], refs[nb:2 * nb], refs[2 * nb:3 * nb], refs[3 * nb:4 * nb]
    alog_ref, dtb_ref, nw_ref, s0_ref, o_ref, sout_ref = refs[4 * nb:]

    @pl.when(pl.program_id(0) == 0)
    def _():
        sout_ref[...] = s0_ref[...]

    groups = [(qkv_refs[b][...], z_refs[b][...], ba_refs[b][...], gate_refs[b][...]) for b in range(nb)]

    def read_state(g, s, h):
        return sout_ref[g, h]

    def write_state(g, s, h, val):
        sout_ref[g, h] = val

    outs = _dn_core(groups, alog_ref[...], dtb_ref[...], nw_ref[...], read_state, write_state, 1, chunk)
    for b in range(nb):
        o_ref[b] = outs[b]


def _dn_prompt(proj, ba, alog_row, dtb_row, dn_nw, s0, n_batch, seq):
    chunk = min(DN_CHUNK, seq)
    assert seq % chunk == 0 and chunk % SUBLANES == 0
    nc = seq // chunk
    const2 = lambda c: (0, 0)
    rows = lambda b, col: (lambda c: (b * nc + c, col))
    batches = range(n_batch)
    o, s_out = pl.pallas_call(
        functools.partial(_dn_prompt_kernel, chunk=chunk, n_batch=n_batch),
        grid=(nc,),
        in_specs=(
            [pl.BlockSpec((chunk, DN_CONV_W), rows(b, COL_QKV // DN_CONV_W)) for b in batches]
            + [pl.BlockSpec((chunk, DN_V_W), rows(b, COL_Z // DN_V_W)) for b in batches]
            + [pl.BlockSpec((chunk, LANES), rows(b, 0)) for b in batches]
            + [pl.BlockSpec((chunk, D_MODEL), rows(b, COL_GA // D_MODEL)) for b in batches]
            + [
                pl.BlockSpec((1, LANES), const2),
                pl.BlockSpec((1, LANES), const2),
                pl.BlockSpec((1, DN_DV), const2),
                pl.BlockSpec((n_batch, DN_HEADS, DN_DK, DN_DV), lambda c: (0, 0, 0, 0)),
            ]
        ),
        out_specs=[
            pl.BlockSpec((n_batch, chunk, DN_V_W), lambda c: (0, c, 0)),
            pl.BlockSpec((n_batch, DN_HEADS, DN_DK, DN_DV), lambda c: (0, 0, 0, 0)),
        ],
        out_shape=[
            jax.ShapeDtypeStruct((n_batch, seq, DN_V_W), F32),
            jax.ShapeDtypeStruct((n_batch, DN_HEADS, DN_DK, DN_DV), F32),
        ],
        compiler_params=_cparams(("arbitrary",)),
        name="dn_prompt",
    )(*([proj] * n_batch), *([proj] * n_batch), *([ba] * n_batch), *([proj] * n_batch), alog_row, dtb_row, dn_nw, s0)
    return o.reshape(n_batch * seq, DN_V_W), s_out


def _spread_rows(ref, n_bb, seq):
    per = SUBLANES // seq
    pieces = []
    for j in range(n_bb // per):
        x8 = ref[j * SUBLANES:(j + 1) * SUBLANES, :]
        for r in range(per):
            pieces.append(x8 if r == 0 else pltpu.roll(x8, SUBLANES - r * seq, axis=0))
    return pieces


def _gather_rows(pieces, seq):
    per = SUBLANES // seq
    rows = _row_iota(pieces[0].shape)
    tiles = []
    for j in range(len(pieces) // per):
        tile = pieces[j * per]
        for r in range(1, per):
            tile = jnp.where(rows >= r * seq, pltpu.roll(pieces[j * per + r], r * seq, axis=0), tile)
        tiles.append(tile)
    return jnp.concatenate(tiles, axis=0)


def _dn_sample_kernel(qkv_ref, z_ref, ba_ref, gate_ref, convw_ref, alog_ref, dtb_ref, nw_ref, conv0_ref, s0_ref,
                      o_ref, convout_ref, sout_ref, cbuf_ref, *, seq, n_bb):
    hist = SUBLANES - (DN_CONV - 1)
    spread = lambda ref: _spread_rows(ref, n_bb, seq)
    for bb, piece in enumerate(spread(qkv_ref)):
        cbuf_ref[bb, SUBLANES:2 * SUBLANES, :] = piece
    cbuf_ref[:, hist:SUBLANES, :] = conv0_ref[...]
    w = convw_ref[...]
    acc = cbuf_ref[:, hist:hist + SUBLANES, :] * w[0:1, :]
    for i in range(1, DN_CONV):
        acc = acc + cbuf_ref[:, hist + i:hist + i + SUBLANES, :] * w[i:i + 1, :]
    live = _row_iota(acc.shape) < seq
    qkv = jnp.where(live, acc * _sigmoid(acc), 0.0).reshape(n_bb * SUBLANES, DN_CONV_W)
    convout_ref[...] = cbuf_ref[:, SUBLANES + seq - (DN_CONV - 1):SUBLANES + seq, :]

    def read_state(g, s, h):
        return s0_ref[s, h]

    def write_state(g, s, h, val):
        sout_ref[s, h] = val

    group = (qkv, jnp.concatenate(spread(z_ref), axis=0), jnp.concatenate(spread(ba_ref), axis=0),
             jnp.concatenate(spread(gate_ref), axis=0))
    o = _dn_core([group], alog_ref[...], dtb_ref[...], nw_ref[...], read_state, write_state, n_bb, seq)[0]
    o_ref[...] = _gather_rows([o[bb * SUBLANES:(bb + 1) * SUBLANES] for bb in range(n_bb)], seq)


def _dn_sample(proj, ba, conv_w, alog_row, dtb_row, dn_nw, conv0, s0, row0, n_batch, seq):
    assert SUBLANES % seq == 0 and seq >= DN_CONV - 1
    n_bb = SUBLANES
    rows_in = n_bb * seq
    assert n_batch % n_bb == 0 and row0 % rows_in == 0
    rb0 = row0 // rows_in
    const1 = lambda i: (0, 0)
    return pl.pallas_call(
        functools.partial(_dn_sample_kernel, seq=seq, n_bb=n_bb),
        grid=(n_batch // n_bb,),
        in_specs=[
            pl.BlockSpec((rows_in, DN_CONV_W), lambda i: (rb0 + i, COL_QKV // DN_CONV_W)),
            pl.BlockSpec((rows_in, DN_V_W), lambda i: (rb0 + i, COL_Z // DN_V_W)),
            pl.BlockSpec((rows_in, LANES), lambda i: (rb0 + i, 0)),
            pl.BlockSpec((rows_in, D_MODEL), lambda i: (rb0 + i, COL_GA // D_MODEL)),
            pl.BlockSpec((DN_CONV, DN_CONV_W), const1),
            pl.BlockSpec((1, LANES), const1),
            pl.BlockSpec((1, LANES), const1),
            pl.BlockSpec((1, DN_DV), const1),
            pl.BlockSpec((n_bb, DN_CONV - 1, DN_CONV_W), lambda i: (i, 0, 0)),
            pl.BlockSpec((n_bb, DN_HEADS, DN_DK, DN_DV), lambda i: (i, 0, 0, 0)),
        ],
        out_specs=[
            pl.BlockSpec((rows_in, DN_V_W), lambda i: (i, 0)),
            pl.BlockSpec((n_bb, DN_CONV - 1, DN_CONV_W), lambda i: (i, 0, 0)),
            pl.BlockSpec((n_bb, DN_HEADS, DN_DK, DN_DV), lambda i: (i, 0, 0, 0)),
        ],
        out_shape=[
            jax.ShapeDtypeStruct((n_batch * seq, DN_V_W), F32),
            jax.ShapeDtypeStruct((n_batch, DN_CONV - 1, DN_CONV_W), F32),
            jax.ShapeDtypeStruct((n_batch, DN_HEADS, DN_DK, DN_DV), F32),
        ],
        scratch_shapes=[pltpu.VMEM((n_bb, 2 * SUBLANES, DN_CONV_W), F32)],
        compiler_params=_cparams(("arbitrary",)),
        name="dn_sample",
    )(proj, proj, ba, proj, conv_w, alog_row, dtb_row, dn_nw, conv0, s0)


def _masked_bucket(dist, ok):
    n = np.maximum(dist, 0)
    max_exact = REL_BUCKETS // 2
    large = max_exact + (np.log(np.maximum(n, 1).astype(np.float32) / max_exact)
                         / math.log(REL_MAX_DIST / max_exact) * (REL_BUCKETS - max_exact)).astype(np.int32)
    bucket = np.where(n < max_exact, n, np.minimum(large, REL_BUCKETS - 1))
    return np.where(ok, bucket, -1).astype(np.int32)


def _relbias_kernel(tab_ref, bucket_ref, o_ref):
    h = pl.program_id(0)
    bk = bucket_ref[...]
    acc = jnp.full(bk.shape, -jnp.inf, F32)
    for b in range(REL_BUCKETS):
        acc = jnp.where(bk == b, tab_ref[b * SW_HEADS + h], acc)
    o_ref[0] = acc


def _relbias(rel_table, bucket):
    nq, ns = bucket.shape
    return pl.pallas_call(
        _relbias_kernel,
        grid=(SW_HEADS,),
        in_specs=[
            pl.BlockSpec(memory_space=pltpu.SMEM),
            pl.BlockSpec((nq, ns), lambda h: (0, 0)),
        ],
        out_specs=pl.BlockSpec((1, nq, ns), lambda h: (h, 0, 0)),
        out_shape=jax.ShapeDtypeStruct((SW_HEADS, nq, ns), F32),
        compiler_params=_cparams(("arbitrary",)),
        name="relbias",
    )(rel_table.reshape(-1), bucket)


def _dup_halves(x):
    lo = _lane_iota(x.shape) < SW_HD
    xr = pltpu.roll(x, SW_HD, axis=1)
    return jnp.where(lo, x, xr).astype(BF16), jnp.where(lo, xr, x).astype(BF16)


def _sink_softmax_pv(s, sink, vv):
    m = jnp.maximum(jnp.max(s, axis=-1, keepdims=True), sink)
    p = jnp.exp(s - m)
    denom = jnp.sum(p, axis=-1, keepdims=True) + jnp.exp(sink - m)
    return _dot(p, vv) * (1.0 / denom)


def _swa_prompt_kernel(sinks_ref, q_ref, kc_ref, kp_ref, vc_ref, vp_ref, bias_ref, gate_ref, other_ref,
                       o_ref, klast_ref, vlast_ref, *, n_sub):
    i = pl.program_id(1)
    klast_ref[0] = kc_ref[(n_sub - 1) * SW_BLOCK:, :]
    vlast_ref[0] = vc_ref[(n_sub - 1) * SW_BLOCK:, :]
    kk_all = _dup_halves(jnp.concatenate([kp_ref[...], kc_ref[...]], axis=0))
    vv_all = _dup_halves(jnp.concatenate([vp_ref[...], vc_ref[...]], axis=0))
    lo = _lane_iota((SW_BLOCK, LANES)) < SW_HD
    for sub in range(n_sub):
        rows = slice(sub * SW_BLOCK, (sub + 1) * SW_BLOCK)
        keys = slice(sub * SW_BLOCK, (sub + 2) * SW_BLOCK)
        bias_rows = pl.ds(pl.multiple_of(jnp.minimum(i, 1) * SW_BLOCK, SW_BLOCK), SW_BLOCK) if sub == 0 \
            else pl.ds(SW_BLOCK, SW_BLOCK)
        for pair in range(SW_HEADS // 2):
            cols = slice(pair * LANES, (pair + 1) * LANES)
            qp = q_ref[rows, cols] * (SW_HD ** -0.5)
            outs = []
            for half in range(2):
                hq = 2 * pair + half
                kv = hq // SW_GROUP
                qm = jnp.where(lo if half == 0 else ~lo, qp, 0.0)
                s = _dot_nt(qm, kk_all[kv][keys]) + bias_ref[hq, bias_rows, :]
                outs.append(_sink_softmax_pv(s, sinks_ref[hq], vv_all[kv][keys]))
            o_ref[rows, cols] = other_ref[rows, cols] + _sigmoid(gate_ref[rows, cols]) * jnp.where(lo, outs[0], outs[1])


def _swa_prompt(proj, sinks, bias, other, n_batch, seq):
    assert seq % SW_BLOCK == 0 and WINDOW == SW_BLOCK
    nb = seq // SW_BLOCK
    n_sub = 2 if nb % 2 == 0 else 1
    ns = nb // n_sub
    rows = n_sub * SW_BLOCK
    cur = lambda col: (lambda b, i: (b * ns + i, col))
    prev = lambda col: (lambda b, i: (b * nb + jnp.maximum(i * n_sub - 1, 0), col))
    return pl.pallas_call(
        functools.partial(_swa_prompt_kernel, n_sub=n_sub),
        grid=(n_batch, ns),
        in_specs=[
            pl.BlockSpec(memory_space=pltpu.SMEM),
            pl.BlockSpec((rows, SW_HEADS * SW_HD), cur(COL_SQ // (SW_HEADS * SW_HD))),
            pl.BlockSpec((rows, SW_KV_W), cur(COL_SK // SW_KV_W)),
            pl.BlockSpec((SW_BLOCK, SW_KV_W), prev(COL_SK // SW_KV_W)),
            pl.BlockSpec((rows, SW_KV_W), cur(COL_SV // SW_KV_W)),
            pl.BlockSpec((SW_BLOCK, SW_KV_W), prev(COL_SV // SW_KV_W)),
            pl.BlockSpec((SW_HEADS, 2 * SW_BLOCK, 2 * SW_BLOCK), lambda b, i: (0, 0, 0)),
            pl.BlockSpec((rows, D_MODEL), cur(COL_GB // D_MODEL)),
            pl.BlockSpec((rows, D_MODEL), lambda b, i: (b * ns + i, 0)),
        ],
        out_specs=[
            pl.BlockSpec((rows, SW_HEADS * SW_HD), lambda b, i: (b * ns + i, 0)),
            pl.BlockSpec((1, SW_BLOCK, SW_KV_W), lambda b, i: (b, 0, 0)),
            pl.BlockSpec((1, SW_BLOCK, SW_KV_W), lambda b, i: (b, 0, 0)),
        ],
        out_shape=[
            jax.ShapeDtypeStruct((n_batch * seq, SW_HEADS * SW_HD), F32),
            jax.ShapeDtypeStruct((n_batch, SW_BLOCK, SW_KV_W), F32),
            jax.ShapeDtypeStruct((n_batch, SW_BLOCK, SW_KV_W), F32),
        ],
        compiler_params=_cparams(("arbitrary", "arbitrary")),
        name="swa_prompt",
    )(sinks, proj, proj, proj, proj, proj, bias, proj, other)


def _swa_sample_kernel(q_ref, kn_ref, vn_ref, kc_ref, vc_ref, bias_ref, sink_ref, gate_ref, other_ref,
                       o_ref, ko_ref, vo_ref, kall_ref, vall_ref, *, seq, n_bb, n_cache):
    n_keys = kall_ref.shape[1]
    zeros_tail = jnp.zeros((n_bb, n_keys - n_cache - SUBLANES, LANES), F32)
    lo = _lane_iota((SUBLANES, LANES)) < SW_HD
    for ref, cache_ref, new_ref in ((kall_ref, kc_ref, kn_ref), (vall_ref, vc_ref, vn_ref)):
        ref[:, 0:n_cache, :] = cache_ref[...]
        for bb, piece in enumerate(_spread_rows(new_ref, n_bb, seq)):
            ref[bb, n_cache:n_cache + SUBLANES, :] = piece
        ref[:, n_cache + SUBLANES:, :] = zeros_tail
    ko_ref[...] = kall_ref[:, seq:seq + n_cache, :]
    vo_ref[...] = vall_ref[:, seq:seq + n_cache, :]
    outs = []
    for bb, q8 in enumerate(_spread_rows(q_ref, n_bb, seq)):
        kk = _dup_halves(kall_ref[bb])
        vv = _dup_halves(vall_ref[bb])
        pairs = []
        for kv in range(SW_KV_HEADS):
            pieces = []
            for g in range(SW_GROUP):
                hq = kv * SW_GROUP + g
                qp = q8[:, (hq // 2) * LANES:(hq // 2 + 1) * LANES]
                pieces.append(jnp.where(lo if hq % 2 == 0 else ~lo, qp, 0.0))
            qs = jnp.concatenate(pieces, axis=0) * (SW_HD ** -0.5)
            bias = jnp.concatenate([bias_ref[kv * SW_GROUP + g] for g in range(SW_GROUP)], axis=0)
            s = _dot_nt(qs, kk[kv]) + bias
            res = _sink_softmax_pv(s, _col(sink_ref[kv], 0), vv[kv])
            for g in range(0, SW_GROUP, 2):
                pairs.append(jnp.where(lo, res[g * SUBLANES:(g + 1) * SUBLANES],
                                       res[(g + 1) * SUBLANES:(g + 2) * SUBLANES]))
        outs.append(jnp.concatenate(pairs, axis=1))
    o_ref[...] = other_ref[...] + _sigmoid(gate_ref[...]) * _gather_rows(outs, seq)


def _swa_sample(proj, k_cache, v_cache, bias, bias_row0, sink_rows, other, row0, n_batch, seq):
    assert SUBLANES % seq == 0
    n_bb = SUBLANES
    rows_in = n_bb * seq
    n_cache = k_cache.shape[1]
    assert n_batch % n_bb == 0 and row0 % rows_in == 0 and n_cache % SUBLANES == 0
    n_keys = bias.shape[-1]
    rb0 = row0 // rows_in
    blk = lambda col: (lambda i: (rb0 + i, col))
    return pl.pallas_call(
        functools.partial(_swa_sample_kernel, seq=seq, n_bb=n_bb, n_cache=n_cache),
        grid=(n_batch // n_bb,),
        in_specs=[
            pl.BlockSpec((rows_in, SW_HEADS * SW_HD), blk(COL_SQ // (SW_HEADS * SW_HD))),
            pl.BlockSpec((rows_in, SW_KV_W), blk(COL_SK // SW_KV_W)),
            pl.BlockSpec((rows_in, SW_KV_W), blk(COL_SV // SW_KV_W)),
            pl.BlockSpec((n_bb, n_cache, SW_KV_W), lambda i: (i, 0, 0)),
            pl.BlockSpec((n_bb, n_cache, SW_KV_W), lambda i: (i, 0, 0)),
            pl.BlockSpec((SW_HEADS, SUBLANES, n_keys), lambda i: (0, bias_row0 // SUBLANES, 0)),
            pl.BlockSpec((SW_KV_HEADS, SW_GROUP * SUBLANES, n_keys), lambda i: (0, 0, 0)),
            pl.BlockSpec((rows_in, D_MODEL), blk(COL_GB // D_MODEL)),
            pl.BlockSpec((rows_in, D_MODEL), lambda i: (i, 0)),
        ],
        out_specs=[
            pl.BlockSpec((rows_in, SW_HEADS * SW_HD), lambda i: (i, 0)),
            pl.BlockSpec((n_bb, n_cache, SW_KV_W), lambda i: (i, 0, 0)),
            pl.BlockSpec((n_bb, n_cache, SW_KV_W), lambda i: (i, 0, 0)),
        ],
        out_shape=[
            jax.ShapeDtypeStruct((n_batch * seq, SW_HEADS * SW_HD), F32),
            jax.ShapeDtypeStruct(k_cache.shape, F32),
            jax.ShapeDtypeStruct(v_cache.shape, F32),
        ],
        scratch_shapes=[pltpu.VMEM((n_bb, n_keys, SW_KV_W), F32), pltpu.VMEM((n_bb, n_keys, SW_KV_W), F32)],
        compiler_params=_cparams(("arbitrary",)),
        name="swa_sample",
    )(proj, proj, proj, k_cache, v_cache, bias, sink_rows, proj, other)


def _mix_kernel(xp_ref, xs_ref, mp_ref, ms_ref, wo_ref, nw_ref, wr_ref, br_ref,
                x1_ref, h2_ref, route_ref, *, n_prompt_tiles):
    i = pl.program_id(0)

    def run(x_ref, mixed_ref):
        x1 = x_ref[...] + _dot(mixed_ref[...], wo_ref[...])
        x1_ref[...] = x1
        h2 = x1 * lax.rsqrt(jnp.mean(x1 * x1, axis=-1, keepdims=True) + EPS) * nw_ref[...]
        h2_ref[...] = h2
        logits = _dot(h2, wr_ref[...]) + br_ref[...]
        lane = _lane_iota(logits.shape)
        lanef = lane.astype(F32)
        big = float(2 * LANES)
        is_g = lane < N_GROUPS
        gl = jnp.where(is_g, logits, -jnp.inf)
        gmax = jnp.max(gl, axis=-1, keepdims=True)
        gval = 1.0 / jnp.sum(jnp.where(is_g, jnp.exp(gl - gmax), 0.0), axis=-1, keepdims=True)
        grp = jnp.min(jnp.where(gl == gmax, lanef, big), axis=-1, keepdims=True)
        e_grp = ((lane - N_GROUPS) >> 3).astype(F32)
        is_e = (lane >= N_GROUPS) & (lane < N_GROUPS + N_EXPERTS) & (e_grp == grp)
        el = jnp.where(is_e, logits, -jnp.inf)
        v1 = jnp.max(el, axis=-1, keepdims=True)
        i1 = jnp.min(jnp.where(el == v1, lanef, big), axis=-1, keepdims=True)
        el2 = jnp.where(lanef == i1, -jnp.inf, el)
        v2 = jnp.max(el2, axis=-1, keepdims=True)
        i2 = jnp.min(jnp.where(el2 == v2, lanef, big), axis=-1, keepdims=True)
        e2 = jnp.exp(v2 - v1)
        w1 = gval / (1.0 + e2)
        w2 = gval * e2 / (1.0 + e2)
        route_ref[...] = jnp.where(lane == 0, i1 - N_GROUPS,
                                   jnp.where(lane == 1, i2 - N_GROUPS,
                                             jnp.where(lane == 2, w1, jnp.where(lane == 3, w2, 0.0))))

    @pl.when(i < n_prompt_tiles)
    def _():
        run(xp_ref, mp_ref)

    @pl.when(i >= n_prompt_tiles)
    def _():
        run(xs_ref, ms_ref)


def _mix(xp, xs, mixed_p, mixed_s, w_out, norm_w, w_router, b_router):
    tp, ts = xp.shape[0], xs.shape[0]
    tm = _token_tile(tp, ts)
    npt, nst = tp // tm, ts // tm
    const = lambda i: (0, 0)
    row = lambda i: (i, 0)
    return pl.pallas_call(
        functools.partial(_mix_kernel, n_prompt_tiles=npt),
        grid=(npt + nst,),
        in_specs=[
            pl.BlockSpec((tm, D_MODEL), lambda i: (jnp.minimum(i, npt - 1), 0)),
            pl.BlockSpec((tm, D_MODEL), lambda i: (jnp.maximum(i - npt, 0), 0)),
            pl.BlockSpec((tm, D_MODEL), lambda i: (jnp.minimum(i, npt - 1), 0)),
            pl.BlockSpec((tm, D_MODEL), lambda i: (jnp.maximum(i - npt, 0), 0)),
            pl.BlockSpec((D_MODEL, D_MODEL), const),
            pl.BlockSpec((1, D_MODEL), const),
            pl.BlockSpec((D_MODEL, LANES), const),
            pl.BlockSpec((1, LANES), const),
        ],
        out_specs=[
            pl.BlockSpec((tm, D_MODEL), row),
            pl.BlockSpec((tm, D_MODEL), row),
            pl.BlockSpec((tm, LANES), row),
        ],
        out_shape=[
            jax.ShapeDtypeStruct((tp + ts, D_MODEL), F32),
            jax.ShapeDtypeStruct((tp + ts, D_MODEL), F32),
            jax.ShapeDtypeStruct((tp + ts, LANES), F32),
        ],
        compiler_params=_cparams(("arbitrary",)),
        name="mix_router",
    )(xp, xs, mixed_p, mixed_s, w_out, norm_w, w_router, b_router)


def _rank_kernel(route_ref, dest_ref, meta_ref, rank_ref, cnt_ref, *, tile, blk):
    phase = pl.program_id(0)
    i = pl.program_id(1)
    shape = (tile, LANES)
    lane = _lane_iota(shape)
    lanef = lane.astype(F32)
    r = route_ref[...]
    oh0 = lanef == _col(r, 0)
    oh1 = lanef == _col(r, 1)
    rows = pl.ds(pl.multiple_of(i * tile, tile), tile)

    @pl.when(phase == 0)
    def _():
        @pl.when(i == 0)
        def _():
            cnt_ref[...] = jnp.zeros(cnt_ref.shape, F32)

        oh = jnp.where(oh0 | oh1, 1.0, 0.0)
        tri = jnp.where(_row_iota((tile, tile)) > _lane_iota((tile, tile)), 1.0, 0.0)
        before = _dot(tri, oh) + cnt_ref[0:1, :]
        rank0 = jnp.sum(jnp.where(oh0, before, 0.0), axis=-1, keepdims=True)
        rank1 = jnp.sum(jnp.where(oh1, before, 0.0), axis=-1, keepdims=True)
        rank_ref[rows, :] = jnp.where(lane == 0, rank0, jnp.where(lane == 1, rank1, 0.0))
        cnt_ref[0:1, :] = cnt_ref[0:1, :] + jnp.sum(oh, axis=0, keepdims=True)

    @pl.when(phase == 1)
    def _():
        cnt = cnt_ref[0:1, :]
        padded = jnp.floor((cnt + (blk - 1)) / blk) * blk
        before_lane = jnp.where(_row_iota((LANES, LANES)) < _lane_iota((LANES, LANES)), 1.0, 0.0)
        start = _dot_exact(jnp.broadcast_to(padded, (SUBLANES, LANES)), before_lane)[0:1, :]
        rk = rank_ref[rows, :]
        d0 = jnp.sum(jnp.where(oh0, start, 0.0), axis=-1, keepdims=True) + _col(rk, 0)
        d1 = jnp.sum(jnp.where(oh1, start, 0.0), axis=-1, keepdims=True) + _col(rk, 1)
        dest_ref[...] = jnp.where(lane == 0, d0, jnp.where(lane == 1, d1, 0.0)).astype(I32)

        @pl.when(i == 0)
        def _():
            end = start + padded
            mshape = meta_ref.shape
            blk_start = (_row_iota(mshape) * blk).astype(F32)
            hit = (_lane_iota(mshape) < N_EXPERTS) & (end <= blk_start)
            be = jnp.minimum(jnp.sum(jnp.where(hit, 1.0, 0.0), axis=-1, keepdims=True), N_EXPERTS - 1.0)
            n_used = _col(end, N_EXPERTS - 1) / blk
            ml = _lane_iota(mshape)
            mine = ml.astype(F32) == be
            seg_start = jnp.sum(jnp.where(mine, start, 0.0), axis=-1, keepdims=True)
            seg_count = jnp.sum(jnp.where(mine, cnt, 0.0), axis=-1, keepdims=True)
            n_valid = jnp.clip(seg_count - (blk_start[:, 0:1] - seg_start), 0.0, float(blk))
            meta_ref[...] = jnp.where(ml == 0, be, jnp.where(ml == 1, n_used,
                                                              jnp.where(ml == 2, n_valid, 0.0))).astype(I32)


def _rank(route, tile, blk, n_blocks):
    t = route.shape[0]
    nt = t // tile
    nbp = -(-n_blocks // SUBLANES) * SUBLANES
    return pl.pallas_call(
        functools.partial(_rank_kernel, tile=tile, blk=blk),
        grid=(2, nt),
        in_specs=[pl.BlockSpec((tile, LANES), lambda p, i: (i, 0))],
        out_specs=[
            pl.BlockSpec((tile, LANES), lambda p, i: (i * p, 0)),
            pl.BlockSpec((nbp, LANES), lambda p, i: (0, 0)),
        ],
        out_shape=[
            jax.ShapeDtypeStruct((t, LANES), I32),
            jax.ShapeDtypeStruct((nbp, LANES), I32),
        ],
        scratch_shapes=[pltpu.VMEM((t, LANES), F32), pltpu.VMEM((SUBLANES, LANES), F32)],
        compiler_params=_cparams(("arbitrary", "arbitrary")),
        name="moe_rank",
    )(route)


def _row_copy(src, src_row, dst, dst_row, sem):
    return pltpu.make_async_copy(src.at[pl.ds(src_row, 1)], dst.at[pl.ds(dst_row, 1)], sem)


def _last_used(i, nu_ref):
    return jnp.minimum(i, jnp.maximum(nu_ref[0] - 1, 0))


def _for_rows(n, body):
    def group(g, carry):
        for u in range(SUBLANES):
            body(g * SUBLANES + u, u)
        return carry

    def single(t, carry):
        body(t, 0)
        return carry

    n_groups = n // SUBLANES
    lax.fori_loop(0, n_groups, group, 0)
    lax.fori_loop(n_groups * SUBLANES, n, single, 0)


def _expert_kernel(be_ref, nu_ref, nv_ref, dest_ref, h2_ref, wg_ref, wu_ref, wd_ref, y_ref,
                   xbuf_ref, wgu_ref, wdn_ref, inv_ref, sem, *, blk):
    i = pl.program_id(0)
    n_used = nu_ref[0]
    used = i < n_used
    slot = i % 2
    blk_i = _last_used(i, nu_ref)
    fresh = (i == 0) | (be_ref[blk_i] != be_ref[jnp.maximum(blk_i - 1, 0)])

    def gather(block, into):
        _for_rows(nv_ref[block], lambda t, u: _row_copy(h2_ref, inv_ref[block * blk + t], xbuf_ref.at[into], t,
                                                       sem.at[into]).start(priority=u % 2))

    @pl.when(i == 0)
    def _():
        def place(a, carry):
            inv_ref[dest_ref[a]] = a >> 1
            return carry

        lax.fori_loop(0, dest_ref.shape[0], place, 0, unroll=16)
        xbuf_ref[...] = jnp.zeros(xbuf_ref.shape, F32)
        gather(0, 0)

    @pl.when(i + 1 < n_used)
    def _():
        gather(i + 1, 1 - slot)

    @pl.when(used & fresh)
    def _():
        wgu_ref[:, :D_EXPERT] = wg_ref[0].astype(BF16)
        wgu_ref[:, D_EXPERT:] = wu_ref[0].astype(BF16)
        wdn_ref[...] = wd_ref[0].astype(BF16)

    @pl.when(used)
    def _():
        _for_rows(nv_ref[i], lambda t, u: _row_copy(h2_ref, 0, xbuf_ref.at[slot], 0, sem.at[slot]).wait())
        gu = jnp.dot(xbuf_ref[slot].astype(BF16), wgu_ref[...], preferred_element_type=F32)
        g = gu[:, :D_EXPERT]
        hidden = (g * _sigmoid(g) * gu[:, D_EXPERT:]).astype(BF16)
        y_ref[...] = jnp.dot(hidden, wdn_ref[...], preferred_element_type=F32)

    @pl.when(jnp.logical_not(used))
    def _():
        y_ref[...] = jnp.zeros(y_ref.shape, F32)


def _experts(block_expert, n_used, n_valid, dest_flat, h2, w_gate, w_up, w_down, blk):
    n_blocks = block_expert.shape[0]
    wsel = lambda i, be, nu, nv, de: (be[_last_used(i, nu)], 0, 0)
    return pl.pallas_call(
        functools.partial(_expert_kernel, blk=blk),
        grid_spec=pltpu.PrefetchScalarGridSpec(
            num_scalar_prefetch=4,
            grid=(n_blocks,),
            in_specs=[
                pl.BlockSpec(memory_space=pl.ANY),
                pl.BlockSpec((1, D_MODEL, D_EXPERT), wsel),
                pl.BlockSpec((1, D_MODEL, D_EXPERT), wsel),
                pl.BlockSpec((1, D_EXPERT, D_MODEL), wsel),
            ],
            out_specs=pl.BlockSpec((blk, D_MODEL), lambda i, be, nu, nv, de: (i, 0)),
            scratch_shapes=[
                pltpu.VMEM((2, blk, D_MODEL), F32),
                pltpu.VMEM((D_MODEL, 2 * D_EXPERT), BF16),
                pltpu.VMEM((D_EXPERT, D_MODEL), BF16),
                pltpu.SMEM((n_blocks * blk,), I32),
                pltpu.SemaphoreType.DMA((2,)),
            ],
        ),
        out_shape=jax.ShapeDtypeStruct((n_blocks * blk, D_MODEL), F32),
        compiler_params=_cparams(("arbitrary",)),
        name="moe_experts",
    )(block_expert, n_used, n_valid, dest_flat, h2, w_gate, w_up, w_down)


def _combine_kernel(dest_ref, x1_ref, route_ref, nw_ref, ys_ref, yp_ref, ysm_ref, ybuf_ref, sem,
                    *, tile, n_prompt_tiles):
    i = pl.program_id(0)
    slot = i % 2

    def gather(step, into):
        def issue(t, carry):
            for k in range(2):
                _row_copy(ys_ref, dest_ref[2 * (step * tile + t) + k], ybuf_ref.at[into], k * tile + t,
                          sem.at[into]).start(priority=k)
            return carry

        lax.fori_loop(0, tile, issue, 0, unroll=4)

    @pl.when(i == 0)
    def _():
        gather(0, 0)

    @pl.when(i + 1 < pl.num_programs(0))
    def _():
        gather(i + 1, 1 - slot)

    def drain(t, carry):
        _row_copy(ys_ref, 0, ybuf_ref.at[slot], 0, sem.at[slot]).wait()
        return carry

    lax.fori_loop(0, 2 * tile, drain, 0, unroll=8)
    r = route_ref[...]
    ybuf = ybuf_ref.at[slot]
    y = ybuf[0:tile, :] * _col(r, 2) + ybuf[tile:2 * tile, :] * _col(r, 3)
    x2 = x1_ref[...] + y
    out = x2 * lax.rsqrt(jnp.mean(x2 * x2, axis=-1, keepdims=True) + EPS) * nw_ref[...]

    @pl.when(i < n_prompt_tiles)
    def _():
        yp_ref[...] = out

    @pl.when(i >= n_prompt_tiles)
    def _():
        ysm_ref[...] = out


def _combine(dest_flat, x1, route, norm_w, ys, tp, ts):
    tile = _token_tile(tp, ts)
    npt, nst = tp // tile, ts // tile
    return pl.pallas_call(
        functools.partial(_combine_kernel, tile=tile, n_prompt_tiles=npt),
        grid_spec=pltpu.PrefetchScalarGridSpec(
            num_scalar_prefetch=1,
            grid=(npt + nst,),
            in_specs=[
                pl.BlockSpec((tile, D_MODEL), lambda i, d: (i, 0)),
                pl.BlockSpec((tile, LANES), lambda i, d: (i, 0)),
                pl.BlockSpec((1, D_MODEL), lambda i, d: (0, 0)),
                pl.BlockSpec(memory_space=pl.ANY),
            ],
            out_specs=[
                pl.BlockSpec((tile, D_MODEL), lambda i, d: (jnp.minimum(i, npt - 1), 0)),
                pl.BlockSpec((tile, D_MODEL), lambda i, d: (jnp.maximum(i - npt, 0), 0)),
            ],
            scratch_shapes=[pltpu.VMEM((2, 2 * tile, D_MODEL), F32), pltpu.SemaphoreType.DMA((2,))],
        ),
        out_shape=[
            jax.ShapeDtypeStruct((tp, D_MODEL), F32),
            jax.ShapeDtypeStruct((ts, D_MODEL), F32),
        ],
        compiler_params=_cparams(("arbitrary",)),
        name="moe_combine",
    )(dest_flat, x1, route, norm_w, ys)


def _layer(xp, xs, n_batch, seq, s_batch, s_seq, conv_state, dn_state, k_cache, v_cache,
           w_in, conv_w, a_log, dt_bias, dn_norm_w, sinks, rel_bias, w_out, norm_mix_w, norm_ffn_w,
           w_rg, b_rg, w_re, b_re, w_gate, w_up, w_down, norm_final_w):
    tp, ts = xp.shape[0], xs.shape[0]
    t_all = tp + ts
    row = lambda v: v.reshape(1, -1).astype(F32)

    o = np.cumsum((0, DN_QK_W, DN_QK_W, DN_V_W, DN_V_W, DN_HEADS, DN_HEADS, SW_HEADS * SW_HD, SW_KV_W, SW_KV_W,
                   D_MODEL, D_MODEL)).tolist()
    w_big = jnp.concatenate([w_in[:, o[0]:o[4]], w_in[:, o[6]:o[7]], w_in[:, o[9]:o[11]], w_in[:, o[7]:o[9]]],
                            axis=1).astype(BF16)
    w_small = jnp.pad(w_in[:, o[4]:o[6]], ((0, 0), (0, LANES - 2 * DN_HEADS))).astype(BF16)
    head_row = lambda v: jnp.pad(v.astype(F32), (DN_HEADS, LANES - 2 * DN_HEADS)).reshape(1, LANES)
    w_router = jnp.pad(jnp.concatenate([w_rg, w_re], axis=1),
                       ((0, 0), (0, LANES - N_GROUPS - N_EXPERTS))).astype(BF16)
    b_router = jnp.pad(jnp.concatenate([b_rg, b_re]).astype(F32), (0, LANES - N_GROUPS - N_EXPERTS)).reshape(1, LANES)

    conv0 = jnp.zeros((n_batch, DN_CONV - 1, DN_CONV_W), F32)
    proj, ba, conv_tail = _inproj(xp, xs, row(norm_mix_w), w_big, w_small, conv_w.astype(F32), conv0, seq)
    p_conv = conv_tail[:, SUBLANES - (DN_CONV - 1):, :]

    dn0 = jnp.zeros((n_batch, DN_HEADS, DN_DK, DN_DV), F32)
    oa_p, p_dn = _dn_prompt(proj, ba, head_row(a_log), head_row(dt_bias), row(dn_norm_w), dn0, n_batch, seq)
    oa_s, s_conv, s_dn = _dn_sample(proj, ba, conv_w.astype(F32), head_row(a_log), head_row(dt_bias), row(dn_norm_w),
                                    conv_state, dn_state, tp, s_batch, s_seq)

    n_cache = k_cache.shape[1]
    n_keys = -(-(n_cache + SUBLANES) // LANES) * LANES
    assert n_keys == 2 * SW_BLOCK, "the prompt and sample bias tables share one (rows, keys) array"
    dist_p = np.arange(SW_BLOCK)[:, None] - (np.arange(2 * SW_BLOCK)[None, :] - SW_BLOCK)
    dist_s = n_cache + np.arange(SUBLANES)[:, None] - np.arange(n_keys)[None, :]
    in_window = lambda dist: (dist >= 0) & (dist < WINDOW)
    has_prev = np.arange(2 * SW_BLOCK)[None, :] >= SW_BLOCK
    buckets = np.concatenate([_masked_bucket(dist_p, in_window(dist_p) & has_prev),
                              _masked_bucket(dist_p, in_window(dist_p)),
                              _masked_bucket(dist_s, in_window(dist_s))], axis=0)
    bias = _relbias(rel_bias.astype(F32), jnp.asarray(buckets))
    mixed_p, p_k, p_v = _swa_prompt(proj, sinks.astype(F32), bias, oa_p, n_batch, seq)
    sink_rows = jnp.broadcast_to(jnp.repeat(sinks.astype(F32).reshape(SW_KV_HEADS, SW_GROUP), SUBLANES, axis=1)[:, :, None],
                                 (SW_KV_HEADS, SW_GROUP * SUBLANES, n_keys))
    mixed_s, s_k, s_v = _swa_sample(proj, k_cache.reshape(s_batch, n_cache, SW_KV_W),
                                    v_cache.reshape(s_batch, n_cache, SW_KV_W), bias, 2 * SW_BLOCK, sink_rows, oa_s,
                                    tp, s_batch, s_seq)

    x1, h2, route = _mix(xp, xs, mixed_p, mixed_s, w_out.astype(BF16), row(norm_ffn_w), w_router, b_router)

    tile = _token_tile(tp, ts)
    n_blocks = -(-2 * t_all // MOE_BLOCK) + N_EXPERTS
    dest, meta = _rank(route, _token_tile(t_all, cands=(512, 256, 128, 64, 32, 16, 8)), MOE_BLOCK, n_blocks)
    dest_flat = dest[:, :2].reshape(-1)
    block_expert = meta[:n_blocks, 0]
    n_used = meta[0:1, 1]
    n_valid = meta[:n_blocks, 2]
    ys = _experts(block_expert, n_used, n_valid, dest_flat, h2, w_gate, w_up, w_down, MOE_BLOCK)
    y_p, y_s = _combine(dest_flat, x1, route, row(norm_final_w), ys, tp, ts)

    kv_shape = (n_batch, WINDOW, SW_KV_HEADS, SW_HD)
    return (y_p, y_s, p_conv, p_dn, p_k.reshape(kv_shape), p_v.reshape(kv_shape), s_conv, s_dn,
            s_k.reshape(k_cache.shape), s_v.reshape(v_cache.shape))


def kernel(x_prompt, x_sample, state_dn_conv, state_dn, cache_swa_k, cache_swa_v, w_in, conv_w, a_log, dt_bias, dn_norm_w, sinks, rel_bias, w_out, norm_mix_w, norm_ffn_w, w_router_group, b_router_group, w_router_expert, b_router_expert, w_gate, w_up, w_down, norm_final_w):
    depth = w_in.shape[0]
    assert depth == 1, "the final-norm fusion below assumes a single layer"
    n_batch, seq, _ = x_prompt.shape
    s_batch, s_seq, _ = x_sample.shape
    outs = _layer(x_prompt.reshape(-1, D_MODEL), x_sample.reshape(-1, D_MODEL), n_batch, seq, s_batch, s_seq,
                  state_dn_conv[0], state_dn[0], cache_swa_k[0], cache_swa_v[0],
                  w_in[0], conv_w[0], a_log[0], dt_bias[0], dn_norm_w[0], sinks[0], rel_bias,
                  w_out[0], norm_mix_w[0], norm_ffn_w[0], w_router_group[0], b_router_group[0],
                  w_router_expert[0], b_router_expert[0], w_gate[0], w_up[0], w_down[0], norm_final_w)
    y_p, y_s, p_conv, p_dn, p_k, p_v, s_conv, s_dn, s_k, s_v = outs
    return (y_p.reshape(x_prompt.shape), y_s.reshape(x_sample.shape), p_conv[None], p_dn[None], p_k[None], p_v[None],
            s_conv[None], s_dn[None], s_k[None], s_v[None])
```

```python
import functools
import math

import jax
import jax.numpy as jnp
import numpy as np
from jax import lax
from jax.experimental import pallas as pl
from jax.experimental.pallas import tpu as pltpu

F32 = jnp.float32
BF16 = jnp.bfloat16
I32 = jnp.int32

D_MODEL = 1024
DN_HEADS = 8
DN_DK = 128
DN_DV = 128
DN_CONV = 4
DN_CHUNK = 64
DN_QK_W = DN_HEADS * DN_DK
DN_V_W = DN_HEADS * DN_DV
DN_CONV_W = 2 * DN_QK_W + DN_V_W
SW_HEADS = 16
SW_KV_HEADS = 2
SW_GROUP = SW_HEADS // SW_KV_HEADS
SW_HD = 64
SW_KV_W = SW_KV_HEADS * SW_HD
WINDOW = 128
SW_BLOCK = 128
REL_BUCKETS = 32
REL_MAX_DIST = 128
N_GROUPS = 8
EXP_PER_GROUP = 8
N_EXPERTS = N_GROUPS * EXP_PER_GROUP
D_EXPERT = 256
MOE_BLOCK = 256
EPS = 1e-6

LANES = 128
SUBLANES = 8
VMEM_LIMIT = 56 * 1024 * 1024

COL_QKV = 0
COL_Z = 3072
COL_SQ = 4096
COL_GA = 5120
COL_GB = 6144
COL_SK = 7168
COL_SV = 7296
PROJ_W = 7424
PROJ_CHUNK = 512


def _cparams(sem):
    return pltpu.CompilerParams(dimension_semantics=sem, vmem_limit_bytes=VMEM_LIMIT)


def _sigmoid(x):
    return 0.5 * jnp.tanh(0.5 * x) + 0.5


def _dot(a, b):
    return jnp.dot(a.astype(BF16), b.astype(BF16), preferred_element_type=F32)


def _dot_nt(a, b):
    return lax.dot_general(a.astype(BF16), b.astype(BF16), (((1,), (1,)), ((), ())), preferred_element_type=F32)


def _dot_tn(a, b):
    return lax.dot_general(a.astype(BF16), b.astype(BF16), (((0,), (0,)), ((), ())), preferred_element_type=F32)


def _dot_exact(a, b):
    return jnp.dot(a, b, precision=lax.Precision.HIGHEST, preferred_element_type=F32)


def _lane_iota(shape):
    return lax.broadcasted_iota(I32, shape, len(shape) - 1)


def _row_iota(shape):
    return lax.broadcasted_iota(I32, shape, len(shape) - 2)


def _col(x, j):
    return jnp.sum(jnp.where(_lane_iota(x.shape) == j, x, 0.0), axis=-1, keepdims=True)


BIG_TILES = (512, 256, 128, 64, 32, 16, 8)


def _token_tile(*sizes, cands=BIG_TILES[1:]):
    for t in cands:
        if all(s % t == 0 for s in sizes):
            return t
    raise ValueError(f"token counts {sizes} need a common tile that is a multiple of 8")


def _inproj_kernel(xp_ref, xs_ref, nw_ref, wb_ref, ws_ref, convw_ref, conv0_ref, proj_ref, ba_ref, tail_ref, cbuf_ref,
                   *, n_prompt_tiles, tiles_per_seq):
    i = pl.program_id(0)
    tm = xp_ref.shape[0]
    hist = SUBLANES - (DN_CONV - 1)

    def project(x_ref, conv):
        x = x_ref[...]
        h = (x * lax.rsqrt(jnp.mean(x * x, axis=-1, keepdims=True) + EPS) * nw_ref[...]).astype(BF16)
        ba_ref[...] = jnp.dot(h, ws_ref[...], preferred_element_type=F32)
        top = _row_iota((SUBLANES, PROJ_CHUNK))
        starts = list(range(0, PROJ_W, PROJ_CHUNK))
        if conv:
            with_conv = [c for c in starts if c + PROJ_CHUNK <= COL_QKV + DN_CONV_W]
            plain = [c for c in starts if c not in with_conv]
            starts = [c for pair in zip(with_conv, plain) for c in pair] + plain[len(with_conv):]
        for c0 in starts:
            c1 = min(c0 + PROJ_CHUNK, PROJ_W)
            cur = jnp.dot(h, wb_ref[:, c0:c1], preferred_element_type=F32)
            if conv and c1 <= COL_QKV + DN_CONV_W:
                prev = cbuf_ref[:, c0:c1]
                acc = cur * convw_ref[DN_CONV - 1:DN_CONV, c0:c1]
                for s in range(1, DN_CONV):
                    sh = pltpu.roll(cur, s, axis=0)
                    head = jnp.where(top < s, pltpu.roll(prev, s, axis=0), sh[:SUBLANES])
                    sh = jnp.concatenate([head, sh[SUBLANES:]], axis=0)
                    acc = acc + sh * convw_ref[DN_CONV - 1 - s:DN_CONV - s, c0:c1]
                cbuf_ref[:, c0:c1] = cur[tm - SUBLANES:]
                cur = acc * _sigmoid(acc)
            proj_ref[:, c0:c1] = cur

    @pl.when(i < n_prompt_tiles)
    def _():
        @pl.when(i % tiles_per_seq == 0)
        def _():
            cbuf_ref[...] = jnp.zeros(cbuf_ref.shape, F32)
            cbuf_ref[hist:SUBLANES, :] = conv0_ref[0]

        project(xp_ref, True)
        tail_ref[0] = cbuf_ref[...]

    @pl.when(i >= n_prompt_tiles)
    def _():
        project(xs_ref, False)


def _inproj(xp, xs, norm_w, w_big, w_small, conv_w, conv0, seq):
    tp, ts = xp.shape[0], xs.shape[0]
    tm = _token_tile(tp, ts, seq)
    assert COL_QKV == 0 and DN_CONV_W % PROJ_CHUNK == 0
    npt, nst = tp // tm, ts // tm
    tps = seq // tm
    const = lambda i: (0, 0)
    seq_of = lambda i: (jnp.minimum(i, npt - 1) // tps, 0, 0)
    return pl.pallas_call(
        functools.partial(_inproj_kernel, n_prompt_tiles=npt, tiles_per_seq=tps),
        grid=(npt + nst,),
        in_specs=[
            pl.BlockSpec((tm, D_MODEL), lambda i: (jnp.minimum(i, npt - 1), 0)),
            pl.BlockSpec((tm, D_MODEL), lambda i: (jnp.maximum(i - npt, 0), 0)),
            pl.BlockSpec((1, D_MODEL), const),
            pl.BlockSpec((D_MODEL, PROJ_W), const, pipeline_mode=pl.Buffered(1)),
            pl.BlockSpec((D_MODEL, LANES), const),
            pl.BlockSpec((DN_CONV, DN_CONV_W), const),
            pl.BlockSpec((1, DN_CONV - 1, DN_CONV_W), seq_of),
        ],
        out_specs=[
            pl.BlockSpec((tm, PROJ_W), lambda i: (i, 0)),
            pl.BlockSpec((tm, LANES), lambda i: (i, 0)),
            pl.BlockSpec((1, SUBLANES, DN_CONV_W), seq_of),
        ],
        out_shape=[
            jax.ShapeDtypeStruct((tp + ts, PROJ_W), F32),
            jax.ShapeDtypeStruct((tp + ts, LANES), F32),
            jax.ShapeDtypeStruct((tp // seq, SUBLANES, DN_CONV_W), F32),
        ],
        scratch_shapes=[pltpu.VMEM((SUBLANES, DN_CONV_W), F32)],
        compiler_params=_cparams(("arbitrary",)),
        name="inproj",
    )(xp, xs, norm_w, w_big, w_small, conv_w, conv0)


def _dn_core(groups, alog, dtb, nw, read_state, write_state, n_seg, seg_valid):
    rows = groups[0][0].shape[0]
    sr = rows // n_seg
    assert sr * n_seg == rows and sr & (sr - 1) == 0 and rows <= LANES
    seg_shift = sr.bit_length() - 1
    ri = _row_iota((rows, rows))
    ci = _lane_iota((rows, rows))
    incl = ri >= ci
    strict = ri > ci
    if n_seg > 1:
        same = (ri >> seg_shift) == (ci >> seg_shift)
        incl = incl & same
        strict = strict & same
    l_incl = incl.astype(F32)
    eye = (ri == ci).astype(F32)
    levels = max(1, math.ceil(math.log2(seg_valid)))

    beta_all, gsum_all, gtot_all, gsum_t = [], [], [], []
    for _, _, ba, _ in groups:
        b_all = _sigmoid(ba)
        sp = ba + dtb
        softplus = jnp.maximum(sp, 0.0) + jnp.log1p(jnp.exp(-jnp.abs(sp)))
        g_all = -jnp.exp(alog) * softplus
        if seg_valid < sr:
            live = (_row_iota((rows, LANES)) & (sr - 1)) < seg_valid
            b_all = jnp.where(live, b_all, 0.0)
            g_all = jnp.where(live, g_all, 0.0)
        gs = _dot_exact(l_incl, g_all)
        beta_all.append(b_all)
        gsum_all.append(gs)
        gtot_all.append(_dot_exact(same.astype(F32), g_all) if n_seg > 1 else gs[rows - 1:rows, :])
        padded = gs if rows == LANES else jnp.concatenate([gs, jnp.zeros((LANES - rows, LANES), F32)], axis=0)
        gsum_t.append(padded.T)

    probs = [(g, h) for g in range(len(groups)) for h in range(DN_HEADS)]
    segs = range(n_seg)
    q, k, v, kb, beta, gsum, gtot = {}, {}, {}, {}, {}, {}, {}
    for p in probs:
        g, h = p
        qkv = groups[g][0]
        qh = qkv[:, h * DN_DK:(h + 1) * DN_DK]
        kh = qkv[:, DN_QK_W + h * DN_DK:DN_QK_W + (h + 1) * DN_DK]
        v[p] = qkv[:, 2 * DN_QK_W + h * DN_DV:2 * DN_QK_W + (h + 1) * DN_DV]
        q[p] = qh * lax.rsqrt(jnp.sum(qh * qh, axis=-1, keepdims=True) + 1e-6) * (DN_DK ** -0.5)
        k[p] = kh * lax.rsqrt(jnp.sum(kh * kh, axis=-1, keepdims=True) + 1e-6)
        beta[p] = _col(beta_all[g], h)
        gsum[p] = _col(gsum_all[g], DN_HEADS + h)
        gtot[p] = _col(gtot_all[g], DN_HEADS + h)
        kb[p] = k[p] * beta[p]
    kq = {p: _dot_nt(jnp.concatenate([kb[p], q[p]], axis=0), k[p]) for p in probs}
    gamma = {(g, h): jnp.exp(jnp.where(incl, gsum[(g, h)] - gsum_t[g][DN_HEADS + h:DN_HEADS + h + 1, :rows], -jnp.inf))
             for g, h in probs}
    attn = {p: kq[p][rows:] * gamma[p] for p in probs}
    pw = {p: -jnp.where(strict, kq[p][:rows] * gamma[p], 0.0) for p in probs}
    t = {p: eye + pw[p] for p in probs}
    for _ in range(1, levels):
        pw = {p: _dot(pw[p], pw[p]) for p in probs}
        t = {p: t[p] + _dot(t[p], pw[p]) for p in probs}
    eg = {p: jnp.exp(gsum[p]) for p in probs}
    uw = {p: _dot(t[p], jnp.concatenate([v[p] * beta[p], kb[p] * eg[p]], axis=1)) for p in probs}
    qg = {p: q[p] * eg[p] for p in probs}
    state = {(p, s): read_state(p[0], s, p[1]) for p in probs for s in segs}
    wq = {(p, s): _dot(jnp.concatenate([uw[p][s * sr:(s + 1) * sr, DN_DV:], qg[p][s * sr:(s + 1) * sr]], axis=0),
                       state[(p, s)]) for p in probs for s in segs}
    join = lambda pieces: pieces[0] if len(pieces) == 1 else jnp.concatenate(pieces, axis=0)
    v_new = {p: uw[p][:, :DN_DV] - join([wq[(p, s)][:sr] for s in segs]) for p in probs}
    o = {p: join([wq[(p, s)][sr:] for s in segs]) + _dot(attn[p], v_new[p]) for p in probs}
    kd = {p: k[p] * jnp.exp(gtot[p] - gsum[p]) for p in probs}
    for p in probs:
        for s in segs:
            r0 = s * sr if n_seg > 1 else 0
            decay = jnp.exp(gtot[p][r0:r0 + 1, :])
            write_state(p[0], s, p[1],
                        state[(p, s)] * decay + _dot_tn(kd[p][s * sr:(s + 1) * sr], v_new[p][s * sr:(s + 1) * sr]))
    outs = []
    for g, (_, z, _, gate) in enumerate(groups):
        heads = []
        for h in range(DN_HEADS):
            oh = o[(g, h)]
            zz = z[:, h * DN_DV:(h + 1) * DN_DV]
            on = oh * lax.rsqrt(jnp.mean(oh * oh, axis=-1, keepdims=True) + EPS) * nw
            heads.append(on * (zz * _sigmoid(zz)) * _sigmoid(gate[:, h * DN_DV:(h + 1) * DN_DV]))
        outs.append(jnp.concatenate(heads, axis=1))
    return outs


def _dn_prompt_kernel(*refs, chunk, n_batch):
    nb = n_batch
    qkv_refs, z_refs, ba_refs, gate_refs = refs[0:nb], refs[nb:2 * nb], refs[2 * nb:3 * nb], refs[3 * nb:4 * nb]
    alog_ref, dtb_ref, nw_ref, s0_ref, o_ref, sout_ref = refs[4 * nb:]

    @pl.when(pl.program_id(0) == 0)
    def _():
        sout_ref[...] = s0_ref[...]

    groups = [(qkv_refs[b][...], z_refs[b][...], ba_refs[b][...], gate_refs[b][...]) for b in range(nb)]

    def read_state(g, s, h):
        return sout_ref[g, h]

    def write_state(g, s, h, val):
        sout_ref[g, h] = val

    outs = _dn_core(groups, alog_ref[...], dtb_ref[...], nw_ref[...], read_state, write_state, 1, chunk)
    for b in range(nb):
        o_ref[b] = outs[b]


def _dn_prompt(proj, ba, alog_row, dtb_row, dn_nw, s0, n_batch, seq):
    chunk = min(DN_CHUNK, seq)
    assert seq % chunk == 0 and chunk % SUBLANES == 0
    nc = seq // chunk
    const2 = lambda c: (0, 0)
    rows = lambda b, col: (lambda c: (b * nc + c, col))
    batches = range(n_batch)
    o, s_out = pl.pallas_call(
        functools.partial(_dn_prompt_kernel, chunk=chunk, n_batch=n_batch),
        grid=(nc,),
        in_specs=(
            [pl.BlockSpec((chunk, DN_CONV_W), rows(b, COL_QKV // DN_CONV_W)) for b in batches]
            + [pl.BlockSpec((chunk, DN_V_W), rows(b, COL_Z // DN_V_W)) for b in batches]
            + [pl.BlockSpec((chunk, LANES), rows(b, 0)) for b in batches]
            + [pl.BlockSpec((chunk, D_MODEL), rows(b, COL_GA // D_MODEL)) for b in batches]
            + [
                pl.BlockSpec((1, LANES), const2),
                pl.BlockSpec((1, LANES), const2),
                pl.BlockSpec((1, DN_DV), const2),
                pl.BlockSpec((n_batch, DN_HEADS, DN_DK, DN_DV), lambda c: (0, 0, 0, 0)),
            ]
        ),
        out_specs=[
            pl.BlockSpec((n_batch, chunk, DN_V_W), lambda c: (0, c, 0)),
            pl.BlockSpec((n_batch, DN_HEADS, DN_DK, DN_DV), lambda c: (0, 0, 0, 0)),
        ],
        out_shape=[
            jax.ShapeDtypeStruct((n_batch, seq, DN_V_W), F32),
            jax.ShapeDtypeStruct((n_batch, DN_HEADS, DN_DK, DN_DV), F32),
        ],
        compiler_params=_cparams(("arbitrary",)),
        name="dn_prompt",
    )(*([proj] * n_batch), *([proj] * n_batch), *([ba] * n_batch), *([proj] * n_batch), alog_row, dtb_row, dn_nw, s0)
    return o.reshape(n_batch * seq, DN_V_W), s_out


def _spread_rows(ref, n_bb, seq):
    per = SUBLANES // seq
    pieces = []
    for j in range(n_bb // per):
        x8 = ref[j * SUBLANES:(j + 1) * SUBLANES, :]
        for r in range(per):
            pieces.append(x8 if r == 0 else pltpu.roll(x8, SUBLANES - r * seq, axis=0))
    return pieces


def _gather_rows(pieces, seq):
    per = SUBLANES // seq
    rows = _row_iota(pieces[0].shape)
    tiles = []
    for j in range(len(pieces) // per):
        tile = pieces[j * per]
        for r in range(1, per):
            tile = jnp.where(rows >= r * seq, pltpu.roll(pieces[j * per + r], r * seq, axis=0), tile)
        tiles.append(tile)
    return jnp.concatenate(tiles, axis=0)


def _dn_sample_kernel(qkv_ref, z_ref, ba_ref, gate_ref, convw_ref, alog_ref, dtb_ref, nw_ref, conv0_ref, s0_ref,
                      o_ref, convout_ref, sout_ref, cbuf_ref, *, seq, n_bb):
    hist = SUBLANES - (DN_CONV - 1)
    spread = lambda ref: _spread_rows(ref, n_bb, seq)
    for bb, piece in enumerate(spread(qkv_ref)):
        cbuf_ref[bb, SUBLANES:2 * SUBLANES, :] = piece
    cbuf_ref[:, hist:SUBLANES, :] = conv0_ref[...]
    w = convw_ref[...]
    acc = cbuf_ref[:, hist:hist + SUBLANES, :] * w[0:1, :]
    for i in range(1, DN_CONV):
        acc = acc + cbuf_ref[:, hist + i:hist + i + SUBLANES, :] * w[i:i + 1, :]
    live = _row_iota(acc.shape) < seq
    qkv = jnp.where(live, acc * _sigmoid(acc), 0.0).reshape(n_bb * SUBLANES, DN_CONV_W)
    convout_ref[...] = cbuf_ref[:, SUBLANES + seq - (DN_CONV - 1):SUBLANES + seq, :]

    def read_state(g, s, h):
        return s0_ref[s, h]

    def write_state(g, s, h, val):
        sout_ref[s, h] = val

    group = (qkv, jnp.concatenate(spread(z_ref), axis=0), jnp.concatenate(spread(ba_ref), axis=0),
             jnp.concatenate(spread(gate_ref), axis=0))
    o = _dn_core([group], alog_ref[...], dtb_ref[...], nw_ref[...], read_state, write_state, n_bb, seq)[0]
    o_ref[...] = _gather_rows([o[bb * SUBLANES:(bb + 1) * SUBLANES] for bb in range(n_bb)], seq)


def _dn_sample(proj, ba, conv_w, alog_row, dtb_row, dn_nw, conv0, s0, row0, n_batch, seq):
    assert SUBLANES % seq == 0 and seq >= DN_CONV - 1
    n_bb = SUBLANES
    rows_in = n_bb * seq
    assert n_batch % n_bb == 0 and row0 % rows_in == 0
    rb0 = row0 // rows_in
    const1 = lambda i: (0, 0)
    return pl.pallas_call(
        functools.partial(_dn_sample_kernel, seq=seq, n_bb=n_bb),
        grid=(n_batch // n_bb,),
        in_specs=[
            pl.BlockSpec((rows_in, DN_CONV_W), lambda i: (rb0 + i, COL_QKV // DN_CONV_W)),
            pl.BlockSpec((rows_in, DN_V_W), lambda i: (rb0 + i, COL_Z // DN_V_W)),
            pl.BlockSpec((rows_in, LANES), lambda i: (rb0 + i, 0)),
            pl.BlockSpec((rows_in, D_MODEL), lambda i: (rb0 + i, COL_GA // D_MODEL)),
            pl.BlockSpec((DN_CONV, DN_CONV_W), const1),
            pl.BlockSpec((1, LANES), const1),
            pl.BlockSpec((1, LANES), const1),
            pl.BlockSpec((1, DN_DV), const1),
            pl.BlockSpec((n_bb, DN_CONV - 1, DN_CONV_W), lambda i: (i, 0, 0)),
            pl.BlockSpec((n_bb, DN_HEADS, DN_DK, DN_DV), lambda i: (i, 0, 0, 0)),
        ],
        out_specs=[
            pl.BlockSpec((rows_in, DN_V_W), lambda i: (i, 0)),
            pl.BlockSpec((n_bb, DN_CONV - 1, DN_CONV_W), lambda i: (i, 0, 0)),
            pl.BlockSpec((n_bb, DN_HEADS, DN_DK, DN_DV), lambda i: (i, 0, 0, 0)),
        ],
        out_shape=[
            jax.ShapeDtypeStruct((n_batch * seq, DN_V_W), F32),
            jax.ShapeDtypeStruct((n_batch, DN_CONV - 1, DN_CONV_W), F32),
            jax.ShapeDtypeStruct((n_batch, DN_HEADS, DN_DK, DN_DV), F32),
        ],
        scratch_shapes=[pltpu.VMEM((n_bb, 2 * SUBLANES, DN_CONV_W), F32)],
        compiler_params=_cparams(("arbitrary",)),
        name="dn_sample",
    )(proj, proj, ba, proj, conv_w, alog_row, dtb_row, dn_nw, conv0, s0)


def _masked_bucket(dist, ok):
    n = np.maximum(dist, 0)
    max_exact = REL_BUCKETS // 2
    large = max_exact + (np.log(np.maximum(n, 1).astype(np.float32) / max_exact)
                         / math.log(REL_MAX_DIST / max_exact) * (REL_BUCKETS - max_exact)).astype(np.int32)
    bucket = np.where(n < max_exact, n, np.minimum(large, REL_BUCKETS - 1))
    return np.where(ok, bucket, -1).astype(np.int32)


def _relbias_kernel(tab_ref, bucket_ref, o_ref):
    h = pl.program_id(0)
    bk = bucket_ref[...]
    acc = jnp.full(bk.shape, -jnp.inf, F32)
    for b in range(REL_BUCKETS):
        acc = jnp.where(bk == b, tab_ref[b * SW_HEADS + h], acc)
    o_ref[0] = acc


def _relbias(rel_table, bucket):
    nq, ns = bucket.shape
    return pl.pallas_call(
        _relbias_kernel,
        grid=(SW_HEADS,),
        in_specs=[
            pl.BlockSpec(memory_space=pltpu.SMEM),
            pl.BlockSpec((nq, ns), lambda h: (0, 0)),
        ],
        out_specs=pl.BlockSpec((1, nq, ns), lambda h: (h, 0, 0)),
        out_shape=jax.ShapeDtypeStruct((SW_HEADS, nq, ns), F32),
        compiler_params=_cparams(("arbitrary",)),
        name="relbias",
    )(rel_table.reshape(-1), bucket)


def _dup_halves(x):
    lo = _lane_iota(x.shape) < SW_HD
    xr = pltpu.roll(x, SW_HD, axis=1)
    return jnp.where(lo, x, xr).astype(BF16), jnp.where(lo, xr, x).astype(BF16)


def _sink_softmax_pv(s, sink, vv):
    m = jnp.maximum(jnp.max(s, axis=-1, keepdims=True), sink)
    p = jnp.exp(s - m)
    denom = jnp.sum(p, axis=-1, keepdims=True) + jnp.exp(sink - m)
    return _dot(p, vv) * (1.0 / denom)


def _swa_prompt_kernel(sinks_ref, q_ref, kc_ref, kp_ref, vc_ref, vp_ref, bias_ref, gate_ref, other_ref,
                       o_ref, klast_ref, vlast_ref, *, n_sub):
    i = pl.program_id(1)
    klast_ref[0] = kc_ref[(n_sub - 1) * SW_BLOCK:, :]
    vlast_ref[0] = vc_ref[(n_sub - 1) * SW_BLOCK:, :]
    kk_all = _dup_halves(jnp.concatenate([kp_ref[...], kc_ref[...]], axis=0))
    vv_all = _dup_halves(jnp.concatenate([vp_ref[...], vc_ref[...]], axis=0))
    lo = _lane_iota((SW_BLOCK, LANES)) < SW_HD
    for sub in range(n_sub):
        rows = slice(sub * SW_BLOCK, (sub + 1) * SW_BLOCK)
        keys = slice(sub * SW_BLOCK, (sub + 2) * SW_BLOCK)
        bias_rows = pl.ds(pl.multiple_of(jnp.minimum(i, 1) * SW_BLOCK, SW_BLOCK), SW_BLOCK) if sub == 0 \
            else pl.ds(SW_BLOCK, SW_BLOCK)
        for pair in range(SW_HEADS // 2):
            cols = slice(pair * LANES, (pair + 1) * LANES)
            qp = q_ref[rows, cols] * (SW_HD ** -0.5)
            outs = []
            for half in range(2):
                hq = 2 * pair + half
                kv = hq // SW_GROUP
                qm = jnp.where(lo if half == 0 else ~lo, qp, 0.0)
                s = _dot_nt(qm, kk_all[kv][keys]) + bias_ref[hq, bias_rows, :]
                outs.append(_sink_softmax_pv(s, sinks_ref[hq], vv_all[kv][keys]))
            o_ref[rows, cols] = other_ref[rows, cols] + _sigmoid(gate_ref[rows, cols]) * jnp.where(lo, outs[0], outs[1])


def _swa_prompt(proj, sinks, bias, other, n_batch, seq):
    assert seq % SW_BLOCK == 0 and WINDOW == SW_BLOCK
    nb = seq // SW_BLOCK
    n_sub = next(n for n in (4, 2, 1) if nb % n == 0)
    ns = nb // n_sub
    rows = n_sub * SW_BLOCK
    cur = lambda col: (lambda b, i: (b * ns + i, col))
    prev = lambda col: (lambda b, i: (b * nb + jnp.maximum(i * n_sub - 1, 0), col))
    return pl.pallas_call(
        functools.partial(_swa_prompt_kernel, n_sub=n_sub),
        grid=(n_batch, ns),
        in_specs=[
            pl.BlockSpec(memory_space=pltpu.SMEM),
            pl.BlockSpec((rows, SW_HEADS * SW_HD), cur(COL_SQ // (SW_HEADS * SW_HD))),
            pl.BlockSpec((rows, SW_KV_W), cur(COL_SK // SW_KV_W)),
            pl.BlockSpec((SW_BLOCK, SW_KV_W), prev(COL_SK // SW_KV_W)),
            pl.BlockSpec((rows, SW_KV_W), cur(COL_SV // SW_KV_W)),
            pl.BlockSpec((SW_BLOCK, SW_KV_W), prev(COL_SV // SW_KV_W)),
            pl.BlockSpec((SW_HEADS, 2 * SW_BLOCK, 2 * SW_BLOCK), lambda b, i: (0, 0, 0)),
            pl.BlockSpec((rows, D_MODEL), cur(COL_GB // D_MODEL)),
            pl.BlockSpec((rows, D_MODEL), lambda b, i: (b * ns + i, 0)),
        ],
        out_specs=[
            pl.BlockSpec((rows, SW_HEADS * SW_HD), lambda b, i: (b * ns + i, 0)),
            pl.BlockSpec((1, SW_BLOCK, SW_KV_W), lambda b, i: (b, 0, 0)),
            pl.BlockSpec((1, SW_BLOCK, SW_KV_W), lambda b, i: (b, 0, 0)),
        ],
        out_shape=[
            jax.ShapeDtypeStruct((n_batch * seq, SW_HEADS * SW_HD), F32),
            jax.ShapeDtypeStruct((n_batch, SW_BLOCK, SW_KV_W), F32),
            jax.ShapeDtypeStruct((n_batch, SW_BLOCK, SW_KV_W), F32),
        ],
        compiler_params=_cparams(("arbitrary", "arbitrary")),
        name="swa_prompt",
    )(sinks, proj, proj, proj, proj, proj, bias, proj, other)


def _swa_sample_kernel(q_ref, kn_ref, vn_ref, kc_ref, vc_ref, bias_ref, sink_ref, gate_ref, other_ref,
                       o_ref, ko_ref, vo_ref, kall_ref, vall_ref, *, seq, n_bb, n_cache):
    n_keys = kall_ref.shape[1]
    zeros_tail = jnp.zeros((n_bb, n_keys - n_cache - SUBLANES, LANES), F32)
    lo = _lane_iota((SUBLANES, LANES)) < SW_HD
    for ref, cache_ref, new_ref in ((kall_ref, kc_ref, kn_ref), (vall_ref, vc_ref, vn_ref)):
        ref[:, 0:n_cache, :] = cache_ref[...]
        for bb, piece in enumerate(_spread_rows(new_ref, n_bb, seq)):
            ref[bb, n_cache:n_cache + SUBLANES, :] = piece
        ref[:, n_cache + SUBLANES:, :] = zeros_tail
    ko_ref[...] = kall_ref[:, seq:seq + n_cache, :]
    vo_ref[...] = vall_ref[:, seq:seq + n_cache, :]
    outs = []
    for bb, q8 in enumerate(_spread_rows(q_ref, n_bb, seq)):
        kk = _dup_halves(kall_ref[bb])
        vv = _dup_halves(vall_ref[bb])
        pairs = []
        for kv in range(SW_KV_HEADS):
            pieces = []
            for g in range(SW_GROUP):
                hq = kv * SW_GROUP + g
                qp = q8[:, (hq // 2) * LANES:(hq // 2 + 1) * LANES]
                pieces.append(jnp.where(lo if hq % 2 == 0 else ~lo, qp, 0.0))
            qs = jnp.concatenate(pieces, axis=0) * (SW_HD ** -0.5)
            bias = jnp.concatenate([bias_ref[kv * SW_GROUP + g] for g in range(SW_GROUP)], axis=0)
            s = _dot_nt(qs, kk[kv]) + bias
            res = _sink_softmax_pv(s, _col(sink_ref[kv], 0), vv[kv])
            for g in range(0, SW_GROUP, 2):
                pairs.append(jnp.where(lo, res[g * SUBLANES:(g + 1) * SUBLANES],
                                       res[(g + 1) * SUBLANES:(g + 2) * SUBLANES]))
        outs.append(jnp.concatenate(pairs, axis=1))
    o_ref[...] = other_ref[...] + _sigmoid(gate_ref[...]) * _gather_rows(outs, seq)


def _swa_sample(proj, k_cache, v_cache, bias, bias_row0, sink_rows, other, row0, n_batch, seq):
    assert SUBLANES % seq == 0
    n_bb = SUBLANES
    rows_in = n_bb * seq
    n_cache = k_cache.shape[1]
    assert n_batch % n_bb == 0 and row0 % rows_in == 0 and n_cache % SUBLANES == 0
    n_keys = bias.shape[-1]
    rb0 = row0 // rows_in
    blk = lambda col: (lambda i: (rb0 + i, col))
    return pl.pallas_call(
        functools.partial(_swa_sample_kernel, seq=seq, n_bb=n_bb, n_cache=n_cache),
        grid=(n_batch // n_bb,),
        in_specs=[
            pl.BlockSpec((rows_in, SW_HEADS * SW_HD), blk(COL_SQ // (SW_HEADS * SW_HD))),
            pl.BlockSpec((rows_in, SW_KV_W), blk(COL_SK // SW_KV_W)),
            pl.BlockSpec((rows_in, SW_KV_W), blk(COL_SV // SW_KV_W)),
            pl.BlockSpec((n_bb, n_cache, SW_KV_W), lambda i: (i, 0, 0)),
            pl.BlockSpec((n_bb, n_cache, SW_KV_W), lambda i: (i, 0, 0)),
            pl.BlockSpec((SW_HEADS, SUBLANES, n_keys), lambda i: (0, bias_row0 // SUBLANES, 0)),
            pl.BlockSpec((SW_KV_HEADS, SW_GROUP * SUBLANES, n_keys), lambda i: (0, 0, 0)),
            pl.BlockSpec((rows_in, D_MODEL), blk(COL_GB // D_MODEL)),
            pl.BlockSpec((rows_in, D_MODEL), lambda i: (i, 0)),
        ],
        out_specs=[
            pl.BlockSpec((rows_in, SW_HEADS * SW_HD), lambda i: (i, 0)),
            pl.BlockSpec((n_bb, n_cache, SW_KV_W), lambda i: (i, 0, 0)),
            pl.BlockSpec((n_bb, n_cache, SW_KV_W), lambda i: (i, 0, 0)),
        ],
        out_shape=[
            jax.ShapeDtypeStruct((n_batch * seq, SW_HEADS * SW_HD), F32),
            jax.ShapeDtypeStruct(k_cache.shape, F32),
            jax.ShapeDtypeStruct(v_cache.shape, F32),
        ],
        scratch_shapes=[pltpu.VMEM((n_bb, n_keys, SW_KV_W), F32), pltpu.VMEM((n_bb, n_keys, SW_KV_W), F32)],
        compiler_params=_cparams(("arbitrary",)),
        name="swa_sample",
    )(proj, proj, proj, k_cache, v_cache, bias, sink_rows, proj, other)


def _mix_kernel(xp_ref, xs_ref, mp_ref, ms_ref, wo_ref, nw_ref, wr_ref, br_ref,
                x1_ref, h2_ref, route_ref, *, n_prompt_tiles):
    i = pl.program_id(0)

    def run(x_ref, mixed_ref):
        x1 = x_ref[...] + _dot(mixed_ref[...], wo_ref[...])
        x1_ref[...] = x1
        h2 = x1 * lax.rsqrt(jnp.mean(x1 * x1, axis=-1, keepdims=True) + EPS) * nw_ref[...]
        h2_ref[...] = h2
        logits = _dot(h2, wr_ref[...]) + br_ref[...]
        lane = _lane_iota(logits.shape)
        lanef = lane.astype(F32)
        big = float(2 * LANES)
        is_g = lane < N_GROUPS
        gl = jnp.where(is_g, logits, -jnp.inf)
        gmax = jnp.max(gl, axis=-1, keepdims=True)
        gval = 1.0 / jnp.sum(jnp.where(is_g, jnp.exp(gl - gmax), 0.0), axis=-1, keepdims=True)
        grp = jnp.min(jnp.where(gl == gmax, lanef, big), axis=-1, keepdims=True)
        e_grp = ((lane - N_GROUPS) >> 3).astype(F32)
        is_e = (lane >= N_GROUPS) & (lane < N_GROUPS + N_EXPERTS) & (e_grp == grp)
        el = jnp.where(is_e, logits, -jnp.inf)
        v1 = jnp.max(el, axis=-1, keepdims=True)
        i1 = jnp.min(jnp.where(el == v1, lanef, big), axis=-1, keepdims=True)
        el2 = jnp.where(lanef == i1, -jnp.inf, el)
        v2 = jnp.max(el2, axis=-1, keepdims=True)
        i2 = jnp.min(jnp.where(el2 == v2, lanef, big), axis=-1, keepdims=True)
        e2 = jnp.exp(v2 - v1)
        w1 = gval / (1.0 + e2)
        w2 = gval * e2 / (1.0 + e2)
        route_ref[...] = jnp.where(lane == 0, i1 - N_GROUPS,
                                   jnp.where(lane == 1, i2 - N_GROUPS,
                                             jnp.where(lane == 2, w1, jnp.where(lane == 3, w2, 0.0))))

    @pl.when(i < n_prompt_tiles)
    def _():
        run(xp_ref, mp_ref)

    @pl.when(i >= n_prompt_tiles)
    def _():
        run(xs_ref, ms_ref)


def _mix(xp, xs, mixed_p, mixed_s, w_out, norm_w, w_router, b_router):
    tp, ts = xp.shape[0], xs.shape[0]
    tm = _token_tile(tp, ts, cands=BIG_TILES)
    npt, nst = tp // tm, ts // tm
    const = lambda i: (0, 0)
    row = lambda i: (i, 0)
    return pl.pallas_call(
        functools.partial(_mix_kernel, n_prompt_tiles=npt),
        grid=(npt + nst,),
        in_specs=[
            pl.BlockSpec((tm, D_MODEL), lambda i: (jnp.minimum(i, npt - 1), 0)),
            pl.BlockSpec((tm, D_MODEL), lambda i: (jnp.maximum(i - npt, 0), 0)),
            pl.BlockSpec((tm, D_MODEL), lambda i: (jnp.minimum(i, npt - 1), 0)),
            pl.BlockSpec((tm, D_MODEL), lambda i: (jnp.maximum(i - npt, 0), 0)),
            pl.BlockSpec((D_MODEL, D_MODEL), const),
            pl.BlockSpec((1, D_MODEL), const),
            pl.BlockSpec((D_MODEL, LANES), const),
            pl.BlockSpec((1, LANES), const),
        ],
        out_specs=[
            pl.BlockSpec((tm, D_MODEL), row),
            pl.BlockSpec((tm, D_MODEL), row),
            pl.BlockSpec((tm, LANES), row),
        ],
        out_shape=[
            jax.ShapeDtypeStruct((tp + ts, D_MODEL), F32),
            jax.ShapeDtypeStruct((tp + ts, D_MODEL), F32),
            jax.ShapeDtypeStruct((tp + ts, LANES), F32),
        ],
        compiler_params=_cparams(("arbitrary",)),
        name="mix_router",
    )(xp, xs, mixed_p, mixed_s, w_out, norm_w, w_router, b_router)


def _rank_kernel(route_ref, dest_ref, meta_ref, rank_ref, cnt_ref, *, tile, blk):
    phase = pl.program_id(0)
    i = pl.program_id(1)
    shape = (tile, LANES)
    lane = _lane_iota(shape)
    lanef = lane.astype(F32)
    r = route_ref[...]
    oh0 = lanef == _col(r, 0)
    oh1 = lanef == _col(r, 1)
    rows = pl.ds(pl.multiple_of(i * tile, tile), tile)

    @pl.when(phase == 0)
    def _():
        @pl.when(i == 0)
        def _():
            cnt_ref[...] = jnp.zeros(cnt_ref.shape, F32)

        oh = jnp.where(oh0 | oh1, 1.0, 0.0)
        tri = jnp.where(_row_iota((tile, tile)) > _lane_iota((tile, tile)), 1.0, 0.0)
        before = _dot(tri, oh) + cnt_ref[0:1, :]
        rank0 = jnp.sum(jnp.where(oh0, before, 0.0), axis=-1, keepdims=True)
        rank1 = jnp.sum(jnp.where(oh1, before, 0.0), axis=-1, keepdims=True)
        rank_ref[rows, :] = jnp.where(lane == 0, rank0, jnp.where(lane == 1, rank1, 0.0))
        cnt_ref[0:1, :] = cnt_ref[0:1, :] + jnp.sum(oh, axis=0, keepdims=True)

    @pl.when(phase == 1)
    def _():
        cnt = cnt_ref[0:1, :]
        padded = jnp.floor((cnt + (blk - 1)) / blk) * blk
        before_lane = jnp.where(_row_iota((LANES, LANES)) < _lane_iota((LANES, LANES)), 1.0, 0.0)
        start = _dot_exact(jnp.broadcast_to(padded, (SUBLANES, LANES)), before_lane)[0:1, :]
        rk = rank_ref[rows, :]
        d0 = jnp.sum(jnp.where(oh0, start, 0.0), axis=-1, keepdims=True) + _col(rk, 0)
        d1 = jnp.sum(jnp.where(oh1, start, 0.0), axis=-1, keepdims=True) + _col(rk, 1)
        dest_ref[...] = jnp.where(lane == 0, d0, jnp.where(lane == 1, d1, 0.0)).astype(I32)

        @pl.when(i == 0)
        def _():
            end = start + padded
            mshape = meta_ref.shape
            blk_start = (_row_iota(mshape) * blk).astype(F32)
            hit = (_lane_iota(mshape) < N_EXPERTS) & (end <= blk_start)
            be = jnp.minimum(jnp.sum(jnp.where(hit, 1.0, 0.0), axis=-1, keepdims=True), N_EXPERTS - 1.0)
            n_used = _col(end, N_EXPERTS - 1) / blk
            ml = _lane_iota(mshape)
            mine = ml.astype(F32) == be
            seg_start = jnp.sum(jnp.where(mine, start, 0.0), axis=-1, keepdims=True)
            seg_count = jnp.sum(jnp.where(mine, cnt, 0.0), axis=-1, keepdims=True)
            n_valid = jnp.clip(seg_count - (blk_start[:, 0:1] - seg_start), 0.0, float(blk))
            meta_ref[...] = jnp.where(ml == 0, be, jnp.where(ml == 1, n_used,
                                                              jnp.where(ml == 2, n_valid, 0.0))).astype(I32)


def _rank(route, tile, blk, n_blocks):
    t = route.shape[0]
    nt = t // tile
    nbp = -(-n_blocks // SUBLANES) * SUBLANES
    return pl.pallas_call(
        functools.partial(_rank_kernel, tile=tile, blk=blk),
        grid=(2, nt),
        in_specs=[pl.BlockSpec((tile, LANES), lambda p, i: (i, 0))],
        out_specs=[
            pl.BlockSpec((tile, LANES), lambda p, i: (i * p, 0)),
            pl.BlockSpec((nbp, LANES), lambda p, i: (0, 0)),
        ],
        out_shape=[
            jax.ShapeDtypeStruct((t, LANES), I32),
            jax.ShapeDtypeStruct((nbp, LANES), I32),
        ],
        scratch_shapes=[pltpu.VMEM((t, LANES), F32), pltpu.VMEM((SUBLANES, LANES), F32)],
        compiler_params=_cparams(("arbitrary", "arbitrary")),
        name="moe_rank",
    )(route)


def _row_copy(src, src_row, dst, dst_row, sem):
    return pltpu.make_async_copy(src.at[pl.ds(src_row, 1)], dst.at[pl.ds(dst_row, 1)], sem)


def _last_used(i, nu_ref):
    return jnp.minimum(i, jnp.maximum(nu_ref[0] - 1, 0))


def _for_rows(n, body):
    def group(g, carry):
        for u in range(SUBLANES):
            body(g * SUBLANES + u, u)
        return carry

    def single(t, carry):
        body(t, 0)
        return carry

    n_groups = n // SUBLANES
    lax.fori_loop(0, n_groups, group, 0)
    lax.fori_loop(n_groups * SUBLANES, n, single, 0)


def _expert_kernel(be_ref, nu_ref, nv_ref, dest_ref, h2_ref, wg_ref, wu_ref, wd_ref, y_ref,
                   xbuf_ref, wgu_ref, wdn_ref, inv_ref, sem, *, blk):
    i = pl.program_id(0)
    n_used = nu_ref[0]
    used = i < n_used
    slot = i % 2
    blk_i = _last_used(i, nu_ref)
    fresh = (i == 0) | (be_ref[blk_i] != be_ref[jnp.maximum(blk_i - 1, 0)])

    def gather(block, into):
        _for_rows(nv_ref[block], lambda t, u: _row_copy(h2_ref, inv_ref[block * blk + t], xbuf_ref.at[into], t,
                                                       sem.at[into]).start(priority=u % 2))

    @pl.when(i == 0)
    def _():
        def place(a, carry):
            inv_ref[dest_ref[a]] = a >> 1
            return carry

        lax.fori_loop(0, dest_ref.shape[0], place, 0, unroll=16)
        xbuf_ref[...] = jnp.zeros(xbuf_ref.shape, F32)
        gather(0, 0)

    @pl.when(i + 1 < n_used)
    def _():
        gather(i + 1, 1 - slot)

    @pl.when(used & fresh)
    def _():
        wgu_ref[:, :D_EXPERT] = wg_ref[0].astype(BF16)
        wgu_ref[:, D_EXPERT:] = wu_ref[0].astype(BF16)
        wdn_ref[...] = wd_ref[0].astype(BF16)

    @pl.when(used)
    def _():
        _for_rows(nv_ref[i], lambda t, u: _row_copy(h2_ref, 0, xbuf_ref.at[slot], 0, sem.at[slot]).wait())
        gu = jnp.dot(xbuf_ref[slot].astype(BF16), wgu_ref[...], preferred_element_type=F32)
        g = gu[:, :D_EXPERT]
        hidden = (g * _sigmoid(g) * gu[:, D_EXPERT:]).astype(BF16)
        y_ref[...] = jnp.dot(hidden, wdn_ref[...], preferred_element_type=F32)

    @pl.when(jnp.logical_not(used))
    def _():
        y_ref[...] = jnp.zeros(y_ref.shape, F32)


def _experts(block_expert, n_used, n_valid, dest_flat, h2, w_gate, w_up, w_down, blk):
    n_blocks = block_expert.shape[0]
    wsel = lambda i, be, nu, nv, de: (be[_last_used(i, nu)], 0, 0)
    return pl.pallas_call(
        functools.partial(_expert_kernel, blk=blk),
        grid_spec=pltpu.PrefetchScalarGridSpec(
            num_scalar_prefetch=4,
            grid=(n_blocks,),
            in_specs=[
                pl.BlockSpec(memory_space=pl.ANY),
                pl.BlockSpec((1, D_MODEL, D_EXPERT), wsel),
                pl.BlockSpec((1, D_MODEL, D_EXPERT), wsel),
                pl.BlockSpec((1, D_EXPERT, D_MODEL), wsel),
            ],
            out_specs=pl.BlockSpec((blk, D_MODEL), lambda i, be, nu, nv, de: (i, 0)),
            scratch_shapes=[
                pltpu.VMEM((2, blk, D_MODEL), F32),
                pltpu.VMEM((D_MODEL, 2 * D_EXPERT), BF16),
                pltpu.VMEM((D_EXPERT, D_MODEL), BF16),
                pltpu.SMEM((n_blocks * blk,), I32),
                pltpu.SemaphoreType.DMA((2,)),
            ],
        ),
        out_shape=jax.ShapeDtypeStruct((n_blocks * blk, D_MODEL), F32),
        compiler_params=_cparams(("arbitrary",)),
        name="moe_experts",
    )(block_expert, n_used, n_valid, dest_flat, h2, w_gate, w_up, w_down)


def _combine_kernel(dest_ref, x1_ref, route_ref, nw_ref, ys_ref, yp_ref, ysm_ref, ybuf_ref, sem,
                    *, tile, n_prompt_tiles):
    i = pl.program_id(0)
    slot = i % 2

    def gather(step, into):
        def issue(t, carry):
            for k in range(2):
                _row_copy(ys_ref, dest_ref[2 * (step * tile + t) + k], ybuf_ref.at[into], k * tile + t,
                          sem.at[into]).start(priority=k)
            return carry

        lax.fori_loop(0, tile, issue, 0, unroll=4)

    @pl.when(i == 0)
    def _():
        gather(0, 0)

    @pl.when(i + 1 < pl.num_programs(0))
    def _():
        gather(i + 1, 1 - slot)

    def drain(t, carry):
        _row_copy(ys_ref, 0, ybuf_ref.at[slot], 0, sem.at[slot]).wait()
        return carry

    lax.fori_loop(0, 2 * tile, drain, 0, unroll=8)
    r = route_ref[...]
    ybuf = ybuf_ref.at[slot]
    y = ybuf[0:tile, :] * _col(r, 2) + ybuf[tile:2 * tile, :] * _col(r, 3)
    x2 = x1_ref[...] + y
    out = x2 * lax.rsqrt(jnp.mean(x2 * x2, axis=-1, keepdims=True) + EPS) * nw_ref[...]

    @pl.when(i < n_prompt_tiles)
    def _():
        yp_ref[...] = out

    @pl.when(i >= n_prompt_tiles)
    def _():
        ysm_ref[...] = out


def _combine(dest_flat, x1, route, norm_w, ys, tp, ts):
    tile = _token_tile(tp, ts, cands=BIG_TILES)
    npt, nst = tp // tile, ts // tile
    return pl.pallas_call(
        functools.partial(_combine_kernel, tile=tile, n_prompt_tiles=npt),
        grid_spec=pltpu.PrefetchScalarGridSpec(
            num_scalar_prefetch=1,
            grid=(npt + nst,),
            in_specs=[
                pl.BlockSpec((tile, D_MODEL), lambda i, d: (i, 0)),
                pl.BlockSpec((tile, LANES), lambda i, d: (i, 0)),
                pl.BlockSpec((1, D_MODEL), lambda i, d: (0, 0)),
                pl.BlockSpec(memory_space=pl.ANY),
            ],
            out_specs=[
                pl.BlockSpec((tile, D_MODEL), lambda i, d: (jnp.minimum(i, npt - 1), 0)),
                pl.BlockSpec((tile, D_MODEL), lambda i, d: (jnp.maximum(i - npt, 0), 0)),
            ],
            scratch_shapes=[pltpu.VMEM((2, 2 * tile, D_MODEL), F32), pltpu.SemaphoreType.DMA((2,))],
        ),
        out_shape=[
            jax.ShapeDtypeStruct((tp, D_MODEL), F32),
            jax.ShapeDtypeStruct((ts, D_MODEL), F32),
        ],
        compiler_params=_cparams(("arbitrary",)),
        name="moe_combine",
    )(dest_flat, x1, route, norm_w, ys)


def _layer(xp, xs, n_batch, seq, s_batch, s_seq, conv_state, dn_state, k_cache, v_cache,
           w_in, conv_w, a_log, dt_bias, dn_norm_w, sinks, rel_bias, w_out, norm_mix_w, norm_ffn_w,
           w_rg, b_rg, w_re, b_re, w_gate, w_up, w_down, norm_final_w):
    tp, ts = xp.shape[0], xs.shape[0]
    t_all = tp + ts
    row = lambda v: v.reshape(1, -1).astype(F32)

    o = np.cumsum((0, DN_QK_W, DN_QK_W, DN_V_W, DN_V_W, DN_HEADS, DN_HEADS, SW_HEADS * SW_HD, SW_KV_W, SW_KV_W,
                   D_MODEL, D_MODEL)).tolist()
    w_big = jnp.concatenate([w_in[:, o[0]:o[4]], w_in[:, o[6]:o[7]], w_in[:, o[9]:o[11]], w_in[:, o[7]:o[9]]],
                            axis=1).astype(BF16)
    w_small = jnp.pad(w_in[:, o[4]:o[6]], ((0, 0), (0, LANES - 2 * DN_HEADS))).astype(BF16)
    head_row = lambda v: jnp.pad(v.astype(F32), (DN_HEADS, LANES - 2 * DN_HEADS)).reshape(1, LANES)
    w_router = jnp.pad(jnp.concatenate([w_rg, w_re], axis=1),
                       ((0, 0), (0, LANES - N_GROUPS - N_EXPERTS))).astype(BF16)
    b_router = jnp.pad(jnp.concatenate([b_rg, b_re]).astype(F32), (0, LANES - N_GROUPS - N_EXPERTS)).reshape(1, LANES)

    conv0 = jnp.zeros((n_batch, DN_CONV - 1, DN_CONV_W), F32)
    proj, ba, conv_tail = _inproj(xp, xs, row(norm_mix_w), w_big, w_small, conv_w.astype(F32), conv0, seq)
    p_conv = conv_tail[:, SUBLANES - (DN_CONV - 1):, :]

    dn0 = jnp.zeros((n_batch, DN_HEADS, DN_DK, DN_DV), F32)
    oa_p, p_dn = _dn_prompt(proj, ba, head_row(a_log), head_row(dt_bias), row(dn_norm_w), dn0, n_batch, seq)
    oa_s, s_conv, s_dn = _dn_sample(proj, ba, conv_w.astype(F32), head_row(a_log), head_row(dt_bias), row(dn_norm_w),
                                    conv_state, dn_state, tp, s_batch, s_seq)

    n_cache = k_cache.shape[1]
    n_keys = -(-(n_cache + SUBLANES) // LANES) * LANES
    assert n_keys == 2 * SW_BLOCK, "the prompt and sample bias tables share one (rows, keys) array"
    dist_p = np.arange(SW_BLOCK)[:, None] - (np.arange(2 * SW_BLOCK)[None, :] - SW_BLOCK)
    dist_s = n_cache + np.arange(SUBLANES)[:, None] - np.arange(n_keys)[None, :]
    in_window = lambda dist: (dist >= 0) & (dist < WINDOW)
    has_prev = np.arange(2 * SW_BLOCK)[None, :] >= SW_BLOCK
    buckets = np.concatenate([_masked_bucket(dist_p, in_window(dist_p) & has_prev),
                              _masked_bucket(dist_p, in_window(dist_p)),
                              _masked_bucket(dist_s, in_window(dist_s))], axis=0)
    bias = _relbias(rel_bias.astype(F32), jnp.asarray(buckets))
    mixed_p, p_k, p_v = _swa_prompt(proj, sinks.astype(F32), bias, oa_p, n_batch, seq)
    sink_rows = jnp.broadcast_to(jnp.repeat(sinks.astype(F32).reshape(SW_KV_HEADS, SW_GROUP), SUBLANES, axis=1)[:, :, None],
                                 (SW_KV_HEADS, SW_GROUP * SUBLANES, n_keys))
    mixed_s, s_k, s_v = _swa_sample(proj, k_cache.reshape(s_batch, n_cache, SW_KV_W),
                                    v_cache.reshape(s_batch, n_cache, SW_KV_W), bias, 2 * SW_BLOCK, sink_rows, oa_s,
                                    tp, s_batch, s_seq)

    x1, h2, route = _mix(xp, xs, mixed_p, mixed_s, w_out.astype(BF16), row(norm_ffn_w), w_router, b_router)

    n_blocks = -(-2 * t_all // MOE_BLOCK) + N_EXPERTS
    dest, meta = _rank(route, _token_tile(t_all, cands=BIG_TILES), MOE_BLOCK, n_blocks)
    dest_flat = dest[:, :2].reshape(-1)
    block_expert = meta[:n_blocks, 0]
    n_used = meta[0:1, 1]
    n_valid = meta[:n_blocks, 2]
    ys = _experts(block_expert, n_used, n_valid, dest_flat, h2, w_gate, w_up, w_down, MOE_BLOCK)
    y_p, y_s = _combine(dest_flat, x1, route, row(norm_final_w), ys, tp, ts)

    kv_shape = (n_batch, WINDOW, SW_KV_HEADS, SW_HD)
    return (y_p, y_s, p_conv, p_dn, p_k.reshape(kv_shape), p_v.reshape(kv_shape), s_conv, s_dn,
            s_k.reshape(k_cache.shape), s_v.reshape(v_cache.shape))


def kernel(x_prompt, x_sample, state_dn_conv, state_dn, cache_swa_k, cache_swa_v, w_in, conv_w, a_log, dt_bias, dn_norm_w, sinks, rel_bias, w_out, norm_mix_w, norm_ffn_w, w_router_group, b_router_group, w_router_expert, b_router_expert, w_gate, w_up, w_down, norm_final_w):
    depth = w_in.shape[0]
    assert depth == 1, "the final-norm fusion below assumes a single layer"
    n_batch, seq, _ = x_prompt.shape
    s_batch, s_seq, _ = x_sample.shape
    outs = _layer(x_prompt.reshape(-1, D_MODEL), x_sample.reshape(-1, D_MODEL), n_batch, seq, s_batch, s_seq,
                  state_dn_conv[0], state_dn[0], cache_swa_k[0], cache_swa_v[0],
                  w_in[0], conv_w[0], a_log[0], dt_bias[0], dn_norm_w[0], sinks[0], rel_bias,
                  w_out[0], norm_mix_w[0], norm_ffn_w[0], w_router_group[0], b_router_group[0],
                  w_router_expert[0], b_router_expert[0], w_gate[0], w_up[0], w_down[0], norm_final_w)
    y_p, y_s, p_conv, p_dn, p_k, p_v, s_conv, s_dn, s_k, s_v = outs
    return (y_p.reshape(x_prompt.shape), y_s.reshape(x_sample.shape), p_conv[None], p_dn[None], p_k[None], p_v[None],
            s_conv[None], s_dn[None], s_k[None], s_v[None])
```

```python
import functools
import math

import jax
import jax.numpy as jnp
import numpy as np
from jax import lax
from jax.experimental import pallas as pl
from jax.experimental.pallas import tpu as pltpu

F32 = jnp.float32
BF16 = jnp.bfloat16
I32 = jnp.int32

D_MODEL = 1024
DN_HEADS = 8
DN_DK = 128
DN_DV = 128
DN_CONV = 4
DN_CHUNK = 64
DN_QK_W = DN_HEADS * DN_DK
DN_V_W = DN_HEADS * DN_DV
DN_CONV_W = 2 * DN_QK_W + DN_V_W
SW_HEADS = 16
SW_KV_HEADS = 2
SW_GROUP = SW_HEADS // SW_KV_HEADS
SW_HD = 64
SW_KV_W = SW_KV_HEADS * SW_HD
WINDOW = 128
SW_BLOCK = 128
REL_BUCKETS = 32
REL_MAX_DIST = 128
N_GROUPS = 8
EXP_PER_GROUP = 8
N_EXPERTS = N_GROUPS * EXP_PER_GROUP
D_EXPERT = 256
MOE_BLOCK = 256
EPS = 1e-6

LANES = 128
SUBLANES = 8
VMEM_LIMIT = 56 * 1024 * 1024

COL_QKV = 0
COL_Z = 3072
COL_SQ = 4096
COL_GA = 5120
COL_GB = 6144
COL_SK = 7168
COL_SV = 7296
PROJ_W = 7424
PROJ_CHUNK = 512


def _cparams(sem):
    return pltpu.CompilerParams(dimension_semantics=sem, vmem_limit_bytes=VMEM_LIMIT)


def _sigmoid(x):
    return 0.5 * jnp.tanh(0.5 * x) + 0.5


def _dot(a, b):
    return jnp.dot(a.astype(BF16), b.astype(BF16), preferred_element_type=F32)


def _dot_nt(a, b):
    return lax.dot_general(a.astype(BF16), b.astype(BF16), (((1,), (1,)), ((), ())), preferred_element_type=F32)


def _dot_tn(a, b):
    return lax.dot_general(a.astype(BF16), b.astype(BF16), (((0,), (0,)), ((), ())), preferred_element_type=F32)


def _dot_exact(a, b):
    return jnp.dot(a, b, precision=lax.Precision.HIGHEST, preferred_element_type=F32)


def _lane_iota(shape):
    return lax.broadcasted_iota(I32, shape, len(shape) - 1)


def _row_iota(shape):
    return lax.broadcasted_iota(I32, shape, len(shape) - 2)


def _col(x, j):
    return jnp.sum(jnp.where(_lane_iota(x.shape) == j, x, 0.0), axis=-1, keepdims=True)


BIG_TILES = (512, 256, 128, 64, 32, 16, 8)


def _token_tile(*sizes, cands=BIG_TILES[1:]):
    for t in cands:
        if all(s % t == 0 for s in sizes):
            return t
    raise ValueError(f"token counts {sizes} need a common tile that is a multiple of 8")


def _inproj_kernel(xp_ref, xs_ref, nw_ref, wb_ref, ws_ref, convw_ref, conv0_ref, proj_ref, ba_ref, tail_ref, cbuf_ref,
                   *, n_prompt_tiles, tiles_per_seq):
    i = pl.program_id(0)
    tm = xp_ref.shape[0]
    hist = SUBLANES - (DN_CONV - 1)

    def project(x_ref, conv):
        x = x_ref[...]
        h = (x * lax.rsqrt(jnp.mean(x * x, axis=-1, keepdims=True) + EPS) * nw_ref[...]).astype(BF16)
        ba_ref[...] = jnp.dot(h, ws_ref[...], preferred_element_type=F32)
        top = _row_iota((SUBLANES, PROJ_CHUNK))
        starts = list(range(0, PROJ_W, PROJ_CHUNK))
        if conv:
            with_conv = [c for c in starts if c + PROJ_CHUNK <= COL_QKV + DN_CONV_W]
            plain = [c for c in starts if c not in with_conv]
            starts = [c for pair in zip(with_conv, plain) for c in pair] + plain[len(with_conv):]
        for c0 in starts:
            c1 = min(c0 + PROJ_CHUNK, PROJ_W)
            cur = jnp.dot(h, wb_ref[:, c0:c1], preferred_element_type=F32)
            if conv and c1 <= COL_QKV + DN_CONV_W:
                prev = cbuf_ref[:, c0:c1]
                acc = cur * convw_ref[DN_CONV - 1:DN_CONV, c0:c1]
                for s in range(1, DN_CONV):
                    sh = pltpu.roll(cur, s, axis=0)
                    head = jnp.where(top < s, pltpu.roll(prev, s, axis=0), sh[:SUBLANES])
                    sh = jnp.concatenate([head, sh[SUBLANES:]], axis=0)
                    acc = acc + sh * convw_ref[DN_CONV - 1 - s:DN_CONV - s, c0:c1]
                cbuf_ref[:, c0:c1] = cur[tm - SUBLANES:]
                cur = acc * _sigmoid(acc)
            proj_ref[:, c0:c1] = cur

    @pl.when(i < n_prompt_tiles)
    def _():
        @pl.when(i % tiles_per_seq == 0)
        def _():
            cbuf_ref[...] = jnp.zeros(cbuf_ref.shape, F32)
            cbuf_ref[hist:SUBLANES, :] = conv0_ref[0]

        project(xp_ref, True)
        tail_ref[0] = cbuf_ref[...]

    @pl.when(i >= n_prompt_tiles)
    def _():
        project(xs_ref, False)


def _inproj(xp, xs, norm_w, w_big, w_small, conv_w, conv0, seq):
    tp, ts = xp.shape[0], xs.shape[0]
    tm = _token_tile(tp, ts, seq)
    assert COL_QKV == 0 and DN_CONV_W % PROJ_CHUNK == 0
    npt, nst = tp // tm, ts // tm
    tps = seq // tm
    const = lambda i: (0, 0)
    seq_of = lambda i: (jnp.minimum(i, npt - 1) // tps, 0, 0)
    return pl.pallas_call(
        functools.partial(_inproj_kernel, n_prompt_tiles=npt, tiles_per_seq=tps),
        grid=(npt + nst,),
        in_specs=[
            pl.BlockSpec((tm, D_MODEL), lambda i: (jnp.minimum(i, npt - 1), 0)),
            pl.BlockSpec((tm, D_MODEL), lambda i: (jnp.maximum(i - npt, 0), 0)),
            pl.BlockSpec((1, D_MODEL), const),
            pl.BlockSpec((D_MODEL, PROJ_W), const, pipeline_mode=pl.Buffered(1)),
            pl.BlockSpec((D_MODEL, LANES), const),
            pl.BlockSpec((DN_CONV, DN_CONV_W), const),
            pl.BlockSpec((1, DN_CONV - 1, DN_CONV_W), seq_of),
        ],
        out_specs=[
            pl.BlockSpec((tm, PROJ_W), lambda i: (i, 0)),
            pl.BlockSpec((tm, LANES), lambda i: (i, 0)),
            pl.BlockSpec((1, SUBLANES, DN_CONV_W), seq_of),
        ],
        out_shape=[
            jax.ShapeDtypeStruct((tp + ts, PROJ_W), F32),
            jax.ShapeDtypeStruct((tp + ts, LANES), F32),
            jax.ShapeDtypeStruct((tp // seq, SUBLANES, DN_CONV_W), F32),
        ],
        scratch_shapes=[pltpu.VMEM((SUBLANES, DN_CONV_W), F32)],
        compiler_params=_cparams(("arbitrary",)),
        name="inproj",
    )(xp, xs, norm_w, w_big, w_small, conv_w, conv0)


def _dn_core(groups, alog, dtb, nw, read_state, write_state, n_seg, seg_valid):
    rows = groups[0][0].shape[0]
    sr = rows // n_seg
    assert sr * n_seg == rows and sr & (sr - 1) == 0 and rows <= LANES
    seg_shift = sr.bit_length() - 1
    ri = _row_iota((rows, rows))
    ci = _lane_iota((rows, rows))
    incl = ri >= ci
    strict = ri > ci
    if n_seg > 1:
        same = (ri >> seg_shift) == (ci >> seg_shift)
        incl = incl & same
        strict = strict & same
    l_incl = incl.astype(F32)
    eye = (ri == ci).astype(F32)
    levels = max(1, math.ceil(math.log2(seg_valid)))

    beta_all, gsum_all, gtot_all, gsum_t = [], [], [], []
    for _, _, ba, _ in groups:
        b_all = _sigmoid(ba)
        sp = ba + dtb
        softplus = jnp.maximum(sp, 0.0) + jnp.log1p(jnp.exp(-jnp.abs(sp)))
        g_all = -jnp.exp(alog) * softplus
        if seg_valid < sr:
            live = (_row_iota((rows, LANES)) & (sr - 1)) < seg_valid
            b_all = jnp.where(live, b_all, 0.0)
            g_all = jnp.where(live, g_all, 0.0)
        gs = _dot_exact(l_incl, g_all)
        beta_all.append(b_all)
        gsum_all.append(gs)
        gtot_all.append(_dot_exact(same.astype(F32), g_all) if n_seg > 1 else gs[rows - 1:rows, :])
        padded = gs if rows == LANES else jnp.concatenate([gs, jnp.zeros((LANES - rows, LANES), F32)], axis=0)
        gsum_t.append(padded.T)

    probs = [(g, h) for g in range(len(groups)) for h in range(DN_HEADS)]
    segs = range(n_seg)
    q, k, v, kb, beta, gsum, gtot = {}, {}, {}, {}, {}, {}, {}
    for p in probs:
        g, h = p
        qkv = groups[g][0]
        qh = qkv[:, h * DN_DK:(h + 1) * DN_DK]
        kh = qkv[:, DN_QK_W + h * DN_DK:DN_QK_W + (h + 1) * DN_DK]
        v[p] = qkv[:, 2 * DN_QK_W + h * DN_DV:2 * DN_QK_W + (h + 1) * DN_DV]
        q[p] = qh * lax.rsqrt(jnp.sum(qh * qh, axis=-1, keepdims=True) + 1e-6) * (DN_DK ** -0.5)
        k[p] = kh * lax.rsqrt(jnp.sum(kh * kh, axis=-1, keepdims=True) + 1e-6)
        beta[p] = _col(beta_all[g], h)
        gsum[p] = _col(gsum_all[g], DN_HEADS + h)
        gtot[p] = _col(gtot_all[g], DN_HEADS + h)
        kb[p] = k[p] * beta[p]
    kq = {p: _dot_nt(jnp.concatenate([kb[p], q[p]], axis=0), k[p]) for p in probs}
    gamma = {(g, h): jnp.exp(jnp.where(incl, gsum[(g, h)] - gsum_t[g][DN_HEADS + h:DN_HEADS + h + 1, :rows], -jnp.inf))
             for g, h in probs}
    attn = {p: kq[p][rows:] * gamma[p] for p in probs}
    pw = {p: -jnp.where(strict, kq[p][:rows] * gamma[p], 0.0) for p in probs}
    t = {p: eye + pw[p] for p in probs}
    for _ in range(1, levels):
        pw = {p: _dot(pw[p], pw[p]) for p in probs}
        t = {p: t[p] + _dot(t[p], pw[p]) for p in probs}
    eg = {p: jnp.exp(gsum[p]) for p in probs}
    uw = {p: _dot(t[p], jnp.concatenate([v[p] * beta[p], kb[p] * eg[p]], axis=1)) for p in probs}
    qg = {p: q[p] * eg[p] for p in probs}
    state = {(p, s): read_state(p[0], s, p[1]) for p in probs for s in segs}
    wq = {(p, s): _dot(jnp.concatenate([uw[p][s * sr:(s + 1) * sr, DN_DV:], qg[p][s * sr:(s + 1) * sr]], axis=0),
                       state[(p, s)]) for p in probs for s in segs}
    join = lambda pieces: pieces[0] if len(pieces) == 1 else jnp.concatenate(pieces, axis=0)
    v_new = {p: uw[p][:, :DN_DV] - join([wq[(p, s)][:sr] for s in segs]) for p in probs}
    o = {p: join([wq[(p, s)][sr:] for s in segs]) + _dot(attn[p], v_new[p]) for p in probs}
    kd = {p: k[p] * jnp.exp(gtot[p] - gsum[p]) for p in probs}
    for p in probs:
        for s in segs:
            r0 = s * sr if n_seg > 1 else 0
            decay = jnp.exp(gtot[p][r0:r0 + 1, :])
            write_state(p[0], s, p[1],
                        state[(p, s)] * decay + _dot_tn(kd[p][s * sr:(s + 1) * sr], v_new[p][s * sr:(s + 1) * sr]))
    outs = []
    for g, (_, z, _, gate) in enumerate(groups):
        heads = []
        for h in range(DN_HEADS):
            oh = o[(g, h)]
            zz = z[:, h * DN_DV:(h + 1) * DN_DV]
            on = oh * lax.rsqrt(jnp.mean(oh * oh, axis=-1, keepdims=True) + EPS) * nw
            heads.append(on * (zz * _sigmoid(zz)) * _sigmoid(gate[:, h * DN_DV:(h + 1) * DN_DV]))
        outs.append(jnp.concatenate(heads, axis=1))
    return outs


def _dn_prompt_kernel(*refs, chunk, n_batch):
    nb = n_batch
    qkv_refs, z_refs, ba_refs, gate_refs = refs[0:nb], refs[nb:2 * nb], refs[2 * nb:3 * nb], refs[3 * nb:4 * nb]
    alog_ref, dtb_ref, nw_ref, s0_ref, o_ref, sout_ref = refs[4 * nb:]

    @pl.when(pl.program_id(0) == 0)
    def _():
        sout_ref[...] = s0_ref[...]

    groups = [(qkv_refs[b][...], z_refs[b][...], ba_refs[b][...], gate_refs[b][...]) for b in range(nb)]

    def read_state(g, s, h):
        return sout_ref[g, h]

    def write_state(g, s, h, val):
        sout_ref[g, h] = val

    outs = _dn_core(groups, alog_ref[...], dtb_ref[...], nw_ref[...], read_state, write_state, 1, chunk)
    for b in range(nb):
        o_ref[b] = outs[b]


def _dn_prompt(proj, ba, alog_row, dtb_row, dn_nw, s0, n_batch, seq):
    chunk = min(DN_CHUNK, seq)
    assert seq % chunk == 0 and chunk % SUBLANES == 0
    nc = seq // chunk
    const2 = lambda c: (0, 0)
    rows = lambda b, col: (lambda c: (b * nc + c, col))
    batches = range(n_batch)
    o, s_out = pl.pallas_call(
        functools.partial(_dn_prompt_kernel, chunk=chunk, n_batch=n_batch),
        grid=(nc,),
        in_specs=(
            [pl.BlockSpec((chunk, DN_CONV_W), rows(b, COL_QKV // DN_CONV_W)) for b in batches]
            + [pl.BlockSpec((chunk, DN_V_W), rows(b, COL_Z // DN_V_W)) for b in batches]
            + [pl.BlockSpec((chunk, LANES), rows(b, 0)) for b in batches]
            + [pl.BlockSpec((chunk, D_MODEL), rows(b, COL_GA // D_MODEL)) for b in batches]
            + [
                pl.BlockSpec((1, LANES), const2),
                pl.BlockSpec((1, LANES), const2),
                pl.BlockSpec((1, DN_DV), const2),
                pl.BlockSpec((n_batch, DN_HEADS, DN_DK, DN_DV), lambda c: (0, 0, 0, 0)),
            ]
        ),
        out_specs=[
            pl.BlockSpec((n_batch, chunk, DN_V_W), lambda c: (0, c, 0)),
            pl.BlockSpec((n_batch, DN_HEADS, DN_DK, DN_DV), lambda c: (0, 0, 0, 0)),
        ],
        out_shape=[
            jax.ShapeDtypeStruct((n_batch, seq, DN_V_W), F32),
            jax.ShapeDtypeStruct((n_batch, DN_HEADS, DN_DK, DN_DV), F32),
        ],
        compiler_params=_cparams(("arbitrary",)),
        name="dn_prompt",
    )(*([proj] * n_batch), *([proj] * n_batch), *([ba] * n_batch), *([proj] * n_batch), alog_row, dtb_row, dn_nw, s0)
    return o.reshape(n_batch * seq, DN_V_W), s_out


def _spread_rows(ref, n_bb, seq):
    per = SUBLANES // seq
    pieces = []
    for j in range(n_bb // per):
        x8 = ref[j * SUBLANES:(j + 1) * SUBLANES, :]
        for r in range(per):
            pieces.append(x8 if r == 0 else pltpu.roll(x8, SUBLANES - r * seq, axis=0))
    return pieces


def _gather_rows(pieces, seq):
    per = SUBLANES // seq
    rows = _row_iota(pieces[0].shape)
    tiles = []
    for j in range(len(pieces) // per):
        tile = pieces[j * per]
        for r in range(1, per):
            tile = jnp.where(rows >= r * seq, pltpu.roll(pieces[j * per + r], r * seq, axis=0), tile)
        tiles.append(tile)
    return jnp.concatenate(tiles, axis=0)


def _dn_sample_kernel(qkv_ref, z_ref, ba_ref, gate_ref, convw_ref, alog_ref, dtb_ref, nw_ref, conv0_ref, s0_ref,
                      o_ref, convout_ref, sout_ref, cbuf_ref, *, seq, n_bb):
    hist = SUBLANES - (DN_CONV - 1)
    spread = lambda ref: _spread_rows(ref, n_bb, seq)
    for bb, piece in enumerate(spread(qkv_ref)):
        cbuf_ref[bb, SUBLANES:2 * SUBLANES, :] = piece
    cbuf_ref[:, hist:SUBLANES, :] = conv0_ref[...]
    w = convw_ref[...]
    acc = cbuf_ref[:, hist:hist + SUBLANES, :] * w[0:1, :]
    for i in range(1, DN_CONV):
        acc = acc + cbuf_ref[:, hist + i:hist + i + SUBLANES, :] * w[i:i + 1, :]
    live = _row_iota(acc.shape) < seq
    qkv = jnp.where(live, acc * _sigmoid(acc), 0.0).reshape(n_bb * SUBLANES, DN_CONV_W)
    convout_ref[...] = cbuf_ref[:, SUBLANES + seq - (DN_CONV - 1):SUBLANES + seq, :]

    def read_state(g, s, h):
        return s0_ref[s, h]

    def write_state(g, s, h, val):
        sout_ref[s, h] = val

    group = (qkv, jnp.concatenate(spread(z_ref), axis=0), jnp.concatenate(spread(ba_ref), axis=0),
             jnp.concatenate(spread(gate_ref), axis=0))
    o = _dn_core([group], alog_ref[...], dtb_ref[...], nw_ref[...], read_state, write_state, n_bb, seq)[0]
    o_ref[...] = _gather_rows([o[bb * SUBLANES:(bb + 1) * SUBLANES] for bb in range(n_bb)], seq)


def _dn_sample(proj, ba, conv_w, alog_row, dtb_row, dn_nw, conv0, s0, row0, n_batch, seq):
    assert SUBLANES % seq == 0 and seq >= DN_CONV - 1
    n_bb = SUBLANES
    rows_in = n_bb * seq
    assert n_batch % n_bb == 0 and row0 % rows_in == 0
    rb0 = row0 // rows_in
    const1 = lambda i: (0, 0)
    return pl.pallas_call(
        functools.partial(_dn_sample_kernel, seq=seq, n_bb=n_bb),
        grid=(n_batch // n_bb,),
        in_specs=[
            pl.BlockSpec((rows_in, DN_CONV_W), lambda i: (rb0 + i, COL_QKV // DN_CONV_W)),
            pl.BlockSpec((rows_in, DN_V_W), lambda i: (rb0 + i, COL_Z // DN_V_W)),
            pl.BlockSpec((rows_in, LANES), lambda i: (rb0 + i, 0)),
            pl.BlockSpec((rows_in, D_MODEL), lambda i: (rb0 + i, COL_GA // D_MODEL)),
            pl.BlockSpec((DN_CONV, DN_CONV_W), const1),
            pl.BlockSpec((1, LANES), const1),
            pl.BlockSpec((1, LANES), const1),
            pl.BlockSpec((1, DN_DV), const1),
            pl.BlockSpec((n_bb, DN_CONV - 1, DN_CONV_W), lambda i: (i, 0, 0)),
            pl.BlockSpec((n_bb, DN_HEADS, DN_DK, DN_DV), lambda i: (i, 0, 0, 0)),
        ],
        out_specs=[
            pl.BlockSpec((rows_in, DN_V_W), lambda i: (i, 0)),
            pl.BlockSpec((n_bb, DN_CONV - 1, DN_CONV_W), lambda i: (i, 0, 0)),
            pl.BlockSpec((n_bb, DN_HEADS, DN_DK, DN_DV), lambda i: (i, 0, 0, 0)),
        ],
        out_shape=[
            jax.ShapeDtypeStruct((n_batch * seq, DN_V_W), F32),
            jax.ShapeDtypeStruct((n_batch, DN_CONV - 1, DN_CONV_W), F32),
            jax.ShapeDtypeStruct((n_batch, DN_HEADS, DN_DK, DN_DV), F32),
        ],
        scratch_shapes=[pltpu.VMEM((n_bb, 2 * SUBLANES, DN_CONV_W), F32)],
        compiler_params=_cparams(("arbitrary",)),
        name="dn_sample",
    )(proj, proj, ba, proj, conv_w, alog_row, dtb_row, dn_nw, conv0, s0)


def _masked_bucket(dist, ok):
    n = np.maximum(dist, 0)
    max_exact = REL_BUCKETS // 2
    large = max_exact + (np.log(np.maximum(n, 1).astype(np.float32) / max_exact)
                         / math.log(REL_MAX_DIST / max_exact) * (REL_BUCKETS - max_exact)).astype(np.int32)
    bucket = np.where(n < max_exact, n, np.minimum(large, REL_BUCKETS - 1))
    return np.where(ok, bucket, -1).astype(np.int32)


def _relbias_kernel(tab_ref, bucket_ref, o_ref):
    h = pl.program_id(0)
    bk = bucket_ref[...]
    acc = jnp.full(bk.shape, -jnp.inf, F32)
    for b in range(REL_BUCKETS):
        acc = jnp.where(bk == b, tab_ref[b * SW_HEADS + h], acc)
    o_ref[0] = acc


def _relbias(rel_table, bucket):
    nq, ns = bucket.shape
    return pl.pallas_call(
        _relbias_kernel,
        grid=(SW_HEADS,),
        in_specs=[
            pl.BlockSpec(memory_space=pltpu.SMEM),
            pl.BlockSpec((nq, ns), lambda h: (0, 0)),
        ],
        out_specs=pl.BlockSpec((1, nq, ns), lambda h: (h, 0, 0)),
        out_shape=jax.ShapeDtypeStruct((SW_HEADS, nq, ns), F32),
        compiler_params=_cparams(("arbitrary",)),
        name="relbias",
    )(rel_table.reshape(-1), bucket)


def _dup_halves(x):
    lo = _lane_iota(x.shape) < SW_HD
    xr = pltpu.roll(x, SW_HD, axis=1)
    return jnp.where(lo, x, xr).astype(BF16), jnp.where(lo, xr, x).astype(BF16)


def _sink_softmax_pv(s, sink, vv):
    m = jnp.maximum(jnp.max(s, axis=-1, keepdims=True), sink)
    p = jnp.exp(s - m)
    denom = jnp.sum(p, axis=-1, keepdims=True) + jnp.exp(sink - m)
    return _dot(p, vv) * (1.0 / denom)


def _swa_prompt_kernel(sinks_ref, q_ref, kc_ref, kp_ref, vc_ref, vp_ref, bias_ref, gate_ref, other_ref,
                       o_ref, klast_ref, vlast_ref, *, n_sub):
    i = pl.program_id(1)
    klast_ref[0] = kc_ref[(n_sub - 1) * SW_BLOCK:, :]
    vlast_ref[0] = vc_ref[(n_sub - 1) * SW_BLOCK:, :]
    kk_all = _dup_halves(jnp.concatenate([kp_ref[...], kc_ref[...]], axis=0))
    vv_all = _dup_halves(jnp.concatenate([vp_ref[...], vc_ref[...]], axis=0))
    lo = _lane_iota((SW_BLOCK, LANES)) < SW_HD
    for sub in range(n_sub):
        rows = slice(sub * SW_BLOCK, (sub + 1) * SW_BLOCK)
        keys = slice(sub * SW_BLOCK, (sub + 2) * SW_BLOCK)
        bias_rows = pl.ds(pl.multiple_of(jnp.minimum(i, 1) * SW_BLOCK, SW_BLOCK), SW_BLOCK) if sub == 0 \
            else pl.ds(SW_BLOCK, SW_BLOCK)
        for pair in range(SW_HEADS // 2):
            cols = slice(pair * LANES, (pair + 1) * LANES)
            qp = q_ref[rows, cols] * (SW_HD ** -0.5)
            outs = []
            for half in range(2):
                hq = 2 * pair + half
                kv = hq // SW_GROUP
                qm = jnp.where(lo if half == 0 else ~lo, qp, 0.0)
                s = _dot_nt(qm, kk_all[kv][keys]) + bias_ref[hq, bias_rows, :]
                outs.append(_sink_softmax_pv(s, sinks_ref[hq], vv_all[kv][keys]))
            o_ref[rows, cols] = other_ref[rows, cols] + _sigmoid(gate_ref[rows, cols]) * jnp.where(lo, outs[0], outs[1])


def _swa_prompt(proj, sinks, bias, other, n_batch, seq):
    assert seq % SW_BLOCK == 0 and WINDOW == SW_BLOCK
    nb = seq // SW_BLOCK
    n_sub = next(n for n in (4, 2, 1) if nb % n == 0)
    ns = nb // n_sub
    rows = n_sub * SW_BLOCK
    cur = lambda col: (lambda b, i: (b * ns + i, col))
    prev = lambda col: (lambda b, i: (b * nb + jnp.maximum(i * n_sub - 1, 0), col))
    return pl.pallas_call(
        functools.partial(_swa_prompt_kernel, n_sub=n_sub),
        grid=(n_batch, ns),
        in_specs=[
            pl.BlockSpec(memory_space=pltpu.SMEM),
            pl.BlockSpec((rows, SW_HEADS * SW_HD), cur(COL_SQ // (SW_HEADS * SW_HD))),
            pl.BlockSpec((rows, SW_KV_W), cur(COL_SK // SW_KV_W)),
            pl.BlockSpec((SW_BLOCK, SW_KV_W), prev(COL_SK // SW_KV_W)),
            pl.BlockSpec((rows, SW_KV_W), cur(COL_SV // SW_KV_W)),
            pl.BlockSpec((SW_BLOCK, SW_KV_W), prev(COL_SV // SW_KV_W)),
            pl.BlockSpec((SW_HEADS, 2 * SW_BLOCK, 2 * SW_BLOCK), lambda b, i: (0, 0, 0)),
            pl.BlockSpec((rows, D_MODEL), cur(COL_GB // D_MODEL)),
            pl.BlockSpec((rows, D_MODEL), lambda b, i: (b * ns + i, 0)),
        ],
        out_specs=[
            pl.BlockSpec((rows, SW_HEADS * SW_HD), lambda b, i: (b * ns + i, 0)),
            pl.BlockSpec((1, SW_BLOCK, SW_KV_W), lambda b, i: (b, 0, 0)),
            pl.BlockSpec((1, SW_BLOCK, SW_KV_W), lambda b, i: (b, 0, 0)),
        ],
        out_shape=[
            jax.ShapeDtypeStruct((n_batch * seq, SW_HEADS * SW_HD), F32),
            jax.ShapeDtypeStruct((n_batch, SW_BLOCK, SW_KV_W), F32),
            jax.ShapeDtypeStruct((n_batch, SW_BLOCK, SW_KV_W), F32),
        ],
        compiler_params=_cparams(("arbitrary", "arbitrary")),
        name="swa_prompt",
    )(sinks, proj, proj, proj, proj, proj, bias, proj, other)


def _swa_sample_kernel(q_ref, kn_ref, vn_ref, kc_ref, vc_ref, bias_ref, sink_ref, gate_ref, other_ref,
                       o_ref, ko_ref, vo_ref, kall_ref, vall_ref, *, seq, n_bb, n_cache):
    n_keys = kall_ref.shape[1]
    zeros_tail = jnp.zeros((n_bb, n_keys - n_cache - SUBLANES, LANES), F32)
    lo = _lane_iota((SUBLANES, LANES)) < SW_HD
    for ref, cache_ref, new_ref in ((kall_ref, kc_ref, kn_ref), (vall_ref, vc_ref, vn_ref)):
        ref[:, 0:n_cache, :] = cache_ref[...]
        for bb, piece in enumerate(_spread_rows(new_ref, n_bb, seq)):
            ref[bb, n_cache:n_cache + SUBLANES, :] = piece
        ref[:, n_cache + SUBLANES:, :] = zeros_tail
    ko_ref[...] = kall_ref[:, seq:seq + n_cache, :]
    vo_ref[...] = vall_ref[:, seq:seq + n_cache, :]
    outs = []
    for bb, q8 in enumerate(_spread_rows(q_ref, n_bb, seq)):
        kk = _dup_halves(kall_ref[bb])
        vv = _dup_halves(vall_ref[bb])
        pairs = []
        for kv in range(SW_KV_HEADS):
            pieces = []
            for g in range(SW_GROUP):
                hq = kv * SW_GROUP + g
                qp = q8[:, (hq // 2) * LANES:(hq // 2 + 1) * LANES]
                pieces.append(jnp.where(lo if hq % 2 == 0 else ~lo, qp, 0.0))
            qs = jnp.concatenate(pieces, axis=0) * (SW_HD ** -0.5)
            bias = jnp.concatenate([bias_ref[kv * SW_GROUP + g] for g in range(SW_GROUP)], axis=0)
            s = _dot_nt(qs, kk[kv]) + bias
            res = _sink_softmax_pv(s, _col(sink_ref[kv], 0), vv[kv])
            for g in range(0, SW_GROUP, 2):
                pairs.append(jnp.where(lo, res[g * SUBLANES:(g + 1) * SUBLANES],
                                       res[(g + 1) * SUBLANES:(g + 2) * SUBLANES]))
        outs.append(jnp.concatenate(pairs, axis=1))
    o_ref[...] = other_ref[...] + _sigmoid(gate_ref[...]) * _gather_rows(outs, seq)


def _swa_sample(proj, k_cache, v_cache, bias, bias_row0, sink_rows, other, row0, n_batch, seq):
    assert SUBLANES % seq == 0
    n_bb = SUBLANES
    rows_in = n_bb * seq
    n_cache = k_cache.shape[1]
    assert n_batch % n_bb == 0 and row0 % rows_in == 0 and n_cache % SUBLANES == 0
    n_keys = bias.shape[-1]
    rb0 = row0 // rows_in
    blk = lambda col: (lambda i: (rb0 + i, col))
    return pl.pallas_call(
        functools.partial(_swa_sample_kernel, seq=seq, n_bb=n_bb, n_cache=n_cache),
        grid=(n_batch // n_bb,),
        in_specs=[
            pl.BlockSpec((rows_in, SW_HEADS * SW_HD), blk(COL_SQ // (SW_HEADS * SW_HD))),
            pl.BlockSpec((rows_in, SW_KV_W), blk(COL_SK // SW_KV_W)),
            pl.BlockSpec((rows_in, SW_KV_W), blk(COL_SV // SW_KV_W)),
            pl.BlockSpec((n_bb, n_cache, SW_KV_W), lambda i: (i, 0, 0)),
            pl.BlockSpec((n_bb, n_cache, SW_KV_W), lambda i: (i, 0, 0)),
            pl.BlockSpec((SW_HEADS, SUBLANES, n_keys), lambda i: (0, bias_row0 // SUBLANES, 0)),
            pl.BlockSpec((SW_KV_HEADS, SW_GROUP * SUBLANES, n_keys), lambda i: (0, 0, 0)),
            pl.BlockSpec((rows_in, D_MODEL), blk(COL_GB // D_MODEL)),
            pl.BlockSpec((rows_in, D_MODEL), lambda i: (i, 0)),
        ],
        out_specs=[
            pl.BlockSpec((rows_in, SW_HEADS * SW_HD), lambda i: (i, 0)),
            pl.BlockSpec((n_bb, n_cache, SW_KV_W), lambda i: (i, 0, 0)),
            pl.BlockSpec((n_bb, n_cache, SW_KV_W), lambda i: (i, 0, 0)),
        ],
        out_shape=[
            jax.ShapeDtypeStruct((n_batch * seq, SW_HEADS * SW_HD), F32),
            jax.ShapeDtypeStruct(k_cache.shape, F32),
            jax.ShapeDtypeStruct(v_cache.shape, F32),
        ],
        scratch_shapes=[pltpu.VMEM((n_bb, n_keys, SW_KV_W), F32), pltpu.VMEM((n_bb, n_keys, SW_KV_W), F32)],
        compiler_params=_cparams(("arbitrary",)),
        name="swa_sample",
    )(proj, proj, proj, k_cache, v_cache, bias, sink_rows, proj, other)


def _mix_kernel(xp_ref, xs_ref, mp_ref, ms_ref, wo_ref, nw_ref, wr_ref, br_ref,
                x1_ref, h2_ref, route_ref, *, n_prompt_tiles):
    i = pl.program_id(0)

    def run(x_ref, mixed_ref):
        x1 = x_ref[...] + _dot(mixed_ref[...], wo_ref[...])
        x1_ref[...] = x1
        h2 = x1 * lax.rsqrt(jnp.mean(x1 * x1, axis=-1, keepdims=True) + EPS) * nw_ref[...]
        h2_ref[...] = h2
        logits = _dot(h2, wr_ref[...]) + br_ref[...]
        lane = _lane_iota(logits.shape)
        lanef = lane.astype(F32)
        big = float(2 * LANES)
        is_g = lane < N_GROUPS
        gl = jnp.where(is_g, logits, -jnp.inf)
        gmax = jnp.max(gl, axis=-1, keepdims=True)
        gval = 1.0 / jnp.sum(jnp.where(is_g, jnp.exp(gl - gmax), 0.0), axis=-1, keepdims=True)
        grp = jnp.min(jnp.where(gl == gmax, lanef, big), axis=-1, keepdims=True)
        e_grp = ((lane - N_GROUPS) >> 3).astype(F32)
        is_e = (lane >= N_GROUPS) & (lane < N_GROUPS + N_EXPERTS) & (e_grp == grp)
        el = jnp.where(is_e, logits, -jnp.inf)
        v1 = jnp.max(el, axis=-1, keepdims=True)
        i1 = jnp.min(jnp.where(el == v1, lanef, big), axis=-1, keepdims=True)
        el2 = jnp.where(lanef == i1, -jnp.inf, el)
        v2 = jnp.max(el2, axis=-1, keepdims=True)
        i2 = jnp.min(jnp.where(el2 == v2, lanef, big), axis=-1, keepdims=True)
        e2 = jnp.exp(v2 - v1)
        w1 = gval / (1.0 + e2)
        w2 = gval * e2 / (1.0 + e2)
        route_ref[...] = jnp.where(lane == 0, i1 - N_GROUPS,
                                   jnp.where(lane == 1, i2 - N_GROUPS,
                                             jnp.where(lane == 2, w1, jnp.where(lane == 3, w2, 0.0))))

    @pl.when(i < n_prompt_tiles)
    def _():
        run(xp_ref, mp_ref)

    @pl.when(i >= n_prompt_tiles)
    def _():
        run(xs_ref, ms_ref)


def _mix(xp, xs, mixed_p, mixed_s, w_out, norm_w, w_router, b_router):
    tp, ts = xp.shape[0], xs.shape[0]
    tm = _token_tile(tp, ts, cands=BIG_TILES)
    npt, nst = tp // tm, ts // tm
    const = lambda i: (0, 0)
    row = lambda i: (i, 0)
    return pl.pallas_call(
        functools.partial(_mix_kernel, n_prompt_tiles=npt),
        grid=(npt + nst,),
        in_specs=[
            pl.BlockSpec((tm, D_MODEL), lambda i: (jnp.minimum(i, npt - 1), 0)),
            pl.BlockSpec((tm, D_MODEL), lambda i: (jnp.maximum(i - npt, 0), 0)),
            pl.BlockSpec((tm, D_MODEL), lambda i: (jnp.minimum(i, npt - 1), 0)),
            pl.BlockSpec((tm, D_MODEL), lambda i: (jnp.maximum(i - npt, 0), 0)),
            pl.BlockSpec((D_MODEL, D_MODEL), const),
            pl.BlockSpec((1, D_MODEL), const),
            pl.BlockSpec((D_MODEL, LANES), const),
            pl.BlockSpec((1, LANES), const),
        ],
        out_specs=[
            pl.BlockSpec((tm, D_MODEL), row),
            pl.BlockSpec((tm, D_MODEL), row),
            pl.BlockSpec((tm, LANES), row),
        ],
        out_shape=[
            jax.ShapeDtypeStruct((tp + ts, D_MODEL), F32),
            jax.ShapeDtypeStruct((tp + ts, D_MODEL), F32),
            jax.ShapeDtypeStruct((tp + ts, LANES), F32),
        ],
        compiler_params=_cparams(("arbitrary",)),
        name="mix_router",
    )(xp, xs, mixed_p, mixed_s, w_out, norm_w, w_router, b_router)


def _rank_kernel(route_ref, dest_ref, meta_ref, rank_ref, cnt_ref, *, tile, blk):
    phase = pl.program_id(0)
    i = pl.program_id(1)
    shape = (tile, LANES)
    lane = _lane_iota(shape)
    lanef = lane.astype(F32)
    r = route_ref[...]
    oh0 = lanef == _col(r, 0)
    oh1 = lanef == _col(r, 1)
    rows = pl.ds(pl.multiple_of(i * tile, tile), tile)

    @pl.when(phase == 0)
    def _():
        @pl.when(i == 0)
        def _():
            cnt_ref[...] = jnp.zeros(cnt_ref.shape, F32)

        oh = jnp.where(oh0 | oh1, 1.0, 0.0)
        tri = jnp.where(_row_iota((tile, tile)) > _lane_iota((tile, tile)), 1.0, 0.0)
        before = _dot(tri, oh) + cnt_ref[0:1, :]
        rank0 = jnp.sum(jnp.where(oh0, before, 0.0), axis=-1, keepdims=True)
        rank1 = jnp.sum(jnp.where(oh1, before, 0.0), axis=-1, keepdims=True)
        rank_ref[rows, :] = jnp.where(lane == 0, rank0, jnp.where(lane == 1, rank1, 0.0))
        cnt_ref[0:1, :] = cnt_ref[0:1, :] + jnp.sum(oh, axis=0, keepdims=True)

    @pl.when(phase == 1)
    def _():
        cnt = cnt_ref[0:1, :]
        padded = jnp.floor((cnt + (blk - 1)) / blk) * blk
        before_lane = jnp.where(_row_iota((LANES, LANES)) < _lane_iota((LANES, LANES)), 1.0, 0.0)
        start = _dot_exact(jnp.broadcast_to(padded, (SUBLANES, LANES)), before_lane)[0:1, :]
        rk = rank_ref[rows, :]
        d0 = jnp.sum(jnp.where(oh0, start, 0.0), axis=-1, keepdims=True) + _col(rk, 0)
        d1 = jnp.sum(jnp.where(oh1, start, 0.0), axis=-1, keepdims=True) + _col(rk, 1)
        dest_ref[...] = jnp.where(lane == 0, d0, jnp.where(lane == 1, d1, 0.0)).astype(I32)

        @pl.when(i == 0)
        def _():
            end = start + padded
            mshape = meta_ref.shape
            blk_start = (_row_iota(mshape) * blk).astype(F32)
            hit = (_lane_iota(mshape) < N_EXPERTS) & (end <= blk_start)
            be = jnp.minimum(jnp.sum(jnp.where(hit, 1.0, 0.0), axis=-1, keepdims=True), N_EXPERTS - 1.0)
            n_used = _col(end, N_EXPERTS - 1) / blk
            ml = _lane_iota(mshape)
            mine = ml.astype(F32) == be
            seg_start = jnp.sum(jnp.where(mine, start, 0.0), axis=-1, keepdims=True)
            seg_count = jnp.sum(jnp.where(mine, cnt, 0.0), axis=-1, keepdims=True)
            n_valid = jnp.clip(seg_count - (blk_start[:, 0:1] - seg_start), 0.0, float(blk))
            meta_ref[...] = jnp.where(ml == 0, be, jnp.where(ml == 1, n_used,
                                                              jnp.where(ml == 2, n_valid, 0.0))).astype(I32)


def _rank(route, tile, blk, n_blocks):
    t = route.shape[0]
    nt = t // tile
    nbp = -(-n_blocks // SUBLANES) * SUBLANES
    return pl.pallas_call(
        functools.partial(_rank_kernel, tile=tile, blk=blk),
        grid=(2, nt),
        in_specs=[pl.BlockSpec((tile, LANES), lambda p, i: (i, 0))],
        out_specs=[
            pl.BlockSpec((tile, LANES), lambda p, i: (i * p, 0)),
            pl.BlockSpec((nbp, LANES), lambda p, i: (0, 0)),
        ],
        out_shape=[
            jax.ShapeDtypeStruct((t, LANES), I32),
            jax.ShapeDtypeStruct((nbp, LANES), I32),
        ],
        scratch_shapes=[pltpu.VMEM((t, LANES), F32), pltpu.VMEM((SUBLANES, LANES), F32)],
        compiler_params=_cparams(("arbitrary", "arbitrary")),
        name="moe_rank",
    )(route)


def _row_copy(src, src_row, dst, group, sub, sem):
    return pltpu.make_async_copy(src.at[pl.ds(src_row, 1)], dst.at[group, pl.ds(sub, 1)], sem)


def _last_used(i, nu_ref):
    return jnp.minimum(i, jnp.maximum(nu_ref[0] - 1, 0))


def _for_rows(n, body):
    def group(g, carry):
        for u in range(SUBLANES):
            body(g, u)
        return carry

    def single(t, carry):
        body(t // SUBLANES, t % SUBLANES)
        return carry

    n_groups = n // SUBLANES
    lax.fori_loop(0, n_groups, group, 0)
    lax.fori_loop(n_groups * SUBLANES, n, single, 0)


def _expert_kernel(be_ref, nu_ref, nv_ref, dest_ref, h2_ref, wg_ref, wu_ref, wd_ref, y_ref,
                   xbuf_ref, wgu_ref, wdn_ref, inv_ref, sem, *, blk):
    i = pl.program_id(0)
    n_used = nu_ref[0]
    used = i < n_used
    slot = i % 2
    blk_i = _last_used(i, nu_ref)
    fresh = (i == 0) | (be_ref[blk_i] != be_ref[jnp.maximum(blk_i - 1, 0)])

    def gather(block, into):
        def issue(g, u):
            tok = inv_ref[block * blk + g * SUBLANES + u]
            _row_copy(h2_ref, tok, xbuf_ref.at[into], g, u, sem.at[into]).start(priority=u % 2 if isinstance(u, int) else 0)

        _for_rows(nv_ref[block], issue)

    @pl.when(i == 0)
    def _():
        def place(a, carry):
            inv_ref[dest_ref[a]] = a >> 1
            return carry

        lax.fori_loop(0, dest_ref.shape[0], place, 0, unroll=16)
        xbuf_ref[...] = jnp.zeros(xbuf_ref.shape, F32)
        gather(0, 0)

    @pl.when(i + 1 < n_used)
    def _():
        gather(i + 1, 1 - slot)

    @pl.when(used & fresh)
    def _():
        wgu_ref[:, :D_EXPERT] = wg_ref[0].astype(BF16)
        wgu_ref[:, D_EXPERT:] = wu_ref[0].astype(BF16)
        wdn_ref[...] = wd_ref[0].astype(BF16)

    @pl.when(used)
    def _():
        _for_rows(nv_ref[i], lambda g, u: _row_copy(h2_ref, 0, xbuf_ref.at[slot], 0, 0, sem.at[slot]).wait())
        x = xbuf_ref[slot].reshape(blk, D_MODEL)
        gu = jnp.dot(x.astype(BF16), wgu_ref[...], preferred_element_type=F32)
        g = gu[:, :D_EXPERT]
        hidden = (g * _sigmoid(g) * gu[:, D_EXPERT:]).astype(BF16)
        y_ref[...] = jnp.dot(hidden, wdn_ref[...], preferred_element_type=F32)

    @pl.when(jnp.logical_not(used))
    def _():
        y_ref[...] = jnp.zeros(y_ref.shape, F32)


def _experts(block_expert, n_used, n_valid, dest_flat, h2, w_gate, w_up, w_down, blk):
    n_blocks = block_expert.shape[0]
    wsel = lambda i, be, nu, nv, de: (be[_last_used(i, nu)], 0, 0)
    return pl.pallas_call(
        functools.partial(_expert_kernel, blk=blk),
        grid_spec=pltpu.PrefetchScalarGridSpec(
            num_scalar_prefetch=4,
            grid=(n_blocks,),
            in_specs=[
                pl.BlockSpec(memory_space=pl.ANY),
                pl.BlockSpec((1, D_MODEL, D_EXPERT), wsel),
                pl.BlockSpec((1, D_MODEL, D_EXPERT), wsel),
                pl.BlockSpec((1, D_EXPERT, D_MODEL), wsel),
            ],
            out_specs=pl.BlockSpec((blk, D_MODEL), lambda i, be, nu, nv, de: (i, 0)),
            scratch_shapes=[
                pltpu.VMEM((2, blk // SUBLANES, SUBLANES, D_MODEL), F32),
                pltpu.VMEM((D_MODEL, 2 * D_EXPERT), BF16),
                pltpu.VMEM((D_EXPERT, D_MODEL), BF16),
                pltpu.SMEM((n_blocks * blk,), I32),
                pltpu.SemaphoreType.DMA((2,)),
            ],
        ),
        out_shape=jax.ShapeDtypeStruct((n_blocks * blk, D_MODEL), F32),
        compiler_params=_cparams(("arbitrary",)),
        name="moe_experts",
    )(block_expert, n_used, n_valid, dest_flat, h2, w_gate, w_up, w_down)


def _combine_kernel(dest_ref, x1_ref, route_ref, nw_ref, ys_ref, yp_ref, ysm_ref, ybuf_ref, sem,
                    *, tile, n_prompt_tiles):
    i = pl.program_id(0)
    slot = i % 2

    groups = tile // SUBLANES

    def gather(step, into):
        def issue(g, carry):
            for u in range(SUBLANES):
                for k in range(2):
                    slot_row = dest_ref[2 * step * tile + g * (2 * SUBLANES) + 2 * u + k]
                    _row_copy(ys_ref, slot_row, ybuf_ref.at[into], k * groups + g, u,
                              sem.at[into]).start(priority=k)
            return carry

        lax.fori_loop(0, groups, issue, 0)

    @pl.when(i == 0)
    def _():
        gather(0, 0)

    @pl.when(i + 1 < pl.num_programs(0))
    def _():
        gather(i + 1, 1 - slot)

    def drain(t, carry):
        _row_copy(ys_ref, 0, ybuf_ref.at[slot], 0, 0, sem.at[slot]).wait()
        return carry

    lax.fori_loop(0, 2 * tile, drain, 0, unroll=8)
    r = route_ref[...]
    y = (ybuf_ref[slot, 0:groups].reshape(tile, D_MODEL) * _col(r, 2)
         + ybuf_ref[slot, groups:2 * groups].reshape(tile, D_MODEL) * _col(r, 3))
    x2 = x1_ref[...] + y
    out = x2 * lax.rsqrt(jnp.mean(x2 * x2, axis=-1, keepdims=True) + EPS) * nw_ref[...]

    @pl.when(i < n_prompt_tiles)
    def _():
        yp_ref[...] = out

    @pl.when(i >= n_prompt_tiles)
    def _():
        ysm_ref[...] = out


def _combine(dest_flat, x1, route, norm_w, ys, tp, ts):
    tile = _token_tile(tp, ts, cands=BIG_TILES)
    npt, nst = tp // tile, ts // tile
    return pl.pallas_call(
        functools.partial(_combine_kernel, tile=tile, n_prompt_tiles=npt),
        grid_spec=pltpu.PrefetchScalarGridSpec(
            num_scalar_prefetch=1,
            grid=(npt + nst,),
            in_specs=[
                pl.BlockSpec((tile, D_MODEL), lambda i, d: (i, 0)),
                pl.BlockSpec((tile, LANES), lambda i, d: (i, 0)),
                pl.BlockSpec((1, D_MODEL), lambda i, d: (0, 0)),
                pl.BlockSpec(memory_space=pl.ANY),
            ],
            out_specs=[
                pl.BlockSpec((tile, D_MODEL), lambda i, d: (jnp.minimum(i, npt - 1), 0)),
                pl.BlockSpec((tile, D_MODEL), lambda i, d: (jnp.maximum(i - npt, 0), 0)),
            ],
            scratch_shapes=[pltpu.VMEM((2, 2 * tile // SUBLANES, SUBLANES, D_MODEL), F32), pltpu.SemaphoreType.DMA((2,))],
        ),
        out_shape=[
            jax.ShapeDtypeStruct((tp, D_MODEL), F32),
            jax.ShapeDtypeStruct((ts, D_MODEL), F32),
        ],
        compiler_params=_cparams(("arbitrary",)),
        name="moe_combine",
    )(dest_flat, x1, route, norm_w, ys)


def _layer(xp, xs, n_batch, seq, s_batch, s_seq, conv_state, dn_state, k_cache, v_cache,
           w_in, conv_w, a_log, dt_bias, dn_norm_w, sinks, rel_bias, w_out, norm_mix_w, norm_ffn_w,
           w_rg, b_rg, w_re, b_re, w_gate, w_up, w_down, norm_final_w):
    tp, ts = xp.shape[0], xs.shape[0]
    t_all = tp + ts
    row = lambda v: v.reshape(1, -1).astype(F32)

    o = np.cumsum((0, DN_QK_W, DN_QK_W, DN_V_W, DN_V_W, DN_HEADS, DN_HEADS, SW_HEADS * SW_HD, SW_KV_W, SW_KV_W,
                   D_MODEL, D_MODEL)).tolist()
    w_big = jnp.concatenate([w_in[:, o[0]:o[4]], w_in[:, o[6]:o[7]], w_in[:, o[9]:o[11]], w_in[:, o[7]:o[9]]],
                            axis=1).astype(BF16)
    w_small = jnp.pad(w_in[:, o[4]:o[6]], ((0, 0), (0, LANES - 2 * DN_HEADS))).astype(BF16)
    head_row = lambda v: jnp.pad(v.astype(F32), (DN_HEADS, LANES - 2 * DN_HEADS)).reshape(1, LANES)
    w_router = jnp.pad(jnp.concatenate([w_rg, w_re], axis=1),
                       ((0, 0), (0, LANES - N_GROUPS - N_EXPERTS))).astype(BF16)
    b_router = jnp.pad(jnp.concatenate([b_rg, b_re]).astype(F32), (0, LANES - N_GROUPS - N_EXPERTS)).reshape(1, LANES)

    conv0 = jnp.zeros((n_batch, DN_CONV - 1, DN_CONV_W), F32)
    proj, ba, conv_tail = _inproj(xp, xs, row(norm_mix_w), w_big, w_small, conv_w.astype(F32), conv0, seq)
    p_conv = conv_tail[:, SUBLANES - (DN_CONV - 1):, :]

    dn0 = jnp.zeros((n_batch, DN_HEADS, DN_DK, DN_DV), F32)
    oa_p, p_dn = _dn_prompt(proj, ba, head_row(a_log), head_row(dt_bias), row(dn_norm_w), dn0, n_batch, seq)
    oa_s, s_conv, s_dn = _dn_sample(proj, ba, conv_w.astype(F32), head_row(a_log), head_row(dt_bias), row(dn_norm_w),
                                    conv_state, dn_state, tp, s_batch, s_seq)

    n_cache = k_cache.shape[1]
    n_keys = -(-(n_cache + SUBLANES) // LANES) * LANES
    assert n_keys == 2 * SW_BLOCK, "the prompt and sample bias tables share one (rows, keys) array"
    dist_p = np.arange(SW_BLOCK)[:, None] - (np.arange(2 * SW_BLOCK)[None, :] - SW_BLOCK)
    dist_s = n_cache + np.arange(SUBLANES)[:, None] - np.arange(n_keys)[None, :]
    in_window = lambda dist: (dist >= 0) & (dist < WINDOW)
    has_prev = np.arange(2 * SW_BLOCK)[None, :] >= SW_BLOCK
    buckets = np.concatenate([_masked_bucket(dist_p, in_window(dist_p) & has_prev),
                              _masked_bucket(dist_p, in_window(dist_p)),
                              _masked_bucket(dist_s, in_window(dist_s))], axis=0)
    bias = _relbias(rel_bias.astype(F32), jnp.asarray(buckets))
    mixed_p, p_k, p_v = _swa_prompt(proj, sinks.astype(F32), bias, oa_p, n_batch, seq)
    sink_rows = jnp.broadcast_to(jnp.repeat(sinks.astype(F32).reshape(SW_KV_HEADS, SW_GROUP), SUBLANES, axis=1)[:, :, None],
                                 (SW_KV_HEADS, SW_GROUP * SUBLANES, n_keys))
    mixed_s, s_k, s_v = _swa_sample(proj, k_cache.reshape(s_batch, n_cache, SW_KV_W),
                                    v_cache.reshape(s_batch, n_cache, SW_KV_W), bias, 2 * SW_BLOCK, sink_rows, oa_s,
                                    tp, s_batch, s_seq)

    x1, h2, route = _mix(xp, xs, mixed_p, mixed_s, w_out.astype(BF16), row(norm_ffn_w), w_router, b_router)

    n_blocks = -(-2 * t_all // MOE_BLOCK) + N_EXPERTS
    dest, meta = _rank(route, _token_tile(t_all, cands=BIG_TILES), MOE_BLOCK, n_blocks)
    dest_flat = dest[:, :2].reshape(-1)
    block_expert = meta[:n_blocks, 0]
    n_used = meta[0:1, 1]
    n_valid = meta[:n_blocks, 2]
    ys = _experts(block_expert, n_used, n_valid, dest_flat, h2, w_gate, w_up, w_down, MOE_BLOCK)
    y_p, y_s = _combine(dest_flat, x1, route, row(norm_final_w), ys, tp, ts)

    kv_shape = (n_batch, WINDOW, SW_KV_HEADS, SW_HD)
    return (y_p, y_s, p_conv, p_dn, p_k.reshape(kv_shape), p_v.reshape(kv_shape), s_conv, s_dn,
            s_k.reshape(k_cache.shape), s_v.reshape(v_cache.shape))


def kernel(x_prompt, x_sample, state_dn_conv, state_dn, cache_swa_k, cache_swa_v, w_in, conv_w, a_log, dt_bias, dn_norm_w, sinks, rel_bias, w_out, norm_mix_w, norm_ffn_w, w_router_group, b_router_group, w_router_expert, b_router_expert, w_gate, w_up, w_down, norm_final_w):
    depth = w_in.shape[0]
    assert depth == 1, "the final-norm fusion below assumes a single layer"
    n_batch, seq, _ = x_prompt.shape
    s_batch, s_seq, _ = x_sample.shape
    outs = _layer(x_prompt.reshape(-1, D_MODEL), x_sample.reshape(-1, D_MODEL), n_batch, seq, s_batch, s_seq,
                  state_dn_conv[0], state_dn[0], cache_swa_k[0], cache_swa_v[0],
                  w_in[0], conv_w[0], a_log[0], dt_bias[0], dn_norm_w[0], sinks[0], rel_bias,
                  w_out[0], norm_mix_w[0], norm_ffn_w[0], w_router_group[0], b_router_group[0],
                  w_router_expert[0], b_router_expert[0], w_gate[0], w_up[0], w_down[0], norm_final_w)
    y_p, y_s, p_conv, p_dn, p_k, p_v, s_conv, s_dn, s_k, s_v = outs
    return (y_p.reshape(x_prompt.shape), y_s.reshape(x_sample.shape), p_conv[None], p_dn[None], p_k[None], p_v[None],
            s_conv[None], s_dn[None], s_k[None], s_v[None])
```

```python
import functools
import math

import jax
import jax.numpy as jnp
import numpy as np
from jax import lax
from jax.experimental import pallas as pl
from jax.experimental.pallas import tpu as pltpu

F32 = jnp.float32
BF16 = jnp.bfloat16
I32 = jnp.int32

D_MODEL = 1024
DN_HEADS = 8
DN_DK = 128
DN_DV = 128
DN_CONV = 4
DN_CHUNK = 64
DN_QK_W = DN_HEADS * DN_DK
DN_V_W = DN_HEADS * DN_DV
DN_CONV_W = 2 * DN_QK_W + DN_V_W
SW_HEADS = 16
SW_KV_HEADS = 2
SW_GROUP = SW_HEADS // SW_KV_HEADS
SW_HD = 64
SW_KV_W = SW_KV_HEADS * SW_HD
WINDOW = 128
SW_BLOCK = 128
REL_BUCKETS = 32
REL_MAX_DIST = 128
N_GROUPS = 8
EXP_PER_GROUP = 8
N_EXPERTS = N_GROUPS * EXP_PER_GROUP
D_EXPERT = 256
MOE_BLOCK = 304
EPS = 1e-6

LANES = 128
SUBLANES = 8
VMEM_LIMIT = 56 * 1024 * 1024

COL_QKV = 0
COL_Z = 3072
COL_SQ = 4096
COL_GA = 5120
COL_GB = 6144
COL_SK = 7168
COL_SV = 7296
PROJ_W = 7424
PROJ_CHUNK = 512


def _cparams(sem):
    return pltpu.CompilerParams(dimension_semantics=sem, vmem_limit_bytes=VMEM_LIMIT)


def _sigmoid(x):
    return 0.5 * jnp.tanh(0.5 * x) + 0.5


def _dot(a, b):
    return jnp.dot(a.astype(BF16), b.astype(BF16), preferred_element_type=F32)


def _dot_nt(a, b):
    return lax.dot_general(a.astype(BF16), b.astype(BF16), (((1,), (1,)), ((), ())), preferred_element_type=F32)


def _dot_tn(a, b):
    return lax.dot_general(a.astype(BF16), b.astype(BF16), (((0,), (0,)), ((), ())), preferred_element_type=F32)


def _dot_exact(a, b):
    return jnp.dot(a, b, precision=lax.Precision.HIGHEST, preferred_element_type=F32)


def _lane_iota(shape):
    return lax.broadcasted_iota(I32, shape, len(shape) - 1)


def _row_iota(shape):
    return lax.broadcasted_iota(I32, shape, len(shape) - 2)


def _col(x, j):
    return jnp.sum(jnp.where(_lane_iota(x.shape) == j, x, 0.0), axis=-1, keepdims=True)


BIG_TILES = (512, 256, 128, 64, 32, 16, 8)


def _token_tile(*sizes, cands=BIG_TILES[1:]):
    for t in cands:
        if all(s % t == 0 for s in sizes):
            return t
    raise ValueError(f"token counts {sizes} need a common tile that is a multiple of 8")


def _inproj_kernel(xp_ref, xs_ref, nw_ref, wb_ref, ws_ref, convw_ref, conv0_ref, proj_ref, ba_ref, tail_ref, cbuf_ref,
                   *, n_prompt_tiles, tiles_per_seq):
    i = pl.program_id(0)
    tm = xp_ref.shape[0]
    hist = SUBLANES - (DN_CONV - 1)

    def project(x_ref, conv):
        x = x_ref[...]
        h = (x * lax.rsqrt(jnp.mean(x * x, axis=-1, keepdims=True) + EPS) * nw_ref[...]).astype(BF16)
        ba_ref[...] = jnp.dot(h, ws_ref[...], preferred_element_type=F32)
        top = _row_iota((SUBLANES, PROJ_CHUNK))
        starts = list(range(0, PROJ_W, PROJ_CHUNK))
        if conv:
            with_conv = [c for c in starts if c + PROJ_CHUNK <= COL_QKV + DN_CONV_W]
            plain = [c for c in starts if c not in with_conv]
            starts = [c for pair in zip(with_conv, plain) for c in pair] + plain[len(with_conv):]
        for c0 in starts:
            c1 = min(c0 + PROJ_CHUNK, PROJ_W)
            cur = jnp.dot(h, wb_ref[:, c0:c1], preferred_element_type=F32)
            if conv and c1 <= COL_QKV + DN_CONV_W:
                prev = cbuf_ref[:, c0:c1]
                acc = cur * convw_ref[DN_CONV - 1:DN_CONV, c0:c1]
                for s in range(1, DN_CONV):
                    sh = pltpu.roll(cur, s, axis=0)
                    head = jnp.where(top < s, pltpu.roll(prev, s, axis=0), sh[:SUBLANES])
                    sh = jnp.concatenate([head, sh[SUBLANES:]], axis=0)
                    acc = acc + sh * convw_ref[DN_CONV - 1 - s:DN_CONV - s, c0:c1]
                cbuf_ref[:, c0:c1] = cur[tm - SUBLANES:]
                cur = acc * _sigmoid(acc)
            proj_ref[:, c0:c1] = cur

    @pl.when(i < n_prompt_tiles)
    def _():
        @pl.when(i % tiles_per_seq == 0)
        def _():
            cbuf_ref[...] = jnp.zeros(cbuf_ref.shape, F32)
            cbuf_ref[hist:SUBLANES, :] = conv0_ref[0]

        project(xp_ref, True)
        tail_ref[0] = cbuf_ref[...]

    @pl.when(i >= n_prompt_tiles)
    def _():
        project(xs_ref, False)


def _inproj(xp, xs, norm_w, w_big, w_small, conv_w, conv0, seq):
    tp, ts = xp.shape[0], xs.shape[0]
    tm = _token_tile(tp, ts, seq)
    assert COL_QKV == 0 and DN_CONV_W % PROJ_CHUNK == 0
    npt, nst = tp // tm, ts // tm
    tps = seq // tm
    const = lambda i: (0, 0)
    seq_of = lambda i: (jnp.minimum(i, npt - 1) // tps, 0, 0)
    return pl.pallas_call(
        functools.partial(_inproj_kernel, n_prompt_tiles=npt, tiles_per_seq=tps),
        grid=(npt + nst,),
        in_specs=[
            pl.BlockSpec((tm, D_MODEL), lambda i: (jnp.minimum(i, npt - 1), 0)),
            pl.BlockSpec((tm, D_MODEL), lambda i: (jnp.maximum(i - npt, 0), 0)),
            pl.BlockSpec((1, D_MODEL), const),
            pl.BlockSpec((D_MODEL, PROJ_W), const, pipeline_mode=pl.Buffered(1)),
            pl.BlockSpec((D_MODEL, LANES), const),
            pl.BlockSpec((DN_CONV, DN_CONV_W), const),
            pl.BlockSpec((1, DN_CONV - 1, DN_CONV_W), seq_of),
        ],
        out_specs=[
            pl.BlockSpec((tm, PROJ_W), lambda i: (i, 0)),
            pl.BlockSpec((tm, LANES), lambda i: (i, 0)),
            pl.BlockSpec((1, SUBLANES, DN_CONV_W), seq_of),
        ],
        out_shape=[
            jax.ShapeDtypeStruct((tp + ts, PROJ_W), F32),
            jax.ShapeDtypeStruct((tp + ts, LANES), F32),
            jax.ShapeDtypeStruct((tp // seq, SUBLANES, DN_CONV_W), F32),
        ],
        scratch_shapes=[pltpu.VMEM((SUBLANES, DN_CONV_W), F32)],
        compiler_params=_cparams(("arbitrary",)),
        name="inproj",
    )(xp, xs, norm_w, w_big, w_small, conv_w, conv0)


def _dn_core(groups, alog, dtb, nw, read_state, write_state, n_seg, seg_valid):
    rows = groups[0][0].shape[0]
    sr = rows // n_seg
    assert sr * n_seg == rows and sr & (sr - 1) == 0 and rows <= LANES
    seg_shift = sr.bit_length() - 1
    ri = _row_iota((rows, rows))
    ci = _lane_iota((rows, rows))
    incl = ri >= ci
    strict = ri > ci
    if n_seg > 1:
        same = (ri >> seg_shift) == (ci >> seg_shift)
        incl = incl & same
        strict = strict & same
    l_incl = incl.astype(F32)
    eye = (ri == ci).astype(F32)
    levels = max(1, math.ceil(math.log2(seg_valid)))

    beta_all, gsum_all, gtot_all, gsum_t = [], [], [], []
    for _, _, ba, _ in groups:
        b_all = _sigmoid(ba)
        sp = ba + dtb
        softplus = jnp.maximum(sp, 0.0) + jnp.log1p(jnp.exp(-jnp.abs(sp)))
        g_all = -jnp.exp(alog) * softplus
        if seg_valid < sr:
            live = (_row_iota((rows, LANES)) & (sr - 1)) < seg_valid
            b_all = jnp.where(live, b_all, 0.0)
            g_all = jnp.where(live, g_all, 0.0)
        gs = _dot_exact(l_incl, g_all)
        beta_all.append(b_all)
        gsum_all.append(gs)
        gtot_all.append(_dot_exact(same.astype(F32), g_all) if n_seg > 1 else gs[rows - 1:rows, :])
        padded = gs if rows == LANES else jnp.concatenate([gs, jnp.zeros((LANES - rows, LANES), F32)], axis=0)
        gsum_t.append(padded.T)

    probs = [(g, h) for g in range(len(groups)) for h in range(DN_HEADS)]
    segs = range(n_seg)
    q, k, v, kb, beta, gsum, gtot = {}, {}, {}, {}, {}, {}, {}
    for p in probs:
        g, h = p
        qkv = groups[g][0]
        qh = qkv[:, h * DN_DK:(h + 1) * DN_DK]
        kh = qkv[:, DN_QK_W + h * DN_DK:DN_QK_W + (h + 1) * DN_DK]
        v[p] = qkv[:, 2 * DN_QK_W + h * DN_DV:2 * DN_QK_W + (h + 1) * DN_DV]
        q[p] = qh * lax.rsqrt(jnp.sum(qh * qh, axis=-1, keepdims=True) + 1e-6) * (DN_DK ** -0.5)
        k[p] = kh * lax.rsqrt(jnp.sum(kh * kh, axis=-1, keepdims=True) + 1e-6)
        beta[p] = _col(beta_all[g], h)
        gsum[p] = _col(gsum_all[g], DN_HEADS + h)
        gtot[p] = _col(gtot_all[g], DN_HEADS + h)
        kb[p] = k[p] * beta[p]
    kq = {p: _dot_nt(jnp.concatenate([kb[p], q[p]], axis=0), k[p]) for p in probs}
    gamma = {(g, h): jnp.exp(jnp.where(incl, gsum[(g, h)] - gsum_t[g][DN_HEADS + h:DN_HEADS + h + 1, :rows], -jnp.inf))
             for g, h in probs}
    attn = {p: kq[p][rows:] * gamma[p] for p in probs}
    pw = {p: -jnp.where(strict, kq[p][:rows] * gamma[p], 0.0) for p in probs}
    t = {p: eye + pw[p] for p in probs}
    for _ in range(1, levels):
        pw = {p: _dot(pw[p], pw[p]) for p in probs}
        t = {p: t[p] + _dot(t[p], pw[p]) for p in probs}
    eg = {p: jnp.exp(gsum[p]) for p in probs}
    uw = {p: _dot(t[p], jnp.concatenate([v[p] * beta[p], kb[p] * eg[p]], axis=1)) for p in probs}
    qg = {p: q[p] * eg[p] for p in probs}
    state = {(p, s): read_state(p[0], s, p[1]) for p in probs for s in segs}
    wq = {(p, s): _dot(jnp.concatenate([uw[p][s * sr:(s + 1) * sr, DN_DV:], qg[p][s * sr:(s + 1) * sr]], axis=0),
                       state[(p, s)]) for p in probs for s in segs}
    join = lambda pieces: pieces[0] if len(pieces) == 1 else jnp.concatenate(pieces, axis=0)
    v_new = {p: uw[p][:, :DN_DV] - join([wq[(p, s)][:sr] for s in segs]) for p in probs}
    o = {p: join([wq[(p, s)][sr:] for s in segs]) + _dot(attn[p], v_new[p]) for p in probs}
    kd = {p: k[p] * jnp.exp(gtot[p] - gsum[p]) for p in probs}
    for p in probs:
        for s in segs:
            r0 = s * sr if n_seg > 1 else 0
            decay = jnp.exp(gtot[p][r0:r0 + 1, :])
            write_state(p[0], s, p[1],
                        state[(p, s)] * decay + _dot_tn(kd[p][s * sr:(s + 1) * sr], v_new[p][s * sr:(s + 1) * sr]))
    outs = []
    for g, (_, z, _, gate) in enumerate(groups):
        heads = []
        for h in range(DN_HEADS):
            oh = o[(g, h)]
            zz = z[:, h * DN_DV:(h + 1) * DN_DV]
            on = oh * lax.rsqrt(jnp.mean(oh * oh, axis=-1, keepdims=True) + EPS) * nw
            heads.append(on * (zz * _sigmoid(zz)) * _sigmoid(gate[:, h * DN_DV:(h + 1) * DN_DV]))
        outs.append(jnp.concatenate(heads, axis=1))
    return outs


def _dn_prompt_kernel(*refs, chunk, n_batch):
    nb = n_batch
    qkv_refs, z_refs, ba_refs, gate_refs = refs[0:nb], refs[nb:2 * nb], refs[2 * nb:3 * nb], refs[3 * nb:4 * nb]
    alog_ref, dtb_ref, nw_ref, s0_ref, o_ref, sout_ref = refs[4 * nb:]

    @pl.when(pl.program_id(0) == 0)
    def _():
        sout_ref[...] = s0_ref[...]

    groups = [(qkv_refs[b][...], z_refs[b][...], ba_refs[b][...], gate_refs[b][...]) for b in range(nb)]

    def read_state(g, s, h):
        return sout_ref[g, h]

    def write_state(g, s, h, val):
        sout_ref[g, h] = val

    outs = _dn_core(groups, alog_ref[...], dtb_ref[...], nw_ref[...], read_state, write_state, 1, chunk)
    for b in range(nb):
        o_ref[b] = outs[b]


def _dn_prompt(proj, ba, alog_row, dtb_row, dn_nw, s0, n_batch, seq):
    chunk = min(DN_CHUNK, seq)
    assert seq % chunk == 0 and chunk % SUBLANES == 0
    nc = seq // chunk
    const2 = lambda c: (0, 0)
    rows = lambda b, col: (lambda c: (b * nc + c, col))
    batches = range(n_batch)
    o, s_out = pl.pallas_call(
        functools.partial(_dn_prompt_kernel, chunk=chunk, n_batch=n_batch),
        grid=(nc,),
        in_specs=(
            [pl.BlockSpec((chunk, DN_CONV_W), rows(b, COL_QKV // DN_CONV_W)) for b in batches]
            + [pl.BlockSpec((chunk, DN_V_W), rows(b, COL_Z // DN_V_W)) for b in batches]
            + [pl.BlockSpec((chunk, LANES), rows(b, 0)) for b in batches]
            + [pl.BlockSpec((chunk, D_MODEL), rows(b, COL_GA // D_MODEL)) for b in batches]
            + [
                pl.BlockSpec((1, LANES), const2),
                pl.BlockSpec((1, LANES), const2),
                pl.BlockSpec((1, DN_DV), const2),
                pl.BlockSpec((n_batch, DN_HEADS, DN_DK, DN_DV), lambda c: (0, 0, 0, 0)),
            ]
        ),
        out_specs=[
            pl.BlockSpec((n_batch, chunk, DN_V_W), lambda c: (0, c, 0)),
            pl.BlockSpec((n_batch, DN_HEADS, DN_DK, DN_DV), lambda c: (0, 0, 0, 0)),
        ],
        out_shape=[
            jax.ShapeDtypeStruct((n_batch, seq, DN_V_W), F32),
            jax.ShapeDtypeStruct((n_batch, DN_HEADS, DN_DK, DN_DV), F32),
        ],
        compiler_params=_cparams(("arbitrary",)),
        name="dn_prompt",
    )(*([proj] * n_batch), *([proj] * n_batch), *([ba] * n_batch), *([proj] * n_batch), alog_row, dtb_row, dn_nw, s0)
    return o.reshape(n_batch * seq, DN_V_W), s_out


def _spread_rows(ref, n_bb, seq):
    per = SUBLANES // seq
    pieces = []
    for j in range(n_bb // per):
        x8 = ref[j * SUBLANES:(j + 1) * SUBLANES, :]
        for r in range(per):
            pieces.append(x8 if r == 0 else pltpu.roll(x8, SUBLANES - r * seq, axis=0))
    return pieces


def _gather_rows(pieces, seq):
    per = SUBLANES // seq
    rows = _row_iota(pieces[0].shape)
    tiles = []
    for j in range(len(pieces) // per):
        tile = pieces[j * per]
        for r in range(1, per):
            tile = jnp.where(rows >= r * seq, pltpu.roll(pieces[j * per + r], r * seq, axis=0), tile)
        tiles.append(tile)
    return jnp.concatenate(tiles, axis=0)


def _dn_sample_kernel(qkv_ref, z_ref, ba_ref, gate_ref, convw_ref, alog_ref, dtb_ref, nw_ref, conv0_ref, s0_ref,
                      o_ref, convout_ref, sout_ref, cbuf_ref, *, seq, n_bb):
    hist = SUBLANES - (DN_CONV - 1)
    spread = lambda ref: _spread_rows(ref, n_bb, seq)
    for bb, piece in enumerate(spread(qkv_ref)):
        cbuf_ref[bb, SUBLANES:2 * SUBLANES, :] = piece
    cbuf_ref[:, hist:SUBLANES, :] = conv0_ref[...]
    w = convw_ref[...]
    acc = cbuf_ref[:, hist:hist + SUBLANES, :] * w[0:1, :]
    for i in range(1, DN_CONV):
        acc = acc + cbuf_ref[:, hist + i:hist + i + SUBLANES, :] * w[i:i + 1, :]
    live = _row_iota(acc.shape) < seq
    qkv = jnp.where(live, acc * _sigmoid(acc), 0.0).reshape(n_bb * SUBLANES, DN_CONV_W)
    convout_ref[...] = cbuf_ref[:, SUBLANES + seq - (DN_CONV - 1):SUBLANES + seq, :]

    def read_state(g, s, h):
        return s0_ref[s, h]

    def write_state(g, s, h, val):
        sout_ref[s, h] = val

    group = (qkv, jnp.concatenate(spread(z_ref), axis=0), jnp.concatenate(spread(ba_ref), axis=0),
             jnp.concatenate(spread(gate_ref), axis=0))
    o = _dn_core([group], alog_ref[...], dtb_ref[...], nw_ref[...], read_state, write_state, n_bb, seq)[0]
    o_ref[...] = _gather_rows([o[bb * SUBLANES:(bb + 1) * SUBLANES] for bb in range(n_bb)], seq)


def _dn_sample(proj, ba, conv_w, alog_row, dtb_row, dn_nw, conv0, s0, row0, n_batch, seq):
    assert SUBLANES % seq == 0 and seq >= DN_CONV - 1
    n_bb = SUBLANES
    rows_in = n_bb * seq
    assert n_batch % n_bb == 0 and row0 % rows_in == 0
    rb0 = row0 // rows_in
    const1 = lambda i: (0, 0)
    return pl.pallas_call(
        functools.partial(_dn_sample_kernel, seq=seq, n_bb=n_bb),
        grid=(n_batch // n_bb,),
        in_specs=[
            pl.BlockSpec((rows_in, DN_CONV_W), lambda i: (rb0 + i, COL_QKV // DN_CONV_W)),
            pl.BlockSpec((rows_in, DN_V_W), lambda i: (rb0 + i, COL_Z // DN_V_W)),
            pl.BlockSpec((rows_in, LANES), lambda i: (rb0 + i, 0)),
            pl.BlockSpec((rows_in, D_MODEL), lambda i: (rb0 + i, COL_GA // D_MODEL)),
            pl.BlockSpec((DN_CONV, DN_CONV_W), const1),
            pl.BlockSpec((1, LANES), const1),
            pl.BlockSpec((1, LANES), const1),
            pl.BlockSpec((1, DN_DV), const1),
            pl.BlockSpec((n_bb, DN_CONV - 1, DN_CONV_W), lambda i: (i, 0, 0)),
            pl.BlockSpec((n_bb, DN_HEADS, DN_DK, DN_DV), lambda i: (i, 0, 0, 0)),
        ],
        out_specs=[
            pl.BlockSpec((rows_in, DN_V_W), lambda i: (i, 0)),
            pl.BlockSpec((n_bb, DN_CONV - 1, DN_CONV_W), lambda i: (i, 0, 0)),
            pl.BlockSpec((n_bb, DN_HEADS, DN_DK, DN_DV), lambda i: (i, 0, 0, 0)),
        ],
        out_shape=[
            jax.ShapeDtypeStruct((n_batch * seq, DN_V_W), F32),
            jax.ShapeDtypeStruct((n_batch, DN_CONV - 1, DN_CONV_W), F32),
            jax.ShapeDtypeStruct((n_batch, DN_HEADS, DN_DK, DN_DV), F32),
        ],
        scratch_shapes=[pltpu.VMEM((n_bb, 2 * SUBLANES, DN_CONV_W), F32)],
        compiler_params=_cparams(("arbitrary",)),
        name="dn_sample",
    )(proj, proj, ba, proj, conv_w, alog_row, dtb_row, dn_nw, conv0, s0)


def _masked_bucket(dist, ok):
    n = np.maximum(dist, 0)
    max_exact = REL_BUCKETS // 2
    large = max_exact + (np.log(np.maximum(n, 1).astype(np.float32) / max_exact)
                         / math.log(REL_MAX_DIST / max_exact) * (REL_BUCKETS - max_exact)).astype(np.int32)
    bucket = np.where(n < max_exact, n, np.minimum(large, REL_BUCKETS - 1))
    return np.where(ok, bucket, -1).astype(np.int32)


def _relbias_kernel(tab_ref, bucket_ref, o_ref):
    h = pl.program_id(0)
    bk = bucket_ref[...]
    acc = jnp.full(bk.shape, -jnp.inf, F32)
    for b in range(REL_BUCKETS):
        acc = jnp.where(bk == b, tab_ref[b * SW_HEADS + h], acc)
    o_ref[0] = acc


def _relbias(rel_table, bucket):
    nq, ns = bucket.shape
    return pl.pallas_call(
        _relbias_kernel,
        grid=(SW_HEADS,),
        in_specs=[
            pl.BlockSpec(memory_space=pltpu.SMEM),
            pl.BlockSpec((nq, ns), lambda h: (0, 0)),
        ],
        out_specs=pl.BlockSpec((1, nq, ns), lambda h: (h, 0, 0)),
        out_shape=jax.ShapeDtypeStruct((SW_HEADS, nq, ns), F32),
        compiler_params=_cparams(("arbitrary",)),
        name="relbias",
    )(rel_table.reshape(-1), bucket)


def _dup_halves(x):
    lo = _lane_iota(x.shape) < SW_HD
    xr = pltpu.roll(x, SW_HD, axis=1)
    return jnp.where(lo, x, xr).astype(BF16), jnp.where(lo, xr, x).astype(BF16)


def _sink_softmax_pv(s, sink, vv):
    m = jnp.maximum(jnp.max(s, axis=-1, keepdims=True), sink)
    p = jnp.exp(s - m)
    denom = jnp.sum(p, axis=-1, keepdims=True) + jnp.exp(sink - m)
    return _dot(p, vv) * (1.0 / denom)


def _swa_prompt_kernel(sinks_ref, q_ref, kc_ref, kp_ref, vc_ref, vp_ref, bias_ref, gate_ref, other_ref,
                       o_ref, klast_ref, vlast_ref, *, n_sub):
    i = pl.program_id(1)
    klast_ref[0] = kc_ref[(n_sub - 1) * SW_BLOCK:, :]
    vlast_ref[0] = vc_ref[(n_sub - 1) * SW_BLOCK:, :]
    kk_all = _dup_halves(jnp.concatenate([kp_ref[...], kc_ref[...]], axis=0))
    vv_all = _dup_halves(jnp.concatenate([vp_ref[...], vc_ref[...]], axis=0))
    lo = _lane_iota((SW_BLOCK, LANES)) < SW_HD
    for sub in range(n_sub):
        rows = slice(sub * SW_BLOCK, (sub + 1) * SW_BLOCK)
        keys = slice(sub * SW_BLOCK, (sub + 2) * SW_BLOCK)
        bias_rows = pl.ds(pl.multiple_of(jnp.minimum(i, 1) * SW_BLOCK, SW_BLOCK), SW_BLOCK) if sub == 0 \
            else pl.ds(SW_BLOCK, SW_BLOCK)
        for pair in range(SW_HEADS // 2):
            cols = slice(pair * LANES, (pair + 1) * LANES)
            qp = q_ref[rows, cols] * (SW_HD ** -0.5)
            outs = []
            for half in range(2):
                hq = 2 * pair + half
                kv = hq // SW_GROUP
                qm = jnp.where(lo if half == 0 else ~lo, qp, 0.0)
                s = _dot_nt(qm, kk_all[kv][keys]) + bias_ref[hq, bias_rows, :]
                outs.append(_sink_softmax_pv(s, sinks_ref[hq], vv_all[kv][keys]))
            o_ref[rows, cols] = other_ref[rows, cols] + _sigmoid(gate_ref[rows, cols]) * jnp.where(lo, outs[0], outs[1])


def _swa_prompt(proj, sinks, bias, other, n_batch, seq):
    assert seq % SW_BLOCK == 0 and WINDOW == SW_BLOCK
    nb = seq // SW_BLOCK
    n_sub = next(n for n in (4, 2, 1) if nb % n == 0)
    ns = nb // n_sub
    rows = n_sub * SW_BLOCK
    cur = lambda col: (lambda b, i: (b * ns + i, col))
    prev = lambda col: (lambda b, i: (b * nb + jnp.maximum(i * n_sub - 1, 0), col))
    return pl.pallas_call(
        functools.partial(_swa_prompt_kernel, n_sub=n_sub),
        grid=(n_batch, ns),
        in_specs=[
            pl.BlockSpec(memory_space=pltpu.SMEM),
            pl.BlockSpec((rows, SW_HEADS * SW_HD), cur(COL_SQ // (SW_HEADS * SW_HD))),
            pl.BlockSpec((rows, SW_KV_W), cur(COL_SK // SW_KV_W)),
            pl.BlockSpec((SW_BLOCK, SW_KV_W), prev(COL_SK // SW_KV_W)),
            pl.BlockSpec((rows, SW_KV_W), cur(COL_SV // SW_KV_W)),
            pl.BlockSpec((SW_BLOCK, SW_KV_W), prev(COL_SV // SW_KV_W)),
            pl.BlockSpec((SW_HEADS, 2 * SW_BLOCK, 2 * SW_BLOCK), lambda b, i: (0, 0, 0)),
            pl.BlockSpec((rows, D_MODEL), cur(COL_GB // D_MODEL)),
            pl.BlockSpec((rows, D_MODEL), lambda b, i: (b * ns + i, 0)),
        ],
        out_specs=[
            pl.BlockSpec((rows, SW_HEADS * SW_HD), lambda b, i: (b * ns + i, 0)),
            pl.BlockSpec((1, SW_BLOCK, SW_KV_W), lambda b, i: (b, 0, 0)),
            pl.BlockSpec((1, SW_BLOCK, SW_KV_W), lambda b, i: (b, 0, 0)),
        ],
        out_shape=[
            jax.ShapeDtypeStruct((n_batch * seq, SW_HEADS * SW_HD), F32),
            jax.ShapeDtypeStruct((n_batch, SW_BLOCK, SW_KV_W), F32),
            jax.ShapeDtypeStruct((n_batch, SW_BLOCK, SW_KV_W), F32),
        ],
        compiler_params=_cparams(("arbitrary", "arbitrary")),
        name="swa_prompt",
    )(sinks, proj, proj, proj, proj, proj, bias, proj, other)


def _swa_sample_kernel(q_ref, kn_ref, vn_ref, kc_ref, vc_ref, bias_ref, sink_ref, gate_ref, other_ref,
                       o_ref, ko_ref, vo_ref, kall_ref, vall_ref, *, seq, n_bb, n_cache):
    n_keys = kall_ref.shape[1]
    zeros_tail = jnp.zeros((n_bb, n_keys - n_cache - SUBLANES, LANES), F32)
    lo = _lane_iota((SUBLANES, LANES)) < SW_HD
    for ref, cache_ref, new_ref in ((kall_ref, kc_ref, kn_ref), (vall_ref, vc_ref, vn_ref)):
        ref[:, 0:n_cache, :] = cache_ref[...]
        for bb, piece in enumerate(_spread_rows(new_ref, n_bb, seq)):
            ref[bb, n_cache:n_cache + SUBLANES, :] = piece
        ref[:, n_cache + SUBLANES:, :] = zeros_tail
    ko_ref[...] = kall_ref[:, seq:seq + n_cache, :]
    vo_ref[...] = vall_ref[:, seq:seq + n_cache, :]
    outs = []
    for bb, q8 in enumerate(_spread_rows(q_ref, n_bb, seq)):
        kk = _dup_halves(kall_ref[bb])
        vv = _dup_halves(vall_ref[bb])
        pairs = []
        for kv in range(SW_KV_HEADS):
            pieces = []
            for g in range(SW_GROUP):
                hq = kv * SW_GROUP + g
                qp = q8[:, (hq // 2) * LANES:(hq // 2 + 1) * LANES]
                pieces.append(jnp.where(lo if hq % 2 == 0 else ~lo, qp, 0.0))
            qs = jnp.concatenate(pieces, axis=0) * (SW_HD ** -0.5)
            bias = jnp.concatenate([bias_ref[kv * SW_GROUP + g] for g in range(SW_GROUP)], axis=0)
            s = _dot_nt(qs, kk[kv]) + bias
            res = _sink_softmax_pv(s, _col(sink_ref[kv], 0), vv[kv])
            for g in range(0, SW_GROUP, 2):
                pairs.append(jnp.where(lo, res[g * SUBLANES:(g + 1) * SUBLANES],
                                       res[(g + 1) * SUBLANES:(g + 2) * SUBLANES]))
        outs.append(jnp.concatenate(pairs, axis=1))
    o_ref[...] = other_ref[...] + _sigmoid(gate_ref[...]) * _gather_rows(outs, seq)


def _swa_sample(proj, k_cache, v_cache, bias, bias_row0, sink_rows, other, row0, n_batch, seq):
    assert SUBLANES % seq == 0
    n_bb = SUBLANES
    rows_in = n_bb * seq
    n_cache = k_cache.shape[1]
    assert n_batch % n_bb == 0 and row0 % rows_in == 0 and n_cache % SUBLANES == 0
    n_keys = bias.shape[-1]
    rb0 = row0 // rows_in
    blk = lambda col: (lambda i: (rb0 + i, col))
    return pl.pallas_call(
        functools.partial(_swa_sample_kernel, seq=seq, n_bb=n_bb, n_cache=n_cache),
        grid=(n_batch // n_bb,),
        in_specs=[
            pl.BlockSpec((rows_in, SW_HEADS * SW_HD), blk(COL_SQ // (SW_HEADS * SW_HD))),
            pl.BlockSpec((rows_in, SW_KV_W), blk(COL_SK // SW_KV_W)),
            pl.BlockSpec((rows_in, SW_KV_W), blk(COL_SV // SW_KV_W)),
            pl.BlockSpec((n_bb, n_cache, SW_KV_W), lambda i: (i, 0, 0)),
            pl.BlockSpec((n_bb, n_cache, SW_KV_W), lambda i: (i, 0, 0)),
            pl.BlockSpec((SW_HEADS, SUBLANES, n_keys), lambda i: (0, bias_row0 // SUBLANES, 0)),
            pl.BlockSpec((SW_KV_HEADS, SW_GROUP * SUBLANES, n_keys), lambda i: (0, 0, 0)),
            pl.BlockSpec((rows_in, D_MODEL), blk(COL_GB // D_MODEL)),
            pl.BlockSpec((rows_in, D_MODEL), lambda i: (i, 0)),
        ],
        out_specs=[
            pl.BlockSpec((rows_in, SW_HEADS * SW_HD), lambda i: (i, 0)),
            pl.BlockSpec((n_bb, n_cache, SW_KV_W), lambda i: (i, 0, 0)),
            pl.BlockSpec((n_bb, n_cache, SW_KV_W), lambda i: (i, 0, 0)),
        ],
        out_shape=[
            jax.ShapeDtypeStruct((n_batch * seq, SW_HEADS * SW_HD), F32),
            jax.ShapeDtypeStruct(k_cache.shape, F32),
            jax.ShapeDtypeStruct(v_cache.shape, F32),
        ],
        scratch_shapes=[pltpu.VMEM((n_bb, n_keys, SW_KV_W), F32), pltpu.VMEM((n_bb, n_keys, SW_KV_W), F32)],
        compiler_params=_cparams(("arbitrary",)),
        name="swa_sample",
    )(proj, proj, proj, k_cache, v_cache, bias, sink_rows, proj, other)


def _mix_kernel(xp_ref, xs_ref, mp_ref, ms_ref, wo_ref, nw_ref, wr_ref, br_ref,
                x1_ref, h2_ref, route_ref, *, n_prompt_tiles):
    i = pl.program_id(0)

    def run(x_ref, mixed_ref):
        x1 = x_ref[...] + _dot(mixed_ref[...], wo_ref[...])
        x1_ref[...] = x1
        h2 = x1 * lax.rsqrt(jnp.mean(x1 * x1, axis=-1, keepdims=True) + EPS) * nw_ref[...]
        h2_ref[...] = h2
        logits = _dot(h2, wr_ref[...]) + br_ref[...]
        lane = _lane_iota(logits.shape)
        lanef = lane.astype(F32)
        big = float(2 * LANES)
        is_g = lane < N_GROUPS
        gl = jnp.where(is_g, logits, -jnp.inf)
        gmax = jnp.max(gl, axis=-1, keepdims=True)
        gval = 1.0 / jnp.sum(jnp.where(is_g, jnp.exp(gl - gmax), 0.0), axis=-1, keepdims=True)
        grp = jnp.min(jnp.where(gl == gmax, lanef, big), axis=-1, keepdims=True)
        e_grp = ((lane - N_GROUPS) >> 3).astype(F32)
        is_e = (lane >= N_GROUPS) & (lane < N_GROUPS + N_EXPERTS) & (e_grp == grp)
        el = jnp.where(is_e, logits, -jnp.inf)
        v1 = jnp.max(el, axis=-1, keepdims=True)
        i1 = jnp.min(jnp.where(el == v1, lanef, big), axis=-1, keepdims=True)
        el2 = jnp.where(lanef == i1, -jnp.inf, el)
        v2 = jnp.max(el2, axis=-1, keepdims=True)
        i2 = jnp.min(jnp.where(el2 == v2, lanef, big), axis=-1, keepdims=True)
        e2 = jnp.exp(v2 - v1)
        w1 = gval / (1.0 + e2)
        w2 = gval * e2 / (1.0 + e2)
        route_ref[...] = jnp.where(lane == 0, i1 - N_GROUPS,
                                   jnp.where(lane == 1, i2 - N_GROUPS,
                                             jnp.where(lane == 2, w1, jnp.where(lane == 3, w2, 0.0))))

    @pl.when(i < n_prompt_tiles)
    def _():
        run(xp_ref, mp_ref)

    @pl.when(i >= n_prompt_tiles)
    def _():
        run(xs_ref, ms_ref)


def _mix(xp, xs, mixed_p, mixed_s, w_out, norm_w, w_router, b_router):
    tp, ts = xp.shape[0], xs.shape[0]
    tm = _token_tile(tp, ts, cands=BIG_TILES)
    npt, nst = tp // tm, ts // tm
    const = lambda i: (0, 0)
    row = lambda i: (i, 0)
    return pl.pallas_call(
        functools.partial(_mix_kernel, n_prompt_tiles=npt),
        grid=(npt + nst,),
        in_specs=[
            pl.BlockSpec((tm, D_MODEL), lambda i: (jnp.minimum(i, npt - 1), 0)),
            pl.BlockSpec((tm, D_MODEL), lambda i: (jnp.maximum(i - npt, 0), 0)),
            pl.BlockSpec((tm, D_MODEL), lambda i: (jnp.minimum(i, npt - 1), 0)),
            pl.BlockSpec((tm, D_MODEL), lambda i: (jnp.maximum(i - npt, 0), 0)),
            pl.BlockSpec((D_MODEL, D_MODEL), const),
            pl.BlockSpec((1, D_MODEL), const),
            pl.BlockSpec((D_MODEL, LANES), const),
            pl.BlockSpec((1, LANES), const),
        ],
        out_specs=[
            pl.BlockSpec((tm, D_MODEL), row),
            pl.BlockSpec((tm, D_MODEL), row),
            pl.BlockSpec((tm, LANES), row),
        ],
        out_shape=[
            jax.ShapeDtypeStruct((tp + ts, D_MODEL), F32),
            jax.ShapeDtypeStruct((tp + ts, D_MODEL), F32),
            jax.ShapeDtypeStruct((tp + ts, LANES), F32),
        ],
        compiler_params=_cparams(("arbitrary",)),
        name="mix_router",
    )(xp, xs, mixed_p, mixed_s, w_out, norm_w, w_router, b_router)


def _rank_kernel(route_ref, dest_ref, meta_ref, rank_ref, cnt_ref, *, tile, blk):
    phase = pl.program_id(0)
    i = pl.program_id(1)
    shape = (tile, LANES)
    lane = _lane_iota(shape)
    lanef = lane.astype(F32)
    r = route_ref[...]
    oh0 = lanef == _col(r, 0)
    oh1 = lanef == _col(r, 1)
    rows = pl.ds(pl.multiple_of(i * tile, tile), tile)

    @pl.when(phase == 0)
    def _():
        @pl.when(i == 0)
        def _():
            cnt_ref[...] = jnp.zeros(cnt_ref.shape, F32)

        oh = jnp.where(oh0 | oh1, 1.0, 0.0)
        tri = jnp.where(_row_iota((tile, tile)) > _lane_iota((tile, tile)), 1.0, 0.0)
        before = _dot(tri, oh) + cnt_ref[0:1, :]
        rank0 = jnp.sum(jnp.where(oh0, before, 0.0), axis=-1, keepdims=True)
        rank1 = jnp.sum(jnp.where(oh1, before, 0.0), axis=-1, keepdims=True)
        rank_ref[rows, :] = jnp.where(lane == 0, rank0, jnp.where(lane == 1, rank1, 0.0))
        cnt_ref[0:1, :] = cnt_ref[0:1, :] + jnp.sum(oh, axis=0, keepdims=True)

    @pl.when(phase == 1)
    def _():
        cnt = cnt_ref[0:1, :]
        padded = jnp.floor((cnt + (blk - 1)) / blk) * blk
        before_lane = jnp.where(_row_iota((LANES, LANES)) < _lane_iota((LANES, LANES)), 1.0, 0.0)
        start = _dot_exact(jnp.broadcast_to(padded, (SUBLANES, LANES)), before_lane)[0:1, :]
        rk = rank_ref[rows, :]
        d0 = jnp.sum(jnp.where(oh0, start, 0.0), axis=-1, keepdims=True) + _col(rk, 0)
        d1 = jnp.sum(jnp.where(oh1, start, 0.0), axis=-1, keepdims=True) + _col(rk, 1)
        dest_ref[...] = jnp.where(lane == 0, d0, jnp.where(lane == 1, d1, 0.0)).astype(I32)

        @pl.when(i == 0)
        def _():
            end = start + padded
            mshape = meta_ref.shape
            blk_start = (_row_iota(mshape) * blk).astype(F32)
            hit = (_lane_iota(mshape) < N_EXPERTS) & (end <= blk_start)
            be = jnp.minimum(jnp.sum(jnp.where(hit, 1.0, 0.0), axis=-1, keepdims=True), N_EXPERTS - 1.0)
            n_used = _col(end, N_EXPERTS - 1) / blk
            ml = _lane_iota(mshape)
            mine = ml.astype(F32) == be
            seg_start = jnp.sum(jnp.where(mine, start, 0.0), axis=-1, keepdims=True)
            seg_count = jnp.sum(jnp.where(mine, cnt, 0.0), axis=-1, keepdims=True)
            n_valid = jnp.clip(seg_count - (blk_start[:, 0:1] - seg_start), 0.0, float(blk))
            meta_ref[...] = jnp.where(ml == 0, be, jnp.where(ml == 1, n_used,
                                                              jnp.where(ml == 2, n_valid, 0.0))).astype(I32)


def _rank(route, tile, blk, n_blocks):
    t = route.shape[0]
    nt = t // tile
    nbp = -(-n_blocks // SUBLANES) * SUBLANES
    return pl.pallas_call(
        functools.partial(_rank_kernel, tile=tile, blk=blk),
        grid=(2, nt),
        in_specs=[pl.BlockSpec((tile, LANES), lambda p, i: (i, 0))],
        out_specs=[
            pl.BlockSpec((tile, LANES), lambda p, i: (i * p, 0)),
            pl.BlockSpec((nbp, LANES), lambda p, i: (0, 0)),
        ],
        out_shape=[
            jax.ShapeDtypeStruct((t, LANES), I32),
            jax.ShapeDtypeStruct((nbp, LANES), I32),
        ],
        scratch_shapes=[pltpu.VMEM((t, LANES), F32), pltpu.VMEM((SUBLANES, LANES), F32)],
        compiler_params=_cparams(("arbitrary", "arbitrary")),
        name="moe_rank",
    )(route)


def _row_copy(src, src_row, dst, group, sub, sem):
    return pltpu.make_async_copy(src.at[pl.ds(src_row, 1)], dst.at[group, pl.ds(sub, 1)], sem)


def _last_used(i, nu_ref):
    return jnp.minimum(i, jnp.maximum(nu_ref[0] - 1, 0))


def _for_rows(n, body):
    def group(g, carry):
        for u in range(SUBLANES):
            body(g, u)
        return carry

    def single(t, carry):
        body(t // SUBLANES, t % SUBLANES)
        return carry

    n_groups = n // SUBLANES
    lax.fori_loop(0, n_groups, group, 0)
    lax.fori_loop(n_groups * SUBLANES, n, single, 0)


def _expert_kernel(be_ref, nu_ref, nv_ref, dest_ref, h2_ref, wg_ref, wu_ref, wd_ref, y_ref,
                   xbuf_ref, wgu_ref, wdn_ref, inv_ref, sem, *, blk):
    i = pl.program_id(0)
    n_used = nu_ref[0]
    used = i < n_used
    slot = i % 2
    blk_i = _last_used(i, nu_ref)
    fresh = (i == 0) | (be_ref[blk_i] != be_ref[jnp.maximum(blk_i - 1, 0)])

    def gather(block, into):
        def issue(g, u):
            tok = inv_ref[block * blk + g * SUBLANES + u]
            _row_copy(h2_ref, tok, xbuf_ref.at[into], g, u, sem.at[into]).start(priority=u % 2 if isinstance(u, int) else 0)

        _for_rows(nv_ref[block], issue)

    @pl.when(i == 0)
    def _():
        def place(a, carry):
            inv_ref[dest_ref[a]] = a >> 1
            return carry

        lax.fori_loop(0, dest_ref.shape[0], place, 0, unroll=16)
        xbuf_ref[...] = jnp.zeros(xbuf_ref.shape, F32)
        gather(0, 0)

    @pl.when(i + 1 < n_used)
    def _():
        gather(i + 1, 1 - slot)

    @pl.when(used & fresh)
    def _():
        wgu_ref[:, :D_EXPERT] = wg_ref[0].astype(BF16)
        wgu_ref[:, D_EXPERT:] = wu_ref[0].astype(BF16)
        wdn_ref[...] = wd_ref[0].astype(BF16)

    @pl.when(used)
    def _():
        _for_rows(nv_ref[i], lambda g, u: _row_copy(h2_ref, 0, xbuf_ref.at[slot], 0, 0, sem.at[slot]).wait())
        x = xbuf_ref[slot].reshape(blk, D_MODEL)
        gu = jnp.dot(x.astype(BF16), wgu_ref[...], preferred_element_type=F32)
        g = gu[:, :D_EXPERT]
        hidden = (g * _sigmoid(g) * gu[:, D_EXPERT:]).astype(BF16)
        y_ref[...] = jnp.dot(hidden, wdn_ref[...], preferred_element_type=F32)

    @pl.when(jnp.logical_not(used))
    def _():
        y_ref[...] = jnp.zeros(y_ref.shape, F32)


def _experts(block_expert, n_used, n_valid, dest_flat, h2, w_gate, w_up, w_down, blk):
    n_blocks = block_expert.shape[0]
    wsel = lambda i, be, nu, nv, de: (be[_last_used(i, nu)], 0, 0)
    return pl.pallas_call(
        functools.partial(_expert_kernel, blk=blk),
        grid_spec=pltpu.PrefetchScalarGridSpec(
            num_scalar_prefetch=4,
            grid=(n_blocks,),
            in_specs=[
                pl.BlockSpec(memory_space=pl.ANY),
                pl.BlockSpec((1, D_MODEL, D_EXPERT), wsel),
                pl.BlockSpec((1, D_MODEL, D_EXPERT), wsel),
                pl.BlockSpec((1, D_EXPERT, D_MODEL), wsel),
            ],
            out_specs=pl.BlockSpec((blk, D_MODEL), lambda i, be, nu, nv, de: (i, 0)),
            scratch_shapes=[
                pltpu.VMEM((2, blk // SUBLANES, SUBLANES, D_MODEL), F32),
                pltpu.VMEM((D_MODEL, 2 * D_EXPERT), BF16),
                pltpu.VMEM((D_EXPERT, D_MODEL), BF16),
                pltpu.SMEM((n_blocks * blk,), I32),
                pltpu.SemaphoreType.DMA((2,)),
            ],
        ),
        out_shape=jax.ShapeDtypeStruct((n_blocks * blk, D_MODEL), F32),
        compiler_params=_cparams(("arbitrary",)),
        name="moe_experts",
    )(block_expert, n_used, n_valid, dest_flat, h2, w_gate, w_up, w_down)


def _combine_kernel(dest_ref, x1_ref, route_ref, nw_ref, ys_ref, yp_ref, ysm_ref, ybuf_ref, sem,
                    *, tile, n_prompt_tiles):
    i = pl.program_id(0)
    slot = i % 2

    groups = tile // SUBLANES

    def gather(step, into):
        def issue(g, carry):
            for u in range(SUBLANES):
                for k in range(2):
                    slot_row = dest_ref[2 * step * tile + g * (2 * SUBLANES) + 2 * u + k]
                    _row_copy(ys_ref, slot_row, ybuf_ref.at[into], k * groups + g, u,
                              sem.at[into]).start(priority=k)
            return carry

        lax.fori_loop(0, groups, issue, 0)

    @pl.when(i == 0)
    def _():
        gather(0, 0)

    @pl.when(i + 1 < pl.num_programs(0))
    def _():
        gather(i + 1, 1 - slot)

    def drain(t, carry):
        _row_copy(ys_ref, 0, ybuf_ref.at[slot], 0, 0, sem.at[slot]).wait()
        return carry

    lax.fori_loop(0, 2 * tile, drain, 0, unroll=8)
    r = route_ref[...]
    y = (ybuf_ref[slot, 0:groups].reshape(tile, D_MODEL) * _col(r, 2)
         + ybuf_ref[slot, groups:2 * groups].reshape(tile, D_MODEL) * _col(r, 3))
    x2 = x1_ref[...] + y
    out = x2 * lax.rsqrt(jnp.mean(x2 * x2, axis=-1, keepdims=True) + EPS) * nw_ref[...]

    @pl.when(i < n_prompt_tiles)
    def _():
        yp_ref[...] = out

    @pl.when(i >= n_prompt_tiles)
    def _():
        ysm_ref[...] = out


def _combine(dest_flat, x1, route, norm_w, ys, tp, ts):
    tile = _token_tile(tp, ts, cands=BIG_TILES)
    npt, nst = tp // tile, ts // tile
    return pl.pallas_call(
        functools.partial(_combine_kernel, tile=tile, n_prompt_tiles=npt),
        grid_spec=pltpu.PrefetchScalarGridSpec(
            num_scalar_prefetch=1,
            grid=(npt + nst,),
            in_specs=[
                pl.BlockSpec((tile, D_MODEL), lambda i, d: (i, 0)),
                pl.BlockSpec((tile, LANES), lambda i, d: (i, 0)),
                pl.BlockSpec((1, D_MODEL), lambda i, d: (0, 0)),
                pl.BlockSpec(memory_space=pl.ANY),
            ],
            out_specs=[
                pl.BlockSpec((tile, D_MODEL), lambda i, d: (jnp.minimum(i, npt - 1), 0)),
                pl.BlockSpec((tile, D_MODEL), lambda i, d: (jnp.maximum(i - npt, 0), 0)),
            ],
            scratch_shapes=[pltpu.VMEM((2, 2 * tile // SUBLANES, SUBLANES, D_MODEL), F32), pltpu.SemaphoreType.DMA((2,))],
        ),
        out_shape=[
            jax.ShapeDtypeStruct((tp, D_MODEL), F32),
            jax.ShapeDtypeStruct((ts, D_MODEL), F32),
        ],
        compiler_params=_cparams(("arbitrary",)),
        name="moe_combine",
    )(dest_flat, x1, route, norm_w, ys)


def _layer(xp, xs, n_batch, seq, s_batch, s_seq, conv_state, dn_state, k_cache, v_cache,
           w_in, conv_w, a_log, dt_bias, dn_norm_w, sinks, rel_bias, w_out, norm_mix_w, norm_ffn_w,
           w_rg, b_rg, w_re, b_re, w_gate, w_up, w_down, norm_final_w):
    tp, ts = xp.shape[0], xs.shape[0]
    t_all = tp + ts
    row = lambda v: v.reshape(1, -1).astype(F32)

    o = np.cumsum((0, DN_QK_W, DN_QK_W, DN_V_W, DN_V_W, DN_HEADS, DN_HEADS, SW_HEADS * SW_HD, SW_KV_W, SW_KV_W,
                   D_MODEL, D_MODEL)).tolist()
    w_big = jnp.concatenate([w_in[:, o[0]:o[4]], w_in[:, o[6]:o[7]], w_in[:, o[9]:o[11]], w_in[:, o[7]:o[9]]],
                            axis=1).astype(BF16)
    w_small = jnp.pad(w_in[:, o[4]:o[6]], ((0, 0), (0, LANES - 2 * DN_HEADS))).astype(BF16)
    head_row = lambda v: jnp.pad(v.astype(F32), (DN_HEADS, LANES - 2 * DN_HEADS)).reshape(1, LANES)
    w_router = jnp.pad(jnp.concatenate([w_rg, w_re], axis=1),
                       ((0, 0), (0, LANES - N_GROUPS - N_EXPERTS))).astype(BF16)
    b_router = jnp.pad(jnp.concatenate([b_rg, b_re]).astype(F32), (0, LANES - N_GROUPS - N_EXPERTS)).reshape(1, LANES)

    conv0 = jnp.zeros((n_batch, DN_CONV - 1, DN_CONV_W), F32)
    proj, ba, conv_tail = _inproj(xp, xs, row(norm_mix_w), w_big, w_small, conv_w.astype(F32), conv0, seq)
    p_conv = conv_tail[:, SUBLANES - (DN_CONV - 1):, :]

    dn0 = jnp.zeros((n_batch, DN_HEADS, DN_DK, DN_DV), F32)
    oa_p, p_dn = _dn_prompt(proj, ba, head_row(a_log), head_row(dt_bias), row(dn_norm_w), dn0, n_batch, seq)
    oa_s, s_conv, s_dn = _dn_sample(proj, ba, conv_w.astype(F32), head_row(a_log), head_row(dt_bias), row(dn_norm_w),
                                    conv_state, dn_state, tp, s_batch, s_seq)

    n_cache = k_cache.shape[1]
    n_keys = -(-(n_cache + SUBLANES) // LANES) * LANES
    assert n_keys == 2 * SW_BLOCK, "the prompt and sample bias tables share one (rows, keys) array"
    dist_p = np.arange(SW_BLOCK)[:, None] - (np.arange(2 * SW_BLOCK)[None, :] - SW_BLOCK)
    dist_s = n_cache + np.arange(SUBLANES)[:, None] - np.arange(n_keys)[None, :]
    in_window = lambda dist: (dist >= 0) & (dist < WINDOW)
    has_prev = np.arange(2 * SW_BLOCK)[None, :] >= SW_BLOCK
    buckets = np.concatenate([_masked_bucket(dist_p, in_window(dist_p) & has_prev),
                              _masked_bucket(dist_p, in_window(dist_p)),
                              _masked_bucket(dist_s, in_window(dist_s))], axis=0)
    bias = _relbias(rel_bias.astype(F32), jnp.asarray(buckets))
    mixed_p, p_k, p_v = _swa_prompt(proj, sinks.astype(F32), bias, oa_p, n_batch, seq)
    sink_rows = jnp.broadcast_to(jnp.repeat(sinks.astype(F32).reshape(SW_KV_HEADS, SW_GROUP), SUBLANES, axis=1)[:, :, None],
                                 (SW_KV_HEADS, SW_GROUP * SUBLANES, n_keys))
    mixed_s, s_k, s_v = _swa_sample(proj, k_cache.reshape(s_batch, n_cache, SW_KV_W),
                                    v_cache.reshape(s_batch, n_cache, SW_KV_W), bias, 2 * SW_BLOCK, sink_rows, oa_s,
                                    tp, s_batch, s_seq)

    x1, h2, route = _mix(xp, xs, mixed_p, mixed_s, w_out.astype(BF16), row(norm_ffn_w), w_router, b_router)

    n_blocks = -(-2 * t_all // MOE_BLOCK) + N_EXPERTS
    dest, meta = _rank(route, _token_tile(t_all, cands=BIG_TILES), MOE_BLOCK, n_blocks)
    dest_flat = dest[:, :2].reshape(-1)
    block_expert = meta[:n_blocks, 0]
    n_used = meta[0:1, 1]
    n_valid = meta[:n_blocks, 2]
    ys = _experts(block_expert, n_used, n_valid, dest_flat, h2, w_gate, w_up, w_down, MOE_BLOCK)
    y_p, y_s = _combine(dest_flat, x1, route, row(norm_final_w), ys, tp, ts)

    kv_shape = (n_batch, WINDOW, SW_KV_HEADS, SW_HD)
    return (y_p, y_s, p_conv, p_dn, p_k.reshape(kv_shape), p_v.reshape(kv_shape), s_conv, s_dn,
            s_k.reshape(k_cache.shape), s_v.reshape(v_cache.shape))


def kernel(x_prompt, x_sample, state_dn_conv, state_dn, cache_swa_k, cache_swa_v, w_in, conv_w, a_log, dt_bias, dn_norm_w, sinks, rel_bias, w_out, norm_mix_w, norm_ffn_w, w_router_group, b_router_group, w_router_expert, b_router_expert, w_gate, w_up, w_down, norm_final_w):
    depth = w_in.shape[0]
    assert depth == 1, "the final-norm fusion below assumes a single layer"
    n_batch, seq, _ = x_prompt.shape
    s_batch, s_seq, _ = x_sample.shape
    outs = _layer(x_prompt.reshape(-1, D_MODEL), x_sample.reshape(-1, D_MODEL), n_batch, seq, s_batch, s_seq,
                  state_dn_conv[0], state_dn[0], cache_swa_k[0], cache_swa_v[0],
                  w_in[0], conv_w[0], a_log[0], dt_bias[0], dn_norm_w[0], sinks[0], rel_bias,
                  w_out[0], norm_mix_w[0], norm_ffn_w[0], w_router_group[0], b_router_group[0],
                  w_router_expert[0], b_router_expert[0], w_gate[0], w_up[0], w_down[0], norm_final_w)
    y_p, y_s, p_conv, p_dn, p_k, p_v, s_conv, s_dn, s_k, s_v = outs
    return (y_p.reshape(x_prompt.shape), y_s.reshape(x_sample.shape), p_conv[None], p_dn[None], p_k[None], p_v[None],
            s_conv[None], s_dn[None], s_k[None], s_v[None])
```

```python
import functools
import math

import jax
import jax.numpy as jnp
import numpy as np
from jax import lax
from jax.experimental import pallas as pl
from jax.experimental.pallas import tpu as pltpu

F32 = jnp.float32
BF16 = jnp.bfloat16
I32 = jnp.int32

D_MODEL = 1024
DN_HEADS = 8
DN_DK = 128
DN_DV = 128
DN_CONV = 4
DN_CHUNK = 64
DN_QK_W = DN_HEADS * DN_DK
DN_V_W = DN_HEADS * DN_DV
DN_CONV_W = 2 * DN_QK_W + DN_V_W
SW_HEADS = 16
SW_KV_HEADS = 2
SW_GROUP = SW_HEADS // SW_KV_HEADS
SW_HD = 64
SW_KV_W = SW_KV_HEADS * SW_HD
WINDOW = 128
SW_BLOCK = 128
REL_BUCKETS = 32
REL_MAX_DIST = 128
N_GROUPS = 8
EXP_PER_GROUP = 8
N_EXPERTS = N_GROUPS * EXP_PER_GROUP
D_EXPERT = 256
MOE_BLOCK = 304
EPS = 1e-6

LANES = 128
SUBLANES = 8
VMEM_LIMIT = 56 * 1024 * 1024

COL_QKV = 0
COL_Z = 3072
COL_SQ = 4096
COL_GA = 5120
COL_GB = 6144
COL_SK = 7168
COL_SV = 7296
PROJ_W = 7424
PROJ_CHUNK = 512


def _cparams(sem):
    return pltpu.CompilerParams(dimension_semantics=sem, vmem_limit_bytes=VMEM_LIMIT)


def _sigmoid(x):
    return 0.5 * jnp.tanh(0.5 * x) + 0.5


def _dot(a, b):
    return jnp.dot(a.astype(BF16), b.astype(BF16), preferred_element_type=F32)


def _dot_nt(a, b):
    return lax.dot_general(a.astype(BF16), b.astype(BF16), (((1,), (1,)), ((), ())), preferred_element_type=F32)


def _dot_tn(a, b):
    return lax.dot_general(a.astype(BF16), b.astype(BF16), (((0,), (0,)), ((), ())), preferred_element_type=F32)


def _dot_exact(a, b):
    return jnp.dot(a, b, precision=lax.Precision.HIGHEST, preferred_element_type=F32)


def _lane_iota(shape):
    return lax.broadcasted_iota(I32, shape, len(shape) - 1)


def _row_iota(shape):
    return lax.broadcasted_iota(I32, shape, len(shape) - 2)


def _col(x, j):
    return jnp.sum(jnp.where(_lane_iota(x.shape) == j, x, 0.0), axis=-1, keepdims=True)


BIG_TILES = (512, 256, 128, 64, 32, 16, 8)


def _token_tile(*sizes, cands=BIG_TILES[1:]):
    for t in cands:
        if all(s % t == 0 for s in sizes):
            return t
    raise ValueError(f"token counts {sizes} need a common tile that is a multiple of 8")


def _inproj_kernel(xp_ref, xs_ref, nw_ref, wb_ref, ws_ref, convw_ref, conv0_ref, proj_ref, ba_ref, tail_ref, cbuf_ref,
                   *, n_prompt_tiles, tiles_per_seq):
    i = pl.program_id(0)
    tm = xp_ref.shape[0]
    hist = SUBLANES - (DN_CONV - 1)

    def project(x_ref, conv):
        x = x_ref[...]
        h = (x * lax.rsqrt(jnp.mean(x * x, axis=-1, keepdims=True) + EPS) * nw_ref[...]).astype(BF16)
        ba_ref[...] = jnp.dot(h, ws_ref[...], preferred_element_type=F32)
        top = _row_iota((SUBLANES, PROJ_CHUNK))
        starts = list(range(0, PROJ_W, PROJ_CHUNK))
        if conv:
            with_conv = [c for c in starts if c + PROJ_CHUNK <= COL_QKV + DN_CONV_W]
            plain = [c for c in starts if c not in with_conv]
            starts = [c for pair in zip(with_conv, plain) for c in pair] + plain[len(with_conv):]
        for c0 in starts:
            c1 = min(c0 + PROJ_CHUNK, PROJ_W)
            cur = jnp.dot(h, wb_ref[:, c0:c1], preferred_element_type=F32)
            if conv and c1 <= COL_QKV + DN_CONV_W:
                prev = cbuf_ref[:, c0:c1]
                acc = cur * convw_ref[DN_CONV - 1:DN_CONV, c0:c1]
                for s in range(1, DN_CONV):
                    sh = pltpu.roll(cur, s, axis=0)
                    head = jnp.where(top < s, pltpu.roll(prev, s, axis=0), sh[:SUBLANES])
                    sh = jnp.concatenate([head, sh[SUBLANES:]], axis=0)
                    acc = acc + sh * convw_ref[DN_CONV - 1 - s:DN_CONV - s, c0:c1]
                cbuf_ref[:, c0:c1] = cur[tm - SUBLANES:]
                cur = acc * _sigmoid(acc)
            proj_ref[:, c0:c1] = cur

    @pl.when(i < n_prompt_tiles)
    def _():
        @pl.when(i % tiles_per_seq == 0)
        def _():
            cbuf_ref[...] = jnp.zeros(cbuf_ref.shape, F32)
            cbuf_ref[hist:SUBLANES, :] = conv0_ref[0]

        project(xp_ref, True)
        tail_ref[0] = cbuf_ref[...]

    @pl.when(i >= n_prompt_tiles)
    def _():
        project(xs_ref, False)


def _inproj(xp, xs, norm_w, w_big, w_small, conv_w, conv0, seq):
    tp, ts = xp.shape[0], xs.shape[0]
    tm = _token_tile(tp, ts, seq)
    assert COL_QKV == 0 and DN_CONV_W % PROJ_CHUNK == 0
    npt, nst = tp // tm, ts // tm
    tps = seq // tm
    const = lambda i: (0, 0)
    seq_of = lambda i: (jnp.minimum(i, npt - 1) // tps, 0, 0)
    return pl.pallas_call(
        functools.partial(_inproj_kernel, n_prompt_tiles=npt, tiles_per_seq=tps),
        grid=(npt + nst,),
        in_specs=[
            pl.BlockSpec((tm, D_MODEL), lambda i: (jnp.minimum(i, npt - 1), 0)),
            pl.BlockSpec((tm, D_MODEL), lambda i: (jnp.maximum(i - npt, 0), 0)),
            pl.BlockSpec((1, D_MODEL), const),
            pl.BlockSpec((D_MODEL, PROJ_W), const, pipeline_mode=pl.Buffered(1)),
            pl.BlockSpec((D_MODEL, LANES), const),
            pl.BlockSpec((DN_CONV, DN_CONV_W), const),
            pl.BlockSpec((1, DN_CONV - 1, DN_CONV_W), seq_of),
        ],
        out_specs=[
            pl.BlockSpec((tm, PROJ_W), lambda i: (i, 0)),
            pl.BlockSpec((tm, LANES), lambda i: (i, 0)),
            pl.BlockSpec((1, SUBLANES, DN_CONV_W), seq_of),
        ],
        out_shape=[
            jax.ShapeDtypeStruct((tp + ts, PROJ_W), F32),
            jax.ShapeDtypeStruct((tp + ts, LANES), F32),
            jax.ShapeDtypeStruct((tp // seq, SUBLANES, DN_CONV_W), F32),
        ],
        scratch_shapes=[pltpu.VMEM((SUBLANES, DN_CONV_W), F32)],
        compiler_params=_cparams(("arbitrary",)),
        name="inproj",
    )(xp, xs, norm_w, w_big, w_small, conv_w, conv0)


def _dn_core(groups, alog, dtb, nw, read_state, write_state, n_seg, seg_valid):
    rows = groups[0][0].shape[0]
    sr = rows // n_seg
    assert sr * n_seg == rows and sr & (sr - 1) == 0 and rows <= LANES
    seg_shift = sr.bit_length() - 1
    ri = _row_iota((rows, rows))
    ci = _lane_iota((rows, rows))
    incl = ri >= ci
    strict = ri > ci
    if n_seg > 1:
        same = (ri >> seg_shift) == (ci >> seg_shift)
        incl = incl & same
        strict = strict & same
    l_incl = incl.astype(F32)
    eye = (ri == ci).astype(F32)
    levels = max(1, math.ceil(math.log2(seg_valid)))

    beta_all, gsum_all, gtot_all, gsum_t = [], [], [], []
    for _, _, ba, _ in groups:
        b_all = _sigmoid(ba)
        sp = ba + dtb
        softplus = jnp.maximum(sp, 0.0) + jnp.log1p(jnp.exp(-jnp.abs(sp)))
        g_all = -jnp.exp(alog) * softplus
        if seg_valid < sr:
            live = (_row_iota((rows, LANES)) & (sr - 1)) < seg_valid
            b_all = jnp.where(live, b_all, 0.0)
            g_all = jnp.where(live, g_all, 0.0)
        gs = _dot_exact(l_incl, g_all)
        beta_all.append(b_all)
        gsum_all.append(gs)
        gtot_all.append(_dot_exact(same.astype(F32), g_all) if n_seg > 1 else gs[rows - 1:rows, :])
        padded = gs if rows == LANES else jnp.concatenate([gs, jnp.zeros((LANES - rows, LANES), F32)], axis=0)
        gsum_t.append(padded.T)

    probs = [(g, h) for g in range(len(groups)) for h in range(DN_HEADS)]
    segs = range(n_seg)
    q, k, v, kb, beta, gsum, gtot = {}, {}, {}, {}, {}, {}, {}
    for p in probs:
        g, h = p
        qkv = groups[g][0]
        qh = qkv[:, h * DN_DK:(h + 1) * DN_DK]
        kh = qkv[:, DN_QK_W + h * DN_DK:DN_QK_W + (h + 1) * DN_DK]
        v[p] = qkv[:, 2 * DN_QK_W + h * DN_DV:2 * DN_QK_W + (h + 1) * DN_DV]
        q[p] = qh * lax.rsqrt(jnp.sum(qh * qh, axis=-1, keepdims=True) + 1e-6) * (DN_DK ** -0.5)
        k[p] = kh * lax.rsqrt(jnp.sum(kh * kh, axis=-1, keepdims=True) + 1e-6)
        beta[p] = _col(beta_all[g], h)
        gsum[p] = _col(gsum_all[g], DN_HEADS + h)
        gtot[p] = _col(gtot_all[g], DN_HEADS + h)
        kb[p] = k[p] * beta[p]
    kq = {p: _dot_nt(jnp.concatenate([kb[p], q[p]], axis=0), k[p]) for p in probs}
    gamma = {(g, h): jnp.exp(jnp.where(incl, gsum[(g, h)] - gsum_t[g][DN_HEADS + h:DN_HEADS + h + 1, :rows], -jnp.inf))
             for g, h in probs}
    attn = {p: kq[p][rows:] * gamma[p] for p in probs}
    pw = {p: -jnp.where(strict, kq[p][:rows] * gamma[p], 0.0) for p in probs}
    t = {p: eye + pw[p] for p in probs}
    for _ in range(1, levels):
        pw = {p: _dot(pw[p], pw[p]) for p in probs}
        t = {p: t[p] + _dot(t[p], pw[p]) for p in probs}
    eg = {p: jnp.exp(gsum[p]) for p in probs}
    uw = {p: _dot(t[p], jnp.concatenate([v[p] * beta[p], kb[p] * eg[p]], axis=1)) for p in probs}
    qg = {p: q[p] * eg[p] for p in probs}
    state = {(p, s): read_state(p[0], s, p[1]) for p in probs for s in segs}
    wq = {(p, s): _dot(jnp.concatenate([uw[p][s * sr:(s + 1) * sr, DN_DV:], qg[p][s * sr:(s + 1) * sr]], axis=0),
                       state[(p, s)]) for p in probs for s in segs}
    join = lambda pieces: pieces[0] if len(pieces) == 1 else jnp.concatenate(pieces, axis=0)
    v_new = {p: uw[p][:, :DN_DV] - join([wq[(p, s)][:sr] for s in segs]) for p in probs}
    o = {p: join([wq[(p, s)][sr:] for s in segs]) + _dot(attn[p], v_new[p]) for p in probs}
    kd = {p: k[p] * jnp.exp(gtot[p] - gsum[p]) for p in probs}
    for p in probs:
        for s in segs:
            r0 = s * sr if n_seg > 1 else 0
            decay = jnp.exp(gtot[p][r0:r0 + 1, :])
            write_state(p[0], s, p[1],
                        state[(p, s)] * decay + _dot_tn(kd[p][s * sr:(s + 1) * sr], v_new[p][s * sr:(s + 1) * sr]))
    outs = []
    for g, (_, z, _, gate) in enumerate(groups):
        heads = []
        for h in range(DN_HEADS):
            oh = o[(g, h)]
            zz = z[:, h * DN_DV:(h + 1) * DN_DV]
            on = oh * lax.rsqrt(jnp.mean(oh * oh, axis=-1, keepdims=True) + EPS) * nw
            heads.append(on * (zz * _sigmoid(zz)) * _sigmoid(gate[:, h * DN_DV:(h + 1) * DN_DV]))
        outs.append(jnp.concatenate(heads, axis=1))
    return outs


def _dn_prompt_kernel(*refs, chunk, n_batch):
    nb = n_batch
    qkv_refs, z_refs, ba_refs, gate_refs = refs[0:nb], refs[nb:2 * nb], refs[2 * nb:3 * nb], refs[3 * nb:4 * nb]
    alog_ref, dtb_ref, nw_ref, s0_ref, o_ref, sout_ref = refs[4 * nb:]

    @pl.when(pl.program_id(0) == 0)
    def _():
        sout_ref[...] = s0_ref[...]

    groups = [(qkv_refs[b][...], z_refs[b][...], ba_refs[b][...], gate_refs[b][...]) for b in range(nb)]

    def read_state(g, s, h):
        return sout_ref[g, h]

    def write_state(g, s, h, val):
        sout_ref[g, h] = val

    outs = _dn_core(groups, alog_ref[...], dtb_ref[...], nw_ref[...], read_state, write_state, 1, chunk)
    for b in range(nb):
        o_ref[b] = outs[b]


def _dn_prompt(proj, ba, alog_row, dtb_row, dn_nw, s0, n_batch, seq):
    chunk = min(DN_CHUNK, seq)
    assert seq % chunk == 0 and chunk % SUBLANES == 0
    nc = seq // chunk
    const2 = lambda c: (0, 0)
    rows = lambda b, col: (lambda c: (b * nc + c, col))
    batches = range(n_batch)
    o, s_out = pl.pallas_call(
        functools.partial(_dn_prompt_kernel, chunk=chunk, n_batch=n_batch),
        grid=(nc,),
        in_specs=(
            [pl.BlockSpec((chunk, DN_CONV_W), rows(b, COL_QKV // DN_CONV_W)) for b in batches]
            + [pl.BlockSpec((chunk, DN_V_W), rows(b, COL_Z // DN_V_W)) for b in batches]
            + [pl.BlockSpec((chunk, LANES), rows(b, 0)) for b in batches]
            + [pl.BlockSpec((chunk, D_MODEL), rows(b, COL_GA // D_MODEL)) for b in batches]
            + [
                pl.BlockSpec((1, LANES), const2),
                pl.BlockSpec((1, LANES), const2),
                pl.BlockSpec((1, DN_DV), const2),
                pl.BlockSpec((n_batch, DN_HEADS, DN_DK, DN_DV), lambda c: (0, 0, 0, 0)),
            ]
        ),
        out_specs=[
            pl.BlockSpec((n_batch, chunk, DN_V_W), lambda c: (0, c, 0)),
            pl.BlockSpec((n_batch, DN_HEADS, DN_DK, DN_DV), lambda c: (0, 0, 0, 0)),
        ],
        out_shape=[
            jax.ShapeDtypeStruct((n_batch, seq, DN_V_W), F32),
            jax.ShapeDtypeStruct((n_batch, DN_HEADS, DN_DK, DN_DV), F32),
        ],
        compiler_params=_cparams(("arbitrary",)),
        name="dn_prompt",
    )(*([proj] * n_batch), *([proj] * n_batch), *([ba] * n_batch), *([proj] * n_batch), alog_row, dtb_row, dn_nw, s0)
    return o.reshape(n_batch * seq, DN_V_W), s_out


def _spread_rows(ref, n_bb, seq):
    per = SUBLANES // seq
    pieces = []
    for j in range(n_bb // per):
        x8 = ref[j * SUBLANES:(j + 1) * SUBLANES, :]
        for r in range(per):
            pieces.append(x8 if r == 0 else pltpu.roll(x8, SUBLANES - r * seq, axis=0))
    return pieces


def _gather_rows(pieces, seq):
    per = SUBLANES // seq
    rows = _row_iota(pieces[0].shape)
    tiles = []
    for j in range(len(pieces) // per):
        tile = pieces[j * per]
        for r in range(1, per):
            tile = jnp.where(rows >= r * seq, pltpu.roll(pieces[j * per + r], r * seq, axis=0), tile)
        tiles.append(tile)
    return jnp.concatenate(tiles, axis=0)


def _dn_sample_kernel(qkv_ref, z_ref, ba_ref, gate_ref, convw_ref, alog_ref, dtb_ref, nw_ref, conv0_ref, s0_ref,
                      o_ref, convout_ref, sout_ref, cbuf_ref, *, seq, n_bb):
    hist = SUBLANES - (DN_CONV - 1)
    spread = lambda ref: _spread_rows(ref, n_bb, seq)
    for bb, piece in enumerate(spread(qkv_ref)):
        cbuf_ref[bb, SUBLANES:2 * SUBLANES, :] = piece
    cbuf_ref[:, hist:SUBLANES, :] = conv0_ref[...]
    w = convw_ref[...]
    acc = cbuf_ref[:, hist:hist + SUBLANES, :] * w[0:1, :]
    for i in range(1, DN_CONV):
        acc = acc + cbuf_ref[:, hist + i:hist + i + SUBLANES, :] * w[i:i + 1, :]
    live = _row_iota(acc.shape) < seq
    qkv = jnp.where(live, acc * _sigmoid(acc), 0.0).reshape(n_bb * SUBLANES, DN_CONV_W)
    convout_ref[...] = cbuf_ref[:, SUBLANES + seq - (DN_CONV - 1):SUBLANES + seq, :]

    def read_state(g, s, h):
        return s0_ref[s, h]

    def write_state(g, s, h, val):
        sout_ref[s, h] = val

    group = (qkv, jnp.concatenate(spread(z_ref), axis=0), jnp.concatenate(spread(ba_ref), axis=0),
             jnp.concatenate(spread(gate_ref), axis=0))
    o = _dn_core([group], alog_ref[...], dtb_ref[...], nw_ref[...], read_state, write_state, n_bb, seq)[0]
    o_ref[...] = _gather_rows([o[bb * SUBLANES:(bb + 1) * SUBLANES] for bb in range(n_bb)], seq)


def _dn_sample(proj, ba, conv_w, alog_row, dtb_row, dn_nw, conv0, s0, row0, n_batch, seq):
    assert SUBLANES % seq == 0 and seq >= DN_CONV - 1
    n_bb = SUBLANES
    rows_in = n_bb * seq
    assert n_batch % n_bb == 0 and row0 % rows_in == 0
    rb0 = row0 // rows_in
    const1 = lambda i: (0, 0)
    return pl.pallas_call(
        functools.partial(_dn_sample_kernel, seq=seq, n_bb=n_bb),
        grid=(n_batch // n_bb,),
        in_specs=[
            pl.BlockSpec((rows_in, DN_CONV_W), lambda i: (rb0 + i, COL_QKV // DN_CONV_W)),
            pl.BlockSpec((rows_in, DN_V_W), lambda i: (rb0 + i, COL_Z // DN_V_W)),
            pl.BlockSpec((rows_in, LANES), lambda i: (rb0 + i, 0)),
            pl.BlockSpec((rows_in, D_MODEL), lambda i: (rb0 + i, COL_GA // D_MODEL)),
            pl.BlockSpec((DN_CONV, DN_CONV_W), const1),
            pl.BlockSpec((1, LANES), const1),
            pl.BlockSpec((1, LANES), const1),
            pl.BlockSpec((1, DN_DV), const1),
            pl.BlockSpec((n_bb, DN_CONV - 1, DN_CONV_W), lambda i: (i, 0, 0)),
            pl.BlockSpec((n_bb, DN_HEADS, DN_DK, DN_DV), lambda i: (i, 0, 0, 0)),
        ],
        out_specs=[
            pl.BlockSpec((rows_in, DN_V_W), lambda i: (i, 0)),
            pl.BlockSpec((n_bb, DN_CONV - 1, DN_CONV_W), lambda i: (i, 0, 0)),
            pl.BlockSpec((n_bb, DN_HEADS, DN_DK, DN_DV), lambda i: (i, 0, 0, 0)),
        ],
        out_shape=[
            jax.ShapeDtypeStruct((n_batch * seq, DN_V_W), F32),
            jax.ShapeDtypeStruct((n_batch, DN_CONV - 1, DN_CONV_W), F32),
            jax.ShapeDtypeStruct((n_batch, DN_HEADS, DN_DK, DN_DV), F32),
        ],
        scratch_shapes=[pltpu.VMEM((n_bb, 2 * SUBLANES, DN_CONV_W), F32)],
        compiler_params=_cparams(("arbitrary",)),
        name="dn_sample",
    )(proj, proj, ba, proj, conv_w, alog_row, dtb_row, dn_nw, conv0, s0)


def _masked_bucket(dist, ok):
    n = np.maximum(dist, 0)
    max_exact = REL_BUCKETS // 2
    large = max_exact + (np.log(np.maximum(n, 1).astype(np.float32) / max_exact)
                         / math.log(REL_MAX_DIST / max_exact) * (REL_BUCKETS - max_exact)).astype(np.int32)
    bucket = np.where(n < max_exact, n, np.minimum(large, REL_BUCKETS - 1))
    return np.where(ok, bucket, -1).astype(np.int32)


def _relbias_kernel(tab_ref, bucket_ref, o_ref):
    h = pl.program_id(0)
    bk = bucket_ref[...]
    acc = jnp.full(bk.shape, -jnp.inf, F32)
    for b in range(REL_BUCKETS):
        acc = jnp.where(bk == b, tab_ref[b * SW_HEADS + h], acc)
    o_ref[0] = acc


def _relbias(rel_table, bucket):
    nq, ns = bucket.shape
    return pl.pallas_call(
        _relbias_kernel,
        grid=(SW_HEADS,),
        in_specs=[
            pl.BlockSpec(memory_space=pltpu.SMEM),
            pl.BlockSpec((nq, ns), lambda h: (0, 0)),
        ],
        out_specs=pl.BlockSpec((1, nq, ns), lambda h: (h, 0, 0)),
        out_shape=jax.ShapeDtypeStruct((SW_HEADS, nq, ns), F32),
        compiler_params=_cparams(("arbitrary",)),
        name="relbias",
    )(rel_table.reshape(-1), bucket)


def _dup_halves(x):
    lo = _lane_iota(x.shape) < SW_HD
    xr = pltpu.roll(x, SW_HD, axis=1)
    return jnp.where(lo, x, xr).astype(BF16), jnp.where(lo, xr, x).astype(BF16)


def _sink_softmax_pv(s, sink, vv):
    m = jnp.maximum(jnp.max(s, axis=-1, keepdims=True), sink)
    p = jnp.exp(s - m)
    denom = jnp.sum(p, axis=-1, keepdims=True) + jnp.exp(sink - m)
    return _dot(p, vv) * (1.0 / denom)


def _swa_prompt_kernel(sinks_ref, q_ref, kc_ref, kp_ref, vc_ref, vp_ref, bias_ref, gate_ref, other_ref,
                       o_ref, klast_ref, vlast_ref, *, n_sub):
    i = pl.program_id(1)
    klast_ref[0] = kc_ref[(n_sub - 1) * SW_BLOCK:, :]
    vlast_ref[0] = vc_ref[(n_sub - 1) * SW_BLOCK:, :]
    kk_all = _dup_halves(jnp.concatenate([kp_ref[...], kc_ref[...]], axis=0))
    vv_all = _dup_halves(jnp.concatenate([vp_ref[...], vc_ref[...]], axis=0))
    lo = _lane_iota((SW_BLOCK, LANES)) < SW_HD
    for sub in range(n_sub):
        rows = slice(sub * SW_BLOCK, (sub + 1) * SW_BLOCK)
        keys = slice(sub * SW_BLOCK, (sub + 2) * SW_BLOCK)
        bias_rows = pl.ds(pl.multiple_of(jnp.minimum(i, 1) * SW_BLOCK, SW_BLOCK), SW_BLOCK) if sub == 0 \
            else pl.ds(SW_BLOCK, SW_BLOCK)
        for pair in range(SW_HEADS // 2):
            cols = slice(pair * LANES, (pair + 1) * LANES)
            qp = q_ref[rows, cols] * (SW_HD ** -0.5)
            outs = []
            for half in range(2):
                hq = 2 * pair + half
                kv = hq // SW_GROUP
                qm = jnp.where(lo if half == 0 else ~lo, qp, 0.0)
                s = _dot_nt(qm, kk_all[kv][keys]) + bias_ref[hq, bias_rows, :]
                outs.append(_sink_softmax_pv(s, sinks_ref[hq], vv_all[kv][keys]))
            o_ref[rows, cols] = other_ref[rows, cols] + _sigmoid(gate_ref[rows, cols]) * jnp.where(lo, outs[0], outs[1])


def _swa_prompt(proj, sinks, bias, other, n_batch, seq):
    assert seq % SW_BLOCK == 0 and WINDOW == SW_BLOCK
    nb = seq // SW_BLOCK
    n_sub = next(n for n in (4, 2, 1) if nb % n == 0)
    ns = nb // n_sub
    rows = n_sub * SW_BLOCK
    cur = lambda col: (lambda b, i: (b * ns + i, col))
    prev = lambda col: (lambda b, i: (b * nb + jnp.maximum(i * n_sub - 1, 0), col))
    return pl.pallas_call(
        functools.partial(_swa_prompt_kernel, n_sub=n_sub),
        grid=(n_batch, ns),
        in_specs=[
            pl.BlockSpec(memory_space=pltpu.SMEM),
            pl.BlockSpec((rows, SW_HEADS * SW_HD), cur(COL_SQ // (SW_HEADS * SW_HD))),
            pl.BlockSpec((rows, SW_KV_W), cur(COL_SK // SW_KV_W)),
            pl.BlockSpec((SW_BLOCK, SW_KV_W), prev(COL_SK // SW_KV_W)),
            pl.BlockSpec((rows, SW_KV_W), cur(COL_SV // SW_KV_W)),
            pl.BlockSpec((SW_BLOCK, SW_KV_W), prev(COL_SV // SW_KV_W)),
            pl.BlockSpec((SW_HEADS, 2 * SW_BLOCK, 2 * SW_BLOCK), lambda b, i: (0, 0, 0)),
            pl.BlockSpec((rows, D_MODEL), cur(COL_GB // D_MODEL)),
            pl.BlockSpec((rows, D_MODEL), lambda b, i: (b * ns + i, 0)),
        ],
        out_specs=[
            pl.BlockSpec((rows, SW_HEADS * SW_HD), lambda b, i: (b * ns + i, 0)),
            pl.BlockSpec((1, SW_BLOCK, SW_KV_W), lambda b, i: (b, 0, 0)),
            pl.BlockSpec((1, SW_BLOCK, SW_KV_W), lambda b, i: (b, 0, 0)),
        ],
        out_shape=[
            jax.ShapeDtypeStruct((n_batch * seq, SW_HEADS * SW_HD), F32),
            jax.ShapeDtypeStruct((n_batch, SW_BLOCK, SW_KV_W), F32),
            jax.ShapeDtypeStruct((n_batch, SW_BLOCK, SW_KV_W), F32),
        ],
        compiler_params=_cparams(("arbitrary", "arbitrary")),
        name="swa_prompt",
    )(sinks, proj, proj, proj, proj, proj, bias, proj, other)


def _swa_sample_kernel(q_ref, kn_ref, vn_ref, kc_ref, vc_ref, bias_ref, sink_ref, gate_ref, other_ref,
                       o_ref, ko_ref, vo_ref, kall_ref, vall_ref, *, seq, n_bb, n_cache):
    n_keys = kall_ref.shape[1]
    zeros_tail = jnp.zeros((n_bb, n_keys - n_cache - SUBLANES, LANES), F32)
    lo = _lane_iota((SUBLANES, LANES)) < SW_HD
    for ref, cache_ref, new_ref in ((kall_ref, kc_ref, kn_ref), (vall_ref, vc_ref, vn_ref)):
        ref[:, 0:n_cache, :] = cache_ref[...]
        for bb, piece in enumerate(_spread_rows(new_ref, n_bb, seq)):
            ref[bb, n_cache:n_cache + SUBLANES, :] = piece
        ref[:, n_cache + SUBLANES:, :] = zeros_tail
    ko_ref[...] = kall_ref[:, seq:seq + n_cache, :]
    vo_ref[...] = vall_ref[:, seq:seq + n_cache, :]
    outs = []
    for bb, q8 in enumerate(_spread_rows(q_ref, n_bb, seq)):
        kk = _dup_halves(kall_ref[bb])
        vv = _dup_halves(vall_ref[bb])
        pairs = []
        for kv in range(SW_KV_HEADS):
            pieces = []
            for g in range(SW_GROUP):
                hq = kv * SW_GROUP + g
                qp = q8[:, (hq // 2) * LANES:(hq // 2 + 1) * LANES]
                pieces.append(jnp.where(lo if hq % 2 == 0 else ~lo, qp, 0.0))
            qs = jnp.concatenate(pieces, axis=0) * (SW_HD ** -0.5)
            bias = jnp.concatenate([bias_ref[kv * SW_GROUP + g] for g in range(SW_GROUP)], axis=0)
            s = _dot_nt(qs, kk[kv]) + bias
            res = _sink_softmax_pv(s, _col(sink_ref[kv], 0), vv[kv])
            for g in range(0, SW_GROUP, 2):
                pairs.append(jnp.where(lo, res[g * SUBLANES:(g + 1) * SUBLANES],
                                       res[(g + 1) * SUBLANES:(g + 2) * SUBLANES]))
        outs.append(jnp.concatenate(pairs, axis=1))
    o_ref[...] = other_ref[...] + _sigmoid(gate_ref[...]) * _gather_rows(outs, seq)


def _swa_sample(proj, k_cache, v_cache, bias, bias_row0, sink_rows, other, row0, n_batch, seq):
    assert SUBLANES % seq == 0
    n_bb = SUBLANES
    rows_in = n_bb * seq
    n_cache = k_cache.shape[1]
    assert n_batch % n_bb == 0 and row0 % rows_in == 0 and n_cache % SUBLANES == 0
    n_keys = bias.shape[-1]
    rb0 = row0 // rows_in
    blk = lambda col: (lambda i: (rb0 + i, col))
    return pl.pallas_call(
        functools.partial(_swa_sample_kernel, seq=seq, n_bb=n_bb, n_cache=n_cache),
        grid=(n_batch // n_bb,),
        in_specs=[
            pl.BlockSpec((rows_in, SW_HEADS * SW_HD), blk(COL_SQ // (SW_HEADS * SW_HD))),
            pl.BlockSpec((rows_in, SW_KV_W), blk(COL_SK // SW_KV_W)),
            pl.BlockSpec((rows_in, SW_KV_W), blk(COL_SV // SW_KV_W)),
            pl.BlockSpec((n_bb, n_cache, SW_KV_W), lambda i: (i, 0, 0)),
            pl.BlockSpec((n_bb, n_cache, SW_KV_W), lambda i: (i, 0, 0)),
            pl.BlockSpec((SW_HEADS, SUBLANES, n_keys), lambda i: (0, bias_row0 // SUBLANES, 0)),
            pl.BlockSpec((SW_KV_HEADS, SW_GROUP * SUBLANES, n_keys), lambda i: (0, 0, 0)),
            pl.BlockSpec((rows_in, D_MODEL), blk(COL_GB // D_MODEL)),
            pl.BlockSpec((rows_in, D_MODEL), lambda i: (i, 0)),
        ],
        out_specs=[
            pl.BlockSpec((rows_in, SW_HEADS * SW_HD), lambda i: (i, 0)),
            pl.BlockSpec((n_bb, n_cache, SW_KV_W), lambda i: (i, 0, 0)),
            pl.BlockSpec((n_bb, n_cache, SW_KV_W), lambda i: (i, 0, 0)),
        ],
        out_shape=[
            jax.ShapeDtypeStruct((n_batch * seq, SW_HEADS * SW_HD), F32),
            jax.ShapeDtypeStruct(k_cache.shape, F32),
            jax.ShapeDtypeStruct(v_cache.shape, F32),
        ],
        scratch_shapes=[pltpu.VMEM((n_bb, n_keys, SW_KV_W), F32), pltpu.VMEM((n_bb, n_keys, SW_KV_W), F32)],
        compiler_params=_cparams(("arbitrary",)),
        name="swa_sample",
    )(proj, proj, proj, k_cache, v_cache, bias, sink_rows, proj, other)


def _mix_kernel(xp_ref, xs_ref, mp_ref, ms_ref, wo_ref, nw_ref, wr_ref, br_ref,
                x1_ref, h2_ref, route_ref, *, n_prompt_tiles):
    i = pl.program_id(0)

    def run(x_ref, mixed_ref):
        x1 = x_ref[...] + _dot(mixed_ref[...], wo_ref[...])
        x1_ref[...] = x1
        h2 = x1 * lax.rsqrt(jnp.mean(x1 * x1, axis=-1, keepdims=True) + EPS) * nw_ref[...]
        _to_tiles(h2_ref, h2)
        logits = _dot(h2, wr_ref[...]) + br_ref[...]
        lane = _lane_iota(logits.shape)
        lanef = lane.astype(F32)
        big = float(2 * LANES)
        is_g = lane < N_GROUPS
        gl = jnp.where(is_g, logits, -jnp.inf)
        gmax = jnp.max(gl, axis=-1, keepdims=True)
        gval = 1.0 / jnp.sum(jnp.where(is_g, jnp.exp(gl - gmax), 0.0), axis=-1, keepdims=True)
        grp = jnp.min(jnp.where(gl == gmax, lanef, big), axis=-1, keepdims=True)
        e_grp = ((lane - N_GROUPS) >> 3).astype(F32)
        is_e = (lane >= N_GROUPS) & (lane < N_GROUPS + N_EXPERTS) & (e_grp == grp)
        el = jnp.where(is_e, logits, -jnp.inf)
        v1 = jnp.max(el, axis=-1, keepdims=True)
        i1 = jnp.min(jnp.where(el == v1, lanef, big), axis=-1, keepdims=True)
        el2 = jnp.where(lanef == i1, -jnp.inf, el)
        v2 = jnp.max(el2, axis=-1, keepdims=True)
        i2 = jnp.min(jnp.where(el2 == v2, lanef, big), axis=-1, keepdims=True)
        e2 = jnp.exp(v2 - v1)
        w1 = gval / (1.0 + e2)
        w2 = gval * e2 / (1.0 + e2)
        route_ref[...] = jnp.where(lane == 0, i1 - N_GROUPS,
                                   jnp.where(lane == 1, i2 - N_GROUPS,
                                             jnp.where(lane == 2, w1, jnp.where(lane == 3, w2, 0.0))))

    @pl.when(i < n_prompt_tiles)
    def _():
        run(xp_ref, mp_ref)

    @pl.when(i >= n_prompt_tiles)
    def _():
        run(xs_ref, ms_ref)


def _mix(xp, xs, mixed_p, mixed_s, w_out, norm_w, w_router, b_router):
    tp, ts = xp.shape[0], xs.shape[0]
    tm = _token_tile(tp, ts, cands=BIG_TILES)
    npt, nst = tp // tm, ts // tm
    const = lambda i: (0, 0)
    row = lambda i: (i, 0)
    return pl.pallas_call(
        functools.partial(_mix_kernel, n_prompt_tiles=npt),
        grid=(npt + nst,),
        in_specs=[
            pl.BlockSpec((tm, D_MODEL), lambda i: (jnp.minimum(i, npt - 1), 0)),
            pl.BlockSpec((tm, D_MODEL), lambda i: (jnp.maximum(i - npt, 0), 0)),
            pl.BlockSpec((tm, D_MODEL), lambda i: (jnp.minimum(i, npt - 1), 0)),
            pl.BlockSpec((tm, D_MODEL), lambda i: (jnp.maximum(i - npt, 0), 0)),
            pl.BlockSpec((D_MODEL, D_MODEL), const),
            pl.BlockSpec((1, D_MODEL), const),
            pl.BlockSpec((D_MODEL, LANES), const),
            pl.BlockSpec((1, LANES), const),
        ],
        out_specs=[
            pl.BlockSpec((tm, D_MODEL), row),
            pl.BlockSpec((tm,) + TOK_TILE, lambda i: (i, 0, 0)),
            pl.BlockSpec((tm, LANES), row),
        ],
        out_shape=[
            jax.ShapeDtypeStruct((tp + ts, D_MODEL), F32),
            jax.ShapeDtypeStruct((tp + ts,) + TOK_TILE, F32),
            jax.ShapeDtypeStruct((tp + ts, LANES), F32),
        ],
        compiler_params=_cparams(("arbitrary",)),
        name="mix_router",
    )(xp, xs, mixed_p, mixed_s, w_out, norm_w, w_router, b_router)


def _rank_kernel(route_ref, dest_ref, meta_ref, rank_ref, cnt_ref, *, tile, blk):
    phase = pl.program_id(0)
    i = pl.program_id(1)
    shape = (tile, LANES)
    lane = _lane_iota(shape)
    lanef = lane.astype(F32)
    r = route_ref[...]
    oh0 = lanef == _col(r, 0)
    oh1 = lanef == _col(r, 1)
    rows = pl.ds(pl.multiple_of(i * tile, tile), tile)

    @pl.when(phase == 0)
    def _():
        @pl.when(i == 0)
        def _():
            cnt_ref[...] = jnp.zeros(cnt_ref.shape, F32)

        oh = jnp.where(oh0 | oh1, 1.0, 0.0)
        tri = jnp.where(_row_iota((tile, tile)) > _lane_iota((tile, tile)), 1.0, 0.0)
        before = _dot(tri, oh) + cnt_ref[0:1, :]
        rank0 = jnp.sum(jnp.where(oh0, before, 0.0), axis=-1, keepdims=True)
        rank1 = jnp.sum(jnp.where(oh1, before, 0.0), axis=-1, keepdims=True)
        rank_ref[rows, :] = jnp.where(lane == 0, rank0, jnp.where(lane == 1, rank1, 0.0))
        cnt_ref[0:1, :] = cnt_ref[0:1, :] + jnp.sum(oh, axis=0, keepdims=True)

    @pl.when(phase == 1)
    def _():
        cnt = cnt_ref[0:1, :]
        padded = jnp.floor((cnt + (blk - 1)) / blk) * blk
        before_lane = jnp.where(_row_iota((LANES, LANES)) < _lane_iota((LANES, LANES)), 1.0, 0.0)
        start = _dot_exact(jnp.broadcast_to(padded, (SUBLANES, LANES)), before_lane)[0:1, :]
        rk = rank_ref[rows, :]
        d0 = jnp.sum(jnp.where(oh0, start, 0.0), axis=-1, keepdims=True) + _col(rk, 0)
        d1 = jnp.sum(jnp.where(oh1, start, 0.0), axis=-1, keepdims=True) + _col(rk, 1)
        dest_ref[...] = jnp.where(lane == 0, d0, jnp.where(lane == 1, d1, 0.0)).astype(I32)

        @pl.when(i == 0)
        def _():
            end = start + padded
            mshape = meta_ref.shape
            blk_start = (_row_iota(mshape) * blk).astype(F32)
            hit = (_lane_iota(mshape) < N_EXPERTS) & (end <= blk_start)
            be = jnp.minimum(jnp.sum(jnp.where(hit, 1.0, 0.0), axis=-1, keepdims=True), N_EXPERTS - 1.0)
            n_used = _col(end, N_EXPERTS - 1) / blk
            ml = _lane_iota(mshape)
            mine = ml.astype(F32) == be
            seg_start = jnp.sum(jnp.where(mine, start, 0.0), axis=-1, keepdims=True)
            seg_count = jnp.sum(jnp.where(mine, cnt, 0.0), axis=-1, keepdims=True)
            n_valid = jnp.clip(seg_count - (blk_start[:, 0:1] - seg_start), 0.0, float(blk))
            meta_ref[...] = jnp.where(ml == 0, be, jnp.where(ml == 1, n_used,
                                                              jnp.where(ml == 2, n_valid, 0.0))).astype(I32)


def _rank(route, tile, blk, n_blocks):
    t = route.shape[0]
    nt = t // tile
    nbp = -(-n_blocks // SUBLANES) * SUBLANES
    return pl.pallas_call(
        functools.partial(_rank_kernel, tile=tile, blk=blk),
        grid=(2, nt),
        in_specs=[pl.BlockSpec((tile, LANES), lambda p, i: (i, 0))],
        out_specs=[
            pl.BlockSpec((tile, LANES), lambda p, i: (i * p, 0)),
            pl.BlockSpec((nbp, LANES), lambda p, i: (0, 0)),
        ],
        out_shape=[
            jax.ShapeDtypeStruct((t, LANES), I32),
            jax.ShapeDtypeStruct((nbp, LANES), I32),
        ],
        scratch_shapes=[pltpu.VMEM((t, LANES), F32), pltpu.VMEM((SUBLANES, LANES), F32)],
        compiler_params=_cparams(("arbitrary", "arbitrary")),
        name="moe_rank",
    )(route)


TOK_TILE = (D_MODEL // LANES, LANES)


def _to_tiles(ref, x):
    for j in range(TOK_TILE[0]):
        ref[:, j, :] = x[:, j * LANES:(j + 1) * LANES]


def _from_tiles(ref):
    return jnp.concatenate([ref[:, j, :] for j in range(TOK_TILE[0])], axis=1)


def _tile_copy(src, src_row, dst, dst_row, sem):
    return pltpu.make_async_copy(src.at[src_row], dst.at[dst_row], sem)


def _row_copy(src, src_row, dst, group, sub, sem):
    return pltpu.make_async_copy(src.at[pl.ds(src_row, 1)], dst.at[group, pl.ds(sub, 1)], sem)


def _last_used(i, nu_ref):
    return jnp.minimum(i, jnp.maximum(nu_ref[0] - 1, 0))


def _for_rows(n, body):
    def group(g, carry):
        for u in range(SUBLANES):
            body(g, u)
        return carry

    def single(t, carry):
        body(t // SUBLANES, t % SUBLANES)
        return carry

    n_groups = n // SUBLANES
    lax.fori_loop(0, n_groups, group, 0)
    lax.fori_loop(n_groups * SUBLANES, n, single, 0)


def _expert_kernel(be_ref, nu_ref, nv_ref, dest_ref, h2_ref, wg_ref, wu_ref, wd_ref, y_ref,
                   xbuf_ref, wgu_ref, wdn_ref, inv_ref, sem, *, blk):
    i = pl.program_id(0)
    n_used = nu_ref[0]
    used = i < n_used
    slot = i % 2
    blk_i = _last_used(i, nu_ref)
    fresh = (i == 0) | (be_ref[blk_i] != be_ref[jnp.maximum(blk_i - 1, 0)])

    def gather(block, into):
        def issue(g, u):
            t = g * SUBLANES + u
            _tile_copy(h2_ref, inv_ref[block * blk + t], xbuf_ref.at[into], t,
                       sem.at[into]).start(priority=u % 2 if isinstance(u, int) else 0)

        _for_rows(nv_ref[block], issue)

    @pl.when(i == 0)
    def _():
        def place(a, carry):
            inv_ref[dest_ref[a]] = a >> 1
            return carry

        lax.fori_loop(0, dest_ref.shape[0], place, 0, unroll=16)
        xbuf_ref[...] = jnp.zeros(xbuf_ref.shape, F32)
        gather(0, 0)

    @pl.when(i + 1 < n_used)
    def _():
        gather(i + 1, 1 - slot)

    @pl.when(used & fresh)
    def _():
        wgu_ref[:, :D_EXPERT] = wg_ref[0].astype(BF16)
        wgu_ref[:, D_EXPERT:] = wu_ref[0].astype(BF16)
        wdn_ref[...] = wd_ref[0].astype(BF16)

    @pl.when(used)
    def _():
        _for_rows(nv_ref[i], lambda g, u: _tile_copy(h2_ref, 0, xbuf_ref.at[slot], 0, sem.at[slot]).wait())
        gu = jnp.dot(_from_tiles(xbuf_ref.at[slot]).astype(BF16), wgu_ref[...], preferred_element_type=F32)
        g = gu[:, :D_EXPERT]
        hidden = (g * _sigmoid(g) * gu[:, D_EXPERT:]).astype(BF16)
        y_ref[...] = jnp.dot(hidden, wdn_ref[...], preferred_element_type=F32)

    @pl.when(jnp.logical_not(used))
    def _():
        y_ref[...] = jnp.zeros(y_ref.shape, F32)


def _experts(block_expert, n_used, n_valid, dest_flat, h2, w_gate, w_up, w_down, blk):
    n_blocks = block_expert.shape[0]
    wsel = lambda i, be, nu, nv, de: (be[_last_used(i, nu)], 0, 0)
    return pl.pallas_call(
        functools.partial(_expert_kernel, blk=blk),
        grid_spec=pltpu.PrefetchScalarGridSpec(
            num_scalar_prefetch=4,
            grid=(n_blocks,),
            in_specs=[
                pl.BlockSpec(memory_space=pl.ANY),
                pl.BlockSpec((1, D_MODEL, D_EXPERT), wsel),
                pl.BlockSpec((1, D_MODEL, D_EXPERT), wsel),
                pl.BlockSpec((1, D_EXPERT, D_MODEL), wsel),
            ],
            out_specs=pl.BlockSpec((blk, D_MODEL), lambda i, be, nu, nv, de: (i, 0)),
            scratch_shapes=[
                pltpu.VMEM((2, blk) + TOK_TILE, F32),
                pltpu.VMEM((D_MODEL, 2 * D_EXPERT), BF16),
                pltpu.VMEM((D_EXPERT, D_MODEL), BF16),
                pltpu.SMEM((n_blocks * blk,), I32),
                pltpu.SemaphoreType.DMA((2,)),
            ],
        ),
        out_shape=jax.ShapeDtypeStruct((n_blocks * blk, D_MODEL), F32),
        compiler_params=_cparams(("arbitrary",)),
        name="moe_experts",
    )(block_expert, n_used, n_valid, dest_flat, h2, w_gate, w_up, w_down)


def _combine_kernel(dest_ref, x1_ref, route_ref, nw_ref, ys_ref, yp_ref, ysm_ref, ybuf_ref, sem,
                    *, tile, n_prompt_tiles):
    i = pl.program_id(0)
    slot = i % 2

    groups = tile // SUBLANES

    def gather(step, into):
        def issue(g, carry):
            for u in range(SUBLANES):
                for k in range(2):
                    slot_row = dest_ref[2 * step * tile + g * (2 * SUBLANES) + 2 * u + k]
                    _row_copy(ys_ref, slot_row, ybuf_ref.at[into], k * groups + g, u,
                              sem.at[into]).start(priority=k)
            return carry

        lax.fori_loop(0, groups, issue, 0)

    @pl.when(i == 0)
    def _():
        gather(0, 0)

    @pl.when(i + 1 < pl.num_programs(0))
    def _():
        gather(i + 1, 1 - slot)

    def drain(t, carry):
        _row_copy(ys_ref, 0, ybuf_ref.at[slot], 0, 0, sem.at[slot]).wait()
        return carry

    lax.fori_loop(0, 2 * tile, drain, 0, unroll=8)
    r = route_ref[...]
    y = (ybuf_ref[slot, 0:groups].reshape(tile, D_MODEL) * _col(r, 2)
         + ybuf_ref[slot, groups:2 * groups].reshape(tile, D_MODEL) * _col(r, 3))
    x2 = x1_ref[...] + y
    out = x2 * lax.rsqrt(jnp.mean(x2 * x2, axis=-1, keepdims=True) + EPS) * nw_ref[...]

    @pl.when(i < n_prompt_tiles)
    def _():
        yp_ref[...] = out

    @pl.when(i >= n_prompt_tiles)
    def _():
        ysm_ref[...] = out


def _combine(dest_flat, x1, route, norm_w, ys, tp, ts):
    tile = _token_tile(tp, ts, cands=BIG_TILES)
    npt, nst = tp // tile, ts // tile
    return pl.pallas_call(
        functools.partial(_combine_kernel, tile=tile, n_prompt_tiles=npt),
        grid_spec=pltpu.PrefetchScalarGridSpec(
            num_scalar_prefetch=1,
            grid=(npt + nst,),
            in_specs=[
                pl.BlockSpec((tile, D_MODEL), lambda i, d: (i, 0)),
                pl.BlockSpec((tile, LANES), lambda i, d: (i, 0)),
                pl.BlockSpec((1, D_MODEL), lambda i, d: (0, 0)),
                pl.BlockSpec(memory_space=pl.ANY),
            ],
            out_specs=[
                pl.BlockSpec((tile, D_MODEL), lambda i, d: (jnp.minimum(i, npt - 1), 0)),
                pl.BlockSpec((tile, D_MODEL), lambda i, d: (jnp.maximum(i - npt, 0), 0)),
            ],
            scratch_shapes=[pltpu.VMEM((2, 2 * tile // SUBLANES, SUBLANES, D_MODEL), F32), pltpu.SemaphoreType.DMA((2,))],
        ),
        out_shape=[
            jax.ShapeDtypeStruct((tp, D_MODEL), F32),
            jax.ShapeDtypeStruct((ts, D_MODEL), F32),
        ],
        compiler_params=_cparams(("arbitrary",)),
        name="moe_combine",
    )(dest_flat, x1, route, norm_w, ys)


def _layer(xp, xs, n_batch, seq, s_batch, s_seq, conv_state, dn_state, k_cache, v_cache,
           w_in, conv_w, a_log, dt_bias, dn_norm_w, sinks, rel_bias, w_out, norm_mix_w, norm_ffn_w,
           w_rg, b_rg, w_re, b_re, w_gate, w_up, w_down, norm_final_w):
    tp, ts = xp.shape[0], xs.shape[0]
    t_all = tp + ts
    row = lambda v: v.reshape(1, -1).astype(F32)

    o = np.cumsum((0, DN_QK_W, DN_QK_W, DN_V_W, DN_V_W, DN_HEADS, DN_HEADS, SW_HEADS * SW_HD, SW_KV_W, SW_KV_W,
                   D_MODEL, D_MODEL)).tolist()
    w_big = jnp.concatenate([w_in[:, o[0]:o[4]], w_in[:, o[6]:o[7]], w_in[:, o[9]:o[11]], w_in[:, o[7]:o[9]]],
                            axis=1).astype(BF16)
    w_small = jnp.pad(w_in[:, o[4]:o[6]], ((0, 0), (0, LANES - 2 * DN_HEADS))).astype(BF16)
    head_row = lambda v: jnp.pad(v.astype(F32), (DN_HEADS, LANES - 2 * DN_HEADS)).reshape(1, LANES)
    w_router = jnp.pad(jnp.concatenate([w_rg, w_re], axis=1),
                       ((0, 0), (0, LANES - N_GROUPS - N_EXPERTS))).astype(BF16)
    b_router = jnp.pad(jnp.concatenate([b_rg, b_re]).astype(F32), (0, LANES - N_GROUPS - N_EXPERTS)).reshape(1, LANES)

    conv0 = jnp.zeros((n_batch, DN_CONV - 1, DN_CONV_W), F32)
    proj, ba, conv_tail = _inproj(xp, xs, row(norm_mix_w), w_big, w_small, conv_w.astype(F32), conv0, seq)
    p_conv = conv_tail[:, SUBLANES - (DN_CONV - 1):, :]

    dn0 = jnp.zeros((n_batch, DN_HEADS, DN_DK, DN_DV), F32)
    oa_p, p_dn = _dn_prompt(proj, ba, head_row(a_log), head_row(dt_bias), row(dn_norm_w), dn0, n_batch, seq)
    oa_s, s_conv, s_dn = _dn_sample(proj, ba, conv_w.astype(F32), head_row(a_log), head_row(dt_bias), row(dn_norm_w),
                                    conv_state, dn_state, tp, s_batch, s_seq)

    n_cache = k_cache.shape[1]
    n_keys = -(-(n_cache + SUBLANES) // LANES) * LANES
    assert n_keys == 2 * SW_BLOCK, "the prompt and sample bias tables share one (rows, keys) array"
    dist_p = np.arange(SW_BLOCK)[:, None] - (np.arange(2 * SW_BLOCK)[None, :] - SW_BLOCK)
    dist_s = n_cache + np.arange(SUBLANES)[:, None] - np.arange(n_keys)[None, :]
    in_window = lambda dist: (dist >= 0) & (dist < WINDOW)
    has_prev = np.arange(2 * SW_BLOCK)[None, :] >= SW_BLOCK
    buckets = np.concatenate([_masked_bucket(dist_p, in_window(dist_p) & has_prev),
                              _masked_bucket(dist_p, in_window(dist_p)),
                              _masked_bucket(dist_s, in_window(dist_s))], axis=0)
    bias = _relbias(rel_bias.astype(F32), jnp.asarray(buckets))
    mixed_p, p_k, p_v = _swa_prompt(proj, sinks.astype(F32), bias, oa_p, n_batch, seq)
    sink_rows = jnp.broadcast_to(jnp.repeat(sinks.astype(F32).reshape(SW_KV_HEADS, SW_GROUP), SUBLANES, axis=1)[:, :, None],
                                 (SW_KV_HEADS, SW_GROUP * SUBLANES, n_keys))
    mixed_s, s_k, s_v = _swa_sample(proj, k_cache.reshape(s_batch, n_cache, SW_KV_W),
                                    v_cache.reshape(s_batch, n_cache, SW_KV_W), bias, 2 * SW_BLOCK, sink_rows, oa_s,
                                    tp, s_batch, s_seq)

    x1, h2, route = _mix(xp, xs, mixed_p, mixed_s, w_out.astype(BF16), row(norm_ffn_w), w_router, b_router)

    n_blocks = -(-2 * t_all // MOE_BLOCK) + N_EXPERTS
    dest, meta = _rank(route, _token_tile(t_all, cands=BIG_TILES), MOE_BLOCK, n_blocks)
    dest_flat = dest[:, :2].reshape(-1)
    block_expert = meta[:n_blocks, 0]
    n_used = meta[0:1, 1]
    n_valid = meta[:n_blocks, 2]
    ys = _experts(block_expert, n_used, n_valid, dest_flat, h2, w_gate, w_up, w_down, MOE_BLOCK)
    y_p, y_s = _combine(dest_flat, x1, route, row(norm_final_w), ys, tp, ts)

    kv_shape = (n_batch, WINDOW, SW_KV_HEADS, SW_HD)
    return (y_p, y_s, p_conv, p_dn, p_k.reshape(kv_shape), p_v.reshape(kv_shape), s_conv, s_dn,
            s_k.reshape(k_cache.shape), s_v.reshape(v_cache.shape))


def kernel(x_prompt, x_sample, state_dn_conv, state_dn, cache_swa_k, cache_swa_v, w_in, conv_w, a_log, dt_bias, dn_norm_w, sinks, rel_bias, w_out, norm_mix_w, norm_ffn_w, w_router_group, b_router_group, w_router_expert, b_router_expert, w_gate, w_up, w_down, norm_final_w):
    depth = w_in.shape[0]
    assert depth == 1, "the final-norm fusion below assumes a single layer"
    n_batch, seq, _ = x_prompt.shape
    s_batch, s_seq, _ = x_sample.shape
    outs = _layer(x_prompt.reshape(-1, D_MODEL), x_sample.reshape(-1, D_MODEL), n_batch, seq, s_batch, s_seq,
                  state_dn_conv[0], state_dn[0], cache_swa_k[0], cache_swa_v[0],
                  w_in[0], conv_w[0], a_log[0], dt_bias[0], dn_norm_w[0], sinks[0], rel_bias,
                  w_out[0], norm_mix_w[0], norm_ffn_w[0], w_router_group[0], b_router_group[0],
                  w_router_expert[0], b_router_expert[0], w_gate[0], w_up[0], w_down[0], norm_final_w)
    y_p, y_s, p_conv, p_dn, p_k, p_v, s_conv, s_dn, s_k, s_v = outs
    return (y_p.reshape(x_prompt.shape), y_s.reshape(x_sample.shape), p_conv[None], p_dn[None], p_k[None], p_v[None],
            s_conv[None], s_dn[None], s_k[None], s_v[None])
```

```python
import functools
import math

import jax
import jax.numpy as jnp
import numpy as np
from jax import lax
from jax.experimental import pallas as pl
from jax.experimental.pallas import tpu as pltpu

F32 = jnp.float32
BF16 = jnp.bfloat16
I32 = jnp.int32

D_MODEL = 1024
DN_HEADS = 8
DN_DK = 128
DN_DV = 128
DN_CONV = 4
DN_CHUNK = 64
DN_QK_W = DN_HEADS * DN_DK
DN_V_W = DN_HEADS * DN_DV
DN_CONV_W = 2 * DN_QK_W + DN_V_W
SW_HEADS = 16
SW_KV_HEADS = 2
SW_GROUP = SW_HEADS // SW_KV_HEADS
SW_HD = 64
SW_KV_W = SW_KV_HEADS * SW_HD
WINDOW = 128
SW_BLOCK = 128
REL_BUCKETS = 32
REL_MAX_DIST = 128
N_GROUPS = 8
EXP_PER_GROUP = 8
N_EXPERTS = N_GROUPS * EXP_PER_GROUP
D_EXPERT = 256
MOE_BLOCK = 304
EPS = 1e-6

LANES = 128
SUBLANES = 8
VMEM_LIMIT = 56 * 1024 * 1024

COL_QKV = 0
COL_Z = 3072
COL_SQ = 4096
COL_GA = 5120
COL_GB = 6144
COL_SK = 7168
COL_SV = 7296
PROJ_W = 7424
PROJ_CHUNK = 512


def _cparams(sem):
    return pltpu.CompilerParams(dimension_semantics=sem, vmem_limit_bytes=VMEM_LIMIT)


def _sigmoid(x):
    return 0.5 * jnp.tanh(0.5 * x) + 0.5


def _dot(a, b):
    return jnp.dot(a.astype(BF16), b.astype(BF16), preferred_element_type=F32)


def _dot_nt(a, b):
    return lax.dot_general(a.astype(BF16), b.astype(BF16), (((1,), (1,)), ((), ())), preferred_element_type=F32)


def _dot_tn(a, b):
    return lax.dot_general(a.astype(BF16), b.astype(BF16), (((0,), (0,)), ((), ())), preferred_element_type=F32)


def _dot_exact(a, b):
    return jnp.dot(a, b, precision=lax.Precision.HIGHEST, preferred_element_type=F32)


def _lane_iota(shape):
    return lax.broadcasted_iota(I32, shape, len(shape) - 1)


def _row_iota(shape):
    return lax.broadcasted_iota(I32, shape, len(shape) - 2)


def _col(x, j):
    return jnp.sum(jnp.where(_lane_iota(x.shape) == j, x, 0.0), axis=-1, keepdims=True)


BIG_TILES = (512, 256, 128, 64, 32, 16, 8)


def _token_tile(*sizes, cands=BIG_TILES[1:]):
    for t in cands:
        if all(s % t == 0 for s in sizes):
            return t
    raise ValueError(f"token counts {sizes} need a common tile that is a multiple of 8")


def _inproj_kernel(xp_ref, xs_ref, nw_ref, wb_ref, ws_ref, convw_ref, conv0_ref, proj_ref, ba_ref, tail_ref, cbuf_ref,
                   *, n_prompt_tiles, tiles_per_seq):
    i = pl.program_id(0)
    tm = xp_ref.shape[0]
    hist = SUBLANES - (DN_CONV - 1)

    def project(x_ref, conv):
        x = x_ref[...]
        h = (x * lax.rsqrt(jnp.mean(x * x, axis=-1, keepdims=True) + EPS) * nw_ref[...]).astype(BF16)
        ba_ref[...] = jnp.dot(h, ws_ref[...], preferred_element_type=F32)
        top = _row_iota((SUBLANES, PROJ_CHUNK))
        starts = list(range(0, PROJ_W, PROJ_CHUNK))
        if conv:
            with_conv = [c for c in starts if c + PROJ_CHUNK <= COL_QKV + DN_CONV_W]
            plain = [c for c in starts if c not in with_conv]
            starts = [c for pair in zip(with_conv, plain) for c in pair] + plain[len(with_conv):]
        for c0 in starts:
            c1 = min(c0 + PROJ_CHUNK, PROJ_W)
            cur = jnp.dot(h, wb_ref[:, c0:c1], preferred_element_type=F32)
            if conv and c1 <= COL_QKV + DN_CONV_W:
                prev = cbuf_ref[:, c0:c1]
                acc = cur * convw_ref[DN_CONV - 1:DN_CONV, c0:c1]
                for s in range(1, DN_CONV):
                    sh = pltpu.roll(cur, s, axis=0)
                    head = jnp.where(top < s, pltpu.roll(prev, s, axis=0), sh[:SUBLANES])
                    sh = jnp.concatenate([head, sh[SUBLANES:]], axis=0)
                    acc = acc + sh * convw_ref[DN_CONV - 1 - s:DN_CONV - s, c0:c1]
                cbuf_ref[:, c0:c1] = cur[tm - SUBLANES:]
                cur = acc * _sigmoid(acc)
            proj_ref[:, c0:c1] = cur

    @pl.when(i < n_prompt_tiles)
    def _():
        @pl.when(i % tiles_per_seq == 0)
        def _():
            cbuf_ref[...] = jnp.zeros(cbuf_ref.shape, F32)
            cbuf_ref[hist:SUBLANES, :] = conv0_ref[0]

        project(xp_ref, True)
        tail_ref[0] = cbuf_ref[...]

    @pl.when(i >= n_prompt_tiles)
    def _():
        project(xs_ref, False)


def _inproj(xp, xs, norm_w, w_big, w_small, conv_w, conv0, seq):
    tp, ts = xp.shape[0], xs.shape[0]
    tm = _token_tile(tp, ts, seq)
    assert COL_QKV == 0 and DN_CONV_W % PROJ_CHUNK == 0
    npt, nst = tp // tm, ts // tm
    tps = seq // tm
    const = lambda i: (0, 0)
    seq_of = lambda i: (jnp.minimum(i, npt - 1) // tps, 0, 0)
    return pl.pallas_call(
        functools.partial(_inproj_kernel, n_prompt_tiles=npt, tiles_per_seq=tps),
        grid=(npt + nst,),
        in_specs=[
            pl.BlockSpec((tm, D_MODEL), lambda i: (jnp.minimum(i, npt - 1), 0)),
            pl.BlockSpec((tm, D_MODEL), lambda i: (jnp.maximum(i - npt, 0), 0)),
            pl.BlockSpec((1, D_MODEL), const),
            pl.BlockSpec((D_MODEL, PROJ_W), const, pipeline_mode=pl.Buffered(1)),
            pl.BlockSpec((D_MODEL, LANES), const),
            pl.BlockSpec((DN_CONV, DN_CONV_W), const),
            pl.BlockSpec((1, DN_CONV - 1, DN_CONV_W), seq_of),
        ],
        out_specs=[
            pl.BlockSpec((tm, PROJ_W), lambda i: (i, 0)),
            pl.BlockSpec((tm, LANES), lambda i: (i, 0)),
            pl.BlockSpec((1, SUBLANES, DN_CONV_W), seq_of),
        ],
        out_shape=[
            jax.ShapeDtypeStruct((tp + ts, PROJ_W), F32),
            jax.ShapeDtypeStruct((tp + ts, LANES), F32),
            jax.ShapeDtypeStruct((tp // seq, SUBLANES, DN_CONV_W), F32),
        ],
        scratch_shapes=[pltpu.VMEM((SUBLANES, DN_CONV_W), F32)],
        compiler_params=_cparams(("arbitrary",)),
        name="inproj",
    )(xp, xs, norm_w, w_big, w_small, conv_w, conv0)


def _dn_core(groups, alog, dtb, nw, read_state, write_state, n_seg, seg_valid):
    rows = groups[0][0].shape[0]
    sr = rows // n_seg
    assert sr * n_seg == rows and sr & (sr - 1) == 0 and rows <= LANES
    seg_shift = sr.bit_length() - 1
    ri = _row_iota((rows, rows))
    ci = _lane_iota((rows, rows))
    incl = ri >= ci
    strict = ri > ci
    if n_seg > 1:
        same = (ri >> seg_shift) == (ci >> seg_shift)
        incl = incl & same
        strict = strict & same
    l_incl = incl.astype(F32)
    eye = (ri == ci).astype(F32)
    levels = max(1, math.ceil(math.log2(seg_valid)))

    beta_all, gsum_all, gtot_all, gsum_t = [], [], [], []
    for _, _, ba, _ in groups:
        b_all = _sigmoid(ba)
        sp = ba + dtb
        softplus = jnp.maximum(sp, 0.0) + jnp.log1p(jnp.exp(-jnp.abs(sp)))
        g_all = -jnp.exp(alog) * softplus
        if seg_valid < sr:
            live = (_row_iota((rows, LANES)) & (sr - 1)) < seg_valid
            b_all = jnp.where(live, b_all, 0.0)
            g_all = jnp.where(live, g_all, 0.0)
        gs = _dot_exact(l_incl, g_all)
        beta_all.append(b_all)
        gsum_all.append(gs)
        gtot_all.append(_dot_exact(same.astype(F32), g_all) if n_seg > 1 else gs[rows - 1:rows, :])
        padded = gs if rows == LANES else jnp.concatenate([gs, jnp.zeros((LANES - rows, LANES), F32)], axis=0)
        gsum_t.append(padded.T)

    probs = [(g, h) for g in range(len(groups)) for h in range(DN_HEADS)]
    segs = range(n_seg)
    q, k, v, kb, beta, gsum, gtot = {}, {}, {}, {}, {}, {}, {}
    for p in probs:
        g, h = p
        qkv = groups[g][0]
        qh = qkv[:, h * DN_DK:(h + 1) * DN_DK]
        kh = qkv[:, DN_QK_W + h * DN_DK:DN_QK_W + (h + 1) * DN_DK]
        v[p] = qkv[:, 2 * DN_QK_W + h * DN_DV:2 * DN_QK_W + (h + 1) * DN_DV]
        q[p] = qh * lax.rsqrt(jnp.sum(qh * qh, axis=-1, keepdims=True) + 1e-6) * (DN_DK ** -0.5)
        k[p] = kh * lax.rsqrt(jnp.sum(kh * kh, axis=-1, keepdims=True) + 1e-6)
        beta[p] = _col(beta_all[g], h)
        gsum[p] = _col(gsum_all[g], DN_HEADS + h)
        gtot[p] = _col(gtot_all[g], DN_HEADS + h)
        kb[p] = k[p] * beta[p]
    kq = {p: _dot_nt(jnp.concatenate([kb[p], q[p]], axis=0), k[p]) for p in probs}
    gamma = {(g, h): jnp.exp(jnp.where(incl, gsum[(g, h)] - gsum_t[g][DN_HEADS + h:DN_HEADS + h + 1, :rows], -jnp.inf))
             for g, h in probs}
    attn = {p: kq[p][rows:] * gamma[p] for p in probs}
    pw = {p: -jnp.where(strict, kq[p][:rows] * gamma[p], 0.0) for p in probs}
    t = {p: eye + pw[p] for p in probs}
    for _ in range(1, levels):
        pw = {p: _dot(pw[p], pw[p]) for p in probs}
        t = {p: t[p] + _dot(t[p], pw[p]) for p in probs}
    eg = {p: jnp.exp(gsum[p]) for p in probs}
    uw = {p: _dot(t[p], jnp.concatenate([v[p] * beta[p], kb[p] * eg[p]], axis=1)) for p in probs}
    qg = {p: q[p] * eg[p] for p in probs}
    state = {(p, s): read_state(p[0], s, p[1]) for p in probs for s in segs}
    wq = {(p, s): _dot(jnp.concatenate([uw[p][s * sr:(s + 1) * sr, DN_DV:], qg[p][s * sr:(s + 1) * sr]], axis=0),
                       state[(p, s)]) for p in probs for s in segs}
    join = lambda pieces: pieces[0] if len(pieces) == 1 else jnp.concatenate(pieces, axis=0)
    v_new = {p: uw[p][:, :DN_DV] - join([wq[(p, s)][:sr] for s in segs]) for p in probs}
    o = {p: join([wq[(p, s)][sr:] for s in segs]) + _dot(attn[p], v_new[p]) for p in probs}
    kd = {p: k[p] * jnp.exp(gtot[p] - gsum[p]) for p in probs}
    for p in probs:
        for s in segs:
            r0 = s * sr if n_seg > 1 else 0
            decay = jnp.exp(gtot[p][r0:r0 + 1, :])
            write_state(p[0], s, p[1],
                        state[(p, s)] * decay + _dot_tn(kd[p][s * sr:(s + 1) * sr], v_new[p][s * sr:(s + 1) * sr]))
    outs = []
    for g, (_, z, _, gate) in enumerate(groups):
        heads = []
        for h in range(DN_HEADS):
            oh = o[(g, h)]
            zz = z[:, h * DN_DV:(h + 1) * DN_DV]
            on = oh * lax.rsqrt(jnp.mean(oh * oh, axis=-1, keepdims=True) + EPS) * nw
            heads.append(on * (zz * _sigmoid(zz)) * _sigmoid(gate[:, h * DN_DV:(h + 1) * DN_DV]))
        outs.append(jnp.concatenate(heads, axis=1))
    return outs


def _dn_prompt_kernel(*refs, chunk, n_batch):
    nb = n_batch
    qkv_refs, z_refs, ba_refs, gate_refs = refs[0:nb], refs[nb:2 * nb], refs[2 * nb:3 * nb], refs[3 * nb:4 * nb]
    alog_ref, dtb_ref, nw_ref, s0_ref, o_ref, sout_ref = refs[4 * nb:]

    @pl.when(pl.program_id(0) == 0)
    def _():
        sout_ref[...] = s0_ref[...]

    groups = [(qkv_refs[b][...], z_refs[b][...], ba_refs[b][...], gate_refs[b][...]) for b in range(nb)]

    def read_state(g, s, h):
        return sout_ref[g, h]

    def write_state(g, s, h, val):
        sout_ref[g, h] = val

    outs = _dn_core(groups, alog_ref[...], dtb_ref[...], nw_ref[...], read_state, write_state, 1, chunk)
    for b in range(nb):
        o_ref[b] = outs[b]


def _dn_prompt(proj, ba, alog_row, dtb_row, dn_nw, s0, n_batch, seq):
    chunk = min(DN_CHUNK, seq)
    assert seq % chunk == 0 and chunk % SUBLANES == 0
    nc = seq // chunk
    const2 = lambda c: (0, 0)
    rows = lambda b, col: (lambda c: (b * nc + c, col))
    batches = range(n_batch)
    o, s_out = pl.pallas_call(
        functools.partial(_dn_prompt_kernel, chunk=chunk, n_batch=n_batch),
        grid=(nc,),
        in_specs=(
            [pl.BlockSpec((chunk, DN_CONV_W), rows(b, COL_QKV // DN_CONV_W)) for b in batches]
            + [pl.BlockSpec((chunk, DN_V_W), rows(b, COL_Z // DN_V_W)) for b in batches]
            + [pl.BlockSpec((chunk, LANES), rows(b, 0)) for b in batches]
            + [pl.BlockSpec((chunk, D_MODEL), rows(b, COL_GA // D_MODEL)) for b in batches]
            + [
                pl.BlockSpec((1, LANES), const2),
                pl.BlockSpec((1, LANES), const2),
                pl.BlockSpec((1, DN_DV), const2),
                pl.BlockSpec((n_batch, DN_HEADS, DN_DK, DN_DV), lambda c: (0, 0, 0, 0)),
            ]
        ),
        out_specs=[
            pl.BlockSpec((n_batch, chunk, DN_V_W), lambda c: (0, c, 0)),
            pl.BlockSpec((n_batch, DN_HEADS, DN_DK, DN_DV), lambda c: (0, 0, 0, 0)),
        ],
        out_shape=[
            jax.ShapeDtypeStruct((n_batch, seq, DN_V_W), F32),
            jax.ShapeDtypeStruct((n_batch, DN_HEADS, DN_DK, DN_DV), F32),
        ],
        compiler_params=_cparams(("arbitrary",)),
        name="dn_prompt",
    )(*([proj] * n_batch), *([proj] * n_batch), *([ba] * n_batch), *([proj] * n_batch), alog_row, dtb_row, dn_nw, s0)
    return o.reshape(n_batch * seq, DN_V_W), s_out


def _spread_rows(ref, n_bb, seq):
    per = SUBLANES // seq
    pieces = []
    for j in range(n_bb // per):
        x8 = ref[j * SUBLANES:(j + 1) * SUBLANES, :]
        for r in range(per):
            pieces.append(x8 if r == 0 else pltpu.roll(x8, SUBLANES - r * seq, axis=0))
    return pieces


def _gather_rows(pieces, seq):
    per = SUBLANES // seq
    rows = _row_iota(pieces[0].shape)
    tiles = []
    for j in range(len(pieces) // per):
        tile = pieces[j * per]
        for r in range(1, per):
            tile = jnp.where(rows >= r * seq, pltpu.roll(pieces[j * per + r], r * seq, axis=0), tile)
        tiles.append(tile)
    return jnp.concatenate(tiles, axis=0)


def _dn_sample_kernel(qkv_ref, z_ref, ba_ref, gate_ref, convw_ref, alog_ref, dtb_ref, nw_ref, conv0_ref, s0_ref,
                      o_ref, convout_ref, sout_ref, cbuf_ref, *, seq, n_bb):
    hist = SUBLANES - (DN_CONV - 1)
    spread = lambda ref: _spread_rows(ref, n_bb, seq)
    for bb, piece in enumerate(spread(qkv_ref)):
        cbuf_ref[bb, SUBLANES:2 * SUBLANES, :] = piece
    cbuf_ref[:, hist:SUBLANES, :] = conv0_ref[...]
    w = convw_ref[...]
    acc = cbuf_ref[:, hist:hist + SUBLANES, :] * w[0:1, :]
    for i in range(1, DN_CONV):
        acc = acc + cbuf_ref[:, hist + i:hist + i + SUBLANES, :] * w[i:i + 1, :]
    live = _row_iota(acc.shape) < seq
    qkv = jnp.where(live, acc * _sigmoid(acc), 0.0).reshape(n_bb * SUBLANES, DN_CONV_W)
    convout_ref[...] = cbuf_ref[:, SUBLANES + seq - (DN_CONV - 1):SUBLANES + seq, :]

    def read_state(g, s, h):
        return s0_ref[s, h]

    def write_state(g, s, h, val):
        sout_ref[s, h] = val

    group = (qkv, jnp.concatenate(spread(z_ref), axis=0), jnp.concatenate(spread(ba_ref), axis=0),
             jnp.concatenate(spread(gate_ref), axis=0))
    o = _dn_core([group], alog_ref[...], dtb_ref[...], nw_ref[...], read_state, write_state, n_bb, seq)[0]
    o_ref[...] = _gather_rows([o[bb * SUBLANES:(bb + 1) * SUBLANES] for bb in range(n_bb)], seq)


def _dn_sample(proj, ba, conv_w, alog_row, dtb_row, dn_nw, conv0, s0, row0, n_batch, seq):
    assert SUBLANES % seq == 0 and seq >= DN_CONV - 1
    n_bb = SUBLANES
    rows_in = n_bb * seq
    assert n_batch % n_bb == 0 and row0 % rows_in == 0
    rb0 = row0 // rows_in
    const1 = lambda i: (0, 0)
    return pl.pallas_call(
        functools.partial(_dn_sample_kernel, seq=seq, n_bb=n_bb),
        grid=(n_batch // n_bb,),
        in_specs=[
            pl.BlockSpec((rows_in, DN_CONV_W), lambda i: (rb0 + i, COL_QKV // DN_CONV_W)),
            pl.BlockSpec((rows_in, DN_V_W), lambda i: (rb0 + i, COL_Z // DN_V_W)),
            pl.BlockSpec((rows_in, LANES), lambda i: (rb0 + i, 0)),
            pl.BlockSpec((rows_in, D_MODEL), lambda i: (rb0 + i, COL_GA // D_MODEL)),
            pl.BlockSpec((DN_CONV, DN_CONV_W), const1),
            pl.BlockSpec((1, LANES), const1),
            pl.BlockSpec((1, LANES), const1),
            pl.BlockSpec((1, DN_DV), const1),
            pl.BlockSpec((n_bb, DN_CONV - 1, DN_CONV_W), lambda i: (i, 0, 0)),
            pl.BlockSpec((n_bb, DN_HEADS, DN_DK, DN_DV), lambda i: (i, 0, 0, 0)),
        ],
        out_specs=[
            pl.BlockSpec((rows_in, DN_V_W), lambda i: (i, 0)),
            pl.BlockSpec((n_bb, DN_CONV - 1, DN_CONV_W), lambda i: (i, 0, 0)),
            pl.BlockSpec((n_bb, DN_HEADS, DN_DK, DN_DV), lambda i: (i, 0, 0, 0)),
        ],
        out_shape=[
            jax.ShapeDtypeStruct((n_batch * seq, DN_V_W), F32),
            jax.ShapeDtypeStruct((n_batch, DN_CONV - 1, DN_CONV_W), F32),
            jax.ShapeDtypeStruct((n_batch, DN_HEADS, DN_DK, DN_DV), F32),
        ],
        scratch_shapes=[pltpu.VMEM((n_bb, 2 * SUBLANES, DN_CONV_W), F32)],
        compiler_params=_cparams(("arbitrary",)),
        name="dn_sample",
    )(proj, proj, ba, proj, conv_w, alog_row, dtb_row, dn_nw, conv0, s0)


def _masked_bucket(dist, ok):
    n = np.maximum(dist, 0)
    max_exact = REL_BUCKETS // 2
    large = max_exact + (np.log(np.maximum(n, 1).astype(np.float32) / max_exact)
                         / math.log(REL_MAX_DIST / max_exact) * (REL_BUCKETS - max_exact)).astype(np.int32)
    bucket = np.where(n < max_exact, n, np.minimum(large, REL_BUCKETS - 1))
    return np.where(ok, bucket, -1).astype(np.int32)


def _relbias_kernel(tab_ref, bucket_ref, o_ref):
    h = pl.program_id(0)
    bk = bucket_ref[...]
    acc = jnp.full(bk.shape, -jnp.inf, F32)
    for b in range(REL_BUCKETS):
        acc = jnp.where(bk == b, tab_ref[b * SW_HEADS + h], acc)
    o_ref[0] = acc


def _relbias(rel_table, bucket):
    nq, ns = bucket.shape
    return pl.pallas_call(
        _relbias_kernel,
        grid=(SW_HEADS,),
        in_specs=[
            pl.BlockSpec(memory_space=pltpu.SMEM),
            pl.BlockSpec((nq, ns), lambda h: (0, 0)),
        ],
        out_specs=pl.BlockSpec((1, nq, ns), lambda h: (h, 0, 0)),
        out_shape=jax.ShapeDtypeStruct((SW_HEADS, nq, ns), F32),
        compiler_params=_cparams(("arbitrary",)),
        name="relbias",
    )(rel_table.reshape(-1), bucket)


def _dup_halves(x):
    lo = _lane_iota(x.shape) < SW_HD
    xr = pltpu.roll(x, SW_HD, axis=1)
    return jnp.where(lo, x, xr).astype(BF16), jnp.where(lo, xr, x).astype(BF16)


def _sink_softmax_pv(s, sink, vv):
    m = jnp.maximum(jnp.max(s, axis=-1, keepdims=True), sink)
    p = jnp.exp(s - m)
    denom = jnp.sum(p, axis=-1, keepdims=True) + jnp.exp(sink - m)
    return _dot(p, vv) * (1.0 / denom)


def _swa_prompt_kernel(sinks_ref, q_ref, kc_ref, kp_ref, vc_ref, vp_ref, bias_ref, gate_ref, other_ref,
                       o_ref, klast_ref, vlast_ref, *, n_sub):
    i = pl.program_id(1)
    klast_ref[0] = kc_ref[(n_sub - 1) * SW_BLOCK:, :]
    vlast_ref[0] = vc_ref[(n_sub - 1) * SW_BLOCK:, :]
    kk_all = _dup_halves(jnp.concatenate([kp_ref[...], kc_ref[...]], axis=0))
    vv_all = _dup_halves(jnp.concatenate([vp_ref[...], vc_ref[...]], axis=0))
    lo = _lane_iota((SW_BLOCK, LANES)) < SW_HD
    for sub in range(n_sub):
        rows = slice(sub * SW_BLOCK, (sub + 1) * SW_BLOCK)
        keys = slice(sub * SW_BLOCK, (sub + 2) * SW_BLOCK)
        bias_rows = pl.ds(pl.multiple_of(jnp.minimum(i, 1) * SW_BLOCK, SW_BLOCK), SW_BLOCK) if sub == 0 \
            else pl.ds(SW_BLOCK, SW_BLOCK)
        for pair in range(SW_HEADS // 2):
            cols = slice(pair * LANES, (pair + 1) * LANES)
            qp = q_ref[rows, cols] * (SW_HD ** -0.5)
            outs = []
            for half in range(2):
                hq = 2 * pair + half
                kv = hq // SW_GROUP
                qm = jnp.where(lo if half == 0 else ~lo, qp, 0.0)
                s = _dot_nt(qm, kk_all[kv][keys]) + bias_ref[hq, bias_rows, :]
                outs.append(_sink_softmax_pv(s, sinks_ref[hq], vv_all[kv][keys]))
            o_ref[rows, cols] = other_ref[rows, cols] + _sigmoid(gate_ref[rows, cols]) * jnp.where(lo, outs[0], outs[1])


def _swa_prompt(proj, sinks, bias, other, n_batch, seq):
    assert seq % SW_BLOCK == 0 and WINDOW == SW_BLOCK
    nb = seq // SW_BLOCK
    n_sub = next(n for n in (4, 2, 1) if nb % n == 0)
    ns = nb // n_sub
    rows = n_sub * SW_BLOCK
    cur = lambda col: (lambda b, i: (b * ns + i, col))
    prev = lambda col: (lambda b, i: (b * nb + jnp.maximum(i * n_sub - 1, 0), col))
    return pl.pallas_call(
        functools.partial(_swa_prompt_kernel, n_sub=n_sub),
        grid=(n_batch, ns),
        in_specs=[
            pl.BlockSpec(memory_space=pltpu.SMEM),
            pl.BlockSpec((rows, SW_HEADS * SW_HD), cur(COL_SQ // (SW_HEADS * SW_HD))),
            pl.BlockSpec((rows, SW_KV_W), cur(COL_SK // SW_KV_W)),
            pl.BlockSpec((SW_BLOCK, SW_KV_W), prev(COL_SK // SW_KV_W)),
            pl.BlockSpec((rows, SW_KV_W), cur(COL_SV // SW_KV_W)),
            pl.BlockSpec((SW_BLOCK, SW_KV_W), prev(COL_SV // SW_KV_W)),
            pl.BlockSpec((SW_HEADS, 2 * SW_BLOCK, 2 * SW_BLOCK), lambda b, i: (0, 0, 0)),
            pl.BlockSpec((rows, D_MODEL), cur(COL_GB // D_MODEL)),
            pl.BlockSpec((rows, D_MODEL), lambda b, i: (b * ns + i, 0)),
        ],
        out_specs=[
            pl.BlockSpec((rows, SW_HEADS * SW_HD), lambda b, i: (b * ns + i, 0)),
            pl.BlockSpec((1, SW_BLOCK, SW_KV_W), lambda b, i: (b, 0, 0)),
            pl.BlockSpec((1, SW_BLOCK, SW_KV_W), lambda b, i: (b, 0, 0)),
        ],
        out_shape=[
            jax.ShapeDtypeStruct((n_batch * seq, SW_HEADS * SW_HD), F32),
            jax.ShapeDtypeStruct((n_batch, SW_BLOCK, SW_KV_W), F32),
            jax.ShapeDtypeStruct((n_batch, SW_BLOCK, SW_KV_W), F32),
        ],
        compiler_params=_cparams(("arbitrary", "arbitrary")),
        name="swa_prompt",
    )(sinks, proj, proj, proj, proj, proj, bias, proj, other)


def _swa_sample_kernel(q_ref, kn_ref, vn_ref, kc_ref, vc_ref, bias_ref, sink_ref, gate_ref, other_ref,
                       o_ref, ko_ref, vo_ref, kall_ref, vall_ref, *, seq, n_bb, n_cache):
    n_keys = kall_ref.shape[1]
    zeros_tail = jnp.zeros((n_bb, n_keys - n_cache - SUBLANES, LANES), F32)
    lo = _lane_iota((SUBLANES, LANES)) < SW_HD
    for ref, cache_ref, new_ref in ((kall_ref, kc_ref, kn_ref), (vall_ref, vc_ref, vn_ref)):
        ref[:, 0:n_cache, :] = cache_ref[...]
        for bb, piece in enumerate(_spread_rows(new_ref, n_bb, seq)):
            ref[bb, n_cache:n_cache + SUBLANES, :] = piece
        ref[:, n_cache + SUBLANES:, :] = zeros_tail
    ko_ref[...] = kall_ref[:, seq:seq + n_cache, :]
    vo_ref[...] = vall_ref[:, seq:seq + n_cache, :]
    outs = []
    for bb, q8 in enumerate(_spread_rows(q_ref, n_bb, seq)):
        kk = _dup_halves(kall_ref[bb])
        vv = _dup_halves(vall_ref[bb])
        pairs = []
        for kv in range(SW_KV_HEADS):
            pieces = []
            for g in range(SW_GROUP):
                hq = kv * SW_GROUP + g
                qp = q8[:, (hq // 2) * LANES:(hq // 2 + 1) * LANES]
                pieces.append(jnp.where(lo if hq % 2 == 0 else ~lo, qp, 0.0))
            qs = jnp.concatenate(pieces, axis=0) * (SW_HD ** -0.5)
            bias = jnp.concatenate([bias_ref[kv * SW_GROUP + g] for g in range(SW_GROUP)], axis=0)
            s = _dot_nt(qs, kk[kv]) + bias
            res = _sink_softmax_pv(s, _col(sink_ref[kv], 0), vv[kv])
            for g in range(0, SW_GROUP, 2):
                pairs.append(jnp.where(lo, res[g * SUBLANES:(g + 1) * SUBLANES],
                                       res[(g + 1) * SUBLANES:(g + 2) * SUBLANES]))
        outs.append(jnp.concatenate(pairs, axis=1))
    o_ref[...] = other_ref[...] + _sigmoid(gate_ref[...]) * _gather_rows(outs, seq)


def _swa_sample(proj, k_cache, v_cache, bias, bias_row0, sink_rows, other, row0, n_batch, seq):
    assert SUBLANES % seq == 0
    n_bb = SUBLANES
    rows_in = n_bb * seq
    n_cache = k_cache.shape[1]
    assert n_batch % n_bb == 0 and row0 % rows_in == 0 and n_cache % SUBLANES == 0
    n_keys = bias.shape[-1]
    rb0 = row0 // rows_in
    blk = lambda col: (lambda i: (rb0 + i, col))
    return pl.pallas_call(
        functools.partial(_swa_sample_kernel, seq=seq, n_bb=n_bb, n_cache=n_cache),
        grid=(n_batch // n_bb,),
        in_specs=[
            pl.BlockSpec((rows_in, SW_HEADS * SW_HD), blk(COL_SQ // (SW_HEADS * SW_HD))),
            pl.BlockSpec((rows_in, SW_KV_W), blk(COL_SK // SW_KV_W)),
            pl.BlockSpec((rows_in, SW_KV_W), blk(COL_SV // SW_KV_W)),
            pl.BlockSpec((n_bb, n_cache, SW_KV_W), lambda i: (i, 0, 0)),
            pl.BlockSpec((n_bb, n_cache, SW_KV_W), lambda i: (i, 0, 0)),
            pl.BlockSpec((SW_HEADS, SUBLANES, n_keys), lambda i: (0, bias_row0 // SUBLANES, 0)),
            pl.BlockSpec((SW_KV_HEADS, SW_GROUP * SUBLANES, n_keys), lambda i: (0, 0, 0)),
            pl.BlockSpec((rows_in, D_MODEL), blk(COL_GB // D_MODEL)),
            pl.BlockSpec((rows_in, D_MODEL), lambda i: (i, 0)),
        ],
        out_specs=[
            pl.BlockSpec((rows_in, SW_HEADS * SW_HD), lambda i: (i, 0)),
            pl.BlockSpec((n_bb, n_cache, SW_KV_W), lambda i: (i, 0, 0)),
            pl.BlockSpec((n_bb, n_cache, SW_KV_W), lambda i: (i, 0, 0)),
        ],
        out_shape=[
            jax.ShapeDtypeStruct((n_batch * seq, SW_HEADS * SW_HD), F32),
            jax.ShapeDtypeStruct(k_cache.shape, F32),
            jax.ShapeDtypeStruct(v_cache.shape, F32),
        ],
        scratch_shapes=[pltpu.VMEM((n_bb, n_keys, SW_KV_W), F32), pltpu.VMEM((n_bb, n_keys, SW_KV_W), F32)],
        compiler_params=_cparams(("arbitrary",)),
        name="swa_sample",
    )(proj, proj, proj, k_cache, v_cache, bias, sink_rows, proj, other)


def _mix_kernel(xp_ref, xs_ref, mp_ref, ms_ref, wo_ref, nw_ref, wr_ref, br_ref,
                x1_ref, h2_ref, route_ref, *, n_prompt_tiles):
    i = pl.program_id(0)

    def run(x_ref, mixed_ref):
        x1 = x_ref[...] + _dot(mixed_ref[...], wo_ref[...])
        x1_ref[...] = x1
        h2 = x1 * lax.rsqrt(jnp.mean(x1 * x1, axis=-1, keepdims=True) + EPS) * nw_ref[...]
        _to_tiles(h2_ref, h2)
        logits = _dot(h2, wr_ref[...]) + br_ref[...]
        lane = _lane_iota(logits.shape)
        lanef = lane.astype(F32)
        big = float(2 * LANES)
        is_g = lane < N_GROUPS
        gl = jnp.where(is_g, logits, -jnp.inf)
        gmax = jnp.max(gl, axis=-1, keepdims=True)
        gval = 1.0 / jnp.sum(jnp.where(is_g, jnp.exp(gl - gmax), 0.0), axis=-1, keepdims=True)
        grp = jnp.min(jnp.where(gl == gmax, lanef, big), axis=-1, keepdims=True)
        e_grp = ((lane - N_GROUPS) >> 3).astype(F32)
        is_e = (lane >= N_GROUPS) & (lane < N_GROUPS + N_EXPERTS) & (e_grp == grp)
        el = jnp.where(is_e, logits, -jnp.inf)
        v1 = jnp.max(el, axis=-1, keepdims=True)
        i1 = jnp.min(jnp.where(el == v1, lanef, big), axis=-1, keepdims=True)
        el2 = jnp.where(lanef == i1, -jnp.inf, el)
        v2 = jnp.max(el2, axis=-1, keepdims=True)
        i2 = jnp.min(jnp.where(el2 == v2, lanef, big), axis=-1, keepdims=True)
        e2 = jnp.exp(v2 - v1)
        w1 = gval / (1.0 + e2)
        w2 = gval * e2 / (1.0 + e2)
        route_ref[...] = jnp.where(lane == 0, i1 - N_GROUPS,
                                   jnp.where(lane == 1, i2 - N_GROUPS,
                                             jnp.where(lane == 2, w1, jnp.where(lane == 3, w2, 0.0))))

    @pl.when(i < n_prompt_tiles)
    def _():
        run(xp_ref, mp_ref)

    @pl.when(i >= n_prompt_tiles)
    def _():
        run(xs_ref, ms_ref)


def _mix(xp, xs, mixed_p, mixed_s, w_out, norm_w, w_router, b_router):
    tp, ts = xp.shape[0], xs.shape[0]
    tm = _token_tile(tp, ts, cands=BIG_TILES)
    npt, nst = tp // tm, ts // tm
    const = lambda i: (0, 0)
    row = lambda i: (i, 0)
    return pl.pallas_call(
        functools.partial(_mix_kernel, n_prompt_tiles=npt),
        grid=(npt + nst,),
        in_specs=[
            pl.BlockSpec((tm, D_MODEL), lambda i: (jnp.minimum(i, npt - 1), 0)),
            pl.BlockSpec((tm, D_MODEL), lambda i: (jnp.maximum(i - npt, 0), 0)),
            pl.BlockSpec((tm, D_MODEL), lambda i: (jnp.minimum(i, npt - 1), 0)),
            pl.BlockSpec((tm, D_MODEL), lambda i: (jnp.maximum(i - npt, 0), 0)),
            pl.BlockSpec((D_MODEL, D_MODEL), const),
            pl.BlockSpec((1, D_MODEL), const),
            pl.BlockSpec((D_MODEL, LANES), const),
            pl.BlockSpec((1, LANES), const),
        ],
        out_specs=[
            pl.BlockSpec((tm, D_MODEL), row),
            pl.BlockSpec((tm,) + TOK_TILE, lambda i: (i, 0, 0)),
            pl.BlockSpec((tm, LANES), row),
        ],
        out_shape=[
            jax.ShapeDtypeStruct((tp + ts, D_MODEL), F32),
            jax.ShapeDtypeStruct((tp + ts,) + TOK_TILE, F32),
            jax.ShapeDtypeStruct((tp + ts, LANES), F32),
        ],
        compiler_params=_cparams(("arbitrary",)),
        name="mix_router",
    )(xp, xs, mixed_p, mixed_s, w_out, norm_w, w_router, b_router)


def _rank_kernel(route_ref, dest_ref, meta_ref, rank_ref, cnt_ref, *, tile, blk):
    phase = pl.program_id(0)
    i = pl.program_id(1)
    shape = (tile, LANES)
    lane = _lane_iota(shape)
    lanef = lane.astype(F32)
    r = route_ref[...]
    oh0 = lanef == _col(r, 0)
    oh1 = lanef == _col(r, 1)
    rows = pl.ds(pl.multiple_of(i * tile, tile), tile)

    @pl.when(phase == 0)
    def _():
        @pl.when(i == 0)
        def _():
            cnt_ref[...] = jnp.zeros(cnt_ref.shape, F32)

        oh = jnp.where(oh0 | oh1, 1.0, 0.0)
        tri = jnp.where(_row_iota((tile, tile)) > _lane_iota((tile, tile)), 1.0, 0.0)
        before = _dot(tri, oh) + cnt_ref[0:1, :]
        rank0 = jnp.sum(jnp.where(oh0, before, 0.0), axis=-1, keepdims=True)
        rank1 = jnp.sum(jnp.where(oh1, before, 0.0), axis=-1, keepdims=True)
        rank_ref[rows, :] = jnp.where(lane == 0, rank0, jnp.where(lane == 1, rank1, 0.0))
        cnt_ref[0:1, :] = cnt_ref[0:1, :] + jnp.sum(oh, axis=0, keepdims=True)

    @pl.when(phase == 1)
    def _():
        cnt = cnt_ref[0:1, :]
        padded = jnp.floor((cnt + (blk - 1)) / blk) * blk
        before_lane = jnp.where(_row_iota((LANES, LANES)) < _lane_iota((LANES, LANES)), 1.0, 0.0)
        start = _dot_exact(jnp.broadcast_to(padded, (SUBLANES, LANES)), before_lane)[0:1, :]
        rk = rank_ref[rows, :]
        d0 = jnp.sum(jnp.where(oh0, start, 0.0), axis=-1, keepdims=True) + _col(rk, 0)
        d1 = jnp.sum(jnp.where(oh1, start, 0.0), axis=-1, keepdims=True) + _col(rk, 1)
        dest_ref[...] = jnp.where(lane == 0, d0, jnp.where(lane == 1, d1, 0.0)).astype(I32)

        @pl.when(i == 0)
        def _():
            end = start + padded
            mshape = meta_ref.shape
            blk_start = (_row_iota(mshape) * blk).astype(F32)
            hit = (_lane_iota(mshape) < N_EXPERTS) & (end <= blk_start)
            be = jnp.minimum(jnp.sum(jnp.where(hit, 1.0, 0.0), axis=-1, keepdims=True), N_EXPERTS - 1.0)
            n_used = _col(end, N_EXPERTS - 1) / blk
            ml = _lane_iota(mshape)
            mine = ml.astype(F32) == be
            seg_start = jnp.sum(jnp.where(mine, start, 0.0), axis=-1, keepdims=True)
            seg_count = jnp.sum(jnp.where(mine, cnt, 0.0), axis=-1, keepdims=True)
            n_valid = jnp.clip(seg_count - (blk_start[:, 0:1] - seg_start), 0.0, float(blk))
            meta_ref[...] = jnp.where(ml == 0, be, jnp.where(ml == 1, n_used,
                                                              jnp.where(ml == 2, n_valid, 0.0))).astype(I32)


def _rank(route, tile, blk, n_blocks):
    t = route.shape[0]
    nt = t // tile
    nbp = -(-n_blocks // SUBLANES) * SUBLANES
    return pl.pallas_call(
        functools.partial(_rank_kernel, tile=tile, blk=blk),
        grid=(2, nt),
        in_specs=[pl.BlockSpec((tile, LANES), lambda p, i: (i, 0))],
        out_specs=[
            pl.BlockSpec((tile, LANES), lambda p, i: (i * p, 0)),
            pl.BlockSpec((nbp, LANES), lambda p, i: (0, 0)),
        ],
        out_shape=[
            jax.ShapeDtypeStruct((t, LANES), I32),
            jax.ShapeDtypeStruct((nbp, LANES), I32),
        ],
        scratch_shapes=[pltpu.VMEM((t, LANES), F32), pltpu.VMEM((SUBLANES, LANES), F32)],
        compiler_params=_cparams(("arbitrary", "arbitrary")),
        name="moe_rank",
    )(route)


TOK_TILE = (D_MODEL // LANES, LANES)


def _to_tiles(ref, x):
    for j in range(TOK_TILE[0]):
        ref[:, j, :] = x[:, j * LANES:(j + 1) * LANES]


def _from_tiles(ref):
    return jnp.concatenate([ref[:, j, :] for j in range(TOK_TILE[0])], axis=1)


def _tile_copy(src, src_row, dst, dst_row, sem):
    return pltpu.make_async_copy(src.at[src_row], dst.at[dst_row], sem)


def _row_copy(src, src_row, dst, group, sub, sem):
    return pltpu.make_async_copy(src.at[pl.ds(src_row, 1)], dst.at[group, pl.ds(sub, 1)], sem)


def _last_used(i, nu_ref):
    return jnp.minimum(i, jnp.maximum(nu_ref[0] - 1, 0))


def _for_rows(n, body):
    def group(g, carry):
        for u in range(SUBLANES):
            body(g, u)
        return carry

    def single(t, carry):
        body(t // SUBLANES, t % SUBLANES)
        return carry

    n_groups = n // SUBLANES
    lax.fori_loop(0, n_groups, group, 0)
    lax.fori_loop(n_groups * SUBLANES, n, single, 0)


def _expert_kernel(be_ref, nu_ref, nv_ref, dest_ref, h2_ref, wg_ref, wu_ref, wd_ref, y_ref,
                   xbuf_ref, wgu_ref, wdn_ref, inv_ref, sem, *, blk):
    i = pl.program_id(0)
    n_used = nu_ref[0]
    used = i < n_used
    slot = i % 2
    blk_i = _last_used(i, nu_ref)
    fresh = (i == 0) | (be_ref[blk_i] != be_ref[jnp.maximum(blk_i - 1, 0)])

    def gather(block, into):
        def issue(g, u):
            t = g * SUBLANES + u
            _tile_copy(h2_ref, inv_ref[block * blk + t], xbuf_ref.at[into], t,
                       sem.at[into]).start(priority=u % 2 if isinstance(u, int) else 0)

        _for_rows(nv_ref[block], issue)

    @pl.when(i == 0)
    def _():
        def place(t, carry):
            for k in range(2):
                inv_ref[dest_ref[2 * t + k]] = t
            return carry

        lax.fori_loop(0, dest_ref.shape[0] // 2, place, 0, unroll=8)
        xbuf_ref[...] = jnp.zeros(xbuf_ref.shape, F32)
        gather(0, 0)

    @pl.when(i + 1 < n_used)
    def _():
        gather(i + 1, 1 - slot)

    @pl.when(used & fresh)
    def _():
        wgu_ref[:, :D_EXPERT] = wg_ref[0].astype(BF16)
        wgu_ref[:, D_EXPERT:] = wu_ref[0].astype(BF16)
        wdn_ref[...] = wd_ref[0].astype(BF16)

    @pl.when(used)
    def _():
        _for_rows(nv_ref[i], lambda g, u: _tile_copy(h2_ref, 0, xbuf_ref.at[slot], 0, sem.at[slot]).wait())
        gu = jnp.dot(_from_tiles(xbuf_ref.at[slot]).astype(BF16), wgu_ref[...], preferred_element_type=F32)
        g = gu[:, :D_EXPERT]
        hidden = (g * _sigmoid(g) * gu[:, D_EXPERT:]).astype(BF16)
        y_ref[...] = jnp.dot(hidden, wdn_ref[...], preferred_element_type=F32)

    @pl.when(jnp.logical_not(used))
    def _():
        y_ref[...] = jnp.zeros(y_ref.shape, F32)


def _experts(block_expert, n_used, n_valid, dest_flat, h2, w_gate, w_up, w_down, blk):
    n_blocks = block_expert.shape[0]
    wsel = lambda i, be, nu, nv, de: (be[_last_used(i, nu)], 0, 0)
    return pl.pallas_call(
        functools.partial(_expert_kernel, blk=blk),
        grid_spec=pltpu.PrefetchScalarGridSpec(
            num_scalar_prefetch=4,
            grid=(n_blocks,),
            in_specs=[
                pl.BlockSpec(memory_space=pl.ANY),
                pl.BlockSpec((1, D_MODEL, D_EXPERT), wsel),
                pl.BlockSpec((1, D_MODEL, D_EXPERT), wsel),
                pl.BlockSpec((1, D_EXPERT, D_MODEL), wsel),
            ],
            out_specs=pl.BlockSpec((blk, D_MODEL), lambda i, be, nu, nv, de: (i, 0)),
            scratch_shapes=[
                pltpu.VMEM((2, blk) + TOK_TILE, F32),
                pltpu.VMEM((D_MODEL, 2 * D_EXPERT), BF16),
                pltpu.VMEM((D_EXPERT, D_MODEL), BF16),
                pltpu.SMEM((n_blocks * blk,), I32),
                pltpu.SemaphoreType.DMA((2,)),
            ],
        ),
        out_shape=jax.ShapeDtypeStruct((n_blocks * blk, D_MODEL), F32),
        compiler_params=_cparams(("arbitrary",)),
        name="moe_experts",
    )(block_expert, n_used, n_valid, dest_flat, h2, w_gate, w_up, w_down)


def _combine_kernel(dest_ref, x1_ref, route_ref, nw_ref, ys_ref, yp_ref, ysm_ref, ybuf_ref, sem,
                    *, tile, n_prompt_tiles):
    i = pl.program_id(0)
    slot = i % 2

    groups = tile // SUBLANES

    def gather(step, into):
        def issue(g, carry):
            for u in range(SUBLANES):
                for k in range(2):
                    slot_row = dest_ref[2 * step * tile + g * (2 * SUBLANES) + 2 * u + k]
                    _row_copy(ys_ref, slot_row, ybuf_ref.at[into], k * groups + g, u,
                              sem.at[into]).start(priority=k)
            return carry

        lax.fori_loop(0, groups, issue, 0)

    @pl.when(i == 0)
    def _():
        gather(0, 0)

    @pl.when(i + 1 < pl.num_programs(0))
    def _():
        gather(i + 1, 1 - slot)

    def drain(t, carry):
        _row_copy(ys_ref, 0, ybuf_ref.at[slot], 0, 0, sem.at[slot]).wait()
        return carry

    lax.fori_loop(0, 2 * tile, drain, 0, unroll=8)
    r = route_ref[...]
    y = (ybuf_ref[slot, 0:groups].reshape(tile, D_MODEL) * _col(r, 2)
         + ybuf_ref[slot, groups:2 * groups].reshape(tile, D_MODEL) * _col(r, 3))
    x2 = x1_ref[...] + y
    out = x2 * lax.rsqrt(jnp.mean(x2 * x2, axis=-1, keepdims=True) + EPS) * nw_ref[...]

    @pl.when(i < n_prompt_tiles)
    def _():
        yp_ref[...] = out

    @pl.when(i >= n_prompt_tiles)
    def _():
        ysm_ref[...] = out


def _combine(dest_flat, x1, route, norm_w, ys, tp, ts):
    tile = _token_tile(tp, ts, cands=BIG_TILES)
    npt, nst = tp // tile, ts // tile
    return pl.pallas_call(
        functools.partial(_combine_kernel, tile=tile, n_prompt_tiles=npt),
        grid_spec=pltpu.PrefetchScalarGridSpec(
            num_scalar_prefetch=1,
            grid=(npt + nst,),
            in_specs=[
                pl.BlockSpec((tile, D_MODEL), lambda i, d: (i, 0)),
                pl.BlockSpec((tile, LANES), lambda i, d: (i, 0)),
                pl.BlockSpec((1, D_MODEL), lambda i, d: (0, 0)),
                pl.BlockSpec(memory_space=pl.ANY),
            ],
            out_specs=[
                pl.BlockSpec((tile, D_MODEL), lambda i, d: (jnp.minimum(i, npt - 1), 0)),
                pl.BlockSpec((tile, D_MODEL), lambda i, d: (jnp.maximum(i - npt, 0), 0)),
            ],
            scratch_shapes=[pltpu.VMEM((2, 2 * tile // SUBLANES, SUBLANES, D_MODEL), F32), pltpu.SemaphoreType.DMA((2,))],
        ),
        out_shape=[
            jax.ShapeDtypeStruct((tp, D_MODEL), F32),
            jax.ShapeDtypeStruct((ts, D_MODEL), F32),
        ],
        compiler_params=_cparams(("arbitrary",)),
        name="moe_combine",
    )(dest_flat, x1, route, norm_w, ys)


def _layer(xp, xs, n_batch, seq, s_batch, s_seq, conv_state, dn_state, k_cache, v_cache,
           w_in, conv_w, a_log, dt_bias, dn_norm_w, sinks, rel_bias, w_out, norm_mix_w, norm_ffn_w,
           w_rg, b_rg, w_re, b_re, w_gate, w_up, w_down, norm_final_w):
    tp, ts = xp.shape[0], xs.shape[0]
    t_all = tp + ts
    row = lambda v: v.reshape(1, -1).astype(F32)

    o = np.cumsum((0, DN_QK_W, DN_QK_W, DN_V_W, DN_V_W, DN_HEADS, DN_HEADS, SW_HEADS * SW_HD, SW_KV_W, SW_KV_W,
                   D_MODEL, D_MODEL)).tolist()
    w_big = jnp.concatenate([w_in[:, o[0]:o[4]], w_in[:, o[6]:o[7]], w_in[:, o[9]:o[11]], w_in[:, o[7]:o[9]]],
                            axis=1).astype(BF16)
    w_small = jnp.pad(w_in[:, o[4]:o[6]], ((0, 0), (0, LANES - 2 * DN_HEADS))).astype(BF16)
    head_row = lambda v: jnp.pad(v.astype(F32), (DN_HEADS, LANES - 2 * DN_HEADS)).reshape(1, LANES)
    w_router = jnp.pad(jnp.concatenate([w_rg, w_re], axis=1),
                       ((0, 0), (0, LANES - N_GROUPS - N_EXPERTS))).astype(BF16)
    b_router = jnp.pad(jnp.concatenate([b_rg, b_re]).astype(F32), (0, LANES - N_GROUPS - N_EXPERTS)).reshape(1, LANES)

    conv0 = jnp.zeros((n_batch, DN_CONV - 1, DN_CONV_W), F32)
    proj, ba, conv_tail = _inproj(xp, xs, row(norm_mix_w), w_big, w_small, conv_w.astype(F32), conv0, seq)
    p_conv = conv_tail[:, SUBLANES - (DN_CONV - 1):, :]

    dn0 = jnp.zeros((n_batch, DN_HEADS, DN_DK, DN_DV), F32)
    oa_p, p_dn = _dn_prompt(proj, ba, head_row(a_log), head_row(dt_bias), row(dn_norm_w), dn0, n_batch, seq)
    oa_s, s_conv, s_dn = _dn_sample(proj, ba, conv_w.astype(F32), head_row(a_log), head_row(dt_bias), row(dn_norm_w),
                                    conv_state, dn_state, tp, s_batch, s_seq)

    n_cache = k_cache.shape[1]
    n_keys = -(-(n_cache + SUBLANES) // LANES) * LANES
    assert n_keys == 2 * SW_BLOCK, "the prompt and sample bias tables share one (rows, keys) array"
    dist_p = np.arange(SW_BLOCK)[:, None] - (np.arange(2 * SW_BLOCK)[None, :] - SW_BLOCK)
    dist_s = n_cache + np.arange(SUBLANES)[:, None] - np.arange(n_keys)[None, :]
    in_window = lambda dist: (dist >= 0) & (dist < WINDOW)
    has_prev = np.arange(2 * SW_BLOCK)[None, :] >= SW_BLOCK
    buckets = np.concatenate([_masked_bucket(dist_p, in_window(dist_p) & has_prev),
                              _masked_bucket(dist_p, in_window(dist_p)),
                              _masked_bucket(dist_s, in_window(dist_s))], axis=0)
    bias = _relbias(rel_bias.astype(F32), jnp.asarray(buckets))
    mixed_p, p_k, p_v = _swa_prompt(proj, sinks.astype(F32), bias, oa_p, n_batch, seq)
    sink_rows = jnp.broadcast_to(jnp.repeat(sinks.astype(F32).reshape(SW_KV_HEADS, SW_GROUP), SUBLANES, axis=1)[:, :, None],
                                 (SW_KV_HEADS, SW_GROUP * SUBLANES, n_keys))
    mixed_s, s_k, s_v = _swa_sample(proj, k_cache.reshape(s_batch, n_cache, SW_KV_W),
                                    v_cache.reshape(s_batch, n_cache, SW_KV_W), bias, 2 * SW_BLOCK, sink_rows, oa_s,
                                    tp, s_batch, s_seq)

    x1, h2, route = _mix(xp, xs, mixed_p, mixed_s, w_out.astype(BF16), row(norm_ffn_w), w_router, b_router)

    n_blocks = -(-2 * t_all // MOE_BLOCK) + N_EXPERTS
    dest, meta = _rank(route, _token_tile(t_all, cands=BIG_TILES), MOE_BLOCK, n_blocks)
    dest_flat = dest[:, :2].reshape(-1)
    block_expert = meta[:n_blocks, 0]
    n_used = meta[0:1, 1]
    n_valid = meta[:n_blocks, 2]
    ys = _experts(block_expert, n_used, n_valid, dest_flat, h2, w_gate, w_up, w_down, MOE_BLOCK)
    y_p, y_s = _combine(dest_flat, x1, route, row(norm_final_w), ys, tp, ts)

    kv_shape = (n_batch, WINDOW, SW_KV_HEADS, SW_HD)
    return (y_p, y_s, p_conv, p_dn, p_k.reshape(kv_shape), p_v.reshape(kv_shape), s_conv, s_dn,
            s_k.reshape(k_cache.shape), s_v.reshape(v_cache.shape))


def kernel(x_prompt, x_sample, state_dn_conv, state_dn, cache_swa_k, cache_swa_v, w_in, conv_w, a_log, dt_bias, dn_norm_w, sinks, rel_bias, w_out, norm_mix_w, norm_ffn_w, w_router_group, b_router_group, w_router_expert, b_router_expert, w_gate, w_up, w_down, norm_final_w):
    depth = w_in.shape[0]
    assert depth == 1, "the final-norm fusion below assumes a single layer"
    n_batch, seq, _ = x_prompt.shape
    s_batch, s_seq, _ = x_sample.shape
    outs = _layer(x_prompt.reshape(-1, D_MODEL), x_sample.reshape(-1, D_MODEL), n_batch, seq, s_batch, s_seq,
                  state_dn_conv[0], state_dn[0], cache_swa_k[0], cache_swa_v[0],
                  w_in[0], conv_w[0], a_log[0], dt_bias[0], dn_norm_w[0], sinks[0], rel_bias,
                  w_out[0], norm_mix_w[0], norm_ffn_w[0], w_router_group[0], b_router_group[0],
                  w_router_expert[0], b_router_expert[0], w_gate[0], w_up[0], w_down[0], norm_final_w)
    y_p, y_s, p_conv, p_dn, p_k, p_v, s_conv, s_dn, s_k, s_v = outs
    return (y_p.reshape(x_prompt.shape), y_s.reshape(x_sample.shape), p_conv[None], p_dn[None], p_k[None], p_v[None],
            s_conv[None], s_dn[None], s_k[None], s_v[None])
```

```python
import functools
import math

import jax
import jax.numpy as jnp
import numpy as np
from jax import lax
from jax.experimental import pallas as pl
from jax.experimental.pallas import tpu as pltpu

F32 = jnp.float32
BF16 = jnp.bfloat16
I32 = jnp.int32

D_MODEL = 1024
DN_HEADS = 8
DN_DK = 128
DN_DV = 128
DN_CONV = 4
DN_CHUNK = 64
DN_QK_W = DN_HEADS * DN_DK
DN_V_W = DN_HEADS * DN_DV
DN_CONV_W = 2 * DN_QK_W + DN_V_W
SW_HEADS = 16
SW_KV_HEADS = 2
SW_GROUP = SW_HEADS // SW_KV_HEADS
SW_HD = 64
SW_KV_W = SW_KV_HEADS * SW_HD
WINDOW = 128
SW_BLOCK = 128
REL_BUCKETS = 32
REL_MAX_DIST = 128
N_GROUPS = 8
EXP_PER_GROUP = 8
N_EXPERTS = N_GROUPS * EXP_PER_GROUP
D_EXPERT = 256
MOE_BLOCK = 304
EPS = 1e-6

LANES = 128
SUBLANES = 8
VMEM_LIMIT = 56 * 1024 * 1024

COL_QKV = 0
COL_Z = 3072
COL_SQ = 4096
COL_GA = 5120
COL_GB = 6144
COL_SK = 7168
COL_SV = 7296
PROJ_W = 7424
PROJ_CHUNK = 512


def _cparams(sem):
    return pltpu.CompilerParams(dimension_semantics=sem, vmem_limit_bytes=VMEM_LIMIT)


def _sigmoid(x):
    return 0.5 * jnp.tanh(0.5 * x) + 0.5


def _dot(a, b):
    return jnp.dot(a.astype(BF16), b.astype(BF16), preferred_element_type=F32)


def _dot_nt(a, b):
    return lax.dot_general(a.astype(BF16), b.astype(BF16), (((1,), (1,)), ((), ())), preferred_element_type=F32)


def _dot_tn(a, b):
    return lax.dot_general(a.astype(BF16), b.astype(BF16), (((0,), (0,)), ((), ())), preferred_element_type=F32)


def _dot_exact(a, b):
    return jnp.dot(a, b, precision=lax.Precision.HIGHEST, preferred_element_type=F32)


def _lane_iota(shape):
    return lax.broadcasted_iota(I32, shape, len(shape) - 1)


def _row_iota(shape):
    return lax.broadcasted_iota(I32, shape, len(shape) - 2)


def _col(x, j):
    return jnp.sum(jnp.where(_lane_iota(x.shape) == j, x, 0.0), axis=-1, keepdims=True)


BIG_TILES = (512, 256, 128, 64, 32, 16, 8)


def _token_tile(*sizes, cands=BIG_TILES[1:]):
    for t in cands:
        if all(s % t == 0 for s in sizes):
            return t
    raise ValueError(f"token counts {sizes} need a common tile that is a multiple of 8")


def _inproj_kernel(xp_ref, xs_ref, nw_ref, wb_ref, ws_ref, convw_ref, conv0_ref, proj_ref, ba_ref, tail_ref, cbuf_ref,
                   *, n_prompt_tiles, tiles_per_seq):
    i = pl.program_id(0)
    tm = xp_ref.shape[0]
    hist = SUBLANES - (DN_CONV - 1)

    def project(x_ref, conv):
        x = x_ref[...]
        h = (x * lax.rsqrt(jnp.mean(x * x, axis=-1, keepdims=True) + EPS) * nw_ref[...]).astype(BF16)
        ba_ref[...] = jnp.dot(h, ws_ref[...], preferred_element_type=F32)
        top = _row_iota((SUBLANES, PROJ_CHUNK))
        starts = list(range(0, PROJ_W, PROJ_CHUNK))
        if conv:
            with_conv = [c for c in starts if c + PROJ_CHUNK <= COL_QKV + DN_CONV_W]
            plain = [c for c in starts if c not in with_conv]
            starts = [c for pair in zip(with_conv, plain) for c in pair] + plain[len(with_conv):]
        for c0 in starts:
            c1 = min(c0 + PROJ_CHUNK, PROJ_W)
            cur = jnp.dot(h, wb_ref[:, c0:c1], preferred_element_type=F32)
            if conv and c1 <= COL_QKV + DN_CONV_W:
                prev = cbuf_ref[:, c0:c1]
                acc = cur * convw_ref[DN_CONV - 1:DN_CONV, c0:c1]
                for s in range(1, DN_CONV):
                    sh = pltpu.roll(cur, s, axis=0)
                    head = jnp.where(top < s, pltpu.roll(prev, s, axis=0), sh[:SUBLANES])
                    sh = jnp.concatenate([head, sh[SUBLANES:]], axis=0)
                    acc = acc + sh * convw_ref[DN_CONV - 1 - s:DN_CONV - s, c0:c1]
                cbuf_ref[:, c0:c1] = cur[tm - SUBLANES:]
                cur = acc * _sigmoid(acc)
            proj_ref[:, c0:c1] = cur

    @pl.when(i < n_prompt_tiles)
    def _():
        @pl.when(i % tiles_per_seq == 0)
        def _():
            cbuf_ref[...] = jnp.zeros(cbuf_ref.shape, F32)
            cbuf_ref[hist:SUBLANES, :] = conv0_ref[0]

        project(xp_ref, True)
        tail_ref[0] = cbuf_ref[...]

    @pl.when(i >= n_prompt_tiles)
    def _():
        project(xs_ref, False)


def _inproj(xp, xs, norm_w, w_big, w_small, conv_w, conv0, seq):
    tp, ts = xp.shape[0], xs.shape[0]
    tm = _token_tile(tp, ts, seq)
    assert COL_QKV == 0 and DN_CONV_W % PROJ_CHUNK == 0
    npt, nst = tp // tm, ts // tm
    tps = seq // tm
    const = lambda i: (0, 0)
    seq_of = lambda i: (jnp.minimum(i, npt - 1) // tps, 0, 0)
    return pl.pallas_call(
        functools.partial(_inproj_kernel, n_prompt_tiles=npt, tiles_per_seq=tps),
        grid=(npt + nst,),
        in_specs=[
            pl.BlockSpec((tm, D_MODEL), lambda i: (jnp.minimum(i, npt - 1), 0)),
            pl.BlockSpec((tm, D_MODEL), lambda i: (jnp.maximum(i - npt, 0), 0)),
            pl.BlockSpec((1, D_MODEL), const),
            pl.BlockSpec((D_MODEL, PROJ_W), const, pipeline_mode=pl.Buffered(1)),
            pl.BlockSpec((D_MODEL, LANES), const),
            pl.BlockSpec((DN_CONV, DN_CONV_W), const),
            pl.BlockSpec((1, DN_CONV - 1, DN_CONV_W), seq_of),
        ],
        out_specs=[
            pl.BlockSpec((tm, PROJ_W), lambda i: (i, 0)),
            pl.BlockSpec((tm, LANES), lambda i: (i, 0)),
            pl.BlockSpec((1, SUBLANES, DN_CONV_W), seq_of),
        ],
        out_shape=[
            jax.ShapeDtypeStruct((tp + ts, PROJ_W), F32),
            jax.ShapeDtypeStruct((tp + ts, LANES), F32),
            jax.ShapeDtypeStruct((tp // seq, SUBLANES, DN_CONV_W), F32),
        ],
        scratch_shapes=[pltpu.VMEM((SUBLANES, DN_CONV_W), F32)],
        compiler_params=_cparams(("arbitrary",)),
        name="inproj",
    )(xp, xs, norm_w, w_big, w_small, conv_w, conv0)


def _dn_core(groups, alog, dtb, nw, read_state, write_state, n_seg, seg_valid):
    rows = groups[0][0].shape[0]
    sr = rows // n_seg
    assert sr * n_seg == rows and sr & (sr - 1) == 0 and rows <= LANES
    seg_shift = sr.bit_length() - 1
    ri = _row_iota((rows, rows))
    ci = _lane_iota((rows, rows))
    incl = ri >= ci
    strict = ri > ci
    if n_seg > 1:
        same = (ri >> seg_shift) == (ci >> seg_shift)
        incl = incl & same
        strict = strict & same
    l_incl = incl.astype(F32)
    eye = (ri == ci).astype(F32)
    levels = max(1, math.ceil(math.log2(seg_valid)))

    beta_all, gsum_all, gtot_all, gsum_t = [], [], [], []
    for _, _, ba, _ in groups:
        b_all = _sigmoid(ba)
        sp = ba + dtb
        softplus = jnp.maximum(sp, 0.0) + jnp.log1p(jnp.exp(-jnp.abs(sp)))
        g_all = -jnp.exp(alog) * softplus
        if seg_valid < sr:
            live = (_row_iota((rows, LANES)) & (sr - 1)) < seg_valid
            b_all = jnp.where(live, b_all, 0.0)
            g_all = jnp.where(live, g_all, 0.0)
        gs = _dot_exact(l_incl, g_all)
        beta_all.append(b_all)
        gsum_all.append(gs)
        gtot_all.append(_dot_exact(same.astype(F32), g_all) if n_seg > 1 else gs[rows - 1:rows, :])
        padded = gs if rows == LANES else jnp.concatenate([gs, jnp.zeros((LANES - rows, LANES), F32)], axis=0)
        gsum_t.append(padded.T)

    probs = [(g, h) for g in range(len(groups)) for h in range(DN_HEADS)]
    segs = range(n_seg)
    q, k, v, kb, beta, gsum, gtot = {}, {}, {}, {}, {}, {}, {}
    for p in probs:
        g, h = p
        qkv = groups[g][0]
        qh = qkv[:, h * DN_DK:(h + 1) * DN_DK]
        kh = qkv[:, DN_QK_W + h * DN_DK:DN_QK_W + (h + 1) * DN_DK]
        v[p] = qkv[:, 2 * DN_QK_W + h * DN_DV:2 * DN_QK_W + (h + 1) * DN_DV]
        q[p] = qh * lax.rsqrt(jnp.sum(qh * qh, axis=-1, keepdims=True) + 1e-6) * (DN_DK ** -0.5)
        k[p] = kh * lax.rsqrt(jnp.sum(kh * kh, axis=-1, keepdims=True) + 1e-6)
        beta[p] = _col(beta_all[g], h)
        gsum[p] = _col(gsum_all[g], DN_HEADS + h)
        gtot[p] = _col(gtot_all[g], DN_HEADS + h)
        kb[p] = k[p] * beta[p]
    kq = {p: _dot_nt(jnp.concatenate([kb[p], q[p]], axis=0), k[p]) for p in probs}
    gamma = {(g, h): jnp.exp(jnp.where(incl, gsum[(g, h)] - gsum_t[g][DN_HEADS + h:DN_HEADS + h + 1, :rows], -jnp.inf))
             for g, h in probs}
    attn = {p: kq[p][rows:] * gamma[p] for p in probs}
    pw = {p: -jnp.where(strict, kq[p][:rows] * gamma[p], 0.0) for p in probs}
    t = {p: eye + pw[p] for p in probs}
    for _ in range(1, levels):
        pw = {p: _dot(pw[p], pw[p]) for p in probs}
        t = {p: t[p] + _dot(t[p], pw[p]) for p in probs}
    eg = {p: jnp.exp(gsum[p]) for p in probs}
    uw = {p: _dot(t[p], jnp.concatenate([v[p] * beta[p], kb[p] * eg[p]], axis=1)) for p in probs}
    qg = {p: q[p] * eg[p] for p in probs}
    state = {(p, s): read_state(p[0], s, p[1]) for p in probs for s in segs}
    wq = {(p, s): _dot(jnp.concatenate([uw[p][s * sr:(s + 1) * sr, DN_DV:], qg[p][s * sr:(s + 1) * sr]], axis=0),
                       state[(p, s)]) for p in probs for s in segs}
    join = lambda pieces: pieces[0] if len(pieces) == 1 else jnp.concatenate(pieces, axis=0)
    v_new = {p: uw[p][:, :DN_DV] - join([wq[(p, s)][:sr] for s in segs]) for p in probs}
    o = {p: join([wq[(p, s)][sr:] for s in segs]) + _dot(attn[p], v_new[p]) for p in probs}
    kd = {p: k[p] * jnp.exp(gtot[p] - gsum[p]) for p in probs}
    for p in probs:
        for s in segs:
            r0 = s * sr if n_seg > 1 else 0
            decay = jnp.exp(gtot[p][r0:r0 + 1, :])
            write_state(p[0], s, p[1],
                        state[(p, s)] * decay + _dot_tn(kd[p][s * sr:(s + 1) * sr], v_new[p][s * sr:(s + 1) * sr]))
    outs = []
    for g, (_, z, _, gate) in enumerate(groups):
        heads = []
        for h in range(DN_HEADS):
            oh = o[(g, h)]
            zz = z[:, h * DN_DV:(h + 1) * DN_DV]
            on = oh * lax.rsqrt(jnp.mean(oh * oh, axis=-1, keepdims=True) + EPS) * nw
            heads.append(on * (zz * _sigmoid(zz)) * _sigmoid(gate[:, h * DN_DV:(h + 1) * DN_DV]))
        outs.append(jnp.concatenate(heads, axis=1))
    return outs


def _dn_prompt_kernel(*refs, chunk, n_batch):
    nb = n_batch
    qkv_refs, z_refs, ba_refs, gate_refs = refs[0:nb], refs[nb:2 * nb], refs[2 * nb:3 * nb], refs[3 * nb:4 * nb]
    alog_ref, dtb_ref, nw_ref, s0_ref, o_ref, sout_ref = refs[4 * nb:]

    @pl.when(pl.program_id(0) == 0)
    def _():
        sout_ref[...] = s0_ref[...]

    groups = [(qkv_refs[b][...], z_refs[b][...], ba_refs[b][...], gate_refs[b][...]) for b in range(nb)]

    def read_state(g, s, h):
        return sout_ref[g, h]

    def write_state(g, s, h, val):
        sout_ref[g, h] = val

    outs = _dn_core(groups, alog_ref[...], dtb_ref[...], nw_ref[...], read_state, write_state, 1, chunk)
    for b in range(nb):
        o_ref[b] = outs[b]


def _dn_prompt(proj, ba, alog_row, dtb_row, dn_nw, s0, n_batch, seq):
    chunk = min(DN_CHUNK, seq)
    assert seq % chunk == 0 and chunk % SUBLANES == 0
    nc = seq // chunk
    const2 = lambda c: (0, 0)
    rows = lambda b, col: (lambda c: (b * nc + c, col))
    batches = range(n_batch)
    o, s_out = pl.pallas_call(
        functools.partial(_dn_prompt_kernel, chunk=chunk, n_batch=n_batch),
        grid=(nc,),
        in_specs=(
            [pl.BlockSpec((chunk, DN_CONV_W), rows(b, COL_QKV // DN_CONV_W)) for b in batches]
            + [pl.BlockSpec((chunk, DN_V_W), rows(b, COL_Z // DN_V_W)) for b in batches]
            + [pl.BlockSpec((chunk, LANES), rows(b, 0)) for b in batches]
            + [pl.BlockSpec((chunk, D_MODEL), rows(b, COL_GA // D_MODEL)) for b in batches]
            + [
                pl.BlockSpec((1, LANES), const2),
                pl.BlockSpec((1, LANES), const2),
                pl.BlockSpec((1, DN_DV), const2),
                pl.BlockSpec((n_batch, DN_HEADS, DN_DK, DN_DV), lambda c: (0, 0, 0, 0)),
            ]
        ),
        out_specs=[
            pl.BlockSpec((n_batch, chunk, DN_V_W), lambda c: (0, c, 0)),
            pl.BlockSpec((n_batch, DN_HEADS, DN_DK, DN_DV), lambda c: (0, 0, 0, 0)),
        ],
        out_shape=[
            jax.ShapeDtypeStruct((n_batch, seq, DN_V_W), F32),
            jax.ShapeDtypeStruct((n_batch, DN_HEADS, DN_DK, DN_DV), F32),
        ],
        compiler_params=_cparams(("arbitrary",)),
        name="dn_prompt",
    )(*([proj] * n_batch), *([proj] * n_batch), *([ba] * n_batch), *([proj] * n_batch), alog_row, dtb_row, dn_nw, s0)
    return o.reshape(n_batch * seq, DN_V_W), s_out


def _spread_rows(ref, n_bb, seq):
    per = SUBLANES // seq
    pieces = []
    for j in range(n_bb // per):
        x8 = ref[j * SUBLANES:(j + 1) * SUBLANES, :]
        for r in range(per):
            pieces.append(x8 if r == 0 else pltpu.roll(x8, SUBLANES - r * seq, axis=0))
    return pieces


def _gather_rows(pieces, seq):
    per = SUBLANES // seq
    rows = _row_iota(pieces[0].shape)
    tiles = []
    for j in range(len(pieces) // per):
        tile = pieces[j * per]
        for r in range(1, per):
            tile = jnp.where(rows >= r * seq, pltpu.roll(pieces[j * per + r], r * seq, axis=0), tile)
        tiles.append(tile)
    return jnp.concatenate(tiles, axis=0)


def _dn_sample_kernel(qkv_ref, z_ref, ba_ref, gate_ref, convw_ref, alog_ref, dtb_ref, nw_ref, conv0_ref, s0_ref,
                      o_ref, convout_ref, sout_ref, cbuf_ref, *, seq, n_bb):
    hist = SUBLANES - (DN_CONV - 1)
    spread = lambda ref: _spread_rows(ref, n_bb, seq)
    for bb, piece in enumerate(spread(qkv_ref)):
        cbuf_ref[bb, SUBLANES:2 * SUBLANES, :] = piece
    cbuf_ref[:, hist:SUBLANES, :] = conv0_ref[...]
    w = convw_ref[...]
    acc = cbuf_ref[:, hist:hist + SUBLANES, :] * w[0:1, :]
    for i in range(1, DN_CONV):
        acc = acc + cbuf_ref[:, hist + i:hist + i + SUBLANES, :] * w[i:i + 1, :]
    live = _row_iota(acc.shape) < seq
    qkv = jnp.where(live, acc * _sigmoid(acc), 0.0).reshape(n_bb * SUBLANES, DN_CONV_W)
    convout_ref[...] = cbuf_ref[:, SUBLANES + seq - (DN_CONV - 1):SUBLANES + seq, :]

    def read_state(g, s, h):
        return s0_ref[s, h]

    def write_state(g, s, h, val):
        sout_ref[s, h] = val

    group = (qkv, jnp.concatenate(spread(z_ref), axis=0), jnp.concatenate(spread(ba_ref), axis=0),
             jnp.concatenate(spread(gate_ref), axis=0))
    o = _dn_core([group], alog_ref[...], dtb_ref[...], nw_ref[...], read_state, write_state, n_bb, seq)[0]
    o_ref[...] = _gather_rows([o[bb * SUBLANES:(bb + 1) * SUBLANES] for bb in range(n_bb)], seq)


def _dn_sample(proj, ba, conv_w, alog_row, dtb_row, dn_nw, conv0, s0, row0, n_batch, seq):
    assert SUBLANES % seq == 0 and seq >= DN_CONV - 1
    n_bb = SUBLANES
    rows_in = n_bb * seq
    assert n_batch % n_bb == 0 and row0 % rows_in == 0
    rb0 = row0 // rows_in
    const1 = lambda i: (0, 0)
    return pl.pallas_call(
        functools.partial(_dn_sample_kernel, seq=seq, n_bb=n_bb),
        grid=(n_batch // n_bb,),
        in_specs=[
            pl.BlockSpec((rows_in, DN_CONV_W), lambda i: (rb0 + i, COL_QKV // DN_CONV_W)),
            pl.BlockSpec((rows_in, DN_V_W), lambda i: (rb0 + i, COL_Z // DN_V_W)),
            pl.BlockSpec((rows_in, LANES), lambda i: (rb0 + i, 0)),
            pl.BlockSpec((rows_in, D_MODEL), lambda i: (rb0 + i, COL_GA // D_MODEL)),
            pl.BlockSpec((DN_CONV, DN_CONV_W), const1),
            pl.BlockSpec((1, LANES), const1),
            pl.BlockSpec((1, LANES), const1),
            pl.BlockSpec((1, DN_DV), const1),
            pl.BlockSpec((n_bb, DN_CONV - 1, DN_CONV_W), lambda i: (i, 0, 0)),
            pl.BlockSpec((n_bb, DN_HEADS, DN_DK, DN_DV), lambda i: (i, 0, 0, 0)),
        ],
        out_specs=[
            pl.BlockSpec((rows_in, DN_V_W), lambda i: (i, 0)),
            pl.BlockSpec((n_bb, DN_CONV - 1, DN_CONV_W), lambda i: (i, 0, 0)),
            pl.BlockSpec((n_bb, DN_HEADS, DN_DK, DN_DV), lambda i: (i, 0, 0, 0)),
        ],
        out_shape=[
            jax.ShapeDtypeStruct((n_batch * seq, DN_V_W), F32),
            jax.ShapeDtypeStruct((n_batch, DN_CONV - 1, DN_CONV_W), F32),
            jax.ShapeDtypeStruct((n_batch, DN_HEADS, DN_DK, DN_DV), F32),
        ],
        scratch_shapes=[pltpu.VMEM((n_bb, 2 * SUBLANES, DN_CONV_W), F32)],
        compiler_params=_cparams(("arbitrary",)),
        name="dn_sample",
    )(proj, proj, ba, proj, conv_w, alog_row, dtb_row, dn_nw, conv0, s0)


def _masked_bucket(dist, ok):
    n = np.maximum(dist, 0)
    max_exact = REL_BUCKETS // 2
    large = max_exact + (np.log(np.maximum(n, 1).astype(np.float32) / max_exact)
                         / math.log(REL_MAX_DIST / max_exact) * (REL_BUCKETS - max_exact)).astype(np.int32)
    bucket = np.where(n < max_exact, n, np.minimum(large, REL_BUCKETS - 1))
    return np.where(ok, bucket, -1).astype(np.int32)


def _relbias_kernel(tab_ref, bucket_ref, o_ref):
    h = pl.program_id(0)
    bk = bucket_ref[...]
    acc = jnp.full(bk.shape, -jnp.inf, F32)
    for b in range(REL_BUCKETS):
        acc = jnp.where(bk == b, tab_ref[b * SW_HEADS + h], acc)
    o_ref[0] = acc


def _relbias(rel_table, bucket):
    nq, ns = bucket.shape
    return pl.pallas_call(
        _relbias_kernel,
        grid=(SW_HEADS,),
        in_specs=[
            pl.BlockSpec(memory_space=pltpu.SMEM),
            pl.BlockSpec((nq, ns), lambda h: (0, 0)),
        ],
        out_specs=pl.BlockSpec((1, nq, ns), lambda h: (h, 0, 0)),
        out_shape=jax.ShapeDtypeStruct((SW_HEADS, nq, ns), F32),
        compiler_params=_cparams(("arbitrary",)),
        name="relbias",
    )(rel_table.reshape(-1), bucket)


def _dup_halves(x):
    lo = _lane_iota(x.shape) < SW_HD
    xr = pltpu.roll(x, SW_HD, axis=1)
    return jnp.where(lo, x, xr).astype(BF16), jnp.where(lo, xr, x).astype(BF16)


def _sink_softmax_pv(s, sink, vv):
    m = jnp.maximum(jnp.max(s, axis=-1, keepdims=True), sink)
    p = jnp.exp(s - m)
    denom = jnp.sum(p, axis=-1, keepdims=True) + jnp.exp(sink - m)
    return _dot(p, vv) * (1.0 / denom)


def _swa_prompt_kernel(sinks_ref, q_ref, kc_ref, kp_ref, vc_ref, vp_ref, bias_ref, gate_ref, other_ref,
                       o_ref, klast_ref, vlast_ref, *, n_sub):
    i = pl.program_id(1)
    klast_ref[0] = kc_ref[(n_sub - 1) * SW_BLOCK:, :]
    vlast_ref[0] = vc_ref[(n_sub - 1) * SW_BLOCK:, :]
    kk_all = _dup_halves(jnp.concatenate([kp_ref[...], kc_ref[...]], axis=0))
    vv_all = _dup_halves(jnp.concatenate([vp_ref[...], vc_ref[...]], axis=0))
    lo = _lane_iota((SW_BLOCK, LANES)) < SW_HD
    for sub in range(n_sub):
        rows = slice(sub * SW_BLOCK, (sub + 1) * SW_BLOCK)
        keys = slice(sub * SW_BLOCK, (sub + 2) * SW_BLOCK)
        bias_rows = pl.ds(pl.multiple_of(jnp.minimum(i, 1) * SW_BLOCK, SW_BLOCK), SW_BLOCK) if sub == 0 \
            else pl.ds(SW_BLOCK, SW_BLOCK)
        for pair in range(SW_HEADS // 2):
            cols = slice(pair * LANES, (pair + 1) * LANES)
            qp = q_ref[rows, cols] * (SW_HD ** -0.5)
            outs = []
            for half in range(2):
                hq = 2 * pair + half
                kv = hq // SW_GROUP
                qm = jnp.where(lo if half == 0 else ~lo, qp, 0.0)
                s = _dot_nt(qm, kk_all[kv][keys]) + bias_ref[hq, bias_rows, :]
                outs.append(_sink_softmax_pv(s, sinks_ref[hq], vv_all[kv][keys]))
            o_ref[rows, cols] = other_ref[rows, cols] + _sigmoid(gate_ref[rows, cols]) * jnp.where(lo, outs[0], outs[1])


def _swa_prompt(proj, sinks, bias, other, n_batch, seq):
    assert seq % SW_BLOCK == 0 and WINDOW == SW_BLOCK
    nb = seq // SW_BLOCK
    n_sub = next(n for n in (4, 2, 1) if nb % n == 0)
    ns = nb // n_sub
    rows = n_sub * SW_BLOCK
    cur = lambda col: (lambda b, i: (b * ns + i, col))
    prev = lambda col: (lambda b, i: (b * nb + jnp.maximum(i * n_sub - 1, 0), col))
    return pl.pallas_call(
        functools.partial(_swa_prompt_kernel, n_sub=n_sub),
        grid=(n_batch, ns),
        in_specs=[
            pl.BlockSpec(memory_space=pltpu.SMEM),
            pl.BlockSpec((rows, SW_HEADS * SW_HD), cur(COL_SQ // (SW_HEADS * SW_HD))),
            pl.BlockSpec((rows, SW_KV_W), cur(COL_SK // SW_KV_W)),
            pl.BlockSpec((SW_BLOCK, SW_KV_W), prev(COL_SK // SW_KV_W)),
            pl.BlockSpec((rows, SW_KV_W), cur(COL_SV // SW_KV_W)),
            pl.BlockSpec((SW_BLOCK, SW_KV_W), prev(COL_SV // SW_KV_W)),
            pl.BlockSpec((SW_HEADS, 2 * SW_BLOCK, 2 * SW_BLOCK), lambda b, i: (0, 0, 0)),
            pl.BlockSpec((rows, D_MODEL), cur(COL_GB // D_MODEL)),
            pl.BlockSpec((rows, D_MODEL), lambda b, i: (b * ns + i, 0)),
        ],
        out_specs=[
            pl.BlockSpec((rows, SW_HEADS * SW_HD), lambda b, i: (b * ns + i, 0)),
            pl.BlockSpec((1, SW_BLOCK, SW_KV_W), lambda b, i: (b, 0, 0)),
            pl.BlockSpec((1, SW_BLOCK, SW_KV_W), lambda b, i: (b, 0, 0)),
        ],
        out_shape=[
            jax.ShapeDtypeStruct((n_batch * seq, SW_HEADS * SW_HD), F32),
            jax.ShapeDtypeStruct((n_batch, SW_BLOCK, SW_KV_W), F32),
            jax.ShapeDtypeStruct((n_batch, SW_BLOCK, SW_KV_W), F32),
        ],
        compiler_params=_cparams(("arbitrary", "arbitrary")),
        name="swa_prompt",
    )(sinks, proj, proj, proj, proj, proj, bias, proj, other)


def _swa_sample_kernel(q_ref, kn_ref, vn_ref, kc_ref, vc_ref, bias_ref, sink_ref, gate_ref, other_ref,
                       o_ref, ko_ref, vo_ref, kall_ref, vall_ref, *, seq, n_bb, n_cache):
    n_keys = kall_ref.shape[1]
    zeros_tail = jnp.zeros((n_bb, n_keys - n_cache - SUBLANES, LANES), F32)
    lo = _lane_iota((SUBLANES, LANES)) < SW_HD
    for ref, cache_ref, new_ref in ((kall_ref, kc_ref, kn_ref), (vall_ref, vc_ref, vn_ref)):
        ref[:, 0:n_cache, :] = cache_ref[...]
        for bb, piece in enumerate(_spread_rows(new_ref, n_bb, seq)):
            ref[bb, n_cache:n_cache + SUBLANES, :] = piece
        ref[:, n_cache + SUBLANES:, :] = zeros_tail
    ko_ref[...] = kall_ref[:, seq:seq + n_cache, :]
    vo_ref[...] = vall_ref[:, seq:seq + n_cache, :]
    outs = []
    for bb, q8 in enumerate(_spread_rows(q_ref, n_bb, seq)):
        kk = _dup_halves(kall_ref[bb])
        vv = _dup_halves(vall_ref[bb])
        pairs = []
        for kv in range(SW_KV_HEADS):
            pieces = []
            for g in range(SW_GROUP):
                hq = kv * SW_GROUP + g
                qp = q8[:, (hq // 2) * LANES:(hq // 2 + 1) * LANES]
                pieces.append(jnp.where(lo if hq % 2 == 0 else ~lo, qp, 0.0))
            qs = jnp.concatenate(pieces, axis=0) * (SW_HD ** -0.5)
            bias = jnp.concatenate([bias_ref[kv * SW_GROUP + g] for g in range(SW_GROUP)], axis=0)
            s = _dot_nt(qs, kk[kv]) + bias
            res = _sink_softmax_pv(s, _col(sink_ref[kv], 0), vv[kv])
            for g in range(0, SW_GROUP, 2):
                pairs.append(jnp.where(lo, res[g * SUBLANES:(g + 1) * SUBLANES],
                                       res[(g + 1) * SUBLANES:(g + 2) * SUBLANES]))
        outs.append(jnp.concatenate(pairs, axis=1))
    o_ref[...] = other_ref[...] + _sigmoid(gate_ref[...]) * _gather_rows(outs, seq)


def _swa_sample(proj, k_cache, v_cache, bias, bias_row0, sink_rows, other, row0, n_batch, seq):
    assert SUBLANES % seq == 0
    n_bb = SUBLANES
    rows_in = n_bb * seq
    n_cache = k_cache.shape[1]
    assert n_batch % n_bb == 0 and row0 % rows_in == 0 and n_cache % SUBLANES == 0
    n_keys = bias.shape[-1]
    rb0 = row0 // rows_in
    blk = lambda col: (lambda i: (rb0 + i, col))
    return pl.pallas_call(
        functools.partial(_swa_sample_kernel, seq=seq, n_bb=n_bb, n_cache=n_cache),
        grid=(n_batch // n_bb,),
        in_specs=[
            pl.BlockSpec((rows_in, SW_HEADS * SW_HD), blk(COL_SQ // (SW_HEADS * SW_HD))),
            pl.BlockSpec((rows_in, SW_KV_W), blk(COL_SK // SW_KV_W)),
            pl.BlockSpec((rows_in, SW_KV_W), blk(COL_SV // SW_KV_W)),
            pl.BlockSpec((n_bb, n_cache, SW_KV_W), lambda i: (i, 0, 0)),
            pl.BlockSpec((n_bb, n_cache, SW_KV_W), lambda i: (i, 0, 0)),
            pl.BlockSpec((SW_HEADS, SUBLANES, n_keys), lambda i: (0, bias_row0 // SUBLANES, 0)),
            pl.BlockSpec((SW_KV_HEADS, SW_GROUP * SUBLANES, n_keys), lambda i: (0, 0, 0)),
            pl.BlockSpec((rows_in, D_MODEL), blk(COL_GB // D_MODEL)),
            pl.BlockSpec((rows_in, D_MODEL), lambda i: (i, 0)),
        ],
        out_specs=[
            pl.BlockSpec((rows_in, SW_HEADS * SW_HD), lambda i: (i, 0)),
            pl.BlockSpec((n_bb, n_cache, SW_KV_W), lambda i: (i, 0, 0)),
            pl.BlockSpec((n_bb, n_cache, SW_KV_W), lambda i: (i, 0, 0)),
        ],
        out_shape=[
            jax.ShapeDtypeStruct((n_batch * seq, SW_HEADS * SW_HD), F32),
            jax.ShapeDtypeStruct(k_cache.shape, F32),
            jax.ShapeDtypeStruct(v_cache.shape, F32),
        ],
        scratch_shapes=[pltpu.VMEM((n_bb, n_keys, SW_KV_W), F32), pltpu.VMEM((n_bb, n_keys, SW_KV_W), F32)],
        compiler_params=_cparams(("arbitrary",)),
        name="swa_sample",
    )(proj, proj, proj, k_cache, v_cache, bias, sink_rows, proj, other)


def _mix_kernel(xp_ref, xs_ref, mp_ref, ms_ref, wo_ref, nw_ref, wr_ref, br_ref,
                x1_ref, h2_ref, route_ref, *, n_prompt_tiles):
    i = pl.program_id(0)

    def run(x_ref, mixed_ref):
        x1 = x_ref[...] + _dot(mixed_ref[...], wo_ref[...])
        x1_ref[...] = x1
        h2 = x1 * lax.rsqrt(jnp.mean(x1 * x1, axis=-1, keepdims=True) + EPS) * nw_ref[...]
        _to_tiles(h2_ref, h2)
        logits = _dot(h2, wr_ref[...]) + br_ref[...]
        lane = _lane_iota(logits.shape)
        lanef = lane.astype(F32)
        big = float(2 * LANES)
        is_g = lane < N_GROUPS
        gl = jnp.where(is_g, logits, -jnp.inf)
        gmax = jnp.max(gl, axis=-1, keepdims=True)
        gval = 1.0 / jnp.sum(jnp.where(is_g, jnp.exp(gl - gmax), 0.0), axis=-1, keepdims=True)
        grp = jnp.min(jnp.where(gl == gmax, lanef, big), axis=-1, keepdims=True)
        e_grp = ((lane - N_GROUPS) >> 3).astype(F32)
        is_e = (lane >= N_GROUPS) & (lane < N_GROUPS + N_EXPERTS) & (e_grp == grp)
        el = jnp.where(is_e, logits, -jnp.inf)
        v1 = jnp.max(el, axis=-1, keepdims=True)
        i1 = jnp.min(jnp.where(el == v1, lanef, big), axis=-1, keepdims=True)
        el2 = jnp.where(lanef == i1, -jnp.inf, el)
        v2 = jnp.max(el2, axis=-1, keepdims=True)
        i2 = jnp.min(jnp.where(el2 == v2, lanef, big), axis=-1, keepdims=True)
        e2 = jnp.exp(v2 - v1)
        w1 = gval / (1.0 + e2)
        w2 = gval * e2 / (1.0 + e2)
        route_ref[...] = jnp.where(lane == 0, i1 - N_GROUPS,
                                   jnp.where(lane == 1, i2 - N_GROUPS,
                                             jnp.where(lane == 2, w1, jnp.where(lane == 3, w2, 0.0))))

    @pl.when(i < n_prompt_tiles)
    def _():
        run(xp_ref, mp_ref)

    @pl.when(i >= n_prompt_tiles)
    def _():
        run(xs_ref, ms_ref)


def _mix(xp, xs, mixed_p, mixed_s, w_out, norm_w, w_router, b_router):
    tp, ts = xp.shape[0], xs.shape[0]
    tm = _token_tile(tp, ts, cands=BIG_TILES)
    npt, nst = tp // tm, ts // tm
    const = lambda i: (0, 0)
    row = lambda i: (i, 0)
    return pl.pallas_call(
        functools.partial(_mix_kernel, n_prompt_tiles=npt),
        grid=(npt + nst,),
        in_specs=[
            pl.BlockSpec((tm, D_MODEL), lambda i: (jnp.minimum(i, npt - 1), 0)),
            pl.BlockSpec((tm, D_MODEL), lambda i: (jnp.maximum(i - npt, 0), 0)),
            pl.BlockSpec((tm, D_MODEL), lambda i: (jnp.minimum(i, npt - 1), 0)),
            pl.BlockSpec((tm, D_MODEL), lambda i: (jnp.maximum(i - npt, 0), 0)),
            pl.BlockSpec((D_MODEL, D_MODEL), const),
            pl.BlockSpec((1, D_MODEL), const),
            pl.BlockSpec((D_MODEL, LANES), const),
            pl.BlockSpec((1, LANES), const),
        ],
        out_specs=[
            pl.BlockSpec((tm, D_MODEL), row),
            pl.BlockSpec((tm,) + TOK_TILE, lambda i: (i, 0, 0)),
            pl.BlockSpec((tm, LANES), row),
        ],
        out_shape=[
            jax.ShapeDtypeStruct((tp + ts, D_MODEL), F32),
            jax.ShapeDtypeStruct((tp + ts,) + TOK_TILE, F32),
            jax.ShapeDtypeStruct((tp + ts, LANES), F32),
        ],
        compiler_params=_cparams(("arbitrary",)),
        name="mix_router",
    )(xp, xs, mixed_p, mixed_s, w_out, norm_w, w_router, b_router)


def _rank_kernel(route_ref, dest_ref, meta_ref, rank_ref, cnt_ref, *, tile, blk):
    phase = pl.program_id(0)
    i = pl.program_id(1)
    shape = (tile, LANES)
    lane = _lane_iota(shape)
    lanef = lane.astype(F32)
    r = route_ref[...]
    oh0 = lanef == _col(r, 0)
    oh1 = lanef == _col(r, 1)
    rows = pl.ds(pl.multiple_of(i * tile, tile), tile)

    @pl.when(phase == 0)
    def _():
        @pl.when(i == 0)
        def _():
            cnt_ref[...] = jnp.zeros(cnt_ref.shape, F32)

        oh = jnp.where(oh0 | oh1, 1.0, 0.0)
        tri = jnp.where(_row_iota((tile, tile)) > _lane_iota((tile, tile)), 1.0, 0.0)
        before = _dot(tri, oh) + cnt_ref[0:1, :]
        rank0 = jnp.sum(jnp.where(oh0, before, 0.0), axis=-1, keepdims=True)
        rank1 = jnp.sum(jnp.where(oh1, before, 0.0), axis=-1, keepdims=True)
        rank_ref[rows, :] = jnp.where(lane == 0, rank0, jnp.where(lane == 1, rank1, 0.0))
        cnt_ref[0:1, :] = cnt_ref[0:1, :] + jnp.sum(oh, axis=0, keepdims=True)

    @pl.when(phase == 1)
    def _():
        cnt = cnt_ref[0:1, :]
        padded = jnp.floor((cnt + (blk - 1)) / blk) * blk
        before_lane = jnp.where(_row_iota((LANES, LANES)) < _lane_iota((LANES, LANES)), 1.0, 0.0)
        start = _dot_exact(jnp.broadcast_to(padded, (SUBLANES, LANES)), before_lane)[0:1, :]
        rk = rank_ref[rows, :]
        d0 = jnp.sum(jnp.where(oh0, start, 0.0), axis=-1, keepdims=True) + _col(rk, 0)
        d1 = jnp.sum(jnp.where(oh1, start, 0.0), axis=-1, keepdims=True) + _col(rk, 1)
        dest_ref[...] = jnp.where(lane == 0, d0, jnp.where(lane == 1, d1, 0.0)).astype(I32)

        @pl.when(i == 0)
        def _():
            end = start + padded
            mshape = meta_ref.shape
            blk_start = (_row_iota(mshape) * blk).astype(F32)
            hit = (_lane_iota(mshape) < N_EXPERTS) & (end <= blk_start)
            be = jnp.minimum(jnp.sum(jnp.where(hit, 1.0, 0.0), axis=-1, keepdims=True), N_EXPERTS - 1.0)
            n_used = _col(end, N_EXPERTS - 1) / blk
            ml = _lane_iota(mshape)
            mine = ml.astype(F32) == be
            seg_start = jnp.sum(jnp.where(mine, start, 0.0), axis=-1, keepdims=True)
            seg_count = jnp.sum(jnp.where(mine, cnt, 0.0), axis=-1, keepdims=True)
            n_valid = jnp.clip(seg_count - (blk_start[:, 0:1] - seg_start), 0.0, float(blk))
            meta_ref[...] = jnp.where(ml == 0, be, jnp.where(ml == 1, n_used,
                                                              jnp.where(ml == 2, n_valid, 0.0))).astype(I32)


def _rank(route, tile, blk, n_blocks):
    t = route.shape[0]
    nt = t // tile
    nbp = -(-n_blocks // SUBLANES) * SUBLANES
    return pl.pallas_call(
        functools.partial(_rank_kernel, tile=tile, blk=blk),
        grid=(2, nt),
        in_specs=[pl.BlockSpec((tile, LANES), lambda p, i: (i, 0))],
        out_specs=[
            pl.BlockSpec((tile, LANES), lambda p, i: (i * p, 0)),
            pl.BlockSpec((nbp, LANES), lambda p, i: (0, 0)),
        ],
        out_shape=[
            jax.ShapeDtypeStruct((t, LANES), I32),
            jax.ShapeDtypeStruct((nbp, LANES), I32),
        ],
        scratch_shapes=[pltpu.VMEM((t, LANES), F32), pltpu.VMEM((SUBLANES, LANES), F32)],
        compiler_params=_cparams(("arbitrary", "arbitrary")),
        name="moe_rank",
    )(route)


TOK_TILE = (D_MODEL // LANES, LANES)


def _to_tiles(ref, x):
    for j in range(TOK_TILE[0]):
        ref[:, j, :] = x[:, j * LANES:(j + 1) * LANES]


def _from_tiles(ref):
    return jnp.concatenate([ref[:, j, :] for j in range(TOK_TILE[0])], axis=1)


def _tile_copy(src, src_row, dst, dst_row, sem):
    return pltpu.make_async_copy(src.at[src_row], dst.at[dst_row], sem)


def _row_copy(src, src_row, dst, group, sub, sem):
    return pltpu.make_async_copy(src.at[pl.ds(src_row, 1)], dst.at[group, pl.ds(sub, 1)], sem)


def _last_used(i, nu_ref):
    return jnp.minimum(i, jnp.maximum(nu_ref[0] - 1, 0))


def _for_rows(n, body):
    def group(g, carry):
        for u in range(SUBLANES):
            body(g, u)
        return carry

    def single(t, carry):
        body(t // SUBLANES, t % SUBLANES)
        return carry

    n_groups = n // SUBLANES
    lax.fori_loop(0, n_groups, group, 0)
    lax.fori_loop(n_groups * SUBLANES, n, single, 0)


def _expert_kernel(be_ref, nu_ref, nv_ref, dest_ref, h2_ref, wg_ref, wu_ref, wd_ref, y_ref,
                   xbuf_ref, wgu_ref, wdn_ref, inv_ref, sem, *, blk):
    i = pl.program_id(0)
    n_used = nu_ref[0]
    used = i < n_used
    slot = i % 2
    blk_i = _last_used(i, nu_ref)
    fresh = (i == 0) | (be_ref[blk_i] != be_ref[jnp.maximum(blk_i - 1, 0)])

    def gather(block, into):
        def issue(g, u):
            t = g * SUBLANES + u
            _tile_copy(h2_ref, inv_ref[block * blk + t], xbuf_ref.at[into], t,
                       sem.at[into]).start(priority=u % 2 if isinstance(u, int) else 0)

        _for_rows(nv_ref[block], issue)

    @pl.when(i == 0)
    def _():
        def place(t, carry):
            for k in range(2):
                inv_ref[dest_ref[2 * t + k]] = t
            return carry

        lax.fori_loop(0, dest_ref.shape[0] // 2, place, 0, unroll=8)
        xbuf_ref[...] = jnp.zeros(xbuf_ref.shape, F32)
        gather(0, 0)

    @pl.when(i + 1 < n_used)
    def _():
        gather(i + 1, 1 - slot)

    @pl.when(used & fresh)
    def _():
        wgu_ref[:, :D_EXPERT] = wg_ref[0].astype(BF16)
        wgu_ref[:, D_EXPERT:] = wu_ref[0].astype(BF16)
        wdn_ref[...] = wd_ref[0].astype(BF16)

    @pl.when(used)
    def _():
        n_rows = nv_ref[i]
        pltpu.make_async_copy(h2_ref.at[pl.ds(0, n_rows)], xbuf_ref.at[slot, pl.ds(0, n_rows)], sem.at[slot]).wait()
        gu = jnp.dot(_from_tiles(xbuf_ref.at[slot]).astype(BF16), wgu_ref[...], preferred_element_type=F32)
        g = gu[:, :D_EXPERT]
        hidden = (g * _sigmoid(g) * gu[:, D_EXPERT:]).astype(BF16)
        y_ref[...] = jnp.dot(hidden, wdn_ref[...], preferred_element_type=F32)

    @pl.when(jnp.logical_not(used))
    def _():
        y_ref[...] = jnp.zeros(y_ref.shape, F32)


def _experts(block_expert, n_used, n_valid, dest_flat, h2, w_gate, w_up, w_down, blk):
    n_blocks = block_expert.shape[0]
    wsel = lambda i, be, nu, nv, de: (be[_last_used(i, nu)], 0, 0)
    return pl.pallas_call(
        functools.partial(_expert_kernel, blk=blk),
        grid_spec=pltpu.PrefetchScalarGridSpec(
            num_scalar_prefetch=4,
            grid=(n_blocks,),
            in_specs=[
                pl.BlockSpec(memory_space=pl.ANY),
                pl.BlockSpec((1, D_MODEL, D_EXPERT), wsel),
                pl.BlockSpec((1, D_MODEL, D_EXPERT), wsel),
                pl.BlockSpec((1, D_EXPERT, D_MODEL), wsel),
            ],
            out_specs=pl.BlockSpec((blk, D_MODEL), lambda i, be, nu, nv, de: (i, 0)),
            scratch_shapes=[
                pltpu.VMEM((2, blk) + TOK_TILE, F32),
                pltpu.VMEM((D_MODEL, 2 * D_EXPERT), BF16),
                pltpu.VMEM((D_EXPERT, D_MODEL), BF16),
                pltpu.SMEM((n_blocks * blk,), I32),
                pltpu.SemaphoreType.DMA((2,)),
            ],
        ),
        out_shape=jax.ShapeDtypeStruct((n_blocks * blk, D_MODEL), F32),
        compiler_params=_cparams(("arbitrary",)),
        name="moe_experts",
    )(block_expert, n_used, n_valid, dest_flat, h2, w_gate, w_up, w_down)


def _combine_kernel(dest_ref, x1_ref, route_ref, nw_ref, ys_ref, yp_ref, ysm_ref, ybuf_ref, sem,
                    *, tile, n_prompt_tiles):
    i = pl.program_id(0)
    slot = i % 2

    groups = tile // SUBLANES

    def gather(step, into):
        def issue(g, carry):
            for u in range(SUBLANES):
                for k in range(2):
                    slot_row = dest_ref[2 * step * tile + g * (2 * SUBLANES) + 2 * u + k]
                    _row_copy(ys_ref, slot_row, ybuf_ref.at[into], k * groups + g, u,
                              sem.at[into]).start(priority=k)
            return carry

        lax.fori_loop(0, groups, issue, 0)

    @pl.when(i == 0)
    def _():
        gather(0, 0)

    @pl.when(i + 1 < pl.num_programs(0))
    def _():
        gather(i + 1, 1 - slot)

    def drain(t, carry):
        _row_copy(ys_ref, 0, ybuf_ref.at[slot], 0, 0, sem.at[slot]).wait()
        return carry

    lax.fori_loop(0, 2 * tile, drain, 0, unroll=8)
    r = route_ref[...]
    y = (ybuf_ref[slot, 0:groups].reshape(tile, D_MODEL) * _col(r, 2)
         + ybuf_ref[slot, groups:2 * groups].reshape(tile, D_MODEL) * _col(r, 3))
    x2 = x1_ref[...] + y
    out = x2 * lax.rsqrt(jnp.mean(x2 * x2, axis=-1, keepdims=True) + EPS) * nw_ref[...]

    @pl.when(i < n_prompt_tiles)
    def _():
        yp_ref[...] = out

    @pl.when(i >= n_prompt_tiles)
    def _():
        ysm_ref[...] = out


def _combine(dest_flat, x1, route, norm_w, ys, tp, ts):
    tile = _token_tile(tp, ts, cands=BIG_TILES)
    npt, nst = tp // tile, ts // tile
    return pl.pallas_call(
        functools.partial(_combine_kernel, tile=tile, n_prompt_tiles=npt),
        grid_spec=pltpu.PrefetchScalarGridSpec(
            num_scalar_prefetch=1,
            grid=(npt + nst,),
            in_specs=[
                pl.BlockSpec((tile, D_MODEL), lambda i, d: (i, 0)),
                pl.BlockSpec((tile, LANES), lambda i, d: (i, 0)),
                pl.BlockSpec((1, D_MODEL), lambda i, d: (0, 0)),
                pl.BlockSpec(memory_space=pl.ANY),
            ],
            out_specs=[
                pl.BlockSpec((tile, D_MODEL), lambda i, d: (jnp.minimum(i, npt - 1), 0)),
                pl.BlockSpec((tile, D_MODEL), lambda i, d: (jnp.maximum(i - npt, 0), 0)),
            ],
            scratch_shapes=[pltpu.VMEM((2, 2 * tile // SUBLANES, SUBLANES, D_MODEL), F32), pltpu.SemaphoreType.DMA((2,))],
        ),
        out_shape=[
            jax.ShapeDtypeStruct((tp, D_MODEL), F32),
            jax.ShapeDtypeStruct((ts, D_MODEL), F32),
        ],
        compiler_params=_cparams(("arbitrary",)),
        name="moe_combine",
    )(dest_flat, x1, route, norm_w, ys)


def _layer(xp, xs, n_batch, seq, s_batch, s_seq, conv_state, dn_state, k_cache, v_cache,
           w_in, conv_w, a_log, dt_bias, dn_norm_w, sinks, rel_bias, w_out, norm_mix_w, norm_ffn_w,
           w_rg, b_rg, w_re, b_re, w_gate, w_up, w_down, norm_final_w):
    tp, ts = xp.shape[0], xs.shape[0]
    t_all = tp + ts
    row = lambda v: v.reshape(1, -1).astype(F32)

    o = np.cumsum((0, DN_QK_W, DN_QK_W, DN_V_W, DN_V_W, DN_HEADS, DN_HEADS, SW_HEADS * SW_HD, SW_KV_W, SW_KV_W,
                   D_MODEL, D_MODEL)).tolist()
    w_big = jnp.concatenate([w_in[:, o[0]:o[4]], w_in[:, o[6]:o[7]], w_in[:, o[9]:o[11]], w_in[:, o[7]:o[9]]],
                            axis=1).astype(BF16)
    w_small = jnp.pad(w_in[:, o[4]:o[6]], ((0, 0), (0, LANES - 2 * DN_HEADS))).astype(BF16)
    head_row = lambda v: jnp.pad(v.astype(F32), (DN_HEADS, LANES - 2 * DN_HEADS)).reshape(1, LANES)
    w_router = jnp.pad(jnp.concatenate([w_rg, w_re], axis=1),
                       ((0, 0), (0, LANES - N_GROUPS - N_EXPERTS))).astype(BF16)
    b_router = jnp.pad(jnp.concatenate([b_rg, b_re]).astype(F32), (0, LANES - N_GROUPS - N_EXPERTS)).reshape(1, LANES)

    conv0 = jnp.zeros((n_batch, DN_CONV - 1, DN_CONV_W), F32)
    proj, ba, conv_tail = _inproj(xp, xs, row(norm_mix_w), w_big, w_small, conv_w.astype(F32), conv0, seq)
    p_conv = conv_tail[:, SUBLANES - (DN_CONV - 1):, :]

    dn0 = jnp.zeros((n_batch, DN_HEADS, DN_DK, DN_DV), F32)
    oa_p, p_dn = _dn_prompt(proj, ba, head_row(a_log), head_row(dt_bias), row(dn_norm_w), dn0, n_batch, seq)
    oa_s, s_conv, s_dn = _dn_sample(proj, ba, conv_w.astype(F32), head_row(a_log), head_row(dt_bias), row(dn_norm_w),
                                    conv_state, dn_state, tp, s_batch, s_seq)

    n_cache = k_cache.shape[1]
    n_keys = -(-(n_cache + SUBLANES) // LANES) * LANES
    assert n_keys == 2 * SW_BLOCK, "the prompt and sample bias tables share one (rows, keys) array"
    dist_p = np.arange(SW_BLOCK)[:, None] - (np.arange(2 * SW_BLOCK)[None, :] - SW_BLOCK)
    dist_s = n_cache + np.arange(SUBLANES)[:, None] - np.arange(n_keys)[None, :]
    in_window = lambda dist: (dist >= 0) & (dist < WINDOW)
    has_prev = np.arange(2 * SW_BLOCK)[None, :] >= SW_BLOCK
    buckets = np.concatenate([_masked_bucket(dist_p, in_window(dist_p) & has_prev),
                              _masked_bucket(dist_p, in_window(dist_p)),
                              _masked_bucket(dist_s, in_window(dist_s))], axis=0)
    bias = _relbias(rel_bias.astype(F32), jnp.asarray(buckets))
    mixed_p, p_k, p_v = _swa_prompt(proj, sinks.astype(F32), bias, oa_p, n_batch, seq)
    sink_rows = jnp.broadcast_to(jnp.repeat(sinks.astype(F32).reshape(SW_KV_HEADS, SW_GROUP), SUBLANES, axis=1)[:, :, None],
                                 (SW_KV_HEADS, SW_GROUP * SUBLANES, n_keys))
    mixed_s, s_k, s_v = _swa_sample(proj, k_cache.reshape(s_batch, n_cache, SW_KV_W),
                                    v_cache.reshape(s_batch, n_cache, SW_KV_W), bias, 2 * SW_BLOCK, sink_rows, oa_s,
                                    tp, s_batch, s_seq)

    x1, h2, route = _mix(xp, xs, mixed_p, mixed_s, w_out.astype(BF16), row(norm_ffn_w), w_router, b_router)

    n_blocks = -(-2 * t_all // MOE_BLOCK) + N_EXPERTS
    dest, meta = _rank(route, _token_tile(t_all, cands=BIG_TILES), MOE_BLOCK, n_blocks)
    dest_flat = dest[:, :2].reshape(-1)
    block_expert = meta[:n_blocks, 0]
    n_used = meta[0:1, 1]
    n_valid = meta[:n_blocks, 2]
    ys = _experts(block_expert, n_used, n_valid, dest_flat, h2, w_gate, w_up, w_down, MOE_BLOCK)
    y_p, y_s = _combine(dest_flat, x1, route, row(norm_final_w), ys, tp, ts)

    kv_shape = (n_batch, WINDOW, SW_KV_HEADS, SW_HD)
    return (y_p, y_s, p_conv, p_dn, p_k.reshape(kv_shape), p_v.reshape(kv_shape), s_conv, s_dn,
            s_k.reshape(k_cache.shape), s_v.reshape(v_cache.shape))


def kernel(x_prompt, x_sample, state_dn_conv, state_dn, cache_swa_k, cache_swa_v, w_in, conv_w, a_log, dt_bias, dn_norm_w, sinks, rel_bias, w_out, norm_mix_w, norm_ffn_w, w_router_group, b_router_group, w_router_expert, b_router_expert, w_gate, w_up, w_down, norm_final_w):
    depth = w_in.shape[0]
    assert depth == 1, "the final-norm fusion below assumes a single layer"
    n_batch, seq, _ = x_prompt.shape
    s_batch, s_seq, _ = x_sample.shape
    outs = _layer(x_prompt.reshape(-1, D_MODEL), x_sample.reshape(-1, D_MODEL), n_batch, seq, s_batch, s_seq,
                  state_dn_conv[0], state_dn[0], cache_swa_k[0], cache_swa_v[0],
                  w_in[0], conv_w[0], a_log[0], dt_bias[0], dn_norm_w[0], sinks[0], rel_bias,
                  w_out[0], norm_mix_w[0], norm_ffn_w[0], w_router_group[0], b_router_group[0],
                  w_router_expert[0], b_router_expert[0], w_gate[0], w_up[0], w_down[0], norm_final_w)
    y_p, y_s, p_conv, p_dn, p_k, p_v, s_conv, s_dn, s_k, s_v = outs
    return (y_p.reshape(x_prompt.shape), y_s.reshape(x_sample.shape), p_conv[None], p_dn[None], p_k[None], p_v[None],
            s_conv[None], s_dn[None], s_k[None], s_v[None])
```
